```python
import jax
import jax.numpy as jnp
from jax import lax
import numpy as np

D_MODEL = 1024
BATCH = 8
SEQ = 4096
DEPTH = 1

HEAD_DIM = 64
MIX_WIDTH = D_MODEL
N_FOX_HEADS = MIX_WIDTH // (2 * HEAD_DIM)
N_SWA_HEADS = MIX_WIDTH // (2 * HEAD_DIM)
N_SWA_KV_HEADS = 2
SWA_GROUP = N_SWA_HEADS // N_SWA_KV_HEADS
WINDOW = 128
Q_BLOCK = 128
ROPE_THETA = 10000.0
N_MEM = 256
N_XATTN_HEADS = 4
XATTN_HEAD_DIM = D_MODEL // N_XATTN_HEADS
N_GROUPS = 4
EXPERTS_PER_GROUP = 8
N_EXPERTS = N_GROUPS * EXPERTS_PER_GROUP
TOP_K_IN_GROUP = 2
D_EXPERT = 512
ROW_BLOCK = 128
LN_EPS = 1e-5
FOX_W = N_FOX_HEADS * HEAD_DIM
SWA_Q_W = N_SWA_HEADS * HEAD_DIM
SWA_KV_W = N_SWA_KV_HEADS * HEAD_DIM
IN_SIZES = (FOX_W, FOX_W, FOX_W, N_FOX_HEADS, SWA_Q_W, SWA_KV_W, SWA_KV_W)
IN_PROJ_W = sum(IN_SIZES)

kernel_name = 'hybrid_fox_swa_sink_hmoe_deepnorm'


def layer_norm(x, g, b):
    xf = x.astype(jnp.float32)
    mu = jnp.mean(xf, axis=-1, keepdims=True)
    var = jnp.mean(jnp.square(xf - mu), axis=-1, keepdims=True)
    y = (xf - mu) * lax.rsqrt(var + LN_EPS) * g.astype(jnp.float32) + b.astype(jnp.float32)
    return y.astype(x.dtype)


def rope(x, positions):
    half = x.shape[-1] // 2
    inv_freq = ROPE_THETA ** (-jnp.arange(half, dtype=jnp.float32) / half)
    ang = positions.astype(jnp.float32)[..., None] * inv_freq
    cos = jnp.cos(ang)[:, :, None, :]
    sin = jnp.sin(ang)[:, :, None, :]
    xf = x.astype(jnp.float32)
    x1, x2 = xf[..., :half], xf[..., half:]
    return jnp.concatenate([x1 * cos - x2 * sin, x2 * cos + x1 * sin], axis=-1).astype(x.dtype)


def forgetting_attention(q, k, v, log_f):
    B, S, H, dh = q.shape
    nb = S // Q_BLOCK
    scale = dh ** -0.5
    c = jnp.cumsum(log_f, axis=1).transpose(0, 2, 1)
    kh = k.transpose(0, 2, 1, 3)
    vh = v.transpose(0, 2, 1, 3)
    qb = q.reshape(B, nb, Q_BLOCK, H, dh).transpose(1, 0, 3, 2, 4)
    cb = c.reshape(B, H, nb, Q_BLOCK).transpose(2, 0, 1, 3)
    key_pos = jnp.arange(S)

    def block(args):
        q_blk, c_blk, i = args
        s = jnp.einsum('bhqd,bhsd->bhqs', q_blk, kh).astype(jnp.float32) * scale
        s = s + c_blk[..., None] - c[:, :, None, :]
        qpos = i * Q_BLOCK + jnp.arange(Q_BLOCK)
        causal = key_pos[None, :] <= qpos[:, None]
        s = jnp.where(causal, s, -jnp.inf)
        p = jax.nn.softmax(s, axis=-1)
        return jnp.einsum('bhqs,bhsd->bhqd', p.astype(vh.dtype), vh)

    o = lax.map(block, (qb, cb, jnp.arange(nb)))
    return o.transpose(1, 0, 3, 2, 4).reshape(B, S, H * dh)


def sliding_window_sink_attention(q, k, v, sinks):
    B, S, _, dh = q.shape
    nb = S // WINDOW
    scale = dh ** -0.5
    qb = q.reshape(B, nb, WINDOW, N_SWA_KV_HEADS, SWA_GROUP, dh)

    def band(t):
        t = t.reshape(B, nb, WINDOW, N_SWA_KV_HEADS, dh)
        prev = jnp.concatenate([jnp.zeros_like(t[:, :1]), t[:, :-1]], axis=1)
        return jnp.concatenate([prev, t], axis=2)

    kb, vb = band(k), band(v)
    s = jnp.einsum('bnqkgd,bnskd->bnkgqs', qb, kb).astype(jnp.float32) * scale
    blk = jnp.arange(nb)[:, None, None] * WINDOW
    qpos = blk + jnp.arange(WINDOW)[None, :, None]
    kpos = blk + jnp.arange(2 * WINDOW)[None, None, :] - WINDOW
    valid = (kpos <= qpos) & (qpos - kpos < WINDOW) & (kpos >= 0)
    s = jnp.where(valid[None, :, None, None], s, -jnp.inf)
    sink = jnp.broadcast_to(
        sinks.astype(jnp.float32).reshape(1, 1, N_SWA_KV_HEADS, SWA_GROUP, 1, 1),
        s.shape[:-1] + (1,))
    p = jax.nn.softmax(jnp.concatenate([s, sink], axis=-1), axis=-1)[..., :-1]
    o = jnp.einsum('bnkgqs,bnskd->bnqkgd', p.astype(vb.dtype), vb)
    return o.reshape(B, S, N_SWA_HEADS * dh)


def hybrid_mixer(h, positions, w_in, b_forget, sinks, w_out):
    B, S, _ = h.shape
    z = h @ w_in
    splits = [int(v) for v in np.cumsum(IN_SIZES)[:-1]]
    q_f, k_f, v_f, f_logit, q_s, k_s, v_s = jnp.split(z, splits, axis=-1)
    log_f = jax.nn.log_sigmoid((f_logit + b_forget).astype(jnp.float32))
    o_fox = forgetting_attention(
        q_f.reshape(B, S, N_FOX_HEADS, HEAD_DIM),
        k_f.reshape(B, S, N_FOX_HEADS, HEAD_DIM),
        v_f.reshape(B, S, N_FOX_HEADS, HEAD_DIM), log_f)
    q_s = rope(q_s.reshape(B, S, N_SWA_HEADS, HEAD_DIM), positions)
    k_s = rope(k_s.reshape(B, S, N_SWA_KV_HEADS, HEAD_DIM), positions)
    v_s = v_s.reshape(B, S, N_SWA_KV_HEADS, HEAD_DIM)
    o_swa = sliding_window_sink_attention(q_s, k_s, v_s, sinks)
    return jnp.concatenate([o_fox, o_swa], axis=-1) @ w_out


def memory_cross_attention(h, mem, w_q, w_kv, w_out):
    B, S, D = h.shape
    M = mem.shape[1]
    q = (h @ w_q).reshape(B, S, N_XATTN_HEADS, XATTN_HEAD_DIM)
    k, v = jnp.split(mem @ w_kv, 2, axis=-1)
    k = k.reshape(B, M, N_XATTN_HEADS, XATTN_HEAD_DIM)
    v = v.reshape(B, M, N_XATTN_HEADS, XATTN_HEAD_DIM)
    s = jnp.einsum('bqhd,bmhd->bhqm', q, k).astype(jnp.float32) * (XATTN_HEAD_DIM ** -0.5)
    p = jax.nn.softmax(s, axis=-1)
    o = jnp.einsum('bhqm,bmhd->bqhd', p.astype(v.dtype), v).reshape(B, S, D)
    return o @ w_out


def hierarchical_moe(h, w_rg, b_rg, w_re, b_re, w_gate, w_up, w_down):
    B, S, D = h.shape
    T = B * S
    xt = h.reshape(T, D)
    p_g = jax.nn.softmax((xt @ w_rg + b_rg).astype(jnp.float32), axis=-1)
    g_val, g_idx = lax.top_k(p_g, 1)
    g_val, g_idx = g_val[:, 0], g_idx[:, 0]
    el = jnp.einsum('td,gde->tge', xt, w_re) + b_re
    sel = jnp.take_along_axis(el, g_idx[:, None, None], axis=1)[:, 0].astype(jnp.float32)
    e_val, e_idx = lax.top_k(sel, TOP_K_IN_GROUP)
    gate = (g_val[:, None] * jax.nn.softmax(e_val, axis=-1)).astype(h.dtype)
    expert_id = g_idx[:, None] * EXPERTS_PER_GROUP + e_idx

    A = T * TOP_K_IN_GROUP
    flat_e = expert_id.reshape(-1).astype(jnp.int32)
    flat_tok = jnp.repeat(jnp.arange(T, dtype=jnp.int32), TOP_K_IN_GROUP)
    flat_w = gate.reshape(-1)
    order = jnp.argsort(flat_e, stable=True)
    se, st, sw = flat_e[order], flat_tok[order], flat_w[order]
    counts = jax.ops.segment_sum(jnp.ones_like(flat_e), flat_e, num_segments=N_EXPERTS)
    start = jnp.cumsum(counts) - counts
    padded = ((counts + ROW_BLOCK - 1) // ROW_BLOCK) * ROW_BLOCK
    pad_end = jnp.cumsum(padded)
    pad_start = pad_end - padded
    dest = pad_start[se] + (jnp.arange(A, dtype=jnp.int32) - start[se])
    P = A + N_EXPERTS * ROW_BLOCK
    NB = P // ROW_BLOCK
    row_tok = jnp.full((P,), T, dtype=jnp.int32).at[dest].set(st)
    row_w = jnp.zeros((P,), dtype=h.dtype).at[dest].set(sw)
    block_e = jnp.clip(jnp.searchsorted(pad_end, jnp.arange(NB, dtype=jnp.int32) * ROW_BLOCK,
                                        side='right'), 0, N_EXPERTS - 1)
    xt_pad = jnp.concatenate([xt, jnp.zeros((1, D), dtype=xt.dtype)], axis=0)
    xr = xt_pad[row_tok].reshape(NB, ROW_BLOCK, D)

    def expert_block(args):
        xb, e = args
        a = jax.nn.silu(xb @ w_gate[e]) * (xb @ w_up[e])
        return a @ w_down[e]

    yr = lax.map(expert_block, (xr, block_e)).reshape(P, D)
    y = jax.ops.segment_sum(yr * row_w[:, None], row_tok, num_segments=T + 1)[:T]
    return y.reshape(B, S, D)


def setup_inputs(seed: int = 0) -> dict:
    key = jax.random.key(seed)
    ks = jax.random.split(key, 24)
    f32 = jnp.float32

    def nrm(k, shape, s):
        return jax.random.normal(k, shape, f32) * s

    beta = (8.0 * DEPTH) ** -0.25
    d_in = D_MODEL ** -0.5
    L = DEPTH
    x = jax.random.normal(ks[0], (BATCH, SEQ, D_MODEL), f32)
    mem = jax.random.normal(ks[1], (BATCH, N_MEM, D_MODEL), f32)
    start = jax.random.randint(ks[2], (BATCH, 1), 0, 1024, dtype=jnp.int32)
    positions = start + jnp.arange(SEQ, dtype=jnp.int32)[None, :]
    w_in = nrm(ks[3], (L, D_MODEL, IN_PROJ_W), d_in)
    b_forget = jnp.linspace(1.0, 6.0, N_FOX_HEADS, dtype=f32)[None, :] + nrm(ks[4], (L, N_FOX_HEADS), 0.01)
    sinks = nrm(ks[5], (L, N_SWA_HEADS), 0.5)
    w_mix_out = nrm(ks[6], (L, MIX_WIDTH, D_MODEL), MIX_WIDTH ** -0.5 * beta)
    ln_mix_g = 1.0 + nrm(ks[7], (L, D_MODEL), 0.02)
    ln_mix_b = nrm(ks[8], (L, D_MODEL), 0.02)
    w_xq = nrm(ks[9], (L, D_MODEL, D_MODEL), d_in)
    w_xkv = nrm(ks[10], (L, D_MODEL, 2 * D_MODEL), d_in)
    w_xout = nrm(ks[11], (L, D_MODEL, D_MODEL), d_in * beta)
    ln_x_g = 1.0 + nrm(ks[12], (L, D_MODEL), 0.02)
    ln_x_b = nrm(ks[13], (L, D_MODEL), 0.02)
    w_route_group = nrm(ks[14], (L, D_MODEL, N_GROUPS), d_in)
    b_route_group = nrm(ks[15], (L, N_GROUPS), 0.01)
    w_route_expert = nrm(ks[16], (L, N_GROUPS, D_MODEL, EXPERTS_PER_GROUP), d_in)
    b_route_expert = nrm(ks[17], (L, N_GROUPS, EXPERTS_PER_GROUP), 0.01)
    w_exp_gate = nrm(ks[18], (L, N_EXPERTS, D_MODEL, D_EXPERT), d_in)
    w_exp_up = nrm(ks[19], (L, N_EXPERTS, D_MODEL, D_EXPERT), d_in)
    w_exp_down = nrm(ks[20], (L, N_EXPERTS, D_EXPERT, D_MODEL), D_EXPERT ** -0.5 * beta)
    ln_ffn_g = 1.0 + nrm(ks[21], (L, D_MODEL), 0.02)
    ln_ffn_b = nrm(ks[22], (L, D_MODEL), 0.02)
    return {'x': x, 'mem': mem, 'positions': positions, 'w_in': w_in, 'b_forget': b_forget,
            'sinks': sinks, 'w_mix_out': w_mix_out, 'ln_mix_g': ln_mix_g, 'ln_mix_b': ln_mix_b,
            'w_xq': w_xq, 'w_xkv': w_xkv, 'w_xout': w_xout, 'ln_x_g': ln_x_g, 'ln_x_b': ln_x_b,
            'w_route_group': w_route_group, 'b_route_group': b_route_group,
            'w_route_expert': w_route_expert, 'b_route_expert': b_route_expert,
            'w_exp_gate': w_exp_gate, 'w_exp_up': w_exp_up, 'w_exp_down': w_exp_down,
            'ln_ffn_g': ln_ffn_g, 'ln_ffn_b': ln_ffn_b}


def reference(x, mem, positions, w_in, b_forget, sinks, w_mix_out, ln_mix_g, ln_mix_b,
              w_xq, w_xkv, w_xout, ln_x_g, ln_x_b, w_route_group, b_route_group,
              w_route_expert, b_route_expert, w_exp_gate, w_exp_up, w_exp_down,
              ln_ffn_g, ln_ffn_b):
    alpha = (2.0 * DEPTH) ** 0.25
    h = x
    for l in range(DEPTH):
        h = layer_norm(alpha * h + hybrid_mixer(h, positions, w_in[l], b_forget[l], sinks[l], w_mix_out[l]),
                       ln_mix_g[l], ln_mix_b[l])
        h = layer_norm(alpha * h + memory_cross_attention(h, mem, w_xq[l], w_xkv[l], w_xout[l]),
                       ln_x_g[l], ln_x_b[l])
        h = layer_norm(alpha * h + hierarchical_moe(h, w_route_group[l], b_route_group[l],
                                                    w_route_expert[l], b_route_expert[l],
                                                    w_exp_gate[l], w_exp_up[l], w_exp_down[l]),
                       ln_ffn_g[l], ln_ffn_b[l])
    return h
```

```python
import functools

import jax
import jax.numpy as jnp
import numpy as np
from jax import lax
from jax.experimental import pallas as pl
from jax.experimental.pallas import tpu as pltpu

f32 = jnp.float32
bf16 = jnp.bfloat16
i32 = jnp.int32

D = 1024
HD = 64
N_FOX = 8
N_SWA = 8
N_SWA_KV = 2
FOX_W = 512
SWA_Q_W = 512
SWA_KV_W = 128
WINDOW = 128
ROPE_THETA = 10000.0
N_XH = 4
XHD = 256
N_GROUPS = 4
EPG = 8
N_EXPERTS = 32
D_EXPERT = 512
LN_EPS = 1e-5
NEG = -1e30

LANES = 128
ROW_BLOCK = 128
N_PAIRS = EPG * (EPG - 1) // 2
N_BUCKETS = N_GROUPS * N_PAIRS
XW = D + LANES
VMEM_LIMIT = 56 * 1024 * 1024


def _cparams(sem):
    return pltpu.CompilerParams(dimension_semantics=sem, vmem_limit_bytes=VMEM_LIMIT)


def _layer_norm(v, g, b):
    mu = jnp.mean(v, axis=-1, keepdims=True)
    c = v - mu
    var = jnp.mean(c * c, axis=-1, keepdims=True)
    return c * lax.rsqrt(var + LN_EPS) * g + b


def _dot(a, b):
    return jnp.dot(a, b, preferred_element_type=f32)


def _dot_nt(a, b):
    return lax.dot_general(a, b, (((1,), (1,)), ((), ())), preferred_element_type=f32)


def _inproj_kernel(x_ref, pos_ref, w_ref, wfl_ref, bf_ref, invf_ref,
                   qf_ref, kf_ref, vf_ref, qs_ref, ks_ref, vs_ref, lf_ref):
    tm = x_ref.shape[0]
    xb = x_ref[...].astype(bf16)

    def proj(lo, hi):
        return _dot(xb, w_ref[:, lo:hi])

    qf_ref[...] = (proj(0, 512) * 0.125).astype(bf16)
    kf_ref[...] = proj(512, 1024).astype(bf16)
    vf_ref[...] = proj(1024, 1536).astype(bf16)

    ang = pos_ref[...].astype(f32) * invf_ref[...]
    cos = jnp.cos(ang)
    sin = jnp.sin(ang)
    lane = lax.broadcasted_iota(i32, (tm, LANES), 1)
    lo_half = (lane % HD) < (HD // 2)
    sin_s = jnp.where(lo_half, -sin, sin)

    def rope(z):
        rot = jnp.where(lo_half, pltpu.roll(z, LANES - HD // 2, 1), pltpu.roll(z, HD // 2, 1))
        return z * cos + rot * sin_s

    zq = proj(1536, 2048)
    for g in range(4):
        sl = slice(g * LANES, (g + 1) * LANES)
        qs_ref[:, sl] = (rope(zq[:, sl]) * 0.125).astype(bf16)
    ks_ref[...] = rope(proj(2048, 2176)).astype(bf16)
    vs_ref[...] = proj(2176, 2304).astype(bf16)

    z = _dot_nt(wfl_ref[...], xb) + bf_ref[...]
    lf_ref[...] = jnp.minimum(z, 0.0) - jnp.log(1.0 + jnp.exp(-jnp.abs(z)))


def _in_proj(x2, pos2, w_all, wfl, bfc, invf, tm=512):
    T = x2.shape[0]
    nw = w_all.shape[1]
    row = lambda w: pl.BlockSpec((tm, w), lambda i: (i, 0))
    full = lambda a: pl.BlockSpec(a.shape, lambda i: (0,) * a.ndim)
    return pl.pallas_call(
        _inproj_kernel,
        grid=(T // tm,),
        in_specs=[row(D), row(1), full(w_all), full(wfl), full(bfc), full(invf)],
        out_specs=[row(512), row(512), row(512), row(512), row(128), row(128),
                   pl.BlockSpec((N_FOX, tm), lambda i: (0, i))],
        out_shape=[jax.ShapeDtypeStruct((T, 512), bf16)] * 4
        + [jax.ShapeDtypeStruct((T, 128), bf16)] * 2
        + [jax.ShapeDtypeStruct((N_FOX, T), f32)],
        compiler_params=_cparams(("parallel",)),
        name="in_proj",
    )(x2, pos2, w_all, wfl, bfc, invf)


def _cumsum_kernel(lf_ref, c_ref):
    S = lf_ref.shape[1]
    ch = 256
    r = lax.broadcasted_iota(i32, (ch, ch), 0)
    c = lax.broadcasted_iota(i32, (ch, ch), 1)
    tri = (r <= c).astype(f32)
    carry = jnp.zeros((N_FOX, 1), f32)
    for j in range(S // ch):
        a = lf_ref[:, j * ch:(j + 1) * ch]
        cc = jnp.dot(a, tri, precision=lax.Precision.HIGHEST, preferred_element_type=f32) + carry
        c_ref[:, j * ch:(j + 1) * ch] = cc
        carry = cc[:, ch - 1:ch]


def _cumsum(lf, S):
    T = lf.shape[1]
    spec = pl.BlockSpec((N_FOX, S), lambda b: (0, b))
    return pl.pallas_call(
        _cumsum_kernel, grid=(T // S,), in_specs=[spec], out_specs=spec,
        out_shape=jax.ShapeDtypeStruct((N_FOX, T), f32),
        compiler_params=_cparams(("parallel",)), name="cumsum",
    )(lf)


def _fox_kernel(q_ref, k_ref, v_ref, c_ref, o_ref, *, tq):
    i = pl.program_id(2)
    q = q_ref[0]
    lane = lax.broadcasted_iota(i32, (tq, LANES), 1)
    is_a = lane < HD
    zero = jnp.zeros_like(q)
    q_heads = (jnp.where(is_a, q, zero), jnp.where(is_a, zero, q))
    rr = lax.broadcasted_iota(i32, (tq, tq), 0)
    cc = lax.broadcasted_iota(i32, (tq, tq), 1)
    eye = rr == cc
    causal = rr >= cc
    cq = [jnp.sum(jnp.where(eye, c_ref[0, h, pl.ds(i, 1), :], 0.0), axis=1, keepdims=True)
          for h in range(2)]

    def step(j, carry, masked):
        off = pl.multiple_of(j * tq, tq)
        ks = k_ref[0, pl.ds(off, tq), :]
        vs = v_ref[0, pl.ds(off, tq), :]
        new = []
        for h in range(2):
            m, l, acc = carry[h]
            t = _dot_nt(q_heads[h], ks) - c_ref[0, h, pl.ds(j, 1), :]
            if masked:
                t = jnp.where(causal, t, NEG)
            m_new = jnp.maximum(m, jnp.max(t, axis=1, keepdims=True) + cq[h])
            alpha = jnp.exp(m - m_new)
            p = jnp.exp(t + (cq[h] - m_new))
            l = alpha * l + jnp.sum(p, axis=1, keepdims=True)
            acc = alpha * acc + _dot(p.astype(bf16), vs)
            new.append((m_new, l, acc))
        return tuple(new)

    init = tuple((jnp.full((tq, 1), NEG, f32), jnp.zeros((tq, 1), f32), jnp.zeros((tq, LANES), f32))
                 for _ in range(2))
    carry = lax.fori_loop(0, i, lambda j, c: step(j, c, False), init)
    (_, la, acca), (_, lb, accb) = step(i, carry, True)
    o_ref[0] = jnp.where(is_a, acca / la, accb / lb).astype(bf16)


def _fox(qf, kf, vf, c4, B, S, tq=512):
    nq = S // tq
    kernel = functools.partial(_fox_kernel, tq=tq)
    return pl.pallas_call(
        kernel,
        grid=(B, N_FOX // 2, nq),
        in_specs=[
            pl.BlockSpec((1, tq, LANES), lambda b, hp, i: (b, i, hp)),
            pl.BlockSpec((1, S, LANES), lambda b, hp, i: (b, 0, hp)),
            pl.BlockSpec((1, S, LANES), lambda b, hp, i: (b, 0, hp)),
            pl.BlockSpec((1, 2, nq, tq), lambda b, hp, i: (hp, 0, b, 0)),
        ],
        out_specs=pl.BlockSpec((1, tq, LANES), lambda b, hp, i: (b, i, hp)),
        out_shape=jax.ShapeDtypeStruct((B, S, FOX_W), bf16),
        compiler_params=_cparams(("parallel", "parallel", "arbitrary")),
        name="fox",
    )(qf, kf, vf, c4)


def _swa_kernel(sink_ref, q_ref, k_ref, v_ref, o_ref, *, tq):
    W = WINDOW
    nsub = tq // W
    n0 = pl.program_id(1) * nsub
    lane = lax.broadcasted_iota(i32, (W, LANES), 1)
    is0 = lane < HD
    rows = lax.broadcasted_iota(i32, (4 * W, 2 * W), 0)
    cols = lax.broadcasted_iota(i32, (4 * W, 2 * W), 1)
    rgrp = lax.broadcasted_iota(i32, (4 * W, 1), 0) // W
    for r in range(nsub):
        nb = n0 + r
        kstart = pl.multiple_of(jnp.maximum(nb * W - W, 0), W)
        ks = k_ref[0, pl.ds(kstart, 2 * W), :]
        vs = v_ref[0, pl.ds(kstart, 2 * W), :]
        qpos = nb * W + rows % W
        kpos = kstart + cols
        valid = (kpos <= qpos) & (qpos - kpos < W)
        outs = []
        for kv in range(2):
            keep = is0 if kv == 0 else jnp.logical_not(is0)
            parts = []
            for g in range(4):
                qg = q_ref[0, r * W:(r + 1) * W, g * LANES:(g + 1) * LANES]
                parts.append(jnp.where(keep, qg, jnp.zeros_like(qg)))
            qstack = jnp.concatenate(parts, axis=0)
            s = jnp.where(valid, _dot_nt(qstack, ks), NEG)
            sink = jnp.zeros((4 * W, 1), f32)
            for g in range(4):
                sink = jnp.where(rgrp == g, sink_ref[kv * 4 + g], sink)
            m = jnp.maximum(jnp.max(s, axis=1, keepdims=True), sink)
            e = jnp.exp(s - m)
            den = jnp.sum(e, axis=1, keepdims=True) + jnp.exp(sink - m)
            outs.append(_dot(e.astype(bf16), vs) / den)
        for g in range(4):
            og = jnp.where(is0, outs[0][g * W:(g + 1) * W], outs[1][g * W:(g + 1) * W])
            o_ref[0, r * W:(r + 1) * W, g * LANES:(g + 1) * LANES] = og.astype(bf16)


def _swa(sinks, qs, ks, vs, B, S, tq=512):
    kernel = functools.partial(_swa_kernel, tq=tq)
    return pl.pallas_call(
        kernel,
        grid=(B, S // tq),
        in_specs=[
            pl.BlockSpec(memory_space=pltpu.SMEM),
            pl.BlockSpec((1, tq, SWA_Q_W), lambda b, i: (b, i, 0)),
            pl.BlockSpec((1, S, SWA_KV_W), lambda b, i: (b, 0, 0)),
            pl.BlockSpec((1, S, SWA_KV_W), lambda b, i: (b, 0, 0)),
        ],
        out_specs=pl.BlockSpec((1, tq, SWA_Q_W), lambda b, i: (b, i, 0)),
        out_shape=jax.ShapeDtypeStruct((B, S, SWA_Q_W), bf16),
        compiler_params=_cparams(("parallel", "arbitrary")),
        name="swa",
    )(sinks, qs, ks, vs)


def _kvproj_kernel(m_ref, w_ref, k_ref, v_ref):
    mb = m_ref[...].astype(bf16)
    k_ref[...] = _dot(mb, w_ref[:, :D]).astype(bf16)
    v_ref[...] = _dot(mb, w_ref[:, D:]).astype(bf16)


def _kvproj(mem2, w_xkv, tm=512):
    R = mem2.shape[0]
    row = pl.BlockSpec((tm, D), lambda i: (i, 0))
    return pl.pallas_call(
        _kvproj_kernel, grid=(R // tm,),
        in_specs=[row, pl.BlockSpec(w_xkv.shape, lambda i: (0, 0))],
        out_specs=[row, row],
        out_shape=[jax.ShapeDtypeStruct((R, D), bf16)] * 2,
        compiler_params=_cparams(("parallel",)), name="kvproj",
    )(mem2, w_xkv)


def _mid_kernel(x_ref, of_ref, os_ref, wo_ref, g1_ref, b1_ref, wq_ref, k_ref, v_ref,
                wxo_ref, g2_ref, b2_ref, wrh_ref, wrl_ref, br_ref,
                h2_ref, bk_ref, oc_ref, *, alpha):
    tm = x_ref.shape[0]
    mix = _dot(of_ref[...], wo_ref[:FOX_W, :]) + _dot(os_ref[...], wo_ref[FOX_W:, :])
    h1 = _layer_norm(alpha * x_ref[...] + mix, g1_ref[...], b1_ref[...])

    q = (_dot(h1.astype(bf16), wq_ref[...]) * 0.0625).astype(bf16)
    for h in range(N_XH):
        sl = slice(h * XHD, (h + 1) * XHD)
        s = _dot_nt(q[:, sl], k_ref[:, sl])
        e = jnp.exp(s - jnp.max(s, axis=1, keepdims=True))
        p = e / jnp.sum(e, axis=1, keepdims=True)
        oc_ref[:, sl] = _dot(p.astype(bf16), v_ref[:, sl]).astype(bf16)
    xo = _dot(oc_ref[...], wxo_ref[...])
    h2 = _layer_norm(alpha * h1 + xo, g2_ref[...], b2_ref[...])
    h2_ref[:, :D] = h2

    hh = h2.astype(bf16)
    hl = (h2 - hh.astype(f32)).astype(bf16)
    lg = _dot(hh, wrh_ref[...]) + _dot(hl, wrh_ref[...]) + _dot(hh, wrl_ref[...]) + br_ref[...]

    lane = lax.broadcasted_iota(i32, (tm, LANES), 1).astype(f32)
    big = float(LANES)

    def first_max(vals, mask):
        vm = jnp.where(mask, vals, NEG)
        top = jnp.max(vm, axis=1, keepdims=True)
        idx = jnp.min(jnp.where(mask & (vm == top), lane, big), axis=1, keepdims=True)
        return top, idx

    gmask = lane < float(N_GROUPS)
    gmax, gidx = first_max(lg, gmask)
    g_val = 1.0 / jnp.sum(jnp.where(gmask, jnp.exp(lg - gmax), 0.0), axis=1, keepdims=True)
    lo = float(N_GROUPS) + float(EPG) * gidx
    emask = (lane >= lo) & (lane < lo + float(EPG))
    v1, i1 = first_max(lg, emask)
    v2, i2 = first_max(lg, emask & (lane != i1))
    ex = jnp.exp(v2 - v1)
    w1 = g_val * (1.0 / (1.0 + ex))
    w2 = g_val * (ex / (1.0 + ex))
    e1 = i1 - lo
    e2 = i2 - lo
    first_low = e1 < e2
    ea = jnp.where(first_low, e1, e2)
    eb = jnp.where(first_low, e2, e1)
    ga = jnp.where(first_low, w1, w2)
    gb = jnp.where(first_low, w2, w1)
    pidx = ea * float(EPG - 1) - ea * (ea - 1.0) * 0.5 + (eb - ea - 1.0)
    bucket = gidx * float(N_PAIRS) + pidx

    lane_i = lax.broadcasted_iota(i32, (tm, LANES), 1)
    h2_ref[:, D:] = jnp.where(lane_i == 0, ga, jnp.where(lane_i == 1, gb, 0.0))
    bk_t = jnp.transpose(jnp.broadcast_to(bucket, (tm, LANES)))
    bk_ref[...] = bk_t[:8, :]


def _mid(x2, of2, os2, w_out, g1, b1, wq, kx, vx, wxo, g2, b2, wrh, wrl, br, alpha, S, tm=256):
    T = x2.shape[0]
    M = kx.shape[0] // (T // S)
    per_b = S // tm
    row = lambda w: pl.BlockSpec((tm, w), lambda i: (i, 0))
    full = lambda a: pl.BlockSpec(a.shape, lambda i: (0,) * a.ndim)
    kvspec = pl.BlockSpec((M, D), lambda i: (i // per_b, 0))
    kernel = functools.partial(_mid_kernel, alpha=alpha)
    return pl.pallas_call(
        kernel,
        grid=(T // tm,),
        in_specs=[row(D), row(512), row(512), full(w_out), full(g1), full(b1), full(wq),
                  kvspec, kvspec, full(wxo), full(g2), full(b2), full(wrh), full(wrl), full(br)],
        out_specs=[row(XW), pl.BlockSpec((8, tm), lambda i: (0, i))],
        out_shape=[jax.ShapeDtypeStruct((T, XW), f32), jax.ShapeDtypeStruct((8, T), f32)],
        scratch_shapes=[pltpu.VMEM((tm, D), bf16)],
        compiler_params=_cparams(("parallel",)),
        name="mid",
    )(x2, of2, os2, w_out, g1, b1, wq, kx, vx, wxo, g2, b2, wrh, wrl, br)


def _rank_kernel(bk_ref, rank_ref, cnt_ref, carry_ref, *, chunk):
    sub = 256

    @pl.when(pl.program_id(0) == 0)
    def _():
        carry_ref[...] = jnp.zeros_like(carry_ref)

    r = lax.broadcasted_iota(i32, (sub, sub), 0)
    c = lax.broadcasted_iota(i32, (sub, sub), 1)
    before = (r < c).astype(bf16)
    bid = lax.broadcasted_iota(i32, (LANES, sub), 0).astype(f32)
    carry = carry_ref[...]
    for j in range(chunk // sub):
        bk = bk_ref[0:1, j * sub:(j + 1) * sub]
        hit = bid == bk
        oh = jnp.where(hit, 1.0, 0.0)
        prior = _dot(oh.astype(bf16), before) + carry
        rank_ref[:, j * sub:(j + 1) * sub] = jnp.sum(jnp.where(hit, prior, 0.0), axis=0, keepdims=True)
        carry = carry + jnp.sum(oh, axis=1, keepdims=True)
    carry_ref[...] = carry
    cnt_ref[...] = carry


def _rank(bk8, chunk=2048):
    T = bk8.shape[1]
    kernel = functools.partial(_rank_kernel, chunk=chunk)
    return pl.pallas_call(
        kernel, grid=(T // chunk,),
        in_specs=[pl.BlockSpec((8, chunk), lambda i: (0, i))],
        out_specs=[pl.BlockSpec((1, chunk), lambda i: (0, i)),
                   pl.BlockSpec((LANES, 1), lambda i: (0, 0))],
        out_shape=[jax.ShapeDtypeStruct((1, T), f32), jax.ShapeDtypeStruct((LANES, 1), f32)],
        scratch_shapes=[pltpu.VMEM((LANES, 1), f32)],
        compiler_params=_cparams(("arbitrary",)), name="rank",
    )(bk8)


def _row_copy(src, dst, sem, s, d):
    return pltpu.make_async_copy(src.at[pl.ds(s, 1)], dst.at[pl.ds(d, 1)], sem)


def _scatter_kernel(bk_ref, rk_ref, ps_ref, src_ref, init_ref, dst_ref, sem, *, chunk):
    del init_ref
    base = pl.program_id(0) * chunk

    def issue(t, _):
        d = ps_ref[bk_ref[t]] + rk_ref[t]
        _row_copy(src_ref, dst_ref, sem, base + t, d).start()
        return 0

    lax.fori_loop(0, chunk, issue, 0)
    pltpu.make_async_copy(src_ref.at[pl.ds(0, chunk)], dst_ref.at[pl.ds(0, chunk)], sem).wait()


def _scatter_rows(bk, rk, ps, src, init, chunk=2048):
    T = src.shape[0]
    sm = lambda: pl.BlockSpec((chunk,), lambda i: (i,), memory_space=pltpu.SMEM)
    kernel = functools.partial(_scatter_kernel, chunk=chunk)
    return pl.pallas_call(
        kernel, grid=(T // chunk,),
        in_specs=[sm(), sm(), pl.BlockSpec(memory_space=pltpu.SMEM),
                  pl.BlockSpec(memory_space=pl.ANY), pl.BlockSpec(memory_space=pl.ANY)],
        out_specs=pl.BlockSpec(memory_space=pl.ANY),
        out_shape=jax.ShapeDtypeStruct(init.shape, init.dtype),
        scratch_shapes=[pltpu.SemaphoreType.DMA(())],
        input_output_aliases={4: 0},
        compiler_params=pltpu.CompilerParams(dimension_semantics=("arbitrary",), has_side_effects=True),
        name="scatter_rows",
    )(bk, rk, ps, src, init)


def _gather_kernel(bk_ref, rk_ref, ps_ref, src_ref, dst_ref, sem, *, chunk):
    base = pl.program_id(0) * chunk

    def issue(t, _):
        s = ps_ref[bk_ref[t]] + rk_ref[t]
        _row_copy(src_ref, dst_ref, sem, s, base + t).start()
        return 0

    lax.fori_loop(0, chunk, issue, 0)
    pltpu.make_async_copy(src_ref.at[pl.ds(0, chunk)], dst_ref.at[pl.ds(0, chunk)], sem).wait()


def _gather_rows(bk, rk, ps, src, T, chunk=2048):
    sm = lambda: pl.BlockSpec((chunk,), lambda i: (i,), memory_space=pltpu.SMEM)
    kernel = functools.partial(_gather_kernel, chunk=chunk)
    return pl.pallas_call(
        kernel, grid=(T // chunk,),
        in_specs=[sm(), sm(), pl.BlockSpec(memory_space=pltpu.SMEM), pl.BlockSpec(memory_space=pl.ANY)],
        out_specs=pl.BlockSpec(memory_space=pl.ANY),
        out_shape=jax.ShapeDtypeStruct((T, src.shape[1]), src.dtype),
        scratch_shapes=[pltpu.SemaphoreType.DMA(())],
        compiler_params=pltpu.CompilerParams(dimension_semantics=("arbitrary",), has_side_effects=True),
        name="gather_rows",
    )(bk, rk, ps, src)


def _expert_kernel(ea_ref, eb_ref, used_ref, xs_ref, wga_ref, wua_ref, wda_ref,
                   wgb_ref, wub_ref, wdb_ref, y_ref):
    n = pl.program_id(0)

    @pl.when(n < used_ref[0])
    def _():
        x = xs_ref[:, :D].astype(bf16)

        def expert(wg, wu, wd):
            a = _dot(x, wg[0])
            u = _dot(x, wu[0])
            act = a * (1.0 / (1.0 + jnp.exp(-a))) * u
            return _dot(act.astype(bf16), wd[0])

        ga = xs_ref[:, D:D + 1]
        gb = xs_ref[:, D + 1:D + 2]
        y_ref[...] = ga * expert(wga_ref, wua_ref, wda_ref) + gb * expert(wgb_ref, wub_ref, wdb_ref)

    @pl.when(n >= used_ref[0])
    def _():
        y_ref[...] = jnp.zeros_like(y_ref)


def _experts(ea, eb, used, xs, wg, wu, wd):
    P = xs.shape[0]
    nblk = P // ROW_BLOCK

    def xmap(n, ea, eb, used):
        return (jnp.minimum(n, used[0] - 1), 0)

    wa = lambda n, ea, eb, used: (ea[n], 0, 0)
    wb = lambda n, ea, eb, used: (eb[n], 0, 0)
    wgs = lambda m: pl.BlockSpec((1, D, D_EXPERT), m)
    wds = lambda m: pl.BlockSpec((1, D_EXPERT, D), m)
    grid_spec = pltpu.PrefetchScalarGridSpec(
        num_scalar_prefetch=3, grid=(nblk,),
        in_specs=[pl.BlockSpec((ROW_BLOCK, XW), xmap), wgs(wa), wgs(wa), wds(wa), wgs(wb), wgs(wb), wds(wb)],
        out_specs=pl.BlockSpec((ROW_BLOCK, D), lambda n, ea, eb, used: (n, 0)),
    )
    return pl.pallas_call(
        _expert_kernel, grid_spec=grid_spec,
        out_shape=jax.ShapeDtypeStruct((P, D), f32),
        compiler_params=_cparams(("arbitrary",)), name="experts",
    )(ea, eb, used, xs, wg, wu, wd, wg, wu, wd)


def _final_kernel(h_ref, y_ref, g_ref, b_ref, o_ref, *, alpha):
    o_ref[...] = _layer_norm(alpha * h_ref[...] + y_ref[...], g_ref[...], b_ref[...])


def _final(h2x, y, g, b, alpha, tm=512):
    T = y.shape[0]
    row = pl.BlockSpec((tm, D), lambda i: (i, 0))
    vec = pl.BlockSpec((1, D), lambda i: (0, 0))
    return pl.pallas_call(
        functools.partial(_final_kernel, alpha=alpha), grid=(T // tm,),
        in_specs=[row, row, vec, vec], out_specs=row,
        out_shape=jax.ShapeDtypeStruct((T, D), f32),
        compiler_params=_cparams(("parallel",)), name="final_ln",
    )(h2x, y, g, b)


def _pair_tables():
    ea = np.zeros((LANES,), np.int32)
    eb = np.zeros((LANES,), np.int32)
    for g in range(N_GROUPS):
        k = 0
        for a in range(EPG):
            for b in range(a + 1, EPG):
                ea[g * N_PAIRS + k] = g * EPG + a
                eb[g * N_PAIRS + k] = g * EPG + b
                k += 1
    return ea, eb


_PAIR_A, _PAIR_B = _pair_tables()


def _layer(h, mem, positions, w_in, b_forget, sinks, w_mix_out, ln_mix_g, ln_mix_b,
           w_xq, w_xkv, w_xout, ln_x_g, ln_x_b, w_rg, b_rg, w_re, b_re,
           w_eg, w_eu, w_ed, ln_f_g, ln_f_b, alpha):
    B, S, _ = h.shape
    T = B * S
    x2 = h.reshape(T, D)
    pos2 = positions.reshape(T, 1).astype(i32)

    o = np.cumsum((0, FOX_W, FOX_W, FOX_W, N_FOX, SWA_Q_W, SWA_KV_W, SWA_KV_W))
    w_qf, w_kf, w_vf, w_fl, w_qs, w_ks, w_vs = (w_in[:, o[i]:o[i + 1]] for i in range(7))
    perm = np.concatenate([np.concatenate([np.arange(j * HD, (j + 1) * HD),
                                           np.arange((j + 4) * HD, (j + 5) * HD)]) for j in range(4)])
    w_all = jnp.concatenate([w_qf, w_kf, w_vf, w_qs[:, perm], w_ks, w_vs], axis=1).astype(bf16)
    wfl = w_fl.T.astype(bf16)
    bfc = b_forget.reshape(N_FOX, 1).astype(f32)
    half = HD // 2
    inv_freq = ROPE_THETA ** (-jnp.arange(half, dtype=f32) / half)
    invf = jnp.tile(inv_freq, LANES // half).reshape(1, LANES)
    w_out = jnp.concatenate([w_mix_out[:FOX_W], w_mix_out[FOX_W:][perm]], axis=0).astype(bf16)

    qf, kf, vf, qs, ks, vs, lf = _in_proj(x2, pos2, w_all, wfl, bfc, invf)
    c = _cumsum(lf, S)
    tq = 512
    c4 = c.reshape(N_FOX // 2, 2, T // tq, tq)
    r3 = lambda a: a.reshape(B, S, a.shape[-1])
    o_fox = _fox(r3(qf), r3(kf), r3(vf), c4, B, S, tq=tq)
    o_swa = _swa(sinks.astype(f32), r3(qs), r3(ks), r3(vs), B, S)

    kx, vx = _kvproj(mem.reshape(-1, D), w_xkv.astype(bf16))

    wr = jnp.concatenate([w_rg, jnp.transpose(w_re, (1, 0, 2)).reshape(D, N_EXPERTS)], axis=1)
    wr = jnp.pad(wr, ((0, 0), (0, LANES - wr.shape[1]))).astype(f32)
    wrh = wr.astype(bf16)
    wrl = (wr - wrh.astype(f32)).astype(bf16)
    br = jnp.pad(jnp.concatenate([b_rg, b_re.reshape(-1)]), (0, LANES - N_GROUPS - N_EXPERTS))
    br = br.reshape(1, LANES).astype(f32)
    v2 = lambda a: a.reshape(1, D).astype(f32)
    h2x, bk8 = _mid(x2, o_fox.reshape(T, FOX_W), o_swa.reshape(T, SWA_Q_W), w_out,
                    v2(ln_mix_g), v2(ln_mix_b), w_xq.astype(bf16), kx, vx, w_xout.astype(bf16),
                    v2(ln_x_g), v2(ln_x_b), wrh, wrl, br, alpha, S)

    rank, cnt = _rank(bk8)
    bucket_i = bk8[0].astype(i32)
    rank_i = rank[0].astype(i32)
    counts = cnt[:, 0].astype(i32)
    padded = ((counts + ROW_BLOCK - 1) // ROW_BLOCK) * ROW_BLOCK
    pad_end = jnp.cumsum(padded)
    pad_start = (pad_end - padded).astype(i32)
    P = T + N_BUCKETS * ROW_BLOCK
    nblk = P // ROW_BLOCK
    used = (pad_end[-1] // ROW_BLOCK).astype(i32).reshape(1)
    blk_bucket = jnp.clip(jnp.searchsorted(pad_end, jnp.arange(nblk, dtype=i32) * ROW_BLOCK, side='right'),
                          0, N_BUCKETS - 1)
    blk_a = jnp.asarray(_PAIR_A)[blk_bucket]
    blk_b = jnp.asarray(_PAIR_B)[blk_bucket]

    xs = _scatter_rows(bucket_i, rank_i, pad_start, h2x, jnp.zeros((P, XW), f32))
    ys = _experts(blk_a, blk_b, used, xs, w_eg.astype(bf16), w_eu.astype(bf16), w_ed.astype(bf16))
    y = _gather_rows(bucket_i, rank_i, pad_start, ys, T)
    out = _final(h2x, y, v2(ln_f_g), v2(ln_f_b), alpha)
    return out.reshape(B, S, D)


def kernel(x, mem, positions, w_in, b_forget, sinks, w_mix_out, ln_mix_g, ln_mix_b, w_xq, w_xkv, w_xout,
           ln_x_g, ln_x_b, w_route_group, b_route_group, w_route_expert, b_route_expert,
           w_exp_gate, w_exp_up, w_exp_down, ln_ffn_g, ln_ffn_b):
    depth = w_in.shape[0]
    alpha = (2.0 * depth) ** 0.25
    h = x
    for l in range(depth):
        h = _layer(h, mem, positions, w_in[l], b_forget[l], sinks[l], w_mix_out[l], ln_mix_g[l], ln_mix_b[l],
                   w_xq[l], w_xkv[l], w_xout[l], ln_x_g[l], ln_x_b[l], w_route_group[l], b_route_group[l],
                   w_route_expert[l], b_route_expert[l], w_exp_gate[l], w_exp_up[l], w_exp_down[l],
                   ln_ffn_g[l], ln_ffn_b[l], alpha)
    return h
```

```python
import functools

import jax
import jax.numpy as jnp
import numpy as np
from jax import lax
from jax.experimental import pallas as pl
from jax.experimental.pallas import tpu as pltpu

f32 = jnp.float32
bf16 = jnp.bfloat16
i32 = jnp.int32

D = 1024
HD = 64
N_FOX = 8
N_SWA = 8
N_SWA_KV = 2
FOX_W = 512
SWA_Q_W = 512
SWA_KV_W = 128
WINDOW = 128
ROPE_THETA = 10000.0
N_XH = 4
XHD = 256
N_GROUPS = 4
EPG = 8
N_EXPERTS = 32
D_EXPERT = 512
LN_EPS = 1e-5
NEG = -1e30

LANES = 128
ROW_BLOCK = 128
N_PAIRS = EPG * (EPG - 1) // 2
N_BUCKETS = N_GROUPS * N_PAIRS
XW = D + LANES
VMEM_LIMIT = 56 * 1024 * 1024


def _cparams(sem):
    return pltpu.CompilerParams(dimension_semantics=sem, vmem_limit_bytes=VMEM_LIMIT)


def _layer_norm(v, g, b):
    mu = jnp.mean(v, axis=-1, keepdims=True)
    c = v - mu
    var = jnp.mean(c * c, axis=-1, keepdims=True)
    return c * lax.rsqrt(var + LN_EPS) * g + b


def _dot(a, b):
    return jnp.dot(a, b, preferred_element_type=f32)


def _dot_nt(a, b):
    return lax.dot_general(a, b, (((1,), (1,)), ((), ())), preferred_element_type=f32)


def _inproj_kernel(x_ref, pos_ref, w_ref, wfl_ref, bf_ref, invf_ref,
                   qf_ref, kf_ref, vf_ref, qs_ref, ks_ref, vs_ref, lf_ref):
    tm = x_ref.shape[0]
    xb = x_ref[...].astype(bf16)

    def proj(lo, hi):
        return _dot(xb, w_ref[:, lo:hi])

    qf_ref[...] = (proj(0, 512) * 0.125).astype(bf16)
    kf_ref[...] = proj(512, 1024).astype(bf16)
    vf_ref[...] = proj(1024, 1536).astype(bf16)

    ang = pos_ref[...].astype(f32) * invf_ref[...]
    cos = jnp.cos(ang)
    sin = jnp.sin(ang)
    lane = lax.broadcasted_iota(i32, (tm, LANES), 1)
    lo_half = (lane % HD) < (HD // 2)
    sin_s = jnp.where(lo_half, -sin, sin)

    def rope(z):
        rot = jnp.where(lo_half, pltpu.roll(z, LANES - HD // 2, 1), pltpu.roll(z, HD // 2, 1))
        return z * cos + rot * sin_s

    zq = proj(1536, 2048)
    for g in range(4):
        sl = slice(g * LANES, (g + 1) * LANES)
        qs_ref[:, sl] = (rope(zq[:, sl]) * 0.125).astype(bf16)
    ks_ref[...] = rope(proj(2048, 2176)).astype(bf16)
    vs_ref[...] = proj(2176, 2304).astype(bf16)

    z = _dot_nt(wfl_ref[...], xb) + bf_ref[...]
    lf_ref[...] = jnp.minimum(z, 0.0) - jnp.log(1.0 + jnp.exp(-jnp.abs(z)))


def _in_proj(x2, pos2, w_all, wfl, bfc, invf, tm=512):
    T = x2.shape[0]
    nw = w_all.shape[1]
    row = lambda w: pl.BlockSpec((tm, w), lambda i: (i, 0))
    full = lambda a: pl.BlockSpec(a.shape, lambda i: (0,) * a.ndim)
    return pl.pallas_call(
        _inproj_kernel,
        grid=(T // tm,),
        in_specs=[row(D), row(1), full(w_all), full(wfl), full(bfc), full(invf)],
        out_specs=[row(512), row(512), row(512), row(512), row(128), row(128),
                   pl.BlockSpec((N_FOX, tm), lambda i: (0, i))],
        out_shape=[jax.ShapeDtypeStruct((T, 512), bf16)] * 4
        + [jax.ShapeDtypeStruct((T, 128), bf16)] * 2
        + [jax.ShapeDtypeStruct((N_FOX, T), f32)],
        compiler_params=_cparams(("parallel",)),
        name="in_proj",
    )(x2, pos2, w_all, wfl, bfc, invf)


def _cumsum_kernel(lf_ref, c_ref):
    S = lf_ref.shape[1]
    ch = 256
    r = lax.broadcasted_iota(i32, (ch, ch), 0)
    c = lax.broadcasted_iota(i32, (ch, ch), 1)
    tri = (r <= c).astype(f32)
    carry = jnp.zeros((N_FOX, 1), f32)
    for j in range(S // ch):
        a = lf_ref[:, j * ch:(j + 1) * ch]
        cc = jnp.dot(a, tri, precision=lax.Precision.HIGHEST, preferred_element_type=f32) + carry
        c_ref[:, j * ch:(j + 1) * ch] = cc
        carry = cc[:, ch - 1:ch]


def _cumsum(lf, S):
    T = lf.shape[1]
    spec = pl.BlockSpec((N_FOX, S), lambda b: (0, b))
    return pl.pallas_call(
        _cumsum_kernel, grid=(T // S,), in_specs=[spec], out_specs=spec,
        out_shape=jax.ShapeDtypeStruct((N_FOX, T), f32),
        compiler_params=_cparams(("parallel",)), name="cumsum",
    )(lf)


def _fox_kernel(q_ref, k_ref, v_ref, c_ref, o_ref, *, tq):
    i = pl.program_id(2)
    q = q_ref[0]
    lane = lax.broadcasted_iota(i32, (tq, LANES), 1)
    is_a = lane < HD
    zero = jnp.zeros_like(q)
    q_heads = (jnp.where(is_a, q, zero), jnp.where(is_a, zero, q))
    rr = lax.broadcasted_iota(i32, (tq, tq), 0)
    cc = lax.broadcasted_iota(i32, (tq, tq), 1)
    eye = rr == cc
    causal = rr >= cc
    cq = [jnp.sum(jnp.where(eye, c_ref[0, h, pl.ds(i, 1), :], 0.0), axis=1, keepdims=True)
          for h in range(2)]

    def step(j, carry, masked):
        off = pl.multiple_of(j * tq, tq)
        ks = k_ref[0, pl.ds(off, tq), :]
        vs = v_ref[0, pl.ds(off, tq), :]
        new = []
        for h in range(2):
            m, l, acc = carry[h]
            t = _dot_nt(q_heads[h], ks) - c_ref[0, h, pl.ds(j, 1), :]
            if masked:
                t = jnp.where(causal, t, NEG)
            m_new = jnp.maximum(m, jnp.max(t, axis=1, keepdims=True) + cq[h])
            alpha = jnp.exp(m - m_new)
            p = jnp.exp(t + (cq[h] - m_new))
            l = alpha * l + jnp.sum(p, axis=1, keepdims=True)
            acc = alpha * acc + _dot(p.astype(bf16), vs)
            new.append((m_new, l, acc))
        return tuple(new)

    init = tuple((jnp.full((tq, 1), NEG, f32), jnp.zeros((tq, 1), f32), jnp.zeros((tq, LANES), f32))
                 for _ in range(2))
    carry = lax.fori_loop(0, i, lambda j, c: step(j, c, False), init)
    (_, la, acca), (_, lb, accb) = step(i, carry, True)
    o_ref[0] = jnp.where(is_a, acca / la, accb / lb).astype(bf16)


def _fox(qf, kf, vf, c4, B, S, tq=512):
    nq = S // tq
    kernel = functools.partial(_fox_kernel, tq=tq)
    return pl.pallas_call(
        kernel,
        grid=(B, N_FOX // 2, nq),
        in_specs=[
            pl.BlockSpec((1, tq, LANES), lambda b, hp, i: (b, i, hp)),
            pl.BlockSpec((1, S, LANES), lambda b, hp, i: (b, 0, hp)),
            pl.BlockSpec((1, S, LANES), lambda b, hp, i: (b, 0, hp)),
            pl.BlockSpec((1, 2, nq, tq), lambda b, hp, i: (hp, 0, b, 0)),
        ],
        out_specs=pl.BlockSpec((1, tq, LANES), lambda b, hp, i: (b, i, hp)),
        out_shape=jax.ShapeDtypeStruct((B, S, FOX_W), bf16),
        compiler_params=_cparams(("parallel", "parallel", "arbitrary")),
        name="fox",
    )(qf, kf, vf, c4)


def _swa_kernel(sink_ref, q_ref, k_ref, v_ref, o_ref, *, tq):
    W = WINDOW
    nsub = tq // W
    n0 = pl.program_id(1) * nsub
    lane = lax.broadcasted_iota(i32, (W, LANES), 1)
    is0 = lane < HD
    rows = lax.broadcasted_iota(i32, (4 * W, 2 * W), 0)
    cols = lax.broadcasted_iota(i32, (4 * W, 2 * W), 1)
    rgrp = lax.broadcasted_iota(i32, (4 * W, 1), 0) // W
    for r in range(nsub):
        nb = n0 + r
        kstart = pl.multiple_of(jnp.maximum(nb * W - W, 0), W)
        ks = k_ref[0, pl.ds(kstart, 2 * W), :]
        vs = v_ref[0, pl.ds(kstart, 2 * W), :]
        qpos = nb * W + rows % W
        kpos = kstart + cols
        valid = (kpos <= qpos) & (qpos - kpos < W)
        outs = []
        for kv in range(2):
            keep = is0 if kv == 0 else jnp.logical_not(is0)
            parts = []
            for g in range(4):
                qg = q_ref[0, r * W:(r + 1) * W, g * LANES:(g + 1) * LANES]
                parts.append(jnp.where(keep, qg, jnp.zeros_like(qg)))
            qstack = jnp.concatenate(parts, axis=0)
            s = jnp.where(valid, _dot_nt(qstack, ks), NEG)
            sink = jnp.zeros((4 * W, 1), f32)
            for g in range(4):
                sink = jnp.where(rgrp == g, sink_ref[kv * 4 + g], sink)
            m = jnp.maximum(jnp.max(s, axis=1, keepdims=True), sink)
            e = jnp.exp(s - m)
            den = jnp.sum(e, axis=1, keepdims=True) + jnp.exp(sink - m)
            outs.append(_dot(e.astype(bf16), vs) / den)
        for g in range(4):
            og = jnp.where(is0, outs[0][g * W:(g + 1) * W], outs[1][g * W:(g + 1) * W])
            o_ref[0, r * W:(r + 1) * W, g * LANES:(g + 1) * LANES] = og.astype(bf16)


def _swa(sinks, qs, ks, vs, B, S, tq=512):
    kernel = functools.partial(_swa_kernel, tq=tq)
    return pl.pallas_call(
        kernel,
        grid=(B, S // tq),
        in_specs=[
            pl.BlockSpec(memory_space=pltpu.SMEM),
            pl.BlockSpec((1, tq, SWA_Q_W), lambda b, i: (b, i, 0)),
            pl.BlockSpec((1, S, SWA_KV_W), lambda b, i: (b, 0, 0)),
            pl.BlockSpec((1, S, SWA_KV_W), lambda b, i: (b, 0, 0)),
        ],
        out_specs=pl.BlockSpec((1, tq, SWA_Q_W), lambda b, i: (b, i, 0)),
        out_shape=jax.ShapeDtypeStruct((B, S, SWA_Q_W), bf16),
        compiler_params=_cparams(("parallel", "arbitrary")),
        name="swa",
    )(sinks, qs, ks, vs)


def _kvproj_kernel(m_ref, w_ref, k_ref, v_ref):
    mb = m_ref[...].astype(bf16)
    k_ref[...] = _dot(mb, w_ref[:, :D]).astype(bf16)
    v_ref[...] = _dot(mb, w_ref[:, D:]).astype(bf16)


def _kvproj(mem2, w_xkv, tm=512):
    R = mem2.shape[0]
    row = pl.BlockSpec((tm, D), lambda i: (i, 0))
    return pl.pallas_call(
        _kvproj_kernel, grid=(R // tm,),
        in_specs=[row, pl.BlockSpec(w_xkv.shape, lambda i: (0, 0))],
        out_specs=[row, row],
        out_shape=[jax.ShapeDtypeStruct((R, D), bf16)] * 2,
        compiler_params=_cparams(("parallel",)), name="kvproj",
    )(mem2, w_xkv)


def _mid_kernel(x_ref, of_ref, os_ref, wo_ref, g1_ref, b1_ref, wq_ref, k_ref, v_ref,
                wxo_ref, g2_ref, b2_ref, wrh_ref, wrl_ref, br_ref,
                h2_ref, bk_ref, oc_ref, *, alpha):
    tm = x_ref.shape[0]
    mix = _dot(of_ref[...], wo_ref[:FOX_W, :]) + _dot(os_ref[...], wo_ref[FOX_W:, :])
    h1 = _layer_norm(alpha * x_ref[...] + mix, g1_ref[...], b1_ref[...])

    q = (_dot(h1.astype(bf16), wq_ref[...]) * 0.0625).astype(bf16)
    for h in range(N_XH):
        sl = slice(h * XHD, (h + 1) * XHD)
        s = _dot_nt(q[:, sl], k_ref[:, sl])
        e = jnp.exp(s - jnp.max(s, axis=1, keepdims=True))
        p = e / jnp.sum(e, axis=1, keepdims=True)
        oc_ref[:, sl] = _dot(p.astype(bf16), v_ref[:, sl]).astype(bf16)
    xo = _dot(oc_ref[...], wxo_ref[...])
    h2 = _layer_norm(alpha * h1 + xo, g2_ref[...], b2_ref[...])
    h2_ref[:, :D] = h2

    hh = h2.astype(bf16)
    hl = (h2 - hh.astype(f32)).astype(bf16)
    lg = _dot(hh, wrh_ref[...]) + _dot(hl, wrh_ref[...]) + _dot(hh, wrl_ref[...]) + br_ref[...]

    lane = lax.broadcasted_iota(i32, (tm, LANES), 1).astype(f32)
    big = float(LANES)

    def first_max(vals, mask):
        vm = jnp.where(mask, vals, NEG)
        top = jnp.max(vm, axis=1, keepdims=True)
        idx = jnp.min(jnp.where(mask & (vm == top), lane, big), axis=1, keepdims=True)
        return top, idx

    gmask = lane < float(N_GROUPS)
    gmax, gidx = first_max(lg, gmask)
    g_val = 1.0 / jnp.sum(jnp.where(gmask, jnp.exp(lg - gmax), 0.0), axis=1, keepdims=True)
    lo = float(N_GROUPS) + float(EPG) * gidx
    emask = (lane >= lo) & (lane < lo + float(EPG))
    v1, i1 = first_max(lg, emask)
    v2, i2 = first_max(lg, emask & (lane != i1))
    ex = jnp.exp(v2 - v1)
    w1 = g_val * (1.0 / (1.0 + ex))
    w2 = g_val * (ex / (1.0 + ex))
    e1 = i1 - lo
    e2 = i2 - lo
    first_low = e1 < e2
    ea = jnp.where(first_low, e1, e2)
    eb = jnp.where(first_low, e2, e1)
    ga = jnp.where(first_low, w1, w2)
    gb = jnp.where(first_low, w2, w1)
    pidx = ea * float(EPG - 1) - ea * (ea - 1.0) * 0.5 + (eb - ea - 1.0)
    bucket = gidx * float(N_PAIRS) + pidx

    lane_i = lax.broadcasted_iota(i32, (tm, LANES), 1)
    h2_ref[:, D:] = jnp.where(lane_i == 0, ga, jnp.where(lane_i == 1, gb, 0.0))
    bk_t = jnp.transpose(jnp.broadcast_to(bucket, (tm, LANES)))
    bk_ref[...] = bk_t[:8, :]


def _mid(x2, of2, os2, w_out, g1, b1, wq, kx, vx, wxo, g2, b2, wrh, wrl, br, alpha, S, tm=256):
    T = x2.shape[0]
    M = kx.shape[0] // (T // S)
    per_b = S // tm
    row = lambda w: pl.BlockSpec((tm, w), lambda i: (i, 0))
    full = lambda a: pl.BlockSpec(a.shape, lambda i: (0,) * a.ndim)
    kvspec = pl.BlockSpec((M, D), lambda i: (i // per_b, 0))
    kernel = functools.partial(_mid_kernel, alpha=alpha)
    return pl.pallas_call(
        kernel,
        grid=(T // tm,),
        in_specs=[row(D), row(512), row(512), full(w_out), full(g1), full(b1), full(wq),
                  kvspec, kvspec, full(wxo), full(g2), full(b2), full(wrh), full(wrl), full(br)],
        out_specs=[row(XW), pl.BlockSpec((8, tm), lambda i: (0, i))],
        out_shape=[jax.ShapeDtypeStruct((T, XW), f32), jax.ShapeDtypeStruct((8, T), f32)],
        scratch_shapes=[pltpu.VMEM((tm, D), bf16)],
        compiler_params=_cparams(("parallel",)),
        name="mid",
    )(x2, of2, os2, w_out, g1, b1, wq, kx, vx, wxo, g2, b2, wrh, wrl, br)


def _rank_kernel(bk_ref, rank_ref, cnt_ref, carry_ref, *, chunk):
    sub = 256

    @pl.when(pl.program_id(0) == 0)
    def _():
        carry_ref[...] = jnp.zeros_like(carry_ref)

    r = lax.broadcasted_iota(i32, (sub, sub), 0)
    c = lax.broadcasted_iota(i32, (sub, sub), 1)
    before = (r < c).astype(bf16)
    bid = lax.broadcasted_iota(i32, (LANES, sub), 0).astype(f32)
    carry = carry_ref[...]
    for j in range(chunk // sub):
        bk = bk_ref[0:1, j * sub:(j + 1) * sub]
        hit = bid == bk
        oh = jnp.where(hit, 1.0, 0.0)
        prior = _dot(oh.astype(bf16), before) + carry
        rank_ref[:, j * sub:(j + 1) * sub] = jnp.sum(jnp.where(hit, prior, 0.0), axis=0, keepdims=True)
        carry = carry + jnp.sum(oh, axis=1, keepdims=True)
    carry_ref[...] = carry
    cnt_ref[...] = carry


def _rank(bk8, chunk=2048):
    T = bk8.shape[1]
    kernel = functools.partial(_rank_kernel, chunk=chunk)
    return pl.pallas_call(
        kernel, grid=(T // chunk,),
        in_specs=[pl.BlockSpec((8, chunk), lambda i: (0, i))],
        out_specs=[pl.BlockSpec((1, chunk), lambda i: (0, i)),
                   pl.BlockSpec((LANES, 1), lambda i: (0, 0))],
        out_shape=[jax.ShapeDtypeStruct((1, T), f32), jax.ShapeDtypeStruct((LANES, 1), f32)],
        scratch_shapes=[pltpu.VMEM((LANES, 1), f32)],
        compiler_params=_cparams(("arbitrary",)), name="rank",
    )(bk8)


def _row_copy(src, dst, sem, s, d):
    return pltpu.make_async_copy(src.at[pl.ds(s, 1)], dst.at[pl.ds(d, 1)], sem)


def _scatter_kernel(bk_ref, rk_ref, ps_ref, src_ref, init_ref, dst_ref, sem, *, chunk):
    del init_ref

    def issue(t, _):
        d = ps_ref[bk_ref[t]] + rk_ref[t]
        _row_copy(src_ref, dst_ref, sem, t, d).start()
        return 0

    lax.fori_loop(0, chunk, issue, 0)
    pltpu.make_async_copy(src_ref, dst_ref.at[pl.ds(0, chunk)], sem).wait()


def _scatter_rows(bk, rk, ps, src, init, chunk=1024):
    T, w = src.shape
    sm = lambda: pl.BlockSpec((chunk,), lambda i: (i,), memory_space=pltpu.SMEM)
    kernel = functools.partial(_scatter_kernel, chunk=chunk)
    return pl.pallas_call(
        kernel, grid=(T // chunk,),
        in_specs=[sm(), sm(), pl.BlockSpec(memory_space=pltpu.SMEM),
                  pl.BlockSpec((chunk, w), lambda i: (i, 0)), pl.BlockSpec(memory_space=pl.ANY)],
        out_specs=pl.BlockSpec(memory_space=pl.ANY),
        out_shape=jax.ShapeDtypeStruct(init.shape, init.dtype),
        scratch_shapes=[pltpu.SemaphoreType.DMA(())],
        input_output_aliases={4: 0},
        compiler_params=pltpu.CompilerParams(dimension_semantics=("arbitrary",), has_side_effects=True,
                                             vmem_limit_bytes=VMEM_LIMIT),
        name="scatter_rows",
    )(bk, rk, ps, src, init)


def _expert_kernel(ea_ref, eb_ref, used_ref, xs_ref, wga_ref, wua_ref, wda_ref,
                   wgb_ref, wub_ref, wdb_ref, y_ref):
    n = pl.program_id(0)

    @pl.when(n < used_ref[0])
    def _():
        x = xs_ref[:, :D].astype(bf16)

        def expert(wg, wu, wd):
            a = _dot(x, wg[0])
            u = _dot(x, wu[0])
            act = a * (1.0 / (1.0 + jnp.exp(-a))) * u
            return _dot(act.astype(bf16), wd[0])

        ga = xs_ref[:, D:D + 1]
        gb = xs_ref[:, D + 1:D + 2]
        y_ref[...] = ga * expert(wga_ref, wua_ref, wda_ref) + gb * expert(wgb_ref, wub_ref, wdb_ref)

    @pl.when(n >= used_ref[0])
    def _():
        y_ref[...] = jnp.zeros_like(y_ref)


def _experts(ea, eb, used, xs, wg, wu, wd):
    P = xs.shape[0]
    nblk = P // ROW_BLOCK

    def xmap(n, ea, eb, used):
        return (jnp.minimum(n, used[0] - 1), 0)

    wa = lambda n, ea, eb, used: (ea[n], 0, 0)
    wb = lambda n, ea, eb, used: (eb[n], 0, 0)
    wgs = lambda m: pl.BlockSpec((1, D, D_EXPERT), m)
    wds = lambda m: pl.BlockSpec((1, D_EXPERT, D), m)
    grid_spec = pltpu.PrefetchScalarGridSpec(
        num_scalar_prefetch=3, grid=(nblk,),
        in_specs=[pl.BlockSpec((ROW_BLOCK, XW), xmap), wgs(wa), wgs(wa), wds(wa), wgs(wb), wgs(wb), wds(wb)],
        out_specs=pl.BlockSpec((ROW_BLOCK, D), lambda n, ea, eb, used: (n, 0)),
    )
    return pl.pallas_call(
        _expert_kernel, grid_spec=grid_spec,
        out_shape=jax.ShapeDtypeStruct((P, D), f32),
        compiler_params=_cparams(("arbitrary",)), name="experts",
    )(ea, eb, used, xs, wg, wu, wd, wg, wu, wd)


def _final_kernel(bk0_ref, rk0_ref, bkn_ref, rkn_ref, ps_ref, h_ref, ys_ref, g_ref, b_ref, o_ref,
                  buf, sems, *, alpha, tm):
    i = pl.program_id(0)
    n = pl.num_programs(0)

    def issue(bk_ref, rk_ref, slot):
        def body(t, _):
            s = ps_ref[bk_ref[t]] + rk_ref[t]
            pltpu.make_async_copy(ys_ref.at[pl.ds(s, 1)], buf.at[slot, pl.ds(t, 1)], sems.at[slot]).start()
            return 0
        lax.fori_loop(0, tm, body, 0)

    @pl.when(i == 0)
    def _():
        issue(bk0_ref, rk0_ref, 0)

    @pl.when(i + 1 < n)
    def _():
        issue(bkn_ref, rkn_ref, (i + 1) % 2)

    slot = i % 2
    pltpu.make_async_copy(ys_ref.at[pl.ds(0, tm)], buf.at[slot], sems.at[slot]).wait()
    o_ref[...] = _layer_norm(alpha * h_ref[...] + buf[slot], g_ref[...], b_ref[...])


def _final(bk, rk, ps, h2x, ys, g, b, alpha, tm=1024):
    T = h2x.shape[0]
    n = T // tm
    row = pl.BlockSpec((tm, D), lambda i: (i, 0))
    vec = pl.BlockSpec((1, D), lambda i: (0, 0))
    first = lambda: pl.BlockSpec((tm,), lambda i: (0,), memory_space=pltpu.SMEM)
    nxt = lambda: pl.BlockSpec((tm,), lambda i: (jnp.minimum(i + 1, n - 1),), memory_space=pltpu.SMEM)
    return pl.pallas_call(
        functools.partial(_final_kernel, alpha=alpha, tm=tm), grid=(n,),
        in_specs=[first(), first(), nxt(), nxt(), pl.BlockSpec(memory_space=pltpu.SMEM),
                  row, pl.BlockSpec(memory_space=pl.ANY), vec, vec],
        out_specs=row,
        out_shape=jax.ShapeDtypeStruct((T, D), f32),
        scratch_shapes=[pltpu.VMEM((2, tm, D), f32), pltpu.SemaphoreType.DMA((2,))],
        compiler_params=_cparams(("arbitrary",)), name="final_ln",
    )(bk, rk, bk, rk, ps, h2x, ys, g, b)


def _pair_tables():
    ea = np.zeros((LANES,), np.int32)
    eb = np.zeros((LANES,), np.int32)
    for g in range(N_GROUPS):
        k = 0
        for a in range(EPG):
            for b in range(a + 1, EPG):
                ea[g * N_PAIRS + k] = g * EPG + a
                eb[g * N_PAIRS + k] = g * EPG + b
                k += 1
    return ea, eb


_PAIR_A, _PAIR_B = _pair_tables()


def _layer(h, mem, positions, w_in, b_forget, sinks, w_mix_out, ln_mix_g, ln_mix_b,
           w_xq, w_xkv, w_xout, ln_x_g, ln_x_b, w_rg, b_rg, w_re, b_re,
           w_eg, w_eu, w_ed, ln_f_g, ln_f_b, alpha):
    B, S, _ = h.shape
    T = B * S
    x2 = h.reshape(T, D)
    pos2 = positions.reshape(T, 1).astype(i32)

    o = np.cumsum((0, FOX_W, FOX_W, FOX_W, N_FOX, SWA_Q_W, SWA_KV_W, SWA_KV_W))
    w_qf, w_kf, w_vf, w_fl, w_qs, w_ks, w_vs = (w_in[:, o[i]:o[i + 1]] for i in range(7))
    perm = np.concatenate([np.concatenate([np.arange(j * HD, (j + 1) * HD),
                                           np.arange((j + 4) * HD, (j + 5) * HD)]) for j in range(4)])
    w_all = jnp.concatenate([w_qf, w_kf, w_vf, w_qs[:, perm], w_ks, w_vs], axis=1).astype(bf16)
    wfl = w_fl.T.astype(bf16)
    bfc = b_forget.reshape(N_FOX, 1).astype(f32)
    half = HD // 2
    inv_freq = ROPE_THETA ** (-jnp.arange(half, dtype=f32) / half)
    invf = jnp.tile(inv_freq, LANES // half).reshape(1, LANES)
    w_out = jnp.concatenate([w_mix_out[:FOX_W], w_mix_out[FOX_W:][perm]], axis=0).astype(bf16)

    qf, kf, vf, qs, ks, vs, lf = _in_proj(x2, pos2, w_all, wfl, bfc, invf)
    c = _cumsum(lf, S)
    tq = 512
    c4 = c.reshape(N_FOX // 2, 2, T // tq, tq)
    r3 = lambda a: a.reshape(B, S, a.shape[-1])
    o_fox = _fox(r3(qf), r3(kf), r3(vf), c4, B, S, tq=tq)
    o_swa = _swa(sinks.astype(f32), r3(qs), r3(ks), r3(vs), B, S)

    kx, vx = _kvproj(mem.reshape(-1, D), w_xkv.astype(bf16))

    wr = jnp.concatenate([w_rg, jnp.transpose(w_re, (1, 0, 2)).reshape(D, N_EXPERTS)], axis=1)
    wr = jnp.pad(wr, ((0, 0), (0, LANES - wr.shape[1]))).astype(f32)
    wrh = wr.astype(bf16)
    wrl = (wr - wrh.astype(f32)).astype(bf16)
    br = jnp.pad(jnp.concatenate([b_rg, b_re.reshape(-1)]), (0, LANES - N_GROUPS - N_EXPERTS))
    br = br.reshape(1, LANES).astype(f32)
    v2 = lambda a: a.reshape(1, D).astype(f32)
    h2x, bk8 = _mid(x2, o_fox.reshape(T, FOX_W), o_swa.reshape(T, SWA_Q_W), w_out,
                    v2(ln_mix_g), v2(ln_mix_b), w_xq.astype(bf16), kx, vx, w_xout.astype(bf16),
                    v2(ln_x_g), v2(ln_x_b), wrh, wrl, br, alpha, S)

    rank, cnt = _rank(bk8)
    bucket_i = bk8[0].astype(i32)
    rank_i = rank[0].astype(i32)
    counts = cnt[:, 0].astype(i32)
    padded = ((counts + ROW_BLOCK - 1) // ROW_BLOCK) * ROW_BLOCK
    pad_end = jnp.cumsum(padded)
    pad_start = (pad_end - padded).astype(i32)
    P = T + N_BUCKETS * ROW_BLOCK
    nblk = P // ROW_BLOCK
    used = (pad_end[-1] // ROW_BLOCK).astype(i32).reshape(1)
    blk_bucket = jnp.clip(jnp.searchsorted(pad_end, jnp.arange(nblk, dtype=i32) * ROW_BLOCK, side='right'),
                          0, N_BUCKETS - 1)
    blk_a = jnp.asarray(_PAIR_A)[blk_bucket]
    blk_b = jnp.asarray(_PAIR_B)[blk_bucket]

    xs = _scatter_rows(bucket_i, rank_i, pad_start, h2x, jnp.zeros((P, XW), f32))
    ys = _experts(blk_a, blk_b, used, xs, w_eg.astype(bf16), w_eu.astype(bf16), w_ed.astype(bf16))
    out = _final(bucket_i, rank_i, pad_start, h2x, ys, v2(ln_f_g), v2(ln_f_b), alpha)
    return out.reshape(B, S, D)


def kernel(x, mem, positions, w_in, b_forget, sinks, w_mix_out, ln_mix_g, ln_mix_b, w_xq, w_xkv, w_xout,
           ln_x_g, ln_x_b, w_route_group, b_route_group, w_route_expert, b_route_expert,
           w_exp_gate, w_exp_up, w_exp_down, ln_ffn_g, ln_ffn_b):
    depth = w_in.shape[0]
    alpha = (2.0 * depth) ** 0.25
    h = x
    for l in range(depth):
        h = _layer(h, mem, positions, w_in[l], b_forget[l], sinks[l], w_mix_out[l], ln_mix_g[l], ln_mix_b[l],
                   w_xq[l], w_xkv[l], w_xout[l], ln_x_g[l], ln_x_b[l], w_route_group[l], b_route_group[l],
                   w_route_expert[l], b_route_expert[l], w_exp_gate[l], w_exp_up[l], w_exp_down[l],
                   ln_ffn_g[l], ln_ffn_b[l], alpha)
    return h
```

```python
import functools

import jax
import jax.numpy as jnp
import numpy as np
from jax import lax
from jax.experimental import pallas as pl
from jax.experimental.pallas import tpu as pltpu

f32 = jnp.float32
bf16 = jnp.bfloat16
i32 = jnp.int32

D = 1024
HD = 64
N_FOX = 8
N_SWA = 8
N_SWA_KV = 2
FOX_W = 512
SWA_Q_W = 512
SWA_KV_W = 128
WINDOW = 128
ROPE_THETA = 10000.0
N_XH = 4
XHD = 256
N_GROUPS = 4
EPG = 8
N_EXPERTS = 32
D_EXPERT = 512
LN_EPS = 1e-5
NEG = -1e30

LANES = 128
ROW_BLOCK = 128
N_PAIRS = EPG * (EPG - 1) // 2
N_BUCKETS = N_GROUPS * N_PAIRS
XW = D + LANES
VMEM_LIMIT = 56 * 1024 * 1024


def _cparams(sem):
    return pltpu.CompilerParams(dimension_semantics=sem, vmem_limit_bytes=VMEM_LIMIT)


def _layer_norm(v, g, b):
    mu = jnp.mean(v, axis=-1, keepdims=True)
    c = v - mu
    var = jnp.mean(c * c, axis=-1, keepdims=True)
    return c * lax.rsqrt(var + LN_EPS) * g + b


def _dot(a, b):
    return jnp.dot(a, b, preferred_element_type=f32)


def _dot_nt(a, b):
    return lax.dot_general(a, b, (((1,), (1,)), ((), ())), preferred_element_type=f32)


def _inproj_kernel(x_ref, pos_ref, w_ref, wqt_ref, wvt_ref, wfl_ref, bf_ref, invf_ref,
                   qt_ref, kf_ref, vt_ref, qs_ref, ks_ref, vs_ref, lf_ref):
    tm = x_ref.shape[0]
    xb = x_ref[...].astype(bf16)

    def proj(lo, hi):
        return _dot(xb, w_ref[:, lo:hi])

    qt_ref[0] = (_dot_nt(wqt_ref[...], xb) * 0.125).astype(bf16)
    vt_ref[0] = _dot_nt(wvt_ref[...], xb).astype(bf16)
    kf_ref[...] = proj(0, 512).astype(bf16)

    ang = pos_ref[...].astype(f32) * invf_ref[...]
    cos = jnp.cos(ang)
    sin = jnp.sin(ang)
    lane = lax.broadcasted_iota(i32, (tm, LANES), 1)
    lo_half = (lane % HD) < (HD // 2)
    sin_s = jnp.where(lo_half, -sin, sin)

    def rope(z):
        rot = jnp.where(lo_half, pltpu.roll(z, LANES - HD // 2, 1), pltpu.roll(z, HD // 2, 1))
        return z * cos + rot * sin_s

    zq = proj(512, 1024)
    for g in range(4):
        sl = slice(g * LANES, (g + 1) * LANES)
        qs_ref[:, sl] = (rope(zq[:, sl]) * 0.125).astype(bf16)
    ks_ref[...] = rope(proj(1024, 1152)).astype(bf16)
    vs_ref[...] = proj(1152, 1280).astype(bf16)

    z = _dot_nt(wfl_ref[...], xb) + bf_ref[...]
    lf_ref[...] = jnp.minimum(z, 0.0) - jnp.log(1.0 + jnp.exp(-jnp.abs(z)))


def _in_proj(x2, pos2, w_all, wqt, wvt, wfl, bfc, invf, tm):
    T = x2.shape[0]
    row = lambda w: pl.BlockSpec((tm, w), lambda i: (i, 0))
    full = lambda a: pl.BlockSpec(a.shape, lambda i: (0,) * a.ndim)
    fmaj = pl.BlockSpec((1, FOX_W, tm), lambda i: (i, 0, 0))
    return pl.pallas_call(
        _inproj_kernel,
        grid=(T // tm,),
        in_specs=[row(D), row(1), full(w_all), full(wqt), full(wvt), full(wfl), full(bfc), full(invf)],
        out_specs=[fmaj, row(512), fmaj, row(512), row(128), row(128),
                   pl.BlockSpec((N_FOX, tm), lambda i: (0, i))],
        out_shape=[jax.ShapeDtypeStruct((T // tm, FOX_W, tm), bf16), jax.ShapeDtypeStruct((T, 512), bf16),
                   jax.ShapeDtypeStruct((T // tm, FOX_W, tm), bf16), jax.ShapeDtypeStruct((T, 512), bf16),
                   jax.ShapeDtypeStruct((T, 128), bf16), jax.ShapeDtypeStruct((T, 128), bf16),
                   jax.ShapeDtypeStruct((N_FOX, T), f32)],
        compiler_params=_cparams(("parallel",)),
        name="in_proj",
    )(x2, pos2, w_all, wqt, wvt, wfl, bfc, invf)


def _cumsum_kernel(lf_ref, c_ref, ca_ref):
    S = lf_ref.shape[1]
    ch = 256
    r = lax.broadcasted_iota(i32, (ch, ch), 0)
    c = lax.broadcasted_iota(i32, (ch, ch), 1)
    tri = (r <= c).astype(f32)
    eye = (r == c).astype(bf16)
    carry = jnp.zeros((N_FOX, 1), f32)
    for j in range(S // ch):
        a = lf_ref[:, j * ch:(j + 1) * ch]
        cc = jnp.dot(a, tri, precision=lax.Precision.HIGHEST, preferred_element_type=f32) + carry
        c_ref[:, j * ch:(j + 1) * ch] = cc
        carry = cc[:, ch - 1:ch]
        neg = -cc
        hi = neg.astype(bf16)
        r1 = neg - hi.astype(f32)
        mid = r1.astype(bf16)
        lo = (r1 - mid.astype(f32)).astype(bf16)
        terms = jnp.concatenate([hi, mid, lo, jnp.zeros((LANES - 3 * N_FOX, ch), bf16)], axis=0)
        ca_ref[j * ch:(j + 1) * ch, :] = _dot_nt(eye, terms).astype(bf16)


def _cumsum(lf, S):
    T = lf.shape[1]
    spec = pl.BlockSpec((N_FOX, S), lambda b: (0, b))
    return pl.pallas_call(
        _cumsum_kernel, grid=(T // S,), in_specs=[spec],
        out_specs=[spec, pl.BlockSpec((S, LANES), lambda b: (b, 0))],
        out_shape=[jax.ShapeDtypeStruct((N_FOX, T), f32), jax.ShapeDtypeStruct((T, LANES), bf16)],
        compiler_params=_cparams(("parallel",)), name="cumsum",
    )(lf)


def _fox_kernel(qt_ref, k_ref, ca_ref, vt_ref, c_ref, o_ref, t0_ref, t1_ref, *, tq):
    hp = pl.program_id(1)
    i = pl.program_id(2)
    qt = qt_ref[0]
    row = lax.broadcasted_iota(i32, (LANES, tq), 0)
    is_a = row < HD
    zero = jnp.zeros_like(qt)
    q_ops = []
    for h in range(2):
        ones = jnp.where(((row & 7) == 2 * hp + h) & (row < 3 * N_FOX), 1.0, 0.0).astype(bf16)
        qh = jnp.where(is_a, qt, zero) if h == 0 else jnp.where(is_a, zero, qt)
        q_ops.append(jnp.concatenate([qh, ones], axis=0))
    kr = lax.broadcasted_iota(i32, (tq, tq), 0)
    qc = lax.broadcasted_iota(i32, (tq, tq), 1)
    causal = kr <= qc
    cq = [c_ref[0, h, pl.ds(i, 1), :] for h in range(2)]

    def scores(j, t_ref):
        off = pl.multiple_of(j * tq, tq)
        kblk = jnp.concatenate([k_ref[0, pl.ds(off, tq), :], ca_ref[0, pl.ds(off, tq), :]], axis=1)
        for h in range(2):
            t_ref[h] = _dot(kblk, q_ops[h])

    def softmax_pv(j, t_ref, carry, masked):
        vt = vt_ref[j]
        new = []
        for h in range(2):
            m, l, acc = carry[h]
            t = t_ref[h]
            if masked:
                t = jnp.where(causal, t, NEG)
            m_new = jnp.maximum(m, jnp.max(t, axis=0, keepdims=True) + cq[h])
            alpha = jnp.exp(m - m_new)
            p = jnp.exp(t + (cq[h] - m_new))
            l = alpha * l + jnp.sum(p, axis=0, keepdims=True)
            acc = alpha * acc + _dot(vt, p.astype(bf16))
            new.append((m_new, l, acc))
        return tuple(new)

    def pair(k, carry):
        j = 2 * k
        scores(j + 1, t1_ref)
        carry = softmax_pv(j, t0_ref, carry, False)
        scores(j + 2, t0_ref)
        return softmax_pv(j + 1, t1_ref, carry, False)

    def odd_tail(carry):
        scores(i, t1_ref)
        carry = softmax_pv(i - 1, t0_ref, carry, False)
        return softmax_pv(i, t1_ref, carry, True)

    def even_tail(carry):
        return softmax_pv(i, t0_ref, carry, True)

    init = tuple((jnp.full((1, tq), NEG, f32), jnp.zeros((1, tq), f32), jnp.zeros((LANES, tq), f32))
                 for _ in range(2))
    scores(0, t0_ref)
    carry = lax.fori_loop(0, i // 2, pair, init)
    (_, la, acca), (_, lb, accb) = lax.cond(i % 2 == 1, odd_tail, even_tail, carry)
    ot = jnp.where(is_a, acca / la, accb / lb)
    o_ref[0] = jnp.transpose(ot).astype(bf16)


def _fox(qt, kf, ca, vt, c4, B, S, tq):
    nq = S // tq
    kernel = functools.partial(_fox_kernel, tq=tq)
    return pl.pallas_call(
        kernel,
        grid=(B, N_FOX // 2, nq),
        in_specs=[
            pl.BlockSpec((1, LANES, tq), lambda b, hp, i: (b * nq + i, hp, 0)),
            pl.BlockSpec((1, S, LANES), lambda b, hp, i: (b, 0, hp)),
            pl.BlockSpec((1, S, LANES), lambda b, hp, i: (b, 0, 0)),
            pl.BlockSpec((nq, LANES, tq), lambda b, hp, i: (b, hp, 0)),
            pl.BlockSpec((1, 2, nq, tq), lambda b, hp, i: (hp, 0, b, 0)),
        ],
        out_specs=pl.BlockSpec((1, tq, LANES), lambda b, hp, i: (b, i, hp)),
        out_shape=jax.ShapeDtypeStruct((B, S, FOX_W), bf16),
        scratch_shapes=[pltpu.VMEM((2, tq, tq), f32), pltpu.VMEM((2, tq, tq), f32)],
        compiler_params=_cparams(("parallel", "parallel", "arbitrary")),
        name="fox",
    )(qt, kf, ca, vt, c4)


def _swa_kernel(sink_ref, q_ref, k_ref, v_ref, o_ref, *, tq):
    W = WINDOW
    nsub = tq // W
    n0 = pl.program_id(1) * nsub
    lane = lax.broadcasted_iota(i32, (W, LANES), 1)
    is0 = lane < HD
    rows = lax.broadcasted_iota(i32, (4 * W, 2 * W), 0)
    cols = lax.broadcasted_iota(i32, (4 * W, 2 * W), 1)
    rgrp = lax.broadcasted_iota(i32, (4 * W, 1), 0) // W
    for r in range(nsub):
        nb = n0 + r
        kstart = pl.multiple_of(jnp.maximum(nb * W - W, 0), W)
        ks = k_ref[0, pl.ds(kstart, 2 * W), :]
        vs = v_ref[0, pl.ds(kstart, 2 * W), :]
        qpos = nb * W + rows % W
        kpos = kstart + cols
        valid = (kpos <= qpos) & (qpos - kpos < W)
        outs = []
        for kv in range(2):
            keep = is0 if kv == 0 else jnp.logical_not(is0)
            parts = []
            for g in range(4):
                qg = q_ref[0, r * W:(r + 1) * W, g * LANES:(g + 1) * LANES]
                parts.append(jnp.where(keep, qg, jnp.zeros_like(qg)))
            qstack = jnp.concatenate(parts, axis=0)
            s = jnp.where(valid, _dot_nt(qstack, ks), NEG)
            sink = jnp.zeros((4 * W, 1), f32)
            for g in range(4):
                sink = jnp.where(rgrp == g, sink_ref[kv * 4 + g], sink)
            m = jnp.maximum(jnp.max(s, axis=1, keepdims=True), sink)
            e = jnp.exp(s - m)
            den = jnp.sum(e, axis=1, keepdims=True) + jnp.exp(sink - m)
            outs.append(_dot(e.astype(bf16), vs) / den)
        for g in range(4):
            og = jnp.where(is0, outs[0][g * W:(g + 1) * W], outs[1][g * W:(g + 1) * W])
            o_ref[0, r * W:(r + 1) * W, g * LANES:(g + 1) * LANES] = og.astype(bf16)


def _swa(sinks, qs, ks, vs, B, S, tq=512):
    kernel = functools.partial(_swa_kernel, tq=tq)
    return pl.pallas_call(
        kernel,
        grid=(B, S // tq),
        in_specs=[
            pl.BlockSpec(memory_space=pltpu.SMEM),
            pl.BlockSpec((1, tq, SWA_Q_W), lambda b, i: (b, i, 0)),
            pl.BlockSpec((1, S, SWA_KV_W), lambda b, i: (b, 0, 0)),
            pl.BlockSpec((1, S, SWA_KV_W), lambda b, i: (b, 0, 0)),
        ],
        out_specs=pl.BlockSpec((1, tq, SWA_Q_W), lambda b, i: (b, i, 0)),
        out_shape=jax.ShapeDtypeStruct((B, S, SWA_Q_W), bf16),
        compiler_params=_cparams(("parallel", "arbitrary")),
        name="swa",
    )(sinks, qs, ks, vs)


def _kvproj_kernel(m_ref, w_ref, k_ref, v_ref):
    mb = m_ref[...].astype(bf16)
    k_ref[...] = _dot(mb, w_ref[:, :D]).astype(bf16)
    v_ref[...] = _dot(mb, w_ref[:, D:]).astype(bf16)


def _kvproj(mem2, w_xkv, tm=512):
    R = mem2.shape[0]
    row = pl.BlockSpec((tm, D), lambda i: (i, 0))
    return pl.pallas_call(
        _kvproj_kernel, grid=(R // tm,),
        in_specs=[row, pl.BlockSpec(w_xkv.shape, lambda i: (0, 0))],
        out_specs=[row, row],
        out_shape=[jax.ShapeDtypeStruct((R, D), bf16)] * 2,
        compiler_params=_cparams(("parallel",)), name="kvproj",
    )(mem2, w_xkv)


def _mid_kernel(x_ref, of_ref, os_ref, wo_ref, g1_ref, b1_ref, wq_ref, k_ref, v_ref,
                wxo_ref, g2_ref, b2_ref, wrh_ref, wrl_ref, br_ref,
                h2_ref, bk_ref, oc_ref, *, alpha):
    tm = x_ref.shape[0]
    mix = _dot(of_ref[...], wo_ref[:FOX_W, :]) + _dot(os_ref[...], wo_ref[FOX_W:, :])
    h1 = _layer_norm(alpha * x_ref[...] + mix, g1_ref[...], b1_ref[...])

    q = (_dot(h1.astype(bf16), wq_ref[...]) * 0.0625).astype(bf16)
    for h in range(N_XH):
        sl = slice(h * XHD, (h + 1) * XHD)
        s = _dot_nt(q[:, sl], k_ref[:, sl])
        e = jnp.exp(s - jnp.max(s, axis=1, keepdims=True))
        p = e / jnp.sum(e, axis=1, keepdims=True)
        oc_ref[:, sl] = _dot(p.astype(bf16), v_ref[:, sl]).astype(bf16)
    xo = _dot(oc_ref[...], wxo_ref[...])
    h2 = _layer_norm(alpha * h1 + xo, g2_ref[...], b2_ref[...])
    h2_ref[:, :D] = h2

    hh = h2.astype(bf16)
    hl = (h2 - hh.astype(f32)).astype(bf16)
    lg = _dot(hh, wrh_ref[...]) + _dot(hl, wrh_ref[...]) + _dot(hh, wrl_ref[...]) + br_ref[...]

    lane = lax.broadcasted_iota(i32, (tm, LANES), 1).astype(f32)
    big = float(LANES)

    def first_max(vals, mask):
        vm = jnp.where(mask, vals, NEG)
        top = jnp.max(vm, axis=1, keepdims=True)
        idx = jnp.min(jnp.where(mask & (vm == top), lane, big), axis=1, keepdims=True)
        return top, idx

    gmask = lane < float(N_GROUPS)
    gmax, gidx = first_max(lg, gmask)
    g_val = 1.0 / jnp.sum(jnp.where(gmask, jnp.exp(lg - gmax), 0.0), axis=1, keepdims=True)
    lo = float(N_GROUPS) + float(EPG) * gidx
    emask = (lane >= lo) & (lane < lo + float(EPG))
    v1, i1 = first_max(lg, emask)
    v2, i2 = first_max(lg, emask & (lane != i1))
    ex = jnp.exp(v2 - v1)
    w1 = g_val * (1.0 / (1.0 + ex))
    w2 = g_val * (ex / (1.0 + ex))
    e1 = i1 - lo
    e2 = i2 - lo
    first_low = e1 < e2
    ea = jnp.where(first_low, e1, e2)
    eb = jnp.where(first_low, e2, e1)
    ga = jnp.where(first_low, w1, w2)
    gb = jnp.where(first_low, w2, w1)
    pidx = ea * float(EPG - 1) - ea * (ea - 1.0) * 0.5 + (eb - ea - 1.0)
    bucket = gidx * float(N_PAIRS) + pidx

    lane_i = lax.broadcasted_iota(i32, (tm, LANES), 1)
    h2_ref[:, D:] = jnp.where(lane_i == 0, ga, jnp.where(lane_i == 1, gb, 0.0))
    bk_t = jnp.transpose(jnp.broadcast_to(bucket, (tm, LANES)))
    bk_ref[...] = bk_t[:8, :]


def _mid(x2, of2, os2, w_out, g1, b1, wq, kx, vx, wxo, g2, b2, wrh, wrl, br, alpha, S, tm=256):
    T = x2.shape[0]
    M = kx.shape[0] // (T // S)
    per_b = S // tm
    row = lambda w: pl.BlockSpec((tm, w), lambda i: (i, 0))
    full = lambda a: pl.BlockSpec(a.shape, lambda i: (0,) * a.ndim)
    kvspec = pl.BlockSpec((M, D), lambda i: (i // per_b, 0))
    kernel = functools.partial(_mid_kernel, alpha=alpha)
    return pl.pallas_call(
        kernel,
        grid=(T // tm,),
        in_specs=[row(D), row(512), row(512), full(w_out), full(g1), full(b1), full(wq),
                  kvspec, kvspec, full(wxo), full(g2), full(b2), full(wrh), full(wrl), full(br)],
        out_specs=[row(XW), pl.BlockSpec((8, tm), lambda i: (0, i))],
        out_shape=[jax.ShapeDtypeStruct((T, XW), f32), jax.ShapeDtypeStruct((8, T), f32)],
        scratch_shapes=[pltpu.VMEM((tm, D), bf16)],
        compiler_params=_cparams(("parallel",)),
        name="mid",
    )(x2, of2, os2, w_out, g1, b1, wq, kx, vx, wxo, g2, b2, wrh, wrl, br)


def _rank_kernel(bk_ref, rank_ref, cnt_ref, carry_ref, *, chunk):
    sub = 256

    @pl.when(pl.program_id(0) == 0)
    def _():
        carry_ref[...] = jnp.zeros_like(carry_ref)

    r = lax.broadcasted_iota(i32, (sub, sub), 0)
    c = lax.broadcasted_iota(i32, (sub, sub), 1)
    before = (r < c).astype(bf16)
    bid = lax.broadcasted_iota(i32, (LANES, sub), 0).astype(f32)
    carry = carry_ref[...]
    for j in range(chunk // sub):
        bk = bk_ref[0:1, j * sub:(j + 1) * sub]
        hit = bid == bk
        oh = jnp.where(hit, 1.0, 0.0)
        prior = _dot(oh.astype(bf16), before) + carry
        rank_ref[:, j * sub:(j + 1) * sub] = jnp.sum(jnp.where(hit, prior, 0.0), axis=0, keepdims=True)
        carry = carry + jnp.sum(oh, axis=1, keepdims=True)
    carry_ref[...] = carry
    cnt_ref[...] = carry


def _rank(bk8, chunk=2048):
    T = bk8.shape[1]
    kernel = functools.partial(_rank_kernel, chunk=chunk)
    return pl.pallas_call(
        kernel, grid=(T // chunk,),
        in_specs=[pl.BlockSpec((8, chunk), lambda i: (0, i))],
        out_specs=[pl.BlockSpec((1, chunk), lambda i: (0, i)),
                   pl.BlockSpec((LANES, 1), lambda i: (0, 0))],
        out_shape=[jax.ShapeDtypeStruct((1, T), f32), jax.ShapeDtypeStruct((LANES, 1), f32)],
        scratch_shapes=[pltpu.VMEM((LANES, 1), f32)],
        compiler_params=_cparams(("arbitrary",)), name="rank",
    )(bk8)


def _row_copy(src, dst, sem, s, d):
    return pltpu.make_async_copy(src.at[pl.ds(s, 1)], dst.at[pl.ds(d, 1)], sem)


def _scatter_kernel(bk_ref, rk_ref, ps_ref, src_ref, init_ref, dst_ref, sem, *, chunk):
    del init_ref

    def issue(t, _):
        d = ps_ref[bk_ref[t]] + rk_ref[t]
        _row_copy(src_ref, dst_ref, sem, t, d).start()
        return 0

    lax.fori_loop(0, chunk, issue, 0)
    pltpu.make_async_copy(src_ref, dst_ref.at[pl.ds(0, chunk)], sem).wait()


def _scatter_rows(bk, rk, ps, src, init, chunk=1024):
    T, w = src.shape
    sm = lambda: pl.BlockSpec((chunk,), lambda i: (i,), memory_space=pltpu.SMEM)
    kernel = functools.partial(_scatter_kernel, chunk=chunk)
    return pl.pallas_call(
        kernel, grid=(T // chunk,),
        in_specs=[sm(), sm(), pl.BlockSpec(memory_space=pltpu.SMEM),
                  pl.BlockSpec((chunk, w), lambda i: (i, 0)), pl.BlockSpec(memory_space=pl.ANY)],
        out_specs=pl.BlockSpec(memory_space=pl.ANY),
        out_shape=jax.ShapeDtypeStruct(init.shape, init.dtype),
        scratch_shapes=[pltpu.SemaphoreType.DMA(())],
        input_output_aliases={4: 0},
        compiler_params=pltpu.CompilerParams(dimension_semantics=("arbitrary",), has_side_effects=True,
                                             vmem_limit_bytes=VMEM_LIMIT),
        name="scatter_rows",
    )(bk, rk, ps, src, init)


def _expert_kernel(ea_ref, eb_ref, used_ref, xs_ref, wga_ref, wua_ref, wda_ref,
                   wgb_ref, wub_ref, wdb_ref, y_ref):
    n = pl.program_id(0)

    @pl.when(n < used_ref[0])
    def _():
        x = xs_ref[:, :D].astype(bf16)

        def expert(wg, wu, wd):
            a = _dot(x, wg[0])
            u = _dot(x, wu[0])
            act = a * (1.0 / (1.0 + jnp.exp(-a))) * u
            return _dot(act.astype(bf16), wd[0])

        ga = xs_ref[:, D:D + 1]
        gb = xs_ref[:, D + 1:D + 2]
        y_ref[...] = ga * expert(wga_ref, wua_ref, wda_ref) + gb * expert(wgb_ref, wub_ref, wdb_ref)

    @pl.when(n >= used_ref[0])
    def _():
        y_ref[...] = jnp.zeros_like(y_ref)


def _experts(ea, eb, used, xs, wg, wu, wd):
    P = xs.shape[0]
    nblk = P // ROW_BLOCK

    def xmap(n, ea, eb, used):
        return (jnp.minimum(n, used[0] - 1), 0)

    wa = lambda n, ea, eb, used: (ea[n], 0, 0)
    wb = lambda n, ea, eb, used: (eb[n], 0, 0)
    wgs = lambda m: pl.BlockSpec((1, D, D_EXPERT), m)
    wds = lambda m: pl.BlockSpec((1, D_EXPERT, D), m)
    grid_spec = pltpu.PrefetchScalarGridSpec(
        num_scalar_prefetch=3, grid=(nblk,),
        in_specs=[pl.BlockSpec((ROW_BLOCK, XW), xmap), wgs(wa), wgs(wa), wds(wa), wgs(wb), wgs(wb), wds(wb)],
        out_specs=pl.BlockSpec((ROW_BLOCK, D), lambda n, ea, eb, used: (n, 0)),
    )
    return pl.pallas_call(
        _expert_kernel, grid_spec=grid_spec,
        out_shape=jax.ShapeDtypeStruct((P, D), f32),
        compiler_params=_cparams(("arbitrary",)), name="experts",
    )(ea, eb, used, xs, wg, wu, wd, wg, wu, wd)


def _final_kernel(bk0_ref, rk0_ref, bkn_ref, rkn_ref, ps_ref, h_ref, ys_ref, g_ref, b_ref, o_ref,
                  buf, sems, *, alpha, tm):
    i = pl.program_id(0)
    n = pl.num_programs(0)

    def issue(bk_ref, rk_ref, slot):
        def body(t, _):
            s = ps_ref[bk_ref[t]] + rk_ref[t]
            pltpu.make_async_copy(ys_ref.at[pl.ds(s, 1)], buf.at[slot, pl.ds(t, 1)], sems.at[slot]).start()
            return 0
        lax.fori_loop(0, tm, body, 0)

    @pl.when(i == 0)
    def _():
        issue(bk0_ref, rk0_ref, 0)

    @pl.when(i + 1 < n)
    def _():
        issue(bkn_ref, rkn_ref, (i + 1) % 2)

    slot = i % 2
    pltpu.make_async_copy(ys_ref.at[pl.ds(0, tm)], buf.at[slot], sems.at[slot]).wait()
    o_ref[...] = _layer_norm(alpha * h_ref[...] + buf[slot], g_ref[...], b_ref[...])


def _final(bk, rk, ps, h2x, ys, g, b, alpha, tm=1024):
    T = h2x.shape[0]
    n = T // tm
    row = pl.BlockSpec((tm, D), lambda i: (i, 0))
    vec = pl.BlockSpec((1, D), lambda i: (0, 0))
    first = lambda: pl.BlockSpec((tm,), lambda i: (0,), memory_space=pltpu.SMEM)
    nxt = lambda: pl.BlockSpec((tm,), lambda i: (jnp.minimum(i + 1, n - 1),), memory_space=pltpu.SMEM)
    return pl.pallas_call(
        functools.partial(_final_kernel, alpha=alpha, tm=tm), grid=(n,),
        in_specs=[first(), first(), nxt(), nxt(), pl.BlockSpec(memory_space=pltpu.SMEM),
                  row, pl.BlockSpec(memory_space=pl.ANY), vec, vec],
        out_specs=row,
        out_shape=jax.ShapeDtypeStruct((T, D), f32),
        scratch_shapes=[pltpu.VMEM((2, tm, D), f32), pltpu.SemaphoreType.DMA((2,))],
        compiler_params=_cparams(("arbitrary",)), name="final_ln",
    )(bk, rk, bk, rk, ps, h2x, ys, g, b)


def _pair_tables():
    ea = np.zeros((LANES,), np.int32)
    eb = np.zeros((LANES,), np.int32)
    for g in range(N_GROUPS):
        k = 0
        for a in range(EPG):
            for b in range(a + 1, EPG):
                ea[g * N_PAIRS + k] = g * EPG + a
                eb[g * N_PAIRS + k] = g * EPG + b
                k += 1
    return ea, eb


_PAIR_A, _PAIR_B = _pair_tables()


def _layer(h, mem, positions, w_in, b_forget, sinks, w_mix_out, ln_mix_g, ln_mix_b,
           w_xq, w_xkv, w_xout, ln_x_g, ln_x_b, w_rg, b_rg, w_re, b_re,
           w_eg, w_eu, w_ed, ln_f_g, ln_f_b, alpha):
    B, S, _ = h.shape
    T = B * S
    x2 = h.reshape(T, D)
    pos2 = positions.reshape(T, 1).astype(i32)

    o = np.cumsum((0, FOX_W, FOX_W, FOX_W, N_FOX, SWA_Q_W, SWA_KV_W, SWA_KV_W))
    w_qf, w_kf, w_vf, w_fl, w_qs, w_ks, w_vs = (w_in[:, o[i]:o[i + 1]] for i in range(7))
    perm = np.concatenate([np.concatenate([np.arange(j * HD, (j + 1) * HD),
                                           np.arange((j + 4) * HD, (j + 5) * HD)]) for j in range(4)])
    w_all = jnp.concatenate([w_kf, w_qs[:, perm], w_ks, w_vs], axis=1).astype(bf16)
    wqt = w_qf.T.astype(bf16)
    wvt = w_vf.T.astype(bf16)
    wfl = w_fl.T.astype(bf16)
    bfc = b_forget.reshape(N_FOX, 1).astype(f32)
    half = HD // 2
    inv_freq = ROPE_THETA ** (-jnp.arange(half, dtype=f32) / half)
    invf = jnp.tile(inv_freq, LANES // half).reshape(1, LANES)
    w_out = jnp.concatenate([w_mix_out[:FOX_W], w_mix_out[FOX_W:][perm]], axis=0).astype(bf16)

    tq = 512
    qt, kf, vt, qs, ks, vs, lf = _in_proj(x2, pos2, w_all, wqt, wvt, wfl, bfc, invf, tq)
    c, ca = _cumsum(lf, S)
    c4 = c.reshape(N_FOX // 2, 2, T // tq, tq)
    r3 = lambda a: a.reshape(B, S, a.shape[-1])
    o_fox = _fox(qt, r3(kf), r3(ca), vt, c4, B, S, tq)
    o_swa = _swa(sinks.astype(f32), r3(qs), r3(ks), r3(vs), B, S)

    kx, vx = _kvproj(mem.reshape(-1, D), w_xkv.astype(bf16))

    wr = jnp.concatenate([w_rg, jnp.transpose(w_re, (1, 0, 2)).reshape(D, N_EXPERTS)], axis=1)
    wr = jnp.pad(wr, ((0, 0), (0, LANES - wr.shape[1]))).astype(f32)
    wrh = wr.astype(bf16)
    wrl = (wr - wrh.astype(f32)).astype(bf16)
    br = jnp.pad(jnp.concatenate([b_rg, b_re.reshape(-1)]), (0, LANES - N_GROUPS - N_EXPERTS))
    br = br.reshape(1, LANES).astype(f32)
    v2 = lambda a: a.reshape(1, D).astype(f32)
    h2x, bk8 = _mid(x2, o_fox.reshape(T, FOX_W), o_swa.reshape(T, SWA_Q_W), w_out,
                    v2(ln_mix_g), v2(ln_mix_b), w_xq.astype(bf16), kx, vx, w_xout.astype(bf16),
                    v2(ln_x_g), v2(ln_x_b), wrh, wrl, br, alpha, S)

    rank, cnt = _rank(bk8)
    bucket_i = bk8[0].astype(i32)
    rank_i = rank[0].astype(i32)
    counts = cnt[:, 0].astype(i32)
    padded = ((counts + ROW_BLOCK - 1) // ROW_BLOCK) * ROW_BLOCK
    pad_end = jnp.cumsum(padded)
    pad_start = (pad_end - padded).astype(i32)
    P = T + N_BUCKETS * ROW_BLOCK
    nblk = P // ROW_BLOCK
    used = (pad_end[-1] // ROW_BLOCK).astype(i32).reshape(1)
    blk_bucket = jnp.clip(jnp.searchsorted(pad_end, jnp.arange(nblk, dtype=i32) * ROW_BLOCK, side='right'),
                          0, N_BUCKETS - 1)
    blk_a = jnp.asarray(_PAIR_A)[blk_bucket]
    blk_b = jnp.asarray(_PAIR_B)[blk_bucket]

    xs = _scatter_rows(bucket_i, rank_i, pad_start, h2x, jnp.zeros((P, XW), f32))
    ys = _experts(blk_a, blk_b, used, xs, w_eg.astype(bf16), w_eu.astype(bf16), w_ed.astype(bf16))
    out = _final(bucket_i, rank_i, pad_start, h2x, ys, v2(ln_f_g), v2(ln_f_b), alpha)
    return out.reshape(B, S, D)


def kernel(x, mem, positions, w_in, b_forget, sinks, w_mix_out, ln_mix_g, ln_mix_b, w_xq, w_xkv, w_xout,
           ln_x_g, ln_x_b, w_route_group, b_route_group, w_route_expert, b_route_expert,
           w_exp_gate, w_exp_up, w_exp_down, ln_ffn_g, ln_ffn_b):
    depth = w_in.shape[0]
    alpha = (2.0 * depth) ** 0.25
    h = x
    for l in range(depth):
        h = _layer(h, mem, positions, w_in[l], b_forget[l], sinks[l], w_mix_out[l], ln_mix_g[l], ln_mix_b[l],
                   w_xq[l], w_xkv[l], w_xout[l], ln_x_g[l], ln_x_b[l], w_route_group[l], b_route_group[l],
                   w_route_expert[l], b_route_expert[l], w_exp_gate[l], w_exp_up[l], w_exp_down[l],
                   ln_ffn_g[l], ln_ffn_b[l], alpha)
    return h
```

```python
import functools

import jax
import jax.numpy as jnp
import numpy as np
from jax import lax
from jax.experimental import pallas as pl
from jax.experimental.pallas import tpu as pltpu

f32 = jnp.float32
bf16 = jnp.bfloat16
i32 = jnp.int32

D = 1024
HD = 64
N_FOX = 8
N_SWA = 8
N_SWA_KV = 2
FOX_W = 512
SWA_Q_W = 512
SWA_KV_W = 128
WINDOW = 128
ROPE_THETA = 10000.0
N_XH = 4
XHD = 256
N_GROUPS = 4
EPG = 8
N_EXPERTS = 32
D_EXPERT = 512
LN_EPS = 1e-5
NEG = -1e30

LANES = 128
ROW_BLOCK = 128
N_PAIRS = EPG * (EPG - 1) // 2
N_BUCKETS = N_GROUPS * N_PAIRS
XW = D + LANES
VMEM_LIMIT = 56 * 1024 * 1024


def _cparams(sem):
    return pltpu.CompilerParams(dimension_semantics=sem, vmem_limit_bytes=VMEM_LIMIT)


def _layer_norm(v, g, b):
    mu = jnp.mean(v, axis=-1, keepdims=True)
    c = v - mu
    var = jnp.mean(c * c, axis=-1, keepdims=True)
    return c * lax.rsqrt(var + LN_EPS) * g + b


def _dot(a, b):
    return jnp.dot(a, b, preferred_element_type=f32)


def _dot_nt(a, b):
    return lax.dot_general(a, b, (((1,), (1,)), ((), ())), preferred_element_type=f32)


def _inproj_kernel(x_ref, pos_ref, w_ref, wqt_ref, wvt_ref, wfl_ref, bf_ref, invf_ref,
                   qt_ref, kf_ref, vt_ref, qs_ref, ks_ref, vs_ref, lf_ref):
    tm = x_ref.shape[0]
    xb = x_ref[...].astype(bf16)

    def proj(lo, hi):
        return _dot(xb, w_ref[:, lo:hi])

    qt_ref[0] = (_dot_nt(wqt_ref[...], xb) * 0.125).astype(bf16)
    vt_ref[0] = _dot_nt(wvt_ref[...], xb).astype(bf16)
    kf_ref[...] = proj(0, 512).astype(bf16)

    ang = pos_ref[...].astype(f32) * invf_ref[...]
    cos = jnp.cos(ang)
    sin = jnp.sin(ang)
    lane = lax.broadcasted_iota(i32, (tm, LANES), 1)
    lo_half = (lane % HD) < (HD // 2)
    sin_s = jnp.where(lo_half, -sin, sin)

    def rope(z):
        rot = jnp.where(lo_half, pltpu.roll(z, LANES - HD // 2, 1), pltpu.roll(z, HD // 2, 1))
        return z * cos + rot * sin_s

    zq = proj(512, 1024)
    for g in range(4):
        sl = slice(g * LANES, (g + 1) * LANES)
        qs_ref[:, sl] = (rope(zq[:, sl]) * 0.125).astype(bf16)
    ks_ref[...] = rope(proj(1024, 1152)).astype(bf16)
    vs_ref[...] = proj(1152, 1280).astype(bf16)

    z = _dot_nt(wfl_ref[...], xb) + bf_ref[...]
    lf_ref[...] = jnp.minimum(z, 0.0) - jnp.log(1.0 + jnp.exp(-jnp.abs(z)))


def _in_proj(x2, pos2, w_all, wqt, wvt, wfl, bfc, invf, tm):
    T = x2.shape[0]
    row = lambda w: pl.BlockSpec((tm, w), lambda i: (i, 0))
    full = lambda a: pl.BlockSpec(a.shape, lambda i: (0,) * a.ndim)
    fmaj = pl.BlockSpec((1, FOX_W, tm), lambda i: (i, 0, 0))
    return pl.pallas_call(
        _inproj_kernel,
        grid=(T // tm,),
        in_specs=[row(D), row(1), full(w_all), full(wqt), full(wvt), full(wfl), full(bfc), full(invf)],
        out_specs=[fmaj, row(512), fmaj, row(512), row(128), row(128),
                   pl.BlockSpec((N_FOX, tm), lambda i: (0, i))],
        out_shape=[jax.ShapeDtypeStruct((T // tm, FOX_W, tm), bf16), jax.ShapeDtypeStruct((T, 512), bf16),
                   jax.ShapeDtypeStruct((T // tm, FOX_W, tm), bf16), jax.ShapeDtypeStruct((T, 512), bf16),
                   jax.ShapeDtypeStruct((T, 128), bf16), jax.ShapeDtypeStruct((T, 128), bf16),
                   jax.ShapeDtypeStruct((N_FOX, T), f32)],
        compiler_params=_cparams(("parallel",)),
        name="in_proj",
    )(x2, pos2, w_all, wqt, wvt, wfl, bfc, invf)


def _cumsum_kernel(lf_ref, c_ref, ca_ref):
    S = lf_ref.shape[1]
    ch = 256
    r = lax.broadcasted_iota(i32, (ch, ch), 0)
    c = lax.broadcasted_iota(i32, (ch, ch), 1)
    tri = (r <= c).astype(f32)
    eye = (r == c).astype(bf16)
    carry = jnp.zeros((N_FOX, 1), f32)
    for j in range(S // ch):
        a = lf_ref[:, j * ch:(j + 1) * ch]
        cc = jnp.dot(a, tri, precision=lax.Precision.HIGHEST, preferred_element_type=f32) + carry
        c_ref[:, j * ch:(j + 1) * ch] = cc
        carry = cc[:, ch - 1:ch]
        neg = -cc
        hi = neg.astype(bf16)
        r1 = neg - hi.astype(f32)
        mid = r1.astype(bf16)
        lo = (r1 - mid.astype(f32)).astype(bf16)
        terms = jnp.concatenate([hi, mid, lo, jnp.zeros((LANES - 3 * N_FOX, ch), bf16)], axis=0)
        ca_ref[j * ch:(j + 1) * ch, :] = _dot_nt(eye, terms).astype(bf16)


def _cumsum(lf, S):
    T = lf.shape[1]
    spec = pl.BlockSpec((N_FOX, S), lambda b: (0, b))
    return pl.pallas_call(
        _cumsum_kernel, grid=(T // S,), in_specs=[spec],
        out_specs=[spec, pl.BlockSpec((S, LANES), lambda b: (b, 0))],
        out_shape=[jax.ShapeDtypeStruct((N_FOX, T), f32), jax.ShapeDtypeStruct((T, LANES), bf16)],
        compiler_params=_cparams(("parallel",)), name="cumsum",
    )(lf)


def _fox_kernel(qt_ref, k_ref, ca_ref, vt_ref, c_ref, o_ref, t0_ref, t1_ref, *, tq):
    hp = pl.program_id(1)
    i = pl.program_id(2)
    qt = qt_ref[0]
    row = lax.broadcasted_iota(i32, (LANES, tq), 0)
    is_a = row < HD
    zero = jnp.zeros_like(qt)
    q_ops = []
    for h in range(2):
        ones = jnp.where(((row & 7) == 2 * hp + h) & (row < 3 * N_FOX), 1.0, 0.0).astype(bf16)
        qh = jnp.where(is_a, qt, zero) if h == 0 else jnp.where(is_a, zero, qt)
        q_ops.append(jnp.concatenate([qh, ones], axis=0))
    kr = lax.broadcasted_iota(i32, (tq, tq), 0)
    qc = lax.broadcasted_iota(i32, (tq, tq), 1)
    causal = kr <= qc
    cq = [c_ref[0, h, pl.ds(i, 1), :] for h in range(2)]

    def scores(j, t_ref):
        off = pl.multiple_of(j * tq, tq)
        kblk = jnp.concatenate([k_ref[0, pl.ds(off, tq), :], ca_ref[0, pl.ds(off, tq), :]], axis=1)
        for h in range(2):
            t_ref[h] = _dot(kblk, q_ops[h])

    def softmax_pv(j, t_ref, carry, masked):
        vt = vt_ref[j]
        new = []
        for h in range(2):
            m, l, acc = carry[h]
            t = t_ref[h]
            if masked:
                t = jnp.where(causal, t, NEG)
            m_new = jnp.maximum(m, jnp.max(t, axis=0, keepdims=True) + cq[h])
            alpha = jnp.exp(m - m_new)
            p = jnp.exp(t + (cq[h] - m_new))
            l = alpha * l + jnp.sum(p, axis=0, keepdims=True)
            acc = alpha * acc + _dot(vt, p.astype(bf16))
            new.append((m_new, l, acc))
        return tuple(new)

    def pair(k, carry):
        j = 2 * k
        scores(j + 1, t1_ref)
        carry = softmax_pv(j, t0_ref, carry, False)
        scores(j + 2, t0_ref)
        return softmax_pv(j + 1, t1_ref, carry, False)

    def odd_tail(carry):
        scores(i, t1_ref)
        carry = softmax_pv(i - 1, t0_ref, carry, False)
        return softmax_pv(i, t1_ref, carry, True)

    def even_tail(carry):
        return softmax_pv(i, t0_ref, carry, True)

    init = tuple((jnp.full((1, tq), NEG, f32), jnp.zeros((1, tq), f32), jnp.zeros((LANES, tq), f32))
                 for _ in range(2))
    scores(0, t0_ref)
    carry = lax.fori_loop(0, i // 2, pair, init)
    (_, la, acca), (_, lb, accb) = lax.cond(i % 2 == 1, odd_tail, even_tail, carry)
    ot = jnp.where(is_a, acca / la, accb / lb)
    o_ref[0] = jnp.transpose(ot).astype(bf16)


def _fox(qt, kf, ca, vt, c4, B, S, tq):
    nq = S // tq
    kernel = functools.partial(_fox_kernel, tq=tq)
    return pl.pallas_call(
        kernel,
        grid=(B, N_FOX // 2, nq),
        in_specs=[
            pl.BlockSpec((1, LANES, tq), lambda b, hp, i: (b * nq + i, hp, 0)),
            pl.BlockSpec((1, S, LANES), lambda b, hp, i: (b, 0, hp)),
            pl.BlockSpec((1, S, LANES), lambda b, hp, i: (b, 0, 0)),
            pl.BlockSpec((nq, LANES, tq), lambda b, hp, i: (b, hp, 0)),
            pl.BlockSpec((1, 2, nq, tq), lambda b, hp, i: (hp, 0, b, 0)),
        ],
        out_specs=pl.BlockSpec((1, tq, LANES), lambda b, hp, i: (b, i, hp)),
        out_shape=jax.ShapeDtypeStruct((B, S, FOX_W), bf16),
        scratch_shapes=[pltpu.VMEM((2, tq, tq), f32), pltpu.VMEM((2, tq, tq), f32)],
        compiler_params=_cparams(("parallel", "parallel", "arbitrary")),
        name="fox",
    )(qt, kf, ca, vt, c4)


def _swa_kernel(sink_ref, q_ref, k_ref, v_ref, o_ref, *, tq):
    W = WINDOW
    nsub = tq // W
    n0 = pl.program_id(1) * nsub
    lane = lax.broadcasted_iota(i32, (W, LANES), 1)
    is0 = lane < HD
    rows = lax.broadcasted_iota(i32, (4 * W, 2 * W), 0)
    cols = lax.broadcasted_iota(i32, (4 * W, 2 * W), 1)
    rgrp = lax.broadcasted_iota(i32, (4 * W, 1), 0) // W
    for r in range(nsub):
        nb = n0 + r
        kstart = pl.multiple_of(jnp.maximum(nb * W - W, 0), W)
        ks = k_ref[0, pl.ds(kstart, 2 * W), :]
        vs = v_ref[0, pl.ds(kstart, 2 * W), :]
        qpos = nb * W + rows % W
        kpos = kstart + cols
        valid = (kpos <= qpos) & (qpos - kpos < W)
        outs = []
        for kv in range(2):
            keep = is0 if kv == 0 else jnp.logical_not(is0)
            parts = []
            for g in range(4):
                qg = q_ref[0, r * W:(r + 1) * W, g * LANES:(g + 1) * LANES]
                parts.append(jnp.where(keep, qg, jnp.zeros_like(qg)))
            qstack = jnp.concatenate(parts, axis=0)
            s = jnp.where(valid, _dot_nt(qstack, ks), NEG)
            sink = jnp.zeros((4 * W, 1), f32)
            for g in range(4):
                sink = jnp.where(rgrp == g, sink_ref[kv * 4 + g], sink)
            m = jnp.maximum(jnp.max(s, axis=1, keepdims=True), sink)
            e = jnp.exp(s - m)
            den = jnp.sum(e, axis=1, keepdims=True) + jnp.exp(sink - m)
            outs.append(_dot(e.astype(bf16), vs) / den)
        for g in range(4):
            og = jnp.where(is0, outs[0][g * W:(g + 1) * W], outs[1][g * W:(g + 1) * W])
            o_ref[0, r * W:(r + 1) * W, g * LANES:(g + 1) * LANES] = og.astype(bf16)


def _swa(sinks, qs, ks, vs, B, S, tq=512):
    kernel = functools.partial(_swa_kernel, tq=tq)
    return pl.pallas_call(
        kernel,
        grid=(B, S // tq),
        in_specs=[
            pl.BlockSpec(memory_space=pltpu.SMEM),
            pl.BlockSpec((1, tq, SWA_Q_W), lambda b, i: (b, i, 0)),
            pl.BlockSpec((1, S, SWA_KV_W), lambda b, i: (b, 0, 0)),
            pl.BlockSpec((1, S, SWA_KV_W), lambda b, i: (b, 0, 0)),
        ],
        out_specs=pl.BlockSpec((1, tq, SWA_Q_W), lambda b, i: (b, i, 0)),
        out_shape=jax.ShapeDtypeStruct((B, S, SWA_Q_W), bf16),
        compiler_params=_cparams(("parallel", "arbitrary")),
        name="swa",
    )(sinks, qs, ks, vs)


def _kvproj_kernel(m_ref, w_ref, k_ref, v_ref):
    mb = m_ref[...].astype(bf16)
    k_ref[...] = _dot(mb, w_ref[:, :D]).astype(bf16)
    v_ref[...] = _dot(mb, w_ref[:, D:]).astype(bf16)


def _kvproj(mem2, w_xkv, tm=512):
    R = mem2.shape[0]
    row = pl.BlockSpec((tm, D), lambda i: (i, 0))
    return pl.pallas_call(
        _kvproj_kernel, grid=(R // tm,),
        in_specs=[row, pl.BlockSpec(w_xkv.shape, lambda i: (0, 0))],
        out_specs=[row, row],
        out_shape=[jax.ShapeDtypeStruct((R, D), bf16)] * 2,
        compiler_params=_cparams(("parallel",)), name="kvproj",
    )(mem2, w_xkv)


def _mid_kernel(x_ref, of_ref, os_ref, wo_ref, g1_ref, b1_ref, wq_ref, k_ref, v_ref,
                wxo_ref, g2_ref, b2_ref, wrh_ref, wrl_ref, br_ref,
                h2_ref, bk_ref, oc_ref, *, alpha):
    tm = x_ref.shape[0]
    mix = _dot(of_ref[...], wo_ref[:FOX_W, :]) + _dot(os_ref[...], wo_ref[FOX_W:, :])
    h1 = _layer_norm(alpha * x_ref[...] + mix, g1_ref[...], b1_ref[...])

    q = (_dot(h1.astype(bf16), wq_ref[...]) * 0.0625).astype(bf16)
    for h in range(N_XH):
        sl = slice(h * XHD, (h + 1) * XHD)
        s = _dot_nt(q[:, sl], k_ref[:, sl])
        e = jnp.exp(s - jnp.max(s, axis=1, keepdims=True))
        p = e / jnp.sum(e, axis=1, keepdims=True)
        oc_ref[:, sl] = _dot(p.astype(bf16), v_ref[:, sl]).astype(bf16)
    xo = _dot(oc_ref[...], wxo_ref[...])
    h2 = _layer_norm(alpha * h1 + xo, g2_ref[...], b2_ref[...])
    h2_ref[:, :D] = h2

    hh = h2.astype(bf16)
    hl = (h2 - hh.astype(f32)).astype(bf16)
    lg = _dot(hh, wrh_ref[...]) + _dot(hl, wrh_ref[...]) + _dot(hh, wrl_ref[...]) + br_ref[...]

    lane = lax.broadcasted_iota(i32, (tm, LANES), 1).astype(f32)
    big = float(LANES)

    def first_max(vals, mask):
        vm = jnp.where(mask, vals, NEG)
        top = jnp.max(vm, axis=1, keepdims=True)
        idx = jnp.min(jnp.where(mask & (vm == top), lane, big), axis=1, keepdims=True)
        return top, idx

    gmask = lane < float(N_GROUPS)
    gmax, gidx = first_max(lg, gmask)
    g_val = 1.0 / jnp.sum(jnp.where(gmask, jnp.exp(lg - gmax), 0.0), axis=1, keepdims=True)
    lo = float(N_GROUPS) + float(EPG) * gidx
    emask = (lane >= lo) & (lane < lo + float(EPG))
    v1, i1 = first_max(lg, emask)
    v2, i2 = first_max(lg, emask & (lane != i1))
    ex = jnp.exp(v2 - v1)
    w1 = g_val * (1.0 / (1.0 + ex))
    w2 = g_val * (ex / (1.0 + ex))
    e1 = i1 - lo
    e2 = i2 - lo
    first_low = e1 < e2
    ea = jnp.where(first_low, e1, e2)
    eb = jnp.where(first_low, e2, e1)
    ga = jnp.where(first_low, w1, w2)
    gb = jnp.where(first_low, w2, w1)
    pidx = ea * float(EPG - 1) - ea * (ea - 1.0) * 0.5 + (eb - ea - 1.0)
    bucket = gidx * float(N_PAIRS) + pidx

    lane_i = lax.broadcasted_iota(i32, (tm, LANES), 1)
    h2_ref[:, D:] = jnp.where(lane_i == 0, ga, jnp.where(lane_i == 1, gb, 0.0))
    bk_t = jnp.transpose(jnp.broadcast_to(bucket, (tm, LANES)))
    bk_ref[...] = bk_t[:8, :]


def _mid(x2, of2, os2, w_out, g1, b1, wq, kx, vx, wxo, g2, b2, wrh, wrl, br, alpha, S, tm=512):
    T = x2.shape[0]
    M = kx.shape[0] // (T // S)
    per_b = S // tm
    row = lambda w: pl.BlockSpec((tm, w), lambda i: (i, 0))
    full = lambda a: pl.BlockSpec(a.shape, lambda i: (0,) * a.ndim)
    kvspec = pl.BlockSpec((M, D), lambda i: (i // per_b, 0))
    kernel = functools.partial(_mid_kernel, alpha=alpha)
    return pl.pallas_call(
        kernel,
        grid=(T // tm,),
        in_specs=[row(D), row(512), row(512), full(w_out), full(g1), full(b1), full(wq),
                  kvspec, kvspec, full(wxo), full(g2), full(b2), full(wrh), full(wrl), full(br)],
        out_specs=[row(XW), pl.BlockSpec((8, tm), lambda i: (0, i))],
        out_shape=[jax.ShapeDtypeStruct((T, XW), f32), jax.ShapeDtypeStruct((8, T), f32)],
        scratch_shapes=[pltpu.VMEM((tm, D), bf16)],
        compiler_params=_cparams(("parallel",)),
        name="mid",
    )(x2, of2, os2, w_out, g1, b1, wq, kx, vx, wxo, g2, b2, wrh, wrl, br)


def _rank_kernel(bk_ref, rank_ref, cnt_ref, carry_ref, *, chunk):
    sub = 256

    @pl.when(pl.program_id(0) == 0)
    def _():
        carry_ref[...] = jnp.zeros_like(carry_ref)

    r = lax.broadcasted_iota(i32, (sub, sub), 0)
    c = lax.broadcasted_iota(i32, (sub, sub), 1)
    before = (r < c).astype(bf16)
    bid = lax.broadcasted_iota(i32, (LANES, sub), 0).astype(f32)
    carry = carry_ref[...]
    for j in range(chunk // sub):
        bk = bk_ref[0:1, j * sub:(j + 1) * sub]
        hit = bid == bk
        oh = jnp.where(hit, 1.0, 0.0)
        prior = _dot(oh.astype(bf16), before) + carry
        rank_ref[:, j * sub:(j + 1) * sub] = jnp.sum(jnp.where(hit, prior, 0.0), axis=0, keepdims=True)
        carry = carry + jnp.sum(oh, axis=1, keepdims=True)
    carry_ref[...] = carry
    cnt_ref[...] = carry


def _rank(bk8, chunk=2048):
    T = bk8.shape[1]
    kernel = functools.partial(_rank_kernel, chunk=chunk)
    return pl.pallas_call(
        kernel, grid=(T // chunk,),
        in_specs=[pl.BlockSpec((8, chunk), lambda i: (0, i))],
        out_specs=[pl.BlockSpec((1, chunk), lambda i: (0, i)),
                   pl.BlockSpec((LANES, 1), lambda i: (0, 0))],
        out_shape=[jax.ShapeDtypeStruct((1, T), f32), jax.ShapeDtypeStruct((LANES, 1), f32)],
        scratch_shapes=[pltpu.VMEM((LANES, 1), f32)],
        compiler_params=_cparams(("arbitrary",)), name="rank",
    )(bk8)


def _row_copy(src, dst, sem, s, d):
    return pltpu.make_async_copy(src.at[pl.ds(s, 1)], dst.at[pl.ds(d, 1)], sem)


def _scatter_kernel(bk_ref, rk_ref, ps_ref, pe_ref, src_ref, dst_ref, zero_ref, sem, zsem, *, chunk):
    @pl.when(pl.program_id(0) == 0)
    def _():
        zero_ref[...] = jnp.zeros_like(zero_ref)

        def tail_copy(b):
            start = pl.multiple_of(pe_ref[b] - ROW_BLOCK, ROW_BLOCK)
            return pltpu.make_async_copy(zero_ref, dst_ref.at[pl.ds(start, ROW_BLOCK)], zsem)

        def fill(b, _):
            @pl.when(pe_ref[b] > ps_ref[b])
            def _():
                tail_copy(b).start()
            return 0

        def drain(b, _):
            @pl.when(pe_ref[b] > ps_ref[b])
            def _():
                tail_copy(b).wait()
            return 0

        lax.fori_loop(0, N_BUCKETS, fill, 0)
        lax.fori_loop(0, N_BUCKETS, drain, 0)

        def unused_copy(n):
            return pltpu.make_async_copy(zero_ref, dst_ref.at[pl.ds(pl.multiple_of(n * ROW_BLOCK, ROW_BLOCK),
                                                                    ROW_BLOCK)], zsem)

        first_unused = pe_ref[LANES - 1] // ROW_BLOCK
        n_blocks = dst_ref.shape[0] // ROW_BLOCK
        lax.fori_loop(first_unused, n_blocks, lambda n, _: (unused_copy(n).start(), 0)[1], 0)
        lax.fori_loop(first_unused, n_blocks, lambda n, _: (unused_copy(n).wait(), 0)[1], 0)

    def issue(u, _):
        for k in range(2):
            t = 2 * u + k
            d = ps_ref[bk_ref[t]] + rk_ref[t]
            _row_copy(src_ref, dst_ref, sem, t, d).start(priority=k)
        return 0

    lax.fori_loop(0, chunk // 2, issue, 0)
    pltpu.make_async_copy(src_ref, dst_ref.at[pl.ds(0, chunk)], sem).wait()


def _scatter_rows(bk, rk, ps, pe, src, P, chunk=1024):
    T, w = src.shape
    sm = lambda: pl.BlockSpec((chunk,), lambda i: (i,), memory_space=pltpu.SMEM)
    whole = lambda: pl.BlockSpec(memory_space=pltpu.SMEM)
    kernel = functools.partial(_scatter_kernel, chunk=chunk)
    return pl.pallas_call(
        kernel, grid=(T // chunk,),
        in_specs=[sm(), sm(), whole(), whole(), pl.BlockSpec((chunk, w), lambda i: (i, 0))],
        out_specs=pl.BlockSpec(memory_space=pl.ANY),
        out_shape=jax.ShapeDtypeStruct((P, w), src.dtype),
        scratch_shapes=[pltpu.VMEM((ROW_BLOCK, w), src.dtype), pltpu.SemaphoreType.DMA(()),
                        pltpu.SemaphoreType.DMA(())],
        compiler_params=pltpu.CompilerParams(dimension_semantics=("arbitrary",), has_side_effects=True,
                                             vmem_limit_bytes=VMEM_LIMIT),
        name="scatter_rows",
    )(bk, rk, ps, pe, src)


def _expert_kernel(ea_ref, eb_ref, used_ref, xs_ref, wga_ref, wua_ref, wda_ref,
                   wgb_ref, wub_ref, wdb_ref, y_ref):
    n = pl.program_id(0)

    @pl.when(n < used_ref[0])
    def _():
        x = xs_ref[:, :D].astype(bf16)

        def expert(wg, wu, wd):
            a = _dot(x, wg[0])
            u = _dot(x, wu[0])
            act = a * (1.0 / (1.0 + jnp.exp(-a))) * u
            return _dot(act.astype(bf16), wd[0])

        ga = xs_ref[:, D:D + 1]
        gb = xs_ref[:, D + 1:D + 2]
        y_ref[...] = ga * expert(wga_ref, wua_ref, wda_ref) + gb * expert(wgb_ref, wub_ref, wdb_ref)

    @pl.when(n >= used_ref[0])
    def _():
        y_ref[...] = jnp.zeros_like(y_ref)


def _experts(ea, eb, used, xs, wg, wu, wd):
    P = xs.shape[0]
    nblk = P // ROW_BLOCK

    def xmap(n, ea, eb, used):
        return (jnp.minimum(n, used[0] - 1), 0)

    wa = lambda n, ea, eb, used: (ea[n], 0, 0)
    wb = lambda n, ea, eb, used: (eb[n], 0, 0)
    wgs = lambda m: pl.BlockSpec((1, D, D_EXPERT), m)
    wds = lambda m: pl.BlockSpec((1, D_EXPERT, D), m)
    grid_spec = pltpu.PrefetchScalarGridSpec(
        num_scalar_prefetch=3, grid=(nblk,),
        in_specs=[pl.BlockSpec((ROW_BLOCK, XW), xmap), wgs(wa), wgs(wa), wds(wa), wgs(wb), wgs(wb), wds(wb)],
        out_specs=pl.BlockSpec((ROW_BLOCK, D), lambda n, ea, eb, used: (n, 0)),
    )
    return pl.pallas_call(
        _expert_kernel, grid_spec=grid_spec,
        out_shape=jax.ShapeDtypeStruct((P, D), f32),
        compiler_params=_cparams(("arbitrary",)), name="experts",
    )(ea, eb, used, xs, wg, wu, wd, wg, wu, wd)


def _final_kernel(bk0_ref, rk0_ref, bkn_ref, rkn_ref, ps_ref, h_ref, ys_ref, g_ref, b_ref, o_ref,
                  buf, sems, *, alpha, tm):
    i = pl.program_id(0)
    n = pl.num_programs(0)

    def issue(bk_ref, rk_ref, slot):
        def body(u, _):
            for k in range(2):
                t = 2 * u + k
                s = ps_ref[bk_ref[t]] + rk_ref[t]
                pltpu.make_async_copy(ys_ref.at[pl.ds(s, 1)], buf.at[slot, pl.ds(t, 1)],
                                      sems.at[slot]).start(priority=k)
            return 0
        lax.fori_loop(0, tm // 2, body, 0)

    @pl.when(i == 0)
    def _():
        issue(bk0_ref, rk0_ref, 0)

    @pl.when(i + 1 < n)
    def _():
        issue(bkn_ref, rkn_ref, (i + 1) % 2)

    slot = i % 2
    pltpu.make_async_copy(ys_ref.at[pl.ds(0, tm)], buf.at[slot], sems.at[slot]).wait()
    o_ref[...] = _layer_norm(alpha * h_ref[...] + buf[slot], g_ref[...], b_ref[...])


def _final(bk, rk, ps, h2x, ys, g, b, alpha, tm=1024):
    T = h2x.shape[0]
    n = T // tm
    row = pl.BlockSpec((tm, D), lambda i: (i, 0))
    vec = pl.BlockSpec((1, D), lambda i: (0, 0))
    first = lambda: pl.BlockSpec((tm,), lambda i: (0,), memory_space=pltpu.SMEM)
    nxt = lambda: pl.BlockSpec((tm,), lambda i: (jnp.minimum(i + 1, n - 1),), memory_space=pltpu.SMEM)
    return pl.pallas_call(
        functools.partial(_final_kernel, alpha=alpha, tm=tm), grid=(n,),
        in_specs=[first(), first(), nxt(), nxt(), pl.BlockSpec(memory_space=pltpu.SMEM),
                  row, pl.BlockSpec(memory_space=pl.ANY), vec, vec],
        out_specs=row,
        out_shape=jax.ShapeDtypeStruct((T, D), f32),
        scratch_shapes=[pltpu.VMEM((2, tm, D), f32), pltpu.SemaphoreType.DMA((2,))],
        compiler_params=_cparams(("arbitrary",)), name="final_ln",
    )(bk, rk, bk, rk, ps, h2x, ys, g, b)


def _pair_tables():
    ea = np.zeros((LANES,), np.int32)
    eb = np.zeros((LANES,), np.int32)
    for g in range(N_GROUPS):
        k = 0
        for a in range(EPG):
            for b in range(a + 1, EPG):
                ea[g * N_PAIRS + k] = g * EPG + a
                eb[g * N_PAIRS + k] = g * EPG + b
                k += 1
    return ea, eb


_PAIR_A, _PAIR_B = _pair_tables()


def _layer(h, mem, positions, w_in, b_forget, sinks, w_mix_out, ln_mix_g, ln_mix_b,
           w_xq, w_xkv, w_xout, ln_x_g, ln_x_b, w_rg, b_rg, w_re, b_re,
           w_eg, w_eu, w_ed, ln_f_g, ln_f_b, alpha):
    B, S, _ = h.shape
    T = B * S
    x2 = h.reshape(T, D)
    pos2 = positions.reshape(T, 1).astype(i32)

    o = np.cumsum((0, FOX_W, FOX_W, FOX_W, N_FOX, SWA_Q_W, SWA_KV_W, SWA_KV_W))
    w_qf, w_kf, w_vf, w_fl, w_qs, w_ks, w_vs = (w_in[:, o[i]:o[i + 1]] for i in range(7))
    perm = np.concatenate([np.concatenate([np.arange(j * HD, (j + 1) * HD),
                                           np.arange((j + 4) * HD, (j + 5) * HD)]) for j in range(4)])
    w_all = jnp.concatenate([w_kf, w_qs[:, perm], w_ks, w_vs], axis=1).astype(bf16)
    wqt = w_qf.T.astype(bf16)
    wvt = w_vf.T.astype(bf16)
    wfl = w_fl.T.astype(bf16)
    bfc = b_forget.reshape(N_FOX, 1).astype(f32)
    half = HD // 2
    inv_freq = ROPE_THETA ** (-jnp.arange(half, dtype=f32) / half)
    invf = jnp.tile(inv_freq, LANES // half).reshape(1, LANES)
    w_out = jnp.concatenate([w_mix_out[:FOX_W], w_mix_out[FOX_W:][perm]], axis=0).astype(bf16)

    tq = 512
    qt, kf, vt, qs, ks, vs, lf = _in_proj(x2, pos2, w_all, wqt, wvt, wfl, bfc, invf, tq)
    c, ca = _cumsum(lf, S)
    c4 = c.reshape(N_FOX // 2, 2, T // tq, tq)
    r3 = lambda a: a.reshape(B, S, a.shape[-1])
    o_fox = _fox(qt, r3(kf), r3(ca), vt, c4, B, S, tq)
    o_swa = _swa(sinks.astype(f32), r3(qs), r3(ks), r3(vs), B, S)

    kx, vx = _kvproj(mem.reshape(-1, D), w_xkv.astype(bf16))

    wr = jnp.concatenate([w_rg, jnp.transpose(w_re, (1, 0, 2)).reshape(D, N_EXPERTS)], axis=1)
    wr = jnp.pad(wr, ((0, 0), (0, LANES - wr.shape[1]))).astype(f32)
    wrh = wr.astype(bf16)
    wrl = (wr - wrh.astype(f32)).astype(bf16)
    br = jnp.pad(jnp.concatenate([b_rg, b_re.reshape(-1)]), (0, LANES - N_GROUPS - N_EXPERTS))
    br = br.reshape(1, LANES).astype(f32)
    v2 = lambda a: a.reshape(1, D).astype(f32)
    h2x, bk8 = _mid(x2, o_fox.reshape(T, FOX_W), o_swa.reshape(T, SWA_Q_W), w_out,
                    v2(ln_mix_g), v2(ln_mix_b), w_xq.astype(bf16), kx, vx, w_xout.astype(bf16),
                    v2(ln_x_g), v2(ln_x_b), wrh, wrl, br, alpha, S)

    rank, cnt = _rank(bk8)
    bucket_i = bk8[0].astype(i32)
    rank_i = rank[0].astype(i32)
    counts = cnt[:, 0].astype(i32)
    padded = ((counts + ROW_BLOCK - 1) // ROW_BLOCK) * ROW_BLOCK
    pad_end = jnp.cumsum(padded)
    pad_start = (pad_end - padded).astype(i32)
    P = T + N_BUCKETS * ROW_BLOCK
    nblk = P // ROW_BLOCK
    used = (pad_end[-1] // ROW_BLOCK).astype(i32).reshape(1)
    blk_bucket = jnp.clip(jnp.searchsorted(pad_end, jnp.arange(nblk, dtype=i32) * ROW_BLOCK, side='right'),
                          0, N_BUCKETS - 1)
    blk_a = jnp.asarray(_PAIR_A)[blk_bucket]
    blk_b = jnp.asarray(_PAIR_B)[blk_bucket]

    xs = _scatter_rows(bucket_i, rank_i, pad_start, pad_end.astype(i32), h2x, P)
    ys = _experts(blk_a, blk_b, used, xs, w_eg.astype(bf16), w_eu.astype(bf16), w_ed.astype(bf16))
    out = _final(bucket_i, rank_i, pad_start, h2x, ys, v2(ln_f_g), v2(ln_f_b), alpha)
    return out.reshape(B, S, D)


def kernel(x, mem, positions, w_in, b_forget, sinks, w_mix_out, ln_mix_g, ln_mix_b, w_xq, w_xkv, w_xout,
           ln_x_g, ln_x_b, w_route_group, b_route_group, w_route_expert, b_route_expert,
           w_exp_gate, w_exp_up, w_exp_down, ln_ffn_g, ln_ffn_b):
    depth = w_in.shape[0]
    alpha = (2.0 * depth) ** 0.25
    h = x
    for l in range(depth):
        h = _layer(h, mem, positions, w_in[l], b_forget[l], sinks[l], w_mix_out[l], ln_mix_g[l], ln_mix_b[l],
                   w_xq[l], w_xkv[l], w_xout[l], ln_x_g[l], ln_x_b[l], w_route_group[l], b_route_group[l],
                   w_route_expert[l], b_route_expert[l], w_exp_gate[l], w_exp_up[l], w_exp_down[l],
                   ln_ffn_g[l], ln_ffn_b[l], alpha)
    return h
```

```python
import functools

import jax
import jax.numpy as jnp
import numpy as np
from jax import lax
from jax.experimental import pallas as pl
from jax.experimental.pallas import tpu as pltpu

f32 = jnp.float32
bf16 = jnp.bfloat16
i32 = jnp.int32

D = 1024
HD = 64
N_FOX = 8
N_SWA = 8
N_SWA_KV = 2
FOX_W = 512
SWA_Q_W = 512
SWA_KV_W = 128
WINDOW = 128
ROPE_THETA = 10000.0
N_XH = 4
XHD = 256
N_GROUPS = 4
EPG = 8
N_EXPERTS = 32
D_EXPERT = 512
LN_EPS = 1e-5
NEG = -1e30
LOG2E = 1.4426950408889634
L_ROW = (HD, 0)

LANES = 128
ROW_BLOCK = 128
N_PAIRS = EPG * (EPG - 1) // 2
N_BUCKETS = N_GROUPS * N_PAIRS
XW = D + LANES
VMEM_LIMIT = 56 * 1024 * 1024


def _cparams(sem):
    return pltpu.CompilerParams(dimension_semantics=sem, vmem_limit_bytes=VMEM_LIMIT)


def _layer_norm(v, g, b):
    mu = jnp.mean(v, axis=-1, keepdims=True)
    c = v - mu
    var = jnp.mean(c * c, axis=-1, keepdims=True)
    return c * lax.rsqrt(var + LN_EPS) * g + b


def _dot(a, b):
    return jnp.dot(a, b, preferred_element_type=f32)


def _dot_nt(a, b):
    return lax.dot_general(a, b, (((1,), (1,)), ((), ())), preferred_element_type=f32)


def _inproj_kernel(x_ref, pos_ref, w_ref, wqt_ref, wvt_ref, wfl_ref, bf_ref, invf_ref,
                   qt_ref, kf_ref, vt_ref, qs_ref, ks_ref, vs_ref, lf_ref):
    tm = x_ref.shape[0]
    xb = x_ref[...].astype(bf16)

    def proj(lo, hi):
        return _dot(xb, w_ref[:, lo:hi])

    qt_ref[0] = (_dot_nt(wqt_ref[...], xb) * (0.125 * LOG2E)).astype(bf16)
    vt_ref[0] = _dot_nt(wvt_ref[...], xb).astype(bf16)
    kf_ref[...] = proj(0, 512).astype(bf16)

    ang = pos_ref[...].astype(f32) * invf_ref[...]
    cos = jnp.cos(ang)
    sin = jnp.sin(ang)
    lane = lax.broadcasted_iota(i32, (tm, LANES), 1)
    lo_half = (lane % HD) < (HD // 2)
    sin_s = jnp.where(lo_half, -sin, sin)

    def rope(z):
        rot = jnp.where(lo_half, pltpu.roll(z, LANES - HD // 2, 1), pltpu.roll(z, HD // 2, 1))
        return z * cos + rot * sin_s

    zq = proj(512, 1024)
    for g in range(4):
        sl = slice(g * LANES, (g + 1) * LANES)
        qs_ref[:, sl] = (rope(zq[:, sl]) * 0.125).astype(bf16)
    ks_ref[...] = rope(proj(1024, 1152)).astype(bf16)
    vs_ref[...] = proj(1152, 1280).astype(bf16)

    z = _dot_nt(wfl_ref[...], xb) + bf_ref[...]
    lf_ref[...] = jnp.minimum(z, 0.0) - jnp.log(1.0 + jnp.exp(-jnp.abs(z)))


def _in_proj(x2, pos2, w_all, wqt, wvt, wfl, bfc, invf, tm):
    T = x2.shape[0]
    row = lambda w: pl.BlockSpec((tm, w), lambda i: (i, 0))
    full = lambda a: pl.BlockSpec(a.shape, lambda i: (0,) * a.ndim)
    fmaj = pl.BlockSpec((1, FOX_W, tm), lambda i: (i, 0, 0))
    return pl.pallas_call(
        _inproj_kernel,
        grid=(T // tm,),
        in_specs=[row(D), row(1), full(w_all), full(wqt), full(wvt), full(wfl), full(bfc), full(invf)],
        out_specs=[fmaj, row(512), fmaj, row(512), row(128), row(128),
                   pl.BlockSpec((N_FOX, tm), lambda i: (0, i))],
        out_shape=[jax.ShapeDtypeStruct((T // tm, FOX_W, tm), bf16), jax.ShapeDtypeStruct((T, 512), bf16),
                   jax.ShapeDtypeStruct((T // tm, FOX_W, tm), bf16), jax.ShapeDtypeStruct((T, 512), bf16),
                   jax.ShapeDtypeStruct((T, 128), bf16), jax.ShapeDtypeStruct((T, 128), bf16),
                   jax.ShapeDtypeStruct((N_FOX, T), f32)],
        compiler_params=_cparams(("parallel",)),
        name="in_proj",
    )(x2, pos2, w_all, wqt, wvt, wfl, bfc, invf)


def _cumsum_kernel(lf_ref, c_ref, ca_ref):
    S = lf_ref.shape[1]
    ch = 256
    r = lax.broadcasted_iota(i32, (ch, ch), 0)
    c = lax.broadcasted_iota(i32, (ch, ch), 1)
    tri = (r <= c).astype(f32)
    eye = (r == c).astype(bf16)
    carry = jnp.zeros((N_FOX, 1), f32)
    for j in range(S // ch):
        a = lf_ref[:, j * ch:(j + 1) * ch]
        cc = jnp.dot(a, tri, precision=lax.Precision.HIGHEST, preferred_element_type=f32) + carry
        carry = cc[:, ch - 1:ch]
        c2 = cc * LOG2E
        c_ref[:, j * ch:(j + 1) * ch] = c2
        neg = -c2
        hi = neg.astype(bf16)
        r1 = neg - hi.astype(f32)
        mid = r1.astype(bf16)
        lo = (r1 - mid.astype(f32)).astype(bf16)
        terms = jnp.concatenate([hi, mid, lo, jnp.zeros((LANES - 3 * N_FOX, ch), bf16)], axis=0)
        ca_ref[j * ch:(j + 1) * ch, :] = _dot_nt(eye, terms).astype(bf16)


def _cumsum(lf, S):
    T = lf.shape[1]
    spec = pl.BlockSpec((N_FOX, S), lambda b: (0, b))
    return pl.pallas_call(
        _cumsum_kernel, grid=(T // S,), in_specs=[spec],
        out_specs=[spec, pl.BlockSpec((S, LANES), lambda b: (b, 0))],
        out_shape=[jax.ShapeDtypeStruct((N_FOX, T), f32), jax.ShapeDtypeStruct((T, LANES), bf16)],
        compiler_params=_cparams(("parallel",)), name="cumsum",
    )(lf)


def _fox_kernel(qt_ref, k_ref, ca_ref, vt_ref, c_ref, o_ref, t0_ref, t1_ref, *, tq):
    hp = pl.program_id(1)
    i = pl.program_id(2)
    qt = qt_ref[0]
    row = lax.broadcasted_iota(i32, (LANES, tq), 0)
    is_a = row < HD
    zero = jnp.zeros_like(qt)
    q_ops = []
    for h in range(2):
        ones = jnp.where(((row & 7) == 2 * hp + h) & (row < 3 * N_FOX), 1.0, 0.0).astype(bf16)
        qh = jnp.where(is_a, qt, zero) if h == 0 else jnp.where(is_a, zero, qt)
        q_ops.append(jnp.concatenate([qh, ones], axis=0))
    kr = lax.broadcasted_iota(i32, (tq, tq), 0)
    qc = lax.broadcasted_iota(i32, (tq, tq), 1)
    causal = kr <= qc
    cq = [c_ref[0, h, pl.ds(i, 1), :] for h in range(2)]

    def scores(j, t_ref):
        off = pl.multiple_of(j * tq, tq)
        kblk = jnp.concatenate([k_ref[0, pl.ds(off, tq), :], ca_ref[0, pl.ds(off, tq), :]], axis=1)
        for h in range(2):
            t_ref[h] = _dot(kblk, q_ops[h])

    keep = [jnp.where(is_a, 1.0, 0.0).astype(bf16), jnp.where(is_a, 0.0, 1.0).astype(bf16)]
    ones_row = [jnp.where(row == L_ROW[h], 1.0, 0.0).astype(bf16) for h in range(2)]

    def softmax_pv(j, t_ref, carry, masked):
        vt = vt_ref[j]
        vts = [vt * keep[h] + ones_row[h] for h in range(2)]
        new = []
        for h in range(2):
            m, acc = carry[h]
            t = t_ref[h]
            if masked:
                t = jnp.where(causal, t, NEG)
            m_new = jnp.maximum(m, jnp.max(t, axis=0, keepdims=True) + cq[h])
            alpha = jnp.exp2(m - m_new)
            p = jnp.exp2(t + (cq[h] - m_new))
            acc = alpha * acc + _dot(vts[h], p.astype(bf16))
            new.append((m_new, acc))
        return tuple(new)

    def pair(k, carry):
        j = 2 * k
        scores(j + 1, t1_ref)
        carry = softmax_pv(j, t0_ref, carry, False)
        scores(j + 2, t0_ref)
        return softmax_pv(j + 1, t1_ref, carry, False)

    def odd_tail(carry):
        scores(i, t1_ref)
        carry = softmax_pv(i - 1, t0_ref, carry, False)
        return softmax_pv(i, t1_ref, carry, True)

    def even_tail(carry):
        return softmax_pv(i, t0_ref, carry, True)

    init = tuple((jnp.full((1, tq), NEG, f32), jnp.zeros((LANES, tq), f32)) for _ in range(2))
    scores(0, t0_ref)
    carry = lax.fori_loop(0, i // 2, pair, init)
    (_, acca), (_, accb) = lax.cond(i % 2 == 1, odd_tail, even_tail, carry)
    la = acca[L_ROW[0]:L_ROW[0] + 1, :]
    lb = accb[L_ROW[1]:L_ROW[1] + 1, :]
    ot = jnp.where(is_a, acca / la, accb / lb)
    o_ref[0] = jnp.transpose(ot).astype(bf16)


def _fox(qt, kf, ca, vt, c4, B, S, tq):
    nq = S // tq
    kernel = functools.partial(_fox_kernel, tq=tq)
    return pl.pallas_call(
        kernel,
        grid=(B, N_FOX // 2, nq),
        in_specs=[
            pl.BlockSpec((1, LANES, tq), lambda b, hp, i: (b * nq + i, hp, 0)),
            pl.BlockSpec((1, S, LANES), lambda b, hp, i: (b, 0, hp)),
            pl.BlockSpec((1, S, LANES), lambda b, hp, i: (b, 0, 0)),
            pl.BlockSpec((nq, LANES, tq), lambda b, hp, i: (b, hp, 0)),
            pl.BlockSpec((1, 2, nq, tq), lambda b, hp, i: (hp, 0, b, 0)),
        ],
        out_specs=pl.BlockSpec((1, tq, LANES), lambda b, hp, i: (b, i, hp)),
        out_shape=jax.ShapeDtypeStruct((B, S, FOX_W), bf16),
        scratch_shapes=[pltpu.VMEM((2, tq, tq), f32), pltpu.VMEM((2, tq, tq), f32)],
        compiler_params=_cparams(("parallel", "parallel", "arbitrary")),
        name="fox",
    )(qt, kf, ca, vt, c4)


def _swa_kernel(sink_ref, q_ref, k_ref, v_ref, o_ref, *, tq):
    W = WINDOW
    nsub = tq // W
    n0 = pl.program_id(1) * nsub
    lane = lax.broadcasted_iota(i32, (W, LANES), 1)
    is0 = lane < HD
    rows = lax.broadcasted_iota(i32, (4 * W, 2 * W), 0)
    cols = lax.broadcasted_iota(i32, (4 * W, 2 * W), 1)
    rgrp = lax.broadcasted_iota(i32, (4 * W, 1), 0) // W
    for r in range(nsub):
        nb = n0 + r
        kstart = pl.multiple_of(jnp.maximum(nb * W - W, 0), W)
        ks = k_ref[0, pl.ds(kstart, 2 * W), :]
        vs = v_ref[0, pl.ds(kstart, 2 * W), :]
        qpos = nb * W + rows % W
        kpos = kstart + cols
        valid = (kpos <= qpos) & (qpos - kpos < W)
        outs = []
        for kv in range(2):
            keep = is0 if kv == 0 else jnp.logical_not(is0)
            parts = []
            for g in range(4):
                qg = q_ref[0, r * W:(r + 1) * W, g * LANES:(g + 1) * LANES]
                parts.append(jnp.where(keep, qg, jnp.zeros_like(qg)))
            qstack = jnp.concatenate(parts, axis=0)
            s = jnp.where(valid, _dot_nt(qstack, ks), NEG)
            sink = jnp.zeros((4 * W, 1), f32)
            for g in range(4):
                sink = jnp.where(rgrp == g, sink_ref[kv * 4 + g], sink)
            m = jnp.maximum(jnp.max(s, axis=1, keepdims=True), sink)
            e = jnp.exp(s - m)
            den = jnp.sum(e, axis=1, keepdims=True) + jnp.exp(sink - m)
            outs.append(_dot(e.astype(bf16), vs) / den)
        for g in range(4):
            og = jnp.where(is0, outs[0][g * W:(g + 1) * W], outs[1][g * W:(g + 1) * W])
            o_ref[0, r * W:(r + 1) * W, g * LANES:(g + 1) * LANES] = og.astype(bf16)


def _swa(sinks, qs, ks, vs, B, S, tq=512):
    kernel = functools.partial(_swa_kernel, tq=tq)
    return pl.pallas_call(
        kernel,
        grid=(B, S // tq),
        in_specs=[
            pl.BlockSpec(memory_space=pltpu.SMEM),
            pl.BlockSpec((1, tq, SWA_Q_W), lambda b, i: (b, i, 0)),
            pl.BlockSpec((1, S, SWA_KV_W), lambda b, i: (b, 0, 0)),
            pl.BlockSpec((1, S, SWA_KV_W), lambda b, i: (b, 0, 0)),
        ],
        out_specs=pl.BlockSpec((1, tq, SWA_Q_W), lambda b, i: (b, i, 0)),
        out_shape=jax.ShapeDtypeStruct((B, S, SWA_Q_W), bf16),
        compiler_params=_cparams(("parallel", "arbitrary")),
        name="swa",
    )(sinks, qs, ks, vs)


def _kvproj_kernel(m_ref, w_ref, k_ref, v_ref):
    mb = m_ref[...].astype(bf16)
    k_ref[...] = _dot(mb, w_ref[:, :D]).astype(bf16)
    v_ref[...] = _dot(mb, w_ref[:, D:]).astype(bf16)


def _kvproj(mem2, w_xkv, tm=512):
    R = mem2.shape[0]
    row = pl.BlockSpec((tm, D), lambda i: (i, 0))
    return pl.pallas_call(
        _kvproj_kernel, grid=(R // tm,),
        in_specs=[row, pl.BlockSpec(w_xkv.shape, lambda i: (0, 0))],
        out_specs=[row, row],
        out_shape=[jax.ShapeDtypeStruct((R, D), bf16)] * 2,
        compiler_params=_cparams(("parallel",)), name="kvproj",
    )(mem2, w_xkv)


def _mid_kernel(x_ref, of_ref, os_ref, wo_ref, g1_ref, b1_ref, wq_ref, k_ref, v_ref,
                wxo_ref, g2_ref, b2_ref, wrh_ref, wrl_ref, br_ref,
                h2_ref, bk_ref, oc_ref, *, alpha):
    tm = x_ref.shape[0]
    mix = _dot(of_ref[...], wo_ref[:FOX_W, :]) + _dot(os_ref[...], wo_ref[FOX_W:, :])
    h1 = _layer_norm(alpha * x_ref[...] + mix, g1_ref[...], b1_ref[...])

    q = (_dot(h1.astype(bf16), wq_ref[...]) * 0.0625).astype(bf16)
    for h in range(N_XH):
        sl = slice(h * XHD, (h + 1) * XHD)
        s = _dot_nt(q[:, sl], k_ref[:, sl])
        e = jnp.exp(s - jnp.max(s, axis=1, keepdims=True))
        p = e / jnp.sum(e, axis=1, keepdims=True)
        oc_ref[:, sl] = _dot(p.astype(bf16), v_ref[:, sl]).astype(bf16)
    xo = _dot(oc_ref[...], wxo_ref[...])
    h2 = _layer_norm(alpha * h1 + xo, g2_ref[...], b2_ref[...])
    h2_ref[:, :D] = h2

    hh = h2.astype(bf16)
    hl = (h2 - hh.astype(f32)).astype(bf16)
    lg = _dot(hh, wrh_ref[...]) + _dot(hl, wrh_ref[...]) + _dot(hh, wrl_ref[...]) + br_ref[...]

    lane = lax.broadcasted_iota(i32, (tm, LANES), 1).astype(f32)
    big = float(LANES)

    def first_max(vals, mask):
        vm = jnp.where(mask, vals, NEG)
        top = jnp.max(vm, axis=1, keepdims=True)
        idx = jnp.min(jnp.where(mask & (vm == top), lane, big), axis=1, keepdims=True)
        return top, idx

    gmask = lane < float(N_GROUPS)
    gmax, gidx = first_max(lg, gmask)
    g_val = 1.0 / jnp.sum(jnp.where(gmask, jnp.exp(lg - gmax), 0.0), axis=1, keepdims=True)
    lo = float(N_GROUPS) + float(EPG) * gidx
    emask = (lane >= lo) & (lane < lo + float(EPG))
    v1, i1 = first_max(lg, emask)
    v2, i2 = first_max(lg, emask & (lane != i1))
    ex = jnp.exp(v2 - v1)
    w1 = g_val * (1.0 / (1.0 + ex))
    w2 = g_val * (ex / (1.0 + ex))
    e1 = i1 - lo
    e2 = i2 - lo
    first_low = e1 < e2
    ea = jnp.where(first_low, e1, e2)
    eb = jnp.where(first_low, e2, e1)
    ga = jnp.where(first_low, w1, w2)
    gb = jnp.where(first_low, w2, w1)
    pidx = ea * float(EPG - 1) - ea * (ea - 1.0) * 0.5 + (eb - ea - 1.0)
    bucket = gidx * float(N_PAIRS) + pidx

    lane_i = lax.broadcasted_iota(i32, (tm, LANES), 1)
    h2_ref[:, D:] = jnp.where(lane_i == 0, ga, jnp.where(lane_i == 1, gb, 0.0))
    bk_t = jnp.transpose(jnp.broadcast_to(bucket, (tm, LANES)))
    bk_ref[...] = bk_t[:8, :]


def _mid(x2, of2, os2, w_out, g1, b1, wq, kx, vx, wxo, g2, b2, wrh, wrl, br, alpha, S, tm=512):
    T = x2.shape[0]
    M = kx.shape[0] // (T // S)
    per_b = S // tm
    row = lambda w: pl.BlockSpec((tm, w), lambda i: (i, 0))
    full = lambda a: pl.BlockSpec(a.shape, lambda i: (0,) * a.ndim)
    kvspec = pl.BlockSpec((M, D), lambda i: (i // per_b, 0))
    kernel = functools.partial(_mid_kernel, alpha=alpha)
    return pl.pallas_call(
        kernel,
        grid=(T // tm,),
        in_specs=[row(D), row(512), row(512), full(w_out), full(g1), full(b1), full(wq),
                  kvspec, kvspec, full(wxo), full(g2), full(b2), full(wrh), full(wrl), full(br)],
        out_specs=[row(XW), pl.BlockSpec((8, tm), lambda i: (0, i))],
        out_shape=[jax.ShapeDtypeStruct((T, XW), f32), jax.ShapeDtypeStruct((8, T), f32)],
        scratch_shapes=[pltpu.VMEM((tm, D), bf16)],
        compiler_params=_cparams(("parallel",)),
        name="mid",
    )(x2, of2, os2, w_out, g1, b1, wq, kx, vx, wxo, g2, b2, wrh, wrl, br)


def _rank_kernel(bk_ref, rank_ref, cnt_ref, carry_ref, *, chunk):
    sub = 256

    @pl.when(pl.program_id(0) == 0)
    def _():
        carry_ref[...] = jnp.zeros_like(carry_ref)

    r = lax.broadcasted_iota(i32, (sub, sub), 0)
    c = lax.broadcasted_iota(i32, (sub, sub), 1)
    before = (r < c).astype(bf16)
    bid = lax.broadcasted_iota(i32, (LANES, sub), 0).astype(f32)
    carry = carry_ref[...]
    for j in range(chunk // sub):
        bk = bk_ref[0:1, j * sub:(j + 1) * sub]
        hit = bid == bk
        oh = jnp.where(hit, 1.0, 0.0)
        prior = _dot(oh.astype(bf16), before) + carry
        rank_ref[:, j * sub:(j + 1) * sub] = jnp.sum(jnp.where(hit, prior, 0.0), axis=0, keepdims=True)
        carry = carry + jnp.sum(oh, axis=1, keepdims=True)
    carry_ref[...] = carry
    cnt_ref[...] = carry


def _rank(bk8, chunk=2048):
    T = bk8.shape[1]
    kernel = functools.partial(_rank_kernel, chunk=chunk)
    return pl.pallas_call(
        kernel, grid=(T // chunk,),
        in_specs=[pl.BlockSpec((8, chunk), lambda i: (0, i))],
        out_specs=[pl.BlockSpec((1, chunk), lambda i: (0, i)),
                   pl.BlockSpec((LANES, 1), lambda i: (0, 0))],
        out_shape=[jax.ShapeDtypeStruct((1, T), f32), jax.ShapeDtypeStruct((LANES, 1), f32)],
        scratch_shapes=[pltpu.VMEM((LANES, 1), f32)],
        compiler_params=_cparams(("arbitrary",)), name="rank",
    )(bk8)


def _row_copy(src, dst, sem, s, d):
    return pltpu.make_async_copy(src.at[pl.ds(s, 1)], dst.at[pl.ds(d, 1)], sem)


def _scatter_kernel(bk_ref, rk_ref, ps_ref, pe_ref, src_ref, dst_ref, zero_ref, sem, zsem, *, chunk):
    @pl.when(pl.program_id(0) == 0)
    def _():
        zero_ref[...] = jnp.zeros_like(zero_ref)

        def tail_copy(b):
            start = pl.multiple_of(pe_ref[b] - ROW_BLOCK, ROW_BLOCK)
            return pltpu.make_async_copy(zero_ref, dst_ref.at[pl.ds(start, ROW_BLOCK)], zsem)

        def fill(b, _):
            @pl.when(pe_ref[b] > ps_ref[b])
            def _():
                tail_copy(b).start()
            return 0

        def drain(b, _):
            @pl.when(pe_ref[b] > ps_ref[b])
            def _():
                tail_copy(b).wait()
            return 0

        lax.fori_loop(0, N_BUCKETS, fill, 0)
        lax.fori_loop(0, N_BUCKETS, drain, 0)

        def unused_copy(n):
            return pltpu.make_async_copy(zero_ref, dst_ref.at[pl.ds(pl.multiple_of(n * ROW_BLOCK, ROW_BLOCK),
                                                                    ROW_BLOCK)], zsem)

        first_unused = pe_ref[LANES - 1] // ROW_BLOCK
        n_blocks = dst_ref.shape[0] // ROW_BLOCK
        lax.fori_loop(first_unused, n_blocks, lambda n, _: (unused_copy(n).start(), 0)[1], 0)
        lax.fori_loop(first_unused, n_blocks, lambda n, _: (unused_copy(n).wait(), 0)[1], 0)

    def issue(u, _):
        for k in range(2):
            t = 2 * u + k
            d = ps_ref[bk_ref[t]] + rk_ref[t]
            _row_copy(src_ref, dst_ref, sem, t, d).start(priority=k)
        return 0

    lax.fori_loop(0, chunk // 2, issue, 0)
    pltpu.make_async_copy(src_ref, dst_ref.at[pl.ds(0, chunk)], sem).wait()


def _scatter_rows(bk, rk, ps, pe, src, P, chunk=1024):
    T, w = src.shape
    sm = lambda: pl.BlockSpec((chunk,), lambda i: (i,), memory_space=pltpu.SMEM)
    whole = lambda: pl.BlockSpec(memory_space=pltpu.SMEM)
    kernel = functools.partial(_scatter_kernel, chunk=chunk)
    return pl.pallas_call(
        kernel, grid=(T // chunk,),
        in_specs=[sm(), sm(), whole(), whole(), pl.BlockSpec((chunk, w), lambda i: (i, 0))],
        out_specs=pl.BlockSpec(memory_space=pl.ANY),
        out_shape=jax.ShapeDtypeStruct((P, w), src.dtype),
        scratch_shapes=[pltpu.VMEM((ROW_BLOCK, w), src.dtype), pltpu.SemaphoreType.DMA(()),
                        pltpu.SemaphoreType.DMA(())],
        compiler_params=pltpu.CompilerParams(dimension_semantics=("arbitrary",), has_side_effects=True,
                                             vmem_limit_bytes=VMEM_LIMIT),
        name="scatter_rows",
    )(bk, rk, ps, pe, src)


def _expert_kernel(ea_ref, eb_ref, used_ref, xs_ref, wga_ref, wua_ref, wda_ref,
                   wgb_ref, wub_ref, wdb_ref, y_ref):
    n = pl.program_id(0)

    @pl.when(n < used_ref[0])
    def _():
        x = xs_ref[:, :D].astype(bf16)

        def expert(wg, wu, wd):
            a = _dot(x, wg[0])
            u = _dot(x, wu[0])
            act = a * (1.0 / (1.0 + jnp.exp(-a))) * u
            return _dot(act.astype(bf16), wd[0])

        ga = xs_ref[:, D:D + 1]
        gb = xs_ref[:, D + 1:D + 2]
        y_ref[...] = ga * expert(wga_ref, wua_ref, wda_ref) + gb * expert(wgb_ref, wub_ref, wdb_ref)

    @pl.when(n >= used_ref[0])
    def _():
        y_ref[...] = jnp.zeros_like(y_ref)


def _experts(ea, eb, used, xs, wg, wu, wd):
    P = xs.shape[0]
    nblk = P // ROW_BLOCK

    def xmap(n, ea, eb, used):
        return (jnp.minimum(n, used[0] - 1), 0)

    wa = lambda n, ea, eb, used: (ea[n], 0, 0)
    wb = lambda n, ea, eb, used: (eb[n], 0, 0)
    wgs = lambda m: pl.BlockSpec((1, D, D_EXPERT), m)
    wds = lambda m: pl.BlockSpec((1, D_EXPERT, D), m)
    grid_spec = pltpu.PrefetchScalarGridSpec(
        num_scalar_prefetch=3, grid=(nblk,),
        in_specs=[pl.BlockSpec((ROW_BLOCK, XW), xmap), wgs(wa), wgs(wa), wds(wa), wgs(wb), wgs(wb), wds(wb)],
        out_specs=pl.BlockSpec((ROW_BLOCK, D), lambda n, ea, eb, used: (n, 0)),
    )
    return pl.pallas_call(
        _expert_kernel, grid_spec=grid_spec,
        out_shape=jax.ShapeDtypeStruct((P, D), f32),
        compiler_params=_cparams(("arbitrary",)), name="experts",
    )(ea, eb, used, xs, wg, wu, wd, wg, wu, wd)


def _final_kernel(bk0_ref, rk0_ref, bkn_ref, rkn_ref, ps_ref, h_ref, ys_ref, g_ref, b_ref, o_ref,
                  buf, sems, *, alpha, tm):
    i = pl.program_id(0)
    n = pl.num_programs(0)

    def issue(bk_ref, rk_ref, slot):
        def body(u, _):
            for k in range(2):
                t = 2 * u + k
                s = ps_ref[bk_ref[t]] + rk_ref[t]
                pltpu.make_async_copy(ys_ref.at[pl.ds(s, 1)], buf.at[slot, pl.ds(t, 1)],
                                      sems.at[slot]).start(priority=k)
            return 0
        lax.fori_loop(0, tm // 2, body, 0)

    @pl.when(i == 0)
    def _():
        issue(bk0_ref, rk0_ref, 0)

    @pl.when(i + 1 < n)
    def _():
        issue(bkn_ref, rkn_ref, (i + 1) % 2)

    slot = i % 2
    pltpu.make_async_copy(ys_ref.at[pl.ds(0, tm)], buf.at[slot], sems.at[slot]).wait()
    o_ref[...] = _layer_norm(alpha * h_ref[...] + buf[slot], g_ref[...], b_ref[...])


def _final(bk, rk, ps, h2x, ys, g, b, alpha, tm=1024):
    T = h2x.shape[0]
    n = T // tm
    row = pl.BlockSpec((tm, D), lambda i: (i, 0))
    vec = pl.BlockSpec((1, D), lambda i: (0, 0))
    first = lambda: pl.BlockSpec((tm,), lambda i: (0,), memory_space=pltpu.SMEM)
    nxt = lambda: pl.BlockSpec((tm,), lambda i: (jnp.minimum(i + 1, n - 1),), memory_space=pltpu.SMEM)
    return pl.pallas_call(
        functools.partial(_final_kernel, alpha=alpha, tm=tm), grid=(n,),
        in_specs=[first(), first(), nxt(), nxt(), pl.BlockSpec(memory_space=pltpu.SMEM),
                  row, pl.BlockSpec(memory_space=pl.ANY), vec, vec],
        out_specs=row,
        out_shape=jax.ShapeDtypeStruct((T, D), f32),
        scratch_shapes=[pltpu.VMEM((2, tm, D), f32), pltpu.SemaphoreType.DMA((2,))],
        compiler_params=_cparams(("arbitrary",)), name="final_ln",
    )(bk, rk, bk, rk, ps, h2x, ys, g, b)


def _pair_tables():
    ea = np.zeros((LANES,), np.int32)
    eb = np.zeros((LANES,), np.int32)
    for g in range(N_GROUPS):
        k = 0
        for a in range(EPG):
            for b in range(a + 1, EPG):
                ea[g * N_PAIRS + k] = g * EPG + a
                eb[g * N_PAIRS + k] = g * EPG + b
                k += 1
    return ea, eb


_PAIR_A, _PAIR_B = _pair_tables()


def _layer(h, mem, positions, w_in, b_forget, sinks, w_mix_out, ln_mix_g, ln_mix_b,
           w_xq, w_xkv, w_xout, ln_x_g, ln_x_b, w_rg, b_rg, w_re, b_re,
           w_eg, w_eu, w_ed, ln_f_g, ln_f_b, alpha):
    B, S, _ = h.shape
    T = B * S
    x2 = h.reshape(T, D)
    pos2 = positions.reshape(T, 1).astype(i32)

    o = np.cumsum((0, FOX_W, FOX_W, FOX_W, N_FOX, SWA_Q_W, SWA_KV_W, SWA_KV_W))
    w_qf, w_kf, w_vf, w_fl, w_qs, w_ks, w_vs = (w_in[:, o[i]:o[i + 1]] for i in range(7))
    perm = np.concatenate([np.concatenate([np.arange(j * HD, (j + 1) * HD),
                                           np.arange((j + 4) * HD, (j + 5) * HD)]) for j in range(4)])
    w_all = jnp.concatenate([w_kf, w_qs[:, perm], w_ks, w_vs], axis=1).astype(bf16)
    wqt = w_qf.T.astype(bf16)
    wvt = w_vf.T.astype(bf16)
    wfl = w_fl.T.astype(bf16)
    bfc = b_forget.reshape(N_FOX, 1).astype(f32)
    half = HD // 2
    inv_freq = ROPE_THETA ** (-jnp.arange(half, dtype=f32) / half)
    invf = jnp.tile(inv_freq, LANES // half).reshape(1, LANES)
    w_out = jnp.concatenate([w_mix_out[:FOX_W], w_mix_out[FOX_W:][perm]], axis=0).astype(bf16)

    tq = 512
    qt, kf, vt, qs, ks, vs, lf = _in_proj(x2, pos2, w_all, wqt, wvt, wfl, bfc, invf, tq)
    c, ca = _cumsum(lf, S)
    c4 = c.reshape(N_FOX // 2, 2, T // tq, tq)
    r3 = lambda a: a.reshape(B, S, a.shape[-1])
    o_fox = _fox(qt, r3(kf), r3(ca), vt, c4, B, S, tq)
    o_swa = _swa(sinks.astype(f32), r3(qs), r3(ks), r3(vs), B, S)

    kx, vx = _kvproj(mem.reshape(-1, D), w_xkv.astype(bf16))

    wr = jnp.concatenate([w_rg, jnp.transpose(w_re, (1, 0, 2)).reshape(D, N_EXPERTS)], axis=1)
    wr = jnp.pad(wr, ((0, 0), (0, LANES - wr.shape[1]))).astype(f32)
    wrh = wr.astype(bf16)
    wrl = (wr - wrh.astype(f32)).astype(bf16)
    br = jnp.pad(jnp.concatenate([b_rg, b_re.reshape(-1)]), (0, LANES - N_GROUPS - N_EXPERTS))
    br = br.reshape(1, LANES).astype(f32)
    v2 = lambda a: a.reshape(1, D).astype(f32)
    h2x, bk8 = _mid(x2, o_fox.reshape(T, FOX_W), o_swa.reshape(T, SWA_Q_W), w_out,
                    v2(ln_mix_g), v2(ln_mix_b), w_xq.astype(bf16), kx, vx, w_xout.astype(bf16),
                    v2(ln_x_g), v2(ln_x_b), wrh, wrl, br, alpha, S)

    rank, cnt = _rank(bk8)
    bucket_i = bk8[0].astype(i32)
    rank_i = rank[0].astype(i32)
    counts = cnt[:, 0].astype(i32)
    padded = ((counts + ROW_BLOCK - 1) // ROW_BLOCK) * ROW_BLOCK
    pad_end = jnp.cumsum(padded)
    pad_start = (pad_end - padded).astype(i32)
    P = T + N_BUCKETS * ROW_BLOCK
    nblk = P // ROW_BLOCK
    used = (pad_end[-1] // ROW_BLOCK).astype(i32).reshape(1)
    blk_bucket = jnp.clip(jnp.searchsorted(pad_end, jnp.arange(nblk, dtype=i32) * ROW_BLOCK, side='right'),
                          0, N_BUCKETS - 1)
    blk_a = jnp.asarray(_PAIR_A)[blk_bucket]
    blk_b = jnp.asarray(_PAIR_B)[blk_bucket]

    xs = _scatter_rows(bucket_i, rank_i, pad_start, pad_end.astype(i32), h2x, P)
    ys = _experts(blk_a, blk_b, used, xs, w_eg.astype(bf16), w_eu.astype(bf16), w_ed.astype(bf16))
    out = _final(bucket_i, rank_i, pad_start, h2x, ys, v2(ln_f_g), v2(ln_f_b), alpha)
    return out.reshape(B, S, D)


def kernel(x, mem, positions, w_in, b_forget, sinks, w_mix_out, ln_mix_g, ln_mix_b, w_xq, w_xkv, w_xout,
           ln_x_g, ln_x_b, w_route_group, b_route_group, w_route_expert, b_route_expert,
           w_exp_gate, w_exp_up, w_exp_down, ln_ffn_g, ln_ffn_b):
    depth = w_in.shape[0]
    alpha = (2.0 * depth) ** 0.25
    h = x
    for l in range(depth):
        h = _layer(h, mem, positions, w_in[l], b_forget[l], sinks[l], w_mix_out[l], ln_mix_g[l], ln_mix_b[l],
                   w_xq[l], w_xkv[l], w_xout[l], ln_x_g[l], ln_x_b[l], w_route_group[l], b_route_group[l],
                   w_route_expert[l], b_route_expert[l], w_exp_gate[l], w_exp_up[l], w_exp_down[l],
                   ln_ffn_g[l], ln_ffn_b[l], alpha)
    return h
```

```python
import functools

import jax
import jax.numpy as jnp
import numpy as np
from jax import lax
from jax.experimental import pallas as pl
from jax.experimental.pallas import tpu as pltpu
from jax.experimental.pallas import tpu_sc as plsc

f32 = jnp.float32
bf16 = jnp.bfloat16
i32 = jnp.int32

D = 1024
HD = 64
N_FOX = 8
N_SWA = 8
N_SWA_KV = 2
FOX_W = 512
SWA_Q_W = 512
SWA_KV_W = 128
WINDOW = 128
ROPE_THETA = 10000.0
N_XH = 4
XHD = 256
N_GROUPS = 4
EPG = 8
N_EXPERTS = 32
D_EXPERT = 512
LN_EPS = 1e-5
NEG = -1e30
LOG2E = 1.4426950408889634
L_ROW = (HD, 0)

SC_CORES = 2
SC_SUBCORES = 16
LANES = 128
ROW_BLOCK = 128
N_PAIRS = EPG * (EPG - 1) // 2
N_BUCKETS = N_GROUPS * N_PAIRS
XW = D + LANES
VMEM_LIMIT = 56 * 1024 * 1024


def _cparams(sem):
    return pltpu.CompilerParams(dimension_semantics=sem, vmem_limit_bytes=VMEM_LIMIT)


def _layer_norm(v, g, b):
    mu = jnp.mean(v, axis=-1, keepdims=True)
    c = v - mu
    var = jnp.mean(c * c, axis=-1, keepdims=True)
    return c * lax.rsqrt(var + LN_EPS) * g + b


def _dot(a, b):
    return jnp.dot(a, b, preferred_element_type=f32)


def _dot_nt(a, b):
    return lax.dot_general(a, b, (((1,), (1,)), ((), ())), preferred_element_type=f32)


def _inproj_kernel(x_ref, pos_ref, w_ref, wqt_ref, wvt_ref, wfl_ref, bf_ref, invf_ref,
                   qt_ref, kf_ref, vt_ref, qs_ref, ks_ref, vs_ref, lf_ref):
    tm = x_ref.shape[0]
    xb = x_ref[...].astype(bf16)

    def proj(lo, hi):
        return _dot(xb, w_ref[:, lo:hi])

    qt_ref[0] = (_dot_nt(wqt_ref[...], xb) * (0.125 * LOG2E)).astype(bf16)
    vt_ref[0] = _dot_nt(wvt_ref[...], xb).astype(bf16)
    kf_ref[...] = proj(0, 512).astype(bf16)

    ang = pos_ref[...].astype(f32) * invf_ref[...]
    cos = jnp.cos(ang)
    sin = jnp.sin(ang)
    lane = lax.broadcasted_iota(i32, (tm, LANES), 1)
    lo_half = (lane % HD) < (HD // 2)
    sin_s = jnp.where(lo_half, -sin, sin)

    def rope(z):
        rot = jnp.where(lo_half, pltpu.roll(z, LANES - HD // 2, 1), pltpu.roll(z, HD // 2, 1))
        return z * cos + rot * sin_s

    zq = proj(512, 1024)
    for g in range(4):
        sl = slice(g * LANES, (g + 1) * LANES)
        qs_ref[:, sl] = (rope(zq[:, sl]) * 0.125).astype(bf16)
    ks_ref[...] = rope(proj(1024, 1152)).astype(bf16)
    vs_ref[...] = proj(1152, 1280).astype(bf16)

    z = _dot_nt(wfl_ref[...], xb) + bf_ref[...]
    lf_ref[...] = jnp.minimum(z, 0.0) - jnp.log(1.0 + jnp.exp(-jnp.abs(z)))


def _in_proj(x2, pos2, w_all, wqt, wvt, wfl, bfc, invf, tm):
    T = x2.shape[0]
    row = lambda w: pl.BlockSpec((tm, w), lambda i: (i, 0))
    full = lambda a: pl.BlockSpec(a.shape, lambda i: (0,) * a.ndim)
    fmaj = pl.BlockSpec((1, FOX_W, tm), lambda i: (i, 0, 0))
    return pl.pallas_call(
        _inproj_kernel,
        grid=(T // tm,),
        in_specs=[row(D), row(1), full(w_all), full(wqt), full(wvt), full(wfl), full(bfc), full(invf)],
        out_specs=[fmaj, row(512), fmaj, row(512), row(128), row(128),
                   pl.BlockSpec((N_FOX, tm), lambda i: (0, i))],
        out_shape=[jax.ShapeDtypeStruct((T // tm, FOX_W, tm), bf16), jax.ShapeDtypeStruct((T, 512), bf16),
                   jax.ShapeDtypeStruct((T // tm, FOX_W, tm), bf16), jax.ShapeDtypeStruct((T, 512), bf16),
                   jax.ShapeDtypeStruct((T, 128), bf16), jax.ShapeDtypeStruct((T, 128), bf16),
                   jax.ShapeDtypeStruct((N_FOX, T), f32)],
        compiler_params=_cparams(("parallel",)),
        name="in_proj",
    )(x2, pos2, w_all, wqt, wvt, wfl, bfc, invf)


def _cumsum_kernel(lf_ref, c_ref, ca_ref):
    S = lf_ref.shape[1]
    ch = 256
    r = lax.broadcasted_iota(i32, (ch, ch), 0)
    c = lax.broadcasted_iota(i32, (ch, ch), 1)
    tri = (r <= c).astype(f32)
    eye = (r == c).astype(bf16)
    carry = jnp.zeros((N_FOX, 1), f32)
    for j in range(S // ch):
        a = lf_ref[:, j * ch:(j + 1) * ch]
        cc = jnp.dot(a, tri, precision=lax.Precision.HIGHEST, preferred_element_type=f32) + carry
        carry = cc[:, ch - 1:ch]
        c2 = cc * LOG2E
        c_ref[:, j * ch:(j + 1) * ch] = c2
        neg = -c2
        hi = neg.astype(bf16)
        r1 = neg - hi.astype(f32)
        mid = r1.astype(bf16)
        lo = (r1 - mid.astype(f32)).astype(bf16)
        terms = jnp.concatenate([hi, mid, lo, jnp.zeros((LANES - 3 * N_FOX, ch), bf16)], axis=0)
        ca_ref[j * ch:(j + 1) * ch, :] = _dot_nt(eye, terms).astype(bf16)


def _cumsum(lf, S):
    T = lf.shape[1]
    spec = pl.BlockSpec((N_FOX, S), lambda b: (0, b))
    return pl.pallas_call(
        _cumsum_kernel, grid=(T // S,), in_specs=[spec],
        out_specs=[spec, pl.BlockSpec((S, LANES), lambda b: (b, 0))],
        out_shape=[jax.ShapeDtypeStruct((N_FOX, T), f32), jax.ShapeDtypeStruct((T, LANES), bf16)],
        compiler_params=_cparams(("parallel",)), name="cumsum",
    )(lf)


def _fox_kernel(qt_ref, k_ref, ca_ref, vt_ref, c_ref, o_ref, t0_ref, t1_ref, *, tq):
    hp = pl.program_id(1)
    i = pl.program_id(2)
    qt = qt_ref[0]
    row = lax.broadcasted_iota(i32, (LANES, tq), 0)
    is_a = row < HD
    zero = jnp.zeros_like(qt)
    q_ops = []
    for h in range(2):
        ones = jnp.where(((row & 7) == 2 * hp + h) & (row < 3 * N_FOX), 1.0, 0.0).astype(bf16)
        qh = jnp.where(is_a, qt, zero) if h == 0 else jnp.where(is_a, zero, qt)
        q_ops.append(jnp.concatenate([qh, ones], axis=0))
    kr = lax.broadcasted_iota(i32, (tq, tq), 0)
    qc = lax.broadcasted_iota(i32, (tq, tq), 1)
    causal = kr <= qc
    cq = [c_ref[0, h, pl.ds(i, 1), :] for h in range(2)]

    def scores(j, t_ref):
        off = pl.multiple_of(j * tq, tq)
        kblk = jnp.concatenate([k_ref[0, pl.ds(off, tq), :], ca_ref[0, pl.ds(off, tq), :]], axis=1)
        for h in range(2):
            t_ref[h] = _dot(kblk, q_ops[h])

    keep = [jnp.where(is_a, 1.0, 0.0).astype(bf16), jnp.where(is_a, 0.0, 1.0).astype(bf16)]
    ones_row = [jnp.where(row == L_ROW[h], 1.0, 0.0).astype(bf16) for h in range(2)]

    def softmax_pv(j, t_ref, carry, masked):
        vt = vt_ref[j]
        vts = [vt * keep[h] + ones_row[h] for h in range(2)]
        new = []
        for h in range(2):
            m, acc = carry[h]
            t = t_ref[h]
            if masked:
                t = jnp.where(causal, t, NEG)
            m_new = jnp.maximum(m, jnp.max(t, axis=0, keepdims=True) + cq[h])
            alpha = jnp.exp2(m - m_new)
            p = jnp.exp2(t + (cq[h] - m_new))
            acc = alpha * acc + _dot(vts[h], p.astype(bf16))
            new.append((m_new, acc))
        return tuple(new)

    def pair(k, carry):
        j = 2 * k
        scores(j + 1, t1_ref)
        carry = softmax_pv(j, t0_ref, carry, False)
        scores(j + 2, t0_ref)
        return softmax_pv(j + 1, t1_ref, carry, False)

    def odd_tail(carry):
        scores(i, t1_ref)
        carry = softmax_pv(i - 1, t0_ref, carry, False)
        return softmax_pv(i, t1_ref, carry, True)

    def even_tail(carry):
        return softmax_pv(i, t0_ref, carry, True)

    init = tuple((jnp.full((1, tq), NEG, f32), jnp.zeros((LANES, tq), f32)) for _ in range(2))
    scores(0, t0_ref)
    carry = lax.fori_loop(0, i // 2, pair, init)
    (_, acca), (_, accb) = lax.cond(i % 2 == 1, odd_tail, even_tail, carry)
    la = acca[L_ROW[0]:L_ROW[0] + 1, :]
    lb = accb[L_ROW[1]:L_ROW[1] + 1, :]
    ot = jnp.where(is_a, acca / la, accb / lb)
    o_ref[0] = jnp.transpose(ot).astype(bf16)


def _fox(qt, kf, ca, vt, c4, B, S, tq):
    nq = S // tq
    kernel = functools.partial(_fox_kernel, tq=tq)
    return pl.pallas_call(
        kernel,
        grid=(B, N_FOX // 2, nq),
        in_specs=[
            pl.BlockSpec((1, LANES, tq), lambda b, hp, i: (b * nq + i, hp, 0)),
            pl.BlockSpec((1, S, LANES), lambda b, hp, i: (b, 0, hp)),
            pl.BlockSpec((1, S, LANES), lambda b, hp, i: (b, 0, 0)),
            pl.BlockSpec((nq, LANES, tq), lambda b, hp, i: (b, hp, 0)),
            pl.BlockSpec((1, 2, nq, tq), lambda b, hp, i: (hp, 0, b, 0)),
        ],
        out_specs=pl.BlockSpec((1, tq, LANES), lambda b, hp, i: (b, i, hp)),
        out_shape=jax.ShapeDtypeStruct((B, S, FOX_W), bf16),
        scratch_shapes=[pltpu.VMEM((2, tq, tq), f32), pltpu.VMEM((2, tq, tq), f32)],
        compiler_params=_cparams(("parallel", "parallel", "arbitrary")),
        name="fox",
    )(qt, kf, ca, vt, c4)


def _swa_kernel(sink_ref, q_ref, k_ref, v_ref, o_ref, *, tq):
    W = WINDOW
    nsub = tq // W
    n0 = pl.program_id(1) * nsub
    lane = lax.broadcasted_iota(i32, (W, LANES), 1)
    is0 = lane < HD
    rows = lax.broadcasted_iota(i32, (4 * W, 2 * W), 0)
    cols = lax.broadcasted_iota(i32, (4 * W, 2 * W), 1)
    rgrp = lax.broadcasted_iota(i32, (4 * W, 1), 0) // W
    for r in range(nsub):
        nb = n0 + r
        kstart = pl.multiple_of(jnp.maximum(nb * W - W, 0), W)
        ks = k_ref[0, pl.ds(kstart, 2 * W), :]
        vs = v_ref[0, pl.ds(kstart, 2 * W), :]
        qpos = nb * W + rows % W
        kpos = kstart + cols
        valid = (kpos <= qpos) & (qpos - kpos < W)
        outs = []
        for kv in range(2):
            keep = is0 if kv == 0 else jnp.logical_not(is0)
            parts = []
            for g in range(4):
                qg = q_ref[0, r * W:(r + 1) * W, g * LANES:(g + 1) * LANES]
                parts.append(jnp.where(keep, qg, jnp.zeros_like(qg)))
            qstack = jnp.concatenate(parts, axis=0)
            s = jnp.where(valid, _dot_nt(qstack, ks), NEG)
            sink = jnp.zeros((4 * W, 1), f32)
            for g in range(4):
                sink = jnp.where(rgrp == g, sink_ref[kv * 4 + g], sink)
            m = jnp.maximum(jnp.max(s, axis=1, keepdims=True), sink)
            e = jnp.exp(s - m)
            den = jnp.sum(e, axis=1, keepdims=True) + jnp.exp(sink - m)
            outs.append(_dot(e.astype(bf16), vs) / den)
        for g in range(4):
            og = jnp.where(is0, outs[0][g * W:(g + 1) * W], outs[1][g * W:(g + 1) * W])
            o_ref[0, r * W:(r + 1) * W, g * LANES:(g + 1) * LANES] = og.astype(bf16)


def _swa(sinks, qs, ks, vs, B, S, tq=512):
    kernel = functools.partial(_swa_kernel, tq=tq)
    return pl.pallas_call(
        kernel,
        grid=(B, S // tq),
        in_specs=[
            pl.BlockSpec(memory_space=pltpu.SMEM),
            pl.BlockSpec((1, tq, SWA_Q_W), lambda b, i: (b, i, 0)),
            pl.BlockSpec((1, S, SWA_KV_W), lambda b, i: (b, 0, 0)),
            pl.BlockSpec((1, S, SWA_KV_W), lambda b, i: (b, 0, 0)),
        ],
        out_specs=pl.BlockSpec((1, tq, SWA_Q_W), lambda b, i: (b, i, 0)),
        out_shape=jax.ShapeDtypeStruct((B, S, SWA_Q_W), bf16),
        compiler_params=_cparams(("parallel", "arbitrary")),
        name="swa",
    )(sinks, qs, ks, vs)


def _kvproj_kernel(m_ref, w_ref, k_ref, v_ref):
    mb = m_ref[...].astype(bf16)
    k_ref[...] = _dot(mb, w_ref[:, :D]).astype(bf16)
    v_ref[...] = _dot(mb, w_ref[:, D:]).astype(bf16)


def _kvproj(mem2, w_xkv, tm=512):
    R = mem2.shape[0]
    row = pl.BlockSpec((tm, D), lambda i: (i, 0))
    return pl.pallas_call(
        _kvproj_kernel, grid=(R // tm,),
        in_specs=[row, pl.BlockSpec(w_xkv.shape, lambda i: (0, 0))],
        out_specs=[row, row],
        out_shape=[jax.ShapeDtypeStruct((R, D), bf16)] * 2,
        compiler_params=_cparams(("parallel",)), name="kvproj",
    )(mem2, w_xkv)


def _mid_kernel(x_ref, of_ref, os_ref, wo_ref, g1_ref, b1_ref, wq_ref, k_ref, v_ref,
                wxo_ref, g2_ref, b2_ref, wrh_ref, wrl_ref, br_ref,
                h2_ref, bk_ref, oc_ref, *, alpha):
    tm = x_ref.shape[0]
    mix = _dot(of_ref[...], wo_ref[:FOX_W, :]) + _dot(os_ref[...], wo_ref[FOX_W:, :])
    h1 = _layer_norm(alpha * x_ref[...] + mix, g1_ref[...], b1_ref[...])

    q = (_dot(h1.astype(bf16), wq_ref[...]) * 0.0625).astype(bf16)
    for h in range(N_XH):
        sl = slice(h * XHD, (h + 1) * XHD)
        s = _dot_nt(q[:, sl], k_ref[:, sl])
        e = jnp.exp(s - jnp.max(s, axis=1, keepdims=True))
        p = e / jnp.sum(e, axis=1, keepdims=True)
        oc_ref[:, sl] = _dot(p.astype(bf16), v_ref[:, sl]).astype(bf16)
    xo = _dot(oc_ref[...], wxo_ref[...])
    h2 = _layer_norm(alpha * h1 + xo, g2_ref[...], b2_ref[...])
    h2_ref[:, :D] = h2

    hh = h2.astype(bf16)
    hl = (h2 - hh.astype(f32)).astype(bf16)
    lg = _dot(hh, wrh_ref[...]) + _dot(hl, wrh_ref[...]) + _dot(hh, wrl_ref[...]) + br_ref[...]

    lane = lax.broadcasted_iota(i32, (tm, LANES), 1).astype(f32)
    big = float(LANES)

    def first_max(vals, mask):
        vm = jnp.where(mask, vals, NEG)
        top = jnp.max(vm, axis=1, keepdims=True)
        idx = jnp.min(jnp.where(mask & (vm == top), lane, big), axis=1, keepdims=True)
        return top, idx

    gmask = lane < float(N_GROUPS)
    gmax, gidx = first_max(lg, gmask)
    g_val = 1.0 / jnp.sum(jnp.where(gmask, jnp.exp(lg - gmax), 0.0), axis=1, keepdims=True)
    lo = float(N_GROUPS) + float(EPG) * gidx
    emask = (lane >= lo) & (lane < lo + float(EPG))
    v1, i1 = first_max(lg, emask)
    v2, i2 = first_max(lg, emask & (lane != i1))
    ex = jnp.exp(v2 - v1)
    w1 = g_val * (1.0 / (1.0 + ex))
    w2 = g_val * (ex / (1.0 + ex))
    e1 = i1 - lo
    e2 = i2 - lo
    first_low = e1 < e2
    ea = jnp.where(first_low, e1, e2)
    eb = jnp.where(first_low, e2, e1)
    ga = jnp.where(first_low, w1, w2)
    gb = jnp.where(first_low, w2, w1)
    pidx = ea * float(EPG - 1) - ea * (ea - 1.0) * 0.5 + (eb - ea - 1.0)
    bucket = gidx * float(N_PAIRS) + pidx

    lane_i = lax.broadcasted_iota(i32, (tm, LANES), 1)
    h2_ref[:, D:] = jnp.where(lane_i == 0, ga, jnp.where(lane_i == 1, gb, 0.0))
    bk_t = jnp.transpose(jnp.broadcast_to(bucket, (tm, LANES)))
    bk_ref[...] = bk_t[:8, :]


def _mid(x2, of2, os2, w_out, g1, b1, wq, kx, vx, wxo, g2, b2, wrh, wrl, br, alpha, S, tm=512):
    T = x2.shape[0]
    M = kx.shape[0] // (T // S)
    per_b = S // tm
    row = lambda w: pl.BlockSpec((tm, w), lambda i: (i, 0))
    full = lambda a: pl.BlockSpec(a.shape, lambda i: (0,) * a.ndim)
    kvspec = pl.BlockSpec((M, D), lambda i: (i // per_b, 0))
    kernel = functools.partial(_mid_kernel, alpha=alpha)
    return pl.pallas_call(
        kernel,
        grid=(T // tm,),
        in_specs=[row(D), row(512), row(512), full(w_out), full(g1), full(b1), full(wq),
                  kvspec, kvspec, full(wxo), full(g2), full(b2), full(wrh), full(wrl), full(br)],
        out_specs=[row(XW), pl.BlockSpec((8, tm), lambda i: (0, i))],
        out_shape=[jax.ShapeDtypeStruct((T, XW), f32), jax.ShapeDtypeStruct((8, T), f32)],
        scratch_shapes=[pltpu.VMEM((tm, D), bf16)],
        compiler_params=_cparams(("parallel",)),
        name="mid",
    )(x2, of2, os2, w_out, g1, b1, wq, kx, vx, wxo, g2, b2, wrh, wrl, br)


def _rank_kernel(bk_ref, rank_ref, cnt_ref, carry_ref, *, chunk):
    sub = 256

    @pl.when(pl.program_id(0) == 0)
    def _():
        carry_ref[...] = jnp.zeros_like(carry_ref)

    r = lax.broadcasted_iota(i32, (sub, sub), 0)
    c = lax.broadcasted_iota(i32, (sub, sub), 1)
    before = (r < c).astype(bf16)
    bid = lax.broadcasted_iota(i32, (LANES, sub), 0).astype(f32)
    carry = carry_ref[...]
    for j in range(chunk // sub):
        bk = bk_ref[0:1, j * sub:(j + 1) * sub]
        hit = bid == bk
        oh = jnp.where(hit, 1.0, 0.0)
        prior = _dot(oh.astype(bf16), before) + carry
        rank_ref[:, j * sub:(j + 1) * sub] = jnp.sum(jnp.where(hit, prior, 0.0), axis=0, keepdims=True)
        carry = carry + jnp.sum(oh, axis=1, keepdims=True)
    carry_ref[...] = carry
    cnt_ref[...] = carry


def _rank(bk8, chunk=2048):
    T = bk8.shape[1]
    kernel = functools.partial(_rank_kernel, chunk=chunk)
    return pl.pallas_call(
        kernel, grid=(T // chunk,),
        in_specs=[pl.BlockSpec((8, chunk), lambda i: (0, i))],
        out_specs=[pl.BlockSpec((1, chunk), lambda i: (0, i)),
                   pl.BlockSpec((LANES, 1), lambda i: (0, 0))],
        out_shape=[jax.ShapeDtypeStruct((1, T), f32), jax.ShapeDtypeStruct((LANES, 1), f32)],
        scratch_shapes=[pltpu.VMEM((LANES, 1), f32)],
        compiler_params=_cparams(("arbitrary",)), name="rank",
    )(bk8)


def _dest_kernel(bk_ref, rank_ref, ps_ref, dest_ref):
    chunk = bk_ref.shape[1]
    bid = lax.broadcasted_iota(i32, (LANES, chunk), 0).astype(f32)
    start = jnp.sum(jnp.where(bid == bk_ref[0:1, :], ps_ref[...], 0.0), axis=0, keepdims=True)
    dest_ref[...] = (start + rank_ref[...]).astype(i32)


def _dest(bk8, rank, ps_col, chunk=2048):
    T = bk8.shape[1]
    return pl.pallas_call(
        _dest_kernel, grid=(T // chunk,),
        in_specs=[pl.BlockSpec((8, chunk), lambda i: (0, i)), pl.BlockSpec((1, chunk), lambda i: (0, i)),
                  pl.BlockSpec((LANES, 1), lambda i: (0, 0))],
        out_specs=pl.BlockSpec((1, chunk), lambda i: (0, i)),
        out_shape=jax.ShapeDtypeStruct((1, T), i32),
        compiler_params=_cparams(("parallel",)), name="dest",
    )(bk8, rank, ps_col)


def _row_copy(src, dst, sem, s, d):
    return pltpu.make_async_copy(src.at[pl.ds(s, 1)], dst.at[pl.ds(d, 1)], sem)


def _scatter_kernel(bk_ref, rk_ref, ps_ref, pe_ref, src_ref, dst_ref, zero_ref, sem, zsem, *, chunk):
    @pl.when(pl.program_id(0) == 0)
    def _():
        zero_ref[...] = jnp.zeros_like(zero_ref)

        def tail_copy(b):
            start = pl.multiple_of(pe_ref[b] - ROW_BLOCK, ROW_BLOCK)
            return pltpu.make_async_copy(zero_ref, dst_ref.at[pl.ds(start, ROW_BLOCK)], zsem)

        def fill(b, _):
            @pl.when(pe_ref[b] > ps_ref[b])
            def _():
                tail_copy(b).start()
            return 0

        def drain(b, _):
            @pl.when(pe_ref[b] > ps_ref[b])
            def _():
                tail_copy(b).wait()
            return 0

        lax.fori_loop(0, N_BUCKETS, fill, 0)
        lax.fori_loop(0, N_BUCKETS, drain, 0)

        def unused_copy(n):
            return pltpu.make_async_copy(zero_ref, dst_ref.at[pl.ds(pl.multiple_of(n * ROW_BLOCK, ROW_BLOCK),
                                                                    ROW_BLOCK)], zsem)

        first_unused = pe_ref[LANES - 1] // ROW_BLOCK
        n_blocks = dst_ref.shape[0] // ROW_BLOCK
        lax.fori_loop(first_unused, n_blocks, lambda n, _: (unused_copy(n).start(), 0)[1], 0)
        lax.fori_loop(first_unused, n_blocks, lambda n, _: (unused_copy(n).wait(), 0)[1], 0)

    def issue(u, _):
        for k in range(2):
            t = 2 * u + k
            d = ps_ref[bk_ref[t]] + rk_ref[t]
            _row_copy(src_ref, dst_ref, sem, t, d).start(priority=k)
        return 0

    lax.fori_loop(0, chunk // 2, issue, 0)
    pltpu.make_async_copy(src_ref, dst_ref.at[pl.ds(0, chunk)], sem).wait()


def _scatter_rows(bk, rk, ps, pe, src, P, chunk=1024):
    T, w = src.shape
    sm = lambda: pl.BlockSpec((chunk,), lambda i: (i,), memory_space=pltpu.SMEM)
    whole = lambda: pl.BlockSpec(memory_space=pltpu.SMEM)
    kernel = functools.partial(_scatter_kernel, chunk=chunk)
    return pl.pallas_call(
        kernel, grid=(T // chunk,),
        in_specs=[sm(), sm(), whole(), whole(), pl.BlockSpec((chunk, w), lambda i: (i, 0))],
        out_specs=pl.BlockSpec(memory_space=pl.ANY),
        out_shape=jax.ShapeDtypeStruct((P, w), src.dtype),
        scratch_shapes=[pltpu.VMEM((ROW_BLOCK, w), src.dtype), pltpu.SemaphoreType.DMA(()),
                        pltpu.SemaphoreType.DMA(())],
        compiler_params=pltpu.CompilerParams(dimension_semantics=("arbitrary",), has_side_effects=True,
                                             vmem_limit_bytes=VMEM_LIMIT),
        name="scatter_rows",
    )(bk, rk, ps, pe, src)


def _sc_gather_rows(idx, src, chunk=32):
    n = idx.shape[0]
    w = src.shape[1]
    workers = SC_CORES * SC_SUBCORES
    per_worker = n // workers
    mesh = plsc.VectorSubcoreMesh(core_axis_name="core", subcore_axis_name="subcore",
                                  num_cores=SC_CORES, num_subcores=SC_SUBCORES)

    @functools.partial(pl.kernel, out_type=jax.ShapeDtypeStruct((n, w), src.dtype), mesh=mesh,
                       scratch_types=[pltpu.VMEM((chunk,), i32), pltpu.VMEM((chunk, w), src.dtype),
                                      pltpu.SemaphoreType.DMA],
                       name="sc_gather_rows")
    def k(src_hbm, idx_hbm, out_hbm, idx_v, rows_v, sem):
        wid = lax.axis_index("subcore") * SC_CORES + lax.axis_index("core")

        @pl.loop(0, per_worker // chunk)
        def _(c):
            base = wid * per_worker + c * chunk
            pltpu.sync_copy(idx_hbm.at[pl.ds(base, chunk)], idx_v)
            pltpu.async_copy(src_hbm.at[idx_v], rows_v, sem).wait()
            pltpu.sync_copy(rows_v, out_hbm.at[pl.ds(base, chunk)])

    return k(src, idx)


def _expert_kernel(ea_ref, eb_ref, used_ref, xs_ref, wga_ref, wua_ref, wda_ref,
                   wgb_ref, wub_ref, wdb_ref, y_ref):
    n = pl.program_id(0)

    @pl.when(n < used_ref[0])
    def _():
        x = xs_ref[:, :D].astype(bf16)

        def expert(wg, wu, wd):
            a = _dot(x, wg[0])
            u = _dot(x, wu[0])
            act = a * (1.0 / (1.0 + jnp.exp(-a))) * u
            return _dot(act.astype(bf16), wd[0])

        ga = xs_ref[:, D:D + 1]
        gb = xs_ref[:, D + 1:D + 2]
        y_ref[...] = ga * expert(wga_ref, wua_ref, wda_ref) + gb * expert(wgb_ref, wub_ref, wdb_ref)

    @pl.when(n >= used_ref[0])
    def _():
        y_ref[...] = jnp.zeros_like(y_ref)


def _experts(ea, eb, used, xs, wg, wu, wd):
    P = xs.shape[0]
    nblk = P // ROW_BLOCK

    def xmap(n, ea, eb, used):
        return (jnp.minimum(n, used[0] - 1), 0)

    wa = lambda n, ea, eb, used: (ea[n], 0, 0)
    wb = lambda n, ea, eb, used: (eb[n], 0, 0)
    wgs = lambda m: pl.BlockSpec((1, D, D_EXPERT), m)
    wds = lambda m: pl.BlockSpec((1, D_EXPERT, D), m)
    grid_spec = pltpu.PrefetchScalarGridSpec(
        num_scalar_prefetch=3, grid=(nblk,),
        in_specs=[pl.BlockSpec((ROW_BLOCK, XW), xmap), wgs(wa), wgs(wa), wds(wa), wgs(wb), wgs(wb), wds(wb)],
        out_specs=pl.BlockSpec((ROW_BLOCK, D), lambda n, ea, eb, used: (n, 0)),
    )
    return pl.pallas_call(
        _expert_kernel, grid_spec=grid_spec,
        out_shape=jax.ShapeDtypeStruct((P, D), f32),
        compiler_params=_cparams(("arbitrary",)), name="experts",
    )(ea, eb, used, xs, wg, wu, wd, wg, wu, wd)


def _final_kernel(bk0_ref, rk0_ref, bkn_ref, rkn_ref, ps_ref, h_ref, ys_ref, g_ref, b_ref, o_ref,
                  buf, sems, *, alpha, tm):
    i = pl.program_id(0)
    n = pl.num_programs(0)

    def issue(bk_ref, rk_ref, slot):
        def body(u, _):
            for k in range(2):
                t = 2 * u + k
                s = ps_ref[bk_ref[t]] + rk_ref[t]
                pltpu.make_async_copy(ys_ref.at[pl.ds(s, 1)], buf.at[slot, pl.ds(t, 1)],
                                      sems.at[slot]).start(priority=k)
            return 0
        lax.fori_loop(0, tm // 2, body, 0)

    @pl.when(i == 0)
    def _():
        issue(bk0_ref, rk0_ref, 0)

    @pl.when(i + 1 < n)
    def _():
        issue(bkn_ref, rkn_ref, (i + 1) % 2)

    slot = i % 2
    pltpu.make_async_copy(ys_ref.at[pl.ds(0, tm)], buf.at[slot], sems.at[slot]).wait()
    o_ref[...] = _layer_norm(alpha * h_ref[...] + buf[slot], g_ref[...], b_ref[...])


def _final(bk, rk, ps, h2x, ys, g, b, alpha, tm=1024):
    T = h2x.shape[0]
    n = T // tm
    row = pl.BlockSpec((tm, D), lambda i: (i, 0))
    vec = pl.BlockSpec((1, D), lambda i: (0, 0))
    first = lambda: pl.BlockSpec((tm,), lambda i: (0,), memory_space=pltpu.SMEM)
    nxt = lambda: pl.BlockSpec((tm,), lambda i: (jnp.minimum(i + 1, n - 1),), memory_space=pltpu.SMEM)
    return pl.pallas_call(
        functools.partial(_final_kernel, alpha=alpha, tm=tm), grid=(n,),
        in_specs=[first(), first(), nxt(), nxt(), pl.BlockSpec(memory_space=pltpu.SMEM),
                  row, pl.BlockSpec(memory_space=pl.ANY), vec, vec],
        out_specs=row,
        out_shape=jax.ShapeDtypeStruct((T, D), f32),
        scratch_shapes=[pltpu.VMEM((2, tm, D), f32), pltpu.SemaphoreType.DMA((2,))],
        compiler_params=_cparams(("arbitrary",)), name="final_ln",
    )(bk, rk, bk, rk, ps, h2x, ys, g, b)


def _ln_kernel(h_ref, y_ref, g_ref, b_ref, o_ref, *, alpha):
    o_ref[...] = _layer_norm(alpha * h_ref[...] + y_ref[...], g_ref[...], b_ref[...])


def _final_ln(h2x, y, g, b, alpha, tm=512):
    T = y.shape[0]
    row = pl.BlockSpec((tm, D), lambda i: (i, 0))
    vec = pl.BlockSpec((1, D), lambda i: (0, 0))
    return pl.pallas_call(
        functools.partial(_ln_kernel, alpha=alpha), grid=(T // tm,),
        in_specs=[row, row, vec, vec], out_specs=row,
        out_shape=jax.ShapeDtypeStruct((T, D), f32),
        compiler_params=_cparams(("parallel",)), name="final_ln",
    )(h2x, y, g, b)


def _pair_tables():
    ea = np.zeros((LANES,), np.int32)
    eb = np.zeros((LANES,), np.int32)
    for g in range(N_GROUPS):
        k = 0
        for a in range(EPG):
            for b in range(a + 1, EPG):
                ea[g * N_PAIRS + k] = g * EPG + a
                eb[g * N_PAIRS + k] = g * EPG + b
                k += 1
    return ea, eb


_PAIR_A, _PAIR_B = _pair_tables()


def _layer(h, mem, positions, w_in, b_forget, sinks, w_mix_out, ln_mix_g, ln_mix_b,
           w_xq, w_xkv, w_xout, ln_x_g, ln_x_b, w_rg, b_rg, w_re, b_re,
           w_eg, w_eu, w_ed, ln_f_g, ln_f_b, alpha):
    B, S, _ = h.shape
    T = B * S
    x2 = h.reshape(T, D)
    pos2 = positions.reshape(T, 1).astype(i32)

    o = np.cumsum((0, FOX_W, FOX_W, FOX_W, N_FOX, SWA_Q_W, SWA_KV_W, SWA_KV_W))
    w_qf, w_kf, w_vf, w_fl, w_qs, w_ks, w_vs = (w_in[:, o[i]:o[i + 1]] for i in range(7))
    def regroup(a, axis):
        shp = a.shape
        a = jnp.moveaxis(a, axis, 0).reshape(N_SWA_KV, N_SWA // N_SWA_KV, HD, -1)
        return jnp.moveaxis(jnp.swapaxes(a, 0, 1).reshape(N_SWA * HD, -1), 0, axis).reshape(shp)

    w_all = jnp.concatenate([w_kf, regroup(w_qs, 1), w_ks, w_vs], axis=1).astype(bf16)
    wqt = w_qf.T.astype(bf16)
    wvt = w_vf.T.astype(bf16)
    wfl = w_fl.T.astype(bf16)
    bfc = b_forget.reshape(N_FOX, 1).astype(f32)
    half = HD // 2
    inv_freq = ROPE_THETA ** (-jnp.arange(half, dtype=f32) / half)
    invf = jnp.tile(inv_freq, LANES // half).reshape(1, LANES)
    w_out = jnp.concatenate([w_mix_out[:FOX_W], regroup(w_mix_out[FOX_W:], 0)], axis=0).astype(bf16)

    tq = 512
    qt, kf, vt, qs, ks, vs, lf = _in_proj(x2, pos2, w_all, wqt, wvt, wfl, bfc, invf, tq)
    c, ca = _cumsum(lf, S)
    c4 = c.reshape(N_FOX // 2, 2, T // tq, tq)
    r3 = lambda a: a.reshape(B, S, a.shape[-1])
    o_fox = _fox(qt, r3(kf), r3(ca), vt, c4, B, S, tq)
    o_swa = _swa(sinks.astype(f32), r3(qs), r3(ks), r3(vs), B, S)

    kx, vx = _kvproj(mem.reshape(-1, D), w_xkv.astype(bf16))

    wr = jnp.concatenate([w_rg, jnp.transpose(w_re, (1, 0, 2)).reshape(D, N_EXPERTS)], axis=1)
    wr = jnp.pad(wr, ((0, 0), (0, LANES - wr.shape[1]))).astype(f32)
    wrh = wr.astype(bf16)
    wrl = (wr - wrh.astype(f32)).astype(bf16)
    br = jnp.pad(jnp.concatenate([b_rg, b_re.reshape(-1)]), (0, LANES - N_GROUPS - N_EXPERTS))
    br = br.reshape(1, LANES).astype(f32)
    v2 = lambda a: a.reshape(1, D).astype(f32)
    h2x, bk8 = _mid(x2, o_fox.reshape(T, FOX_W), o_swa.reshape(T, SWA_Q_W), w_out,
                    v2(ln_mix_g), v2(ln_mix_b), w_xq.astype(bf16), kx, vx, w_xout.astype(bf16),
                    v2(ln_x_g), v2(ln_x_b), wrh, wrl, br, alpha, S)

    rank, cnt = _rank(bk8)
    bucket_i = bk8[0].astype(i32)
    rank_i = rank[0].astype(i32)
    counts = cnt[:, 0].astype(i32)
    padded = ((counts + ROW_BLOCK - 1) // ROW_BLOCK) * ROW_BLOCK
    pad_end = jnp.cumsum(padded)
    pad_start = (pad_end - padded).astype(i32)
    P = T + N_BUCKETS * ROW_BLOCK
    nblk = P // ROW_BLOCK
    used = (pad_end[-1] // ROW_BLOCK).astype(i32).reshape(1)
    blk_row = jnp.arange(nblk, dtype=i32)[:, None] * ROW_BLOCK
    blk_bucket = jnp.minimum(jnp.sum((pad_end[None, :] <= blk_row).astype(i32), axis=1), N_BUCKETS - 1)
    pick = (blk_bucket[:, None] == jnp.arange(LANES, dtype=i32)[None, :]).astype(i32)
    blk_a = jnp.sum(pick * jnp.asarray(_PAIR_A)[None, :], axis=1)
    blk_b = jnp.sum(pick * jnp.asarray(_PAIR_B)[None, :], axis=1)

    dest = _dest(bk8, rank, pad_start.astype(f32).reshape(LANES, 1))[0]
    xs = _scatter_rows(bucket_i, rank_i, pad_start, pad_end.astype(i32), h2x, P)
    ys = _experts(blk_a, blk_b, used, xs, w_eg.astype(bf16), w_eu.astype(bf16), w_ed.astype(bf16))
    y = _sc_gather_rows(dest, ys)
    out = _final_ln(h2x, y, v2(ln_f_g), v2(ln_f_b), alpha)
    return out.reshape(B, S, D)


def kernel(x, mem, positions, w_in, b_forget, sinks, w_mix_out, ln_mix_g, ln_mix_b, w_xq, w_xkv, w_xout,
           ln_x_g, ln_x_b, w_route_group, b_route_group, w_route_expert, b_route_expert,
           w_exp_gate, w_exp_up, w_exp_down, ln_ffn_g, ln_ffn_b):
    depth = w_in.shape[0]
    alpha = (2.0 * depth) ** 0.25
    h = x
    for l in range(depth):
        h = _layer(h, mem, positions, w_in[l], b_forget[l], sinks[l], w_mix_out[l], ln_mix_g[l], ln_mix_b[l],
                   w_xq[l], w_xkv[l], w_xout[l], ln_x_g[l], ln_x_b[l], w_route_group[l], b_route_group[l],
                   w_route_expert[l], b_route_expert[l], w_exp_gate[l], w_exp_up[l], w_exp_down[l],
                   ln_ffn_g[l], ln_ffn_b[l], alpha)
    return h
```

```python
import functools

import jax
import jax.numpy as jnp
import numpy as np
from jax import lax
from jax.experimental import pallas as pl
from jax.experimental.pallas import tpu as pltpu
from jax.experimental.pallas import tpu_sc as plsc

f32 = jnp.float32
bf16 = jnp.bfloat16
i32 = jnp.int32

D = 1024
HD = 64
N_FOX = 8
N_SWA = 8
N_SWA_KV = 2
FOX_W = 512
SWA_Q_W = 512
SWA_KV_W = 128
WINDOW = 128
ROPE_THETA = 10000.0
N_XH = 4
XHD = 256
N_GROUPS = 4
EPG = 8
N_EXPERTS = 32
D_EXPERT = 512
LN_EPS = 1e-5
NEG = -1e30
LOG2E = 1.4426950408889634
L_ROW = (HD, 0)

SC_CORES = 2
SC_SUBCORES = 16
SC_LANES = 16
LANES = 128
ROW_BLOCK = 128
N_PAIRS = EPG * (EPG - 1) // 2
N_BUCKETS = N_GROUPS * N_PAIRS
XW = D + LANES
VMEM_LIMIT = 56 * 1024 * 1024


def _cparams(sem):
    return pltpu.CompilerParams(dimension_semantics=sem, vmem_limit_bytes=VMEM_LIMIT)


def _layer_norm(v, g, b):
    mu = jnp.mean(v, axis=-1, keepdims=True)
    c = v - mu
    var = jnp.mean(c * c, axis=-1, keepdims=True)
    return c * lax.rsqrt(var + LN_EPS) * g + b


def _dot(a, b):
    return jnp.dot(a, b, preferred_element_type=f32)


def _dot_nt(a, b):
    return lax.dot_general(a, b, (((1,), (1,)), ((), ())), preferred_element_type=f32)


def _inproj_kernel(x_ref, pos_ref, w_ref, wqt_ref, wvt_ref, wfl_ref, bf_ref, invf_ref,
                   qt_ref, kf_ref, vt_ref, qs_ref, ks_ref, vs_ref, lf_ref):
    tm = x_ref.shape[0]
    xb = x_ref[...].astype(bf16)

    def proj(lo, hi):
        return _dot(xb, w_ref[:, lo:hi])

    qt_ref[0] = (_dot_nt(wqt_ref[...], xb) * (0.125 * LOG2E)).astype(bf16)
    vt_ref[0] = _dot_nt(wvt_ref[...], xb).astype(bf16)
    kf_ref[...] = proj(0, 512).astype(bf16)

    ang = pos_ref[...].astype(f32) * invf_ref[...]
    cos = jnp.cos(ang)
    sin = jnp.sin(ang)
    lane = lax.broadcasted_iota(i32, (tm, LANES), 1)
    lo_half = (lane % HD) < (HD // 2)
    sin_s = jnp.where(lo_half, -sin, sin)

    def rope(z):
        rot = jnp.where(lo_half, pltpu.roll(z, LANES - HD // 2, 1), pltpu.roll(z, HD // 2, 1))
        return z * cos + rot * sin_s

    zq = proj(512, 1024)
    for g in range(4):
        sl = slice(g * LANES, (g + 1) * LANES)
        qs_ref[:, sl] = (rope(zq[:, sl]) * 0.125).astype(bf16)
    ks_ref[...] = rope(proj(1024, 1152)).astype(bf16)
    vs_ref[...] = proj(1152, 1280).astype(bf16)

    z = _dot_nt(wfl_ref[...], xb) + bf_ref[...]
    lf_ref[...] = jnp.minimum(z, 0.0) - jnp.log(1.0 + jnp.exp(-jnp.abs(z)))


def _in_proj(x2, pos2, w_all, wqt, wvt, wfl, bfc, invf, tm):
    T = x2.shape[0]
    row = lambda w: pl.BlockSpec((tm, w), lambda i: (i, 0))
    full = lambda a: pl.BlockSpec(a.shape, lambda i: (0,) * a.ndim)
    fmaj = pl.BlockSpec((1, FOX_W, tm), lambda i: (i, 0, 0))
    return pl.pallas_call(
        _inproj_kernel,
        grid=(T // tm,),
        in_specs=[row(D), row(1), full(w_all), full(wqt), full(wvt), full(wfl), full(bfc), full(invf)],
        out_specs=[fmaj, row(512), fmaj, row(512), row(128), row(128),
                   pl.BlockSpec((N_FOX, tm), lambda i: (0, i))],
        out_shape=[jax.ShapeDtypeStruct((T // tm, FOX_W, tm), bf16), jax.ShapeDtypeStruct((T, 512), bf16),
                   jax.ShapeDtypeStruct((T // tm, FOX_W, tm), bf16), jax.ShapeDtypeStruct((T, 512), bf16),
                   jax.ShapeDtypeStruct((T, 128), bf16), jax.ShapeDtypeStruct((T, 128), bf16),
                   jax.ShapeDtypeStruct((N_FOX, T), f32)],
        compiler_params=_cparams(("parallel",)),
        name="in_proj",
    )(x2, pos2, w_all, wqt, wvt, wfl, bfc, invf)


def _cumsum_kernel(lf_ref, c_ref, ca_ref):
    S = lf_ref.shape[1]
    ch = 256
    r = lax.broadcasted_iota(i32, (ch, ch), 0)
    c = lax.broadcasted_iota(i32, (ch, ch), 1)
    tri = (r <= c).astype(f32)
    eye = (r == c).astype(bf16)
    carry = jnp.zeros((N_FOX, 1), f32)
    for j in range(S // ch):
        a = lf_ref[:, j * ch:(j + 1) * ch]
        cc = jnp.dot(a, tri, precision=lax.Precision.HIGHEST, preferred_element_type=f32) + carry
        carry = cc[:, ch - 1:ch]
        c2 = cc * LOG2E
        c_ref[:, j * ch:(j + 1) * ch] = c2
        neg = -c2
        hi = neg.astype(bf16)
        r1 = neg - hi.astype(f32)
        mid = r1.astype(bf16)
        lo = (r1 - mid.astype(f32)).astype(bf16)
        terms = jnp.concatenate([hi, mid, lo, jnp.zeros((LANES - 3 * N_FOX, ch), bf16)], axis=0)
        ca_ref[j * ch:(j + 1) * ch, :] = _dot_nt(eye, terms).astype(bf16)


def _cumsum(lf, S):
    T = lf.shape[1]
    spec = pl.BlockSpec((N_FOX, S), lambda b: (0, b))
    return pl.pallas_call(
        _cumsum_kernel, grid=(T // S,), in_specs=[spec],
        out_specs=[spec, pl.BlockSpec((S, LANES), lambda b: (b, 0))],
        out_shape=[jax.ShapeDtypeStruct((N_FOX, T), f32), jax.ShapeDtypeStruct((T, LANES), bf16)],
        compiler_params=_cparams(("parallel",)), name="cumsum",
    )(lf)


def _fox_kernel(qt_ref, k_ref, ca_ref, vt_ref, c_ref, o_ref, t0_ref, t1_ref, *, tq):
    hp = pl.program_id(1)
    i = pl.program_id(2)
    qt = qt_ref[0]
    row = lax.broadcasted_iota(i32, (LANES, tq), 0)
    is_a = row < HD
    zero = jnp.zeros_like(qt)
    q_ops = []
    for h in range(2):
        ones = jnp.where(((row & 7) == 2 * hp + h) & (row < 3 * N_FOX), 1.0, 0.0).astype(bf16)
        qh = jnp.where(is_a, qt, zero) if h == 0 else jnp.where(is_a, zero, qt)
        q_ops.append(jnp.concatenate([qh, ones], axis=0))
    kr = lax.broadcasted_iota(i32, (tq, tq), 0)
    qc = lax.broadcasted_iota(i32, (tq, tq), 1)
    causal = kr <= qc
    cq = [c_ref[0, h, pl.ds(i, 1), :] for h in range(2)]

    def scores(j, t_ref):
        off = pl.multiple_of(j * tq, tq)
        kblk = jnp.concatenate([k_ref[0, pl.ds(off, tq), :], ca_ref[0, pl.ds(off, tq), :]], axis=1)
        for h in range(2):
            t_ref[h] = _dot(kblk, q_ops[h])

    keep = [jnp.where(is_a, 1.0, 0.0).astype(bf16), jnp.where(is_a, 0.0, 1.0).astype(bf16)]
    ones_row = [jnp.where(row == L_ROW[h], 1.0, 0.0).astype(bf16) for h in range(2)]

    def softmax_pv(j, t_ref, carry, masked):
        vt = vt_ref[j]
        vts = [vt * keep[h] + ones_row[h] for h in range(2)]
        new = []
        for h in range(2):
            m, acc = carry[h]
            t = t_ref[h]
            if masked:
                t = jnp.where(causal, t, NEG)
            m_new = jnp.maximum(m, jnp.max(t, axis=0, keepdims=True) + cq[h])
            alpha = jnp.exp2(m - m_new)
            p = jnp.exp2(t + (cq[h] - m_new))
            acc = alpha * acc + _dot(vts[h], p.astype(bf16))
            new.append((m_new, acc))
        return tuple(new)

    def pair(k, carry):
        j = 2 * k
        scores(j + 1, t1_ref)
        carry = softmax_pv(j, t0_ref, carry, False)
        scores(j + 2, t0_ref)
        return softmax_pv(j + 1, t1_ref, carry, False)

    def odd_tail(carry):
        scores(i, t1_ref)
        carry = softmax_pv(i - 1, t0_ref, carry, False)
        return softmax_pv(i, t1_ref, carry, True)

    def even_tail(carry):
        return softmax_pv(i, t0_ref, carry, True)

    init = tuple((jnp.full((1, tq), NEG, f32), jnp.zeros((LANES, tq), f32)) for _ in range(2))
    scores(0, t0_ref)
    carry = lax.fori_loop(0, i // 2, pair, init)
    (_, acca), (_, accb) = lax.cond(i % 2 == 1, odd_tail, even_tail, carry)
    la = acca[L_ROW[0]:L_ROW[0] + 1, :]
    lb = accb[L_ROW[1]:L_ROW[1] + 1, :]
    ot = jnp.where(is_a, acca / la, accb / lb)
    o_ref[0] = jnp.transpose(ot).astype(bf16)


def _fox(qt, kf, ca, vt, c4, B, S, tq):
    nq = S // tq
    kernel = functools.partial(_fox_kernel, tq=tq)
    return pl.pallas_call(
        kernel,
        grid=(B, N_FOX // 2, nq),
        in_specs=[
            pl.BlockSpec((1, LANES, tq), lambda b, hp, i: (b * nq + i, hp, 0)),
            pl.BlockSpec((1, S, LANES), lambda b, hp, i: (b, 0, hp)),
            pl.BlockSpec((1, S, LANES), lambda b, hp, i: (b, 0, 0)),
            pl.BlockSpec((nq, LANES, tq), lambda b, hp, i: (b, hp, 0)),
            pl.BlockSpec((1, 2, nq, tq), lambda b, hp, i: (hp, 0, b, 0)),
        ],
        out_specs=pl.BlockSpec((1, tq, LANES), lambda b, hp, i: (b, i, hp)),
        out_shape=jax.ShapeDtypeStruct((B, S, FOX_W), bf16),
        scratch_shapes=[pltpu.VMEM((2, tq, tq), f32), pltpu.VMEM((2, tq, tq), f32)],
        compiler_params=_cparams(("parallel", "parallel", "arbitrary")),
        name="fox",
    )(qt, kf, ca, vt, c4)


def _swa_kernel(sink_ref, q_ref, k_ref, v_ref, o_ref, *, tq):
    W = WINDOW
    nsub = tq // W
    n0 = pl.program_id(1) * nsub
    lane = lax.broadcasted_iota(i32, (W, LANES), 1)
    is0 = lane < HD
    rows = lax.broadcasted_iota(i32, (4 * W, 2 * W), 0)
    cols = lax.broadcasted_iota(i32, (4 * W, 2 * W), 1)
    rgrp = lax.broadcasted_iota(i32, (4 * W, 1), 0) // W
    for r in range(nsub):
        nb = n0 + r
        kstart = pl.multiple_of(jnp.maximum(nb * W - W, 0), W)
        ks = k_ref[0, pl.ds(kstart, 2 * W), :]
        vs = v_ref[0, pl.ds(kstart, 2 * W), :]
        qpos = nb * W + rows % W
        kpos = kstart + cols
        valid = (kpos <= qpos) & (qpos - kpos < W)
        outs = []
        for kv in range(2):
            keep = is0 if kv == 0 else jnp.logical_not(is0)
            parts = []
            for g in range(4):
                qg = q_ref[0, r * W:(r + 1) * W, g * LANES:(g + 1) * LANES]
                parts.append(jnp.where(keep, qg, jnp.zeros_like(qg)))
            qstack = jnp.concatenate(parts, axis=0)
            s = jnp.where(valid, _dot_nt(qstack, ks), NEG)
            sink = jnp.zeros((4 * W, 1), f32)
            for g in range(4):
                sink = jnp.where(rgrp == g, sink_ref[kv * 4 + g], sink)
            m = jnp.maximum(jnp.max(s, axis=1, keepdims=True), sink)
            e = jnp.exp(s - m)
            den = jnp.sum(e, axis=1, keepdims=True) + jnp.exp(sink - m)
            outs.append(_dot(e.astype(bf16), vs) / den)
        for g in range(4):
            og = jnp.where(is0, outs[0][g * W:(g + 1) * W], outs[1][g * W:(g + 1) * W])
            o_ref[0, r * W:(r + 1) * W, g * LANES:(g + 1) * LANES] = og.astype(bf16)


def _swa(sinks, qs, ks, vs, B, S, tq=512):
    kernel = functools.partial(_swa_kernel, tq=tq)
    return pl.pallas_call(
        kernel,
        grid=(B, S // tq),
        in_specs=[
            pl.BlockSpec(memory_space=pltpu.SMEM),
            pl.BlockSpec((1, tq, SWA_Q_W), lambda b, i: (b, i, 0)),
            pl.BlockSpec((1, S, SWA_KV_W), lambda b, i: (b, 0, 0)),
            pl.BlockSpec((1, S, SWA_KV_W), lambda b, i: (b, 0, 0)),
        ],
        out_specs=pl.BlockSpec((1, tq, SWA_Q_W), lambda b, i: (b, i, 0)),
        out_shape=jax.ShapeDtypeStruct((B, S, SWA_Q_W), bf16),
        compiler_params=_cparams(("parallel", "arbitrary")),
        name="swa",
    )(sinks, qs, ks, vs)


def _kvproj_kernel(m_ref, w_ref, k_ref, v_ref):
    mb = m_ref[...].astype(bf16)
    k_ref[...] = _dot(mb, w_ref[:, :D]).astype(bf16)
    v_ref[...] = _dot(mb, w_ref[:, D:]).astype(bf16)


def _kvproj(mem2, w_xkv, tm=512):
    R = mem2.shape[0]
    row = pl.BlockSpec((tm, D), lambda i: (i, 0))
    return pl.pallas_call(
        _kvproj_kernel, grid=(R // tm,),
        in_specs=[row, pl.BlockSpec(w_xkv.shape, lambda i: (0, 0))],
        out_specs=[row, row],
        out_shape=[jax.ShapeDtypeStruct((R, D), bf16)] * 2,
        compiler_params=_cparams(("parallel",)), name="kvproj",
    )(mem2, w_xkv)


def _mid_kernel(x_ref, of_ref, os_ref, wo_ref, g1_ref, b1_ref, wq_ref, k_ref, v_ref,
                wxo_ref, g2_ref, b2_ref, wrh_ref, wrl_ref, br_ref,
                h2_ref, bk_ref, oc_ref, *, alpha):
    tm = x_ref.shape[0]
    mix = _dot(of_ref[...], wo_ref[:FOX_W, :]) + _dot(os_ref[...], wo_ref[FOX_W:, :])
    h1 = _layer_norm(alpha * x_ref[...] + mix, g1_ref[...], b1_ref[...])

    q = (_dot(h1.astype(bf16), wq_ref[...]) * 0.0625).astype(bf16)
    for h in range(N_XH):
        sl = slice(h * XHD, (h + 1) * XHD)
        s = _dot_nt(q[:, sl], k_ref[:, sl])
        e = jnp.exp(s - jnp.max(s, axis=1, keepdims=True))
        p = e / jnp.sum(e, axis=1, keepdims=True)
        oc_ref[:, sl] = _dot(p.astype(bf16), v_ref[:, sl]).astype(bf16)
    xo = _dot(oc_ref[...], wxo_ref[...])
    h2 = _layer_norm(alpha * h1 + xo, g2_ref[...], b2_ref[...])
    h2_ref[:, :D] = h2

    hh = h2.astype(bf16)
    hl = (h2 - hh.astype(f32)).astype(bf16)
    lg = _dot(hh, wrh_ref[...]) + _dot(hl, wrh_ref[...]) + _dot(hh, wrl_ref[...]) + br_ref[...]

    lane = lax.broadcasted_iota(i32, (tm, LANES), 1).astype(f32)
    big = float(LANES)

    def first_max(vals, mask):
        vm = jnp.where(mask, vals, NEG)
        top = jnp.max(vm, axis=1, keepdims=True)
        idx = jnp.min(jnp.where(mask & (vm == top), lane, big), axis=1, keepdims=True)
        return top, idx

    gmask = lane < float(N_GROUPS)
    gmax, gidx = first_max(lg, gmask)
    g_val = 1.0 / jnp.sum(jnp.where(gmask, jnp.exp(lg - gmax), 0.0), axis=1, keepdims=True)
    lo = float(N_GROUPS) + float(EPG) * gidx
    emask = (lane >= lo) & (lane < lo + float(EPG))
    v1, i1 = first_max(lg, emask)
    v2, i2 = first_max(lg, emask & (lane != i1))
    ex = jnp.exp(v2 - v1)
    w1 = g_val * (1.0 / (1.0 + ex))
    w2 = g_val * (ex / (1.0 + ex))
    e1 = i1 - lo
    e2 = i2 - lo
    first_low = e1 < e2
    ea = jnp.where(first_low, e1, e2)
    eb = jnp.where(first_low, e2, e1)
    ga = jnp.where(first_low, w1, w2)
    gb = jnp.where(first_low, w2, w1)
    pidx = ea * float(EPG - 1) - ea * (ea - 1.0) * 0.5 + (eb - ea - 1.0)
    bucket = gidx * float(N_PAIRS) + pidx

    lane_i = lax.broadcasted_iota(i32, (tm, LANES), 1)
    h2_ref[:, D:] = jnp.where(lane_i == 0, ga, jnp.where(lane_i == 1, gb, 0.0))
    bk_t = jnp.transpose(jnp.broadcast_to(bucket, (tm, LANES)))
    bk_ref[...] = bk_t[:8, :]


def _mid(x2, of2, os2, w_out, g1, b1, wq, kx, vx, wxo, g2, b2, wrh, wrl, br, alpha, S, tm=512):
    T = x2.shape[0]
    M = kx.shape[0] // (T // S)
    per_b = S // tm
    row = lambda w: pl.BlockSpec((tm, w), lambda i: (i, 0))
    full = lambda a: pl.BlockSpec(a.shape, lambda i: (0,) * a.ndim)
    kvspec = pl.BlockSpec((M, D), lambda i: (i // per_b, 0))
    kernel = functools.partial(_mid_kernel, alpha=alpha)
    return pl.pallas_call(
        kernel,
        grid=(T // tm,),
        in_specs=[row(D), row(512), row(512), full(w_out), full(g1), full(b1), full(wq),
                  kvspec, kvspec, full(wxo), full(g2), full(b2), full(wrh), full(wrl), full(br)],
        out_specs=[row(XW), pl.BlockSpec((8, tm), lambda i: (0, i))],
        out_shape=[jax.ShapeDtypeStruct((T, XW), f32), jax.ShapeDtypeStruct((8, T), f32)],
        scratch_shapes=[pltpu.VMEM((tm, D), bf16)],
        compiler_params=_cparams(("parallel",)),
        name="mid",
    )(x2, of2, os2, w_out, g1, b1, wq, kx, vx, wxo, g2, b2, wrh, wrl, br)


def _rank_kernel(bk_ref, rank_ref, cnt_ref, carry_ref, *, chunk):
    sub = 256

    @pl.when(pl.program_id(0) == 0)
    def _():
        carry_ref[...] = jnp.zeros_like(carry_ref)

    r = lax.broadcasted_iota(i32, (sub, sub), 0)
    c = lax.broadcasted_iota(i32, (sub, sub), 1)
    before = (r < c).astype(bf16)
    bid = lax.broadcasted_iota(i32, (LANES, sub), 0).astype(f32)
    carry = carry_ref[...]
    for j in range(chunk // sub):
        bk = bk_ref[0:1, j * sub:(j + 1) * sub]
        hit = bid == bk
        oh = jnp.where(hit, 1.0, 0.0)
        prior = _dot(oh.astype(bf16), before) + carry
        rank_ref[:, j * sub:(j + 1) * sub] = jnp.sum(jnp.where(hit, prior, 0.0), axis=0, keepdims=True)
        carry = carry + jnp.sum(oh, axis=1, keepdims=True)
    carry_ref[...] = carry
    cnt_ref[...] = carry


def _rank(bk8, chunk=2048):
    T = bk8.shape[1]
    kernel = functools.partial(_rank_kernel, chunk=chunk)
    return pl.pallas_call(
        kernel, grid=(T // chunk,),
        in_specs=[pl.BlockSpec((8, chunk), lambda i: (0, i))],
        out_specs=[pl.BlockSpec((1, chunk), lambda i: (0, i)),
                   pl.BlockSpec((LANES, 1), lambda i: (0, 0))],
        out_shape=[jax.ShapeDtypeStruct((1, T), f32), jax.ShapeDtypeStruct((LANES, 1), f32)],
        scratch_shapes=[pltpu.VMEM((LANES, 1), f32)],
        compiler_params=_cparams(("arbitrary",)), name="rank",
    )(bk8)


def _dest_kernel(bk_ref, rank_ref, ps_ref, dest_ref):
    chunk = bk_ref.shape[1]
    bid = lax.broadcasted_iota(i32, (LANES, chunk), 0).astype(f32)
    start = jnp.sum(jnp.where(bid == bk_ref[0:1, :], ps_ref[...], 0.0), axis=0, keepdims=True)
    dest_ref[...] = (start + rank_ref[...]).astype(i32)


def _dest(bk8, rank, ps_col, chunk=2048):
    T = bk8.shape[1]
    return pl.pallas_call(
        _dest_kernel, grid=(T // chunk,),
        in_specs=[pl.BlockSpec((8, chunk), lambda i: (0, i)), pl.BlockSpec((1, chunk), lambda i: (0, i)),
                  pl.BlockSpec((LANES, 1), lambda i: (0, 0))],
        out_specs=pl.BlockSpec((1, chunk), lambda i: (0, i)),
        out_shape=jax.ShapeDtypeStruct((1, T), i32),
        compiler_params=_cparams(("parallel",)), name="dest",
    )(bk8, rank, ps_col)


def _row_copy(src, dst, sem, s, d):
    return pltpu.make_async_copy(src.at[pl.ds(s, 1)], dst.at[pl.ds(d, 1)], sem)


def _scatter_kernel(bk_ref, rk_ref, ps_ref, pe_ref, src_ref, dst_ref, zero_ref, sem, zsem, *, chunk):
    @pl.when(pl.program_id(0) == 0)
    def _():
        zero_ref[...] = jnp.zeros_like(zero_ref)

        def tail_copy(b):
            start = pl.multiple_of(pe_ref[b] - ROW_BLOCK, ROW_BLOCK)
            return pltpu.make_async_copy(zero_ref, dst_ref.at[pl.ds(start, ROW_BLOCK)], zsem)

        def fill(b, _):
            @pl.when(pe_ref[b] > ps_ref[b])
            def _():
                tail_copy(b).start()
            return 0

        def drain(b, _):
            @pl.when(pe_ref[b] > ps_ref[b])
            def _():
                tail_copy(b).wait()
            return 0

        lax.fori_loop(0, N_BUCKETS, fill, 0)
        lax.fori_loop(0, N_BUCKETS, drain, 0)

        def unused_copy(n):
            return pltpu.make_async_copy(zero_ref, dst_ref.at[pl.ds(pl.multiple_of(n * ROW_BLOCK, ROW_BLOCK),
                                                                    ROW_BLOCK)], zsem)

        first_unused = pe_ref[LANES - 1] // ROW_BLOCK
        n_blocks = dst_ref.shape[0] // ROW_BLOCK
        lax.fori_loop(first_unused, n_blocks, lambda n, _: (unused_copy(n).start(), 0)[1], 0)
        lax.fori_loop(first_unused, n_blocks, lambda n, _: (unused_copy(n).wait(), 0)[1], 0)

    def issue(u, _):
        for k in range(2):
            t = 2 * u + k
            d = ps_ref[bk_ref[t]] + rk_ref[t]
            _row_copy(src_ref, dst_ref, sem, t, d).start(priority=k)
        return 0

    lax.fori_loop(0, chunk // 2, issue, 0)
    pltpu.make_async_copy(src_ref, dst_ref.at[pl.ds(0, chunk)], sem).wait()


def _scatter_rows(bk, rk, ps, pe, src, P, chunk=1024):
    T, w = src.shape
    sm = lambda: pl.BlockSpec((chunk,), lambda i: (i,), memory_space=pltpu.SMEM)
    whole = lambda: pl.BlockSpec(memory_space=pltpu.SMEM)
    kernel = functools.partial(_scatter_kernel, chunk=chunk)
    return pl.pallas_call(
        kernel, grid=(T // chunk,),
        in_specs=[sm(), sm(), whole(), whole(), pl.BlockSpec((chunk, w), lambda i: (i, 0))],
        out_specs=pl.BlockSpec(memory_space=pl.ANY),
        out_shape=jax.ShapeDtypeStruct((P, w), src.dtype),
        scratch_shapes=[pltpu.VMEM((ROW_BLOCK, w), src.dtype), pltpu.SemaphoreType.DMA(()),
                        pltpu.SemaphoreType.DMA(())],
        compiler_params=pltpu.CompilerParams(dimension_semantics=("arbitrary",), has_side_effects=True,
                                             vmem_limit_bytes=VMEM_LIMIT),
        name="scatter_rows",
    )(bk, rk, ps, pe, src)


def _sc_invert(dest, n_rows):
    T = dest.shape[0]
    lanes = SC_LANES
    mesh = plsc.VectorSubcoreMesh(core_axis_name="core", subcore_axis_name="subcore",
                                  num_cores=SC_CORES, num_subcores=SC_SUBCORES)

    @functools.partial(pl.kernel, out_type=jax.ShapeDtypeStruct((n_rows,), i32), mesh=mesh,
                       scratch_types=[pltpu.VMEM((T,), i32), pltpu.VMEM((n_rows,), i32)],
                       compiler_params=pltpu.CompilerParams(needs_layout_passes=False),
                       name="sc_invert")
    def k(dest_hbm, out_hbm, dest_v, table_v):
        wid = lax.axis_index("subcore") * SC_CORES + lax.axis_index("core")

        @pl.when(wid == 0)
        def _():
            pltpu.sync_copy(dest_hbm, dest_v)
            zero = jnp.zeros((lanes,), i32)

            @pl.loop(0, n_rows // lanes)
            def _(j):
                table_v[pl.ds(pl.multiple_of(j * lanes, lanes), lanes)] = zero

            lane = lax.iota(i32, lanes)

            @pl.loop(0, T // lanes)
            def _(j):
                off = pl.multiple_of(j * lanes, lanes)
                plsc.store_scatter(table_v, [dest_v[pl.ds(off, lanes)]], lane + off)

            pltpu.sync_copy(table_v, out_hbm)

    return k(dest)


def _sc_gather_rows(idx, src, chunk=32):
    n = idx.shape[0]
    w = src.shape[1]
    workers = SC_CORES * SC_SUBCORES
    per_worker = n // workers
    mesh = plsc.VectorSubcoreMesh(core_axis_name="core", subcore_axis_name="subcore",
                                  num_cores=SC_CORES, num_subcores=SC_SUBCORES)

    @functools.partial(pl.kernel, out_type=jax.ShapeDtypeStruct((n, w), src.dtype), mesh=mesh,
                       scratch_types=[pltpu.VMEM((chunk,), i32), pltpu.VMEM((chunk, w), src.dtype),
                                      pltpu.SemaphoreType.DMA],
                       name="sc_gather_rows")
    def k(src_hbm, idx_hbm, out_hbm, idx_v, rows_v, sem):
        wid = lax.axis_index("subcore") * SC_CORES + lax.axis_index("core")

        @pl.loop(0, per_worker // chunk)
        def _(c):
            base = wid * per_worker + c * chunk
            pltpu.sync_copy(idx_hbm.at[pl.ds(base, chunk)], idx_v)
            pltpu.async_copy(src_hbm.at[idx_v], rows_v, sem).wait()
            pltpu.sync_copy(rows_v, out_hbm.at[pl.ds(base, chunk)])

    return k(src, idx)


def _expert_kernel(ea_ref, eb_ref, used_ref, xs_ref, wga_ref, wua_ref, wda_ref,
                   wgb_ref, wub_ref, wdb_ref, y_ref):
    n = pl.program_id(0)

    @pl.when(n < used_ref[0])
    def _():
        x = xs_ref[:, :D].astype(bf16)

        def expert(wg, wu, wd):
            a = _dot(x, wg[0])
            u = _dot(x, wu[0])
            act = a * (1.0 / (1.0 + jnp.exp(-a))) * u
            return _dot(act.astype(bf16), wd[0])

        ga = xs_ref[:, D:D + 1]
        gb = xs_ref[:, D + 1:D + 2]
        y_ref[...] = ga * expert(wga_ref, wua_ref, wda_ref) + gb * expert(wgb_ref, wub_ref, wdb_ref)

    @pl.when(n >= used_ref[0])
    def _():
        y_ref[...] = jnp.zeros_like(y_ref)


def _experts(ea, eb, used, xs, wg, wu, wd):
    P = xs.shape[0]
    nblk = P // ROW_BLOCK

    def xmap(n, ea, eb, used):
        return (jnp.minimum(n, used[0] - 1), 0)

    wa = lambda n, ea, eb, used: (ea[n], 0, 0)
    wb = lambda n, ea, eb, used: (eb[n], 0, 0)
    wgs = lambda m: pl.BlockSpec((1, D, D_EXPERT), m)
    wds = lambda m: pl.BlockSpec((1, D_EXPERT, D), m)
    grid_spec = pltpu.PrefetchScalarGridSpec(
        num_scalar_prefetch=3, grid=(nblk,),
        in_specs=[pl.BlockSpec((ROW_BLOCK, XW), xmap), wgs(wa), wgs(wa), wds(wa), wgs(wb), wgs(wb), wds(wb)],
        out_specs=pl.BlockSpec((ROW_BLOCK, D), lambda n, ea, eb, used: (n, 0)),
    )
    return pl.pallas_call(
        _expert_kernel, grid_spec=grid_spec,
        out_shape=jax.ShapeDtypeStruct((P, D), f32),
        compiler_params=_cparams(("arbitrary",)), name="experts",
    )(ea, eb, used, xs, wg, wu, wd, wg, wu, wd)


def _final_kernel(bk0_ref, rk0_ref, bkn_ref, rkn_ref, ps_ref, h_ref, ys_ref, g_ref, b_ref, o_ref,
                  buf, sems, *, alpha, tm):
    i = pl.program_id(0)
    n = pl.num_programs(0)

    def issue(bk_ref, rk_ref, slot):
        def body(u, _):
            for k in range(2):
                t = 2 * u + k
                s = ps_ref[bk_ref[t]] + rk_ref[t]
                pltpu.make_async_copy(ys_ref.at[pl.ds(s, 1)], buf.at[slot, pl.ds(t, 1)],
                                      sems.at[slot]).start(priority=k)
            return 0
        lax.fori_loop(0, tm // 2, body, 0)

    @pl.when(i == 0)
    def _():
        issue(bk0_ref, rk0_ref, 0)

    @pl.when(i + 1 < n)
    def _():
        issue(bkn_ref, rkn_ref, (i + 1) % 2)

    slot = i % 2
    pltpu.make_async_copy(ys_ref.at[pl.ds(0, tm)], buf.at[slot], sems.at[slot]).wait()
    o_ref[...] = _layer_norm(alpha * h_ref[...] + buf[slot], g_ref[...], b_ref[...])


def _final(bk, rk, ps, h2x, ys, g, b, alpha, tm=1024):
    T = h2x.shape[0]
    n = T // tm
    row = pl.BlockSpec((tm, D), lambda i: (i, 0))
    vec = pl.BlockSpec((1, D), lambda i: (0, 0))
    first = lambda: pl.BlockSpec((tm,), lambda i: (0,), memory_space=pltpu.SMEM)
    nxt = lambda: pl.BlockSpec((tm,), lambda i: (jnp.minimum(i + 1, n - 1),), memory_space=pltpu.SMEM)
    return pl.pallas_call(
        functools.partial(_final_kernel, alpha=alpha, tm=tm), grid=(n,),
        in_specs=[first(), first(), nxt(), nxt(), pl.BlockSpec(memory_space=pltpu.SMEM),
                  row, pl.BlockSpec(memory_space=pl.ANY), vec, vec],
        out_specs=row,
        out_shape=jax.ShapeDtypeStruct((T, D), f32),
        scratch_shapes=[pltpu.VMEM((2, tm, D), f32), pltpu.SemaphoreType.DMA((2,))],
        compiler_params=_cparams(("arbitrary",)), name="final_ln",
    )(bk, rk, bk, rk, ps, h2x, ys, g, b)


def _ln_kernel(h_ref, y_ref, g_ref, b_ref, o_ref, *, alpha):
    o_ref[...] = _layer_norm(alpha * h_ref[...] + y_ref[...], g_ref[...], b_ref[...])


def _final_ln(h2x, y, g, b, alpha, tm=512):
    T = y.shape[0]
    row = pl.BlockSpec((tm, D), lambda i: (i, 0))
    vec = pl.BlockSpec((1, D), lambda i: (0, 0))
    return pl.pallas_call(
        functools.partial(_ln_kernel, alpha=alpha), grid=(T // tm,),
        in_specs=[row, row, vec, vec], out_specs=row,
        out_shape=jax.ShapeDtypeStruct((T, D), f32),
        compiler_params=_cparams(("parallel",)), name="final_ln",
    )(h2x, y, g, b)


def _pair_tables():
    ea = np.zeros((LANES,), np.int32)
    eb = np.zeros((LANES,), np.int32)
    for g in range(N_GROUPS):
        k = 0
        for a in range(EPG):
            for b in range(a + 1, EPG):
                ea[g * N_PAIRS + k] = g * EPG + a
                eb[g * N_PAIRS + k] = g * EPG + b
                k += 1
    return ea, eb


_PAIR_A, _PAIR_B = _pair_tables()


def _layer(h, mem, positions, w_in, b_forget, sinks, w_mix_out, ln_mix_g, ln_mix_b,
           w_xq, w_xkv, w_xout, ln_x_g, ln_x_b, w_rg, b_rg, w_re, b_re,
           w_eg, w_eu, w_ed, ln_f_g, ln_f_b, alpha):
    B, S, _ = h.shape
    T = B * S
    x2 = h.reshape(T, D)
    pos2 = positions.reshape(T, 1).astype(i32)

    o = np.cumsum((0, FOX_W, FOX_W, FOX_W, N_FOX, SWA_Q_W, SWA_KV_W, SWA_KV_W))
    w_qf, w_kf, w_vf, w_fl, w_qs, w_ks, w_vs = (w_in[:, o[i]:o[i + 1]] for i in range(7))
    def regroup(a, axis):
        shp = a.shape
        a = jnp.moveaxis(a, axis, 0).reshape(N_SWA_KV, N_SWA // N_SWA_KV, HD, -1)
        return jnp.moveaxis(jnp.swapaxes(a, 0, 1).reshape(N_SWA * HD, -1), 0, axis).reshape(shp)

    w_all = jnp.concatenate([w_kf, regroup(w_qs, 1), w_ks, w_vs], axis=1).astype(bf16)
    wqt = w_qf.T.astype(bf16)
    wvt = w_vf.T.astype(bf16)
    wfl = w_fl.T.astype(bf16)
    bfc = b_forget.reshape(N_FOX, 1).astype(f32)
    half = HD // 2
    inv_freq = ROPE_THETA ** (-jnp.arange(half, dtype=f32) / half)
    invf = jnp.tile(inv_freq, LANES // half).reshape(1, LANES)
    w_out = jnp.concatenate([w_mix_out[:FOX_W], regroup(w_mix_out[FOX_W:], 0)], axis=0).astype(bf16)

    tq = 512
    qt, kf, vt, qs, ks, vs, lf = _in_proj(x2, pos2, w_all, wqt, wvt, wfl, bfc, invf, tq)
    c, ca = _cumsum(lf, S)
    c4 = c.reshape(N_FOX // 2, 2, T // tq, tq)
    r3 = lambda a: a.reshape(B, S, a.shape[-1])
    o_fox = _fox(qt, r3(kf), r3(ca), vt, c4, B, S, tq)
    o_swa = _swa(sinks.astype(f32), r3(qs), r3(ks), r3(vs), B, S)

    kx, vx = _kvproj(mem.reshape(-1, D), w_xkv.astype(bf16))

    wr = jnp.concatenate([w_rg, jnp.transpose(w_re, (1, 0, 2)).reshape(D, N_EXPERTS)], axis=1)
    wr = jnp.pad(wr, ((0, 0), (0, LANES - wr.shape[1]))).astype(f32)
    wrh = wr.astype(bf16)
    wrl = (wr - wrh.astype(f32)).astype(bf16)
    br = jnp.pad(jnp.concatenate([b_rg, b_re.reshape(-1)]), (0, LANES - N_GROUPS - N_EXPERTS))
    br = br.reshape(1, LANES).astype(f32)
    v2 = lambda a: a.reshape(1, D).astype(f32)
    h2x, bk8 = _mid(x2, o_fox.reshape(T, FOX_W), o_swa.reshape(T, SWA_Q_W), w_out,
                    v2(ln_mix_g), v2(ln_mix_b), w_xq.astype(bf16), kx, vx, w_xout.astype(bf16),
                    v2(ln_x_g), v2(ln_x_b), wrh, wrl, br, alpha, S)

    rank, cnt = _rank(bk8)
    bucket_i = bk8[0].astype(i32)
    rank_i = rank[0].astype(i32)
    counts = cnt[:, 0].astype(i32)
    padded = ((counts + ROW_BLOCK - 1) // ROW_BLOCK) * ROW_BLOCK
    pad_end = jnp.cumsum(padded)
    pad_start = (pad_end - padded).astype(i32)
    P = T + N_BUCKETS * ROW_BLOCK
    nblk = P // ROW_BLOCK
    used = (pad_end[-1] // ROW_BLOCK).astype(i32).reshape(1)
    blk_row = jnp.arange(nblk, dtype=i32)[:, None] * ROW_BLOCK
    blk_bucket = jnp.minimum(jnp.sum((pad_end[None, :] <= blk_row).astype(i32), axis=1), N_BUCKETS - 1)
    pick = (blk_bucket[:, None] == jnp.arange(LANES, dtype=i32)[None, :]).astype(i32)
    blk_a = jnp.sum(pick * jnp.asarray(_PAIR_A)[None, :], axis=1)
    blk_b = jnp.sum(pick * jnp.asarray(_PAIR_B)[None, :], axis=1)

    dest = _dest(bk8, rank, pad_start.astype(f32).reshape(LANES, 1))[0]
    xs = _sc_gather_rows(_sc_invert(dest, P), h2x)
    ys = _experts(blk_a, blk_b, used, xs, w_eg.astype(bf16), w_eu.astype(bf16), w_ed.astype(bf16))
    y = _sc_gather_rows(dest, ys)
    out = _final_ln(h2x, y, v2(ln_f_g), v2(ln_f_b), alpha)
    return out.reshape(B, S, D)


def kernel(x, mem, positions, w_in, b_forget, sinks, w_mix_out, ln_mix_g, ln_mix_b, w_xq, w_xkv, w_xout,
           ln_x_g, ln_x_b, w_route_group, b_route_group, w_route_expert, b_route_expert,
           w_exp_gate, w_exp_up, w_exp_down, ln_ffn_g, ln_ffn_b):
    depth = w_in.shape[0]
    alpha = (2.0 * depth) ** 0.25
    h = x
    for l in range(depth):
        h = _layer(h, mem, positions, w_in[l], b_forget[l], sinks[l], w_mix_out[l], ln_mix_g[l], ln_mix_b[l],
                   w_xq[l], w_xkv[l], w_xout[l], ln_x_g[l], ln_x_b[l], w_route_group[l], b_route_group[l],
                   w_route_expert[l], b_route_expert[l], w_exp_gate[l], w_exp_up[l], w_exp_down[l],
                   ln_ffn_g[l], ln_ffn_b[l], alpha)
    return h
```

```python
import functools

import jax
import jax.numpy as jnp
import numpy as np
from jax import lax
from jax.experimental import pallas as pl
from jax.experimental.pallas import tpu as pltpu
from jax.experimental.pallas import tpu_sc as plsc

f32 = jnp.float32
bf16 = jnp.bfloat16
i32 = jnp.int32

D = 1024
HD = 64
N_FOX = 8
N_SWA = 8
N_SWA_KV = 2
FOX_W = 512
SWA_Q_W = 512
SWA_KV_W = 128
WINDOW = 128
ROPE_THETA = 10000.0
N_XH = 4
XHD = 256
N_GROUPS = 4
EPG = 8
N_EXPERTS = 32
D_EXPERT = 512
LN_EPS = 1e-5
NEG = -1e30
LOG2E = 1.4426950408889634
L_ROW = (HD, 0)

SC_CORES = 2
SC_SUBCORES = 16
SC_LANES = 16
LANES = 128
ROW_BLOCK = 128
N_PAIRS = EPG * (EPG - 1) // 2
N_BUCKETS = N_GROUPS * N_PAIRS
XW = D + LANES
VMEM_LIMIT = 56 * 1024 * 1024


def _cparams(sem):
    return pltpu.CompilerParams(dimension_semantics=sem, vmem_limit_bytes=VMEM_LIMIT)


def _layer_norm(v, g, b):
    mu = jnp.mean(v, axis=-1, keepdims=True)
    c = v - mu
    var = jnp.mean(c * c, axis=-1, keepdims=True)
    return c * lax.rsqrt(var + LN_EPS) * g + b


def _dot(a, b):
    return jnp.dot(a, b, preferred_element_type=f32)


def _dot_nt(a, b):
    return lax.dot_general(a, b, (((1,), (1,)), ((), ())), preferred_element_type=f32)


def _inproj_kernel(x_ref, pos_ref, w_ref, wqt_ref, wvt_ref, wfl_ref, bf_ref, invf_ref,
                   qt_ref, kf_ref, vt_ref, qs_ref, ks_ref, vs_ref, lf_ref):
    tm = x_ref.shape[0]
    xb = x_ref[...].astype(bf16)

    def proj(lo, hi):
        return _dot(xb, w_ref[:, lo:hi])

    qt_ref[0] = (_dot_nt(wqt_ref[...], xb) * (0.125 * LOG2E)).astype(bf16)
    vt_ref[0] = _dot_nt(wvt_ref[...], xb).astype(bf16)
    kf_ref[...] = proj(0, 512).astype(bf16)

    ang = pos_ref[...].astype(f32) * invf_ref[...]
    cos = jnp.cos(ang)
    sin = jnp.sin(ang)
    lane = lax.broadcasted_iota(i32, (tm, LANES), 1)
    lo_half = (lane % HD) < (HD // 2)
    sin_s = jnp.where(lo_half, -sin, sin)

    def rope(z):
        rot = jnp.where(lo_half, pltpu.roll(z, LANES - HD // 2, 1), pltpu.roll(z, HD // 2, 1))
        return z * cos + rot * sin_s

    zq = proj(512, 1024)
    for g in range(4):
        sl = slice(g * LANES, (g + 1) * LANES)
        qs_ref[:, sl] = (rope(zq[:, sl]) * 0.125).astype(bf16)
    ks_ref[...] = rope(proj(1024, 1152)).astype(bf16)
    vs_ref[...] = proj(1152, 1280).astype(bf16)

    z = _dot_nt(wfl_ref[...], xb) + bf_ref[...]
    lf_ref[...] = jnp.minimum(z, 0.0) - jnp.log(1.0 + jnp.exp(-jnp.abs(z)))


def _in_proj(x2, pos2, w_all, wqt, wvt, wfl, bfc, invf, tm):
    T = x2.shape[0]
    row = lambda w: pl.BlockSpec((tm, w), lambda i: (i, 0))
    full = lambda a: pl.BlockSpec(a.shape, lambda i: (0,) * a.ndim)
    fmaj = pl.BlockSpec((1, FOX_W, tm), lambda i: (i, 0, 0))
    return pl.pallas_call(
        _inproj_kernel,
        grid=(T // tm,),
        in_specs=[row(D), row(1), full(w_all), full(wqt), full(wvt), full(wfl), full(bfc), full(invf)],
        out_specs=[fmaj, row(512), fmaj, row(512), row(128), row(128),
                   pl.BlockSpec((N_FOX, tm), lambda i: (0, i))],
        out_shape=[jax.ShapeDtypeStruct((T // tm, FOX_W, tm), bf16), jax.ShapeDtypeStruct((T, 512), bf16),
                   jax.ShapeDtypeStruct((T // tm, FOX_W, tm), bf16), jax.ShapeDtypeStruct((T, 512), bf16),
                   jax.ShapeDtypeStruct((T, 128), bf16), jax.ShapeDtypeStruct((T, 128), bf16),
                   jax.ShapeDtypeStruct((N_FOX, T), f32)],
        compiler_params=_cparams(("parallel",)),
        name="in_proj",
    )(x2, pos2, w_all, wqt, wvt, wfl, bfc, invf)


def _cumsum_kernel(lf_ref, c_ref, ca_ref):
    S = lf_ref.shape[1]
    ch = 256
    r = lax.broadcasted_iota(i32, (ch, ch), 0)
    c = lax.broadcasted_iota(i32, (ch, ch), 1)
    tri = (r <= c).astype(f32)
    eye = (r == c).astype(bf16)
    carry = jnp.zeros((N_FOX, 1), f32)
    for j in range(S // ch):
        a = lf_ref[:, j * ch:(j + 1) * ch]
        cc = jnp.dot(a, tri, precision=lax.Precision.HIGHEST, preferred_element_type=f32) + carry
        carry = cc[:, ch - 1:ch]
        c2 = cc * LOG2E
        c_ref[:, j * ch:(j + 1) * ch] = c2
        neg = -c2
        hi = neg.astype(bf16)
        r1 = neg - hi.astype(f32)
        mid = r1.astype(bf16)
        lo = (r1 - mid.astype(f32)).astype(bf16)
        terms = jnp.concatenate([hi, mid, lo, jnp.zeros((LANES - 3 * N_FOX, ch), bf16)], axis=0)
        ca_ref[j * ch:(j + 1) * ch, :] = _dot_nt(eye, terms).astype(bf16)


def _cumsum(lf, S):
    T = lf.shape[1]
    spec = pl.BlockSpec((N_FOX, S), lambda b: (0, b))
    return pl.pallas_call(
        _cumsum_kernel, grid=(T // S,), in_specs=[spec],
        out_specs=[spec, pl.BlockSpec((S, LANES), lambda b: (b, 0))],
        out_shape=[jax.ShapeDtypeStruct((N_FOX, T), f32), jax.ShapeDtypeStruct((T, LANES), bf16)],
        compiler_params=_cparams(("parallel",)), name="cumsum",
    )(lf)


def _fox_kernel(qt_ref, k_ref, ca_ref, vt_ref, c_ref, o_ref, t0_ref, t1_ref, *, tq):
    hp = pl.program_id(1)
    i = pl.program_id(2)
    qt = qt_ref[0]
    row = lax.broadcasted_iota(i32, (LANES, tq), 0)
    is_a = row < HD
    zero = jnp.zeros_like(qt)
    q_ops = []
    for h in range(2):
        ones = jnp.where(((row & 7) == 2 * hp + h) & (row < 3 * N_FOX), 1.0, 0.0).astype(bf16)
        qh = jnp.where(is_a, qt, zero) if h == 0 else jnp.where(is_a, zero, qt)
        q_ops.append(jnp.concatenate([qh, ones], axis=0))
    kr = lax.broadcasted_iota(i32, (tq, tq), 0)
    qc = lax.broadcasted_iota(i32, (tq, tq), 1)
    causal = kr <= qc
    cq = [c_ref[0, h, pl.ds(i, 1), :] for h in range(2)]

    def scores(j, t_ref):
        off = pl.multiple_of(j * tq, tq)
        kblk = jnp.concatenate([k_ref[0, pl.ds(off, tq), :], ca_ref[0, pl.ds(off, tq), :]], axis=1)
        for h in range(2):
            t_ref[h] = _dot(kblk, q_ops[h])

    keep = [jnp.where(is_a, 1.0, 0.0).astype(bf16), jnp.where(is_a, 0.0, 1.0).astype(bf16)]
    ones_row = [jnp.where(row == L_ROW[h], 1.0, 0.0).astype(bf16) for h in range(2)]

    def softmax_pv(j, t_ref, carry, masked):
        vt = vt_ref[j]
        vts = [vt * keep[h] + ones_row[h] for h in range(2)]
        new = []
        for h in range(2):
            m, acc = carry[h]
            t = t_ref[h]
            if masked:
                t = jnp.where(causal, t, NEG)
            m_new = jnp.maximum(m, jnp.max(t, axis=0, keepdims=True) + cq[h])
            alpha = jnp.exp2(m - m_new)
            p = jnp.exp2(t + (cq[h] - m_new))
            acc = alpha * acc + _dot(vts[h], p.astype(bf16))
            new.append((m_new, acc))
        return tuple(new)

    def pair(k, carry):
        j = 2 * k
        scores(j + 1, t1_ref)
        carry = softmax_pv(j, t0_ref, carry, False)
        scores(j + 2, t0_ref)
        return softmax_pv(j + 1, t1_ref, carry, False)

    def odd_tail(carry):
        scores(i, t1_ref)
        carry = softmax_pv(i - 1, t0_ref, carry, False)
        return softmax_pv(i, t1_ref, carry, True)

    def even_tail(carry):
        return softmax_pv(i, t0_ref, carry, True)

    init = tuple((jnp.full((1, tq), NEG, f32), jnp.zeros((LANES, tq), f32)) for _ in range(2))
    scores(0, t0_ref)
    carry = lax.fori_loop(0, i // 2, pair, init)
    (_, acca), (_, accb) = lax.cond(i % 2 == 1, odd_tail, even_tail, carry)
    la = acca[L_ROW[0]:L_ROW[0] + 1, :]
    lb = accb[L_ROW[1]:L_ROW[1] + 1, :]
    ot = jnp.where(is_a, acca / la, accb / lb)
    o_ref[0] = jnp.transpose(ot).astype(bf16)


def _fox(qt, kf, ca, vt, c4, B, S, tq):
    nq = S // tq
    kernel = functools.partial(_fox_kernel, tq=tq)
    return pl.pallas_call(
        kernel,
        grid=(B, N_FOX // 2, nq),
        in_specs=[
            pl.BlockSpec((1, LANES, tq), lambda b, hp, i: (b * nq + i, hp, 0)),
            pl.BlockSpec((1, S, LANES), lambda b, hp, i: (b, 0, hp)),
            pl.BlockSpec((1, S, LANES), lambda b, hp, i: (b, 0, 0)),
            pl.BlockSpec((nq, LANES, tq), lambda b, hp, i: (b, hp, 0)),
            pl.BlockSpec((1, 2, nq, tq), lambda b, hp, i: (hp, 0, b, 0)),
        ],
        out_specs=pl.BlockSpec((1, tq, LANES), lambda b, hp, i: (b, i, hp)),
        out_shape=jax.ShapeDtypeStruct((B, S, FOX_W), bf16),
        scratch_shapes=[pltpu.VMEM((2, tq, tq), f32), pltpu.VMEM((2, tq, tq), f32)],
        compiler_params=_cparams(("parallel", "parallel", "arbitrary")),
        name="fox",
    )(qt, kf, ca, vt, c4)


def _swa_kernel(sink_ref, q_ref, k_ref, v_ref, o_ref, *, tq):
    W = WINDOW
    nsub = tq // W
    n0 = pl.program_id(1) * nsub
    lane = lax.broadcasted_iota(i32, (W, LANES), 1)
    is0 = lane < HD
    rows = lax.broadcasted_iota(i32, (4 * W, 2 * W), 0)
    cols = lax.broadcasted_iota(i32, (4 * W, 2 * W), 1)
    rgrp = lax.broadcasted_iota(i32, (4 * W, 1), 0) // W
    for r in range(nsub):
        nb = n0 + r
        kstart = pl.multiple_of(jnp.maximum(nb * W - W, 0), W)
        ks = k_ref[0, pl.ds(kstart, 2 * W), :]
        vs = v_ref[0, pl.ds(kstart, 2 * W), :]
        qpos = nb * W + rows % W
        kpos = kstart + cols
        valid = (kpos <= qpos) & (qpos - kpos < W)
        outs = []
        for kv in range(2):
            keep = is0 if kv == 0 else jnp.logical_not(is0)
            parts = []
            for g in range(4):
                qg = q_ref[0, r * W:(r + 1) * W, g * LANES:(g + 1) * LANES]
                parts.append(jnp.where(keep, qg, jnp.zeros_like(qg)))
            qstack = jnp.concatenate(parts, axis=0)
            s = jnp.where(valid, _dot_nt(qstack, ks), NEG)
            sink = jnp.zeros((4 * W, 1), f32)
            for g in range(4):
                sink = jnp.where(rgrp == g, sink_ref[kv * 4 + g], sink)
            m = jnp.maximum(jnp.max(s, axis=1, keepdims=True), sink)
            e = jnp.exp(s - m)
            den = jnp.sum(e, axis=1, keepdims=True) + jnp.exp(sink - m)
            outs.append(_dot(e.astype(bf16), vs) / den)
        for g in range(4):
            og = jnp.where(is0, outs[0][g * W:(g + 1) * W], outs[1][g * W:(g + 1) * W])
            o_ref[0, r * W:(r + 1) * W, g * LANES:(g + 1) * LANES] = og.astype(bf16)


def _swa(sinks, qs, ks, vs, B, S, tq=512):
    kernel = functools.partial(_swa_kernel, tq=tq)
    return pl.pallas_call(
        kernel,
        grid=(B, S // tq),
        in_specs=[
            pl.BlockSpec(memory_space=pltpu.SMEM),
            pl.BlockSpec((1, tq, SWA_Q_W), lambda b, i: (b, i, 0)),
            pl.BlockSpec((1, S, SWA_KV_W), lambda b, i: (b, 0, 0)),
            pl.BlockSpec((1, S, SWA_KV_W), lambda b, i: (b, 0, 0)),
        ],
        out_specs=pl.BlockSpec((1, tq, SWA_Q_W), lambda b, i: (b, i, 0)),
        out_shape=jax.ShapeDtypeStruct((B, S, SWA_Q_W), bf16),
        compiler_params=_cparams(("parallel", "arbitrary")),
        name="swa",
    )(sinks, qs, ks, vs)


def _kvproj_kernel(m_ref, w_ref, k_ref, v_ref):
    mb = m_ref[...].astype(bf16)
    k_ref[...] = _dot(mb, w_ref[:, :D]).astype(bf16)
    v_ref[...] = _dot(mb, w_ref[:, D:]).astype(bf16)


def _kvproj(mem2, w_xkv, tm=512):
    R = mem2.shape[0]
    row = pl.BlockSpec((tm, D), lambda i: (i, 0))
    return pl.pallas_call(
        _kvproj_kernel, grid=(R // tm,),
        in_specs=[row, pl.BlockSpec(w_xkv.shape, lambda i: (0, 0))],
        out_specs=[row, row],
        out_shape=[jax.ShapeDtypeStruct((R, D), bf16)] * 2,
        compiler_params=_cparams(("parallel",)), name="kvproj",
    )(mem2, w_xkv)


def _mid_kernel(x_ref, of_ref, os_ref, wo_ref, g1_ref, b1_ref, wq_ref, k_ref, v_ref,
                wxo_ref, g2_ref, b2_ref, wrh_ref, wrl_ref, br_ref,
                h2_ref, bk_ref, oc_ref, *, alpha):
    tm = x_ref.shape[0]
    mix = _dot(of_ref[...], wo_ref[:FOX_W, :]) + _dot(os_ref[...], wo_ref[FOX_W:, :])
    h1 = _layer_norm(alpha * x_ref[...] + mix, g1_ref[...], b1_ref[...])

    q = (_dot(h1.astype(bf16), wq_ref[...]) * 0.0625).astype(bf16)
    for h in range(N_XH):
        sl = slice(h * XHD, (h + 1) * XHD)
        s = _dot_nt(q[:, sl], k_ref[:, sl])
        e = jnp.exp(s - jnp.max(s, axis=1, keepdims=True))
        p = e / jnp.sum(e, axis=1, keepdims=True)
        oc_ref[:, sl] = _dot(p.astype(bf16), v_ref[:, sl]).astype(bf16)
    xo = _dot(oc_ref[...], wxo_ref[...])
    h2 = _layer_norm(alpha * h1 + xo, g2_ref[...], b2_ref[...])
    h2_ref[:, :D] = h2

    hh = h2.astype(bf16)
    hl = (h2 - hh.astype(f32)).astype(bf16)
    lg = _dot(hh, wrh_ref[...]) + _dot(hl, wrh_ref[...]) + _dot(hh, wrl_ref[...]) + br_ref[...]

    lane = lax.broadcasted_iota(i32, (tm, LANES), 1).astype(f32)
    big = float(LANES)

    def first_max(vals, mask):
        vm = jnp.where(mask, vals, NEG)
        top = jnp.max(vm, axis=1, keepdims=True)
        idx = jnp.min(jnp.where(mask & (vm == top), lane, big), axis=1, keepdims=True)
        return top, idx

    gmask = lane < float(N_GROUPS)
    gmax, gidx = first_max(lg, gmask)
    g_val = 1.0 / jnp.sum(jnp.where(gmask, jnp.exp(lg - gmax), 0.0), axis=1, keepdims=True)
    lo = float(N_GROUPS) + float(EPG) * gidx
    emask = (lane >= lo) & (lane < lo + float(EPG))
    v1, i1 = first_max(lg, emask)
    v2, i2 = first_max(lg, emask & (lane != i1))
    ex = jnp.exp(v2 - v1)
    w1 = g_val * (1.0 / (1.0 + ex))
    w2 = g_val * (ex / (1.0 + ex))
    e1 = i1 - lo
    e2 = i2 - lo
    first_low = e1 < e2
    ea = jnp.where(first_low, e1, e2)
    eb = jnp.where(first_low, e2, e1)
    ga = jnp.where(first_low, w1, w2)
    gb = jnp.where(first_low, w2, w1)
    pidx = ea * float(EPG - 1) - ea * (ea - 1.0) * 0.5 + (eb - ea - 1.0)
    bucket = gidx * float(N_PAIRS) + pidx

    lane_i = lax.broadcasted_iota(i32, (tm, LANES), 1)
    h2_ref[:, D:] = jnp.where(lane_i == 0, ga, jnp.where(lane_i == 1, gb, 0.0))
    bk_t = jnp.transpose(jnp.broadcast_to(bucket, (tm, LANES)))
    bk_ref[...] = bk_t[:8, :]


def _mid(x2, of2, os2, w_out, g1, b1, wq, kx, vx, wxo, g2, b2, wrh, wrl, br, alpha, S, tm=512):
    T = x2.shape[0]
    M = kx.shape[0] // (T // S)
    per_b = S // tm
    row = lambda w: pl.BlockSpec((tm, w), lambda i: (i, 0))
    full = lambda a: pl.BlockSpec(a.shape, lambda i: (0,) * a.ndim)
    kvspec = pl.BlockSpec((M, D), lambda i: (i // per_b, 0))
    kernel = functools.partial(_mid_kernel, alpha=alpha)
    return pl.pallas_call(
        kernel,
        grid=(T // tm,),
        in_specs=[row(D), row(512), row(512), full(w_out), full(g1), full(b1), full(wq),
                  kvspec, kvspec, full(wxo), full(g2), full(b2), full(wrh), full(wrl), full(br)],
        out_specs=[row(XW), pl.BlockSpec((8, tm), lambda i: (0, i))],
        out_shape=[jax.ShapeDtypeStruct((T, XW), f32), jax.ShapeDtypeStruct((8, T), f32)],
        scratch_shapes=[pltpu.VMEM((tm, D), bf16)],
        compiler_params=_cparams(("parallel",)),
        name="mid",
    )(x2, of2, os2, w_out, g1, b1, wq, kx, vx, wxo, g2, b2, wrh, wrl, br)


def _rank_kernel(bk_ref, rank_ref, cnt_ref, carry_ref, *, chunk):
    sub = 256

    @pl.when(pl.program_id(0) == 0)
    def _():
        carry_ref[...] = jnp.zeros_like(carry_ref)

    r = lax.broadcasted_iota(i32, (sub, sub), 0)
    c = lax.broadcasted_iota(i32, (sub, sub), 1)
    before = (r < c).astype(bf16)
    bid = lax.broadcasted_iota(i32, (LANES, sub), 0).astype(f32)
    carry = carry_ref[...]
    for j in range(chunk // sub):
        bk = bk_ref[0:1, j * sub:(j + 1) * sub]
        hit = bid == bk
        oh = jnp.where(hit, 1.0, 0.0)
        prior = _dot(oh.astype(bf16), before) + carry
        rank_ref[:, j * sub:(j + 1) * sub] = jnp.sum(jnp.where(hit, prior, 0.0), axis=0, keepdims=True)
        carry = carry + jnp.sum(oh, axis=1, keepdims=True)
    carry_ref[...] = carry
    cnt_ref[...] = carry


def _rank(bk8, chunk=2048):
    T = bk8.shape[1]
    kernel = functools.partial(_rank_kernel, chunk=chunk)
    return pl.pallas_call(
        kernel, grid=(T // chunk,),
        in_specs=[pl.BlockSpec((8, chunk), lambda i: (0, i))],
        out_specs=[pl.BlockSpec((1, chunk), lambda i: (0, i)),
                   pl.BlockSpec((LANES, 1), lambda i: (0, 0))],
        out_shape=[jax.ShapeDtypeStruct((1, T), f32), jax.ShapeDtypeStruct((LANES, 1), f32)],
        scratch_shapes=[pltpu.VMEM((LANES, 1), f32)],
        compiler_params=_cparams(("arbitrary",)), name="rank",
    )(bk8)


def _dest_kernel(bk_ref, rank_ref, ps_ref, dest_ref):
    chunk = bk_ref.shape[1]
    bid = lax.broadcasted_iota(i32, (LANES, chunk), 0).astype(f32)
    start = jnp.sum(jnp.where(bid == bk_ref[0:1, :], ps_ref[...], 0.0), axis=0, keepdims=True)
    dest_ref[...] = (start + rank_ref[...]).astype(i32)


def _dest(bk8, rank, ps_col, chunk=2048):
    T = bk8.shape[1]
    return pl.pallas_call(
        _dest_kernel, grid=(T // chunk,),
        in_specs=[pl.BlockSpec((8, chunk), lambda i: (0, i)), pl.BlockSpec((1, chunk), lambda i: (0, i)),
                  pl.BlockSpec((LANES, 1), lambda i: (0, 0))],
        out_specs=pl.BlockSpec((1, chunk), lambda i: (0, i)),
        out_shape=jax.ShapeDtypeStruct((1, T), i32),
        compiler_params=_cparams(("parallel",)), name="dest",
    )(bk8, rank, ps_col)


def _row_copy(src, dst, sem, s, d):
    return pltpu.make_async_copy(src.at[pl.ds(s, 1)], dst.at[pl.ds(d, 1)], sem)


def _scatter_kernel(bk_ref, rk_ref, ps_ref, pe_ref, src_ref, dst_ref, zero_ref, sem, zsem, *, chunk):
    @pl.when(pl.program_id(0) == 0)
    def _():
        zero_ref[...] = jnp.zeros_like(zero_ref)

        def tail_copy(b):
            start = pl.multiple_of(pe_ref[b] - ROW_BLOCK, ROW_BLOCK)
            return pltpu.make_async_copy(zero_ref, dst_ref.at[pl.ds(start, ROW_BLOCK)], zsem)

        def fill(b, _):
            @pl.when(pe_ref[b] > ps_ref[b])
            def _():
                tail_copy(b).start()
            return 0

        def drain(b, _):
            @pl.when(pe_ref[b] > ps_ref[b])
            def _():
                tail_copy(b).wait()
            return 0

        lax.fori_loop(0, N_BUCKETS, fill, 0)
        lax.fori_loop(0, N_BUCKETS, drain, 0)

        def unused_copy(n):
            return pltpu.make_async_copy(zero_ref, dst_ref.at[pl.ds(pl.multiple_of(n * ROW_BLOCK, ROW_BLOCK),
                                                                    ROW_BLOCK)], zsem)

        first_unused = pe_ref[LANES - 1] // ROW_BLOCK
        n_blocks = dst_ref.shape[0] // ROW_BLOCK
        lax.fori_loop(first_unused, n_blocks, lambda n, _: (unused_copy(n).start(), 0)[1], 0)
        lax.fori_loop(first_unused, n_blocks, lambda n, _: (unused_copy(n).wait(), 0)[1], 0)

    def issue(u, _):
        for k in range(2):
            t = 2 * u + k
            d = ps_ref[bk_ref[t]] + rk_ref[t]
            _row_copy(src_ref, dst_ref, sem, t, d).start(priority=k)
        return 0

    lax.fori_loop(0, chunk // 2, issue, 0)
    pltpu.make_async_copy(src_ref, dst_ref.at[pl.ds(0, chunk)], sem).wait()


def _scatter_rows(bk, rk, ps, pe, src, P, chunk=1024):
    T, w = src.shape
    sm = lambda: pl.BlockSpec((chunk,), lambda i: (i,), memory_space=pltpu.SMEM)
    whole = lambda: pl.BlockSpec(memory_space=pltpu.SMEM)
    kernel = functools.partial(_scatter_kernel, chunk=chunk)
    return pl.pallas_call(
        kernel, grid=(T // chunk,),
        in_specs=[sm(), sm(), whole(), whole(), pl.BlockSpec((chunk, w), lambda i: (i, 0))],
        out_specs=pl.BlockSpec(memory_space=pl.ANY),
        out_shape=jax.ShapeDtypeStruct((P, w), src.dtype),
        scratch_shapes=[pltpu.VMEM((ROW_BLOCK, w), src.dtype), pltpu.SemaphoreType.DMA(()),
                        pltpu.SemaphoreType.DMA(())],
        compiler_params=pltpu.CompilerParams(dimension_semantics=("arbitrary",), has_side_effects=True,
                                             vmem_limit_bytes=VMEM_LIMIT),
        name="scatter_rows",
    )(bk, rk, ps, pe, src)


def _sc_invert(dest, n_rows):
    T = dest.shape[0]
    assert T & (T - 1) == 0
    lanes = SC_LANES
    mesh = plsc.VectorSubcoreMesh(core_axis_name="core", subcore_axis_name="subcore",
                                  num_cores=SC_CORES, num_subcores=SC_SUBCORES)

    @functools.partial(pl.kernel, out_type=jax.ShapeDtypeStruct((n_rows,), i32), mesh=mesh,
                       scratch_types=[pltpu.VMEM((T,), i32), pltpu.VMEM((n_rows,), i32)],
                       compiler_params=pltpu.CompilerParams(needs_layout_passes=False),
                       name="sc_invert")
    def k(dest_hbm, out_hbm, dest_v, table_v):
        wid = lax.axis_index("subcore") * SC_CORES + lax.axis_index("core")

        @pl.when(wid == 0)
        def _():
            pltpu.sync_copy(dest_hbm, dest_v)
            lane = lax.iota(i32, lanes)

            @pl.loop(0, n_rows // lanes)
            def _(j):
                off = pl.multiple_of(j * lanes, lanes)
                table_v[pl.ds(off, lanes)] = (lane + off) & (T - 1)


            @pl.loop(0, T // lanes)
            def _(j):
                off = pl.multiple_of(j * lanes, lanes)
                plsc.store_scatter(table_v, [dest_v[pl.ds(off, lanes)]], lane + off)

            pltpu.sync_copy(table_v, out_hbm)

    return k(dest)


def _sc_gather_rows(idx, src, chunk=32):
    n = idx.shape[0]
    w = src.shape[1]
    workers = SC_CORES * SC_SUBCORES
    per_worker = n // workers
    mesh = plsc.VectorSubcoreMesh(core_axis_name="core", subcore_axis_name="subcore",
                                  num_cores=SC_CORES, num_subcores=SC_SUBCORES)

    @functools.partial(pl.kernel, out_type=jax.ShapeDtypeStruct((n, w), src.dtype), mesh=mesh,
                       scratch_types=[pltpu.VMEM((chunk,), i32), pltpu.VMEM((chunk, w), src.dtype),
                                      pltpu.SemaphoreType.DMA],
                       name="sc_gather_rows")
    def k(src_hbm, idx_hbm, out_hbm, idx_v, rows_v, sem):
        wid = lax.axis_index("subcore") * SC_CORES + lax.axis_index("core")

        @pl.loop(0, per_worker // chunk)
        def _(c):
            base = wid * per_worker + c * chunk
            pltpu.sync_copy(idx_hbm.at[pl.ds(base, chunk)], idx_v)
            pltpu.async_copy(src_hbm.at[idx_v], rows_v, sem).wait()
            pltpu.sync_copy(rows_v, out_hbm.at[pl.ds(base, chunk)])

    return k(src, idx)


def _expert_kernel(ea_ref, eb_ref, used_ref, xs_ref, wga_ref, wua_ref, wda_ref,
                   wgb_ref, wub_ref, wdb_ref, y_ref):
    n = pl.program_id(0)

    @pl.when(n < used_ref[0])
    def _():
        x = xs_ref[:, :D].astype(bf16)

        def expert(wg, wu, wd):
            a = _dot(x, wg[0])
            u = _dot(x, wu[0])
            act = a * (1.0 / (1.0 + jnp.exp(-a))) * u
            return _dot(act.astype(bf16), wd[0])

        ga = xs_ref[:, D:D + 1]
        gb = xs_ref[:, D + 1:D + 2]
        y_ref[...] = ga * expert(wga_ref, wua_ref, wda_ref) + gb * expert(wgb_ref, wub_ref, wdb_ref)

    @pl.when(n >= used_ref[0])
    def _():
        y_ref[...] = jnp.zeros_like(y_ref)


def _experts(ea, eb, used, xs, wg, wu, wd):
    P = xs.shape[0]
    nblk = P // ROW_BLOCK

    def xmap(n, ea, eb, used):
        return (jnp.minimum(n, used[0] - 1), 0)

    wa = lambda n, ea, eb, used: (ea[n], 0, 0)
    wb = lambda n, ea, eb, used: (eb[n], 0, 0)
    wgs = lambda m: pl.BlockSpec((1, D, D_EXPERT), m)
    wds = lambda m: pl.BlockSpec((1, D_EXPERT, D), m)
    grid_spec = pltpu.PrefetchScalarGridSpec(
        num_scalar_prefetch=3, grid=(nblk,),
        in_specs=[pl.BlockSpec((ROW_BLOCK, XW), xmap), wgs(wa), wgs(wa), wds(wa), wgs(wb), wgs(wb), wds(wb)],
        out_specs=pl.BlockSpec((ROW_BLOCK, D), lambda n, ea, eb, used: (n, 0)),
    )
    return pl.pallas_call(
        _expert_kernel, grid_spec=grid_spec,
        out_shape=jax.ShapeDtypeStruct((P, D), f32),
        compiler_params=_cparams(("arbitrary",)), name="experts",
    )(ea, eb, used, xs, wg, wu, wd, wg, wu, wd)


def _final_kernel(bk0_ref, rk0_ref, bkn_ref, rkn_ref, ps_ref, h_ref, ys_ref, g_ref, b_ref, o_ref,
                  buf, sems, *, alpha, tm):
    i = pl.program_id(0)
    n = pl.num_programs(0)

    def issue(bk_ref, rk_ref, slot):
        def body(u, _):
            for k in range(2):
                t = 2 * u + k
                s = ps_ref[bk_ref[t]] + rk_ref[t]
                pltpu.make_async_copy(ys_ref.at[pl.ds(s, 1)], buf.at[slot, pl.ds(t, 1)],
                                      sems.at[slot]).start(priority=k)
            return 0
        lax.fori_loop(0, tm // 2, body, 0)

    @pl.when(i == 0)
    def _():
        issue(bk0_ref, rk0_ref, 0)

    @pl.when(i + 1 < n)
    def _():
        issue(bkn_ref, rkn_ref, (i + 1) % 2)

    slot = i % 2
    pltpu.make_async_copy(ys_ref.at[pl.ds(0, tm)], buf.at[slot], sems.at[slot]).wait()
    o_ref[...] = _layer_norm(alpha * h_ref[...] + buf[slot], g_ref[...], b_ref[...])


def _final(bk, rk, ps, h2x, ys, g, b, alpha, tm=1024):
    T = h2x.shape[0]
    n = T // tm
    row = pl.BlockSpec((tm, D), lambda i: (i, 0))
    vec = pl.BlockSpec((1, D), lambda i: (0, 0))
    first = lambda: pl.BlockSpec((tm,), lambda i: (0,), memory_space=pltpu.SMEM)
    nxt = lambda: pl.BlockSpec((tm,), lambda i: (jnp.minimum(i + 1, n - 1),), memory_space=pltpu.SMEM)
    return pl.pallas_call(
        functools.partial(_final_kernel, alpha=alpha, tm=tm), grid=(n,),
        in_specs=[first(), first(), nxt(), nxt(), pl.BlockSpec(memory_space=pltpu.SMEM),
                  row, pl.BlockSpec(memory_space=pl.ANY), vec, vec],
        out_specs=row,
        out_shape=jax.ShapeDtypeStruct((T, D), f32),
        scratch_shapes=[pltpu.VMEM((2, tm, D), f32), pltpu.SemaphoreType.DMA((2,))],
        compiler_params=_cparams(("arbitrary",)), name="final_ln",
    )(bk, rk, bk, rk, ps, h2x, ys, g, b)


def _ln_kernel(h_ref, y_ref, g_ref, b_ref, o_ref, *, alpha):
    o_ref[...] = _layer_norm(alpha * h_ref[...] + y_ref[...], g_ref[...], b_ref[...])


def _final_ln(h2x, y, g, b, alpha, tm=512):
    T = y.shape[0]
    row = pl.BlockSpec((tm, D), lambda i: (i, 0))
    vec = pl.BlockSpec((1, D), lambda i: (0, 0))
    return pl.pallas_call(
        functools.partial(_ln_kernel, alpha=alpha), grid=(T // tm,),
        in_specs=[row, row, vec, vec], out_specs=row,
        out_shape=jax.ShapeDtypeStruct((T, D), f32),
        compiler_params=_cparams(("parallel",)), name="final_ln",
    )(h2x, y, g, b)


def _pair_tables():
    ea = np.zeros((LANES,), np.int32)
    eb = np.zeros((LANES,), np.int32)
    for g in range(N_GROUPS):
        k = 0
        for a in range(EPG):
            for b in range(a + 1, EPG):
                ea[g * N_PAIRS + k] = g * EPG + a
                eb[g * N_PAIRS + k] = g * EPG + b
                k += 1
    return ea, eb


_PAIR_A, _PAIR_B = _pair_tables()


def _layer(h, mem, positions, w_in, b_forget, sinks, w_mix_out, ln_mix_g, ln_mix_b,
           w_xq, w_xkv, w_xout, ln_x_g, ln_x_b, w_rg, b_rg, w_re, b_re,
           w_eg, w_eu, w_ed, ln_f_g, ln_f_b, alpha):
    B, S, _ = h.shape
    T = B * S
    x2 = h.reshape(T, D)
    pos2 = positions.reshape(T, 1).astype(i32)

    o = np.cumsum((0, FOX_W, FOX_W, FOX_W, N_FOX, SWA_Q_W, SWA_KV_W, SWA_KV_W))
    w_qf, w_kf, w_vf, w_fl, w_qs, w_ks, w_vs = (w_in[:, o[i]:o[i + 1]] for i in range(7))
    def regroup(a, axis):
        shp = a.shape
        a = jnp.moveaxis(a, axis, 0).reshape(N_SWA_KV, N_SWA // N_SWA_KV, HD, -1)
        return jnp.moveaxis(jnp.swapaxes(a, 0, 1).reshape(N_SWA * HD, -1), 0, axis).reshape(shp)

    w_all = jnp.concatenate([w_kf, regroup(w_qs, 1), w_ks, w_vs], axis=1).astype(bf16)
    wqt = w_qf.T.astype(bf16)
    wvt = w_vf.T.astype(bf16)
    wfl = w_fl.T.astype(bf16)
    bfc = b_forget.reshape(N_FOX, 1).astype(f32)
    half = HD // 2
    inv_freq = ROPE_THETA ** (-jnp.arange(half, dtype=f32) / half)
    invf = jnp.tile(inv_freq, LANES // half).reshape(1, LANES)
    w_out = jnp.concatenate([w_mix_out[:FOX_W], regroup(w_mix_out[FOX_W:], 0)], axis=0).astype(bf16)

    tq = 512
    qt, kf, vt, qs, ks, vs, lf = _in_proj(x2, pos2, w_all, wqt, wvt, wfl, bfc, invf, tq)
    c, ca = _cumsum(lf, S)
    c4 = c.reshape(N_FOX // 2, 2, T // tq, tq)
    r3 = lambda a: a.reshape(B, S, a.shape[-1])
    o_fox = _fox(qt, r3(kf), r3(ca), vt, c4, B, S, tq)
    o_swa = _swa(sinks.astype(f32), r3(qs), r3(ks), r3(vs), B, S)

    kx, vx = _kvproj(mem.reshape(-1, D), w_xkv.astype(bf16))

    wr = jnp.concatenate([w_rg, jnp.transpose(w_re, (1, 0, 2)).reshape(D, N_EXPERTS)], axis=1)
    wr = jnp.pad(wr, ((0, 0), (0, LANES - wr.shape[1]))).astype(f32)
    wrh = wr.astype(bf16)
    wrl = (wr - wrh.astype(f32)).astype(bf16)
    br = jnp.pad(jnp.concatenate([b_rg, b_re.reshape(-1)]), (0, LANES - N_GROUPS - N_EXPERTS))
    br = br.reshape(1, LANES).astype(f32)
    v2 = lambda a: a.reshape(1, D).astype(f32)
    h2x, bk8 = _mid(x2, o_fox.reshape(T, FOX_W), o_swa.reshape(T, SWA_Q_W), w_out,
                    v2(ln_mix_g), v2(ln_mix_b), w_xq.astype(bf16), kx, vx, w_xout.astype(bf16),
                    v2(ln_x_g), v2(ln_x_b), wrh, wrl, br, alpha, S)

    rank, cnt = _rank(bk8)
    bucket_i = bk8[0].astype(i32)
    rank_i = rank[0].astype(i32)
    counts = cnt[:, 0].astype(i32)
    padded = ((counts + ROW_BLOCK - 1) // ROW_BLOCK) * ROW_BLOCK
    pad_end = jnp.cumsum(padded)
    pad_start = (pad_end - padded).astype(i32)
    P = T + N_BUCKETS * ROW_BLOCK
    nblk = P // ROW_BLOCK
    used = (pad_end[-1] // ROW_BLOCK).astype(i32).reshape(1)
    blk_row = jnp.arange(nblk, dtype=i32)[:, None] * ROW_BLOCK
    blk_bucket = jnp.minimum(jnp.sum((pad_end[None, :] <= blk_row).astype(i32), axis=1), N_BUCKETS - 1)
    pick = (blk_bucket[:, None] == jnp.arange(LANES, dtype=i32)[None, :]).astype(i32)
    blk_a = jnp.sum(pick * jnp.asarray(_PAIR_A)[None, :], axis=1)
    blk_b = jnp.sum(pick * jnp.asarray(_PAIR_B)[None, :], axis=1)

    dest = _dest(bk8, rank, pad_start.astype(f32).reshape(LANES, 1))[0]
    xs = _sc_gather_rows(_sc_invert(dest, P), h2x)
    ys = _experts(blk_a, blk_b, used, xs, w_eg.astype(bf16), w_eu.astype(bf16), w_ed.astype(bf16))
    y = _sc_gather_rows(dest, ys)
    out = _final_ln(h2x, y, v2(ln_f_g), v2(ln_f_b), alpha)
    return out.reshape(B, S, D)


def kernel(x, mem, positions, w_in, b_forget, sinks, w_mix_out, ln_mix_g, ln_mix_b, w_xq, w_xkv, w_xout,
           ln_x_g, ln_x_b, w_route_group, b_route_group, w_route_expert, b_route_expert,
           w_exp_gate, w_exp_up, w_exp_down, ln_ffn_g, ln_ffn_b):
    depth = w_in.shape[0]
    alpha = (2.0 * depth) ** 0.25
    h = x
    for l in range(depth):
        h = _layer(h, mem, positions, w_in[l], b_forget[l], sinks[l], w_mix_out[l], ln_mix_g[l], ln_mix_b[l],
                   w_xq[l], w_xkv[l], w_xout[l], ln_x_g[l], ln_x_b[l], w_route_group[l], b_route_group[l],
                   w_route_expert[l], b_route_expert[l], w_exp_gate[l], w_exp_up[l], w_exp_down[l],
                   ln_ffn_g[l], ln_ffn_b[l], alpha)
    return h
```

```python
import functools

import jax
import jax.numpy as jnp
import numpy as np
from jax import lax
from jax.experimental import pallas as pl
from jax.experimental.pallas import tpu as pltpu
from jax.experimental.pallas import tpu_sc as plsc

f32 = jnp.float32
bf16 = jnp.bfloat16
i32 = jnp.int32

D = 1024
HD = 64
N_FOX = 8
N_SWA = 8
N_SWA_KV = 2
FOX_W = 512
SWA_Q_W = 512
SWA_KV_W = 128
WINDOW = 128
ROPE_THETA = 10000.0
N_XH = 4
XHD = 256
N_GROUPS = 4
EPG = 8
N_EXPERTS = 32
D_EXPERT = 512
LN_EPS = 1e-5
NEG = -1e30
LOG2E = 1.4426950408889634
L_ROW = (HD, 0)

SC_CORES = 2
SC_SUBCORES = 16
SC_LANES = 16
LANES = 128
ROW_BLOCK = 128
N_PAIRS = EPG * (EPG - 1) // 2
N_BUCKETS = N_GROUPS * N_PAIRS
XW = D + LANES
VMEM_LIMIT = 56 * 1024 * 1024


def _cparams(sem):
    return pltpu.CompilerParams(dimension_semantics=sem, vmem_limit_bytes=VMEM_LIMIT)


def _layer_norm(v, g, b):
    mu = jnp.mean(v, axis=-1, keepdims=True)
    c = v - mu
    var = jnp.mean(c * c, axis=-1, keepdims=True)
    return c * lax.rsqrt(var + LN_EPS) * g + b


def _dot(a, b):
    return jnp.dot(a, b, preferred_element_type=f32)


def _dot_nt(a, b):
    return lax.dot_general(a, b, (((1,), (1,)), ((), ())), preferred_element_type=f32)


def _inproj_kernel(x_ref, pos_ref, w_ref, wqt_ref, wvt_ref, wfl_ref, bf_ref, invf_ref,
                   qt_ref, kf_ref, vt_ref, qs_ref, ks_ref, vs_ref, lf_ref):
    tm = x_ref.shape[0]
    xb = x_ref[...].astype(bf16)

    def proj(lo, hi):
        return _dot(xb, w_ref[:, lo:hi])

    qt_ref[0] = (_dot_nt(wqt_ref[...], xb) * (0.125 * LOG2E)).astype(bf16)
    vt_ref[0] = _dot_nt(wvt_ref[...], xb).astype(bf16)
    kf_ref[...] = proj(0, 512).astype(bf16)

    ang = pos_ref[...].astype(f32) * invf_ref[...]
    cos = jnp.cos(ang)
    sin = jnp.sin(ang)
    lane = lax.broadcasted_iota(i32, (tm, LANES), 1)
    lo_half = (lane % HD) < (HD // 2)
    sin_s = jnp.where(lo_half, -sin, sin)

    def rope(z):
        rot = jnp.where(lo_half, pltpu.roll(z, LANES - HD // 2, 1), pltpu.roll(z, HD // 2, 1))
        return z * cos + rot * sin_s

    zq = proj(512, 1024)
    for g in range(4):
        sl = slice(g * LANES, (g + 1) * LANES)
        qs_ref[:, sl] = (rope(zq[:, sl]) * 0.125).astype(bf16)
    ks_ref[...] = rope(proj(1024, 1152)).astype(bf16)
    vs_ref[...] = proj(1152, 1280).astype(bf16)

    z = _dot_nt(wfl_ref[...], xb) + bf_ref[...]
    lf_ref[...] = jnp.minimum(z, 0.0) - jnp.log(1.0 + jnp.exp(-jnp.abs(z)))


def _in_proj(x2, pos2, w_all, wqt, wvt, wfl, bfc, invf, tm):
    T = x2.shape[0]
    row = lambda w: pl.BlockSpec((tm, w), lambda i: (i, 0))
    full = lambda a: pl.BlockSpec(a.shape, lambda i: (0,) * a.ndim)
    fmaj = pl.BlockSpec((1, FOX_W, tm), lambda i: (i, 0, 0))
    return pl.pallas_call(
        _inproj_kernel,
        grid=(T // tm,),
        in_specs=[row(D), row(1), full(w_all), full(wqt), full(wvt), full(wfl), full(bfc), full(invf)],
        out_specs=[fmaj, row(512), fmaj, row(512), row(128), row(128),
                   pl.BlockSpec((N_FOX, tm), lambda i: (0, i))],
        out_shape=[jax.ShapeDtypeStruct((T // tm, FOX_W, tm), bf16), jax.ShapeDtypeStruct((T, 512), bf16),
                   jax.ShapeDtypeStruct((T // tm, FOX_W, tm), bf16), jax.ShapeDtypeStruct((T, 512), bf16),
                   jax.ShapeDtypeStruct((T, 128), bf16), jax.ShapeDtypeStruct((T, 128), bf16),
                   jax.ShapeDtypeStruct((N_FOX, T), f32)],
        compiler_params=_cparams(("parallel",)),
        name="in_proj",
    )(x2, pos2, w_all, wqt, wvt, wfl, bfc, invf)


def _cumsum_kernel(lf_ref, c_ref, ca_ref):
    S = lf_ref.shape[1]
    ch = 256
    r = lax.broadcasted_iota(i32, (ch, ch), 0)
    c = lax.broadcasted_iota(i32, (ch, ch), 1)
    tri = (r <= c).astype(f32)
    eye = (r == c).astype(bf16)
    stacked = jnp.concatenate([lf_ref[:, j * ch:(j + 1) * ch] for j in range(S // ch)], axis=0)
    local = jnp.dot(stacked, tri, precision=lax.Precision.HIGHEST, preferred_element_type=f32)
    carry = jnp.zeros((N_FOX, 1), f32)
    for j in range(S // ch):
        cc = local[j * N_FOX:(j + 1) * N_FOX] + carry
        carry = cc[:, ch - 1:ch]
        c2 = cc * LOG2E
        c_ref[:, j * ch:(j + 1) * ch] = c2
        neg = -c2
        hi = neg.astype(bf16)
        r1 = neg - hi.astype(f32)
        mid = r1.astype(bf16)
        lo = (r1 - mid.astype(f32)).astype(bf16)
        terms = jnp.concatenate([hi, mid, lo, jnp.zeros((LANES - 3 * N_FOX, ch), bf16)], axis=0)
        ca_ref[j * ch:(j + 1) * ch, :] = _dot_nt(eye, terms).astype(bf16)


def _cumsum(lf, S):
    T = lf.shape[1]
    spec = pl.BlockSpec((N_FOX, S), lambda b: (0, b))
    return pl.pallas_call(
        _cumsum_kernel, grid=(T // S,), in_specs=[spec],
        out_specs=[spec, pl.BlockSpec((S, LANES), lambda b: (b, 0))],
        out_shape=[jax.ShapeDtypeStruct((N_FOX, T), f32), jax.ShapeDtypeStruct((T, LANES), bf16)],
        compiler_params=_cparams(("parallel",)), name="cumsum",
    )(lf)


def _fox_kernel(qt_ref, k_ref, ca_ref, vt_ref, c_ref, o_ref, t0_ref, t1_ref, *, tq):
    hp = pl.program_id(1)
    i = pl.program_id(2)
    qt = qt_ref[0]
    row = lax.broadcasted_iota(i32, (LANES, tq), 0)
    is_a = row < HD
    zero = jnp.zeros_like(qt)
    q_ops = []
    for h in range(2):
        ones = jnp.where(((row & 7) == 2 * hp + h) & (row < 3 * N_FOX), 1.0, 0.0).astype(bf16)
        qh = jnp.where(is_a, qt, zero) if h == 0 else jnp.where(is_a, zero, qt)
        q_ops.append(jnp.concatenate([qh, ones], axis=0))
    kr = lax.broadcasted_iota(i32, (tq, tq), 0)
    qc = lax.broadcasted_iota(i32, (tq, tq), 1)
    causal = kr <= qc
    cq = [c_ref[0, h, pl.ds(i, 1), :] for h in range(2)]

    def scores(j, t_ref):
        off = pl.multiple_of(j * tq, tq)
        kblk = jnp.concatenate([k_ref[0, pl.ds(off, tq), :], ca_ref[0, pl.ds(off, tq), :]], axis=1)
        for h in range(2):
            t_ref[h] = _dot(kblk, q_ops[h])

    keep = [jnp.where(is_a, 1.0, 0.0).astype(bf16), jnp.where(is_a, 0.0, 1.0).astype(bf16)]
    ones_row = [jnp.where(row == L_ROW[h], 1.0, 0.0).astype(bf16) for h in range(2)]

    def softmax_pv(j, t_ref, carry, masked):
        vt = vt_ref[j]
        vts = [vt * keep[h] + ones_row[h] for h in range(2)]
        new = []
        for h in range(2):
            m, acc = carry[h]
            t = t_ref[h]
            if masked:
                t = jnp.where(causal, t, NEG)
            m_new = jnp.maximum(m, jnp.max(t, axis=0, keepdims=True) + cq[h])
            alpha = jnp.exp2(m - m_new)
            p = jnp.exp2(t + (cq[h] - m_new))
            acc = alpha * acc + _dot(vts[h], p.astype(bf16))
            new.append((m_new, acc))
        return tuple(new)

    def pair(k, carry):
        j = 2 * k
        scores(j + 1, t1_ref)
        carry = softmax_pv(j, t0_ref, carry, False)
        scores(j + 2, t0_ref)
        return softmax_pv(j + 1, t1_ref, carry, False)

    def odd_tail(carry):
        scores(i, t1_ref)
        carry = softmax_pv(i - 1, t0_ref, carry, False)
        return softmax_pv(i, t1_ref, carry, True)

    def even_tail(carry):
        return softmax_pv(i, t0_ref, carry, True)

    init = tuple((jnp.full((1, tq), NEG, f32), jnp.zeros((LANES, tq), f32)) for _ in range(2))
    scores(0, t0_ref)
    carry = lax.fori_loop(0, i // 2, pair, init)
    (_, acca), (_, accb) = lax.cond(i % 2 == 1, odd_tail, even_tail, carry)
    la = acca[L_ROW[0]:L_ROW[0] + 1, :]
    lb = accb[L_ROW[1]:L_ROW[1] + 1, :]
    ot = jnp.where(is_a, acca / la, accb / lb)
    o_ref[0] = jnp.transpose(ot).astype(bf16)


def _fox(qt, kf, ca, vt, c4, B, S, tq):
    nq = S // tq
    kernel = functools.partial(_fox_kernel, tq=tq)
    return pl.pallas_call(
        kernel,
        grid=(B, N_FOX // 2, nq),
        in_specs=[
            pl.BlockSpec((1, LANES, tq), lambda b, hp, i: (b * nq + i, hp, 0)),
            pl.BlockSpec((1, S, LANES), lambda b, hp, i: (b, 0, hp)),
            pl.BlockSpec((1, S, LANES), lambda b, hp, i: (b, 0, 0)),
            pl.BlockSpec((nq, LANES, tq), lambda b, hp, i: (b, hp, 0)),
            pl.BlockSpec((1, 2, nq, tq), lambda b, hp, i: (hp, 0, b, 0)),
        ],
        out_specs=pl.BlockSpec((1, tq, LANES), lambda b, hp, i: (b, i, hp)),
        out_shape=jax.ShapeDtypeStruct((B, S, FOX_W), bf16),
        scratch_shapes=[pltpu.VMEM((2, tq, tq), f32), pltpu.VMEM((2, tq, tq), f32)],
        compiler_params=_cparams(("parallel", "parallel", "arbitrary")),
        name="fox",
    )(qt, kf, ca, vt, c4)


def _swa_kernel(sink_ref, q_ref, k_ref, v_ref, o_ref, *, tq):
    W = WINDOW
    nsub = tq // W
    n0 = pl.program_id(1) * nsub
    lane = lax.broadcasted_iota(i32, (W, LANES), 1)
    is0 = lane < HD
    rows = lax.broadcasted_iota(i32, (4 * W, 2 * W), 0)
    cols = lax.broadcasted_iota(i32, (4 * W, 2 * W), 1)
    rgrp = lax.broadcasted_iota(i32, (4 * W, 1), 0) // W
    for r in range(nsub):
        nb = n0 + r
        kstart = pl.multiple_of(jnp.maximum(nb * W - W, 0), W)
        ks = k_ref[0, pl.ds(kstart, 2 * W), :]
        vs = v_ref[0, pl.ds(kstart, 2 * W), :]
        qpos = nb * W + rows % W
        kpos = kstart + cols
        valid = (kpos <= qpos) & (qpos - kpos < W)
        outs = []
        for kv in range(2):
            keep = is0 if kv == 0 else jnp.logical_not(is0)
            parts = []
            for g in range(4):
                qg = q_ref[0, r * W:(r + 1) * W, g * LANES:(g + 1) * LANES]
                parts.append(jnp.where(keep, qg, jnp.zeros_like(qg)))
            qstack = jnp.concatenate(parts, axis=0)
            s = jnp.where(valid, _dot_nt(qstack, ks), NEG)
            sink = jnp.zeros((4 * W, 1), f32)
            for g in range(4):
                sink = jnp.where(rgrp == g, sink_ref[kv * 4 + g], sink)
            m = jnp.maximum(jnp.max(s, axis=1, keepdims=True), sink)
            e = jnp.exp(s - m)
            den = jnp.sum(e, axis=1, keepdims=True) + jnp.exp(sink - m)
            outs.append(_dot(e.astype(bf16), vs) / den)
        for g in range(4):
            og = jnp.where(is0, outs[0][g * W:(g + 1) * W], outs[1][g * W:(g + 1) * W])
            o_ref[0, r * W:(r + 1) * W, g * LANES:(g + 1) * LANES] = og.astype(bf16)


def _swa(sinks, qs, ks, vs, B, S, tq=512):
    kernel = functools.partial(_swa_kernel, tq=tq)
    return pl.pallas_call(
        kernel,
        grid=(B, S // tq),
        in_specs=[
            pl.BlockSpec(memory_space=pltpu.SMEM),
            pl.BlockSpec((1, tq, SWA_Q_W), lambda b, i: (b, i, 0)),
            pl.BlockSpec((1, S, SWA_KV_W), lambda b, i: (b, 0, 0)),
            pl.BlockSpec((1, S, SWA_KV_W), lambda b, i: (b, 0, 0)),
        ],
        out_specs=pl.BlockSpec((1, tq, SWA_Q_W), lambda b, i: (b, i, 0)),
        out_shape=jax.ShapeDtypeStruct((B, S, SWA_Q_W), bf16),
        compiler_params=_cparams(("parallel", "arbitrary")),
        name="swa",
    )(sinks, qs, ks, vs)


def _kvproj_kernel(m_ref, w_ref, k_ref, v_ref):
    mb = m_ref[...].astype(bf16)
    k_ref[...] = _dot(mb, w_ref[:, :D]).astype(bf16)
    v_ref[...] = _dot(mb, w_ref[:, D:]).astype(bf16)


def _kvproj(mem2, w_xkv, tm=512):
    R = mem2.shape[0]
    row = pl.BlockSpec((tm, D), lambda i: (i, 0))
    return pl.pallas_call(
        _kvproj_kernel, grid=(R // tm,),
        in_specs=[row, pl.BlockSpec(w_xkv.shape, lambda i: (0, 0))],
        out_specs=[row, row],
        out_shape=[jax.ShapeDtypeStruct((R, D), bf16)] * 2,
        compiler_params=_cparams(("parallel",)), name="kvproj",
    )(mem2, w_xkv)


def _mid_kernel(x_ref, of_ref, os_ref, wo_ref, g1_ref, b1_ref, wq_ref, k_ref, v_ref,
                wxo_ref, g2_ref, b2_ref, wrh_ref, wrl_ref, br_ref,
                h2_ref, bk_ref, oc_ref, *, alpha):
    tm = x_ref.shape[0]
    mix = _dot(of_ref[...], wo_ref[:FOX_W, :]) + _dot(os_ref[...], wo_ref[FOX_W:, :])
    h1 = _layer_norm(alpha * x_ref[...] + mix, g1_ref[...], b1_ref[...])

    q = (_dot(h1.astype(bf16), wq_ref[...]) * 0.0625).astype(bf16)
    for h in range(N_XH):
        sl = slice(h * XHD, (h + 1) * XHD)
        s = _dot_nt(q[:, sl], k_ref[:, sl])
        e = jnp.exp(s - jnp.max(s, axis=1, keepdims=True))
        p = e / jnp.sum(e, axis=1, keepdims=True)
        oc_ref[:, sl] = _dot(p.astype(bf16), v_ref[:, sl]).astype(bf16)
    xo = _dot(oc_ref[...], wxo_ref[...])
    h2 = _layer_norm(alpha * h1 + xo, g2_ref[...], b2_ref[...])
    h2_ref[:, :D] = h2

    hh = h2.astype(bf16)
    hl = (h2 - hh.astype(f32)).astype(bf16)
    lg = _dot(hh, wrh_ref[...]) + _dot(hl, wrh_ref[...]) + _dot(hh, wrl_ref[...]) + br_ref[...]

    lane = lax.broadcasted_iota(i32, (tm, LANES), 1).astype(f32)
    big = float(LANES)

    def first_max(vals, mask):
        vm = jnp.where(mask, vals, NEG)
        top = jnp.max(vm, axis=1, keepdims=True)
        idx = jnp.min(jnp.where(mask & (vm == top), lane, big), axis=1, keepdims=True)
        return top, idx

    gmask = lane < float(N_GROUPS)
    gmax, gidx = first_max(lg, gmask)
    g_val = 1.0 / jnp.sum(jnp.where(gmask, jnp.exp(lg - gmax), 0.0), axis=1, keepdims=True)
    lo = float(N_GROUPS) + float(EPG) * gidx
    emask = (lane >= lo) & (lane < lo + float(EPG))
    v1, i1 = first_max(lg, emask)
    v2, i2 = first_max(lg, emask & (lane != i1))
    ex = jnp.exp(v2 - v1)
    w1 = g_val * (1.0 / (1.0 + ex))
    w2 = g_val * (ex / (1.0 + ex))
    e1 = i1 - lo
    e2 = i2 - lo
    first_low = e1 < e2
    ea = jnp.where(first_low, e1, e2)
    eb = jnp.where(first_low, e2, e1)
    ga = jnp.where(first_low, w1, w2)
    gb = jnp.where(first_low, w2, w1)
    pidx = ea * float(EPG - 1) - ea * (ea - 1.0) * 0.5 + (eb - ea - 1.0)
    bucket = gidx * float(N_PAIRS) + pidx

    lane_i = lax.broadcasted_iota(i32, (tm, LANES), 1)
    h2_ref[:, D:] = jnp.where(lane_i == 0, ga, jnp.where(lane_i == 1, gb, 0.0))
    bk_t = jnp.transpose(jnp.broadcast_to(bucket, (tm, LANES)))
    bk_ref[...] = bk_t[:8, :]


def _mid(x2, of2, os2, w_out, g1, b1, wq, kx, vx, wxo, g2, b2, wrh, wrl, br, alpha, S, tm=512):
    T = x2.shape[0]
    M = kx.shape[0] // (T // S)
    per_b = S // tm
    row = lambda w: pl.BlockSpec((tm, w), lambda i: (i, 0))
    full = lambda a: pl.BlockSpec(a.shape, lambda i: (0,) * a.ndim)
    kvspec = pl.BlockSpec((M, D), lambda i: (i // per_b, 0))
    kernel = functools.partial(_mid_kernel, alpha=alpha)
    return pl.pallas_call(
        kernel,
        grid=(T // tm,),
        in_specs=[row(D), row(512), row(512), full(w_out), full(g1), full(b1), full(wq),
                  kvspec, kvspec, full(wxo), full(g2), full(b2), full(wrh), full(wrl), full(br)],
        out_specs=[row(XW), pl.BlockSpec((8, tm), lambda i: (0, i))],
        out_shape=[jax.ShapeDtypeStruct((T, XW), f32), jax.ShapeDtypeStruct((8, T), f32)],
        scratch_shapes=[pltpu.VMEM((tm, D), bf16)],
        compiler_params=_cparams(("parallel",)),
        name="mid",
    )(x2, of2, os2, w_out, g1, b1, wq, kx, vx, wxo, g2, b2, wrh, wrl, br)


def _rank_kernel(bk_ref, rank_ref, cnt_ref, carry_ref, *, chunk):
    sub = 256

    @pl.when(pl.program_id(0) == 0)
    def _():
        carry_ref[...] = jnp.zeros_like(carry_ref)

    r = lax.broadcasted_iota(i32, (sub, sub), 0)
    c = lax.broadcasted_iota(i32, (sub, sub), 1)
    before = (r < c).astype(bf16)
    bid = lax.broadcasted_iota(i32, (LANES, sub), 0).astype(f32)
    carry = carry_ref[...]
    for j in range(chunk // sub):
        bk = bk_ref[0:1, j * sub:(j + 1) * sub]
        hit = bid == bk
        oh = jnp.where(hit, 1.0, 0.0)
        prior = _dot(oh.astype(bf16), before) + carry
        rank_ref[:, j * sub:(j + 1) * sub] = jnp.sum(jnp.where(hit, prior, 0.0), axis=0, keepdims=True)
        carry = carry + jnp.sum(oh, axis=1, keepdims=True)
    carry_ref[...] = carry
    cnt_ref[...] = carry


def _rank(bk8, chunk=2048):
    T = bk8.shape[1]
    kernel = functools.partial(_rank_kernel, chunk=chunk)
    return pl.pallas_call(
        kernel, grid=(T // chunk,),
        in_specs=[pl.BlockSpec((8, chunk), lambda i: (0, i))],
        out_specs=[pl.BlockSpec((1, chunk), lambda i: (0, i)),
                   pl.BlockSpec((LANES, 1), lambda i: (0, 0))],
        out_shape=[jax.ShapeDtypeStruct((1, T), f32), jax.ShapeDtypeStruct((LANES, 1), f32)],
        scratch_shapes=[pltpu.VMEM((LANES, 1), f32)],
        compiler_params=_cparams(("arbitrary",)), name="rank",
    )(bk8)


def _dest_kernel(bk_ref, rank_ref, ps_ref, dest_ref):
    chunk = bk_ref.shape[1]
    bid = lax.broadcasted_iota(i32, (LANES, chunk), 0).astype(f32)
    start = jnp.sum(jnp.where(bid == bk_ref[0:1, :], ps_ref[...], 0.0), axis=0, keepdims=True)
    dest_ref[...] = (start + rank_ref[...]).astype(i32)


def _dest(bk8, rank, ps_col, chunk=2048):
    T = bk8.shape[1]
    return pl.pallas_call(
        _dest_kernel, grid=(T // chunk,),
        in_specs=[pl.BlockSpec((8, chunk), lambda i: (0, i)), pl.BlockSpec((1, chunk), lambda i: (0, i)),
                  pl.BlockSpec((LANES, 1), lambda i: (0, 0))],
        out_specs=pl.BlockSpec((1, chunk), lambda i: (0, i)),
        out_shape=jax.ShapeDtypeStruct((1, T), i32),
        compiler_params=_cparams(("parallel",)), name="dest",
    )(bk8, rank, ps_col)


def _sc_invert(dest, n_rows):
    T = dest.shape[0]
    assert T & (T - 1) == 0
    lanes = SC_LANES
    mesh = plsc.VectorSubcoreMesh(core_axis_name="core", subcore_axis_name="subcore",
                                  num_cores=SC_CORES, num_subcores=SC_SUBCORES)

    @functools.partial(pl.kernel, out_type=jax.ShapeDtypeStruct((n_rows,), i32), mesh=mesh,
                       scratch_types=[pltpu.VMEM((T,), i32), pltpu.VMEM((n_rows,), i32)],
                       compiler_params=pltpu.CompilerParams(needs_layout_passes=False),
                       name="sc_invert")
    def k(dest_hbm, out_hbm, dest_v, table_v):
        wid = lax.axis_index("subcore") * SC_CORES + lax.axis_index("core")

        @pl.when(wid == 0)
        def _():
            pltpu.sync_copy(dest_hbm, dest_v)
            lane = lax.iota(i32, lanes)

            @pl.loop(0, n_rows // lanes)
            def _(j):
                off = pl.multiple_of(j * lanes, lanes)
                table_v[pl.ds(off, lanes)] = (lane + off) & (T - 1)


            @pl.loop(0, T // lanes)
            def _(j):
                off = pl.multiple_of(j * lanes, lanes)
                plsc.store_scatter(table_v, [dest_v[pl.ds(off, lanes)]], lane + off)

            pltpu.sync_copy(table_v, out_hbm)

    return k(dest)


def _sc_gather_rows(idx, src, chunk=32):
    n = idx.shape[0]
    w = src.shape[1]
    workers = SC_CORES * SC_SUBCORES
    per_worker = n // workers
    mesh = plsc.VectorSubcoreMesh(core_axis_name="core", subcore_axis_name="subcore",
                                  num_cores=SC_CORES, num_subcores=SC_SUBCORES)

    n_chunks = per_worker // chunk
    assert n_chunks % 2 == 0

    @functools.partial(pl.kernel, out_type=jax.ShapeDtypeStruct((n, w), src.dtype), mesh=mesh,
                       scratch_types=[pltpu.VMEM((per_worker,), i32), pltpu.VMEM((2, chunk, w), src.dtype),
                                      pltpu.SemaphoreType.DMA((2,)), pltpu.SemaphoreType.DMA((2,))],
                       name="sc_gather_rows")
    def k(src_hbm, idx_hbm, out_hbm, idx_v, rows_v, gsem, wsem):
        wid = lax.axis_index("subcore") * SC_CORES + lax.axis_index("core")
        base = wid * per_worker
        pltpu.sync_copy(idx_hbm.at[pl.ds(base, per_worker)], idx_v)

        def gather(c, slot):
            rows = idx_v.at[pl.ds(pl.multiple_of(c * chunk, chunk), chunk)]
            return pltpu.make_async_copy(src_hbm.at[rows], rows_v.at[slot], gsem.at[slot])

        def write(c, slot):
            out = out_hbm.at[pl.ds(pl.multiple_of(base + c * chunk, chunk), chunk)]
            return pltpu.make_async_copy(rows_v.at[slot], out, wsem.at[slot])

        gather(0, 0).start()

        @pl.loop(0, n_chunks // 2)
        def _(pair):
            c = 2 * pair

            @pl.when(pair > 0)
            def _():
                write(c - 1, 1).wait()

            gather(c + 1, 1).start()
            gather(c, 0).wait()
            write(c, 0).start()

            @pl.when(c + 2 < n_chunks)
            def _():
                write(c, 0).wait()
                gather(c + 2, 0).start()

            gather(c + 1, 1).wait()
            write(c + 1, 1).start()

        write(n_chunks - 2, 0).wait()
        write(n_chunks - 1, 1).wait()

    return k(src, idx)


def _expert_kernel(ea_ref, eb_ref, used_ref, xs_ref, wga_ref, wua_ref, wda_ref,
                   wgb_ref, wub_ref, wdb_ref, y_ref):
    n = pl.program_id(0)

    @pl.when(n < used_ref[0])
    def _():
        x = xs_ref[:, :D].astype(bf16)

        def expert(wg, wu, wd):
            a = _dot(x, wg[0])
            u = _dot(x, wu[0])
            act = a * (1.0 / (1.0 + jnp.exp(-a))) * u
            return _dot(act.astype(bf16), wd[0])

        ga = xs_ref[:, D:D + 1]
        gb = xs_ref[:, D + 1:D + 2]
        y_ref[...] = ga * expert(wga_ref, wua_ref, wda_ref) + gb * expert(wgb_ref, wub_ref, wdb_ref)

    @pl.when(n >= used_ref[0])
    def _():
        y_ref[...] = jnp.zeros_like(y_ref)


def _experts(ea, eb, used, xs, wg, wu, wd):
    P = xs.shape[0]
    nblk = P // ROW_BLOCK

    def xmap(n, ea, eb, used):
        return (jnp.minimum(n, used[0] - 1), 0)

    wa = lambda n, ea, eb, used: (ea[n], 0, 0)
    wb = lambda n, ea, eb, used: (eb[n], 0, 0)
    wgs = lambda m: pl.BlockSpec((1, D, D_EXPERT), m)
    wds = lambda m: pl.BlockSpec((1, D_EXPERT, D), m)
    grid_spec = pltpu.PrefetchScalarGridSpec(
        num_scalar_prefetch=3, grid=(nblk,),
        in_specs=[pl.BlockSpec((ROW_BLOCK, XW), xmap), wgs(wa), wgs(wa), wds(wa), wgs(wb), wgs(wb), wds(wb)],
        out_specs=pl.BlockSpec((ROW_BLOCK, D), lambda n, ea, eb, used: (n, 0)),
    )
    return pl.pallas_call(
        _expert_kernel, grid_spec=grid_spec,
        out_shape=jax.ShapeDtypeStruct((P, D), f32),
        compiler_params=_cparams(("arbitrary",)), name="experts",
    )(ea, eb, used, xs, wg, wu, wd, wg, wu, wd)


def _ln_kernel(h_ref, y_ref, g_ref, b_ref, o_ref, *, alpha):
    o_ref[...] = _layer_norm(alpha * h_ref[...] + y_ref[...], g_ref[...], b_ref[...])


def _final_ln(h2x, y, g, b, alpha, tm=512):
    T = y.shape[0]
    row = pl.BlockSpec((tm, D), lambda i: (i, 0))
    vec = pl.BlockSpec((1, D), lambda i: (0, 0))
    return pl.pallas_call(
        functools.partial(_ln_kernel, alpha=alpha), grid=(T // tm,),
        in_specs=[row, row, vec, vec], out_specs=row,
        out_shape=jax.ShapeDtypeStruct((T, D), f32),
        compiler_params=_cparams(("parallel",)), name="final_ln",
    )(h2x, y, g, b)


def _pair_tables():
    ea = np.zeros((LANES,), np.int32)
    eb = np.zeros((LANES,), np.int32)
    for g in range(N_GROUPS):
        k = 0
        for a in range(EPG):
            for b in range(a + 1, EPG):
                ea[g * N_PAIRS + k] = g * EPG + a
                eb[g * N_PAIRS + k] = g * EPG + b
                k += 1
    return ea, eb


_PAIR_A, _PAIR_B = _pair_tables()


def _layer(h, mem, positions, w_in, b_forget, sinks, w_mix_out, ln_mix_g, ln_mix_b,
           w_xq, w_xkv, w_xout, ln_x_g, ln_x_b, w_rg, b_rg, w_re, b_re,
           w_eg, w_eu, w_ed, ln_f_g, ln_f_b, alpha):
    B, S, _ = h.shape
    T = B * S
    x2 = h.reshape(T, D)
    pos2 = positions.reshape(T, 1).astype(i32)

    o = np.cumsum((0, FOX_W, FOX_W, FOX_W, N_FOX, SWA_Q_W, SWA_KV_W, SWA_KV_W))
    w_qf, w_kf, w_vf, w_fl, w_qs, w_ks, w_vs = (w_in[:, o[i]:o[i + 1]] for i in range(7))
    def regroup(a, axis):
        shp = a.shape
        a = jnp.moveaxis(a, axis, 0).reshape(N_SWA_KV, N_SWA // N_SWA_KV, HD, -1)
        return jnp.moveaxis(jnp.swapaxes(a, 0, 1).reshape(N_SWA * HD, -1), 0, axis).reshape(shp)

    w_all = jnp.concatenate([w_kf, regroup(w_qs, 1), w_ks, w_vs], axis=1).astype(bf16)
    wqt = w_qf.T.astype(bf16)
    wvt = w_vf.T.astype(bf16)
    wfl = w_fl.T.astype(bf16)
    bfc = b_forget.reshape(N_FOX, 1).astype(f32)
    half = HD // 2
    inv_freq = ROPE_THETA ** (-jnp.arange(half, dtype=f32) / half)
    invf = jnp.tile(inv_freq, LANES // half).reshape(1, LANES)
    w_out = jnp.concatenate([w_mix_out[:FOX_W], regroup(w_mix_out[FOX_W:], 0)], axis=0).astype(bf16)

    tq = 512
    qt, kf, vt, qs, ks, vs, lf = _in_proj(x2, pos2, w_all, wqt, wvt, wfl, bfc, invf, tq)
    c, ca = _cumsum(lf, S)
    c4 = c.reshape(N_FOX // 2, 2, T // tq, tq)
    r3 = lambda a: a.reshape(B, S, a.shape[-1])
    o_fox = _fox(qt, r3(kf), r3(ca), vt, c4, B, S, tq)
    o_swa = _swa(sinks.astype(f32), r3(qs), r3(ks), r3(vs), B, S)

    kx, vx = _kvproj(mem.reshape(-1, D), w_xkv.astype(bf16))

    wr = jnp.concatenate([w_rg, jnp.transpose(w_re, (1, 0, 2)).reshape(D, N_EXPERTS)], axis=1)
    wr = jnp.pad(wr, ((0, 0), (0, LANES - wr.shape[1]))).astype(f32)
    wrh = wr.astype(bf16)
    wrl = (wr - wrh.astype(f32)).astype(bf16)
    br = jnp.pad(jnp.concatenate([b_rg, b_re.reshape(-1)]), (0, LANES - N_GROUPS - N_EXPERTS))
    br = br.reshape(1, LANES).astype(f32)
    v2 = lambda a: a.reshape(1, D).astype(f32)
    h2x, bk8 = _mid(x2, o_fox.reshape(T, FOX_W), o_swa.reshape(T, SWA_Q_W), w_out,
                    v2(ln_mix_g), v2(ln_mix_b), w_xq.astype(bf16), kx, vx, w_xout.astype(bf16),
                    v2(ln_x_g), v2(ln_x_b), wrh, wrl, br, alpha, S)

    rank, cnt = _rank(bk8)
    counts = cnt[:, 0].astype(i32)
    padded = ((counts + ROW_BLOCK - 1) // ROW_BLOCK) * ROW_BLOCK
    pad_end = jnp.cumsum(padded)
    pad_start = (pad_end - padded).astype(i32)
    P = T + N_BUCKETS * ROW_BLOCK
    nblk = P // ROW_BLOCK
    used = (pad_end[-1] // ROW_BLOCK).astype(i32).reshape(1)
    blk_row = jnp.arange(nblk, dtype=i32)[:, None] * ROW_BLOCK
    blk_bucket = jnp.minimum(jnp.sum((pad_end[None, :] <= blk_row).astype(i32), axis=1), N_BUCKETS - 1)
    pick = (blk_bucket[:, None] == jnp.arange(LANES, dtype=i32)[None, :]).astype(i32)
    blk_a = jnp.sum(pick * jnp.asarray(_PAIR_A)[None, :], axis=1)
    blk_b = jnp.sum(pick * jnp.asarray(_PAIR_B)[None, :], axis=1)

    dest = _dest(bk8, rank, pad_start.astype(f32).reshape(LANES, 1))[0]
    xs = _sc_gather_rows(_sc_invert(dest, P), h2x)
    ys = _experts(blk_a, blk_b, used, xs, w_eg.astype(bf16), w_eu.astype(bf16), w_ed.astype(bf16))
    y = _sc_gather_rows(dest, ys)
    out = _final_ln(h2x, y, v2(ln_f_g), v2(ln_f_b), alpha)
    return out.reshape(B, S, D)


def kernel(x, mem, positions, w_in, b_forget, sinks, w_mix_out, ln_mix_g, ln_mix_b, w_xq, w_xkv, w_xout,
           ln_x_g, ln_x_b, w_route_group, b_route_group, w_route_expert, b_route_expert,
           w_exp_gate, w_exp_up, w_exp_down, ln_ffn_g, ln_ffn_b):
    depth = w_in.shape[0]
    alpha = (2.0 * depth) ** 0.25
    h = x
    for l in range(depth):
        h = _layer(h, mem, positions, w_in[l], b_forget[l], sinks[l], w_mix_out[l], ln_mix_g[l], ln_mix_b[l],
                   w_xq[l], w_xkv[l], w_xout[l], ln_x_g[l], ln_x_b[l], w_route_group[l], b_route_group[l],
                   w_route_expert[l], b_route_expert[l], w_exp_gate[l], w_exp_up[l], w_exp_down[l],
                   ln_ffn_g[l], ln_ffn_b[l], alpha)
    return h
```

```python
import functools

import jax
import jax.numpy as jnp
import numpy as np
from jax import lax
from jax.experimental import pallas as pl
from jax.experimental.pallas import tpu as pltpu
from jax.experimental.pallas import tpu_sc as plsc

f32 = jnp.float32
bf16 = jnp.bfloat16
i32 = jnp.int32

D = 1024
HD = 64
N_FOX = 8
N_SWA = 8
N_SWA_KV = 2
FOX_W = 512
SWA_Q_W = 512
SWA_KV_W = 128
WINDOW = 128
ROPE_THETA = 10000.0
N_XH = 4
XHD = 256
N_GROUPS = 4
EPG = 8
N_EXPERTS = 32
D_EXPERT = 512
LN_EPS = 1e-5
NEG = -1e30
LOG2E = 1.4426950408889634
L_ROW = (HD, 0)

SC_CORES = 2
SC_SUBCORES = 16
SC_LANES = 16
LANES = 128
ROW_BLOCK = 128
N_PAIRS = EPG * (EPG - 1) // 2
N_BUCKETS = N_GROUPS * N_PAIRS
XW = D + LANES
VMEM_LIMIT = 56 * 1024 * 1024


def _cparams(sem):
    return pltpu.CompilerParams(dimension_semantics=sem, vmem_limit_bytes=VMEM_LIMIT)


def _layer_norm(v, g, b):
    mu = jnp.mean(v, axis=-1, keepdims=True)
    c = v - mu
    var = jnp.mean(c * c, axis=-1, keepdims=True)
    return c * lax.rsqrt(var + LN_EPS) * g + b


def _dot(a, b):
    return jnp.dot(a, b, preferred_element_type=f32)


def _dot_nt(a, b):
    return lax.dot_general(a, b, (((1,), (1,)), ((), ())), preferred_element_type=f32)


def _inproj_kernel(x_ref, pos_ref, w_ref, wqt_ref, wvt_ref, wfl_ref, bf_ref, invf_ref, eg_ref, eu_ref, ed_ref,
                   qt_ref, kf_ref, vt_ref, qs_ref, ks_ref, vs_ref, lf_ref, egb_ref, eub_ref, edb_ref):
    tm = x_ref.shape[0]
    xb = x_ref[...].astype(bf16)
    egb_ref[...] = eg_ref[...].astype(bf16)
    eub_ref[...] = eu_ref[...].astype(bf16)
    edb_ref[...] = ed_ref[...].astype(bf16)

    def proj(lo, hi):
        return _dot(xb, w_ref[:, lo:hi])

    qt_ref[0] = (_dot_nt(wqt_ref[...], xb) * (0.125 * LOG2E)).astype(bf16)
    vt_ref[0] = _dot_nt(wvt_ref[...], xb).astype(bf16)
    kf_ref[...] = proj(0, 512).astype(bf16)

    ang = pos_ref[...].astype(f32) * invf_ref[...]
    cos = jnp.cos(ang)
    sin = jnp.sin(ang)
    lane = lax.broadcasted_iota(i32, (tm, LANES), 1)
    lo_half = (lane % HD) < (HD // 2)
    sin_s = jnp.where(lo_half, -sin, sin)

    def rope(z):
        rot = jnp.where(lo_half, pltpu.roll(z, LANES - HD // 2, 1), pltpu.roll(z, HD // 2, 1))
        return z * cos + rot * sin_s

    zq = proj(512, 1024)
    for g in range(4):
        sl = slice(g * LANES, (g + 1) * LANES)
        qs_ref[:, sl] = (rope(zq[:, sl]) * 0.125).astype(bf16)
    ks_ref[...] = rope(proj(1024, 1152)).astype(bf16)
    vs_ref[...] = proj(1152, 1280).astype(bf16)

    z = _dot_nt(wfl_ref[...], xb) + bf_ref[...]
    lf_ref[...] = jnp.minimum(z, 0.0) - jnp.log(1.0 + jnp.exp(-jnp.abs(z)))


def _in_proj(x2, pos2, w_all, wqt, wvt, wfl, bfc, invf, w_eg, w_eu, w_ed, tm):
    T = x2.shape[0]
    steps = T // tm
    row = lambda w: pl.BlockSpec((tm, w), lambda i: (i, 0))
    full = lambda a: pl.BlockSpec(a.shape, lambda i: (0,) * a.ndim)
    fmaj = pl.BlockSpec((1, FOX_W, tm), lambda i: (i, 0, 0))
    flat = [w.reshape(-1, w.shape[-1]) for w in (w_eg, w_eu, w_ed)]
    slices = [pl.BlockSpec((w.shape[0] // steps, w.shape[1]), lambda i: (i, 0)) for w in flat]
    outs = pl.pallas_call(
        _inproj_kernel,
        grid=(steps,),
        in_specs=[row(D), row(1), full(w_all), full(wqt), full(wvt), full(wfl), full(bfc), full(invf)] + slices,
        out_specs=[fmaj, row(512), fmaj, row(512), row(128), row(128),
                   pl.BlockSpec((N_FOX, tm), lambda i: (0, i))] + slices,
        out_shape=[jax.ShapeDtypeStruct((steps, FOX_W, tm), bf16), jax.ShapeDtypeStruct((T, 512), bf16),
                   jax.ShapeDtypeStruct((steps, FOX_W, tm), bf16), jax.ShapeDtypeStruct((T, 512), bf16),
                   jax.ShapeDtypeStruct((T, 128), bf16), jax.ShapeDtypeStruct((T, 128), bf16),
                   jax.ShapeDtypeStruct((N_FOX, T), f32)]
        + [jax.ShapeDtypeStruct(w.shape, bf16) for w in flat],
        compiler_params=_cparams(("parallel",)),
        name="in_proj",
    )(x2, pos2, w_all, wqt, wvt, wfl, bfc, invf, *flat)
    experts_bf16 = [o.reshape(w.shape) for o, w in zip(outs[7:], (w_eg, w_eu, w_ed))]
    return outs[:7], experts_bf16


def _cumsum_kernel(lf_ref, c_ref, ca_ref):
    S = lf_ref.shape[1]
    ch = 256
    r = lax.broadcasted_iota(i32, (ch, ch), 0)
    c = lax.broadcasted_iota(i32, (ch, ch), 1)
    tri = (r <= c).astype(f32)
    eye = (r == c).astype(bf16)
    stacked = jnp.concatenate([lf_ref[:, j * ch:(j + 1) * ch] for j in range(S // ch)], axis=0)
    local = jnp.dot(stacked, tri, precision=lax.Precision.HIGHEST, preferred_element_type=f32)
    carry = jnp.zeros((N_FOX, 1), f32)
    for j in range(S // ch):
        cc = local[j * N_FOX:(j + 1) * N_FOX] + carry
        carry = cc[:, ch - 1:ch]
        c2 = cc * LOG2E
        c_ref[:, j * ch:(j + 1) * ch] = c2
        neg = -c2
        hi = neg.astype(bf16)
        r1 = neg - hi.astype(f32)
        mid = r1.astype(bf16)
        lo = (r1 - mid.astype(f32)).astype(bf16)
        terms = jnp.concatenate([hi, mid, lo, jnp.zeros((LANES - 3 * N_FOX, ch), bf16)], axis=0)
        ca_ref[j * ch:(j + 1) * ch, :] = _dot_nt(eye, terms).astype(bf16)


def _cumsum(lf, S):
    T = lf.shape[1]
    spec = pl.BlockSpec((N_FOX, S), lambda b: (0, b))
    return pl.pallas_call(
        _cumsum_kernel, grid=(T // S,), in_specs=[spec],
        out_specs=[spec, pl.BlockSpec((S, LANES), lambda b: (b, 0))],
        out_shape=[jax.ShapeDtypeStruct((N_FOX, T), f32), jax.ShapeDtypeStruct((T, LANES), bf16)],
        compiler_params=_cparams(("parallel",)), name="cumsum",
    )(lf)


def _fox_kernel(qt_ref, k_ref, ca_ref, vt_ref, c_ref, o_ref, t0_ref, t1_ref, *, tq):
    hp = pl.program_id(1)
    i = pl.program_id(2)
    qt = qt_ref[0]
    row = lax.broadcasted_iota(i32, (LANES, tq), 0)
    is_a = row < HD
    zero = jnp.zeros_like(qt)
    q_ops = []
    for h in range(2):
        ones = jnp.where(((row & 7) == 2 * hp + h) & (row < 3 * N_FOX), 1.0, 0.0).astype(bf16)
        qh = jnp.where(is_a, qt, zero) if h == 0 else jnp.where(is_a, zero, qt)
        q_ops.append(jnp.concatenate([qh, ones], axis=0))
    kr = lax.broadcasted_iota(i32, (tq, tq), 0)
    qc = lax.broadcasted_iota(i32, (tq, tq), 1)
    causal = kr <= qc
    cq = [c_ref[0, h, pl.ds(i, 1), :] for h in range(2)]

    def scores(j, t_ref):
        off = pl.multiple_of(j * tq, tq)
        kblk = jnp.concatenate([k_ref[0, pl.ds(off, tq), :], ca_ref[0, pl.ds(off, tq), :]], axis=1)
        for h in range(2):
            t_ref[h] = _dot(kblk, q_ops[h])

    keep = [jnp.where(is_a, 1.0, 0.0).astype(bf16), jnp.where(is_a, 0.0, 1.0).astype(bf16)]
    ones_row = [jnp.where(row == L_ROW[h], 1.0, 0.0).astype(bf16) for h in range(2)]

    def softmax_pv(j, t_ref, carry, masked):
        vt = vt_ref[j]
        vts = [vt * keep[h] + ones_row[h] for h in range(2)]
        new = []
        for h in range(2):
            m, acc = carry[h]
            t = t_ref[h]
            if masked:
                t = jnp.where(causal, t, NEG)
            m_new = jnp.maximum(m, jnp.max(t, axis=0, keepdims=True) + cq[h])
            alpha = jnp.exp2(m - m_new)
            p = jnp.exp2(t + (cq[h] - m_new))
            acc = alpha * acc + _dot(vts[h], p.astype(bf16))
            new.append((m_new, acc))
        return tuple(new)

    def pair(k, carry):
        j = 2 * k
        scores(j + 1, t1_ref)
        carry = softmax_pv(j, t0_ref, carry, False)
        scores(j + 2, t0_ref)
        return softmax_pv(j + 1, t1_ref, carry, False)

    def odd_tail(carry):
        scores(i, t1_ref)
        carry = softmax_pv(i - 1, t0_ref, carry, False)
        return softmax_pv(i, t1_ref, carry, True)

    def even_tail(carry):
        return softmax_pv(i, t0_ref, carry, True)

    init = tuple((jnp.full((1, tq), NEG, f32), jnp.zeros((LANES, tq), f32)) for _ in range(2))
    scores(0, t0_ref)
    carry = lax.fori_loop(0, i // 2, pair, init)
    (_, acca), (_, accb) = lax.cond(i % 2 == 1, odd_tail, even_tail, carry)
    la = acca[L_ROW[0]:L_ROW[0] + 1, :]
    lb = accb[L_ROW[1]:L_ROW[1] + 1, :]
    ot = jnp.where(is_a, acca / la, accb / lb)
    o_ref[0] = jnp.transpose(ot).astype(bf16)


def _fox(qt, kf, ca, vt, c4, B, S, tq):
    nq = S // tq
    kernel = functools.partial(_fox_kernel, tq=tq)
    return pl.pallas_call(
        kernel,
        grid=(B, N_FOX // 2, nq),
        in_specs=[
            pl.BlockSpec((1, LANES, tq), lambda b, hp, i: (b * nq + i, hp, 0)),
            pl.BlockSpec((1, S, LANES), lambda b, hp, i: (b, 0, hp)),
            pl.BlockSpec((1, S, LANES), lambda b, hp, i: (b, 0, 0)),
            pl.BlockSpec((nq, LANES, tq), lambda b, hp, i: (b, hp, 0)),
            pl.BlockSpec((1, 2, nq, tq), lambda b, hp, i: (hp, 0, b, 0)),
        ],
        out_specs=pl.BlockSpec((1, tq, LANES), lambda b, hp, i: (b, i, hp)),
        out_shape=jax.ShapeDtypeStruct((B, S, FOX_W), bf16),
        scratch_shapes=[pltpu.VMEM((2, tq, tq), f32), pltpu.VMEM((2, tq, tq), f32)],
        compiler_params=_cparams(("parallel", "parallel", "arbitrary")),
        name="fox",
    )(qt, kf, ca, vt, c4)


def _swa_kernel(sink_ref, q_ref, k_ref, v_ref, o_ref, *, tq):
    W = WINDOW
    nsub = tq // W
    n0 = pl.program_id(1) * nsub
    lane = lax.broadcasted_iota(i32, (W, LANES), 1)
    is0 = lane < HD
    rows = lax.broadcasted_iota(i32, (4 * W, 2 * W), 0)
    cols = lax.broadcasted_iota(i32, (4 * W, 2 * W), 1)
    rgrp = lax.broadcasted_iota(i32, (4 * W, 1), 0) // W
    for r in range(nsub):
        nb = n0 + r
        kstart = pl.multiple_of(jnp.maximum(nb * W - W, 0), W)
        ks = k_ref[0, pl.ds(kstart, 2 * W), :]
        vs = v_ref[0, pl.ds(kstart, 2 * W), :]
        qpos = nb * W + rows % W
        kpos = kstart + cols
        valid = (kpos <= qpos) & (qpos - kpos < W)
        outs = []
        for kv in range(2):
            keep = is0 if kv == 0 else jnp.logical_not(is0)
            parts = []
            for g in range(4):
                qg = q_ref[0, r * W:(r + 1) * W, g * LANES:(g + 1) * LANES]
                parts.append(jnp.where(keep, qg, jnp.zeros_like(qg)))
            qstack = jnp.concatenate(parts, axis=0)
            s = jnp.where(valid, _dot_nt(qstack, ks), NEG)
            sink = jnp.zeros((4 * W, 1), f32)
            for g in range(4):
                sink = jnp.where(rgrp == g, sink_ref[kv * 4 + g], sink)
            m = jnp.maximum(jnp.max(s, axis=1, keepdims=True), sink)
            e = jnp.exp(s - m)
            den = jnp.sum(e, axis=1, keepdims=True) + jnp.exp(sink - m)
            outs.append(_dot(e.astype(bf16), vs) / den)
        for g in range(4):
            og = jnp.where(is0, outs[0][g * W:(g + 1) * W], outs[1][g * W:(g + 1) * W])
            o_ref[0, r * W:(r + 1) * W, g * LANES:(g + 1) * LANES] = og.astype(bf16)


def _swa(sinks, qs, ks, vs, B, S, tq=512):
    kernel = functools.partial(_swa_kernel, tq=tq)
    return pl.pallas_call(
        kernel,
        grid=(B, S // tq),
        in_specs=[
            pl.BlockSpec(memory_space=pltpu.SMEM),
            pl.BlockSpec((1, tq, SWA_Q_W), lambda b, i: (b, i, 0)),
            pl.BlockSpec((1, S, SWA_KV_W), lambda b, i: (b, 0, 0)),
            pl.BlockSpec((1, S, SWA_KV_W), lambda b, i: (b, 0, 0)),
        ],
        out_specs=pl.BlockSpec((1, tq, SWA_Q_W), lambda b, i: (b, i, 0)),
        out_shape=jax.ShapeDtypeStruct((B, S, SWA_Q_W), bf16),
        compiler_params=_cparams(("parallel", "arbitrary")),
        name="swa",
    )(sinks, qs, ks, vs)


def _kvproj_kernel(m_ref, w_ref, k_ref, v_ref):
    mb = m_ref[...].astype(bf16)
    k_ref[...] = _dot(mb, w_ref[:, :D]).astype(bf16)
    v_ref[...] = _dot(mb, w_ref[:, D:]).astype(bf16)


def _kvproj(mem2, w_xkv, tm=512):
    R = mem2.shape[0]
    row = pl.BlockSpec((tm, D), lambda i: (i, 0))
    return pl.pallas_call(
        _kvproj_kernel, grid=(R // tm,),
        in_specs=[row, pl.BlockSpec(w_xkv.shape, lambda i: (0, 0))],
        out_specs=[row, row],
        out_shape=[jax.ShapeDtypeStruct((R, D), bf16)] * 2,
        compiler_params=_cparams(("parallel",)), name="kvproj",
    )(mem2, w_xkv)


def _mid_kernel(x_ref, of_ref, os_ref, wo_ref, g1_ref, b1_ref, wq_ref, k_ref, v_ref,
                wxo_ref, g2_ref, b2_ref, wrh_ref, wrl_ref, br_ref,
                h2_ref, bk_ref, oc_ref, *, alpha):
    tm = x_ref.shape[0]
    mix = _dot(of_ref[...], wo_ref[:FOX_W, :]) + _dot(os_ref[...], wo_ref[FOX_W:, :])
    h1 = _layer_norm(alpha * x_ref[...] + mix, g1_ref[...], b1_ref[...])

    q = (_dot(h1.astype(bf16), wq_ref[...]) * 0.0625).astype(bf16)
    for h in range(N_XH):
        sl = slice(h * XHD, (h + 1) * XHD)
        s = _dot_nt(q[:, sl], k_ref[:, sl])
        e = jnp.exp(s - jnp.max(s, axis=1, keepdims=True))
        p = e / jnp.sum(e, axis=1, keepdims=True)
        oc_ref[:, sl] = _dot(p.astype(bf16), v_ref[:, sl]).astype(bf16)
    xo = _dot(oc_ref[...], wxo_ref[...])
    h2 = _layer_norm(alpha * h1 + xo, g2_ref[...], b2_ref[...])
    h2_ref[:, :D] = h2

    hh = h2.astype(bf16)
    hl = (h2 - hh.astype(f32)).astype(bf16)
    lg = _dot(hh, wrh_ref[...]) + _dot(hl, wrh_ref[...]) + _dot(hh, wrl_ref[...]) + br_ref[...]

    lane = lax.broadcasted_iota(i32, (tm, LANES), 1).astype(f32)
    big = float(LANES)

    def first_max(vals, mask):
        vm = jnp.where(mask, vals, NEG)
        top = jnp.max(vm, axis=1, keepdims=True)
        idx = jnp.min(jnp.where(mask & (vm == top), lane, big), axis=1, keepdims=True)
        return top, idx

    gmask = lane < float(N_GROUPS)
    gmax, gidx = first_max(lg, gmask)
    g_val = 1.0 / jnp.sum(jnp.where(gmask, jnp.exp(lg - gmax), 0.0), axis=1, keepdims=True)
    lo = float(N_GROUPS) + float(EPG) * gidx
    emask = (lane >= lo) & (lane < lo + float(EPG))
    v1, i1 = first_max(lg, emask)
    v2, i2 = first_max(lg, emask & (lane != i1))
    ex = jnp.exp(v2 - v1)
    w1 = g_val * (1.0 / (1.0 + ex))
    w2 = g_val * (ex / (1.0 + ex))
    e1 = i1 - lo
    e2 = i2 - lo
    first_low = e1 < e2
    ea = jnp.where(first_low, e1, e2)
    eb = jnp.where(first_low, e2, e1)
    ga = jnp.where(first_low, w1, w2)
    gb = jnp.where(first_low, w2, w1)
    pidx = ea * float(EPG - 1) - ea * (ea - 1.0) * 0.5 + (eb - ea - 1.0)
    bucket = gidx * float(N_PAIRS) + pidx

    lane_i = lax.broadcasted_iota(i32, (tm, LANES), 1)
    h2_ref[:, D:] = jnp.where(lane_i == 0, ga, jnp.where(lane_i == 1, gb, 0.0))
    bk_t = jnp.transpose(jnp.broadcast_to(bucket, (tm, LANES)))
    bk_ref[...] = bk_t[:8, :]


def _mid(x2, of2, os2, w_out, g1, b1, wq, kx, vx, wxo, g2, b2, wrh, wrl, br, alpha, S, tm=512):
    T = x2.shape[0]
    M = kx.shape[0] // (T // S)
    per_b = S // tm
    row = lambda w: pl.BlockSpec((tm, w), lambda i: (i, 0))
    full = lambda a: pl.BlockSpec(a.shape, lambda i: (0,) * a.ndim)
    kvspec = pl.BlockSpec((M, D), lambda i: (i // per_b, 0))
    kernel = functools.partial(_mid_kernel, alpha=alpha)
    return pl.pallas_call(
        kernel,
        grid=(T // tm,),
        in_specs=[row(D), row(512), row(512), full(w_out), full(g1), full(b1), full(wq),
                  kvspec, kvspec, full(wxo), full(g2), full(b2), full(wrh), full(wrl), full(br)],
        out_specs=[row(XW), pl.BlockSpec((8, tm), lambda i: (0, i))],
        out_shape=[jax.ShapeDtypeStruct((T, XW), f32), jax.ShapeDtypeStruct((8, T), f32)],
        scratch_shapes=[pltpu.VMEM((tm, D), bf16)],
        compiler_params=_cparams(("parallel",)),
        name="mid",
    )(x2, of2, os2, w_out, g1, b1, wq, kx, vx, wxo, g2, b2, wrh, wrl, br)


def _rank_kernel(bk_ref, rank_ref, cnt_ref, carry_ref, *, chunk):
    sub = 256

    @pl.when(pl.program_id(0) == 0)
    def _():
        carry_ref[...] = jnp.zeros_like(carry_ref)

    r = lax.broadcasted_iota(i32, (sub, sub), 0)
    c = lax.broadcasted_iota(i32, (sub, sub), 1)
    before = (r < c).astype(bf16)
    bid = lax.broadcasted_iota(i32, (LANES, sub), 0).astype(f32)
    carry = carry_ref[...]
    for j in range(chunk // sub):
        bk = bk_ref[0:1, j * sub:(j + 1) * sub]
        hit = bid == bk
        oh = jnp.where(hit, 1.0, 0.0)
        prior = _dot(oh.astype(bf16), before) + carry
        rank_ref[:, j * sub:(j + 1) * sub] = jnp.sum(jnp.where(hit, prior, 0.0), axis=0, keepdims=True)
        carry = carry + jnp.sum(oh, axis=1, keepdims=True)
    carry_ref[...] = carry
    cnt_ref[...] = carry


def _rank(bk8, chunk=2048):
    T = bk8.shape[1]
    kernel = functools.partial(_rank_kernel, chunk=chunk)
    return pl.pallas_call(
        kernel, grid=(T // chunk,),
        in_specs=[pl.BlockSpec((8, chunk), lambda i: (0, i))],
        out_specs=[pl.BlockSpec((1, chunk), lambda i: (0, i)),
                   pl.BlockSpec((LANES, 1), lambda i: (0, 0))],
        out_shape=[jax.ShapeDtypeStruct((1, T), f32), jax.ShapeDtypeStruct((LANES, 1), f32)],
        scratch_shapes=[pltpu.VMEM((LANES, 1), f32)],
        compiler_params=_cparams(("arbitrary",)), name="rank",
    )(bk8)


def _dest_kernel(bk_ref, rank_ref, ps_ref, dest_ref):
    chunk = bk_ref.shape[1]
    bid = lax.broadcasted_iota(i32, (LANES, chunk), 0).astype(f32)
    start = jnp.sum(jnp.where(bid == bk_ref[0:1, :], ps_ref[...], 0.0), axis=0, keepdims=True)
    dest_ref[...] = (start + rank_ref[...]).astype(i32)


def _dest(bk8, rank, ps_col, chunk=2048):
    T = bk8.shape[1]
    return pl.pallas_call(
        _dest_kernel, grid=(T // chunk,),
        in_specs=[pl.BlockSpec((8, chunk), lambda i: (0, i)), pl.BlockSpec((1, chunk), lambda i: (0, i)),
                  pl.BlockSpec((LANES, 1), lambda i: (0, 0))],
        out_specs=pl.BlockSpec((1, chunk), lambda i: (0, i)),
        out_shape=jax.ShapeDtypeStruct((1, T), i32),
        compiler_params=_cparams(("parallel",)), name="dest",
    )(bk8, rank, ps_col)


def _sc_invert(dest, n_rows):
    T = dest.shape[0]
    assert T & (T - 1) == 0
    lanes = SC_LANES
    mesh = plsc.VectorSubcoreMesh(core_axis_name="core", subcore_axis_name="subcore",
                                  num_cores=SC_CORES, num_subcores=SC_SUBCORES)

    @functools.partial(pl.kernel, out_type=jax.ShapeDtypeStruct((n_rows,), i32), mesh=mesh,
                       scratch_types=[pltpu.VMEM((T,), i32), pltpu.VMEM((n_rows,), i32)],
                       compiler_params=pltpu.CompilerParams(needs_layout_passes=False),
                       name="sc_invert")
    def k(dest_hbm, out_hbm, dest_v, table_v):
        wid = lax.axis_index("subcore") * SC_CORES + lax.axis_index("core")

        @pl.when(wid == 0)
        def _():
            pltpu.sync_copy(dest_hbm, dest_v)
            lane = lax.iota(i32, lanes)

            @pl.loop(0, n_rows // lanes)
            def _(j):
                off = pl.multiple_of(j * lanes, lanes)
                table_v[pl.ds(off, lanes)] = (lane + off) & (T - 1)


            @pl.loop(0, T // lanes)
            def _(j):
                off = pl.multiple_of(j * lanes, lanes)
                plsc.store_scatter(table_v, [dest_v[pl.ds(off, lanes)]], lane + off)

            pltpu.sync_copy(table_v, out_hbm)

    return k(dest)


def _sc_gather_rows(idx, src, chunk=32):
    n = idx.shape[0]
    w = src.shape[1]
    workers = SC_CORES * SC_SUBCORES
    per_worker = n // workers
    mesh = plsc.VectorSubcoreMesh(core_axis_name="core", subcore_axis_name="subcore",
                                  num_cores=SC_CORES, num_subcores=SC_SUBCORES)

    n_chunks = per_worker // chunk
    assert n_chunks % 2 == 0

    @functools.partial(pl.kernel, out_type=jax.ShapeDtypeStruct((n, w), src.dtype), mesh=mesh,
                       scratch_types=[pltpu.VMEM((per_worker,), i32), pltpu.VMEM((2, chunk, w), src.dtype),
                                      pltpu.SemaphoreType.DMA((2,)), pltpu.SemaphoreType.DMA((2,))],
                       name="sc_gather_rows")
    def k(src_hbm, idx_hbm, out_hbm, idx_v, rows_v, gsem, wsem):
        wid = lax.axis_index("subcore") * SC_CORES + lax.axis_index("core")
        base = wid * per_worker
        pltpu.sync_copy(idx_hbm.at[pl.ds(base, per_worker)], idx_v)

        def gather(c, slot):
            rows = idx_v.at[pl.ds(pl.multiple_of(c * chunk, chunk), chunk)]
            return pltpu.make_async_copy(src_hbm.at[rows], rows_v.at[slot], gsem.at[slot])

        def write(c, slot):
            out = out_hbm.at[pl.ds(pl.multiple_of(base + c * chunk, chunk), chunk)]
            return pltpu.make_async_copy(rows_v.at[slot], out, wsem.at[slot])

        gather(0, 0).start()

        @pl.loop(0, n_chunks // 2)
        def _(pair):
            c = 2 * pair

            @pl.when(pair > 0)
            def _():
                write(c - 1, 1).wait()

            gather(c + 1, 1).start()
            gather(c, 0).wait()
            write(c, 0).start()

            @pl.when(c + 2 < n_chunks)
            def _():
                write(c, 0).wait()
                gather(c + 2, 0).start()

            gather(c + 1, 1).wait()
            write(c + 1, 1).start()

        write(n_chunks - 2, 0).wait()
        write(n_chunks - 1, 1).wait()

    return k(src, idx)


def _expert_kernel(ea_ref, eb_ref, used_ref, xs_ref, wga_ref, wua_ref, wda_ref,
                   wgb_ref, wub_ref, wdb_ref, y_ref):
    n = pl.program_id(0)

    @pl.when(n < used_ref[0])
    def _():
        x = xs_ref[:, :D].astype(bf16)

        def expert(wg, wu, wd):
            a = _dot(x, wg[0])
            u = _dot(x, wu[0])
            act = a * (1.0 / (1.0 + jnp.exp(-a))) * u
            return _dot(act.astype(bf16), wd[0])

        ga = xs_ref[:, D:D + 1]
        gb = xs_ref[:, D + 1:D + 2]
        y_ref[...] = ga * expert(wga_ref, wua_ref, wda_ref) + gb * expert(wgb_ref, wub_ref, wdb_ref)

    @pl.when(n >= used_ref[0])
    def _():
        y_ref[...] = jnp.zeros_like(y_ref)


def _experts(ea, eb, used, xs, wg, wu, wd):
    P = xs.shape[0]
    nblk = P // ROW_BLOCK

    def xmap(n, ea, eb, used):
        return (jnp.minimum(n, used[0] - 1), 0)

    wa = lambda n, ea, eb, used: (ea[n], 0, 0)
    wb = lambda n, ea, eb, used: (eb[n], 0, 0)
    wgs = lambda m: pl.BlockSpec((1, D, D_EXPERT), m)
    wds = lambda m: pl.BlockSpec((1, D_EXPERT, D), m)
    grid_spec = pltpu.PrefetchScalarGridSpec(
        num_scalar_prefetch=3, grid=(nblk,),
        in_specs=[pl.BlockSpec((ROW_BLOCK, XW), xmap), wgs(wa), wgs(wa), wds(wa), wgs(wb), wgs(wb), wds(wb)],
        out_specs=pl.BlockSpec((ROW_BLOCK, D), lambda n, ea, eb, used: (n, 0)),
    )
    return pl.pallas_call(
        _expert_kernel, grid_spec=grid_spec,
        out_shape=jax.ShapeDtypeStruct((P, D), f32),
        compiler_params=_cparams(("arbitrary",)), name="experts",
    )(ea, eb, used, xs, wg, wu, wd, wg, wu, wd)


def _ln_kernel(h_ref, y_ref, g_ref, b_ref, o_ref, *, alpha):
    o_ref[...] = _layer_norm(alpha * h_ref[...] + y_ref[...], g_ref[...], b_ref[...])


def _final_ln(h2x, y, g, b, alpha, tm=512):
    T = y.shape[0]
    row = pl.BlockSpec((tm, D), lambda i: (i, 0))
    vec = pl.BlockSpec((1, D), lambda i: (0, 0))
    return pl.pallas_call(
        functools.partial(_ln_kernel, alpha=alpha), grid=(T // tm,),
        in_specs=[row, row, vec, vec], out_specs=row,
        out_shape=jax.ShapeDtypeStruct((T, D), f32),
        compiler_params=_cparams(("parallel",)), name="final_ln",
    )(h2x, y, g, b)


def _pair_tables():
    ea = np.zeros((LANES,), np.int32)
    eb = np.zeros((LANES,), np.int32)
    for g in range(N_GROUPS):
        k = 0
        for a in range(EPG):
            for b in range(a + 1, EPG):
                ea[g * N_PAIRS + k] = g * EPG + a
                eb[g * N_PAIRS + k] = g * EPG + b
                k += 1
    return ea, eb


_PAIR_A, _PAIR_B = _pair_tables()


def _layer(h, mem, positions, w_in, b_forget, sinks, w_mix_out, ln_mix_g, ln_mix_b,
           w_xq, w_xkv, w_xout, ln_x_g, ln_x_b, w_rg, b_rg, w_re, b_re,
           w_eg, w_eu, w_ed, ln_f_g, ln_f_b, alpha):
    B, S, _ = h.shape
    T = B * S
    x2 = h.reshape(T, D)
    pos2 = positions.reshape(T, 1).astype(i32)

    o = np.cumsum((0, FOX_W, FOX_W, FOX_W, N_FOX, SWA_Q_W, SWA_KV_W, SWA_KV_W))
    w_qf, w_kf, w_vf, w_fl, w_qs, w_ks, w_vs = (w_in[:, o[i]:o[i + 1]] for i in range(7))
    def regroup(a, axis):
        shp = a.shape
        a = jnp.moveaxis(a, axis, 0).reshape(N_SWA_KV, N_SWA // N_SWA_KV, HD, -1)
        return jnp.moveaxis(jnp.swapaxes(a, 0, 1).reshape(N_SWA * HD, -1), 0, axis).reshape(shp)

    w_all = jnp.concatenate([w_kf, regroup(w_qs, 1), w_ks, w_vs], axis=1).astype(bf16)
    wqt = w_qf.T.astype(bf16)
    wvt = w_vf.T.astype(bf16)
    wfl = w_fl.T.astype(bf16)
    bfc = b_forget.reshape(N_FOX, 1).astype(f32)
    half = HD // 2
    inv_freq = ROPE_THETA ** (-jnp.arange(half, dtype=f32) / half)
    invf = jnp.tile(inv_freq, LANES // half).reshape(1, LANES)
    w_out = jnp.concatenate([w_mix_out[:FOX_W], regroup(w_mix_out[FOX_W:], 0)], axis=0).astype(bf16)

    tq = 512
    (qt, kf, vt, qs, ks, vs, lf), (eg16, eu16, ed16) = _in_proj(
        x2, pos2, w_all, wqt, wvt, wfl, bfc, invf, w_eg, w_eu, w_ed, tq)
    c, ca = _cumsum(lf, S)
    c4 = c.reshape(N_FOX // 2, 2, T // tq, tq)
    r3 = lambda a: a.reshape(B, S, a.shape[-1])
    o_fox = _fox(qt, r3(kf), r3(ca), vt, c4, B, S, tq)
    o_swa = _swa(sinks.astype(f32), r3(qs), r3(ks), r3(vs), B, S)

    kx, vx = _kvproj(mem.reshape(-1, D), w_xkv.astype(bf16))

    wr = jnp.concatenate([w_rg, jnp.transpose(w_re, (1, 0, 2)).reshape(D, N_EXPERTS)], axis=1)
    wr = jnp.pad(wr, ((0, 0), (0, LANES - wr.shape[1]))).astype(f32)
    wrh = wr.astype(bf16)
    wrl = (wr - wrh.astype(f32)).astype(bf16)
    br = jnp.pad(jnp.concatenate([b_rg, b_re.reshape(-1)]), (0, LANES - N_GROUPS - N_EXPERTS))
    br = br.reshape(1, LANES).astype(f32)
    v2 = lambda a: a.reshape(1, D).astype(f32)
    h2x, bk8 = _mid(x2, o_fox.reshape(T, FOX_W), o_swa.reshape(T, SWA_Q_W), w_out,
                    v2(ln_mix_g), v2(ln_mix_b), w_xq.astype(bf16), kx, vx, w_xout.astype(bf16),
                    v2(ln_x_g), v2(ln_x_b), wrh, wrl, br, alpha, S)

    rank, cnt = _rank(bk8)
    counts = cnt[:, 0].astype(i32)
    padded = ((counts + ROW_BLOCK - 1) // ROW_BLOCK) * ROW_BLOCK
    pad_end = jnp.cumsum(padded)
    pad_start = (pad_end - padded).astype(i32)
    P = T + N_BUCKETS * ROW_BLOCK
    nblk = P // ROW_BLOCK
    used = (pad_end[-1] // ROW_BLOCK).astype(i32).reshape(1)
    blk_row = jnp.arange(nblk, dtype=i32)[:, None] * ROW_BLOCK
    blk_bucket = jnp.minimum(jnp.sum((pad_end[None, :] <= blk_row).astype(i32), axis=1), N_BUCKETS - 1)
    pick = (blk_bucket[:, None] == jnp.arange(LANES, dtype=i32)[None, :]).astype(i32)
    blk_a = jnp.sum(pick * jnp.asarray(_PAIR_A)[None, :], axis=1)
    blk_b = jnp.sum(pick * jnp.asarray(_PAIR_B)[None, :], axis=1)

    dest = _dest(bk8, rank, pad_start.astype(f32).reshape(LANES, 1))[0]
    xs = _sc_gather_rows(_sc_invert(dest, P), h2x)
    ys = _experts(blk_a, blk_b, used, xs, eg16, eu16, ed16)
    y = _sc_gather_rows(dest, ys)
    out = _final_ln(h2x, y, v2(ln_f_g), v2(ln_f_b), alpha)
    return out.reshape(B, S, D)


def kernel(x, mem, positions, w_in, b_forget, sinks, w_mix_out, ln_mix_g, ln_mix_b, w_xq, w_xkv, w_xout,
           ln_x_g, ln_x_b, w_route_group, b_route_group, w_route_expert, b_route_expert,
           w_exp_gate, w_exp_up, w_exp_down, ln_ffn_g, ln_ffn_b):
    depth = w_in.shape[0]
    alpha = (2.0 * depth) ** 0.25
    h = x
    for l in range(depth):
        h = _layer(h, mem, positions, w_in[l], b_forget[l], sinks[l], w_mix_out[l], ln_mix_g[l], ln_mix_b[l],
                   w_xq[l], w_xkv[l], w_xout[l], ln_x_g[l], ln_x_b[l], w_route_group[l], b_route_group[l],
                   w_route_expert[l], b_route_expert[l], w_exp_gate[l], w_exp_up[l], w_exp_down[l],
                   ln_ffn_g[l], ln_ffn_b[l], alpha)
    return h
```

```python
import functools

import jax
import jax.numpy as jnp
import numpy as np
from jax import lax
from jax.experimental import pallas as pl
from jax.experimental.pallas import tpu as pltpu
from jax.experimental.pallas import tpu_sc as plsc

f32 = jnp.float32
bf16 = jnp.bfloat16
i32 = jnp.int32

D = 1024
HD = 64
N_FOX = 8
N_SWA = 8
N_SWA_KV = 2
FOX_W = 512
SWA_Q_W = 512
SWA_KV_W = 128
WINDOW = 128
ROPE_THETA = 10000.0
N_XH = 4
XHD = 256
N_GROUPS = 4
EPG = 8
N_EXPERTS = 32
D_EXPERT = 512
LN_EPS = 1e-5
NEG = -1e30
LOG2E = 1.4426950408889634
L_ROW = (HD, 0)

SC_CORES = 2
SC_SUBCORES = 16
SC_LANES = 16
SC_GATHER_ROWS = 32
MOE_CHUNKS = 4
LANES = 128
ROW_BLOCK = 128
N_PAIRS = EPG * (EPG - 1) // 2
N_BUCKETS = N_GROUPS * N_PAIRS
XW = D + LANES
VMEM_LIMIT = 56 * 1024 * 1024


def _cparams(sem):
    return pltpu.CompilerParams(dimension_semantics=sem, vmem_limit_bytes=VMEM_LIMIT)


def _layer_norm(v, g, b):
    mu = jnp.mean(v, axis=-1, keepdims=True)
    c = v - mu
    var = jnp.mean(c * c, axis=-1, keepdims=True)
    return c * lax.rsqrt(var + LN_EPS) * g + b


def _dot(a, b):
    return jnp.dot(a, b, preferred_element_type=f32)


def _dot_nt(a, b):
    return lax.dot_general(a, b, (((1,), (1,)), ((), ())), preferred_element_type=f32)


def _inproj_kernel(x_ref, pos_ref, w_ref, wqt_ref, wvt_ref, wfl_ref, bf_ref, invf_ref, eg_ref, eu_ref, ed_ref,
                   qt_ref, kf_ref, vt_ref, qs_ref, ks_ref, vs_ref, lf_ref, egb_ref, eub_ref, edb_ref):
    tm = x_ref.shape[0]
    xb = x_ref[...].astype(bf16)
    egb_ref[...] = eg_ref[...].astype(bf16)
    eub_ref[...] = eu_ref[...].astype(bf16)
    edb_ref[...] = ed_ref[...].astype(bf16)

    def proj(lo, hi):
        return _dot(xb, w_ref[:, lo:hi])

    qt_ref[0] = (_dot_nt(wqt_ref[...], xb) * (0.125 * LOG2E)).astype(bf16)
    vt_ref[0] = _dot_nt(wvt_ref[...], xb).astype(bf16)
    kf_ref[...] = proj(0, 512).astype(bf16)

    ang = pos_ref[...].astype(f32) * invf_ref[...]
    cos = jnp.cos(ang)
    sin = jnp.sin(ang)
    lane = lax.broadcasted_iota(i32, (tm, LANES), 1)
    lo_half = (lane % HD) < (HD // 2)
    sin_s = jnp.where(lo_half, -sin, sin)

    def rope(z):
        rot = jnp.where(lo_half, pltpu.roll(z, LANES - HD // 2, 1), pltpu.roll(z, HD // 2, 1))
        return z * cos + rot * sin_s

    zq = proj(512, 1024)
    for g in range(4):
        sl = slice(g * LANES, (g + 1) * LANES)
        qs_ref[:, sl] = (rope(zq[:, sl]) * 0.125).astype(bf16)
    ks_ref[...] = rope(proj(1024, 1152)).astype(bf16)
    vs_ref[...] = proj(1152, 1280).astype(bf16)

    z = _dot_nt(wfl_ref[...], xb) + bf_ref[...]
    lf_ref[...] = jnp.minimum(z, 0.0) - jnp.log(1.0 + jnp.exp(-jnp.abs(z)))


def _in_proj(x2, pos2, w_all, wqt, wvt, wfl, bfc, invf, w_eg, w_eu, w_ed, tm):
    T = x2.shape[0]
    steps = T // tm
    row = lambda w: pl.BlockSpec((tm, w), lambda i: (i, 0))
    full = lambda a: pl.BlockSpec(a.shape, lambda i: (0,) * a.ndim)
    fmaj = pl.BlockSpec((1, FOX_W, tm), lambda i: (i, 0, 0))
    flat = [w.reshape(-1, w.shape[-1]) for w in (w_eg, w_eu, w_ed)]
    slices = [pl.BlockSpec((w.shape[0] // steps, w.shape[1]), lambda i: (i, 0)) for w in flat]
    outs = pl.pallas_call(
        _inproj_kernel,
        grid=(steps,),
        in_specs=[row(D), row(1), full(w_all), full(wqt), full(wvt), full(wfl), full(bfc), full(invf)] + slices,
        out_specs=[fmaj, row(512), fmaj, row(512), row(128), row(128),
                   pl.BlockSpec((N_FOX, tm), lambda i: (0, i))] + slices,
        out_shape=[jax.ShapeDtypeStruct((steps, FOX_W, tm), bf16), jax.ShapeDtypeStruct((T, 512), bf16),
                   jax.ShapeDtypeStruct((steps, FOX_W, tm), bf16), jax.ShapeDtypeStruct((T, 512), bf16),
                   jax.ShapeDtypeStruct((T, 128), bf16), jax.ShapeDtypeStruct((T, 128), bf16),
                   jax.ShapeDtypeStruct((N_FOX, T), f32)]
        + [jax.ShapeDtypeStruct(w.shape, bf16) for w in flat],
        compiler_params=_cparams(("parallel",)),
        name="in_proj",
    )(x2, pos2, w_all, wqt, wvt, wfl, bfc, invf, *flat)
    experts_bf16 = [o.reshape(w.shape) for o, w in zip(outs[7:], (w_eg, w_eu, w_ed))]
    return outs[:7], experts_bf16


def _cumsum_kernel(lf_ref, c_ref, ca_ref):
    S = lf_ref.shape[1]
    ch = 256
    r = lax.broadcasted_iota(i32, (ch, ch), 0)
    c = lax.broadcasted_iota(i32, (ch, ch), 1)
    tri = (r <= c).astype(f32)
    eye = (r == c).astype(bf16)
    stacked = jnp.concatenate([lf_ref[:, j * ch:(j + 1) * ch] for j in range(S // ch)], axis=0)
    local = jnp.dot(stacked, tri, precision=lax.Precision.HIGHEST, preferred_element_type=f32)
    carry = jnp.zeros((N_FOX, 1), f32)
    for j in range(S // ch):
        cc = local[j * N_FOX:(j + 1) * N_FOX] + carry
        carry = cc[:, ch - 1:ch]
        c2 = cc * LOG2E
        c_ref[:, j * ch:(j + 1) * ch] = c2
        neg = -c2
        hi = neg.astype(bf16)
        r1 = neg - hi.astype(f32)
        mid = r1.astype(bf16)
        lo = (r1 - mid.astype(f32)).astype(bf16)
        terms = jnp.concatenate([hi, mid, lo, jnp.zeros((LANES - 3 * N_FOX, ch), bf16)], axis=0)
        ca_ref[j * ch:(j + 1) * ch, :] = _dot_nt(eye, terms).astype(bf16)


def _cumsum(lf, S):
    T = lf.shape[1]
    spec = pl.BlockSpec((N_FOX, S), lambda b: (0, b))
    return pl.pallas_call(
        _cumsum_kernel, grid=(T // S,), in_specs=[spec],
        out_specs=[spec, pl.BlockSpec((S, LANES), lambda b: (b, 0))],
        out_shape=[jax.ShapeDtypeStruct((N_FOX, T), f32), jax.ShapeDtypeStruct((T, LANES), bf16)],
        compiler_params=_cparams(("parallel",)), name="cumsum",
    )(lf)


def _fox_kernel(qt_ref, k_ref, ca_ref, vt_ref, c_ref, o_ref, t0_ref, t1_ref, *, tq):
    hp = pl.program_id(1)
    i = pl.program_id(2)
    qt = qt_ref[0]
    row = lax.broadcasted_iota(i32, (LANES, tq), 0)
    is_a = row < HD
    zero = jnp.zeros_like(qt)
    q_ops = []
    for h in range(2):
        ones = jnp.where(((row & 7) == 2 * hp + h) & (row < 3 * N_FOX), 1.0, 0.0).astype(bf16)
        qh = jnp.where(is_a, qt, zero) if h == 0 else jnp.where(is_a, zero, qt)
        q_ops.append(jnp.concatenate([qh, ones], axis=0))
    kr = lax.broadcasted_iota(i32, (tq, tq), 0)
    qc = lax.broadcasted_iota(i32, (tq, tq), 1)
    causal = kr <= qc
    cq = [c_ref[0, h, pl.ds(i, 1), :] for h in range(2)]

    def scores(j, t_ref):
        off = pl.multiple_of(j * tq, tq)
        kblk = jnp.concatenate([k_ref[0, pl.ds(off, tq), :], ca_ref[0, pl.ds(off, tq), :]], axis=1)
        for h in range(2):
            t_ref[h] = _dot(kblk, q_ops[h])

    keep = [jnp.where(is_a, 1.0, 0.0).astype(bf16), jnp.where(is_a, 0.0, 1.0).astype(bf16)]
    ones_row = [jnp.where(row == L_ROW[h], 1.0, 0.0).astype(bf16) for h in range(2)]

    def softmax_pv(j, t_ref, carry, masked):
        vt = vt_ref[j]
        vts = [vt * keep[h] + ones_row[h] for h in range(2)]
        new = []
        for h in range(2):
            m, acc = carry[h]
            t = t_ref[h]
            if masked:
                t = jnp.where(causal, t, NEG)
            m_new = jnp.maximum(m, jnp.max(t, axis=0, keepdims=True) + cq[h])
            alpha = jnp.exp2(m - m_new)
            p = jnp.exp2(t + (cq[h] - m_new))
            acc = alpha * acc + _dot(vts[h], p.astype(bf16))
            new.append((m_new, acc))
        return tuple(new)

    def pair(k, carry):
        j = 2 * k
        scores(j + 1, t1_ref)
        carry = softmax_pv(j, t0_ref, carry, False)
        scores(j + 2, t0_ref)
        return softmax_pv(j + 1, t1_ref, carry, False)

    def odd_tail(carry):
        scores(i, t1_ref)
        carry = softmax_pv(i - 1, t0_ref, carry, False)
        return softmax_pv(i, t1_ref, carry, True)

    def even_tail(carry):
        return softmax_pv(i, t0_ref, carry, True)

    init = tuple((jnp.full((1, tq), NEG, f32), jnp.zeros((LANES, tq), f32)) for _ in range(2))
    scores(0, t0_ref)
    carry = lax.fori_loop(0, i // 2, pair, init)
    (_, acca), (_, accb) = lax.cond(i % 2 == 1, odd_tail, even_tail, carry)
    la = acca[L_ROW[0]:L_ROW[0] + 1, :]
    lb = accb[L_ROW[1]:L_ROW[1] + 1, :]
    ot = jnp.where(is_a, acca / la, accb / lb)
    o_ref[0] = jnp.transpose(ot).astype(bf16)


def _fox(qt, kf, ca, vt, c4, B, S, tq):
    nq = S // tq
    kernel = functools.partial(_fox_kernel, tq=tq)
    return pl.pallas_call(
        kernel,
        grid=(B, N_FOX // 2, nq),
        in_specs=[
            pl.BlockSpec((1, LANES, tq), lambda b, hp, i: (b * nq + i, hp, 0)),
            pl.BlockSpec((1, S, LANES), lambda b, hp, i: (b, 0, hp)),
            pl.BlockSpec((1, S, LANES), lambda b, hp, i: (b, 0, 0)),
            pl.BlockSpec((nq, LANES, tq), lambda b, hp, i: (b, hp, 0)),
            pl.BlockSpec((1, 2, nq, tq), lambda b, hp, i: (hp, 0, b, 0)),
        ],
        out_specs=pl.BlockSpec((1, tq, LANES), lambda b, hp, i: (b, i, hp)),
        out_shape=jax.ShapeDtypeStruct((B, S, FOX_W), bf16),
        scratch_shapes=[pltpu.VMEM((2, tq, tq), f32), pltpu.VMEM((2, tq, tq), f32)],
        compiler_params=_cparams(("parallel", "parallel", "arbitrary")),
        name="fox",
    )(qt, kf, ca, vt, c4)


def _swa_kernel(sink_ref, q_ref, k_ref, v_ref, o_ref, *, tq):
    W = WINDOW
    nsub = tq // W
    n0 = pl.program_id(1) * nsub
    lane = lax.broadcasted_iota(i32, (W, LANES), 1)
    is0 = lane < HD
    rows = lax.broadcasted_iota(i32, (4 * W, 2 * W), 0)
    cols = lax.broadcasted_iota(i32, (4 * W, 2 * W), 1)
    rgrp = lax.broadcasted_iota(i32, (4 * W, 1), 0) // W
    for r in range(nsub):
        nb = n0 + r
        kstart = pl.multiple_of(jnp.maximum(nb * W - W, 0), W)
        ks = k_ref[0, pl.ds(kstart, 2 * W), :]
        vs = v_ref[0, pl.ds(kstart, 2 * W), :]
        qpos = nb * W + rows % W
        kpos = kstart + cols
        valid = (kpos <= qpos) & (qpos - kpos < W)
        outs = []
        for kv in range(2):
            keep = is0 if kv == 0 else jnp.logical_not(is0)
            parts = []
            for g in range(4):
                qg = q_ref[0, r * W:(r + 1) * W, g * LANES:(g + 1) * LANES]
                parts.append(jnp.where(keep, qg, jnp.zeros_like(qg)))
            qstack = jnp.concatenate(parts, axis=0)
            s = jnp.where(valid, _dot_nt(qstack, ks), NEG)
            sink = jnp.zeros((4 * W, 1), f32)
            for g in range(4):
                sink = jnp.where(rgrp == g, sink_ref[kv * 4 + g], sink)
            m = jnp.maximum(jnp.max(s, axis=1, keepdims=True), sink)
            e = jnp.exp(s - m)
            den = jnp.sum(e, axis=1, keepdims=True) + jnp.exp(sink - m)
            outs.append(_dot(e.astype(bf16), vs) / den)
        for g in range(4):
            og = jnp.where(is0, outs[0][g * W:(g + 1) * W], outs[1][g * W:(g + 1) * W])
            o_ref[0, r * W:(r + 1) * W, g * LANES:(g + 1) * LANES] = og.astype(bf16)


def _swa(sinks, qs, ks, vs, B, S, tq=512):
    kernel = functools.partial(_swa_kernel, tq=tq)
    return pl.pallas_call(
        kernel,
        grid=(B, S // tq),
        in_specs=[
            pl.BlockSpec(memory_space=pltpu.SMEM),
            pl.BlockSpec((1, tq, SWA_Q_W), lambda b, i: (b, i, 0)),
            pl.BlockSpec((1, S, SWA_KV_W), lambda b, i: (b, 0, 0)),
            pl.BlockSpec((1, S, SWA_KV_W), lambda b, i: (b, 0, 0)),
        ],
        out_specs=pl.BlockSpec((1, tq, SWA_Q_W), lambda b, i: (b, i, 0)),
        out_shape=jax.ShapeDtypeStruct((B, S, SWA_Q_W), bf16),
        compiler_params=_cparams(("parallel", "arbitrary")),
        name="swa",
    )(sinks, qs, ks, vs)


def _kvproj_kernel(m_ref, w_ref, k_ref, v_ref):
    mb = m_ref[...].astype(bf16)
    k_ref[...] = _dot(mb, w_ref[:, :D]).astype(bf16)
    v_ref[...] = _dot(mb, w_ref[:, D:]).astype(bf16)


def _kvproj(mem2, w_xkv, tm=512):
    R = mem2.shape[0]
    row = pl.BlockSpec((tm, D), lambda i: (i, 0))
    return pl.pallas_call(
        _kvproj_kernel, grid=(R // tm,),
        in_specs=[row, pl.BlockSpec(w_xkv.shape, lambda i: (0, 0))],
        out_specs=[row, row],
        out_shape=[jax.ShapeDtypeStruct((R, D), bf16)] * 2,
        compiler_params=_cparams(("parallel",)), name="kvproj",
    )(mem2, w_xkv)


def _mid_kernel(x_ref, of_ref, os_ref, wo_ref, g1_ref, b1_ref, wq_ref, k_ref, v_ref,
                wxo_ref, g2_ref, b2_ref, wrh_ref, wrl_ref, br_ref,
                h2_ref, bk_ref, oc_ref, *, alpha):
    tm = x_ref.shape[0]
    mix = _dot(of_ref[...], wo_ref[:FOX_W, :]) + _dot(os_ref[...], wo_ref[FOX_W:, :])
    h1 = _layer_norm(alpha * x_ref[...] + mix, g1_ref[...], b1_ref[...])

    q = (_dot(h1.astype(bf16), wq_ref[...]) * 0.0625).astype(bf16)
    for h in range(N_XH):
        sl = slice(h * XHD, (h + 1) * XHD)
        s = _dot_nt(q[:, sl], k_ref[:, sl])
        e = jnp.exp(s - jnp.max(s, axis=1, keepdims=True))
        p = e / jnp.sum(e, axis=1, keepdims=True)
        oc_ref[:, sl] = _dot(p.astype(bf16), v_ref[:, sl]).astype(bf16)
    xo = _dot(oc_ref[...], wxo_ref[...])
    h2 = _layer_norm(alpha * h1 + xo, g2_ref[...], b2_ref[...])
    h2_ref[:, :D] = h2

    hh = h2.astype(bf16)
    hl = (h2 - hh.astype(f32)).astype(bf16)
    lg = _dot(hh, wrh_ref[...]) + _dot(hl, wrh_ref[...]) + _dot(hh, wrl_ref[...]) + br_ref[...]

    lane = lax.broadcasted_iota(i32, (tm, LANES), 1).astype(f32)
    big = float(LANES)

    def first_max(vals, mask):
        vm = jnp.where(mask, vals, NEG)
        top = jnp.max(vm, axis=1, keepdims=True)
        idx = jnp.min(jnp.where(mask & (vm == top), lane, big), axis=1, keepdims=True)
        return top, idx

    gmask = lane < float(N_GROUPS)
    gmax, gidx = first_max(lg, gmask)
    g_val = 1.0 / jnp.sum(jnp.where(gmask, jnp.exp(lg - gmax), 0.0), axis=1, keepdims=True)
    lo = float(N_GROUPS) + float(EPG) * gidx
    emask = (lane >= lo) & (lane < lo + float(EPG))
    v1, i1 = first_max(lg, emask)
    v2, i2 = first_max(lg, emask & (lane != i1))
    ex = jnp.exp(v2 - v1)
    w1 = g_val * (1.0 / (1.0 + ex))
    w2 = g_val * (ex / (1.0 + ex))
    e1 = i1 - lo
    e2 = i2 - lo
    first_low = e1 < e2
    ea = jnp.where(first_low, e1, e2)
    eb = jnp.where(first_low, e2, e1)
    ga = jnp.where(first_low, w1, w2)
    gb = jnp.where(first_low, w2, w1)
    pidx = ea * float(EPG - 1) - ea * (ea - 1.0) * 0.5 + (eb - ea - 1.0)
    bucket = gidx * float(N_PAIRS) + pidx

    lane_i = lax.broadcasted_iota(i32, (tm, LANES), 1)
    h2_ref[:, D:] = jnp.where(lane_i == 0, ga, jnp.where(lane_i == 1, gb, 0.0))
    bk_t = jnp.transpose(jnp.broadcast_to(bucket, (tm, LANES)))
    bk_ref[...] = bk_t[:8, :]


def _mid(x2, of2, os2, w_out, g1, b1, wq, kx, vx, wxo, g2, b2, wrh, wrl, br, alpha, S, tm=512):
    T = x2.shape[0]
    M = kx.shape[0] // (T // S)
    per_b = S // tm
    row = lambda w: pl.BlockSpec((tm, w), lambda i: (i, 0))
    full = lambda a: pl.BlockSpec(a.shape, lambda i: (0,) * a.ndim)
    kvspec = pl.BlockSpec((M, D), lambda i: (i // per_b, 0))
    kernel = functools.partial(_mid_kernel, alpha=alpha)
    return pl.pallas_call(
        kernel,
        grid=(T // tm,),
        in_specs=[row(D), row(512), row(512), full(w_out), full(g1), full(b1), full(wq),
                  kvspec, kvspec, full(wxo), full(g2), full(b2), full(wrh), full(wrl), full(br)],
        out_specs=[row(XW), pl.BlockSpec((8, tm), lambda i: (0, i))],
        out_shape=[jax.ShapeDtypeStruct((T, XW), f32), jax.ShapeDtypeStruct((8, T), f32)],
        scratch_shapes=[pltpu.VMEM((tm, D), bf16)],
        compiler_params=_cparams(("parallel",)),
        name="mid",
    )(x2, of2, os2, w_out, g1, b1, wq, kx, vx, wxo, g2, b2, wrh, wrl, br)


def _rank_kernel(bk_ref, rank_ref, cnt_ref, carry_ref, *, chunk):
    sub = 256

    @pl.when(pl.program_id(0) == 0)
    def _():
        carry_ref[...] = jnp.zeros_like(carry_ref)

    r = lax.broadcasted_iota(i32, (sub, sub), 0)
    c = lax.broadcasted_iota(i32, (sub, sub), 1)
    before = (r < c).astype(bf16)
    bid = lax.broadcasted_iota(i32, (LANES, sub), 0).astype(f32)
    carry = carry_ref[...]
    for j in range(chunk // sub):
        bk = bk_ref[0:1, j * sub:(j + 1) * sub]
        hit = bid == bk
        oh = jnp.where(hit, 1.0, 0.0)
        prior = _dot(oh.astype(bf16), before) + carry
        rank_ref[:, j * sub:(j + 1) * sub] = jnp.sum(jnp.where(hit, prior, 0.0), axis=0, keepdims=True)
        carry = carry + jnp.sum(oh, axis=1, keepdims=True)
    carry_ref[...] = carry
    cnt_ref[...] = carry


def _rank(bk8, chunk=2048):
    T = bk8.shape[1]
    kernel = functools.partial(_rank_kernel, chunk=chunk)
    return pl.pallas_call(
        kernel, grid=(T // chunk,),
        in_specs=[pl.BlockSpec((8, chunk), lambda i: (0, i))],
        out_specs=[pl.BlockSpec((1, chunk), lambda i: (0, i)),
                   pl.BlockSpec((LANES, 1), lambda i: (0, 0))],
        out_shape=[jax.ShapeDtypeStruct((1, T), f32), jax.ShapeDtypeStruct((LANES, 1), f32)],
        scratch_shapes=[pltpu.VMEM((LANES, 1), f32)],
        compiler_params=_cparams(("arbitrary",)), name="rank",
    )(bk8)


def _dest_kernel(bk_ref, rank_ref, ps_ref, dest_ref):
    chunk = bk_ref.shape[1]
    bid = lax.broadcasted_iota(i32, (LANES, chunk), 0).astype(f32)
    start = jnp.sum(jnp.where(bid == bk_ref[0:1, :], ps_ref[...], 0.0), axis=0, keepdims=True)
    dest_ref[...] = (start + rank_ref[...]).astype(i32)


def _dest(bk8, rank, ps_col, chunk=2048):
    T = bk8.shape[1]
    return pl.pallas_call(
        _dest_kernel, grid=(T // chunk,),
        in_specs=[pl.BlockSpec((8, chunk), lambda i: (0, i)), pl.BlockSpec((1, chunk), lambda i: (0, i)),
                  pl.BlockSpec((LANES, 1), lambda i: (0, 0))],
        out_specs=pl.BlockSpec((1, chunk), lambda i: (0, i)),
        out_shape=jax.ShapeDtypeStruct((1, T), i32),
        compiler_params=_cparams(("parallel",)), name="dest",
    )(bk8, rank, ps_col)


def _sc_invert(dest, n_rows):
    T = dest.shape[0]
    assert T & (T - 1) == 0
    lanes = SC_LANES
    mesh = plsc.VectorSubcoreMesh(core_axis_name="core", subcore_axis_name="subcore",
                                  num_cores=SC_CORES, num_subcores=SC_SUBCORES)

    @functools.partial(pl.kernel, out_type=jax.ShapeDtypeStruct((n_rows,), i32), mesh=mesh,
                       scratch_types=[pltpu.VMEM((T,), i32), pltpu.VMEM((n_rows,), i32)],
                       compiler_params=pltpu.CompilerParams(needs_layout_passes=False),
                       name="sc_invert")
    def k(dest_hbm, out_hbm, dest_v, table_v):
        wid = lax.axis_index("subcore") * SC_CORES + lax.axis_index("core")

        @pl.when(wid == 0)
        def _():
            pltpu.sync_copy(dest_hbm, dest_v)
            lane = lax.iota(i32, lanes)

            @pl.loop(0, n_rows // lanes)
            def _(j):
                off = pl.multiple_of(j * lanes, lanes)
                table_v[pl.ds(off, lanes)] = (lane + off) & (T - 1)


            @pl.loop(0, T // lanes)
            def _(j):
                off = pl.multiple_of(j * lanes, lanes)
                plsc.store_scatter(table_v, [dest_v[pl.ds(off, lanes)]], lane + off)

            pltpu.sync_copy(table_v, out_hbm)

    return k(dest)


def _sc_gather_rows(idx, src, chunk=SC_GATHER_ROWS):
    n = idx.shape[0]
    w = src.shape[1]
    workers = SC_CORES * SC_SUBCORES
    per_worker = n // workers
    mesh = plsc.VectorSubcoreMesh(core_axis_name="core", subcore_axis_name="subcore",
                                  num_cores=SC_CORES, num_subcores=SC_SUBCORES)

    n_chunks = per_worker // chunk
    assert n_chunks % 2 == 0

    @functools.partial(pl.kernel, out_type=jax.ShapeDtypeStruct((n, w), src.dtype), mesh=mesh,
                       scratch_types=[pltpu.VMEM((per_worker,), i32), pltpu.VMEM((2, chunk, w), src.dtype),
                                      pltpu.SemaphoreType.DMA((2,)), pltpu.SemaphoreType.DMA((2,))],
                       name="sc_gather_rows")
    def k(src_hbm, idx_hbm, out_hbm, idx_v, rows_v, gsem, wsem):
        wid = lax.axis_index("subcore") * SC_CORES + lax.axis_index("core")
        base = wid * per_worker
        pltpu.sync_copy(idx_hbm.at[pl.ds(base, per_worker)], idx_v)

        def gather(c, slot):
            rows = idx_v.at[pl.ds(pl.multiple_of(c * chunk, chunk), chunk)]
            return pltpu.make_async_copy(src_hbm.at[rows], rows_v.at[slot], gsem.at[slot])

        def write(c, slot):
            out = out_hbm.at[pl.ds(pl.multiple_of(base + c * chunk, chunk), chunk)]
            return pltpu.make_async_copy(rows_v.at[slot], out, wsem.at[slot])

        gather(0, 0).start()

        @pl.loop(0, n_chunks // 2)
        def _(pair):
            c = 2 * pair

            @pl.when(pair > 0)
            def _():
                write(c - 1, 1).wait()

            gather(c + 1, 1).start()
            gather(c, 0).wait()
            write(c, 0).start()

            @pl.when(c + 2 < n_chunks)
            def _():
                write(c, 0).wait()
                gather(c + 2, 0).start()

            gather(c + 1, 1).wait()
            write(c + 1, 1).start()

        write(n_chunks - 2, 0).wait()
        write(n_chunks - 1, 1).wait()

    return k(src, idx)


def _expert_kernel(ea_ref, eb_ref, used_ref, xs_ref, wga_ref, wua_ref, wda_ref,
                   wgb_ref, wub_ref, wdb_ref, y_ref):
    n = pl.program_id(0)

    @pl.when(n < used_ref[0])
    def _():
        x = xs_ref[:, :D].astype(bf16)

        def expert(wg, wu, wd):
            a = _dot(x, wg[0])
            u = _dot(x, wu[0])
            act = a * (1.0 / (1.0 + jnp.exp(-a))) * u
            return _dot(act.astype(bf16), wd[0])

        ga = xs_ref[:, D:D + 1]
        gb = xs_ref[:, D + 1:D + 2]
        y_ref[...] = ga * expert(wga_ref, wua_ref, wda_ref) + gb * expert(wgb_ref, wub_ref, wdb_ref)

    @pl.when(n >= used_ref[0])
    def _():
        y_ref[...] = jnp.zeros_like(y_ref)


def _expert_kernel_into(ea_ref, eb_ref, used_ref, xs_ref, wga_ref, wua_ref, wda_ref,
                        wgb_ref, wub_ref, wdb_ref, ys_ref, y_ref):
    del ys_ref
    _expert_kernel(ea_ref, eb_ref, used_ref, xs_ref, wga_ref, wua_ref, wda_ref, wgb_ref, wub_ref, wdb_ref, y_ref)


def _experts(ea, eb, used, xs, wg, wu, wd, ys, first_block, total_rows):
    nblk = xs.shape[0] // ROW_BLOCK

    def xmap(n, ea, eb, used):
        return (jnp.maximum(jnp.minimum(n, used[0] - 1), 0), 0)

    wa = lambda n, ea, eb, used: (ea[n], 0, 0)
    wb = lambda n, ea, eb, used: (eb[n], 0, 0)
    wgs = lambda m: pl.BlockSpec((1, D, D_EXPERT), m)
    wds = lambda m: pl.BlockSpec((1, D_EXPERT, D), m)
    in_specs = [pl.BlockSpec((ROW_BLOCK, XW), xmap), wgs(wa), wgs(wa), wds(wa), wgs(wb), wgs(wb), wds(wb)]
    operands = [ea, eb, used, xs, wg, wu, wd, wg, wu, wd]
    aliases = {}
    body = _expert_kernel
    if ys is not None:
        in_specs.append(pl.BlockSpec(memory_space=pl.ANY))
        aliases = {len(operands): 0}
        operands.append(ys)
        body = _expert_kernel_into
    grid_spec = pltpu.PrefetchScalarGridSpec(
        num_scalar_prefetch=3, grid=(nblk,), in_specs=in_specs,
        out_specs=pl.BlockSpec((ROW_BLOCK, D), lambda n, ea, eb, used: (n + first_block, 0)),
    )
    return pl.pallas_call(
        body, grid_spec=grid_spec,
        out_shape=jax.ShapeDtypeStruct((total_rows, D), f32),
        input_output_aliases=aliases,
        compiler_params=_cparams(("arbitrary",)), name="experts",
    )(*operands)


def _ln_kernel(h_ref, y_ref, g_ref, b_ref, o_ref, *, alpha):
    o_ref[...] = _layer_norm(alpha * h_ref[...] + y_ref[...], g_ref[...], b_ref[...])


def _final_ln(h2x, y, g, b, alpha, tm=512):
    T = y.shape[0]
    row = pl.BlockSpec((tm, D), lambda i: (i, 0))
    vec = pl.BlockSpec((1, D), lambda i: (0, 0))
    return pl.pallas_call(
        functools.partial(_ln_kernel, alpha=alpha), grid=(T // tm,),
        in_specs=[row, row, vec, vec], out_specs=row,
        out_shape=jax.ShapeDtypeStruct((T, D), f32),
        compiler_params=_cparams(("parallel",)), name="final_ln",
    )(h2x, y, g, b)


def _pair_tables():
    ea = np.zeros((LANES,), np.int32)
    eb = np.zeros((LANES,), np.int32)
    for g in range(N_GROUPS):
        k = 0
        for a in range(EPG):
            for b in range(a + 1, EPG):
                ea[g * N_PAIRS + k] = g * EPG + a
                eb[g * N_PAIRS + k] = g * EPG + b
                k += 1
    return ea, eb


_PAIR_A, _PAIR_B = _pair_tables()


def _layer(h, mem, positions, w_in, b_forget, sinks, w_mix_out, ln_mix_g, ln_mix_b,
           w_xq, w_xkv, w_xout, ln_x_g, ln_x_b, w_rg, b_rg, w_re, b_re,
           w_eg, w_eu, w_ed, ln_f_g, ln_f_b, alpha):
    B, S, _ = h.shape
    T = B * S
    x2 = h.reshape(T, D)
    pos2 = positions.reshape(T, 1).astype(i32)

    o = np.cumsum((0, FOX_W, FOX_W, FOX_W, N_FOX, SWA_Q_W, SWA_KV_W, SWA_KV_W))
    w_qf, w_kf, w_vf, w_fl, w_qs, w_ks, w_vs = (w_in[:, o[i]:o[i + 1]] for i in range(7))
    def regroup(a, axis):
        shp = a.shape
        a = jnp.moveaxis(a, axis, 0).reshape(N_SWA_KV, N_SWA // N_SWA_KV, HD, -1)
        return jnp.moveaxis(jnp.swapaxes(a, 0, 1).reshape(N_SWA * HD, -1), 0, axis).reshape(shp)

    w_all = jnp.concatenate([w_kf, regroup(w_qs, 1), w_ks, w_vs], axis=1).astype(bf16)
    wqt = w_qf.T.astype(bf16)
    wvt = w_vf.T.astype(bf16)
    wfl = w_fl.T.astype(bf16)
    bfc = b_forget.reshape(N_FOX, 1).astype(f32)
    half = HD // 2
    inv_freq = ROPE_THETA ** (-jnp.arange(half, dtype=f32) / half)
    invf = jnp.tile(inv_freq, LANES // half).reshape(1, LANES)
    w_out = jnp.concatenate([w_mix_out[:FOX_W], regroup(w_mix_out[FOX_W:], 0)], axis=0).astype(bf16)

    tq = 512
    (qt, kf, vt, qs, ks, vs, lf), (eg16, eu16, ed16) = _in_proj(
        x2, pos2, w_all, wqt, wvt, wfl, bfc, invf, w_eg, w_eu, w_ed, tq)
    c, ca = _cumsum(lf, S)
    c4 = c.reshape(N_FOX // 2, 2, T // tq, tq)
    r3 = lambda a: a.reshape(B, S, a.shape[-1])
    o_fox = _fox(qt, r3(kf), r3(ca), vt, c4, B, S, tq)
    o_swa = _swa(sinks.astype(f32), r3(qs), r3(ks), r3(vs), B, S)

    kx, vx = _kvproj(mem.reshape(-1, D), w_xkv.astype(bf16))

    wr = jnp.concatenate([w_rg, jnp.transpose(w_re, (1, 0, 2)).reshape(D, N_EXPERTS)], axis=1)
    wr = jnp.pad(wr, ((0, 0), (0, LANES - wr.shape[1]))).astype(f32)
    wrh = wr.astype(bf16)
    wrl = (wr - wrh.astype(f32)).astype(bf16)
    br = jnp.pad(jnp.concatenate([b_rg, b_re.reshape(-1)]), (0, LANES - N_GROUPS - N_EXPERTS))
    br = br.reshape(1, LANES).astype(f32)
    v2 = lambda a: a.reshape(1, D).astype(f32)
    h2x, bk8 = _mid(x2, o_fox.reshape(T, FOX_W), o_swa.reshape(T, SWA_Q_W), w_out,
                    v2(ln_mix_g), v2(ln_mix_b), w_xq.astype(bf16), kx, vx, w_xout.astype(bf16),
                    v2(ln_x_g), v2(ln_x_b), wrh, wrl, br, alpha, S)

    rank, cnt = _rank(bk8)
    counts = cnt[:, 0].astype(i32)
    padded = ((counts + ROW_BLOCK - 1) // ROW_BLOCK) * ROW_BLOCK
    pad_end = jnp.cumsum(padded)
    pad_start = (pad_end - padded).astype(i32)
    step = MOE_CHUNKS * SC_CORES * SC_SUBCORES * 2 * SC_GATHER_ROWS
    P = -(-(T + N_BUCKETS * ROW_BLOCK) // step) * step
    nblk = P // ROW_BLOCK
    used = (pad_end[-1] // ROW_BLOCK).astype(i32).reshape(1)
    blk_row = jnp.arange(nblk, dtype=i32)[:, None] * ROW_BLOCK
    blk_bucket = jnp.minimum(jnp.sum((pad_end[None, :] <= blk_row).astype(i32), axis=1), N_BUCKETS - 1)
    pick = (blk_bucket[:, None] == jnp.arange(LANES, dtype=i32)[None, :]).astype(i32)
    blk_a = jnp.sum(pick * jnp.asarray(_PAIR_A)[None, :], axis=1)
    blk_b = jnp.sum(pick * jnp.asarray(_PAIR_B)[None, :], axis=1)

    dest = _dest(bk8, rank, pad_start.astype(f32).reshape(LANES, 1))[0]
    row_tok = _sc_invert(dest, P)
    cblk = nblk // MOE_CHUNKS
    ys = None
    for cidx in range(MOE_CHUNKS):
        lo = cidx * cblk
        xs = _sc_gather_rows(row_tok[lo * ROW_BLOCK:(lo + cblk) * ROW_BLOCK], h2x)
        used_c = jnp.clip(used - lo, 0, cblk)
        ys = _experts(blk_a[lo:lo + cblk], blk_b[lo:lo + cblk], used_c, xs, eg16, eu16, ed16, ys, lo, P)
    y = _sc_gather_rows(dest, ys)
    out = _final_ln(h2x, y, v2(ln_f_g), v2(ln_f_b), alpha)
    return out.reshape(B, S, D)


def kernel(x, mem, positions, w_in, b_forget, sinks, w_mix_out, ln_mix_g, ln_mix_b, w_xq, w_xkv, w_xout,
           ln_x_g, ln_x_b, w_route_group, b_route_group, w_route_expert, b_route_expert,
           w_exp_gate, w_exp_up, w_exp_down, ln_ffn_g, ln_ffn_b):
    depth = w_in.shape[0]
    alpha = (2.0 * depth) ** 0.25
    h = x
    for l in range(depth):
        h = _layer(h, mem, positions, w_in[l], b_forget[l], sinks[l], w_mix_out[l], ln_mix_g[l], ln_mix_b[l],
                   w_xq[l], w_xkv[l], w_xout[l], ln_x_g[l], ln_x_b[l], w_route_group[l], b_route_group[l],
                   w_route_expert[l], b_route_expert[l], w_exp_gate[l], w_exp_up[l], w_exp_down[l],
                   ln_ffn_g[l], ln_ffn_b[l], alpha)
    return h
```

```python
import functools

import jax
import jax.numpy as jnp
import numpy as np
from jax import lax
from jax.experimental import pallas as pl
from jax.experimental.pallas import tpu as pltpu
from jax.experimental.pallas import tpu_sc as plsc

f32 = jnp.float32
bf16 = jnp.bfloat16
i32 = jnp.int32

D = 1024
HD = 64
N_FOX = 8
N_SWA = 8
N_SWA_KV = 2
FOX_W = 512
SWA_Q_W = 512
SWA_KV_W = 128
WINDOW = 128
ROPE_THETA = 10000.0
N_XH = 4
XHD = 256
N_GROUPS = 4
EPG = 8
N_EXPERTS = 32
D_EXPERT = 512
LN_EPS = 1e-5
NEG = -1e30
LOG2E = 1.4426950408889634
L_ROW = (HD, 0)

SC_CORES = 2
SC_SUBCORES = 16
SC_LANES = 16
SC_GATHER_ROWS = 32
MOE_CHUNKS = 4
LANES = 128
ROW_BLOCK = 128
N_PAIRS = EPG * (EPG - 1) // 2
N_BUCKETS = N_GROUPS * N_PAIRS
XW = D + LANES
VMEM_LIMIT = 56 * 1024 * 1024


def _cparams(sem):
    return pltpu.CompilerParams(dimension_semantics=sem, vmem_limit_bytes=VMEM_LIMIT)


def _layer_norm(v, g, b):
    mu = jnp.mean(v, axis=-1, keepdims=True)
    c = v - mu
    var = jnp.mean(c * c, axis=-1, keepdims=True)
    return c * lax.rsqrt(var + LN_EPS) * g + b


def _dot(a, b):
    return jnp.dot(a, b, preferred_element_type=f32)


def _dot_nt(a, b):
    return lax.dot_general(a, b, (((1,), (1,)), ((), ())), preferred_element_type=f32)


def _inproj_kernel(x_ref, pos_ref, w_ref, wqt_ref, wvt_ref, wfl_ref, bf_ref, invf_ref, eg_ref, eu_ref, ed_ref,
                   qt_ref, kf_ref, vt_ref, qs_ref, ks_ref, vs_ref, lf_ref, egb_ref, eub_ref, edb_ref):
    tm = x_ref.shape[0]
    xb = x_ref[...].astype(bf16)
    egb_ref[...] = eg_ref[...].astype(bf16)
    eub_ref[...] = eu_ref[...].astype(bf16)
    edb_ref[...] = ed_ref[...].astype(bf16)

    def proj(lo, hi):
        return _dot(xb, w_ref[:, lo:hi])

    qt_ref[0] = (_dot_nt(wqt_ref[...], xb) * (0.125 * LOG2E)).astype(bf16)
    vt_ref[0] = _dot_nt(wvt_ref[...], xb).astype(bf16)
    kf_ref[...] = proj(0, 512).astype(bf16)

    ang = pos_ref[...].astype(f32) * invf_ref[...]
    cos = jnp.cos(ang)
    sin = jnp.sin(ang)
    lane = lax.broadcasted_iota(i32, (tm, LANES), 1)
    lo_half = (lane % HD) < (HD // 2)
    sin_s = jnp.where(lo_half, -sin, sin)

    def rope(z):
        rot = jnp.where(lo_half, pltpu.roll(z, LANES - HD // 2, 1), pltpu.roll(z, HD // 2, 1))
        return z * cos + rot * sin_s

    zq = proj(512, 1024)
    for g in range(4):
        sl = slice(g * LANES, (g + 1) * LANES)
        qs_ref[:, sl] = (rope(zq[:, sl]) * 0.125).astype(bf16)
    ks_ref[...] = rope(proj(1024, 1152)).astype(bf16)
    vs_ref[...] = proj(1152, 1280).astype(bf16)

    z = _dot_nt(wfl_ref[...], xb) + bf_ref[...]
    lf_ref[...] = jnp.minimum(z, 0.0) - jnp.log(1.0 + jnp.exp(-jnp.abs(z)))


def _in_proj(x2, pos2, w_all, wqt, wvt, wfl, bfc, invf, w_eg, w_eu, w_ed, tm):
    T = x2.shape[0]
    steps = T // tm
    row = lambda w: pl.BlockSpec((tm, w), lambda i: (i, 0))
    full = lambda a: pl.BlockSpec(a.shape, lambda i: (0,) * a.ndim)
    fmaj = pl.BlockSpec((1, FOX_W, tm), lambda i: (i, 0, 0))
    flat = [w.reshape(-1, w.shape[-1]) for w in (w_eg, w_eu, w_ed)]
    slices = [pl.BlockSpec((w.shape[0] // steps, w.shape[1]), lambda i: (i, 0)) for w in flat]
    outs = pl.pallas_call(
        _inproj_kernel,
        grid=(steps,),
        in_specs=[row(D), row(1), full(w_all), full(wqt), full(wvt), full(wfl), full(bfc), full(invf)] + slices,
        out_specs=[fmaj, row(512), fmaj, row(512), row(128), row(128),
                   pl.BlockSpec((N_FOX, tm), lambda i: (0, i))] + slices,
        out_shape=[jax.ShapeDtypeStruct((steps, FOX_W, tm), bf16), jax.ShapeDtypeStruct((T, 512), bf16),
                   jax.ShapeDtypeStruct((steps, FOX_W, tm), bf16), jax.ShapeDtypeStruct((T, 512), bf16),
                   jax.ShapeDtypeStruct((T, 128), bf16), jax.ShapeDtypeStruct((T, 128), bf16),
                   jax.ShapeDtypeStruct((N_FOX, T), f32)]
        + [jax.ShapeDtypeStruct(w.shape, bf16) for w in flat],
        compiler_params=_cparams(("parallel",)),
        name="in_proj",
    )(x2, pos2, w_all, wqt, wvt, wfl, bfc, invf, *flat)
    experts_bf16 = [o.reshape(w.shape) for o, w in zip(outs[7:], (w_eg, w_eu, w_ed))]
    return outs[:7], experts_bf16


def _cumsum_kernel(lf_ref, c_ref, ca_ref):
    S = lf_ref.shape[1]
    ch = 256
    r = lax.broadcasted_iota(i32, (ch, ch), 0)
    c = lax.broadcasted_iota(i32, (ch, ch), 1)
    tri = (r <= c).astype(f32)
    eye = (r == c).astype(bf16)
    stacked = jnp.concatenate([lf_ref[:, j * ch:(j + 1) * ch] for j in range(S // ch)], axis=0)
    local = jnp.dot(stacked, tri, precision=lax.Precision.HIGHEST, preferred_element_type=f32)
    carry = jnp.zeros((N_FOX, 1), f32)
    for j in range(S // ch):
        cc = local[j * N_FOX:(j + 1) * N_FOX] + carry
        carry = cc[:, ch - 1:ch]
        c2 = cc * LOG2E
        c_ref[:, j * ch:(j + 1) * ch] = c2
        neg = -c2
        hi = neg.astype(bf16)
        r1 = neg - hi.astype(f32)
        mid = r1.astype(bf16)
        lo = (r1 - mid.astype(f32)).astype(bf16)
        terms = jnp.concatenate([hi, mid, lo, jnp.zeros((LANES - 3 * N_FOX, ch), bf16)], axis=0)
        ca_ref[j * ch:(j + 1) * ch, :] = _dot_nt(eye, terms).astype(bf16)


def _cumsum(lf, S):
    T = lf.shape[1]
    spec = pl.BlockSpec((N_FOX, S), lambda b: (0, b))
    return pl.pallas_call(
        _cumsum_kernel, grid=(T // S,), in_specs=[spec],
        out_specs=[spec, pl.BlockSpec((S, LANES), lambda b: (b, 0))],
        out_shape=[jax.ShapeDtypeStruct((N_FOX, T), f32), jax.ShapeDtypeStruct((T, LANES), bf16)],
        compiler_params=_cparams(("parallel",)), name="cumsum",
    )(lf)


def _fox_kernel(qt_ref, k_ref, ca_ref, vt_ref, c_ref, o_ref, t0_ref, t1_ref, *, tq):
    hp = pl.program_id(1)
    i = pl.program_id(2)
    qt = qt_ref[0]
    row = lax.broadcasted_iota(i32, (LANES, tq), 0)
    is_a = row < HD
    zero = jnp.zeros_like(qt)
    q_ops = []
    for h in range(2):
        ones = jnp.where(((row & 7) == 2 * hp + h) & (row < 3 * N_FOX), 1.0, 0.0).astype(bf16)
        qh = jnp.where(is_a, qt, zero) if h == 0 else jnp.where(is_a, zero, qt)
        q_ops.append(jnp.concatenate([qh, ones], axis=0))
    kr = lax.broadcasted_iota(i32, (tq, tq), 0)
    qc = lax.broadcasted_iota(i32, (tq, tq), 1)
    causal = kr <= qc
    cq = [c_ref[0, h, pl.ds(i, 1), :] for h in range(2)]

    def scores(j, t_ref):
        off = pl.multiple_of(j * tq, tq)
        kblk = jnp.concatenate([k_ref[0, pl.ds(off, tq), :], ca_ref[0, pl.ds(off, tq), :]], axis=1)
        for h in range(2):
            t_ref[h] = _dot(kblk, q_ops[h])

    keep = [jnp.where(is_a, 1.0, 0.0).astype(bf16), jnp.where(is_a, 0.0, 1.0).astype(bf16)]
    ones_row = [jnp.where(row == L_ROW[h], 1.0, 0.0).astype(bf16) for h in range(2)]

    def softmax_pv(j, t_ref, carry, masked):
        vt = vt_ref[j]
        vts = [vt * keep[h] + ones_row[h] for h in range(2)]
        new = []
        for h in range(2):
            m, acc = carry[h]
            t = t_ref[h]
            if masked:
                t = jnp.where(causal, t, NEG)
            m_new = jnp.maximum(m, jnp.max(t, axis=0, keepdims=True) + cq[h])
            alpha = jnp.exp2(m - m_new)
            p = jnp.exp2(t + (cq[h] - m_new))
            acc = alpha * acc + _dot(vts[h], p.astype(bf16))
            new.append((m_new, acc))
        return tuple(new)

    def pair(k, carry):
        j = 2 * k
        scores(j + 1, t1_ref)
        carry = softmax_pv(j, t0_ref, carry, False)
        scores(j + 2, t0_ref)
        return softmax_pv(j + 1, t1_ref, carry, False)

    def odd_tail(carry):
        scores(i, t1_ref)
        carry = softmax_pv(i - 1, t0_ref, carry, False)
        return softmax_pv(i, t1_ref, carry, True)

    def even_tail(carry):
        return softmax_pv(i, t0_ref, carry, True)

    init = tuple((jnp.full((1, tq), NEG, f32), jnp.zeros((LANES, tq), f32)) for _ in range(2))
    scores(0, t0_ref)
    carry = lax.fori_loop(0, i // 2, pair, init)
    (_, acca), (_, accb) = lax.cond(i % 2 == 1, odd_tail, even_tail, carry)
    la = acca[L_ROW[0]:L_ROW[0] + 1, :]
    lb = accb[L_ROW[1]:L_ROW[1] + 1, :]
    ot = jnp.where(is_a, acca / la, accb / lb)
    o_ref[0] = jnp.transpose(ot).astype(bf16)


def _fox(qt, kf, ca, vt, c4, B, S, tq):
    nq = S // tq
    kernel = functools.partial(_fox_kernel, tq=tq)
    return pl.pallas_call(
        kernel,
        grid=(B, N_FOX // 2, nq),
        in_specs=[
            pl.BlockSpec((1, LANES, tq), lambda b, hp, i: (b * nq + i, hp, 0)),
            pl.BlockSpec((1, S, LANES), lambda b, hp, i: (b, 0, hp)),
            pl.BlockSpec((1, S, LANES), lambda b, hp, i: (b, 0, 0)),
            pl.BlockSpec((nq, LANES, tq), lambda b, hp, i: (b, hp, 0)),
            pl.BlockSpec((1, 2, nq, tq), lambda b, hp, i: (hp, 0, b, 0)),
        ],
        out_specs=pl.BlockSpec((1, tq, LANES), lambda b, hp, i: (b, i, hp)),
        out_shape=jax.ShapeDtypeStruct((B, S, FOX_W), bf16),
        scratch_shapes=[pltpu.VMEM((2, tq, tq), f32), pltpu.VMEM((2, tq, tq), f32)],
        compiler_params=_cparams(("parallel", "parallel", "arbitrary")),
        name="fox",
    )(qt, kf, ca, vt, c4)


def _swa_kernel(sink_ref, q_ref, k_ref, v_ref, o_ref, *, tq):
    W = WINDOW
    nsub = tq // W
    n0 = pl.program_id(1) * nsub
    lane = lax.broadcasted_iota(i32, (W, LANES), 1)
    is0 = lane < HD
    rows = lax.broadcasted_iota(i32, (4 * W, 2 * W), 0)
    cols = lax.broadcasted_iota(i32, (4 * W, 2 * W), 1)
    rgrp = lax.broadcasted_iota(i32, (4 * W, 1), 0) // W
    for r in range(nsub):
        nb = n0 + r
        kstart = pl.multiple_of(jnp.maximum(nb * W - W, 0), W)
        ks = k_ref[0, pl.ds(kstart, 2 * W), :]
        vs = v_ref[0, pl.ds(kstart, 2 * W), :]
        qpos = nb * W + rows % W
        kpos = kstart + cols
        valid = (kpos <= qpos) & (qpos - kpos < W)
        outs = []
        for kv in range(2):
            keep = is0 if kv == 0 else jnp.logical_not(is0)
            parts = []
            for g in range(4):
                qg = q_ref[0, r * W:(r + 1) * W, g * LANES:(g + 1) * LANES]
                parts.append(jnp.where(keep, qg, jnp.zeros_like(qg)))
            qstack = jnp.concatenate(parts, axis=0)
            s = jnp.where(valid, _dot_nt(qstack, ks), NEG)
            sink = jnp.zeros((4 * W, 1), f32)
            for g in range(4):
                sink = jnp.where(rgrp == g, sink_ref[kv * 4 + g], sink)
            m = jnp.maximum(jnp.max(s, axis=1, keepdims=True), sink)
            e = jnp.exp(s - m)
            den = jnp.sum(e, axis=1, keepdims=True) + jnp.exp(sink - m)
            outs.append(_dot(e.astype(bf16), vs) / den)
        for g in range(4):
            og = jnp.where(is0, outs[0][g * W:(g + 1) * W], outs[1][g * W:(g + 1) * W])
            o_ref[0, r * W:(r + 1) * W, g * LANES:(g + 1) * LANES] = og.astype(bf16)


def _swa(sinks, qs, ks, vs, B, S, tq=512):
    kernel = functools.partial(_swa_kernel, tq=tq)
    return pl.pallas_call(
        kernel,
        grid=(B, S // tq),
        in_specs=[
            pl.BlockSpec(memory_space=pltpu.SMEM),
            pl.BlockSpec((1, tq, SWA_Q_W), lambda b, i: (b, i, 0)),
            pl.BlockSpec((1, S, SWA_KV_W), lambda b, i: (b, 0, 0)),
            pl.BlockSpec((1, S, SWA_KV_W), lambda b, i: (b, 0, 0)),
        ],
        out_specs=pl.BlockSpec((1, tq, SWA_Q_W), lambda b, i: (b, i, 0)),
        out_shape=jax.ShapeDtypeStruct((B, S, SWA_Q_W), bf16),
        compiler_params=_cparams(("parallel", "arbitrary")),
        name="swa",
    )(sinks, qs, ks, vs)


def _kvproj_kernel(m_ref, w_ref, k_ref, v_ref):
    mb = m_ref[...].astype(bf16)
    k_ref[...] = _dot(mb, w_ref[:, :D]).astype(bf16)
    v_ref[...] = _dot(mb, w_ref[:, D:]).astype(bf16)


def _kvproj(mem2, w_xkv, tm=512):
    R = mem2.shape[0]
    row = pl.BlockSpec((tm, D), lambda i: (i, 0))
    return pl.pallas_call(
        _kvproj_kernel, grid=(R // tm,),
        in_specs=[row, pl.BlockSpec(w_xkv.shape, lambda i: (0, 0))],
        out_specs=[row, row],
        out_shape=[jax.ShapeDtypeStruct((R, D), bf16)] * 2,
        compiler_params=_cparams(("parallel",)), name="kvproj",
    )(mem2, w_xkv)


def _mid_kernel(x_ref, of_ref, os_ref, wo_ref, g1_ref, b1_ref, wq_ref, k_ref, v_ref,
                wxo_ref, g2_ref, b2_ref, wrh_ref, wrl_ref, br_ref,
                h2_ref, bk_ref, oc_ref, *, alpha):
    tm = x_ref.shape[0]
    mix = _dot(of_ref[...], wo_ref[:FOX_W, :]) + _dot(os_ref[...], wo_ref[FOX_W:, :])
    h1 = _layer_norm(alpha * x_ref[...] + mix, g1_ref[...], b1_ref[...])

    q = (_dot(h1.astype(bf16), wq_ref[...]) * 0.0625).astype(bf16)
    for h in range(N_XH):
        sl = slice(h * XHD, (h + 1) * XHD)
        s = _dot_nt(q[:, sl], k_ref[:, sl])
        e = jnp.exp(s - jnp.max(s, axis=1, keepdims=True))
        p = e / jnp.sum(e, axis=1, keepdims=True)
        oc_ref[:, sl] = _dot(p.astype(bf16), v_ref[:, sl]).astype(bf16)
    xo = _dot(oc_ref[...], wxo_ref[...])
    h2 = _layer_norm(alpha * h1 + xo, g2_ref[...], b2_ref[...])
    h2_ref[:, :D] = h2

    hh = h2.astype(bf16)
    hl = (h2 - hh.astype(f32)).astype(bf16)
    lg = _dot(hh, wrh_ref[...]) + _dot(hl, wrh_ref[...]) + _dot(hh, wrl_ref[...]) + br_ref[...]

    lane = lax.broadcasted_iota(i32, (tm, LANES), 1).astype(f32)
    big = float(LANES)

    def first_max(vals, mask):
        vm = jnp.where(mask, vals, NEG)
        top = jnp.max(vm, axis=1, keepdims=True)
        idx = jnp.min(jnp.where(mask & (vm == top), lane, big), axis=1, keepdims=True)
        return top, idx

    gmask = lane < float(N_GROUPS)
    gmax, gidx = first_max(lg, gmask)
    g_val = 1.0 / jnp.sum(jnp.where(gmask, jnp.exp(lg - gmax), 0.0), axis=1, keepdims=True)
    lo = float(N_GROUPS) + float(EPG) * gidx
    emask = (lane >= lo) & (lane < lo + float(EPG))
    v1, i1 = first_max(lg, emask)
    v2, i2 = first_max(lg, emask & (lane != i1))
    ex = jnp.exp(v2 - v1)
    w1 = g_val * (1.0 / (1.0 + ex))
    w2 = g_val * (ex / (1.0 + ex))
    e1 = i1 - lo
    e2 = i2 - lo
    first_low = e1 < e2
    ea = jnp.where(first_low, e1, e2)
    eb = jnp.where(first_low, e2, e1)
    ga = jnp.where(first_low, w1, w2)
    gb = jnp.where(first_low, w2, w1)
    pidx = ea * float(EPG - 1) - ea * (ea - 1.0) * 0.5 + (eb - ea - 1.0)
    bucket = gidx * float(N_PAIRS) + pidx

    lane_i = lax.broadcasted_iota(i32, (tm, LANES), 1)
    h2_ref[:, D:] = jnp.where(lane_i == 0, ga, jnp.where(lane_i == 1, gb, 0.0))
    bk_t = jnp.transpose(jnp.broadcast_to(bucket, (tm, LANES)))
    bk_ref[...] = bk_t[:8, :]


def _mid(x2, of2, os2, w_out, g1, b1, wq, kx, vx, wxo, g2, b2, wrh, wrl, br, alpha, S, tm=512):
    T = x2.shape[0]
    M = kx.shape[0] // (T // S)
    per_b = S // tm
    row = lambda w: pl.BlockSpec((tm, w), lambda i: (i, 0))
    full = lambda a: pl.BlockSpec(a.shape, lambda i: (0,) * a.ndim)
    kvspec = pl.BlockSpec((M, D), lambda i: (i // per_b, 0))
    kernel = functools.partial(_mid_kernel, alpha=alpha)
    return pl.pallas_call(
        kernel,
        grid=(T // tm,),
        in_specs=[row(D), row(512), row(512), full(w_out), full(g1), full(b1), full(wq),
                  kvspec, kvspec, full(wxo), full(g2), full(b2), full(wrh), full(wrl), full(br)],
        out_specs=[row(XW), pl.BlockSpec((8, tm), lambda i: (0, i))],
        out_shape=[jax.ShapeDtypeStruct((T, XW), f32), jax.ShapeDtypeStruct((8, T), f32)],
        scratch_shapes=[pltpu.VMEM((tm, D), bf16)],
        compiler_params=_cparams(("parallel",)),
        name="mid",
    )(x2, of2, os2, w_out, g1, b1, wq, kx, vx, wxo, g2, b2, wrh, wrl, br)


def _rank_kernel(bk_ref, rank_ref, cnt_ref, carry_ref, *, chunk):
    sub = 256

    @pl.when(pl.program_id(0) == 0)
    def _():
        carry_ref[...] = jnp.zeros_like(carry_ref)

    r = lax.broadcasted_iota(i32, (sub, sub), 0)
    c = lax.broadcasted_iota(i32, (sub, sub), 1)
    before = (r < c).astype(bf16)
    bid = lax.broadcasted_iota(i32, (LANES, sub), 0).astype(f32)
    carry = carry_ref[...]
    for j in range(chunk // sub):
        bk = bk_ref[0:1, j * sub:(j + 1) * sub]
        hit = bid == bk
        oh = jnp.where(hit, 1.0, 0.0)
        prior = _dot(oh.astype(bf16), before) + carry
        rank_ref[:, j * sub:(j + 1) * sub] = jnp.sum(jnp.where(hit, prior, 0.0), axis=0, keepdims=True)
        carry = carry + jnp.sum(oh, axis=1, keepdims=True)
    carry_ref[...] = carry
    cnt_ref[...] = carry


def _rank(bk8, chunk=2048):
    T = bk8.shape[1]
    kernel = functools.partial(_rank_kernel, chunk=chunk)
    return pl.pallas_call(
        kernel, grid=(T // chunk,),
        in_specs=[pl.BlockSpec((8, chunk), lambda i: (0, i))],
        out_specs=[pl.BlockSpec((1, chunk), lambda i: (0, i)),
                   pl.BlockSpec((LANES, 1), lambda i: (0, 0))],
        out_shape=[jax.ShapeDtypeStruct((1, T), f32), jax.ShapeDtypeStruct((LANES, 1), f32)],
        scratch_shapes=[pltpu.VMEM((LANES, 1), f32)],
        compiler_params=_cparams(("arbitrary",)), name="rank",
    )(bk8)


def _dest_kernel(bk_ref, rank_ref, ps_ref, dest_ref):
    chunk = bk_ref.shape[1]
    bid = lax.broadcasted_iota(i32, (LANES, chunk), 0).astype(f32)
    start = jnp.sum(jnp.where(bid == bk_ref[0:1, :], ps_ref[...], 0.0), axis=0, keepdims=True)
    dest_ref[...] = (start + rank_ref[...]).astype(i32)


def _dest(bk8, rank, ps_col, chunk=2048):
    T = bk8.shape[1]
    return pl.pallas_call(
        _dest_kernel, grid=(T // chunk,),
        in_specs=[pl.BlockSpec((8, chunk), lambda i: (0, i)), pl.BlockSpec((1, chunk), lambda i: (0, i)),
                  pl.BlockSpec((LANES, 1), lambda i: (0, 0))],
        out_specs=pl.BlockSpec((1, chunk), lambda i: (0, i)),
        out_shape=jax.ShapeDtypeStruct((1, T), i32),
        compiler_params=_cparams(("parallel",)), name="dest",
    )(bk8, rank, ps_col)


def _sc_invert(dest, n_rows):
    T = dest.shape[0]
    assert T & (T - 1) == 0
    lanes = SC_LANES
    mesh = plsc.VectorSubcoreMesh(core_axis_name="core", subcore_axis_name="subcore",
                                  num_cores=SC_CORES, num_subcores=SC_SUBCORES)

    @functools.partial(pl.kernel, out_type=jax.ShapeDtypeStruct((n_rows,), i32), mesh=mesh,
                       scratch_types=[pltpu.VMEM((T,), i32), pltpu.VMEM((n_rows,), i32)],
                       compiler_params=pltpu.CompilerParams(needs_layout_passes=False),
                       name="sc_invert")
    def k(dest_hbm, out_hbm, dest_v, table_v):
        wid = lax.axis_index("subcore") * SC_CORES + lax.axis_index("core")

        @pl.when(wid == 0)
        def _():
            pltpu.sync_copy(dest_hbm, dest_v)
            lane = lax.iota(i32, lanes)

            @pl.loop(0, n_rows // lanes)
            def _(j):
                off = pl.multiple_of(j * lanes, lanes)
                table_v[pl.ds(off, lanes)] = (lane + off) & (T - 1)


            @pl.loop(0, T // lanes)
            def _(j):
                off = pl.multiple_of(j * lanes, lanes)
                plsc.store_scatter(table_v, [dest_v[pl.ds(off, lanes)]], lane + off)

            pltpu.sync_copy(table_v, out_hbm)

    return k(dest)


def _sc_gather_rows(idx, src, chunk=SC_GATHER_ROWS):
    n = idx.shape[0]
    w = src.shape[1]
    workers = SC_CORES * SC_SUBCORES
    per_worker = n // workers
    mesh = plsc.VectorSubcoreMesh(core_axis_name="core", subcore_axis_name="subcore",
                                  num_cores=SC_CORES, num_subcores=SC_SUBCORES)

    n_chunks = per_worker // chunk
    assert n_chunks % 2 == 0

    @functools.partial(pl.kernel, out_type=jax.ShapeDtypeStruct((n, w), src.dtype), mesh=mesh,
                       scratch_types=[pltpu.VMEM((per_worker,), i32), pltpu.VMEM((2, chunk, w), src.dtype),
                                      pltpu.SemaphoreType.DMA((2,)), pltpu.SemaphoreType.DMA((2,))],
                       name="sc_gather_rows")
    def k(src_hbm, idx_hbm, out_hbm, idx_v, rows_v, gsem, wsem):
        wid = lax.axis_index("subcore") * SC_CORES + lax.axis_index("core")
        base = wid * per_worker
        pltpu.sync_copy(idx_hbm.at[pl.ds(base, per_worker)], idx_v)

        def gather(c, slot):
            rows = idx_v.at[pl.ds(pl.multiple_of(c * chunk, chunk), chunk)]
            return pltpu.make_async_copy(src_hbm.at[rows], rows_v.at[slot], gsem.at[slot])

        def write(c, slot):
            out = out_hbm.at[pl.ds(pl.multiple_of(base + c * chunk, chunk), chunk)]
            return pltpu.make_async_copy(rows_v.at[slot], out, wsem.at[slot])

        gather(0, 0).start()

        @pl.loop(0, n_chunks // 2)
        def _(pair):
            c = 2 * pair

            @pl.when(pair > 0)
            def _():
                write(c - 1, 1).wait()

            gather(c + 1, 1).start()
            gather(c, 0).wait()
            write(c, 0).start()

            @pl.when(c + 2 < n_chunks)
            def _():
                write(c, 0).wait()
                gather(c + 2, 0).start()

            gather(c + 1, 1).wait()
            write(c + 1, 1).start()

        write(n_chunks - 2, 0).wait()
        write(n_chunks - 1, 1).wait()

    return k(src, idx)


def _expert_kernel(grp_ref, ea_ref, eb_ref, used_ref, xs_ref, wg_ref, wu_ref, wd_ref, y_ref):
    del grp_ref
    n = pl.program_id(0)

    @pl.when(n < used_ref[0])
    def _():
        x = xs_ref[:, :D].astype(bf16)

        def expert(e):
            a = _dot(x, wg_ref[0, e])
            u = _dot(x, wu_ref[0, e])
            act = a * (1.0 / (1.0 + jnp.exp(-a))) * u
            return _dot(act.astype(bf16), wd_ref[0, e])

        ga = xs_ref[:, D:D + 1]
        gb = xs_ref[:, D + 1:D + 2]
        y_ref[...] = ga * expert(ea_ref[n]) + gb * expert(eb_ref[n])

    @pl.when(n >= used_ref[0])
    def _():
        y_ref[...] = jnp.zeros_like(y_ref)


def _expert_kernel_into(grp_ref, ea_ref, eb_ref, used_ref, xs_ref, wg_ref, wu_ref, wd_ref, ys_ref, y_ref):
    del ys_ref
    _expert_kernel(grp_ref, ea_ref, eb_ref, used_ref, xs_ref, wg_ref, wu_ref, wd_ref, y_ref)


def _experts(grp, ea, eb, used, xs, wg, wu, wd, ys, first_block, total_rows):
    nblk = xs.shape[0] // ROW_BLOCK

    def xmap(n, grp, ea, eb, used):
        return (jnp.maximum(jnp.minimum(n, used[0] - 1), 0), 0)

    gmap = lambda n, grp, ea, eb, used: (grp[n], 0, 0, 0)
    gspec = lambda w: pl.BlockSpec((1,) + w.shape[1:], gmap, pipeline_mode=pl.Buffered(1))
    in_specs = [pl.BlockSpec((ROW_BLOCK, XW), xmap), gspec(wg), gspec(wu), gspec(wd)]
    operands = [grp, ea, eb, used, xs, wg, wu, wd]
    aliases = {}
    body = _expert_kernel
    if ys is not None:
        in_specs.append(pl.BlockSpec(memory_space=pl.ANY))
        aliases = {len(operands): 0}
        operands.append(ys)
        body = _expert_kernel_into
    grid_spec = pltpu.PrefetchScalarGridSpec(
        num_scalar_prefetch=4, grid=(nblk,), in_specs=in_specs,
        out_specs=pl.BlockSpec((ROW_BLOCK, D), lambda n, grp, ea, eb, used: (n + first_block, 0)),
    )
    return pl.pallas_call(
        body, grid_spec=grid_spec,
        out_shape=jax.ShapeDtypeStruct((total_rows, D), f32),
        input_output_aliases=aliases,
        compiler_params=_cparams(("arbitrary",)), name="experts",
    )(*operands)


def _ln_kernel(h_ref, y_ref, g_ref, b_ref, o_ref, *, alpha):
    o_ref[...] = _layer_norm(alpha * h_ref[...] + y_ref[...], g_ref[...], b_ref[...])


def _final_ln(h2x, y, g, b, alpha, tm=512):
    T = y.shape[0]
    row = pl.BlockSpec((tm, D), lambda i: (i, 0))
    vec = pl.BlockSpec((1, D), lambda i: (0, 0))
    return pl.pallas_call(
        functools.partial(_ln_kernel, alpha=alpha), grid=(T // tm,),
        in_specs=[row, row, vec, vec], out_specs=row,
        out_shape=jax.ShapeDtypeStruct((T, D), f32),
        compiler_params=_cparams(("parallel",)), name="final_ln",
    )(h2x, y, g, b)


def _pair_tables():
    ea = np.zeros((LANES,), np.int32)
    eb = np.zeros((LANES,), np.int32)
    for g in range(N_GROUPS):
        k = 0
        for a in range(EPG):
            for b in range(a + 1, EPG):
                ea[g * N_PAIRS + k] = a
                eb[g * N_PAIRS + k] = b
                k += 1
    return ea, eb


_PAIR_A, _PAIR_B = _pair_tables()


def _layer(h, mem, positions, w_in, b_forget, sinks, w_mix_out, ln_mix_g, ln_mix_b,
           w_xq, w_xkv, w_xout, ln_x_g, ln_x_b, w_rg, b_rg, w_re, b_re,
           w_eg, w_eu, w_ed, ln_f_g, ln_f_b, alpha):
    B, S, _ = h.shape
    T = B * S
    x2 = h.reshape(T, D)
    pos2 = positions.reshape(T, 1).astype(i32)

    o = np.cumsum((0, FOX_W, FOX_W, FOX_W, N_FOX, SWA_Q_W, SWA_KV_W, SWA_KV_W))
    w_qf, w_kf, w_vf, w_fl, w_qs, w_ks, w_vs = (w_in[:, o[i]:o[i + 1]] for i in range(7))
    def regroup(a, axis):
        shp = a.shape
        a = jnp.moveaxis(a, axis, 0).reshape(N_SWA_KV, N_SWA // N_SWA_KV, HD, -1)
        return jnp.moveaxis(jnp.swapaxes(a, 0, 1).reshape(N_SWA * HD, -1), 0, axis).reshape(shp)

    w_all = jnp.concatenate([w_kf, regroup(w_qs, 1), w_ks, w_vs], axis=1).astype(bf16)
    wqt = w_qf.T.astype(bf16)
    wvt = w_vf.T.astype(bf16)
    wfl = w_fl.T.astype(bf16)
    bfc = b_forget.reshape(N_FOX, 1).astype(f32)
    half = HD // 2
    inv_freq = ROPE_THETA ** (-jnp.arange(half, dtype=f32) / half)
    invf = jnp.tile(inv_freq, LANES // half).reshape(1, LANES)
    w_out = jnp.concatenate([w_mix_out[:FOX_W], regroup(w_mix_out[FOX_W:], 0)], axis=0).astype(bf16)

    tq = 512
    (qt, kf, vt, qs, ks, vs, lf), (eg16, eu16, ed16) = _in_proj(
        x2, pos2, w_all, wqt, wvt, wfl, bfc, invf, w_eg, w_eu, w_ed, tq)
    c, ca = _cumsum(lf, S)
    c4 = c.reshape(N_FOX // 2, 2, T // tq, tq)
    r3 = lambda a: a.reshape(B, S, a.shape[-1])
    o_fox = _fox(qt, r3(kf), r3(ca), vt, c4, B, S, tq)
    o_swa = _swa(sinks.astype(f32), r3(qs), r3(ks), r3(vs), B, S)

    kx, vx = _kvproj(mem.reshape(-1, D), w_xkv.astype(bf16))

    wr = jnp.concatenate([w_rg, jnp.transpose(w_re, (1, 0, 2)).reshape(D, N_EXPERTS)], axis=1)
    wr = jnp.pad(wr, ((0, 0), (0, LANES - wr.shape[1]))).astype(f32)
    wrh = wr.astype(bf16)
    wrl = (wr - wrh.astype(f32)).astype(bf16)
    br = jnp.pad(jnp.concatenate([b_rg, b_re.reshape(-1)]), (0, LANES - N_GROUPS - N_EXPERTS))
    br = br.reshape(1, LANES).astype(f32)
    v2 = lambda a: a.reshape(1, D).astype(f32)
    h2x, bk8 = _mid(x2, o_fox.reshape(T, FOX_W), o_swa.reshape(T, SWA_Q_W), w_out,
                    v2(ln_mix_g), v2(ln_mix_b), w_xq.astype(bf16), kx, vx, w_xout.astype(bf16),
                    v2(ln_x_g), v2(ln_x_b), wrh, wrl, br, alpha, S)

    rank, cnt = _rank(bk8)
    counts = cnt[:, 0].astype(i32)
    padded = ((counts + ROW_BLOCK - 1) // ROW_BLOCK) * ROW_BLOCK
    pad_end = jnp.cumsum(padded)
    pad_start = (pad_end - padded).astype(i32)
    step = MOE_CHUNKS * SC_CORES * SC_SUBCORES * 2 * SC_GATHER_ROWS
    P = -(-(T + N_BUCKETS * ROW_BLOCK) // step) * step
    nblk = P // ROW_BLOCK
    used = (pad_end[-1] // ROW_BLOCK).astype(i32).reshape(1)
    blk_row = jnp.arange(nblk, dtype=i32)[:, None] * ROW_BLOCK
    blk_bucket = jnp.minimum(jnp.sum((pad_end[None, :] <= blk_row).astype(i32), axis=1), N_BUCKETS - 1)
    pick = (blk_bucket[:, None] == jnp.arange(LANES, dtype=i32)[None, :]).astype(i32)
    blk_a = jnp.sum(pick * jnp.asarray(_PAIR_A)[None, :], axis=1)
    blk_b = jnp.sum(pick * jnp.asarray(_PAIR_B)[None, :], axis=1)
    blk_g = blk_bucket // N_PAIRS
    by_group = lambda w: w.reshape((N_GROUPS, EPG) + w.shape[1:])

    dest = _dest(bk8, rank, pad_start.astype(f32).reshape(LANES, 1))[0]
    row_tok = _sc_invert(dest, P)
    cblk = nblk // MOE_CHUNKS
    ys = None
    for cidx in range(MOE_CHUNKS):
        lo = cidx * cblk
        xs = _sc_gather_rows(row_tok[lo * ROW_BLOCK:(lo + cblk) * ROW_BLOCK], h2x)
        used_c = jnp.clip(used - lo, 0, cblk)
        ys = _experts(blk_g[lo:lo + cblk], blk_a[lo:lo + cblk], blk_b[lo:lo + cblk], used_c, xs,
                      by_group(eg16), by_group(eu16), by_group(ed16), ys, lo, P)
    y = _sc_gather_rows(dest, ys)
    out = _final_ln(h2x, y, v2(ln_f_g), v2(ln_f_b), alpha)
    return out.reshape(B, S, D)


def kernel(x, mem, positions, w_in, b_forget, sinks, w_mix_out, ln_mix_g, ln_mix_b, w_xq, w_xkv, w_xout,
           ln_x_g, ln_x_b, w_route_group, b_route_group, w_route_expert, b_route_expert,
           w_exp_gate, w_exp_up, w_exp_down, ln_ffn_g, ln_ffn_b):
    depth = w_in.shape[0]
    alpha = (2.0 * depth) ** 0.25
    h = x
    for l in range(depth):
        h = _layer(h, mem, positions, w_in[l], b_forget[l], sinks[l], w_mix_out[l], ln_mix_g[l], ln_mix_b[l],
                   w_xq[l], w_xkv[l], w_xout[l], ln_x_g[l], ln_x_b[l], w_route_group[l], b_route_group[l],
                   w_route_expert[l], b_route_expert[l], w_exp_gate[l], w_exp_up[l], w_exp_down[l],
                   ln_ffn_g[l], ln_ffn_b[l], alpha)
    return h
```

```python
import functools

import jax
import jax.numpy as jnp
import numpy as np
from jax import lax
from jax.experimental import pallas as pl
from jax.experimental.pallas import tpu as pltpu
from jax.experimental.pallas import tpu_sc as plsc

f32 = jnp.float32
bf16 = jnp.bfloat16
i32 = jnp.int32

D = 1024
HD = 64
N_FOX = 8
N_SWA = 8
N_SWA_KV = 2
FOX_W = 512
SWA_Q_W = 512
SWA_KV_W = 128
WINDOW = 128
ROPE_THETA = 10000.0
N_XH = 4
XHD = 256
N_GROUPS = 4
EPG = 8
N_EXPERTS = 32
D_EXPERT = 512
LN_EPS = 1e-5
NEG = -1e30
LOG2E = 1.4426950408889634
L_ROW = (HD, 0)

SC_CORES = 2
SC_SUBCORES = 16
SC_LANES = 16
SC_GATHER_ROWS = 32
MOE_CHUNKS = 4
LANES = 128
ROW_BLOCK = 128
N_PAIRS = EPG * (EPG - 1) // 2
N_BUCKETS = N_GROUPS * N_PAIRS
XW = D + LANES
VMEM_LIMIT = 56 * 1024 * 1024


def _cparams(sem):
    return pltpu.CompilerParams(dimension_semantics=sem, vmem_limit_bytes=VMEM_LIMIT)


def _layer_norm(v, g, b):
    mu = jnp.mean(v, axis=-1, keepdims=True)
    c = v - mu
    var = jnp.mean(c * c, axis=-1, keepdims=True)
    return c * lax.rsqrt(var + LN_EPS) * g + b


def _dot(a, b):
    return jnp.dot(a, b, preferred_element_type=f32)


def _dot_nt(a, b):
    return lax.dot_general(a, b, (((1,), (1,)), ((), ())), preferred_element_type=f32)


def _inproj_kernel(x_ref, pos_ref, w_ref, wqt_ref, wvt_ref, wfl_ref, bf_ref, invf_ref, eg_ref, eu_ref, ed_ref,
                   qt_ref, kf_ref, vt_ref, qs_ref, ks_ref, vs_ref, lf_ref, egb_ref, eub_ref, edb_ref):
    tm = x_ref.shape[0]
    xb = x_ref[...].astype(bf16)
    egb_ref[...] = eg_ref[...].astype(bf16)
    eub_ref[...] = eu_ref[...].astype(bf16)
    edb_ref[...] = ed_ref[...].astype(bf16)

    def proj(lo, hi):
        return _dot(xb, w_ref[:, lo:hi])

    qt_ref[0] = (_dot_nt(wqt_ref[...], xb) * (0.125 * LOG2E)).astype(bf16)
    vt_ref[0] = _dot_nt(wvt_ref[...], xb).astype(bf16)
    kf_ref[...] = proj(0, 512).astype(bf16)

    ang = pos_ref[...].astype(f32) * invf_ref[...]
    cos = jnp.cos(ang)
    sin = jnp.sin(ang)
    lane = lax.broadcasted_iota(i32, (tm, LANES), 1)
    lo_half = (lane % HD) < (HD // 2)
    sin_s = jnp.where(lo_half, -sin, sin)

    def rope(z):
        rot = jnp.where(lo_half, pltpu.roll(z, LANES - HD // 2, 1), pltpu.roll(z, HD // 2, 1))
        return z * cos + rot * sin_s

    zq = proj(512, 1024)
    for g in range(4):
        sl = slice(g * LANES, (g + 1) * LANES)
        qs_ref[:, sl] = (rope(zq[:, sl]) * 0.125).astype(bf16)
    ks_ref[...] = rope(proj(1024, 1152)).astype(bf16)
    vs_ref[...] = proj(1152, 1280).astype(bf16)

    z = _dot_nt(wfl_ref[...], xb) + bf_ref[...]
    lf_ref[...] = jnp.minimum(z, 0.0) - jnp.log(1.0 + jnp.exp(-jnp.abs(z)))


def _in_proj(x2, pos2, w_all, wqt, wvt, wfl, bfc, invf, w_eg, w_eu, w_ed, tm):
    T = x2.shape[0]
    steps = T // tm
    row = lambda w: pl.BlockSpec((tm, w), lambda i: (i, 0))
    full = lambda a: pl.BlockSpec(a.shape, lambda i: (0,) * a.ndim)
    fmaj = pl.BlockSpec((1, FOX_W, tm), lambda i: (i, 0, 0))
    flat = [w.reshape(-1, w.shape[-1]) for w in (w_eg, w_eu, w_ed)]
    slices = [pl.BlockSpec((w.shape[0] // steps, w.shape[1]), lambda i: (i, 0)) for w in flat]
    outs = pl.pallas_call(
        _inproj_kernel,
        grid=(steps,),
        in_specs=[row(D), row(1), full(w_all), full(wqt), full(wvt), full(wfl), full(bfc), full(invf)] + slices,
        out_specs=[fmaj, row(512), fmaj, row(512), row(128), row(128),
                   pl.BlockSpec((N_FOX, tm), lambda i: (0, i))] + slices,
        out_shape=[jax.ShapeDtypeStruct((steps, FOX_W, tm), bf16), jax.ShapeDtypeStruct((T, 512), bf16),
                   jax.ShapeDtypeStruct((steps, FOX_W, tm), bf16), jax.ShapeDtypeStruct((T, 512), bf16),
                   jax.ShapeDtypeStruct((T, 128), bf16), jax.ShapeDtypeStruct((T, 128), bf16),
                   jax.ShapeDtypeStruct((N_FOX, T), f32)]
        + [jax.ShapeDtypeStruct(w.shape, bf16) for w in flat],
        compiler_params=_cparams(("parallel",)),
        name="in_proj",
    )(x2, pos2, w_all, wqt, wvt, wfl, bfc, invf, *flat)
    experts_bf16 = [o.reshape(w.shape) for o, w in zip(outs[7:], (w_eg, w_eu, w_ed))]
    return outs[:7], experts_bf16


def _cumsum_kernel(lf_ref, c_ref, ca_ref):
    S = lf_ref.shape[1]
    ch = 256
    r = lax.broadcasted_iota(i32, (ch, ch), 0)
    c = lax.broadcasted_iota(i32, (ch, ch), 1)
    tri = (r <= c).astype(f32)
    eye = (r == c).astype(bf16)
    stacked = jnp.concatenate([lf_ref[:, j * ch:(j + 1) * ch] for j in range(S // ch)], axis=0)
    local = jnp.dot(stacked, tri, precision=lax.Precision.HIGHEST, preferred_element_type=f32)
    carry = jnp.zeros((N_FOX, 1), f32)
    for j in range(S // ch):
        cc = local[j * N_FOX:(j + 1) * N_FOX] + carry
        carry = cc[:, ch - 1:ch]
        c2 = cc * LOG2E
        c_ref[:, j * ch:(j + 1) * ch] = c2
        neg = -c2
        hi = neg.astype(bf16)
        r1 = neg - hi.astype(f32)
        mid = r1.astype(bf16)
        lo = (r1 - mid.astype(f32)).astype(bf16)
        terms = jnp.concatenate([hi, mid, lo, jnp.zeros((LANES - 3 * N_FOX, ch), bf16)], axis=0)
        ca_ref[j * ch:(j + 1) * ch, :] = _dot_nt(eye, terms).astype(bf16)


def _cumsum(lf, S):
    T = lf.shape[1]
    spec = pl.BlockSpec((N_FOX, S), lambda b: (0, b))
    return pl.pallas_call(
        _cumsum_kernel, grid=(T // S,), in_specs=[spec],
        out_specs=[spec, pl.BlockSpec((S, LANES), lambda b: (b, 0))],
        out_shape=[jax.ShapeDtypeStruct((N_FOX, T), f32), jax.ShapeDtypeStruct((T, LANES), bf16)],
        compiler_params=_cparams(("parallel",)), name="cumsum",
    )(lf)


def _fox_kernel(qt_ref, k_ref, ca_ref, vt_ref, c_ref, o_ref, t0_ref, t1_ref, *, tq):
    hp = pl.program_id(1)
    i = pl.program_id(2)
    qt = qt_ref[0]
    row = lax.broadcasted_iota(i32, (LANES, tq), 0)
    is_a = row < HD
    zero = jnp.zeros_like(qt)
    q_ops = []
    for h in range(2):
        ones = jnp.where(((row & 7) == 2 * hp + h) & (row < 3 * N_FOX), 1.0, 0.0).astype(bf16)
        qh = jnp.where(is_a, qt, zero) if h == 0 else jnp.where(is_a, zero, qt)
        q_ops.append(jnp.concatenate([qh, ones], axis=0))
    kr = lax.broadcasted_iota(i32, (tq, tq), 0)
    qc = lax.broadcasted_iota(i32, (tq, tq), 1)
    causal = kr <= qc
    cq = [c_ref[0, h, pl.ds(i, 1), :] for h in range(2)]

    def scores(j, t_ref):
        off = pl.multiple_of(j * tq, tq)
        kblk = jnp.concatenate([k_ref[0, pl.ds(off, tq), :], ca_ref[0, pl.ds(off, tq), :]], axis=1)
        for h in range(2):
            t_ref[h] = _dot(kblk, q_ops[h])

    keep = [jnp.where(is_a, 1.0, 0.0).astype(bf16), jnp.where(is_a, 0.0, 1.0).astype(bf16)]
    ones_row = [jnp.where(row == L_ROW[h], 1.0, 0.0).astype(bf16) for h in range(2)]

    def softmax_pv(j, t_ref, carry, masked):
        vt = vt_ref[j]
        vts = [vt * keep[h] + ones_row[h] for h in range(2)]
        new = []
        for h in range(2):
            m, acc = carry[h]
            t = t_ref[h]
            if masked:
                t = jnp.where(causal, t, NEG)
            m_new = jnp.maximum(m, jnp.max(t, axis=0, keepdims=True) + cq[h])
            alpha = jnp.exp2(m - m_new)
            p = jnp.exp2(t + (cq[h] - m_new))
            acc = alpha * acc + _dot(vts[h], p.astype(bf16))
            new.append((m_new, acc))
        return tuple(new)

    def pair(k, carry):
        j = 2 * k
        scores(j + 1, t1_ref)
        carry = softmax_pv(j, t0_ref, carry, False)
        scores(j + 2, t0_ref)
        return softmax_pv(j + 1, t1_ref, carry, False)

    def odd_tail(carry):
        scores(i, t1_ref)
        carry = softmax_pv(i - 1, t0_ref, carry, False)
        return softmax_pv(i, t1_ref, carry, True)

    def even_tail(carry):
        return softmax_pv(i, t0_ref, carry, True)

    init = tuple((jnp.full((1, tq), NEG, f32), jnp.zeros((LANES, tq), f32)) for _ in range(2))
    scores(0, t0_ref)
    carry = lax.fori_loop(0, i // 2, pair, init)
    (_, acca), (_, accb) = lax.cond(i % 2 == 1, odd_tail, even_tail, carry)
    la = acca[L_ROW[0]:L_ROW[0] + 1, :]
    lb = accb[L_ROW[1]:L_ROW[1] + 1, :]
    ot = jnp.where(is_a, acca / la, accb / lb)
    o_ref[0] = jnp.transpose(ot).astype(bf16)


def _fox(qt, kf, ca, vt, c4, B, S, tq):
    nq = S // tq
    kernel = functools.partial(_fox_kernel, tq=tq)
    return pl.pallas_call(
        kernel,
        grid=(B, N_FOX // 2, nq),
        in_specs=[
            pl.BlockSpec((1, LANES, tq), lambda b, hp, i: (b * nq + i, hp, 0)),
            pl.BlockSpec((1, S, LANES), lambda b, hp, i: (b, 0, hp)),
            pl.BlockSpec((1, S, LANES), lambda b, hp, i: (b, 0, 0)),
            pl.BlockSpec((nq, LANES, tq), lambda b, hp, i: (b, hp, 0)),
            pl.BlockSpec((1, 2, nq, tq), lambda b, hp, i: (hp, 0, b, 0)),
        ],
        out_specs=pl.BlockSpec((1, tq, LANES), lambda b, hp, i: (b, i, hp)),
        out_shape=jax.ShapeDtypeStruct((B, S, FOX_W), bf16),
        scratch_shapes=[pltpu.VMEM((2, tq, tq), f32), pltpu.VMEM((2, tq, tq), f32)],
        compiler_params=_cparams(("parallel", "parallel", "arbitrary")),
        name="fox",
    )(qt, kf, ca, vt, c4)


def _swa_kernel(sink_ref, q_ref, k_ref, v_ref, o_ref, *, tq):
    W = WINDOW
    nsub = tq // W
    n0 = pl.program_id(1) * nsub
    lane = lax.broadcasted_iota(i32, (W, LANES), 1)
    is0 = lane < HD
    rows = lax.broadcasted_iota(i32, (4 * W, 2 * W), 0)
    cols = lax.broadcasted_iota(i32, (4 * W, 2 * W), 1)
    rgrp = lax.broadcasted_iota(i32, (4 * W, 1), 0) // W
    for r in range(nsub):
        nb = n0 + r
        kstart = pl.multiple_of(jnp.maximum(nb * W - W, 0), W)
        ks = k_ref[0, pl.ds(kstart, 2 * W), :]
        vs = v_ref[0, pl.ds(kstart, 2 * W), :]
        qpos = nb * W + rows % W
        kpos = kstart + cols
        valid = (kpos <= qpos) & (qpos - kpos < W)
        outs = []
        for kv in range(2):
            keep = is0 if kv == 0 else jnp.logical_not(is0)
            parts = []
            for g in range(4):
                qg = q_ref[0, r * W:(r + 1) * W, g * LANES:(g + 1) * LANES]
                parts.append(jnp.where(keep, qg, jnp.zeros_like(qg)))
            qstack = jnp.concatenate(parts, axis=0)
            s = jnp.where(valid, _dot_nt(qstack, ks), NEG)
            sink = jnp.zeros((4 * W, 1), f32)
            for g in range(4):
                sink = jnp.where(rgrp == g, sink_ref[kv * 4 + g], sink)
            m = jnp.maximum(jnp.max(s, axis=1, keepdims=True), sink)
            e = jnp.exp(s - m)
            den = jnp.sum(e, axis=1, keepdims=True) + jnp.exp(sink - m)
            outs.append(_dot(e.astype(bf16), vs) / den)
        for g in range(4):
            og = jnp.where(is0, outs[0][g * W:(g + 1) * W], outs[1][g * W:(g + 1) * W])
            o_ref[0, r * W:(r + 1) * W, g * LANES:(g + 1) * LANES] = og.astype(bf16)


def _swa(sinks, qs, ks, vs, B, S, tq=512):
    kernel = functools.partial(_swa_kernel, tq=tq)
    return pl.pallas_call(
        kernel,
        grid=(B, S // tq),
        in_specs=[
            pl.BlockSpec(memory_space=pltpu.SMEM),
            pl.BlockSpec((1, tq, SWA_Q_W), lambda b, i: (b, i, 0)),
            pl.BlockSpec((1, S, SWA_KV_W), lambda b, i: (b, 0, 0)),
            pl.BlockSpec((1, S, SWA_KV_W), lambda b, i: (b, 0, 0)),
        ],
        out_specs=pl.BlockSpec((1, tq, SWA_Q_W), lambda b, i: (b, i, 0)),
        out_shape=jax.ShapeDtypeStruct((B, S, SWA_Q_W), bf16),
        compiler_params=_cparams(("parallel", "arbitrary")),
        name="swa",
    )(sinks, qs, ks, vs)


def _kvproj_kernel(m_ref, w_ref, k_ref, v_ref):
    mb = m_ref[...].astype(bf16)
    k_ref[...] = _dot(mb, w_ref[:, :D]).astype(bf16)
    v_ref[...] = _dot(mb, w_ref[:, D:]).astype(bf16)


def _kvproj(mem2, w_xkv, tm=512):
    R = mem2.shape[0]
    row = pl.BlockSpec((tm, D), lambda i: (i, 0))
    return pl.pallas_call(
        _kvproj_kernel, grid=(R // tm,),
        in_specs=[row, pl.BlockSpec(w_xkv.shape, lambda i: (0, 0))],
        out_specs=[row, row],
        out_shape=[jax.ShapeDtypeStruct((R, D), bf16)] * 2,
        compiler_params=_cparams(("parallel",)), name="kvproj",
    )(mem2, w_xkv)


def _mid_kernel(x_ref, of_ref, os_ref, wo_ref, g1_ref, b1_ref, wq_ref, k_ref, v_ref,
                wxo_ref, g2_ref, b2_ref, wrh_ref, wrl_ref, br_ref,
                h2_ref, bk_ref, oc_ref, *, alpha):
    tm = x_ref.shape[0]
    mix = _dot(of_ref[...], wo_ref[:FOX_W, :]) + _dot(os_ref[...], wo_ref[FOX_W:, :])
    h1 = _layer_norm(alpha * x_ref[...] + mix, g1_ref[...], b1_ref[...])

    q = (_dot(h1.astype(bf16), wq_ref[...]) * 0.0625).astype(bf16)
    for h in range(N_XH):
        sl = slice(h * XHD, (h + 1) * XHD)
        s = _dot_nt(q[:, sl], k_ref[:, sl])
        e = jnp.exp(s - jnp.max(s, axis=1, keepdims=True))
        p = e / jnp.sum(e, axis=1, keepdims=True)
        oc_ref[:, sl] = _dot(p.astype(bf16), v_ref[:, sl]).astype(bf16)
    xo = _dot(oc_ref[...], wxo_ref[...])
    h2 = _layer_norm(alpha * h1 + xo, g2_ref[...], b2_ref[...])
    h2_ref[:, :D] = h2

    hh = h2.astype(bf16)
    hl = (h2 - hh.astype(f32)).astype(bf16)
    lg = _dot(hh, wrh_ref[...]) + _dot(hl, wrh_ref[...]) + _dot(hh, wrl_ref[...]) + br_ref[...]

    lane = lax.broadcasted_iota(i32, (tm, LANES), 1).astype(f32)
    big = float(LANES)

    def first_max(vals, mask):
        vm = jnp.where(mask, vals, NEG)
        top = jnp.max(vm, axis=1, keepdims=True)
        idx = jnp.min(jnp.where(mask & (vm == top), lane, big), axis=1, keepdims=True)
        return top, idx

    gmask = lane < float(N_GROUPS)
    gmax, gidx = first_max(lg, gmask)
    g_val = 1.0 / jnp.sum(jnp.where(gmask, jnp.exp(lg - gmax), 0.0), axis=1, keepdims=True)
    lo = float(N_GROUPS) + float(EPG) * gidx
    emask = (lane >= lo) & (lane < lo + float(EPG))
    v1, i1 = first_max(lg, emask)
    v2, i2 = first_max(lg, emask & (lane != i1))
    ex = jnp.exp(v2 - v1)
    w1 = g_val * (1.0 / (1.0 + ex))
    w2 = g_val * (ex / (1.0 + ex))
    e1 = i1 - lo
    e2 = i2 - lo
    first_low = e1 < e2
    ea = jnp.where(first_low, e1, e2)
    eb = jnp.where(first_low, e2, e1)
    ga = jnp.where(first_low, w1, w2)
    gb = jnp.where(first_low, w2, w1)
    pidx = ea * float(EPG - 1) - ea * (ea - 1.0) * 0.5 + (eb - ea - 1.0)
    bucket = gidx * float(N_PAIRS) + pidx

    lane_i = lax.broadcasted_iota(i32, (tm, LANES), 1)
    h2_ref[:, D:] = jnp.where(lane_i == 0, ga, jnp.where(lane_i == 1, gb, 0.0))
    bk_t = jnp.transpose(jnp.broadcast_to(bucket, (tm, LANES)))
    bk_ref[...] = bk_t[:8, :]


def _mid(x2, of2, os2, w_out, g1, b1, wq, kx, vx, wxo, g2, b2, wrh, wrl, br, alpha, S, tm=512):
    T = x2.shape[0]
    M = kx.shape[0] // (T // S)
    per_b = S // tm
    row = lambda w: pl.BlockSpec((tm, w), lambda i: (i, 0))
    full = lambda a: pl.BlockSpec(a.shape, lambda i: (0,) * a.ndim)
    kvspec = pl.BlockSpec((M, D), lambda i: (i // per_b, 0))
    kernel = functools.partial(_mid_kernel, alpha=alpha)
    return pl.pallas_call(
        kernel,
        grid=(T // tm,),
        in_specs=[row(D), row(512), row(512), full(w_out), full(g1), full(b1), full(wq),
                  kvspec, kvspec, full(wxo), full(g2), full(b2), full(wrh), full(wrl), full(br)],
        out_specs=[row(XW), pl.BlockSpec((8, tm), lambda i: (0, i))],
        out_shape=[jax.ShapeDtypeStruct((T, XW), f32), jax.ShapeDtypeStruct((8, T), f32)],
        scratch_shapes=[pltpu.VMEM((tm, D), bf16)],
        compiler_params=_cparams(("parallel",)),
        name="mid",
    )(x2, of2, os2, w_out, g1, b1, wq, kx, vx, wxo, g2, b2, wrh, wrl, br)


def _rank_kernel(bk_ref, rank_ref, cnt_ref, carry_ref, *, chunk):
    sub = 256

    @pl.when(pl.program_id(0) == 0)
    def _():
        carry_ref[...] = jnp.zeros_like(carry_ref)

    r = lax.broadcasted_iota(i32, (sub, sub), 0)
    c = lax.broadcasted_iota(i32, (sub, sub), 1)
    before = (r < c).astype(bf16)
    bid = lax.broadcasted_iota(i32, (LANES, sub), 0).astype(f32)
    carry = carry_ref[...]
    for j in range(chunk // sub):
        bk = bk_ref[0:1, j * sub:(j + 1) * sub]
        hit = bid == bk
        oh = jnp.where(hit, 1.0, 0.0)
        prior = _dot(oh.astype(bf16), before) + carry
        rank_ref[:, j * sub:(j + 1) * sub] = jnp.sum(jnp.where(hit, prior, 0.0), axis=0, keepdims=True)
        carry = carry + jnp.sum(oh, axis=1, keepdims=True)
    carry_ref[...] = carry
    cnt_ref[...] = carry


def _rank(bk8, chunk=2048):
    T = bk8.shape[1]
    kernel = functools.partial(_rank_kernel, chunk=chunk)
    return pl.pallas_call(
        kernel, grid=(T // chunk,),
        in_specs=[pl.BlockSpec((8, chunk), lambda i: (0, i))],
        out_specs=[pl.BlockSpec((1, chunk), lambda i: (0, i)),
                   pl.BlockSpec((LANES, 1), lambda i: (0, 0))],
        out_shape=[jax.ShapeDtypeStruct((1, T), f32), jax.ShapeDtypeStruct((LANES, 1), f32)],
        scratch_shapes=[pltpu.VMEM((LANES, 1), f32)],
        compiler_params=_cparams(("arbitrary",)), name="rank",
    )(bk8)


def _dest_kernel(bk_ref, rank_ref, ps_ref, dest_ref):
    chunk = bk_ref.shape[1]
    bid = lax.broadcasted_iota(i32, (LANES, chunk), 0).astype(f32)
    start = jnp.sum(jnp.where(bid == bk_ref[0:1, :], ps_ref[...], 0.0), axis=0, keepdims=True)
    dest_ref[...] = (start + rank_ref[...]).astype(i32)


def _dest(bk8, rank, ps_col, chunk=2048):
    T = bk8.shape[1]
    return pl.pallas_call(
        _dest_kernel, grid=(T // chunk,),
        in_specs=[pl.BlockSpec((8, chunk), lambda i: (0, i)), pl.BlockSpec((1, chunk), lambda i: (0, i)),
                  pl.BlockSpec((LANES, 1), lambda i: (0, 0))],
        out_specs=pl.BlockSpec((1, chunk), lambda i: (0, i)),
        out_shape=jax.ShapeDtypeStruct((1, T), i32),
        compiler_params=_cparams(("parallel",)), name="dest",
    )(bk8, rank, ps_col)


def _sc_invert(dest, n_rows):
    T = dest.shape[0]
    assert T & (T - 1) == 0
    lanes = SC_LANES
    mesh = plsc.VectorSubcoreMesh(core_axis_name="core", subcore_axis_name="subcore",
                                  num_cores=SC_CORES, num_subcores=SC_SUBCORES)

    @functools.partial(pl.kernel, out_type=jax.ShapeDtypeStruct((n_rows,), i32), mesh=mesh,
                       scratch_types=[pltpu.VMEM((T,), i32), pltpu.VMEM((n_rows,), i32)],
                       compiler_params=pltpu.CompilerParams(needs_layout_passes=False),
                       name="sc_invert")
    def k(dest_hbm, out_hbm, dest_v, table_v):
        wid = lax.axis_index("subcore") * SC_CORES + lax.axis_index("core")

        @pl.when(wid == 0)
        def _():
            pltpu.sync_copy(dest_hbm, dest_v)
            lane = lax.iota(i32, lanes)

            @pl.loop(0, n_rows // lanes)
            def _(j):
                off = pl.multiple_of(j * lanes, lanes)
                table_v[pl.ds(off, lanes)] = (lane + off) & (T - 1)


            @pl.loop(0, T // lanes)
            def _(j):
                off = pl.multiple_of(j * lanes, lanes)
                plsc.store_scatter(table_v, [dest_v[pl.ds(off, lanes)]], lane + off)

            pltpu.sync_copy(table_v, out_hbm)

    return k(dest)


def _sc_gather_rows(idx, src, chunk=SC_GATHER_ROWS):
    n = idx.shape[0]
    w = src.shape[1]
    workers = SC_CORES * SC_SUBCORES
    per_worker = n // workers
    mesh = plsc.VectorSubcoreMesh(core_axis_name="core", subcore_axis_name="subcore",
                                  num_cores=SC_CORES, num_subcores=SC_SUBCORES)

    n_chunks = per_worker // chunk
    assert n_chunks % 2 == 0

    @functools.partial(pl.kernel, out_type=jax.ShapeDtypeStruct((n, w), src.dtype), mesh=mesh,
                       scratch_types=[pltpu.VMEM((per_worker,), i32), pltpu.VMEM((2, chunk, w), src.dtype),
                                      pltpu.SemaphoreType.DMA((2,)), pltpu.SemaphoreType.DMA((2,))],
                       name="sc_gather_rows")
    def k(src_hbm, idx_hbm, out_hbm, idx_v, rows_v, gsem, wsem):
        wid = lax.axis_index("subcore") * SC_CORES + lax.axis_index("core")
        base = wid * per_worker
        pltpu.sync_copy(idx_hbm.at[pl.ds(base, per_worker)], idx_v)

        def gather(c, slot):
            rows = idx_v.at[pl.ds(pl.multiple_of(c * chunk, chunk), chunk)]
            return pltpu.make_async_copy(src_hbm.at[rows], rows_v.at[slot], gsem.at[slot])

        def write(c, slot):
            out = out_hbm.at[pl.ds(pl.multiple_of(base + c * chunk, chunk), chunk)]
            return pltpu.make_async_copy(rows_v.at[slot], out, wsem.at[slot])

        gather(0, 0).start()

        @pl.loop(0, n_chunks // 2)
        def _(pair):
            c = 2 * pair

            @pl.when(pair > 0)
            def _():
                write(c - 1, 1).wait()

            gather(c + 1, 1).start()
            gather(c, 0).wait()
            write(c, 0).start()

            @pl.when(c + 2 < n_chunks)
            def _():
                write(c, 0).wait()
                gather(c + 2, 0).start()

            gather(c + 1, 1).wait()
            write(c + 1, 1).start()

        write(n_chunks - 2, 0).wait()
        write(n_chunks - 1, 1).wait()

    return k(src, idx)


def _expert_kernel(grp_ref, ea_ref, eb_ref, used_ref, xs_ref, wg_ref, wu_ref, wd_ref, g_ref, b_ref, y_ref,
                   *, alpha):
    del grp_ref
    n = pl.program_id(0)

    @pl.when(n < used_ref[0])
    def _():
        h2 = xs_ref[:, :D]
        x = h2.astype(bf16)

        def expert(e):
            a = _dot(x, wg_ref[0, e])
            u = _dot(x, wu_ref[0, e])
            act = a * (1.0 / (1.0 + jnp.exp(-a))) * u
            return _dot(act.astype(bf16), wd_ref[0, e])

        ga = xs_ref[:, D:D + 1]
        gb = xs_ref[:, D + 1:D + 2]
        moe = ga * expert(ea_ref[n]) + gb * expert(eb_ref[n])
        y_ref[...] = _layer_norm(alpha * h2 + moe, g_ref[...], b_ref[...])

    @pl.when(n >= used_ref[0])
    def _():
        y_ref[...] = jnp.zeros_like(y_ref)


def _expert_kernel_into(grp_ref, ea_ref, eb_ref, used_ref, xs_ref, wg_ref, wu_ref, wd_ref, g_ref, b_ref,
                        ys_ref, y_ref, *, alpha):
    del ys_ref
    _expert_kernel(grp_ref, ea_ref, eb_ref, used_ref, xs_ref, wg_ref, wu_ref, wd_ref, g_ref, b_ref, y_ref,
                   alpha=alpha)


def _experts(grp, ea, eb, used, xs, wg, wu, wd, ln_g, ln_b, alpha, ys, first_block, total_rows):
    nblk = xs.shape[0] // ROW_BLOCK

    def xmap(n, grp, ea, eb, used):
        return (jnp.maximum(jnp.minimum(n, used[0] - 1), 0), 0)

    gmap = lambda n, grp, ea, eb, used: (grp[n], 0, 0, 0)
    gspec = lambda w: pl.BlockSpec((1,) + w.shape[1:], gmap, pipeline_mode=pl.Buffered(1))
    vec = pl.BlockSpec((1, D), lambda n, grp, ea, eb, used: (0, 0))
    in_specs = [pl.BlockSpec((ROW_BLOCK, XW), xmap), gspec(wg), gspec(wu), gspec(wd), vec, vec]
    operands = [grp, ea, eb, used, xs, wg, wu, wd, ln_g, ln_b]
    aliases = {}
    body = _expert_kernel
    if ys is not None:
        in_specs.append(pl.BlockSpec(memory_space=pl.ANY))
        aliases = {len(operands): 0}
        operands.append(ys)
        body = _expert_kernel_into
    grid_spec = pltpu.PrefetchScalarGridSpec(
        num_scalar_prefetch=4, grid=(nblk,), in_specs=in_specs,
        out_specs=pl.BlockSpec((ROW_BLOCK, D), lambda n, grp, ea, eb, used: (n + first_block, 0)),
    )
    return pl.pallas_call(
        functools.partial(body, alpha=alpha), grid_spec=grid_spec,
        out_shape=jax.ShapeDtypeStruct((total_rows, D), f32),
        input_output_aliases=aliases,
        compiler_params=_cparams(("arbitrary",)), name="experts",
    )(*operands)


def _pair_tables():
    ea = np.zeros((LANES,), np.int32)
    eb = np.zeros((LANES,), np.int32)
    for g in range(N_GROUPS):
        k = 0
        for a in range(EPG):
            for b in range(a + 1, EPG):
                ea[g * N_PAIRS + k] = a
                eb[g * N_PAIRS + k] = b
                k += 1
    return ea, eb


_PAIR_A, _PAIR_B = _pair_tables()


def _layer(h, mem, positions, w_in, b_forget, sinks, w_mix_out, ln_mix_g, ln_mix_b,
           w_xq, w_xkv, w_xout, ln_x_g, ln_x_b, w_rg, b_rg, w_re, b_re,
           w_eg, w_eu, w_ed, ln_f_g, ln_f_b, alpha):
    B, S, _ = h.shape
    T = B * S
    x2 = h.reshape(T, D)
    pos2 = positions.reshape(T, 1).astype(i32)

    o = np.cumsum((0, FOX_W, FOX_W, FOX_W, N_FOX, SWA_Q_W, SWA_KV_W, SWA_KV_W))
    w_qf, w_kf, w_vf, w_fl, w_qs, w_ks, w_vs = (w_in[:, o[i]:o[i + 1]] for i in range(7))
    def regroup(a, axis):
        shp = a.shape
        a = jnp.moveaxis(a, axis, 0).reshape(N_SWA_KV, N_SWA // N_SWA_KV, HD, -1)
        return jnp.moveaxis(jnp.swapaxes(a, 0, 1).reshape(N_SWA * HD, -1), 0, axis).reshape(shp)

    w_all = jnp.concatenate([w_kf, regroup(w_qs, 1), w_ks, w_vs], axis=1).astype(bf16)
    wqt = w_qf.T.astype(bf16)
    wvt = w_vf.T.astype(bf16)
    wfl = w_fl.T.astype(bf16)
    bfc = b_forget.reshape(N_FOX, 1).astype(f32)
    half = HD // 2
    inv_freq = ROPE_THETA ** (-jnp.arange(half, dtype=f32) / half)
    invf = jnp.tile(inv_freq, LANES // half).reshape(1, LANES)
    w_out = jnp.concatenate([w_mix_out[:FOX_W], regroup(w_mix_out[FOX_W:], 0)], axis=0).astype(bf16)

    tq = 512
    (qt, kf, vt, qs, ks, vs, lf), (eg16, eu16, ed16) = _in_proj(
        x2, pos2, w_all, wqt, wvt, wfl, bfc, invf, w_eg, w_eu, w_ed, tq)
    c, ca = _cumsum(lf, S)
    c4 = c.reshape(N_FOX // 2, 2, T // tq, tq)
    r3 = lambda a: a.reshape(B, S, a.shape[-1])
    o_fox = _fox(qt, r3(kf), r3(ca), vt, c4, B, S, tq)
    o_swa = _swa(sinks.astype(f32), r3(qs), r3(ks), r3(vs), B, S)

    kx, vx = _kvproj(mem.reshape(-1, D), w_xkv.astype(bf16))

    wr = jnp.concatenate([w_rg, jnp.transpose(w_re, (1, 0, 2)).reshape(D, N_EXPERTS)], axis=1)
    wr = jnp.pad(wr, ((0, 0), (0, LANES - wr.shape[1]))).astype(f32)
    wrh = wr.astype(bf16)
    wrl = (wr - wrh.astype(f32)).astype(bf16)
    br = jnp.pad(jnp.concatenate([b_rg, b_re.reshape(-1)]), (0, LANES - N_GROUPS - N_EXPERTS))
    br = br.reshape(1, LANES).astype(f32)
    v2 = lambda a: a.reshape(1, D).astype(f32)
    h2x, bk8 = _mid(x2, o_fox.reshape(T, FOX_W), o_swa.reshape(T, SWA_Q_W), w_out,
                    v2(ln_mix_g), v2(ln_mix_b), w_xq.astype(bf16), kx, vx, w_xout.astype(bf16),
                    v2(ln_x_g), v2(ln_x_b), wrh, wrl, br, alpha, S)

    rank, cnt = _rank(bk8)
    counts = cnt[:, 0].astype(i32)
    padded = ((counts + ROW_BLOCK - 1) // ROW_BLOCK) * ROW_BLOCK
    pad_end = jnp.cumsum(padded)
    pad_start = (pad_end - padded).astype(i32)
    step = MOE_CHUNKS * SC_CORES * SC_SUBCORES * 2 * SC_GATHER_ROWS
    P = -(-(T + N_BUCKETS * ROW_BLOCK) // step) * step
    nblk = P // ROW_BLOCK
    used = (pad_end[-1] // ROW_BLOCK).astype(i32).reshape(1)
    blk_row = jnp.arange(nblk, dtype=i32)[:, None] * ROW_BLOCK
    blk_bucket = jnp.minimum(jnp.sum((pad_end[None, :] <= blk_row).astype(i32), axis=1), N_BUCKETS - 1)
    pick = (blk_bucket[:, None] == jnp.arange(LANES, dtype=i32)[None, :]).astype(i32)
    blk_a = jnp.sum(pick * jnp.asarray(_PAIR_A)[None, :], axis=1)
    blk_b = jnp.sum(pick * jnp.asarray(_PAIR_B)[None, :], axis=1)
    blk_g = blk_bucket // N_PAIRS
    by_group = lambda w: w.reshape((N_GROUPS, EPG) + w.shape[1:])

    dest = _dest(bk8, rank, pad_start.astype(f32).reshape(LANES, 1))[0]
    row_tok = _sc_invert(dest, P)
    cblk = nblk // MOE_CHUNKS
    ys = None
    for cidx in range(MOE_CHUNKS):
        lo = cidx * cblk
        xs = _sc_gather_rows(row_tok[lo * ROW_BLOCK:(lo + cblk) * ROW_BLOCK], h2x)
        used_c = jnp.clip(used - lo, 0, cblk)
        ys = _experts(blk_g[lo:lo + cblk], blk_a[lo:lo + cblk], blk_b[lo:lo + cblk], used_c, xs,
                      by_group(eg16), by_group(eu16), by_group(ed16), v2(ln_f_g), v2(ln_f_b), alpha,
                      ys, lo, P)
    return _sc_gather_rows(dest, ys).reshape(B, S, D)


def kernel(x, mem, positions, w_in, b_forget, sinks, w_mix_out, ln_mix_g, ln_mix_b, w_xq, w_xkv, w_xout,
           ln_x_g, ln_x_b, w_route_group, b_route_group, w_route_expert, b_route_expert,
           w_exp_gate, w_exp_up, w_exp_down, ln_ffn_g, ln_ffn_b):
    depth = w_in.shape[0]
    alpha = (2.0 * depth) ** 0.25
    h = x
    for l in range(depth):
        h = _layer(h, mem, positions, w_in[l], b_forget[l], sinks[l], w_mix_out[l], ln_mix_g[l], ln_mix_b[l],
                   w_xq[l], w_xkv[l], w_xout[l], ln_x_g[l], ln_x_b[l], w_route_group[l], b_route_group[l],
                   w_route_expert[l], b_route_expert[l], w_exp_gate[l], w_exp_up[l], w_exp_down[l],
                   ln_ffn_g[l], ln_ffn_b[l], alpha)
    return h
```

```python
import functools

import jax
import jax.numpy as jnp
import numpy as np
from jax import lax
from jax.experimental import pallas as pl
from jax.experimental.pallas import tpu as pltpu
from jax.experimental.pallas import tpu_sc as plsc

f32 = jnp.float32
bf16 = jnp.bfloat16
i32 = jnp.int32

D = 1024
HD = 64
N_FOX = 8
N_SWA = 8
N_SWA_KV = 2
FOX_W = 512
SWA_Q_W = 512
SWA_KV_W = 128
WINDOW = 128
ROPE_THETA = 10000.0
N_XH = 4
XHD = 256
N_GROUPS = 4
EPG = 8
N_EXPERTS = 32
D_EXPERT = 512
LN_EPS = 1e-5
NEG = -1e30
LOG2E = 1.4426950408889634
L_ROW = (HD, 0)

SC_CORES = 2
SC_SUBCORES = 16
SC_LANES = 16
SC_GATHER_ROWS = 32
MOE_CHUNKS = 4
LANES = 128
ROW_BLOCK = 128
N_PAIRS = EPG * (EPG - 1) // 2
N_BUCKETS = N_GROUPS * N_PAIRS
XW = D + LANES
VMEM_LIMIT = 56 * 1024 * 1024


def _cparams(sem):
    return pltpu.CompilerParams(dimension_semantics=sem, vmem_limit_bytes=VMEM_LIMIT)


def _layer_norm(v, g, b):
    mu = jnp.mean(v, axis=-1, keepdims=True)
    c = v - mu
    var = jnp.mean(c * c, axis=-1, keepdims=True)
    return c * lax.rsqrt(var + LN_EPS) * g + b


def _dot(a, b):
    return jnp.dot(a, b, preferred_element_type=f32)


def _dot_nt(a, b):
    return lax.dot_general(a, b, (((1,), (1,)), ((), ())), preferred_element_type=f32)


def _inproj_kernel(x_ref, pos_ref, w_ref, wqt_ref, wvt_ref, wfl_ref, bf_ref, invf_ref, eg_ref, eu_ref, ed_ref,
                   qt_ref, kf_ref, vt_ref, qs_ref, ks_ref, vs_ref, lf_ref, egb_ref, eub_ref, edb_ref):
    tm = x_ref.shape[0]
    xb = x_ref[...].astype(bf16)
    egb_ref[...] = eg_ref[...].astype(bf16)
    eub_ref[...] = eu_ref[...].astype(bf16)
    edb_ref[...] = ed_ref[...].astype(bf16)

    def proj(lo, hi):
        return _dot(xb, w_ref[:, lo:hi])

    qt_ref[0] = (_dot_nt(wqt_ref[...], xb) * (0.125 * LOG2E)).astype(bf16)
    vt_ref[0] = _dot_nt(wvt_ref[...], xb).astype(bf16)
    kf_ref[...] = proj(0, 512).astype(bf16)

    ang = pos_ref[...].astype(f32) * invf_ref[...]
    cos = jnp.cos(ang)
    sin = jnp.sin(ang)
    lane = lax.broadcasted_iota(i32, (tm, LANES), 1)
    lo_half = (lane % HD) < (HD // 2)
    sin_s = jnp.where(lo_half, -sin, sin)

    def rope(z):
        rot = jnp.where(lo_half, pltpu.roll(z, LANES - HD // 2, 1), pltpu.roll(z, HD // 2, 1))
        return z * cos + rot * sin_s

    zq = proj(512, 1024)
    for g in range(4):
        sl = slice(g * LANES, (g + 1) * LANES)
        qs_ref[:, sl] = (rope(zq[:, sl]) * 0.125).astype(bf16)
    ks_ref[...] = rope(proj(1024, 1152)).astype(bf16)
    vs_ref[...] = proj(1152, 1280).astype(bf16)

    z = _dot_nt(wfl_ref[...], xb) + bf_ref[...]
    lf_ref[...] = jnp.minimum(z, 0.0) - jnp.log(1.0 + jnp.exp(-jnp.abs(z)))


def _in_proj(x2, pos2, w_all, wqt, wvt, wfl, bfc, invf, w_eg, w_eu, w_ed, tm):
    T = x2.shape[0]
    steps = T // tm
    row = lambda w: pl.BlockSpec((tm, w), lambda i: (i, 0))
    full = lambda a: pl.BlockSpec(a.shape, lambda i: (0,) * a.ndim)
    fmaj = pl.BlockSpec((1, FOX_W, tm), lambda i: (i, 0, 0))
    flat = [w.reshape(-1, w.shape[-1]) for w in (w_eg, w_eu, w_ed)]
    slices = [pl.BlockSpec((w.shape[0] // steps, w.shape[1]), lambda i: (i, 0)) for w in flat]
    outs = pl.pallas_call(
        _inproj_kernel,
        grid=(steps,),
        in_specs=[row(D), row(1), full(w_all), full(wqt), full(wvt), full(wfl), full(bfc), full(invf)] + slices,
        out_specs=[fmaj, row(512), fmaj, row(512), row(128), row(128),
                   pl.BlockSpec((N_FOX, tm), lambda i: (0, i))] + slices,
        out_shape=[jax.ShapeDtypeStruct((steps, FOX_W, tm), bf16), jax.ShapeDtypeStruct((T, 512), bf16),
                   jax.ShapeDtypeStruct((steps, FOX_W, tm), bf16), jax.ShapeDtypeStruct((T, 512), bf16),
                   jax.ShapeDtypeStruct((T, 128), bf16), jax.ShapeDtypeStruct((T, 128), bf16),
                   jax.ShapeDtypeStruct((N_FOX, T), f32)]
        + [jax.ShapeDtypeStruct(w.shape, bf16) for w in flat],
        compiler_params=_cparams(("parallel",)),
        name="in_proj",
    )(x2, pos2, w_all, wqt, wvt, wfl, bfc, invf, *flat)
    experts_bf16 = [o.reshape(w.shape) for o, w in zip(outs[7:], (w_eg, w_eu, w_ed))]
    return outs[:7], experts_bf16


def _cumsum_kernel(lf_ref, c_ref, ca_ref):
    S = lf_ref.shape[1]
    ch = 256
    r = lax.broadcasted_iota(i32, (ch, ch), 0)
    c = lax.broadcasted_iota(i32, (ch, ch), 1)
    tri = (r <= c).astype(f32)
    eye = (r == c).astype(bf16)
    stacked = jnp.concatenate([lf_ref[:, j * ch:(j + 1) * ch] for j in range(S // ch)], axis=0)
    local = jnp.dot(stacked, tri, precision=lax.Precision.HIGHEST, preferred_element_type=f32)
    carry = jnp.zeros((N_FOX, 1), f32)
    for j in range(S // ch):
        cc = local[j * N_FOX:(j + 1) * N_FOX] + carry
        carry = cc[:, ch - 1:ch]
        c2 = cc * LOG2E
        c_ref[:, j * ch:(j + 1) * ch] = c2
        neg = -c2
        hi = neg.astype(bf16)
        r1 = neg - hi.astype(f32)
        mid = r1.astype(bf16)
        lo = (r1 - mid.astype(f32)).astype(bf16)
        terms = jnp.concatenate([hi, mid, lo, jnp.zeros((LANES - 3 * N_FOX, ch), bf16)], axis=0)
        ca_ref[j * ch:(j + 1) * ch, :] = _dot_nt(eye, terms).astype(bf16)


def _cumsum(lf, S):
    T = lf.shape[1]
    spec = pl.BlockSpec((N_FOX, S), lambda b: (0, b))
    return pl.pallas_call(
        _cumsum_kernel, grid=(T // S,), in_specs=[spec],
        out_specs=[spec, pl.BlockSpec((S, LANES), lambda b: (b, 0))],
        out_shape=[jax.ShapeDtypeStruct((N_FOX, T), f32), jax.ShapeDtypeStruct((T, LANES), bf16)],
        compiler_params=_cparams(("parallel",)), name="cumsum",
    )(lf)


def _fox_kernel(qt_ref, k_ref, ca_ref, vt_ref, c_ref, o_ref, t0_ref, t1_ref, *, tq):
    hp = pl.program_id(1)
    i = pl.program_id(2)
    qt = qt_ref[0]
    row = lax.broadcasted_iota(i32, (LANES, tq), 0)
    is_a = row < HD
    zero = jnp.zeros_like(qt)
    q_ops = []
    for h in range(2):
        ones = jnp.where(((row & 7) == 2 * hp + h) & (row < 3 * N_FOX), 1.0, 0.0).astype(bf16)
        qh = jnp.where(is_a, qt, zero) if h == 0 else jnp.where(is_a, zero, qt)
        q_ops.append(jnp.concatenate([qh, ones], axis=0))
    kr = lax.broadcasted_iota(i32, (tq, tq), 0)
    qc = lax.broadcasted_iota(i32, (tq, tq), 1)
    causal = kr <= qc
    cq = [c_ref[0, h, pl.ds(i, 1), :] for h in range(2)]

    def scores(j, t_ref):
        off = pl.multiple_of(j * tq, tq)
        kblk = jnp.concatenate([k_ref[0, pl.ds(off, tq), :], ca_ref[0, pl.ds(off, tq), :]], axis=1)
        for h in range(2):
            t_ref[h] = _dot(kblk, q_ops[h])

    keep = [jnp.where(is_a, 1.0, 0.0).astype(bf16), jnp.where(is_a, 0.0, 1.0).astype(bf16)]
    ones_row = [jnp.where(row == L_ROW[h], 1.0, 0.0).astype(bf16) for h in range(2)]

    def softmax_pv(j, t_ref, carry, masked):
        vt = vt_ref[j]
        vts = [vt * keep[h] + ones_row[h] for h in range(2)]
        new = []
        for h in range(2):
            m, acc = carry[h]
            t = t_ref[h]
            if masked:
                t = jnp.where(causal, t, NEG)
            m_new = jnp.maximum(m, jnp.max(t, axis=0, keepdims=True) + cq[h])
            alpha = jnp.exp2(m - m_new)
            p = jnp.exp2(t + (cq[h] - m_new))
            acc = alpha * acc + _dot(vts[h], p.astype(bf16))
            new.append((m_new, acc))
        return tuple(new)

    def pair(k, carry):
        j = 2 * k
        scores(j + 1, t1_ref)
        carry = softmax_pv(j, t0_ref, carry, False)
        scores(j + 2, t0_ref)
        return softmax_pv(j + 1, t1_ref, carry, False)

    def odd_tail(carry):
        scores(i, t1_ref)
        carry = softmax_pv(i - 1, t0_ref, carry, False)
        return softmax_pv(i, t1_ref, carry, True)

    def even_tail(carry):
        return softmax_pv(i, t0_ref, carry, True)

    init = tuple((jnp.full((1, tq), NEG, f32), jnp.zeros((LANES, tq), f32)) for _ in range(2))
    scores(0, t0_ref)
    carry = lax.fori_loop(0, i // 2, pair, init)
    (_, acca), (_, accb) = lax.cond(i % 2 == 1, odd_tail, even_tail, carry)
    la = acca[L_ROW[0]:L_ROW[0] + 1, :]
    lb = accb[L_ROW[1]:L_ROW[1] + 1, :]
    ot = jnp.where(is_a, acca / la, accb / lb)
    o_ref[0] = jnp.transpose(ot).astype(bf16)


def _fox(qt, kf, ca, vt, c4, B, S, tq):
    nq = S // tq
    kernel = functools.partial(_fox_kernel, tq=tq)
    return pl.pallas_call(
        kernel,
        grid=(B, N_FOX // 2, nq),
        in_specs=[
            pl.BlockSpec((1, LANES, tq), lambda b, hp, i: (b * nq + i, hp, 0)),
            pl.BlockSpec((1, S, LANES), lambda b, hp, i: (b, 0, hp)),
            pl.BlockSpec((1, S, LANES), lambda b, hp, i: (b, 0, 0)),
            pl.BlockSpec((nq, LANES, tq), lambda b, hp, i: (b, hp, 0)),
            pl.BlockSpec((1, 2, nq, tq), lambda b, hp, i: (hp, 0, b, 0)),
        ],
        out_specs=pl.BlockSpec((1, tq, LANES), lambda b, hp, i: (b, i, hp)),
        out_shape=jax.ShapeDtypeStruct((B, S, FOX_W), bf16),
        scratch_shapes=[pltpu.VMEM((2, tq, tq), f32), pltpu.VMEM((2, tq, tq), f32)],
        compiler_params=_cparams(("parallel", "parallel", "arbitrary")),
        name="fox",
    )(qt, kf, ca, vt, c4)


def _swa_kernel(sink_ref, q_ref, k_ref, v_ref, o_ref, *, tq):
    W = WINDOW
    nsub = tq // W
    n0 = pl.program_id(1) * nsub
    lane = lax.broadcasted_iota(i32, (W, LANES), 1)
    is0 = lane < HD
    rows = lax.broadcasted_iota(i32, (4 * W, 2 * W), 0)
    cols = lax.broadcasted_iota(i32, (4 * W, 2 * W), 1)
    rgrp = lax.broadcasted_iota(i32, (4 * W, 1), 0) // W
    for r in range(nsub):
        nb = n0 + r
        kstart = pl.multiple_of(jnp.maximum(nb * W - W, 0), W)
        ks = k_ref[0, pl.ds(kstart, 2 * W), :]
        vs = v_ref[0, pl.ds(kstart, 2 * W), :]
        qpos = nb * W + rows % W
        kpos = kstart + cols
        valid = (kpos <= qpos) & (qpos - kpos < W)
        outs = []
        for kv in range(2):
            keep = is0 if kv == 0 else jnp.logical_not(is0)
            parts = []
            for g in range(4):
                qg = q_ref[0, r * W:(r + 1) * W, g * LANES:(g + 1) * LANES]
                parts.append(jnp.where(keep, qg, jnp.zeros_like(qg)))
            qstack = jnp.concatenate(parts, axis=0)
            s = jnp.where(valid, _dot_nt(qstack, ks), NEG)
            sink = jnp.zeros((4 * W, 1), f32)
            for g in range(4):
                sink = jnp.where(rgrp == g, sink_ref[kv * 4 + g], sink)
            m = jnp.maximum(jnp.max(s, axis=1, keepdims=True), sink)
            e = jnp.exp(s - m)
            den = jnp.sum(e, axis=1, keepdims=True) + jnp.exp(sink - m)
            outs.append(_dot(e.astype(bf16), vs) / den)
        for g in range(4):
            og = jnp.where(is0, outs[0][g * W:(g + 1) * W], outs[1][g * W:(g + 1) * W])
            o_ref[0, r * W:(r + 1) * W, g * LANES:(g + 1) * LANES] = og.astype(bf16)


def _swa(sinks, qs, ks, vs, B, S, tq=512):
    kernel = functools.partial(_swa_kernel, tq=tq)
    return pl.pallas_call(
        kernel,
        grid=(B, S // tq),
        in_specs=[
            pl.BlockSpec(memory_space=pltpu.SMEM),
            pl.BlockSpec((1, tq, SWA_Q_W), lambda b, i: (b, i, 0)),
            pl.BlockSpec((1, S, SWA_KV_W), lambda b, i: (b, 0, 0)),
            pl.BlockSpec((1, S, SWA_KV_W), lambda b, i: (b, 0, 0)),
        ],
        out_specs=pl.BlockSpec((1, tq, SWA_Q_W), lambda b, i: (b, i, 0)),
        out_shape=jax.ShapeDtypeStruct((B, S, SWA_Q_W), bf16),
        compiler_params=_cparams(("parallel", "arbitrary")),
        name="swa",
    )(sinks, qs, ks, vs)


def _kvproj_kernel(m_ref, w_ref, k_ref, v_ref):
    mb = m_ref[...].astype(bf16)
    k_ref[...] = _dot(mb, w_ref[:, :D]).astype(bf16)
    v_ref[...] = _dot(mb, w_ref[:, D:]).astype(bf16)


def _kvproj(mem2, w_xkv, tm=512):
    R = mem2.shape[0]
    row = pl.BlockSpec((tm, D), lambda i: (i, 0))
    return pl.pallas_call(
        _kvproj_kernel, grid=(R // tm,),
        in_specs=[row, pl.BlockSpec(w_xkv.shape, lambda i: (0, 0))],
        out_specs=[row, row],
        out_shape=[jax.ShapeDtypeStruct((R, D), bf16)] * 2,
        compiler_params=_cparams(("parallel",)), name="kvproj",
    )(mem2, w_xkv)


def _mid_kernel(x_ref, of_ref, os_ref, wo_ref, g1_ref, b1_ref, wq_ref, k_ref, v_ref,
                wxo_ref, g2_ref, b2_ref, wr_ref, br_ref,
                h2_ref, bk_ref, oc_ref, *, alpha):
    tm = x_ref.shape[0]
    mix = _dot(of_ref[...], wo_ref[:FOX_W, :]) + _dot(os_ref[...], wo_ref[FOX_W:, :])
    h1 = _layer_norm(alpha * x_ref[...] + mix, g1_ref[...], b1_ref[...])

    q = (_dot(h1.astype(bf16), wq_ref[...]) * 0.0625).astype(bf16)
    for h in range(N_XH):
        sl = slice(h * XHD, (h + 1) * XHD)
        s = _dot_nt(q[:, sl], k_ref[:, sl])
        e = jnp.exp(s - jnp.max(s, axis=1, keepdims=True))
        p = e / jnp.sum(e, axis=1, keepdims=True)
        oc_ref[:, sl] = _dot(p.astype(bf16), v_ref[:, sl]).astype(bf16)
    xo = _dot(oc_ref[...], wxo_ref[...])
    h2 = _layer_norm(alpha * h1 + xo, g2_ref[...], b2_ref[...])
    h2_ref[:, :D] = h2

    hh = h2.astype(bf16)
    hl = (h2 - hh.astype(f32)).astype(bf16)
    hi_terms = _dot(hh, wr_ref[...])
    lg = hi_terms[:, :LANES] + _dot(hl, wr_ref[:, :LANES]) + hi_terms[:, LANES:] + br_ref[...]

    lgt = jnp.transpose(lg)
    row = lax.broadcasted_iota(i32, (EPG, tm), 0).astype(f32)
    big = float(EPG)

    def first_max(vals, mask):
        vm = jnp.where(mask, vals, NEG)
        top = jnp.max(vm, axis=0, keepdims=True)
        idx = jnp.min(jnp.where(mask & (vm == top), row, big), axis=0, keepdims=True)
        return top, idx

    gl = lgt[0:EPG]
    gmask = row < float(N_GROUPS)
    gmax, gidx = first_max(gl, gmask)
    g_val = 1.0 / jnp.sum(jnp.where(gmask, jnp.exp(gl - gmax), 0.0), axis=0, keepdims=True)
    sel = jnp.zeros((EPG, tm), f32)
    for g in range(N_GROUPS):
        sel = jnp.where(gidx == float(g), lgt[EPG * (g + 1):EPG * (g + 2)], sel)
    every = row >= 0.0
    v1, e1 = first_max(sel, every)
    v2, e2 = first_max(sel, row != e1)
    ex = jnp.exp(v2 - v1)
    w1 = g_val * (1.0 / (1.0 + ex))
    w2 = g_val * (ex / (1.0 + ex))
    first_low = e1 < e2
    ea = jnp.where(first_low, e1, e2)
    eb = jnp.where(first_low, e2, e1)
    ga = jnp.where(first_low, w1, w2)
    gb = jnp.where(first_low, w2, w1)
    pidx = ea * float(EPG - 1) - ea * (ea - 1.0) * 0.5 + (eb - ea - 1.0)
    bucket = gidx * float(N_PAIRS) + pidx

    bk_ref[...] = jnp.broadcast_to(bucket, (EPG, tm))
    gates = jnp.where(row == 0.0, ga, jnp.where(row == 1.0, gb, 0.0))
    gates = jnp.concatenate([gates, jnp.zeros((LANES - EPG, tm), f32)], axis=0)
    h2_ref[:, D:] = jnp.transpose(gates)


def _mid(x2, of2, os2, w_out, g1, b1, wq, kx, vx, wxo, g2, b2, wr2, br, alpha, S, tm=1024):
    T = x2.shape[0]
    M = kx.shape[0] // (T // S)
    per_b = S // tm
    row = lambda w: pl.BlockSpec((tm, w), lambda i: (i, 0))
    full = lambda a: pl.BlockSpec(a.shape, lambda i: (0,) * a.ndim)
    kvspec = pl.BlockSpec((M, D), lambda i: (i // per_b, 0))
    kernel = functools.partial(_mid_kernel, alpha=alpha)
    return pl.pallas_call(
        kernel,
        grid=(T // tm,),
        in_specs=[row(D), row(512), row(512), full(w_out), full(g1), full(b1), full(wq),
                  kvspec, kvspec, full(wxo), full(g2), full(b2), full(wr2), full(br)],
        out_specs=[row(XW), pl.BlockSpec((8, tm), lambda i: (0, i))],
        out_shape=[jax.ShapeDtypeStruct((T, XW), f32), jax.ShapeDtypeStruct((8, T), f32)],
        scratch_shapes=[pltpu.VMEM((tm, D), bf16)],
        compiler_params=_cparams(("parallel",)),
        name="mid",
    )(x2, of2, os2, w_out, g1, b1, wq, kx, vx, wxo, g2, b2, wr2, br)


def _rank_kernel(bk_ref, rank_ref, cnt_ref, carry_ref, *, chunk):
    sub = 256

    @pl.when(pl.program_id(0) == 0)
    def _():
        carry_ref[...] = jnp.zeros_like(carry_ref)

    r = lax.broadcasted_iota(i32, (sub, sub), 0)
    c = lax.broadcasted_iota(i32, (sub, sub), 1)
    before = (r < c).astype(bf16)
    bid = lax.broadcasted_iota(i32, (LANES, sub), 0).astype(f32)
    carry = carry_ref[...]
    for j in range(chunk // sub):
        bk = bk_ref[0:1, j * sub:(j + 1) * sub]
        hit = bid == bk
        oh = jnp.where(hit, 1.0, 0.0)
        prior = _dot(oh.astype(bf16), before) + carry
        rank_ref[:, j * sub:(j + 1) * sub] = jnp.sum(jnp.where(hit, prior, 0.0), axis=0, keepdims=True)
        carry = carry + jnp.sum(oh, axis=1, keepdims=True)
    carry_ref[...] = carry
    cnt_ref[...] = carry


def _rank(bk8, chunk=2048):
    T = bk8.shape[1]
    kernel = functools.partial(_rank_kernel, chunk=chunk)
    return pl.pallas_call(
        kernel, grid=(T // chunk,),
        in_specs=[pl.BlockSpec((8, chunk), lambda i: (0, i))],
        out_specs=[pl.BlockSpec((1, chunk), lambda i: (0, i)),
                   pl.BlockSpec((LANES, 1), lambda i: (0, 0))],
        out_shape=[jax.ShapeDtypeStruct((1, T), f32), jax.ShapeDtypeStruct((LANES, 1), f32)],
        scratch_shapes=[pltpu.VMEM((LANES, 1), f32)],
        compiler_params=_cparams(("arbitrary",)), name="rank",
    )(bk8)


def _dest_kernel(bk_ref, rank_ref, ps_ref, dest_ref):
    chunk = bk_ref.shape[1]
    bid = lax.broadcasted_iota(i32, (LANES, chunk), 0).astype(f32)
    start = jnp.sum(jnp.where(bid == bk_ref[0:1, :], ps_ref[...], 0.0), axis=0, keepdims=True)
    dest_ref[...] = (start + rank_ref[...]).astype(i32)


def _dest(bk8, rank, ps_col, chunk=2048):
    T = bk8.shape[1]
    return pl.pallas_call(
        _dest_kernel, grid=(T // chunk,),
        in_specs=[pl.BlockSpec((8, chunk), lambda i: (0, i)), pl.BlockSpec((1, chunk), lambda i: (0, i)),
                  pl.BlockSpec((LANES, 1), lambda i: (0, 0))],
        out_specs=pl.BlockSpec((1, chunk), lambda i: (0, i)),
        out_shape=jax.ShapeDtypeStruct((1, T), i32),
        compiler_params=_cparams(("parallel",)), name="dest",
    )(bk8, rank, ps_col)


def _sc_invert(dest, n_rows):
    T = dest.shape[0]
    assert T & (T - 1) == 0
    lanes = SC_LANES
    mesh = plsc.VectorSubcoreMesh(core_axis_name="core", subcore_axis_name="subcore",
                                  num_cores=SC_CORES, num_subcores=SC_SUBCORES)

    @functools.partial(pl.kernel, out_type=jax.ShapeDtypeStruct((n_rows,), i32), mesh=mesh,
                       scratch_types=[pltpu.VMEM((T,), i32), pltpu.VMEM((n_rows,), i32)],
                       compiler_params=pltpu.CompilerParams(needs_layout_passes=False),
                       name="sc_invert")
    def k(dest_hbm, out_hbm, dest_v, table_v):
        wid = lax.axis_index("subcore") * SC_CORES + lax.axis_index("core")

        @pl.when(wid == 0)
        def _():
            pltpu.sync_copy(dest_hbm, dest_v)
            lane = lax.iota(i32, lanes)

            @pl.loop(0, n_rows // lanes)
            def _(j):
                off = pl.multiple_of(j * lanes, lanes)
                table_v[pl.ds(off, lanes)] = (lane + off) & (T - 1)


            @pl.loop(0, T // lanes)
            def _(j):
                off = pl.multiple_of(j * lanes, lanes)
                plsc.store_scatter(table_v, [dest_v[pl.ds(off, lanes)]], lane + off)

            pltpu.sync_copy(table_v, out_hbm)

    return k(dest)


def _sc_gather_rows(idx, src, chunk=SC_GATHER_ROWS):
    n = idx.shape[0]
    w = src.shape[1]
    workers = SC_CORES * SC_SUBCORES
    per_worker = n // workers
    mesh = plsc.VectorSubcoreMesh(core_axis_name="core", subcore_axis_name="subcore",
                                  num_cores=SC_CORES, num_subcores=SC_SUBCORES)

    n_chunks = per_worker // chunk
    assert n_chunks % 2 == 0

    @functools.partial(pl.kernel, out_type=jax.ShapeDtypeStruct((n, w), src.dtype), mesh=mesh,
                       scratch_types=[pltpu.VMEM((per_worker,), i32), pltpu.VMEM((2, chunk, w), src.dtype),
                                      pltpu.SemaphoreType.DMA((2,)), pltpu.SemaphoreType.DMA((2,))],
                       name="sc_gather_rows")
    def k(src_hbm, idx_hbm, out_hbm, idx_v, rows_v, gsem, wsem):
        wid = lax.axis_index("subcore") * SC_CORES + lax.axis_index("core")
        base = wid * per_worker
        pltpu.sync_copy(idx_hbm.at[pl.ds(base, per_worker)], idx_v)

        def gather(c, slot):
            rows = idx_v.at[pl.ds(pl.multiple_of(c * chunk, chunk), chunk)]
            return pltpu.make_async_copy(src_hbm.at[rows], rows_v.at[slot], gsem.at[slot])

        def write(c, slot):
            out = out_hbm.at[pl.ds(pl.multiple_of(base + c * chunk, chunk), chunk)]
            return pltpu.make_async_copy(rows_v.at[slot], out, wsem.at[slot])

        gather(0, 0).start()

        @pl.loop(0, n_chunks // 2)
        def _(pair):
            c = 2 * pair

            @pl.when(pair > 0)
            def _():
                write(c - 1, 1).wait()

            gather(c + 1, 1).start()
            gather(c, 0).wait()
            write(c, 0).start()

            @pl.when(c + 2 < n_chunks)
            def _():
                write(c, 0).wait()
                gather(c + 2, 0).start()

            gather(c + 1, 1).wait()
            write(c + 1, 1).start()

        write(n_chunks - 2, 0).wait()
        write(n_chunks - 1, 1).wait()

    return k(src, idx)


def _expert_kernel(grp_ref, ea_ref, eb_ref, used_ref, xs_ref, wg_ref, wu_ref, wd_ref, g_ref, b_ref, y_ref,
                   *, alpha):
    del grp_ref
    n = pl.program_id(0)

    @pl.when(n < used_ref[0])
    def _():
        h2 = xs_ref[:, :D]
        x = h2.astype(bf16)

        def expert(e):
            a = _dot(x, wg_ref[0, e])
            u = _dot(x, wu_ref[0, e])
            act = a * (1.0 / (1.0 + jnp.exp(-a))) * u
            return _dot(act.astype(bf16), wd_ref[0, e])

        ga = xs_ref[:, D:D + 1]
        gb = xs_ref[:, D + 1:D + 2]
        moe = ga * expert(ea_ref[n]) + gb * expert(eb_ref[n])
        y_ref[...] = _layer_norm(alpha * h2 + moe, g_ref[...], b_ref[...])

    @pl.when(n >= used_ref[0])
    def _():
        y_ref[...] = jnp.zeros_like(y_ref)


def _expert_kernel_into(grp_ref, ea_ref, eb_ref, used_ref, xs_ref, wg_ref, wu_ref, wd_ref, g_ref, b_ref,
                        ys_ref, y_ref, *, alpha):
    del ys_ref
    _expert_kernel(grp_ref, ea_ref, eb_ref, used_ref, xs_ref, wg_ref, wu_ref, wd_ref, g_ref, b_ref, y_ref,
                   alpha=alpha)


def _experts(grp, ea, eb, used, xs, wg, wu, wd, ln_g, ln_b, alpha, ys, first_block, total_rows):
    nblk = xs.shape[0] // ROW_BLOCK

    def xmap(n, grp, ea, eb, used):
        return (jnp.maximum(jnp.minimum(n, used[0] - 1), 0), 0)

    gmap = lambda n, grp, ea, eb, used: (grp[n], 0, 0, 0)
    gspec = lambda w: pl.BlockSpec((1,) + w.shape[1:], gmap, pipeline_mode=pl.Buffered(1))
    vec = pl.BlockSpec((1, D), lambda n, grp, ea, eb, used: (0, 0))
    in_specs = [pl.BlockSpec((ROW_BLOCK, XW), xmap), gspec(wg), gspec(wu), gspec(wd), vec, vec]
    operands = [grp, ea, eb, used, xs, wg, wu, wd, ln_g, ln_b]
    aliases = {}
    body = _expert_kernel
    if ys is not None:
        in_specs.append(pl.BlockSpec(memory_space=pl.ANY))
        aliases = {len(operands): 0}
        operands.append(ys)
        body = _expert_kernel_into
    grid_spec = pltpu.PrefetchScalarGridSpec(
        num_scalar_prefetch=4, grid=(nblk,), in_specs=in_specs,
        out_specs=pl.BlockSpec((ROW_BLOCK, D), lambda n, grp, ea, eb, used: (n + first_block, 0)),
    )
    return pl.pallas_call(
        functools.partial(body, alpha=alpha), grid_spec=grid_spec,
        out_shape=jax.ShapeDtypeStruct((total_rows, D), f32),
        input_output_aliases=aliases,
        compiler_params=_cparams(("arbitrary",)), name="experts",
    )(*operands)


def _pair_tables():
    ea = np.zeros((LANES,), np.int32)
    eb = np.zeros((LANES,), np.int32)
    for g in range(N_GROUPS):
        k = 0
        for a in range(EPG):
            for b in range(a + 1, EPG):
                ea[g * N_PAIRS + k] = a
                eb[g * N_PAIRS + k] = b
                k += 1
    return ea, eb


_PAIR_A, _PAIR_B = _pair_tables()


def _layer(h, mem, positions, w_in, b_forget, sinks, w_mix_out, ln_mix_g, ln_mix_b,
           w_xq, w_xkv, w_xout, ln_x_g, ln_x_b, w_rg, b_rg, w_re, b_re,
           w_eg, w_eu, w_ed, ln_f_g, ln_f_b, alpha):
    B, S, _ = h.shape
    T = B * S
    x2 = h.reshape(T, D)
    pos2 = positions.reshape(T, 1).astype(i32)

    o = np.cumsum((0, FOX_W, FOX_W, FOX_W, N_FOX, SWA_Q_W, SWA_KV_W, SWA_KV_W))
    w_qf, w_kf, w_vf, w_fl, w_qs, w_ks, w_vs = (w_in[:, o[i]:o[i + 1]] for i in range(7))
    def regroup(a, axis):
        shp = a.shape
        a = jnp.moveaxis(a, axis, 0).reshape(N_SWA_KV, N_SWA // N_SWA_KV, HD, -1)
        return jnp.moveaxis(jnp.swapaxes(a, 0, 1).reshape(N_SWA * HD, -1), 0, axis).reshape(shp)

    w_all = jnp.concatenate([w_kf, regroup(w_qs, 1), w_ks, w_vs], axis=1).astype(bf16)
    wqt = w_qf.T.astype(bf16)
    wvt = w_vf.T.astype(bf16)
    wfl = w_fl.T.astype(bf16)
    bfc = b_forget.reshape(N_FOX, 1).astype(f32)
    half = HD // 2
    inv_freq = ROPE_THETA ** (-jnp.arange(half, dtype=f32) / half)
    invf = jnp.tile(inv_freq, LANES // half).reshape(1, LANES)
    w_out = jnp.concatenate([w_mix_out[:FOX_W], regroup(w_mix_out[FOX_W:], 0)], axis=0).astype(bf16)

    tq = 512
    (qt, kf, vt, qs, ks, vs, lf), (eg16, eu16, ed16) = _in_proj(
        x2, pos2, w_all, wqt, wvt, wfl, bfc, invf, w_eg, w_eu, w_ed, tq)
    c, ca = _cumsum(lf, S)
    c4 = c.reshape(N_FOX // 2, 2, T // tq, tq)
    r3 = lambda a: a.reshape(B, S, a.shape[-1])
    o_fox = _fox(qt, r3(kf), r3(ca), vt, c4, B, S, tq)
    o_swa = _swa(sinks.astype(f32), r3(qs), r3(ks), r3(vs), B, S)

    kx, vx = _kvproj(mem.reshape(-1, D), w_xkv.astype(bf16))

    gpad = EPG - N_GROUPS
    wr = jnp.concatenate([jnp.pad(w_rg, ((0, 0), (0, gpad))),
                          jnp.transpose(w_re, (1, 0, 2)).reshape(D, N_EXPERTS)], axis=1)
    wr = jnp.pad(wr, ((0, 0), (0, LANES - wr.shape[1]))).astype(f32)
    wrh = wr.astype(bf16)
    wr2 = jnp.concatenate([wrh, (wr - wrh.astype(f32)).astype(bf16)], axis=1)
    br = jnp.pad(jnp.concatenate([jnp.pad(b_rg, (0, gpad)), b_re.reshape(-1)]), (0, LANES - EPG - N_EXPERTS))
    br = br.reshape(1, LANES).astype(f32)
    v2 = lambda a: a.reshape(1, D).astype(f32)
    h2x, bk8 = _mid(x2, o_fox.reshape(T, FOX_W), o_swa.reshape(T, SWA_Q_W), w_out,
                    v2(ln_mix_g), v2(ln_mix_b), w_xq.astype(bf16), kx, vx, w_xout.astype(bf16),
                    v2(ln_x_g), v2(ln_x_b), wr2, br, alpha, S)

    rank, cnt = _rank(bk8)
    counts = cnt[:, 0].astype(i32)
    padded = ((counts + ROW_BLOCK - 1) // ROW_BLOCK) * ROW_BLOCK
    pad_end = jnp.cumsum(padded)
    pad_start = (pad_end - padded).astype(i32)
    step = MOE_CHUNKS * SC_CORES * SC_SUBCORES * 2 * SC_GATHER_ROWS
    P = -(-(T + N_BUCKETS * ROW_BLOCK) // step) * step
    nblk = P // ROW_BLOCK
    used = (pad_end[-1] // ROW_BLOCK).astype(i32).reshape(1)
    blk_row = jnp.arange(nblk, dtype=i32)[:, None] * ROW_BLOCK
    blk_bucket = jnp.minimum(jnp.sum((pad_end[None, :] <= blk_row).astype(i32), axis=1), N_BUCKETS - 1)
    pick = (blk_bucket[:, None] == jnp.arange(LANES, dtype=i32)[None, :]).astype(i32)
    blk_a = jnp.sum(pick * jnp.asarray(_PAIR_A)[None, :], axis=1)
    blk_b = jnp.sum(pick * jnp.asarray(_PAIR_B)[None, :], axis=1)
    blk_g = blk_bucket // N_PAIRS
    by_group = lambda w: w.reshape((N_GROUPS, EPG) + w.shape[1:])

    dest = _dest(bk8, rank, pad_start.astype(f32).reshape(LANES, 1))[0]
    row_tok = _sc_invert(dest, P)
    cblk = nblk // MOE_CHUNKS
    ys = None
    for cidx in range(MOE_CHUNKS):
        lo = cidx * cblk
        xs = _sc_gather_rows(row_tok[lo * ROW_BLOCK:(lo + cblk) * ROW_BLOCK], h2x)
        used_c = jnp.clip(used - lo, 0, cblk)
        ys = _experts(blk_g[lo:lo + cblk], blk_a[lo:lo + cblk], blk_b[lo:lo + cblk], used_c, xs,
                      by_group(eg16), by_group(eu16), by_group(ed16), v2(ln_f_g), v2(ln_f_b), alpha,
                      ys, lo, P)
    return _sc_gather_rows(dest, ys).reshape(B, S, D)


def kernel(x, mem, positions, w_in, b_forget, sinks, w_mix_out, ln_mix_g, ln_mix_b, w_xq, w_xkv, w_xout,
           ln_x_g, ln_x_b, w_route_group, b_route_group, w_route_expert, b_route_expert,
           w_exp_gate, w_exp_up, w_exp_down, ln_ffn_g, ln_ffn_b):
    depth = w_in.shape[0]
    alpha = (2.0 * depth) ** 0.25
    h = x
    for l in range(depth):
        h = _layer(h, mem, positions, w_in[l], b_forget[l], sinks[l], w_mix_out[l], ln_mix_g[l], ln_mix_b[l],
                   w_xq[l], w_xkv[l], w_xout[l], ln_x_g[l], ln_x_b[l], w_route_group[l], b_route_group[l],
                   w_route_expert[l], b_route_expert[l], w_exp_gate[l], w_exp_up[l], w_exp_down[l],
                   ln_ffn_g[l], ln_ffn_b[l], alpha)
    return h
```

```python
import functools

import jax
import jax.numpy as jnp
import numpy as np
from jax import lax
from jax.experimental import pallas as pl
from jax.experimental.pallas import tpu as pltpu
from jax.experimental.pallas import tpu_sc as plsc

f32 = jnp.float32
bf16 = jnp.bfloat16
i32 = jnp.int32

D = 1024
HD = 64
N_FOX = 8
N_SWA = 8
N_SWA_KV = 2
FOX_W = 512
SWA_Q_W = 512
SWA_KV_W = 128
WINDOW = 128
ROPE_THETA = 10000.0
N_XH = 4
XHD = 256
N_GROUPS = 4
EPG = 8
N_EXPERTS = 32
D_EXPERT = 512
LN_EPS = 1e-5
NEG = -1e30
LOG2E = 1.4426950408889634
L_ROW = (HD, 0)

SC_CORES = 2
SC_SUBCORES = 16
SC_LANES = 16
SC_GATHER_ROWS = 32
MOE_CHUNKS = 4
LANES = 128
ROW_BLOCK = 128
N_PAIRS = EPG * (EPG - 1) // 2
N_BUCKETS = N_GROUPS * N_PAIRS
XW = D + LANES
VMEM_LIMIT = 56 * 1024 * 1024


def _cparams(sem):
    return pltpu.CompilerParams(dimension_semantics=sem, vmem_limit_bytes=VMEM_LIMIT)


def _layer_norm(v, g, b):
    mu = jnp.mean(v, axis=-1, keepdims=True)
    c = v - mu
    var = jnp.mean(c * c, axis=-1, keepdims=True)
    return c * lax.rsqrt(var + LN_EPS) * g + b


def _dot(a, b):
    return jnp.dot(a, b, preferred_element_type=f32)


def _dot_nt(a, b):
    return lax.dot_general(a, b, (((1,), (1,)), ((), ())), preferred_element_type=f32)


def _rope_table_kernel(pos_ref, invf_ref, cos_ref, sin_ref):
    ang = pos_ref[...].astype(f32) * invf_ref[...]
    cos_ref[...] = jnp.cos(ang)
    sin_ref[...] = jnp.sin(ang)


def _rope_table(pos4, invf, rows=1024):
    R = pos4.shape[0]
    blk = pl.BlockSpec((rows, LANES), lambda i: (i, 0))
    return pl.pallas_call(
        _rope_table_kernel, grid=(R // rows,),
        in_specs=[blk, pl.BlockSpec((1, LANES), lambda i: (0, 0))], out_specs=[blk, blk],
        out_shape=[jax.ShapeDtypeStruct((R, LANES), f32)] * 2,
        compiler_params=_cparams(("parallel",)), name="rope_table",
    )(pos4, invf)


def _inproj_kernel(x_ref, cos_ref, sin_ref, w_ref, wqt_ref, wvt_ref, wfl_ref, bf_ref, eg_ref, eu_ref, ed_ref,
                   qt_ref, kf_ref, vt_ref, qs_ref, ks_ref, vs_ref, lf_ref, egb_ref, eub_ref, edb_ref):
    tm = x_ref.shape[0]
    xb = x_ref[...].astype(bf16)
    egb_ref[...] = eg_ref[...].astype(bf16)
    eub_ref[...] = eu_ref[...].astype(bf16)
    edb_ref[...] = ed_ref[...].astype(bf16)

    def proj(lo, hi):
        return _dot(xb, w_ref[:, lo:hi])

    qt_ref[0] = (_dot_nt(wqt_ref[...], xb) * (0.125 * LOG2E)).astype(bf16)
    vt_ref[0] = _dot_nt(wvt_ref[...], xb).astype(bf16)
    kf_ref[...] = proj(0, 512).astype(bf16)

    reps = LANES // cos_ref.shape[1]
    cos = jnp.concatenate([cos_ref[...]] * reps, axis=1)
    sin = jnp.concatenate([sin_ref[...]] * reps, axis=1)
    lane = lax.broadcasted_iota(i32, (tm, LANES), 1)
    lo_half = (lane % HD) < (HD // 2)
    sin_s = jnp.where(lo_half, -sin, sin)

    def rope(z):
        rot = jnp.where(lo_half, pltpu.roll(z, LANES - HD // 2, 1), pltpu.roll(z, HD // 2, 1))
        return z * cos + rot * sin_s

    zq = proj(512, 1024)
    for g in range(4):
        sl = slice(g * LANES, (g + 1) * LANES)
        qs_ref[:, sl] = (rope(zq[:, sl]) * 0.125).astype(bf16)
    ks_ref[...] = rope(proj(1024, 1152)).astype(bf16)
    vs_ref[...] = proj(1152, 1280).astype(bf16)

    z = _dot_nt(wfl_ref[...], xb) + bf_ref[...]
    lf_ref[...] = jnp.minimum(z, 0.0) - jnp.log(1.0 + jnp.exp(-jnp.abs(z)))


def _in_proj(x2, cos, sin, w_all, wqt, wvt, wfl, bfc, w_eg, w_eu, w_ed, tm):
    T = x2.shape[0]
    steps = T // tm
    row = lambda w: pl.BlockSpec((tm, w), lambda i: (i, 0))
    full = lambda a: pl.BlockSpec(a.shape, lambda i: (0,) * a.ndim)
    fmaj = pl.BlockSpec((1, FOX_W, tm), lambda i: (i, 0, 0))
    flat = [w.reshape(-1, w.shape[-1]) for w in (w_eg, w_eu, w_ed)]
    slices = [pl.BlockSpec((w.shape[0] // steps, w.shape[1]), lambda i: (i, 0)) for w in flat]
    outs = pl.pallas_call(
        _inproj_kernel,
        grid=(steps,),
        in_specs=[row(D), row(cos.shape[1]), row(sin.shape[1]), full(w_all), full(wqt), full(wvt), full(wfl),
                  full(bfc)] + slices,
        out_specs=[fmaj, row(512), fmaj, row(512), row(128), row(128),
                   pl.BlockSpec((N_FOX, tm), lambda i: (0, i))] + slices,
        out_shape=[jax.ShapeDtypeStruct((steps, FOX_W, tm), bf16), jax.ShapeDtypeStruct((T, 512), bf16),
                   jax.ShapeDtypeStruct((steps, FOX_W, tm), bf16), jax.ShapeDtypeStruct((T, 512), bf16),
                   jax.ShapeDtypeStruct((T, 128), bf16), jax.ShapeDtypeStruct((T, 128), bf16),
                   jax.ShapeDtypeStruct((N_FOX, T), f32)]
        + [jax.ShapeDtypeStruct(w.shape, bf16) for w in flat],
        compiler_params=_cparams(("parallel",)),
        name="in_proj",
    )(x2, cos, sin, w_all, wqt, wvt, wfl, bfc, *flat)
    experts_bf16 = [o.reshape(w.shape) for o, w in zip(outs[7:], (w_eg, w_eu, w_ed))]
    return outs[:7], experts_bf16


def _cumsum_kernel(lf_ref, c_ref, ca_ref):
    S = lf_ref.shape[1]
    ch = 256
    r = lax.broadcasted_iota(i32, (ch, ch), 0)
    c = lax.broadcasted_iota(i32, (ch, ch), 1)
    tri = (r <= c).astype(f32)
    eye = (r == c).astype(bf16)
    stacked = jnp.concatenate([lf_ref[:, j * ch:(j + 1) * ch] for j in range(S // ch)], axis=0)
    local = jnp.dot(stacked, tri, precision=lax.Precision.HIGHEST, preferred_element_type=f32)
    carry = jnp.zeros((N_FOX, 1), f32)
    for j in range(S // ch):
        cc = local[j * N_FOX:(j + 1) * N_FOX] + carry
        carry = cc[:, ch - 1:ch]
        c2 = cc * LOG2E
        c_ref[:, j * ch:(j + 1) * ch] = c2
        neg = -c2
        hi = neg.astype(bf16)
        r1 = neg - hi.astype(f32)
        mid = r1.astype(bf16)
        lo = (r1 - mid.astype(f32)).astype(bf16)
        terms = jnp.concatenate([hi, mid, lo, jnp.zeros((LANES - 3 * N_FOX, ch), bf16)], axis=0)
        ca_ref[j * ch:(j + 1) * ch, :] = _dot_nt(eye, terms).astype(bf16)


def _cumsum(lf, S):
    T = lf.shape[1]
    spec = pl.BlockSpec((N_FOX, S), lambda b: (0, b))
    return pl.pallas_call(
        _cumsum_kernel, grid=(T // S,), in_specs=[spec],
        out_specs=[spec, pl.BlockSpec((S, LANES), lambda b: (b, 0))],
        out_shape=[jax.ShapeDtypeStruct((N_FOX, T), f32), jax.ShapeDtypeStruct((T, LANES), bf16)],
        compiler_params=_cparams(("parallel",)), name="cumsum",
    )(lf)


def _fox_kernel(qt_ref, k_ref, ca_ref, vt_ref, c_ref, o_ref, t0_ref, t1_ref, *, tq):
    hp = pl.program_id(1)
    i = pl.program_id(2)
    qt = qt_ref[0]
    row = lax.broadcasted_iota(i32, (LANES, tq), 0)
    is_a = row < HD
    zero = jnp.zeros_like(qt)
    q_ops = []
    for h in range(2):
        ones = jnp.where(((row & 7) == 2 * hp + h) & (row < 3 * N_FOX), 1.0, 0.0).astype(bf16)
        qh = jnp.where(is_a, qt, zero) if h == 0 else jnp.where(is_a, zero, qt)
        q_ops.append(jnp.concatenate([qh, ones], axis=0))
    kr = lax.broadcasted_iota(i32, (tq, tq), 0)
    qc = lax.broadcasted_iota(i32, (tq, tq), 1)
    causal = kr <= qc
    cq = [c_ref[0, h, pl.ds(i, 1), :] for h in range(2)]

    def scores(j, t_ref):
        off = pl.multiple_of(j * tq, tq)
        kblk = jnp.concatenate([k_ref[0, pl.ds(off, tq), :], ca_ref[0, pl.ds(off, tq), :]], axis=1)
        for h in range(2):
            t_ref[h] = _dot(kblk, q_ops[h])

    keep = [jnp.where(is_a, 1.0, 0.0).astype(bf16), jnp.where(is_a, 0.0, 1.0).astype(bf16)]
    ones_row = [jnp.where(row == L_ROW[h], 1.0, 0.0).astype(bf16) for h in range(2)]

    def softmax_pv(j, t_ref, carry, masked):
        vt = vt_ref[j]
        vts = [vt * keep[h] + ones_row[h] for h in range(2)]
        new = []
        for h in range(2):
            m, acc = carry[h]
            t = t_ref[h]
            if masked:
                t = jnp.where(causal, t, NEG)
            m_new = jnp.maximum(m, jnp.max(t, axis=0, keepdims=True) + cq[h])
            alpha = jnp.exp2(m - m_new)
            p = jnp.exp2(t + (cq[h] - m_new))
            acc = alpha * acc + _dot(vts[h], p.astype(bf16))
            new.append((m_new, acc))
        return tuple(new)

    def pair(k, carry):
        j = 2 * k
        scores(j + 1, t1_ref)
        carry = softmax_pv(j, t0_ref, carry, False)
        scores(j + 2, t0_ref)
        return softmax_pv(j + 1, t1_ref, carry, False)

    def odd_tail(carry):
        scores(i, t1_ref)
        carry = softmax_pv(i - 1, t0_ref, carry, False)
        return softmax_pv(i, t1_ref, carry, True)

    def even_tail(carry):
        return softmax_pv(i, t0_ref, carry, True)

    init = tuple((jnp.full((1, tq), NEG, f32), jnp.zeros((LANES, tq), f32)) for _ in range(2))
    scores(0, t0_ref)
    carry = lax.fori_loop(0, i // 2, pair, init)
    (_, acca), (_, accb) = lax.cond(i % 2 == 1, odd_tail, even_tail, carry)
    la = acca[L_ROW[0]:L_ROW[0] + 1, :]
    lb = accb[L_ROW[1]:L_ROW[1] + 1, :]
    ot = jnp.where(is_a, acca / la, accb / lb)
    o_ref[0] = jnp.transpose(ot).astype(bf16)


def _fox(qt, kf, ca, vt, c4, B, S, tq):
    nq = S // tq
    kernel = functools.partial(_fox_kernel, tq=tq)
    return pl.pallas_call(
        kernel,
        grid=(B, N_FOX // 2, nq),
        in_specs=[
            pl.BlockSpec((1, LANES, tq), lambda b, hp, i: (b * nq + i, hp, 0)),
            pl.BlockSpec((1, S, LANES), lambda b, hp, i: (b, 0, hp)),
            pl.BlockSpec((1, S, LANES), lambda b, hp, i: (b, 0, 0)),
            pl.BlockSpec((nq, LANES, tq), lambda b, hp, i: (b, hp, 0)),
            pl.BlockSpec((1, 2, nq, tq), lambda b, hp, i: (hp, 0, b, 0)),
        ],
        out_specs=pl.BlockSpec((1, tq, LANES), lambda b, hp, i: (b, i, hp)),
        out_shape=jax.ShapeDtypeStruct((B, S, FOX_W), bf16),
        scratch_shapes=[pltpu.VMEM((2, tq, tq), f32), pltpu.VMEM((2, tq, tq), f32)],
        compiler_params=_cparams(("parallel", "parallel", "arbitrary")),
        name="fox",
    )(qt, kf, ca, vt, c4)


def _swa_kernel(sink_ref, q_ref, k_ref, v_ref, o_ref, *, tq):
    W = WINDOW
    nsub = tq // W
    n0 = pl.program_id(1) * nsub
    lane = lax.broadcasted_iota(i32, (W, LANES), 1)
    is0 = lane < HD
    rows = lax.broadcasted_iota(i32, (4 * W, 2 * W), 0)
    cols = lax.broadcasted_iota(i32, (4 * W, 2 * W), 1)
    rgrp = lax.broadcasted_iota(i32, (4 * W, 1), 0) // W
    for r in range(nsub):
        nb = n0 + r
        kstart = pl.multiple_of(jnp.maximum(nb * W - W, 0), W)
        ks = k_ref[0, pl.ds(kstart, 2 * W), :]
        vs = v_ref[0, pl.ds(kstart, 2 * W), :]
        qpos = nb * W + rows % W
        kpos = kstart + cols
        valid = (kpos <= qpos) & (qpos - kpos < W)
        outs = []
        for kv in range(2):
            keep = is0 if kv == 0 else jnp.logical_not(is0)
            parts = []
            for g in range(4):
                qg = q_ref[0, r * W:(r + 1) * W, g * LANES:(g + 1) * LANES]
                parts.append(jnp.where(keep, qg, jnp.zeros_like(qg)))
            qstack = jnp.concatenate(parts, axis=0)
            s = jnp.where(valid, _dot_nt(qstack, ks), NEG)
            sink = jnp.zeros((4 * W, 1), f32)
            for g in range(4):
                sink = jnp.where(rgrp == g, sink_ref[kv * 4 + g], sink)
            m = jnp.maximum(jnp.max(s, axis=1, keepdims=True), sink)
            e = jnp.exp(s - m)
            den = jnp.sum(e, axis=1, keepdims=True) + jnp.exp(sink - m)
            outs.append(_dot(e.astype(bf16), vs) / den)
        for g in range(4):
            og = jnp.where(is0, outs[0][g * W:(g + 1) * W], outs[1][g * W:(g + 1) * W])
            o_ref[0, r * W:(r + 1) * W, g * LANES:(g + 1) * LANES] = og.astype(bf16)


def _swa(sinks, qs, ks, vs, B, S, tq=512):
    kernel = functools.partial(_swa_kernel, tq=tq)
    return pl.pallas_call(
        kernel,
        grid=(B, S // tq),
        in_specs=[
            pl.BlockSpec(memory_space=pltpu.SMEM),
            pl.BlockSpec((1, tq, SWA_Q_W), lambda b, i: (b, i, 0)),
            pl.BlockSpec((1, S, SWA_KV_W), lambda b, i: (b, 0, 0)),
            pl.BlockSpec((1, S, SWA_KV_W), lambda b, i: (b, 0, 0)),
        ],
        out_specs=pl.BlockSpec((1, tq, SWA_Q_W), lambda b, i: (b, i, 0)),
        out_shape=jax.ShapeDtypeStruct((B, S, SWA_Q_W), bf16),
        compiler_params=_cparams(("parallel", "arbitrary")),
        name="swa",
    )(sinks, qs, ks, vs)


def _kvproj_kernel(m_ref, w_ref, k_ref, v_ref):
    mb = m_ref[...].astype(bf16)
    k_ref[...] = _dot(mb, w_ref[:, :D]).astype(bf16)
    v_ref[...] = _dot(mb, w_ref[:, D:]).astype(bf16)


def _kvproj(mem2, w_xkv, tm=512):
    R = mem2.shape[0]
    row = pl.BlockSpec((tm, D), lambda i: (i, 0))
    return pl.pallas_call(
        _kvproj_kernel, grid=(R // tm,),
        in_specs=[row, pl.BlockSpec(w_xkv.shape, lambda i: (0, 0))],
        out_specs=[row, row],
        out_shape=[jax.ShapeDtypeStruct((R, D), bf16)] * 2,
        compiler_params=_cparams(("parallel",)), name="kvproj",
    )(mem2, w_xkv)


def _mid_kernel(x_ref, of_ref, os_ref, wo_ref, g1_ref, b1_ref, wq_ref, k_ref, v_ref,
                wxo_ref, g2_ref, b2_ref, wr_ref, br_ref,
                h2_ref, bk_ref, oc_ref, *, alpha):
    tm = x_ref.shape[0]
    mix = _dot(of_ref[...], wo_ref[:FOX_W, :]) + _dot(os_ref[...], wo_ref[FOX_W:, :])
    h1 = _layer_norm(alpha * x_ref[...] + mix, g1_ref[...], b1_ref[...])

    q = (_dot(h1.astype(bf16), wq_ref[...]) * 0.0625).astype(bf16)
    for h in range(N_XH):
        sl = slice(h * XHD, (h + 1) * XHD)
        s = _dot_nt(q[:, sl], k_ref[:, sl])
        e = jnp.exp(s - jnp.max(s, axis=1, keepdims=True))
        p = e / jnp.sum(e, axis=1, keepdims=True)
        oc_ref[:, sl] = _dot(p.astype(bf16), v_ref[:, sl]).astype(bf16)
    xo = _dot(oc_ref[...], wxo_ref[...])
    h2 = _layer_norm(alpha * h1 + xo, g2_ref[...], b2_ref[...])
    h2_ref[:, :D] = h2

    hh = h2.astype(bf16)
    hl = (h2 - hh.astype(f32)).astype(bf16)
    hi_terms = _dot(hh, wr_ref[...])
    lg = hi_terms[:, :LANES] + _dot(hl, wr_ref[:, :LANES]) + hi_terms[:, LANES:] + br_ref[...]

    lgt = jnp.transpose(lg)
    row = lax.broadcasted_iota(i32, (EPG, tm), 0).astype(f32)
    big = float(EPG)

    def first_max(vals, mask):
        vm = jnp.where(mask, vals, NEG)
        top = jnp.max(vm, axis=0, keepdims=True)
        idx = jnp.min(jnp.where(mask & (vm == top), row, big), axis=0, keepdims=True)
        return top, idx

    gl = lgt[0:EPG]
    gmask = row < float(N_GROUPS)
    gmax, gidx = first_max(gl, gmask)
    g_val = 1.0 / jnp.sum(jnp.where(gmask, jnp.exp(gl - gmax), 0.0), axis=0, keepdims=True)
    sel = jnp.zeros((EPG, tm), f32)
    for g in range(N_GROUPS):
        sel = jnp.where(gidx == float(g), lgt[EPG * (g + 1):EPG * (g + 2)], sel)
    every = row >= 0.0
    v1, e1 = first_max(sel, every)
    v2, e2 = first_max(sel, row != e1)
    ex = jnp.exp(v2 - v1)
    w1 = g_val * (1.0 / (1.0 + ex))
    w2 = g_val * (ex / (1.0 + ex))
    first_low = e1 < e2
    ea = jnp.where(first_low, e1, e2)
    eb = jnp.where(first_low, e2, e1)
    ga = jnp.where(first_low, w1, w2)
    gb = jnp.where(first_low, w2, w1)
    pidx = ea * float(EPG - 1) - ea * (ea - 1.0) * 0.5 + (eb - ea - 1.0)
    bucket = gidx * float(N_PAIRS) + pidx

    bk_ref[...] = jnp.broadcast_to(bucket, (EPG, tm))
    gates = jnp.where(row == 0.0, ga, jnp.where(row == 1.0, gb, 0.0))
    gates = jnp.concatenate([gates, jnp.zeros((LANES - EPG, tm), f32)], axis=0)
    h2_ref[:, D:] = jnp.transpose(gates)


def _mid(x2, of2, os2, w_out, g1, b1, wq, kx, vx, wxo, g2, b2, wr2, br, alpha, S, tm=1024):
    T = x2.shape[0]
    M = kx.shape[0] // (T // S)
    per_b = S // tm
    row = lambda w: pl.BlockSpec((tm, w), lambda i: (i, 0))
    full = lambda a: pl.BlockSpec(a.shape, lambda i: (0,) * a.ndim)
    kvspec = pl.BlockSpec((M, D), lambda i: (i // per_b, 0))
    kernel = functools.partial(_mid_kernel, alpha=alpha)
    return pl.pallas_call(
        kernel,
        grid=(T // tm,),
        in_specs=[row(D), row(512), row(512), full(w_out), full(g1), full(b1), full(wq),
                  kvspec, kvspec, full(wxo), full(g2), full(b2), full(wr2), full(br)],
        out_specs=[row(XW), pl.BlockSpec((8, tm), lambda i: (0, i))],
        out_shape=[jax.ShapeDtypeStruct((T, XW), f32), jax.ShapeDtypeStruct((8, T), f32)],
        scratch_shapes=[pltpu.VMEM((tm, D), bf16)],
        compiler_params=_cparams(("parallel",)),
        name="mid",
    )(x2, of2, os2, w_out, g1, b1, wq, kx, vx, wxo, g2, b2, wr2, br)


def _rank_kernel(bk_ref, rank_ref, cnt_ref, carry_ref, *, chunk):
    sub = 256

    @pl.when(pl.program_id(0) == 0)
    def _():
        carry_ref[...] = jnp.zeros_like(carry_ref)

    r = lax.broadcasted_iota(i32, (sub, sub), 0)
    c = lax.broadcasted_iota(i32, (sub, sub), 1)
    before = (r < c).astype(bf16)
    bid = lax.broadcasted_iota(i32, (LANES, sub), 0).astype(f32)
    carry = carry_ref[...]
    for j in range(chunk // sub):
        bk = bk_ref[0:1, j * sub:(j + 1) * sub]
        hit = bid == bk
        oh = jnp.where(hit, 1.0, 0.0)
        prior = _dot(oh.astype(bf16), before) + carry
        rank_ref[:, j * sub:(j + 1) * sub] = jnp.sum(jnp.where(hit, prior, 0.0), axis=0, keepdims=True)
        carry = carry + jnp.sum(oh, axis=1, keepdims=True)
    carry_ref[...] = carry
    cnt_ref[...] = carry


def _rank(bk8, chunk=2048):
    T = bk8.shape[1]
    kernel = functools.partial(_rank_kernel, chunk=chunk)
    return pl.pallas_call(
        kernel, grid=(T // chunk,),
        in_specs=[pl.BlockSpec((8, chunk), lambda i: (0, i))],
        out_specs=[pl.BlockSpec((1, chunk), lambda i: (0, i)),
                   pl.BlockSpec((LANES, 1), lambda i: (0, 0))],
        out_shape=[jax.ShapeDtypeStruct((1, T), f32), jax.ShapeDtypeStruct((LANES, 1), f32)],
        scratch_shapes=[pltpu.VMEM((LANES, 1), f32)],
        compiler_params=_cparams(("arbitrary",)), name="rank",
    )(bk8)


def _dest_kernel(bk_ref, rank_ref, ps_ref, dest_ref):
    chunk = bk_ref.shape[1]
    bid = lax.broadcasted_iota(i32, (LANES, chunk), 0).astype(f32)
    start = jnp.sum(jnp.where(bid == bk_ref[0:1, :], ps_ref[...], 0.0), axis=0, keepdims=True)
    dest_ref[...] = (start + rank_ref[...]).astype(i32)


def _dest(bk8, rank, ps_col, chunk=2048):
    T = bk8.shape[1]
    return pl.pallas_call(
        _dest_kernel, grid=(T // chunk,),
        in_specs=[pl.BlockSpec((8, chunk), lambda i: (0, i)), pl.BlockSpec((1, chunk), lambda i: (0, i)),
                  pl.BlockSpec((LANES, 1), lambda i: (0, 0))],
        out_specs=pl.BlockSpec((1, chunk), lambda i: (0, i)),
        out_shape=jax.ShapeDtypeStruct((1, T), i32),
        compiler_params=_cparams(("parallel",)), name="dest",
    )(bk8, rank, ps_col)


def _sc_invert(dest, n_rows):
    T = dest.shape[0]
    assert T & (T - 1) == 0
    lanes = SC_LANES
    mesh = plsc.VectorSubcoreMesh(core_axis_name="core", subcore_axis_name="subcore",
                                  num_cores=SC_CORES, num_subcores=SC_SUBCORES)

    @functools.partial(pl.kernel, out_type=jax.ShapeDtypeStruct((n_rows,), i32), mesh=mesh,
                       scratch_types=[pltpu.VMEM((T,), i32), pltpu.VMEM((n_rows,), i32)],
                       compiler_params=pltpu.CompilerParams(needs_layout_passes=False),
                       name="sc_invert")
    def k(dest_hbm, out_hbm, dest_v, table_v):
        wid = lax.axis_index("subcore") * SC_CORES + lax.axis_index("core")

        @pl.when(wid == 0)
        def _():
            pltpu.sync_copy(dest_hbm, dest_v)
            lane = lax.iota(i32, lanes)

            @pl.loop(0, n_rows // lanes)
            def _(j):
                off = pl.multiple_of(j * lanes, lanes)
                table_v[pl.ds(off, lanes)] = (lane + off) & (T - 1)


            @pl.loop(0, T // lanes)
            def _(j):
                off = pl.multiple_of(j * lanes, lanes)
                plsc.store_scatter(table_v, [dest_v[pl.ds(off, lanes)]], lane + off)

            pltpu.sync_copy(table_v, out_hbm)

    return k(dest)


def _sc_gather_rows(idx, src, chunk=SC_GATHER_ROWS):
    n = idx.shape[0]
    w = src.shape[1]
    workers = SC_CORES * SC_SUBCORES
    per_worker = n // workers
    mesh = plsc.VectorSubcoreMesh(core_axis_name="core", subcore_axis_name="subcore",
                                  num_cores=SC_CORES, num_subcores=SC_SUBCORES)

    n_chunks = per_worker // chunk
    assert n_chunks % 2 == 0

    @functools.partial(pl.kernel, out_type=jax.ShapeDtypeStruct((n, w), src.dtype), mesh=mesh,
                       scratch_types=[pltpu.VMEM((per_worker,), i32), pltpu.VMEM((2, chunk, w), src.dtype),
                                      pltpu.SemaphoreType.DMA((2,)), pltpu.SemaphoreType.DMA((2,))],
                       name="sc_gather_rows")
    def k(src_hbm, idx_hbm, out_hbm, idx_v, rows_v, gsem, wsem):
        wid = lax.axis_index("subcore") * SC_CORES + lax.axis_index("core")
        base = wid * per_worker
        pltpu.sync_copy(idx_hbm.at[pl.ds(base, per_worker)], idx_v)

        def gather(c, slot):
            rows = idx_v.at[pl.ds(pl.multiple_of(c * chunk, chunk), chunk)]
            return pltpu.make_async_copy(src_hbm.at[rows], rows_v.at[slot], gsem.at[slot])

        def write(c, slot):
            out = out_hbm.at[pl.ds(pl.multiple_of(base + c * chunk, chunk), chunk)]
            return pltpu.make_async_copy(rows_v.at[slot], out, wsem.at[slot])

        gather(0, 0).start()

        @pl.loop(0, n_chunks // 2)
        def _(pair):
            c = 2 * pair

            @pl.when(pair > 0)
            def _():
                write(c - 1, 1).wait()

            gather(c + 1, 1).start()
            gather(c, 0).wait()
            write(c, 0).start()

            @pl.when(c + 2 < n_chunks)
            def _():
                write(c, 0).wait()
                gather(c + 2, 0).start()

            gather(c + 1, 1).wait()
            write(c + 1, 1).start()

        write(n_chunks - 2, 0).wait()
        write(n_chunks - 1, 1).wait()

    return k(src, idx)


def _expert_kernel(grp_ref, ea_ref, eb_ref, used_ref, xs_ref, wg_ref, wu_ref, wd_ref, g_ref, b_ref, y_ref,
                   *, alpha):
    del grp_ref
    n = pl.program_id(0)

    @pl.when(n < used_ref[0])
    def _():
        h2 = xs_ref[:, :D]
        x = h2.astype(bf16)

        def expert(e):
            a = _dot(x, wg_ref[0, e])
            u = _dot(x, wu_ref[0, e])
            act = a * (1.0 / (1.0 + jnp.exp(-a))) * u
            return _dot(act.astype(bf16), wd_ref[0, e])

        ga = xs_ref[:, D:D + 1]
        gb = xs_ref[:, D + 1:D + 2]
        moe = ga * expert(ea_ref[n]) + gb * expert(eb_ref[n])
        y_ref[...] = _layer_norm(alpha * h2 + moe, g_ref[...], b_ref[...])

    @pl.when(n >= used_ref[0])
    def _():
        y_ref[...] = jnp.zeros_like(y_ref)


def _expert_kernel_into(grp_ref, ea_ref, eb_ref, used_ref, xs_ref, wg_ref, wu_ref, wd_ref, g_ref, b_ref,
                        ys_ref, y_ref, *, alpha):
    del ys_ref
    _expert_kernel(grp_ref, ea_ref, eb_ref, used_ref, xs_ref, wg_ref, wu_ref, wd_ref, g_ref, b_ref, y_ref,
                   alpha=alpha)


def _experts(grp, ea, eb, used, xs, wg, wu, wd, ln_g, ln_b, alpha, ys, first_block, total_rows):
    nblk = xs.shape[0] // ROW_BLOCK

    def xmap(n, grp, ea, eb, used):
        return (jnp.maximum(jnp.minimum(n, used[0] - 1), 0), 0)

    gmap = lambda n, grp, ea, eb, used: (grp[n], 0, 0, 0)
    gspec = lambda w: pl.BlockSpec((1,) + w.shape[1:], gmap, pipeline_mode=pl.Buffered(1))
    vec = pl.BlockSpec((1, D), lambda n, grp, ea, eb, used: (0, 0))
    in_specs = [pl.BlockSpec((ROW_BLOCK, XW), xmap), gspec(wg), gspec(wu), gspec(wd), vec, vec]
    operands = [grp, ea, eb, used, xs, wg, wu, wd, ln_g, ln_b]
    aliases = {}
    body = _expert_kernel
    if ys is not None:
        in_specs.append(pl.BlockSpec(memory_space=pl.ANY))
        aliases = {len(operands): 0}
        operands.append(ys)
        body = _expert_kernel_into
    grid_spec = pltpu.PrefetchScalarGridSpec(
        num_scalar_prefetch=4, grid=(nblk,), in_specs=in_specs,
        out_specs=pl.BlockSpec((ROW_BLOCK, D), lambda n, grp, ea, eb, used: (n + first_block, 0)),
    )
    return pl.pallas_call(
        functools.partial(body, alpha=alpha), grid_spec=grid_spec,
        out_shape=jax.ShapeDtypeStruct((total_rows, D), f32),
        input_output_aliases=aliases,
        compiler_params=_cparams(("arbitrary",)), name="experts",
    )(*operands)


def _pair_tables():
    ea = np.zeros((LANES,), np.int32)
    eb = np.zeros((LANES,), np.int32)
    for g in range(N_GROUPS):
        k = 0
        for a in range(EPG):
            for b in range(a + 1, EPG):
                ea[g * N_PAIRS + k] = a
                eb[g * N_PAIRS + k] = b
                k += 1
    return ea, eb


_PAIR_A, _PAIR_B = _pair_tables()


def _layer(h, mem, positions, w_in, b_forget, sinks, w_mix_out, ln_mix_g, ln_mix_b,
           w_xq, w_xkv, w_xout, ln_x_g, ln_x_b, w_rg, b_rg, w_re, b_re,
           w_eg, w_eu, w_ed, ln_f_g, ln_f_b, alpha):
    B, S, _ = h.shape
    T = B * S
    x2 = h.reshape(T, D)

    o = np.cumsum((0, FOX_W, FOX_W, FOX_W, N_FOX, SWA_Q_W, SWA_KV_W, SWA_KV_W))
    w_qf, w_kf, w_vf, w_fl, w_qs, w_ks, w_vs = (w_in[:, o[i]:o[i + 1]] for i in range(7))
    def regroup(a, axis):
        shp = a.shape
        a = jnp.moveaxis(a, axis, 0).reshape(N_SWA_KV, N_SWA // N_SWA_KV, HD, -1)
        return jnp.moveaxis(jnp.swapaxes(a, 0, 1).reshape(N_SWA * HD, -1), 0, axis).reshape(shp)

    w_all = jnp.concatenate([w_kf, regroup(w_qs, 1), w_ks, w_vs], axis=1).astype(bf16)
    wqt = w_qf.T.astype(bf16)
    wvt = w_vf.T.astype(bf16)
    wfl = w_fl.T.astype(bf16)
    bfc = b_forget.reshape(N_FOX, 1).astype(f32)
    half = HD // 2
    inv_freq = ROPE_THETA ** (-jnp.arange(half, dtype=f32) / half)
    per_row = LANES // half
    invf = jnp.tile(inv_freq, per_row).reshape(1, LANES)
    pos4 = jnp.repeat(positions.reshape(T // per_row, per_row).astype(i32), half, axis=1)
    cos, sin = (t.reshape(T, half) for t in _rope_table(pos4, invf))
    w_out = jnp.concatenate([w_mix_out[:FOX_W], regroup(w_mix_out[FOX_W:], 0)], axis=0).astype(bf16)

    tq = 512
    (qt, kf, vt, qs, ks, vs, lf), (eg16, eu16, ed16) = _in_proj(
        x2, cos, sin, w_all, wqt, wvt, wfl, bfc, w_eg, w_eu, w_ed, tq)
    c, ca = _cumsum(lf, S)
    c4 = c.reshape(N_FOX // 2, 2, T // tq, tq)
    r3 = lambda a: a.reshape(B, S, a.shape[-1])
    o_fox = _fox(qt, r3(kf), r3(ca), vt, c4, B, S, tq)
    o_swa = _swa(sinks.astype(f32), r3(qs), r3(ks), r3(vs), B, S)

    kx, vx = _kvproj(mem.reshape(-1, D), w_xkv.astype(bf16))

    gpad = EPG - N_GROUPS
    wr = jnp.concatenate([jnp.pad(w_rg, ((0, 0), (0, gpad))),
                          jnp.transpose(w_re, (1, 0, 2)).reshape(D, N_EXPERTS)], axis=1)
    wr = jnp.pad(wr, ((0, 0), (0, LANES - wr.shape[1]))).astype(f32)
    wrh = wr.astype(bf16)
    wr2 = jnp.concatenate([wrh, (wr - wrh.astype(f32)).astype(bf16)], axis=1)
    br = jnp.pad(jnp.concatenate([jnp.pad(b_rg, (0, gpad)), b_re.reshape(-1)]), (0, LANES - EPG - N_EXPERTS))
    br = br.reshape(1, LANES).astype(f32)
    v2 = lambda a: a.reshape(1, D).astype(f32)
    h2x, bk8 = _mid(x2, o_fox.reshape(T, FOX_W), o_swa.reshape(T, SWA_Q_W), w_out,
                    v2(ln_mix_g), v2(ln_mix_b), w_xq.astype(bf16), kx, vx, w_xout.astype(bf16),
                    v2(ln_x_g), v2(ln_x_b), wr2, br, alpha, S)

    rank, cnt = _rank(bk8)
    counts = cnt[:, 0].astype(i32)
    padded = ((counts + ROW_BLOCK - 1) // ROW_BLOCK) * ROW_BLOCK
    pad_end = jnp.cumsum(padded)
    pad_start = (pad_end - padded).astype(i32)
    step = MOE_CHUNKS * SC_CORES * SC_SUBCORES * 2 * SC_GATHER_ROWS
    P = -(-(T + N_BUCKETS * ROW_BLOCK) // step) * step
    nblk = P // ROW_BLOCK
    used = (pad_end[-1] // ROW_BLOCK).astype(i32).reshape(1)
    blk_row = jnp.arange(nblk, dtype=i32)[:, None] * ROW_BLOCK
    blk_bucket = jnp.minimum(jnp.sum((pad_end[None, :] <= blk_row).astype(i32), axis=1), N_BUCKETS - 1)
    pick = (blk_bucket[:, None] == jnp.arange(LANES, dtype=i32)[None, :]).astype(i32)
    blk_a = jnp.sum(pick * jnp.asarray(_PAIR_A)[None, :], axis=1)
    blk_b = jnp.sum(pick * jnp.asarray(_PAIR_B)[None, :], axis=1)
    blk_g = blk_bucket // N_PAIRS
    by_group = lambda w: w.reshape((N_GROUPS, EPG) + w.shape[1:])

    dest = _dest(bk8, rank, pad_start.astype(f32).reshape(LANES, 1))[0]
    row_tok = _sc_invert(dest, P)
    cblk = nblk // MOE_CHUNKS
    ys = None
    for cidx in range(MOE_CHUNKS):
        lo = cidx * cblk
        xs = _sc_gather_rows(row_tok[lo * ROW_BLOCK:(lo + cblk) * ROW_BLOCK], h2x)
        used_c = jnp.clip(used - lo, 0, cblk)
        ys = _experts(blk_g[lo:lo + cblk], blk_a[lo:lo + cblk], blk_b[lo:lo + cblk], used_c, xs,
                      by_group(eg16), by_group(eu16), by_group(ed16), v2(ln_f_g), v2(ln_f_b), alpha,
                      ys, lo, P)
    return _sc_gather_rows(dest, ys).reshape(B, S, D)


def kernel(x, mem, positions, w_in, b_forget, sinks, w_mix_out, ln_mix_g, ln_mix_b, w_xq, w_xkv, w_xout,
           ln_x_g, ln_x_b, w_route_group, b_route_group, w_route_expert, b_route_expert,
           w_exp_gate, w_exp_up, w_exp_down, ln_ffn_g, ln_ffn_b):
    depth = w_in.shape[0]
    alpha = (2.0 * depth) ** 0.25
    h = x
    for l in range(depth):
        h = _layer(h, mem, positions, w_in[l], b_forget[l], sinks[l], w_mix_out[l], ln_mix_g[l], ln_mix_b[l],
                   w_xq[l], w_xkv[l], w_xout[l], ln_x_g[l], ln_x_b[l], w_route_group[l], b_route_group[l],
                   w_route_expert[l], b_route_expert[l], w_exp_gate[l], w_exp_up[l], w_exp_down[l],
                   ln_ffn_g[l], ln_ffn_b[l], alpha)
    return h
```

```python
import functools

import jax
import jax.numpy as jnp
import numpy as np
from jax import lax
from jax.experimental import pallas as pl
from jax.experimental.pallas import tpu as pltpu
from jax.experimental.pallas import tpu_sc as plsc

f32 = jnp.float32
bf16 = jnp.bfloat16
i32 = jnp.int32

D = 1024
HD = 64
N_FOX = 8
N_SWA = 8
N_SWA_KV = 2
FOX_W = 512
SWA_Q_W = 512
SWA_KV_W = 128
WINDOW = 128
ROPE_THETA = 10000.0
N_XH = 4
XHD = 256
N_GROUPS = 4
EPG = 8
N_EXPERTS = 32
D_EXPERT = 512
LN_EPS = 1e-5
NEG = -1e30
LOG2E = 1.4426950408889634
L_ROW = (HD, 0)

SC_CORES = 2
SC_SUBCORES = 16
SC_LANES = 16
SC_GATHER_ROWS = 32
MOE_CHUNKS = 4
LANES = 128
ROW_BLOCK = 128
N_PAIRS = EPG * (EPG - 1) // 2
N_BUCKETS = N_GROUPS * N_PAIRS
XW = D + LANES
VMEM_LIMIT = 56 * 1024 * 1024


def _cparams(sem):
    return pltpu.CompilerParams(dimension_semantics=sem, vmem_limit_bytes=VMEM_LIMIT)


def _layer_norm(v, g, b):
    mu = jnp.mean(v, axis=-1, keepdims=True)
    c = v - mu
    var = jnp.mean(c * c, axis=-1, keepdims=True)
    return c * lax.rsqrt(var + LN_EPS) * g + b


def _dot(a, b):
    return jnp.dot(a, b, preferred_element_type=f32)


def _dot_nt(a, b):
    return lax.dot_general(a, b, (((1,), (1,)), ((), ())), preferred_element_type=f32)


def _rope_table_kernel(pos_ref, invf_ref, cos_ref, sin_ref):
    ang = pos_ref[...].astype(f32) * invf_ref[...]
    cos_ref[...] = jnp.cos(ang)
    sin_ref[...] = jnp.sin(ang)


def _rope_table(pos4, invf, rows=1024):
    R = pos4.shape[0]
    blk = pl.BlockSpec((rows, LANES), lambda i: (i, 0))
    return pl.pallas_call(
        _rope_table_kernel, grid=(R // rows,),
        in_specs=[blk, pl.BlockSpec((1, LANES), lambda i: (0, 0))], out_specs=[blk, blk],
        out_shape=[jax.ShapeDtypeStruct((R, LANES), f32)] * 2,
        compiler_params=_cparams(("parallel",)), name="rope_table",
    )(pos4, invf)


def _inproj_kernel(x_ref, cos_ref, sin_ref, w_ref, wqt_ref, wvt_ref, wfl_ref, bf_ref, eg_ref, eu_ref, ed_ref,
                   qt_ref, kf_ref, vt_ref, qs_ref, ks_ref, vs_ref, lf_ref, egb_ref, eub_ref, edb_ref):
    tm = x_ref.shape[0]
    xb = x_ref[...].astype(bf16)
    egb_ref[...] = eg_ref[...].astype(bf16)
    eub_ref[...] = eu_ref[...].astype(bf16)
    edb_ref[...] = ed_ref[...].astype(bf16)

    def proj(lo, hi):
        return _dot(xb, w_ref[:, lo:hi])

    qt_ref[0] = (_dot_nt(wqt_ref[...], xb) * (0.125 * LOG2E)).astype(bf16)
    vt_ref[0] = _dot_nt(wvt_ref[...], xb).astype(bf16)
    kf_ref[...] = proj(0, 512).astype(bf16)

    reps = LANES // cos_ref.shape[1]
    cos = jnp.concatenate([cos_ref[...]] * reps, axis=1)
    sin = jnp.concatenate([sin_ref[...]] * reps, axis=1)
    lane = lax.broadcasted_iota(i32, (tm, LANES), 1)
    lo_half = (lane % HD) < (HD // 2)
    sin_s = jnp.where(lo_half, -sin, sin)

    def rope(z):
        rot = jnp.where(lo_half, pltpu.roll(z, LANES - HD // 2, 1), pltpu.roll(z, HD // 2, 1))
        return z * cos + rot * sin_s

    zq = proj(512, 1024)
    for g in range(4):
        sl = slice(g * LANES, (g + 1) * LANES)
        qs_ref[:, sl] = (rope(zq[:, sl]) * (0.125 * LOG2E)).astype(bf16)
    ks_ref[...] = rope(proj(1024, 1152)).astype(bf16)
    vs_ref[...] = proj(1152, 1280).astype(bf16)

    z = _dot_nt(wfl_ref[...], xb) + bf_ref[...]
    lf_ref[...] = jnp.minimum(z, 0.0) - jnp.log(1.0 + jnp.exp(-jnp.abs(z)))


def _in_proj(x2, cos, sin, w_all, wqt, wvt, wfl, bfc, w_eg, w_eu, w_ed, tm):
    T = x2.shape[0]
    steps = T // tm
    row = lambda w: pl.BlockSpec((tm, w), lambda i: (i, 0))
    full = lambda a: pl.BlockSpec(a.shape, lambda i: (0,) * a.ndim)
    fmaj = pl.BlockSpec((1, FOX_W, tm), lambda i: (i, 0, 0))
    flat = [w.reshape(-1, w.shape[-1]) for w in (w_eg, w_eu, w_ed)]
    slices = [pl.BlockSpec((w.shape[0] // steps, w.shape[1]), lambda i: (i, 0)) for w in flat]
    outs = pl.pallas_call(
        _inproj_kernel,
        grid=(steps,),
        in_specs=[row(D), row(cos.shape[1]), row(sin.shape[1]), full(w_all), full(wqt), full(wvt), full(wfl),
                  full(bfc)] + slices,
        out_specs=[fmaj, row(512), fmaj, row(512), row(128), row(128),
                   pl.BlockSpec((N_FOX, tm), lambda i: (0, i))] + slices,
        out_shape=[jax.ShapeDtypeStruct((steps, FOX_W, tm), bf16), jax.ShapeDtypeStruct((T, 512), bf16),
                   jax.ShapeDtypeStruct((steps, FOX_W, tm), bf16), jax.ShapeDtypeStruct((T, 512), bf16),
                   jax.ShapeDtypeStruct((T, 128), bf16), jax.ShapeDtypeStruct((T, 128), bf16),
                   jax.ShapeDtypeStruct((N_FOX, T), f32)]
        + [jax.ShapeDtypeStruct(w.shape, bf16) for w in flat],
        compiler_params=_cparams(("parallel",)),
        name="in_proj",
    )(x2, cos, sin, w_all, wqt, wvt, wfl, bfc, *flat)
    experts_bf16 = [o.reshape(w.shape) for o, w in zip(outs[7:], (w_eg, w_eu, w_ed))]
    return outs[:7], experts_bf16


def _cumsum_kernel(lf_ref, c_ref, ca_ref):
    S = lf_ref.shape[1]
    ch = 256
    r = lax.broadcasted_iota(i32, (ch, ch), 0)
    c = lax.broadcasted_iota(i32, (ch, ch), 1)
    tri = (r <= c).astype(f32)
    eye = (r == c).astype(bf16)
    stacked = jnp.concatenate([lf_ref[:, j * ch:(j + 1) * ch] for j in range(S // ch)], axis=0)
    local = jnp.dot(stacked, tri, precision=lax.Precision.HIGHEST, preferred_element_type=f32)
    carry = jnp.zeros((N_FOX, 1), f32)
    for j in range(S // ch):
        cc = local[j * N_FOX:(j + 1) * N_FOX] + carry
        carry = cc[:, ch - 1:ch]
        c2 = cc * LOG2E
        c_ref[:, j * ch:(j + 1) * ch] = c2
        neg = -c2
        hi = neg.astype(bf16)
        r1 = neg - hi.astype(f32)
        mid = r1.astype(bf16)
        lo = (r1 - mid.astype(f32)).astype(bf16)
        terms = jnp.concatenate([hi, mid, lo, jnp.zeros((LANES - 3 * N_FOX, ch), bf16)], axis=0)
        ca_ref[j * ch:(j + 1) * ch, :] = _dot_nt(eye, terms).astype(bf16)


def _cumsum(lf, S):
    T = lf.shape[1]
    spec = pl.BlockSpec((N_FOX, S), lambda b: (0, b))
    return pl.pallas_call(
        _cumsum_kernel, grid=(T // S,), in_specs=[spec],
        out_specs=[spec, pl.BlockSpec((S, LANES), lambda b: (b, 0))],
        out_shape=[jax.ShapeDtypeStruct((N_FOX, T), f32), jax.ShapeDtypeStruct((T, LANES), bf16)],
        compiler_params=_cparams(("parallel",)), name="cumsum",
    )(lf)


def _fox_kernel(qt_ref, k_ref, ca_ref, vt_ref, c_ref, o_ref, t0_ref, t1_ref, *, tq):
    hp = pl.program_id(1)
    i = pl.program_id(2)
    qt = qt_ref[0]
    row = lax.broadcasted_iota(i32, (LANES, tq), 0)
    is_a = row < HD
    zero = jnp.zeros_like(qt)
    q_ops = []
    for h in range(2):
        ones = jnp.where(((row & 7) == 2 * hp + h) & (row < 3 * N_FOX), 1.0, 0.0).astype(bf16)
        qh = jnp.where(is_a, qt, zero) if h == 0 else jnp.where(is_a, zero, qt)
        q_ops.append(jnp.concatenate([qh, ones], axis=0))
    kr = lax.broadcasted_iota(i32, (tq, tq), 0)
    qc = lax.broadcasted_iota(i32, (tq, tq), 1)
    causal = kr <= qc
    cq = [c_ref[0, h, pl.ds(i, 1), :] for h in range(2)]

    def scores(j, t_ref):
        off = pl.multiple_of(j * tq, tq)
        kblk = jnp.concatenate([k_ref[0, pl.ds(off, tq), :], ca_ref[0, pl.ds(off, tq), :]], axis=1)
        for h in range(2):
            t_ref[h] = _dot(kblk, q_ops[h])

    keep = [jnp.where(is_a, 1.0, 0.0).astype(bf16), jnp.where(is_a, 0.0, 1.0).astype(bf16)]
    ones_row = [jnp.where(row == L_ROW[h], 1.0, 0.0).astype(bf16) for h in range(2)]

    def softmax_pv(j, t_ref, carry, masked):
        vt = vt_ref[j]
        vts = [vt * keep[h] + ones_row[h] for h in range(2)]
        new = []
        for h in range(2):
            m, acc = carry[h]
            t = t_ref[h]
            if masked:
                t = jnp.where(causal, t, NEG)
            m_new = jnp.maximum(m, jnp.max(t, axis=0, keepdims=True) + cq[h])
            alpha = jnp.exp2(m - m_new)
            p = jnp.exp2(t + (cq[h] - m_new))
            acc = alpha * acc + _dot(vts[h], p.astype(bf16))
            new.append((m_new, acc))
        return tuple(new)

    def pair(k, carry):
        j = 2 * k
        scores(j + 1, t1_ref)
        carry = softmax_pv(j, t0_ref, carry, False)
        scores(j + 2, t0_ref)
        return softmax_pv(j + 1, t1_ref, carry, False)

    def odd_tail(carry):
        scores(i, t1_ref)
        carry = softmax_pv(i - 1, t0_ref, carry, False)
        return softmax_pv(i, t1_ref, carry, True)

    def even_tail(carry):
        return softmax_pv(i, t0_ref, carry, True)

    init = tuple((jnp.full((1, tq), NEG, f32), jnp.zeros((LANES, tq), f32)) for _ in range(2))
    scores(0, t0_ref)
    carry = lax.fori_loop(0, i // 2, pair, init)
    (_, acca), (_, accb) = lax.cond(i % 2 == 1, odd_tail, even_tail, carry)
    la = acca[L_ROW[0]:L_ROW[0] + 1, :]
    lb = accb[L_ROW[1]:L_ROW[1] + 1, :]
    ot = jnp.where(is_a, acca / la, accb / lb)
    o_ref[0] = jnp.transpose(ot).astype(bf16)


def _fox(qt, kf, ca, vt, c4, B, S, tq):
    nq = S // tq
    kernel = functools.partial(_fox_kernel, tq=tq)
    return pl.pallas_call(
        kernel,
        grid=(B, N_FOX // 2, nq),
        in_specs=[
            pl.BlockSpec((1, LANES, tq), lambda b, hp, i: (b * nq + i, hp, 0)),
            pl.BlockSpec((1, S, LANES), lambda b, hp, i: (b, 0, hp)),
            pl.BlockSpec((1, S, LANES), lambda b, hp, i: (b, 0, 0)),
            pl.BlockSpec((nq, LANES, tq), lambda b, hp, i: (b, hp, 0)),
            pl.BlockSpec((1, 2, nq, tq), lambda b, hp, i: (hp, 0, b, 0)),
        ],
        out_specs=pl.BlockSpec((1, tq, LANES), lambda b, hp, i: (b, i, hp)),
        out_shape=jax.ShapeDtypeStruct((B, S, FOX_W), bf16),
        scratch_shapes=[pltpu.VMEM((2, tq, tq), f32), pltpu.VMEM((2, tq, tq), f32)],
        compiler_params=_cparams(("parallel", "parallel", "arbitrary")),
        name="fox",
    )(qt, kf, ca, vt, c4)


def _swa_kernel(sink_ref, q_ref, k_ref, v_ref, o_ref, *, tq):
    W = WINDOW
    nsub = tq // W
    n0 = pl.program_id(1) * nsub
    lane = lax.broadcasted_iota(i32, (W, LANES), 1)
    is0 = lane < HD
    qoff = lax.broadcasted_iota(i32, (4 * W, 2 * W), 0) % W
    cols = lax.broadcasted_iota(i32, (4 * W, 2 * W), 1)
    bias_mid = jnp.where((cols - W <= qoff) & (qoff - (cols - W) < W), 0.0, NEG)
    bias_first = jnp.where(cols <= qoff, 0.0, NEG)
    rgrp = lax.broadcasted_iota(i32, (4 * W, 1), 0) // W
    for r in range(nsub):
        nb = n0 + r
        kstart = pl.multiple_of(jnp.maximum(nb * W - W, 0), W)
        ks = k_ref[0, pl.ds(kstart, 2 * W), :]
        vs = v_ref[0, pl.ds(kstart, 2 * W), :]
        bias = jnp.where(nb == 0, bias_first, bias_mid)
        outs = []
        for kv in range(2):
            keep = is0 if kv == 0 else jnp.logical_not(is0)
            parts = []
            for g in range(4):
                qg = q_ref[0, r * W:(r + 1) * W, g * LANES:(g + 1) * LANES]
                parts.append(jnp.where(keep, qg, jnp.zeros_like(qg)))
            qstack = jnp.concatenate(parts, axis=0)
            s = bias + _dot_nt(qstack, ks)
            sink = jnp.zeros((4 * W, 1), f32)
            for g in range(4):
                sink = jnp.where(rgrp == g, sink_ref[kv * 4 + g], sink)
            m = jnp.maximum(jnp.max(s, axis=1, keepdims=True), sink)
            e = jnp.exp2(s - m)
            den = jnp.sum(e, axis=1, keepdims=True) + jnp.exp2(sink - m)
            outs.append(_dot(e.astype(bf16), vs) / den)
        for g in range(4):
            og = jnp.where(is0, outs[0][g * W:(g + 1) * W], outs[1][g * W:(g + 1) * W])
            o_ref[0, r * W:(r + 1) * W, g * LANES:(g + 1) * LANES] = og.astype(bf16)


def _swa(sinks, qs, ks, vs, B, S, tq=512):
    kernel = functools.partial(_swa_kernel, tq=tq)
    return pl.pallas_call(
        kernel,
        grid=(B, S // tq),
        in_specs=[
            pl.BlockSpec(memory_space=pltpu.SMEM),
            pl.BlockSpec((1, tq, SWA_Q_W), lambda b, i: (b, i, 0)),
            pl.BlockSpec((1, S, SWA_KV_W), lambda b, i: (b, 0, 0)),
            pl.BlockSpec((1, S, SWA_KV_W), lambda b, i: (b, 0, 0)),
        ],
        out_specs=pl.BlockSpec((1, tq, SWA_Q_W), lambda b, i: (b, i, 0)),
        out_shape=jax.ShapeDtypeStruct((B, S, SWA_Q_W), bf16),
        compiler_params=_cparams(("parallel", "arbitrary")),
        name="swa",
    )(sinks, qs, ks, vs)


def _kvproj_kernel(m_ref, w_ref, k_ref, v_ref):
    mb = m_ref[...].astype(bf16)
    k_ref[...] = _dot(mb, w_ref[:, :D]).astype(bf16)
    v_ref[...] = _dot(mb, w_ref[:, D:]).astype(bf16)


def _kvproj(mem2, w_xkv, tm=512):
    R = mem2.shape[0]
    row = pl.BlockSpec((tm, D), lambda i: (i, 0))
    return pl.pallas_call(
        _kvproj_kernel, grid=(R // tm,),
        in_specs=[row, pl.BlockSpec(w_xkv.shape, lambda i: (0, 0))],
        out_specs=[row, row],
        out_shape=[jax.ShapeDtypeStruct((R, D), bf16)] * 2,
        compiler_params=_cparams(("parallel",)), name="kvproj",
    )(mem2, w_xkv)


def _mid_kernel(x_ref, of_ref, os_ref, wo_ref, g1_ref, b1_ref, wq_ref, k_ref, v_ref,
                wxo_ref, g2_ref, b2_ref, wr_ref, br_ref,
                h2_ref, bk_ref, oc_ref, *, alpha):
    tm = x_ref.shape[0]
    mix = _dot(of_ref[...], wo_ref[:FOX_W, :]) + _dot(os_ref[...], wo_ref[FOX_W:, :])
    h1 = _layer_norm(alpha * x_ref[...] + mix, g1_ref[...], b1_ref[...])

    q = (_dot(h1.astype(bf16), wq_ref[...]) * 0.0625).astype(bf16)
    for h in range(N_XH):
        sl = slice(h * XHD, (h + 1) * XHD)
        s = _dot_nt(q[:, sl], k_ref[:, sl])
        e = jnp.exp(s - jnp.max(s, axis=1, keepdims=True))
        p = e / jnp.sum(e, axis=1, keepdims=True)
        oc_ref[:, sl] = _dot(p.astype(bf16), v_ref[:, sl]).astype(bf16)
    xo = _dot(oc_ref[...], wxo_ref[...])
    h2 = _layer_norm(alpha * h1 + xo, g2_ref[...], b2_ref[...])
    h2_ref[:, :D] = h2

    hh = h2.astype(bf16)
    hl = (h2 - hh.astype(f32)).astype(bf16)
    hi_terms = _dot(hh, wr_ref[...])
    lg = hi_terms[:, :LANES] + _dot(hl, wr_ref[:, :LANES]) + hi_terms[:, LANES:] + br_ref[...]

    lgt = jnp.transpose(lg)
    row = lax.broadcasted_iota(i32, (EPG, tm), 0).astype(f32)
    big = float(EPG)

    def first_max(vals, mask):
        vm = jnp.where(mask, vals, NEG)
        top = jnp.max(vm, axis=0, keepdims=True)
        idx = jnp.min(jnp.where(mask & (vm == top), row, big), axis=0, keepdims=True)
        return top, idx

    gl = lgt[0:EPG]
    gmask = row < float(N_GROUPS)
    gmax, gidx = first_max(gl, gmask)
    g_val = 1.0 / jnp.sum(jnp.where(gmask, jnp.exp(gl - gmax), 0.0), axis=0, keepdims=True)
    sel = jnp.zeros((EPG, tm), f32)
    for g in range(N_GROUPS):
        sel = jnp.where(gidx == float(g), lgt[EPG * (g + 1):EPG * (g + 2)], sel)
    every = row >= 0.0
    v1, e1 = first_max(sel, every)
    v2, e2 = first_max(sel, row != e1)
    ex = jnp.exp(v2 - v1)
    w1 = g_val * (1.0 / (1.0 + ex))
    w2 = g_val * (ex / (1.0 + ex))
    first_low = e1 < e2
    ea = jnp.where(first_low, e1, e2)
    eb = jnp.where(first_low, e2, e1)
    ga = jnp.where(first_low, w1, w2)
    gb = jnp.where(first_low, w2, w1)
    pidx = ea * float(EPG - 1) - ea * (ea - 1.0) * 0.5 + (eb - ea - 1.0)
    bucket = gidx * float(N_PAIRS) + pidx

    bk_ref[...] = jnp.broadcast_to(bucket, (EPG, tm))
    gates = jnp.where(row == 0.0, ga, jnp.where(row == 1.0, gb, 0.0))
    gates = jnp.concatenate([gates, jnp.zeros((LANES - EPG, tm), f32)], axis=0)
    h2_ref[:, D:] = jnp.transpose(gates)


def _mid(x2, of2, os2, w_out, g1, b1, wq, kx, vx, wxo, g2, b2, wr2, br, alpha, S, tm=1024):
    T = x2.shape[0]
    M = kx.shape[0] // (T // S)
    per_b = S // tm
    row = lambda w: pl.BlockSpec((tm, w), lambda i: (i, 0))
    full = lambda a: pl.BlockSpec(a.shape, lambda i: (0,) * a.ndim)
    kvspec = pl.BlockSpec((M, D), lambda i: (i // per_b, 0))
    kernel = functools.partial(_mid_kernel, alpha=alpha)
    return pl.pallas_call(
        kernel,
        grid=(T // tm,),
        in_specs=[row(D), row(512), row(512), full(w_out), full(g1), full(b1), full(wq),
                  kvspec, kvspec, full(wxo), full(g2), full(b2), full(wr2), full(br)],
        out_specs=[row(XW), pl.BlockSpec((8, tm), lambda i: (0, i))],
        out_shape=[jax.ShapeDtypeStruct((T, XW), f32), jax.ShapeDtypeStruct((8, T), f32)],
        scratch_shapes=[pltpu.VMEM((tm, D), bf16)],
        compiler_params=_cparams(("parallel",)),
        name="mid",
    )(x2, of2, os2, w_out, g1, b1, wq, kx, vx, wxo, g2, b2, wr2, br)


def _rank_kernel(bk_ref, rank_ref, cnt_ref, carry_ref, *, chunk):
    sub = 256

    @pl.when(pl.program_id(0) == 0)
    def _():
        carry_ref[...] = jnp.zeros_like(carry_ref)

    r = lax.broadcasted_iota(i32, (sub, sub), 0)
    c = lax.broadcasted_iota(i32, (sub, sub), 1)
    before = (r < c).astype(bf16)
    bid = lax.broadcasted_iota(i32, (LANES, sub), 0).astype(f32)
    carry = carry_ref[...]
    for j in range(chunk // sub):
        bk = bk_ref[0:1, j * sub:(j + 1) * sub]
        hit = bid == bk
        oh = jnp.where(hit, 1.0, 0.0)
        prior = _dot(oh.astype(bf16), before) + carry
        rank_ref[:, j * sub:(j + 1) * sub] = jnp.sum(jnp.where(hit, prior, 0.0), axis=0, keepdims=True)
        carry = carry + jnp.sum(oh, axis=1, keepdims=True)
    carry_ref[...] = carry
    cnt_ref[...] = carry


def _rank(bk8, chunk=2048):
    T = bk8.shape[1]
    kernel = functools.partial(_rank_kernel, chunk=chunk)
    return pl.pallas_call(
        kernel, grid=(T // chunk,),
        in_specs=[pl.BlockSpec((8, chunk), lambda i: (0, i))],
        out_specs=[pl.BlockSpec((1, chunk), lambda i: (0, i)),
                   pl.BlockSpec((LANES, 1), lambda i: (0, 0))],
        out_shape=[jax.ShapeDtypeStruct((1, T), f32), jax.ShapeDtypeStruct((LANES, 1), f32)],
        scratch_shapes=[pltpu.VMEM((LANES, 1), f32)],
        compiler_params=_cparams(("arbitrary",)), name="rank",
    )(bk8)


def _dest_kernel(bk_ref, rank_ref, ps_ref, dest_ref):
    chunk = bk_ref.shape[1]
    bid = lax.broadcasted_iota(i32, (LANES, chunk), 0).astype(f32)
    start = jnp.sum(jnp.where(bid == bk_ref[0:1, :], ps_ref[...], 0.0), axis=0, keepdims=True)
    dest_ref[...] = (start + rank_ref[...]).astype(i32)


def _dest(bk8, rank, ps_col, chunk=2048):
    T = bk8.shape[1]
    return pl.pallas_call(
        _dest_kernel, grid=(T // chunk,),
        in_specs=[pl.BlockSpec((8, chunk), lambda i: (0, i)), pl.BlockSpec((1, chunk), lambda i: (0, i)),
                  pl.BlockSpec((LANES, 1), lambda i: (0, 0))],
        out_specs=pl.BlockSpec((1, chunk), lambda i: (0, i)),
        out_shape=jax.ShapeDtypeStruct((1, T), i32),
        compiler_params=_cparams(("parallel",)), name="dest",
    )(bk8, rank, ps_col)


def _sc_invert(dest, n_rows):
    T = dest.shape[0]
    assert T & (T - 1) == 0
    lanes = SC_LANES
    mesh = plsc.VectorSubcoreMesh(core_axis_name="core", subcore_axis_name="subcore",
                                  num_cores=SC_CORES, num_subcores=SC_SUBCORES)

    @functools.partial(pl.kernel, out_type=jax.ShapeDtypeStruct((n_rows,), i32), mesh=mesh,
                       scratch_types=[pltpu.VMEM((T,), i32), pltpu.VMEM((n_rows,), i32)],
                       compiler_params=pltpu.CompilerParams(needs_layout_passes=False),
                       name="sc_invert")
    def k(dest_hbm, out_hbm, dest_v, table_v):
        wid = lax.axis_index("subcore") * SC_CORES + lax.axis_index("core")

        @pl.when(wid == 0)
        def _():
            pltpu.sync_copy(dest_hbm, dest_v)
            lane = lax.iota(i32, lanes)

            @pl.loop(0, n_rows // lanes)
            def _(j):
                off = pl.multiple_of(j * lanes, lanes)
                table_v[pl.ds(off, lanes)] = (lane + off) & (T - 1)


            @pl.loop(0, T // lanes)
            def _(j):
                off = pl.multiple_of(j * lanes, lanes)
                plsc.store_scatter(table_v, [dest_v[pl.ds(off, lanes)]], lane + off)

            pltpu.sync_copy(table_v, out_hbm)

    return k(dest)


def _sc_gather_rows(idx, src, chunk=SC_GATHER_ROWS):
    n = idx.shape[0]
    w = src.shape[1]
    workers = SC_CORES * SC_SUBCORES
    per_worker = n // workers
    mesh = plsc.VectorSubcoreMesh(core_axis_name="core", subcore_axis_name="subcore",
                                  num_cores=SC_CORES, num_subcores=SC_SUBCORES)

    n_chunks = per_worker // chunk
    assert n_chunks % 2 == 0

    @functools.partial(pl.kernel, out_type=jax.ShapeDtypeStruct((n, w), src.dtype), mesh=mesh,
                       scratch_types=[pltpu.VMEM((per_worker,), i32), pltpu.VMEM((2, chunk, w), src.dtype),
                                      pltpu.SemaphoreType.DMA((2,)), pltpu.SemaphoreType.DMA((2,))],
                       name="sc_gather_rows")
    def k(src_hbm, idx_hbm, out_hbm, idx_v, rows_v, gsem, wsem):
        wid = lax.axis_index("subcore") * SC_CORES + lax.axis_index("core")
        base = wid * per_worker
        pltpu.sync_copy(idx_hbm.at[pl.ds(base, per_worker)], idx_v)

        def gather(c, slot):
            rows = idx_v.at[pl.ds(pl.multiple_of(c * chunk, chunk), chunk)]
            return pltpu.make_async_copy(src_hbm.at[rows], rows_v.at[slot], gsem.at[slot])

        def write(c, slot):
            out = out_hbm.at[pl.ds(pl.multiple_of(base + c * chunk, chunk), chunk)]
            return pltpu.make_async_copy(rows_v.at[slot], out, wsem.at[slot])

        gather(0, 0).start()

        @pl.loop(0, n_chunks // 2)
        def _(pair):
            c = 2 * pair

            @pl.when(pair > 0)
            def _():
                write(c - 1, 1).wait()

            gather(c + 1, 1).start()
            gather(c, 0).wait()
            write(c, 0).start()

            @pl.when(c + 2 < n_chunks)
            def _():
                write(c, 0).wait()
                gather(c + 2, 0).start()

            gather(c + 1, 1).wait()
            write(c + 1, 1).start()

        write(n_chunks - 2, 0).wait()
        write(n_chunks - 1, 1).wait()

    return k(src, idx)


def _expert_kernel(grp_ref, ea_ref, eb_ref, used_ref, xs_ref, wg_ref, wu_ref, wd_ref, g_ref, b_ref, y_ref,
                   *, alpha):
    del grp_ref
    n = pl.program_id(0)

    @pl.when(n < used_ref[0])
    def _():
        h2 = xs_ref[:, :D]
        x = h2.astype(bf16)

        def expert(e):
            a = _dot(x, wg_ref[0, e])
            u = _dot(x, wu_ref[0, e])
            act = a * (1.0 / (1.0 + jnp.exp(-a))) * u
            return _dot(act.astype(bf16), wd_ref[0, e])

        ga = xs_ref[:, D:D + 1]
        gb = xs_ref[:, D + 1:D + 2]
        moe = ga * expert(ea_ref[n]) + gb * expert(eb_ref[n])
        y_ref[...] = _layer_norm(alpha * h2 + moe, g_ref[...], b_ref[...])

    @pl.when(n >= used_ref[0])
    def _():
        y_ref[...] = jnp.zeros_like(y_ref)


def _expert_kernel_into(grp_ref, ea_ref, eb_ref, used_ref, xs_ref, wg_ref, wu_ref, wd_ref, g_ref, b_ref,
                        ys_ref, y_ref, *, alpha):
    del ys_ref
    _expert_kernel(grp_ref, ea_ref, eb_ref, used_ref, xs_ref, wg_ref, wu_ref, wd_ref, g_ref, b_ref, y_ref,
                   alpha=alpha)


def _experts(grp, ea, eb, used, xs, wg, wu, wd, ln_g, ln_b, alpha, ys, first_block, total_rows):
    nblk = xs.shape[0] // ROW_BLOCK

    def xmap(n, grp, ea, eb, used):
        return (jnp.maximum(jnp.minimum(n, used[0] - 1), 0), 0)

    gmap = lambda n, grp, ea, eb, used: (grp[n], 0, 0, 0)
    gspec = lambda w: pl.BlockSpec((1,) + w.shape[1:], gmap, pipeline_mode=pl.Buffered(1))
    vec = pl.BlockSpec((1, D), lambda n, grp, ea, eb, used: (0, 0))
    in_specs = [pl.BlockSpec((ROW_BLOCK, XW), xmap), gspec(wg), gspec(wu), gspec(wd), vec, vec]
    operands = [grp, ea, eb, used, xs, wg, wu, wd, ln_g, ln_b]
    aliases = {}
    body = _expert_kernel
    if ys is not None:
        in_specs.append(pl.BlockSpec(memory_space=pl.ANY))
        aliases = {len(operands): 0}
        operands.append(ys)
        body = _expert_kernel_into
    grid_spec = pltpu.PrefetchScalarGridSpec(
        num_scalar_prefetch=4, grid=(nblk,), in_specs=in_specs,
        out_specs=pl.BlockSpec((ROW_BLOCK, D), lambda n, grp, ea, eb, used: (n + first_block, 0)),
    )
    return pl.pallas_call(
        functools.partial(body, alpha=alpha), grid_spec=grid_spec,
        out_shape=jax.ShapeDtypeStruct((total_rows, D), f32),
        input_output_aliases=aliases,
        compiler_params=_cparams(("arbitrary",)), name="experts",
    )(*operands)


def _pair_tables():
    ea = np.zeros((LANES,), np.int32)
    eb = np.zeros((LANES,), np.int32)
    for g in range(N_GROUPS):
        k = 0
        for a in range(EPG):
            for b in range(a + 1, EPG):
                ea[g * N_PAIRS + k] = a
                eb[g * N_PAIRS + k] = b
                k += 1
    return ea, eb


_PAIR_A, _PAIR_B = _pair_tables()


def _layer(h, mem, positions, w_in, b_forget, sinks, w_mix_out, ln_mix_g, ln_mix_b,
           w_xq, w_xkv, w_xout, ln_x_g, ln_x_b, w_rg, b_rg, w_re, b_re,
           w_eg, w_eu, w_ed, ln_f_g, ln_f_b, alpha):
    B, S, _ = h.shape
    T = B * S
    x2 = h.reshape(T, D)

    o = np.cumsum((0, FOX_W, FOX_W, FOX_W, N_FOX, SWA_Q_W, SWA_KV_W, SWA_KV_W))
    w_qf, w_kf, w_vf, w_fl, w_qs, w_ks, w_vs = (w_in[:, o[i]:o[i + 1]] for i in range(7))
    def regroup(a, axis):
        shp = a.shape
        a = jnp.moveaxis(a, axis, 0).reshape(N_SWA_KV, N_SWA // N_SWA_KV, HD, -1)
        return jnp.moveaxis(jnp.swapaxes(a, 0, 1).reshape(N_SWA * HD, -1), 0, axis).reshape(shp)

    w_all = jnp.concatenate([w_kf, regroup(w_qs, 1), w_ks, w_vs], axis=1).astype(bf16)
    wqt = w_qf.T.astype(bf16)
    wvt = w_vf.T.astype(bf16)
    wfl = w_fl.T.astype(bf16)
    bfc = b_forget.reshape(N_FOX, 1).astype(f32)
    half = HD // 2
    inv_freq = ROPE_THETA ** (-jnp.arange(half, dtype=f32) / half)
    per_row = LANES // half
    invf = jnp.tile(inv_freq, per_row).reshape(1, LANES)
    pos4 = jnp.repeat(positions.reshape(T // per_row, per_row).astype(i32), half, axis=1)
    cos, sin = (t.reshape(T, half) for t in _rope_table(pos4, invf))
    w_out = jnp.concatenate([w_mix_out[:FOX_W], regroup(w_mix_out[FOX_W:], 0)], axis=0).astype(bf16)

    tq = 512
    (qt, kf, vt, qs, ks, vs, lf), (eg16, eu16, ed16) = _in_proj(
        x2, cos, sin, w_all, wqt, wvt, wfl, bfc, w_eg, w_eu, w_ed, tq)
    c, ca = _cumsum(lf, S)
    c4 = c.reshape(N_FOX // 2, 2, T // tq, tq)
    r3 = lambda a: a.reshape(B, S, a.shape[-1])
    o_fox = _fox(qt, r3(kf), r3(ca), vt, c4, B, S, tq)
    o_swa = _swa(sinks.astype(f32) * LOG2E, r3(qs), r3(ks), r3(vs), B, S)

    kx, vx = _kvproj(mem.reshape(-1, D), w_xkv.astype(bf16))

    gpad = EPG - N_GROUPS
    wr = jnp.concatenate([jnp.pad(w_rg, ((0, 0), (0, gpad))),
                          jnp.transpose(w_re, (1, 0, 2)).reshape(D, N_EXPERTS)], axis=1)
    wr = jnp.pad(wr, ((0, 0), (0, LANES - wr.shape[1]))).astype(f32)
    wrh = wr.astype(bf16)
    wr2 = jnp.concatenate([wrh, (wr - wrh.astype(f32)).astype(bf16)], axis=1)
    br = jnp.pad(jnp.concatenate([jnp.pad(b_rg, (0, gpad)), b_re.reshape(-1)]), (0, LANES - EPG - N_EXPERTS))
    br = br.reshape(1, LANES).astype(f32)
    v2 = lambda a: a.reshape(1, D).astype(f32)
    h2x, bk8 = _mid(x2, o_fox.reshape(T, FOX_W), o_swa.reshape(T, SWA_Q_W), w_out,
                    v2(ln_mix_g), v2(ln_mix_b), w_xq.astype(bf16), kx, vx, w_xout.astype(bf16),
                    v2(ln_x_g), v2(ln_x_b), wr2, br, alpha, S)

    rank, cnt = _rank(bk8)
    counts = cnt[:, 0].astype(i32)
    padded = ((counts + ROW_BLOCK - 1) // ROW_BLOCK) * ROW_BLOCK
    pad_end = jnp.cumsum(padded)
    pad_start = (pad_end - padded).astype(i32)
    step = MOE_CHUNKS * SC_CORES * SC_SUBCORES * 2 * SC_GATHER_ROWS
    P = -(-(T + N_BUCKETS * ROW_BLOCK) // step) * step
    nblk = P // ROW_BLOCK
    used = (pad_end[-1] // ROW_BLOCK).astype(i32).reshape(1)
    blk_row = jnp.arange(nblk, dtype=i32)[:, None] * ROW_BLOCK
    blk_bucket = jnp.minimum(jnp.sum((pad_end[None, :] <= blk_row).astype(i32), axis=1), N_BUCKETS - 1)
    pick = (blk_bucket[:, None] == jnp.arange(LANES, dtype=i32)[None, :]).astype(i32)
    blk_a = jnp.sum(pick * jnp.asarray(_PAIR_A)[None, :], axis=1)
    blk_b = jnp.sum(pick * jnp.asarray(_PAIR_B)[None, :], axis=1)
    blk_g = blk_bucket // N_PAIRS
    by_group = lambda w: w.reshape((N_GROUPS, EPG) + w.shape[1:])

    dest = _dest(bk8, rank, pad_start.astype(f32).reshape(LANES, 1))[0]
    row_tok = _sc_invert(dest, P)
    cblk = nblk // MOE_CHUNKS
    ys = None
    for cidx in range(MOE_CHUNKS):
        lo = cidx * cblk
        xs = _sc_gather_rows(row_tok[lo * ROW_BLOCK:(lo + cblk) * ROW_BLOCK], h2x)
        used_c = jnp.clip(used - lo, 0, cblk)
        ys = _experts(blk_g[lo:lo + cblk], blk_a[lo:lo + cblk], blk_b[lo:lo + cblk], used_c, xs,
                      by_group(eg16), by_group(eu16), by_group(ed16), v2(ln_f_g), v2(ln_f_b), alpha,
                      ys, lo, P)
    return _sc_gather_rows(dest, ys).reshape(B, S, D)


def kernel(x, mem, positions, w_in, b_forget, sinks, w_mix_out, ln_mix_g, ln_mix_b, w_xq, w_xkv, w_xout,
           ln_x_g, ln_x_b, w_route_group, b_route_group, w_route_expert, b_route_expert,
           w_exp_gate, w_exp_up, w_exp_down, ln_ffn_g, ln_ffn_b):
    depth = w_in.shape[0]
    alpha = (2.0 * depth) ** 0.25
    h = x
    for l in range(depth):
        h = _layer(h, mem, positions, w_in[l], b_forget[l], sinks[l], w_mix_out[l], ln_mix_g[l], ln_mix_b[l],
                   w_xq[l], w_xkv[l], w_xout[l], ln_x_g[l], ln_x_b[l], w_route_group[l], b_route_group[l],
                   w_route_expert[l], b_route_expert[l], w_exp_gate[l], w_exp_up[l], w_exp_down[l],
                   ln_ffn_g[l], ln_ffn_b[l], alpha)
    return h
```

```python
import functools

import jax
import jax.numpy as jnp
import numpy as np
from jax import lax
from jax.experimental import pallas as pl
from jax.experimental.pallas import tpu as pltpu
from jax.experimental.pallas import tpu_sc as plsc

f32 = jnp.float32
bf16 = jnp.bfloat16
i32 = jnp.int32

D = 1024
HD = 64
N_FOX = 8
N_SWA = 8
N_SWA_KV = 2
FOX_W = 512
SWA_Q_W = 512
SWA_KV_W = 128
WINDOW = 128
ROPE_THETA = 10000.0
N_XH = 4
XHD = 256
N_GROUPS = 4
EPG = 8
N_EXPERTS = 32
D_EXPERT = 512
LN_EPS = 1e-5
NEG = -1e30
LOG2E = 1.4426950408889634
L_ROW = (HD, 0)

SC_CORES = 2
SC_SUBCORES = 16
SC_LANES = 16
SC_GATHER_ROWS = 32
MOE_CHUNKS = 4
LANES = 128
ROW_BLOCK = 192
N_PAIRS = EPG * (EPG - 1) // 2
N_BUCKETS = N_GROUPS * N_PAIRS
XW = D + LANES
VMEM_LIMIT = 56 * 1024 * 1024


def _cparams(sem):
    return pltpu.CompilerParams(dimension_semantics=sem, vmem_limit_bytes=VMEM_LIMIT)


def _layer_norm(v, g, b):
    mu = jnp.mean(v, axis=-1, keepdims=True)
    c = v - mu
    var = jnp.mean(c * c, axis=-1, keepdims=True)
    return c * lax.rsqrt(var + LN_EPS) * g + b


def _dot(a, b):
    return jnp.dot(a, b, preferred_element_type=f32)


def _dot_nt(a, b):
    return lax.dot_general(a, b, (((1,), (1,)), ((), ())), preferred_element_type=f32)


def _rope_table_kernel(pos_ref, invf_ref, cos_ref, sin_ref):
    ang = pos_ref[...].astype(f32) * invf_ref[...]
    cos_ref[...] = jnp.cos(ang)
    sin_ref[...] = jnp.sin(ang)


def _rope_table(pos4, invf, rows=1024):
    R = pos4.shape[0]
    blk = pl.BlockSpec((rows, LANES), lambda i: (i, 0))
    return pl.pallas_call(
        _rope_table_kernel, grid=(R // rows,),
        in_specs=[blk, pl.BlockSpec((1, LANES), lambda i: (0, 0))], out_specs=[blk, blk],
        out_shape=[jax.ShapeDtypeStruct((R, LANES), f32)] * 2,
        compiler_params=_cparams(("parallel",)), name="rope_table",
    )(pos4, invf)


def _inproj_kernel(x_ref, cos_ref, sin_ref, w_ref, wqt_ref, wvt_ref, wfl_ref, bf_ref, eg_ref, eu_ref, ed_ref,
                   qt_ref, kf_ref, vt_ref, qs_ref, ks_ref, vs_ref, lf_ref, egb_ref, eub_ref, edb_ref):
    tm = x_ref.shape[0]
    xb = x_ref[...].astype(bf16)
    egb_ref[...] = eg_ref[...].astype(bf16)
    eub_ref[...] = eu_ref[...].astype(bf16)
    edb_ref[...] = ed_ref[...].astype(bf16)

    def proj(lo, hi):
        return _dot(xb, w_ref[:, lo:hi])

    qt_ref[0] = (_dot_nt(wqt_ref[...], xb) * (0.125 * LOG2E)).astype(bf16)
    vt_ref[0] = _dot_nt(wvt_ref[...], xb).astype(bf16)
    kf_ref[...] = proj(0, 512).astype(bf16)

    reps = LANES // cos_ref.shape[1]
    cos = jnp.concatenate([cos_ref[...]] * reps, axis=1)
    sin = jnp.concatenate([sin_ref[...]] * reps, axis=1)
    lane = lax.broadcasted_iota(i32, (tm, LANES), 1)
    lo_half = (lane % HD) < (HD // 2)
    sin_s = jnp.where(lo_half, -sin, sin)

    def rope(z):
        rot = jnp.where(lo_half, pltpu.roll(z, LANES - HD // 2, 1), pltpu.roll(z, HD // 2, 1))
        return z * cos + rot * sin_s

    zq = proj(512, 1024)
    for g in range(4):
        sl = slice(g * LANES, (g + 1) * LANES)
        qs_ref[:, sl] = (rope(zq[:, sl]) * (0.125 * LOG2E)).astype(bf16)
    ks_ref[...] = rope(proj(1024, 1152)).astype(bf16)
    vs_ref[...] = proj(1152, 1280).astype(bf16)

    z = _dot_nt(wfl_ref[...], xb) + bf_ref[...]
    lf_ref[...] = jnp.minimum(z, 0.0) - jnp.log(1.0 + jnp.exp(-jnp.abs(z)))


def _in_proj(x2, cos, sin, w_all, wqt, wvt, wfl, bfc, w_eg, w_eu, w_ed, tm):
    T = x2.shape[0]
    steps = T // tm
    row = lambda w: pl.BlockSpec((tm, w), lambda i: (i, 0))
    full = lambda a: pl.BlockSpec(a.shape, lambda i: (0,) * a.ndim)
    fmaj = pl.BlockSpec((1, FOX_W, tm), lambda i: (i, 0, 0))
    flat = [w.reshape(-1, w.shape[-1]) for w in (w_eg, w_eu, w_ed)]
    slices = [pl.BlockSpec((w.shape[0] // steps, w.shape[1]), lambda i: (i, 0)) for w in flat]
    outs = pl.pallas_call(
        _inproj_kernel,
        grid=(steps,),
        in_specs=[row(D), row(cos.shape[1]), row(sin.shape[1]), full(w_all), full(wqt), full(wvt), full(wfl),
                  full(bfc)] + slices,
        out_specs=[fmaj, row(512), fmaj, row(512), row(128), row(128),
                   pl.BlockSpec((N_FOX, tm), lambda i: (0, i))] + slices,
        out_shape=[jax.ShapeDtypeStruct((steps, FOX_W, tm), bf16), jax.ShapeDtypeStruct((T, 512), bf16),
                   jax.ShapeDtypeStruct((steps, FOX_W, tm), bf16), jax.ShapeDtypeStruct((T, 512), bf16),
                   jax.ShapeDtypeStruct((T, 128), bf16), jax.ShapeDtypeStruct((T, 128), bf16),
                   jax.ShapeDtypeStruct((N_FOX, T), f32)]
        + [jax.ShapeDtypeStruct(w.shape, bf16) for w in flat],
        compiler_params=_cparams(("parallel",)),
        name="in_proj",
    )(x2, cos, sin, w_all, wqt, wvt, wfl, bfc, *flat)
    experts_bf16 = [o.reshape(w.shape) for o, w in zip(outs[7:], (w_eg, w_eu, w_ed))]
    return outs[:7], experts_bf16


def _cumsum_kernel(lf_ref, c_ref, ca_ref):
    S = lf_ref.shape[1]
    ch = 256
    r = lax.broadcasted_iota(i32, (ch, ch), 0)
    c = lax.broadcasted_iota(i32, (ch, ch), 1)
    tri = (r <= c).astype(f32)
    eye = (r == c).astype(bf16)
    stacked = jnp.concatenate([lf_ref[:, j * ch:(j + 1) * ch] for j in range(S // ch)], axis=0)
    local = jnp.dot(stacked, tri, precision=lax.Precision.HIGHEST, preferred_element_type=f32)
    carry = jnp.zeros((N_FOX, 1), f32)
    for j in range(S // ch):
        cc = local[j * N_FOX:(j + 1) * N_FOX] + carry
        carry = cc[:, ch - 1:ch]
        c2 = cc * LOG2E
        c_ref[:, j * ch:(j + 1) * ch] = c2
        neg = -c2
        hi = neg.astype(bf16)
        r1 = neg - hi.astype(f32)
        mid = r1.astype(bf16)
        lo = (r1 - mid.astype(f32)).astype(bf16)
        terms = jnp.concatenate([hi, mid, lo, jnp.zeros((LANES - 3 * N_FOX, ch), bf16)], axis=0)
        ca_ref[j * ch:(j + 1) * ch, :] = _dot_nt(eye, terms).astype(bf16)


def _cumsum(lf, S):
    T = lf.shape[1]
    spec = pl.BlockSpec((N_FOX, S), lambda b: (0, b))
    return pl.pallas_call(
        _cumsum_kernel, grid=(T // S,), in_specs=[spec],
        out_specs=[spec, pl.BlockSpec((S, LANES), lambda b: (b, 0))],
        out_shape=[jax.ShapeDtypeStruct((N_FOX, T), f32), jax.ShapeDtypeStruct((T, LANES), bf16)],
        compiler_params=_cparams(("parallel",)), name="cumsum",
    )(lf)


def _fox_kernel(qt_ref, k_ref, ca_ref, vt_ref, c_ref, o_ref, t0_ref, t1_ref, *, tq):
    hp = pl.program_id(1)
    i = pl.program_id(2)
    qt = qt_ref[0]
    row = lax.broadcasted_iota(i32, (LANES, tq), 0)
    is_a = row < HD
    zero = jnp.zeros_like(qt)
    q_ops = []
    for h in range(2):
        ones = jnp.where(((row & 7) == 2 * hp + h) & (row < 3 * N_FOX), 1.0, 0.0).astype(bf16)
        qh = jnp.where(is_a, qt, zero) if h == 0 else jnp.where(is_a, zero, qt)
        q_ops.append(jnp.concatenate([qh, ones], axis=0))
    kr = lax.broadcasted_iota(i32, (tq, tq), 0)
    qc = lax.broadcasted_iota(i32, (tq, tq), 1)
    causal = kr <= qc
    cq = [c_ref[0, h, pl.ds(i, 1), :] for h in range(2)]

    def scores(j, t_ref):
        off = pl.multiple_of(j * tq, tq)
        kblk = jnp.concatenate([k_ref[0, pl.ds(off, tq), :], ca_ref[0, pl.ds(off, tq), :]], axis=1)
        for h in range(2):
            t_ref[h] = _dot(kblk, q_ops[h])

    keep = [jnp.where(is_a, 1.0, 0.0).astype(bf16), jnp.where(is_a, 0.0, 1.0).astype(bf16)]
    ones_row = [jnp.where(row == L_ROW[h], 1.0, 0.0).astype(bf16) for h in range(2)]

    def softmax_pv(j, t_ref, carry, masked):
        vt = vt_ref[j]
        vts = [vt * keep[h] + ones_row[h] for h in range(2)]
        new = []
        for h in range(2):
            m, acc = carry[h]
            t = t_ref[h]
            if masked:
                t = jnp.where(causal, t, NEG)
            m_new = jnp.maximum(m, jnp.max(t, axis=0, keepdims=True) + cq[h])
            alpha = jnp.exp2(m - m_new)
            p = jnp.exp2(t + (cq[h] - m_new))
            acc = alpha * acc + _dot(vts[h], p.astype(bf16))
            new.append((m_new, acc))
        return tuple(new)

    def pair(k, carry):
        j = 2 * k
        scores(j + 1, t1_ref)
        carry = softmax_pv(j, t0_ref, carry, False)
        scores(j + 2, t0_ref)
        return softmax_pv(j + 1, t1_ref, carry, False)

    def odd_tail(carry):
        scores(i, t1_ref)
        carry = softmax_pv(i - 1, t0_ref, carry, False)
        return softmax_pv(i, t1_ref, carry, True)

    def even_tail(carry):
        return softmax_pv(i, t0_ref, carry, True)

    init = tuple((jnp.full((1, tq), NEG, f32), jnp.zeros((LANES, tq), f32)) for _ in range(2))
    scores(0, t0_ref)
    carry = lax.fori_loop(0, i // 2, pair, init)
    (_, acca), (_, accb) = lax.cond(i % 2 == 1, odd_tail, even_tail, carry)
    la = acca[L_ROW[0]:L_ROW[0] + 1, :]
    lb = accb[L_ROW[1]:L_ROW[1] + 1, :]
    ot = jnp.where(is_a, acca / la, accb / lb)
    o_ref[0] = jnp.transpose(ot).astype(bf16)


def _fox(qt, kf, ca, vt, c4, B, S, tq):
    nq = S // tq
    kernel = functools.partial(_fox_kernel, tq=tq)
    return pl.pallas_call(
        kernel,
        grid=(B, N_FOX // 2, nq),
        in_specs=[
            pl.BlockSpec((1, LANES, tq), lambda b, hp, i: (b * nq + i, hp, 0)),
            pl.BlockSpec((1, S, LANES), lambda b, hp, i: (b, 0, hp)),
            pl.BlockSpec((1, S, LANES), lambda b, hp, i: (b, 0, 0)),
            pl.BlockSpec((nq, LANES, tq), lambda b, hp, i: (b, hp, 0)),
            pl.BlockSpec((1, 2, nq, tq), lambda b, hp, i: (hp, 0, b, 0)),
        ],
        out_specs=pl.BlockSpec((1, tq, LANES), lambda b, hp, i: (b, i, hp)),
        out_shape=jax.ShapeDtypeStruct((B, S, FOX_W), bf16),
        scratch_shapes=[pltpu.VMEM((2, tq, tq), f32), pltpu.VMEM((2, tq, tq), f32)],
        compiler_params=_cparams(("parallel", "parallel", "arbitrary")),
        name="fox",
    )(qt, kf, ca, vt, c4)


def _swa_kernel(sink_ref, q_ref, k_ref, v_ref, o_ref, *, tq):
    W = WINDOW
    nsub = tq // W
    n0 = pl.program_id(1) * nsub
    lane = lax.broadcasted_iota(i32, (W, LANES), 1)
    is0 = lane < HD
    qoff = lax.broadcasted_iota(i32, (4 * W, 2 * W), 0) % W
    cols = lax.broadcasted_iota(i32, (4 * W, 2 * W), 1)
    bias_mid = jnp.where((cols - W <= qoff) & (qoff - (cols - W) < W), 0.0, NEG)
    bias_first = jnp.where(cols <= qoff, 0.0, NEG)
    rgrp = lax.broadcasted_iota(i32, (4 * W, 1), 0) // W
    for r in range(nsub):
        nb = n0 + r
        kstart = pl.multiple_of(jnp.maximum(nb * W - W, 0), W)
        ks = k_ref[0, pl.ds(kstart, 2 * W), :]
        vs = v_ref[0, pl.ds(kstart, 2 * W), :]
        bias = jnp.where(nb == 0, bias_first, bias_mid)
        outs = []
        for kv in range(2):
            keep = is0 if kv == 0 else jnp.logical_not(is0)
            parts = []
            for g in range(4):
                qg = q_ref[0, r * W:(r + 1) * W, g * LANES:(g + 1) * LANES]
                parts.append(jnp.where(keep, qg, jnp.zeros_like(qg)))
            qstack = jnp.concatenate(parts, axis=0)
            s = bias + _dot_nt(qstack, ks)
            sink = jnp.zeros((4 * W, 1), f32)
            for g in range(4):
                sink = jnp.where(rgrp == g, sink_ref[kv * 4 + g], sink)
            m = jnp.maximum(jnp.max(s, axis=1, keepdims=True), sink)
            e = jnp.exp2(s - m)
            den = jnp.sum(e, axis=1, keepdims=True) + jnp.exp2(sink - m)
            outs.append(_dot(e.astype(bf16), vs) / den)
        for g in range(4):
            og = jnp.where(is0, outs[0][g * W:(g + 1) * W], outs[1][g * W:(g + 1) * W])
            o_ref[0, r * W:(r + 1) * W, g * LANES:(g + 1) * LANES] = og.astype(bf16)


def _swa(sinks, qs, ks, vs, B, S, tq=512):
    kernel = functools.partial(_swa_kernel, tq=tq)
    return pl.pallas_call(
        kernel,
        grid=(B, S // tq),
        in_specs=[
            pl.BlockSpec(memory_space=pltpu.SMEM),
            pl.BlockSpec((1, tq, SWA_Q_W), lambda b, i: (b, i, 0)),
            pl.BlockSpec((1, S, SWA_KV_W), lambda b, i: (b, 0, 0)),
            pl.BlockSpec((1, S, SWA_KV_W), lambda b, i: (b, 0, 0)),
        ],
        out_specs=pl.BlockSpec((1, tq, SWA_Q_W), lambda b, i: (b, i, 0)),
        out_shape=jax.ShapeDtypeStruct((B, S, SWA_Q_W), bf16),
        compiler_params=_cparams(("parallel", "arbitrary")),
        name="swa",
    )(sinks, qs, ks, vs)


def _kvproj_kernel(m_ref, w_ref, k_ref, v_ref):
    mb = m_ref[...].astype(bf16)
    k_ref[...] = _dot(mb, w_ref[:, :D]).astype(bf16)
    v_ref[...] = _dot(mb, w_ref[:, D:]).astype(bf16)


def _kvproj(mem2, w_xkv, tm=512):
    R = mem2.shape[0]
    row = pl.BlockSpec((tm, D), lambda i: (i, 0))
    return pl.pallas_call(
        _kvproj_kernel, grid=(R // tm,),
        in_specs=[row, pl.BlockSpec(w_xkv.shape, lambda i: (0, 0))],
        out_specs=[row, row],
        out_shape=[jax.ShapeDtypeStruct((R, D), bf16)] * 2,
        compiler_params=_cparams(("parallel",)), name="kvproj",
    )(mem2, w_xkv)


def _mid_kernel(x_ref, of_ref, os_ref, wo_ref, g1_ref, b1_ref, wq_ref, k_ref, v_ref,
                wxo_ref, g2_ref, b2_ref, wr_ref, br_ref,
                h2_ref, bk_ref, oc_ref, *, alpha):
    tm = x_ref.shape[0]
    mix = _dot(of_ref[...], wo_ref[:FOX_W, :]) + _dot(os_ref[...], wo_ref[FOX_W:, :])
    h1 = _layer_norm(alpha * x_ref[...] + mix, g1_ref[...], b1_ref[...])

    q = (_dot(h1.astype(bf16), wq_ref[...]) * 0.0625).astype(bf16)
    for h in range(N_XH):
        sl = slice(h * XHD, (h + 1) * XHD)
        s = _dot_nt(q[:, sl], k_ref[:, sl])
        e = jnp.exp(s - jnp.max(s, axis=1, keepdims=True))
        p = e / jnp.sum(e, axis=1, keepdims=True)
        oc_ref[:, sl] = _dot(p.astype(bf16), v_ref[:, sl]).astype(bf16)
    xo = _dot(oc_ref[...], wxo_ref[...])
    h2 = _layer_norm(alpha * h1 + xo, g2_ref[...], b2_ref[...])
    h2_ref[:, :D] = h2

    hh = h2.astype(bf16)
    hl = (h2 - hh.astype(f32)).astype(bf16)
    hi_terms = _dot(hh, wr_ref[...])
    lg = hi_terms[:, :LANES] + _dot(hl, wr_ref[:, :LANES]) + hi_terms[:, LANES:] + br_ref[...]

    lgt = jnp.transpose(lg)
    row = lax.broadcasted_iota(i32, (EPG, tm), 0).astype(f32)
    big = float(EPG)

    def first_max(vals, mask):
        vm = jnp.where(mask, vals, NEG)
        top = jnp.max(vm, axis=0, keepdims=True)
        idx = jnp.min(jnp.where(mask & (vm == top), row, big), axis=0, keepdims=True)
        return top, idx

    gl = lgt[0:EPG]
    gmask = row < float(N_GROUPS)
    gmax, gidx = first_max(gl, gmask)
    g_val = 1.0 / jnp.sum(jnp.where(gmask, jnp.exp(gl - gmax), 0.0), axis=0, keepdims=True)
    sel = jnp.zeros((EPG, tm), f32)
    for g in range(N_GROUPS):
        sel = jnp.where(gidx == float(g), lgt[EPG * (g + 1):EPG * (g + 2)], sel)
    every = row >= 0.0
    v1, e1 = first_max(sel, every)
    v2, e2 = first_max(sel, row != e1)
    ex = jnp.exp(v2 - v1)
    w1 = g_val * (1.0 / (1.0 + ex))
    w2 = g_val * (ex / (1.0 + ex))
    first_low = e1 < e2
    ea = jnp.where(first_low, e1, e2)
    eb = jnp.where(first_low, e2, e1)
    ga = jnp.where(first_low, w1, w2)
    gb = jnp.where(first_low, w2, w1)
    pidx = ea * float(EPG - 1) - ea * (ea - 1.0) * 0.5 + (eb - ea - 1.0)
    bucket = gidx * float(N_PAIRS) + pidx

    bk_ref[...] = jnp.broadcast_to(bucket, (EPG, tm))
    gates = jnp.where(row == 0.0, ga, jnp.where(row == 1.0, gb, 0.0))
    gates = jnp.concatenate([gates, jnp.zeros((LANES - EPG, tm), f32)], axis=0)
    h2_ref[:, D:] = jnp.transpose(gates)


def _mid(x2, of2, os2, w_out, g1, b1, wq, kx, vx, wxo, g2, b2, wr2, br, alpha, S, tm=1024):
    T = x2.shape[0]
    M = kx.shape[0] // (T // S)
    per_b = S // tm
    row = lambda w: pl.BlockSpec((tm, w), lambda i: (i, 0))
    full = lambda a: pl.BlockSpec(a.shape, lambda i: (0,) * a.ndim)
    kvspec = pl.BlockSpec((M, D), lambda i: (i // per_b, 0))
    kernel = functools.partial(_mid_kernel, alpha=alpha)
    return pl.pallas_call(
        kernel,
        grid=(T // tm,),
        in_specs=[row(D), row(512), row(512), full(w_out), full(g1), full(b1), full(wq),
                  kvspec, kvspec, full(wxo), full(g2), full(b2), full(wr2), full(br)],
        out_specs=[row(XW), pl.BlockSpec((8, tm), lambda i: (0, i))],
        out_shape=[jax.ShapeDtypeStruct((T, XW), f32), jax.ShapeDtypeStruct((8, T), f32)],
        scratch_shapes=[pltpu.VMEM((tm, D), bf16)],
        compiler_params=_cparams(("parallel",)),
        name="mid",
    )(x2, of2, os2, w_out, g1, b1, wq, kx, vx, wxo, g2, b2, wr2, br)


def _rank_kernel(bk_ref, rank_ref, cnt_ref, carry_ref, *, chunk):
    sub = 256

    @pl.when(pl.program_id(0) == 0)
    def _():
        carry_ref[...] = jnp.zeros_like(carry_ref)

    r = lax.broadcasted_iota(i32, (sub, sub), 0)
    c = lax.broadcasted_iota(i32, (sub, sub), 1)
    before = (r < c).astype(bf16)
    bid = lax.broadcasted_iota(i32, (LANES, sub), 0).astype(f32)
    carry = carry_ref[...]
    for j in range(chunk // sub):
        bk = bk_ref[0:1, j * sub:(j + 1) * sub]
        hit = bid == bk
        oh = jnp.where(hit, 1.0, 0.0)
        prior = _dot(oh.astype(bf16), before) + carry
        rank_ref[:, j * sub:(j + 1) * sub] = jnp.sum(jnp.where(hit, prior, 0.0), axis=0, keepdims=True)
        carry = carry + jnp.sum(oh, axis=1, keepdims=True)
    carry_ref[...] = carry
    cnt_ref[...] = carry


def _rank(bk8, chunk=2048):
    T = bk8.shape[1]
    kernel = functools.partial(_rank_kernel, chunk=chunk)
    return pl.pallas_call(
        kernel, grid=(T // chunk,),
        in_specs=[pl.BlockSpec((8, chunk), lambda i: (0, i))],
        out_specs=[pl.BlockSpec((1, chunk), lambda i: (0, i)),
                   pl.BlockSpec((LANES, 1), lambda i: (0, 0))],
        out_shape=[jax.ShapeDtypeStruct((1, T), f32), jax.ShapeDtypeStruct((LANES, 1), f32)],
        scratch_shapes=[pltpu.VMEM((LANES, 1), f32)],
        compiler_params=_cparams(("arbitrary",)), name="rank",
    )(bk8)


def _dest_kernel(bk_ref, rank_ref, ps_ref, dest_ref):
    chunk = bk_ref.shape[1]
    bid = lax.broadcasted_iota(i32, (LANES, chunk), 0).astype(f32)
    start = jnp.sum(jnp.where(bid == bk_ref[0:1, :], ps_ref[...], 0.0), axis=0, keepdims=True)
    dest_ref[...] = (start + rank_ref[...]).astype(i32)


def _dest(bk8, rank, ps_col, chunk=2048):
    T = bk8.shape[1]
    return pl.pallas_call(
        _dest_kernel, grid=(T // chunk,),
        in_specs=[pl.BlockSpec((8, chunk), lambda i: (0, i)), pl.BlockSpec((1, chunk), lambda i: (0, i)),
                  pl.BlockSpec((LANES, 1), lambda i: (0, 0))],
        out_specs=pl.BlockSpec((1, chunk), lambda i: (0, i)),
        out_shape=jax.ShapeDtypeStruct((1, T), i32),
        compiler_params=_cparams(("parallel",)), name="dest",
    )(bk8, rank, ps_col)


def _sc_invert(dest, n_rows):
    T = dest.shape[0]
    assert T & (T - 1) == 0
    lanes = SC_LANES
    mesh = plsc.VectorSubcoreMesh(core_axis_name="core", subcore_axis_name="subcore",
                                  num_cores=SC_CORES, num_subcores=SC_SUBCORES)

    @functools.partial(pl.kernel, out_type=jax.ShapeDtypeStruct((n_rows,), i32), mesh=mesh,
                       scratch_types=[pltpu.VMEM((T,), i32), pltpu.VMEM((n_rows,), i32)],
                       compiler_params=pltpu.CompilerParams(needs_layout_passes=False),
                       name="sc_invert")
    def k(dest_hbm, out_hbm, dest_v, table_v):
        wid = lax.axis_index("subcore") * SC_CORES + lax.axis_index("core")

        @pl.when(wid == 0)
        def _():
            pltpu.sync_copy(dest_hbm, dest_v)
            lane = lax.iota(i32, lanes)

            @pl.loop(0, n_rows // lanes)
            def _(j):
                off = pl.multiple_of(j * lanes, lanes)
                table_v[pl.ds(off, lanes)] = (lane + off) & (T - 1)


            @pl.loop(0, T // lanes)
            def _(j):
                off = pl.multiple_of(j * lanes, lanes)
                plsc.store_scatter(table_v, [dest_v[pl.ds(off, lanes)]], lane + off)

            pltpu.sync_copy(table_v, out_hbm)

    return k(dest)


def _sc_gather_rows(idx, src, chunk=SC_GATHER_ROWS):
    n = idx.shape[0]
    w = src.shape[1]
    workers = SC_CORES * SC_SUBCORES
    per_worker = n // workers
    mesh = plsc.VectorSubcoreMesh(core_axis_name="core", subcore_axis_name="subcore",
                                  num_cores=SC_CORES, num_subcores=SC_SUBCORES)

    n_chunks = per_worker // chunk
    assert n_chunks % 2 == 0

    @functools.partial(pl.kernel, out_type=jax.ShapeDtypeStruct((n, w), src.dtype), mesh=mesh,
                       scratch_types=[pltpu.VMEM((per_worker,), i32), pltpu.VMEM((2, chunk, w), src.dtype),
                                      pltpu.SemaphoreType.DMA((2,)), pltpu.SemaphoreType.DMA((2,))],
                       name="sc_gather_rows")
    def k(src_hbm, idx_hbm, out_hbm, idx_v, rows_v, gsem, wsem):
        wid = lax.axis_index("subcore") * SC_CORES + lax.axis_index("core")
        base = wid * per_worker
        pltpu.sync_copy(idx_hbm.at[pl.ds(base, per_worker)], idx_v)

        def gather(c, slot):
            rows = idx_v.at[pl.ds(pl.multiple_of(c * chunk, chunk), chunk)]
            return pltpu.make_async_copy(src_hbm.at[rows], rows_v.at[slot], gsem.at[slot])

        def write(c, slot):
            out = out_hbm.at[pl.ds(pl.multiple_of(base + c * chunk, chunk), chunk)]
            return pltpu.make_async_copy(rows_v.at[slot], out, wsem.at[slot])

        gather(0, 0).start()

        @pl.loop(0, n_chunks // 2)
        def _(pair):
            c = 2 * pair

            @pl.when(pair > 0)
            def _():
                write(c - 1, 1).wait()

            gather(c + 1, 1).start()
            gather(c, 0).wait()
            write(c, 0).start()

            @pl.when(c + 2 < n_chunks)
            def _():
                write(c, 0).wait()
                gather(c + 2, 0).start()

            gather(c + 1, 1).wait()
            write(c + 1, 1).start()

        write(n_chunks - 2, 0).wait()
        write(n_chunks - 1, 1).wait()

    return k(src, idx)


def _expert_kernel(grp_ref, ea_ref, eb_ref, used_ref, xs_ref, wg_ref, wu_ref, wd_ref, g_ref, b_ref, y_ref,
                   *, alpha):
    del grp_ref
    n = pl.program_id(0)

    @pl.when(n < used_ref[0])
    def _():
        h2 = xs_ref[:, :D]
        x = h2.astype(bf16)

        def expert(e):
            a = _dot(x, wg_ref[0, e])
            u = _dot(x, wu_ref[0, e])
            act = a * (1.0 / (1.0 + jnp.exp(-a))) * u
            return _dot(act.astype(bf16), wd_ref[0, e])

        ga = xs_ref[:, D:D + 1]
        gb = xs_ref[:, D + 1:D + 2]
        moe = ga * expert(ea_ref[n]) + gb * expert(eb_ref[n])
        y_ref[...] = _layer_norm(alpha * h2 + moe, g_ref[...], b_ref[...])

    @pl.when(n >= used_ref[0])
    def _():
        y_ref[...] = jnp.zeros_like(y_ref)


def _expert_kernel_into(grp_ref, ea_ref, eb_ref, used_ref, xs_ref, wg_ref, wu_ref, wd_ref, g_ref, b_ref,
                        ys_ref, y_ref, *, alpha):
    del ys_ref
    _expert_kernel(grp_ref, ea_ref, eb_ref, used_ref, xs_ref, wg_ref, wu_ref, wd_ref, g_ref, b_ref, y_ref,
                   alpha=alpha)


def _experts(grp, ea, eb, used, xs, wg, wu, wd, ln_g, ln_b, alpha, ys, first_block, total_rows):
    nblk = xs.shape[0] // ROW_BLOCK

    def xmap(n, grp, ea, eb, used):
        return (jnp.maximum(jnp.minimum(n, used[0] - 1), 0), 0)

    gmap = lambda n, grp, ea, eb, used: (grp[n], 0, 0, 0)
    gspec = lambda w: pl.BlockSpec((1,) + w.shape[1:], gmap, pipeline_mode=pl.Buffered(1))
    vec = pl.BlockSpec((1, D), lambda n, grp, ea, eb, used: (0, 0))
    in_specs = [pl.BlockSpec((ROW_BLOCK, XW), xmap), gspec(wg), gspec(wu), gspec(wd), vec, vec]
    operands = [grp, ea, eb, used, xs, wg, wu, wd, ln_g, ln_b]
    aliases = {}
    body = _expert_kernel
    if ys is not None:
        in_specs.append(pl.BlockSpec(memory_space=pl.ANY))
        aliases = {len(operands): 0}
        operands.append(ys)
        body = _expert_kernel_into
    grid_spec = pltpu.PrefetchScalarGridSpec(
        num_scalar_prefetch=4, grid=(nblk,), in_specs=in_specs,
        out_specs=pl.BlockSpec((ROW_BLOCK, D), lambda n, grp, ea, eb, used: (n + first_block, 0)),
    )
    return pl.pallas_call(
        functools.partial(body, alpha=alpha), grid_spec=grid_spec,
        out_shape=jax.ShapeDtypeStruct((total_rows, D), f32),
        input_output_aliases=aliases,
        compiler_params=_cparams(("arbitrary",)), name="experts",
    )(*operands)


def _pair_tables():
    ea = np.zeros((LANES,), np.int32)
    eb = np.zeros((LANES,), np.int32)
    for g in range(N_GROUPS):
        k = 0
        for a in range(EPG):
            for b in range(a + 1, EPG):
                ea[g * N_PAIRS + k] = a
                eb[g * N_PAIRS + k] = b
                k += 1
    return ea, eb


_PAIR_A, _PAIR_B = _pair_tables()


def _layer(h, mem, positions, w_in, b_forget, sinks, w_mix_out, ln_mix_g, ln_mix_b,
           w_xq, w_xkv, w_xout, ln_x_g, ln_x_b, w_rg, b_rg, w_re, b_re,
           w_eg, w_eu, w_ed, ln_f_g, ln_f_b, alpha):
    B, S, _ = h.shape
    T = B * S
    x2 = h.reshape(T, D)

    o = np.cumsum((0, FOX_W, FOX_W, FOX_W, N_FOX, SWA_Q_W, SWA_KV_W, SWA_KV_W))
    w_qf, w_kf, w_vf, w_fl, w_qs, w_ks, w_vs = (w_in[:, o[i]:o[i + 1]] for i in range(7))
    def regroup(a, axis):
        shp = a.shape
        a = jnp.moveaxis(a, axis, 0).reshape(N_SWA_KV, N_SWA // N_SWA_KV, HD, -1)
        return jnp.moveaxis(jnp.swapaxes(a, 0, 1).reshape(N_SWA * HD, -1), 0, axis).reshape(shp)

    w_all = jnp.concatenate([w_kf, regroup(w_qs, 1), w_ks, w_vs], axis=1).astype(bf16)
    wqt = w_qf.T.astype(bf16)
    wvt = w_vf.T.astype(bf16)
    wfl = w_fl.T.astype(bf16)
    bfc = b_forget.reshape(N_FOX, 1).astype(f32)
    half = HD // 2
    inv_freq = ROPE_THETA ** (-jnp.arange(half, dtype=f32) / half)
    per_row = LANES // half
    invf = jnp.tile(inv_freq, per_row).reshape(1, LANES)
    pos4 = jnp.repeat(positions.reshape(T // per_row, per_row).astype(i32), half, axis=1)
    cos, sin = (t.reshape(T, half) for t in _rope_table(pos4, invf))
    w_out = jnp.concatenate([w_mix_out[:FOX_W], regroup(w_mix_out[FOX_W:], 0)], axis=0).astype(bf16)

    tq = 512
    (qt, kf, vt, qs, ks, vs, lf), (eg16, eu16, ed16) = _in_proj(
        x2, cos, sin, w_all, wqt, wvt, wfl, bfc, w_eg, w_eu, w_ed, tq)
    c, ca = _cumsum(lf, S)
    c4 = c.reshape(N_FOX // 2, 2, T // tq, tq)
    r3 = lambda a: a.reshape(B, S, a.shape[-1])
    o_fox = _fox(qt, r3(kf), r3(ca), vt, c4, B, S, tq)
    o_swa = _swa(sinks.astype(f32) * LOG2E, r3(qs), r3(ks), r3(vs), B, S)

    kx, vx = _kvproj(mem.reshape(-1, D), w_xkv.astype(bf16))

    gpad = EPG - N_GROUPS
    wr = jnp.concatenate([jnp.pad(w_rg, ((0, 0), (0, gpad))),
                          jnp.transpose(w_re, (1, 0, 2)).reshape(D, N_EXPERTS)], axis=1)
    wr = jnp.pad(wr, ((0, 0), (0, LANES - wr.shape[1]))).astype(f32)
    wrh = wr.astype(bf16)
    wr2 = jnp.concatenate([wrh, (wr - wrh.astype(f32)).astype(bf16)], axis=1)
    br = jnp.pad(jnp.concatenate([jnp.pad(b_rg, (0, gpad)), b_re.reshape(-1)]), (0, LANES - EPG - N_EXPERTS))
    br = br.reshape(1, LANES).astype(f32)
    v2 = lambda a: a.reshape(1, D).astype(f32)
    h2x, bk8 = _mid(x2, o_fox.reshape(T, FOX_W), o_swa.reshape(T, SWA_Q_W), w_out,
                    v2(ln_mix_g), v2(ln_mix_b), w_xq.astype(bf16), kx, vx, w_xout.astype(bf16),
                    v2(ln_x_g), v2(ln_x_b), wr2, br, alpha, S)

    rank, cnt = _rank(bk8)
    counts = cnt[:, 0].astype(i32)
    padded = ((counts + ROW_BLOCK - 1) // ROW_BLOCK) * ROW_BLOCK
    pad_end = jnp.cumsum(padded)
    pad_start = (pad_end - padded).astype(i32)
    step = np.lcm(MOE_CHUNKS * SC_CORES * SC_SUBCORES * 2 * SC_GATHER_ROWS, MOE_CHUNKS * ROW_BLOCK)
    P = int(-(-(T + N_BUCKETS * ROW_BLOCK) // step) * step)
    nblk = P // ROW_BLOCK
    used = (pad_end[-1] // ROW_BLOCK).astype(i32).reshape(1)
    blk_row = jnp.arange(nblk, dtype=i32)[:, None] * ROW_BLOCK
    blk_bucket = jnp.minimum(jnp.sum((pad_end[None, :] <= blk_row).astype(i32), axis=1), N_BUCKETS - 1)
    pick = (blk_bucket[:, None] == jnp.arange(LANES, dtype=i32)[None, :]).astype(i32)
    blk_a = jnp.sum(pick * jnp.asarray(_PAIR_A)[None, :], axis=1)
    blk_b = jnp.sum(pick * jnp.asarray(_PAIR_B)[None, :], axis=1)
    blk_g = blk_bucket // N_PAIRS
    by_group = lambda w: w.reshape((N_GROUPS, EPG) + w.shape[1:])

    dest = _dest(bk8, rank, pad_start.astype(f32).reshape(LANES, 1))[0]
    row_tok = _sc_invert(dest, P)
    cblk = nblk // MOE_CHUNKS
    ys = None
    for cidx in range(MOE_CHUNKS):
        lo = cidx * cblk
        xs = _sc_gather_rows(row_tok[lo * ROW_BLOCK:(lo + cblk) * ROW_BLOCK], h2x)
        used_c = jnp.clip(used - lo, 0, cblk)
        ys = _experts(blk_g[lo:lo + cblk], blk_a[lo:lo + cblk], blk_b[lo:lo + cblk], used_c, xs,
                      by_group(eg16), by_group(eu16), by_group(ed16), v2(ln_f_g), v2(ln_f_b), alpha,
                      ys, lo, P)
    return _sc_gather_rows(dest, ys).reshape(B, S, D)


def kernel(x, mem, positions, w_in, b_forget, sinks, w_mix_out, ln_mix_g, ln_mix_b, w_xq, w_xkv, w_xout,
           ln_x_g, ln_x_b, w_route_group, b_route_group, w_route_expert, b_route_expert,
           w_exp_gate, w_exp_up, w_exp_down, ln_ffn_g, ln_ffn_b):
    depth = w_in.shape[0]
    alpha = (2.0 * depth) ** 0.25
    h = x
    for l in range(depth):
        h = _layer(h, mem, positions, w_in[l], b_forget[l], sinks[l], w_mix_out[l], ln_mix_g[l], ln_mix_b[l],
                   w_xq[l], w_xkv[l], w_xout[l], ln_x_g[l], ln_x_b[l], w_route_group[l], b_route_group[l],
                   w_route_expert[l], b_route_expert[l], w_exp_gate[l], w_exp_up[l], w_exp_down[l],
                   ln_ffn_g[l], ln_ffn_b[l], alpha)
    return h
```

```python
import functools

import jax
import jax.numpy as jnp
import numpy as np
from jax import lax
from jax.experimental import pallas as pl
from jax.experimental.pallas import tpu as pltpu
from jax.experimental.pallas import tpu_sc as plsc

f32 = jnp.float32
bf16 = jnp.bfloat16
i32 = jnp.int32

D = 1024
HD = 64
N_FOX = 8
N_SWA = 8
N_SWA_KV = 2
FOX_W = 512
SWA_Q_W = 512
SWA_KV_W = 128
WINDOW = 128
ROPE_THETA = 10000.0
N_XH = 4
XHD = 256
N_GROUPS = 4
EPG = 8
N_EXPERTS = 32
D_EXPERT = 512
LN_EPS = 1e-5
NEG = -1e30
LOG2E = 1.4426950408889634
L_ROW = (HD, 0)
SKIP_LOG2 = 160.0

SC_CORES = 2
SC_SUBCORES = 16
SC_LANES = 16
SC_GATHER_ROWS = 32
MOE_CHUNKS = 4
LANES = 128
ROW_BLOCK = 128
N_PAIRS = EPG * (EPG - 1) // 2
N_BUCKETS = N_GROUPS * N_PAIRS
XW = D + LANES
VMEM_LIMIT = 56 * 1024 * 1024


def _cparams(sem):
    return pltpu.CompilerParams(dimension_semantics=sem, vmem_limit_bytes=VMEM_LIMIT)


def _layer_norm(v, g, b):
    mu = jnp.mean(v, axis=-1, keepdims=True)
    c = v - mu
    var = jnp.mean(c * c, axis=-1, keepdims=True)
    return c * lax.rsqrt(var + LN_EPS) * g + b


def _dot(a, b):
    return jnp.dot(a, b, preferred_element_type=f32)


def _dot_nt(a, b):
    return lax.dot_general(a, b, (((1,), (1,)), ((), ())), preferred_element_type=f32)


def _rope_table_kernel(pos_ref, invf_ref, cos_ref, sin_ref):
    ang = pos_ref[...].astype(f32) * invf_ref[...]
    cos_ref[...] = jnp.cos(ang)
    sin_ref[...] = jnp.sin(ang)


def _rope_table(pos4, invf, rows=1024):
    R = pos4.shape[0]
    blk = pl.BlockSpec((rows, LANES), lambda i: (i, 0))
    return pl.pallas_call(
        _rope_table_kernel, grid=(R // rows,),
        in_specs=[blk, pl.BlockSpec((1, LANES), lambda i: (0, 0))], out_specs=[blk, blk],
        out_shape=[jax.ShapeDtypeStruct((R, LANES), f32)] * 2,
        compiler_params=_cparams(("parallel",)), name="rope_table",
    )(pos4, invf)


def _inproj_kernel(x_ref, cos_ref, sin_ref, w_ref, wqt_ref, wvt_ref, wfl_ref, bf_ref, ind_ref,
                   eg_ref, eu_ref, ed_ref,
                   qt_ref, kf_ref, vt_ref, qs_ref, ks_ref, vs_ref, lf_ref, qn_ref, kn_ref,
                   egb_ref, eub_ref, edb_ref):
    tm = x_ref.shape[0]
    xb = x_ref[...].astype(bf16)
    egb_ref[...] = eg_ref[...].astype(bf16)
    eub_ref[...] = eu_ref[...].astype(bf16)
    edb_ref[...] = ed_ref[...].astype(bf16)

    def proj(lo, hi):
        return _dot(xb, w_ref[:, lo:hi])

    qv = _dot_nt(wqt_ref[...], xb) * (0.125 * LOG2E)
    qt_ref[0] = qv.astype(bf16)
    vt_ref[0] = _dot_nt(wvt_ref[...], xb).astype(bf16)
    kv = proj(0, 512)
    kf_ref[...] = kv.astype(bf16)
    q2 = jnp.sum((qv * qv).reshape(N_FOX, HD, tm), axis=1)
    qn_ref[0] = jnp.broadcast_to(jnp.max(q2, axis=1, keepdims=True), (N_FOX, LANES))
    k2 = _dot((kv * kv).astype(bf16), ind_ref[...])
    kn_ref[0] = jnp.broadcast_to(jnp.max(k2, axis=0, keepdims=True), (N_FOX, LANES))

    reps = LANES // cos_ref.shape[1]
    cos = jnp.concatenate([cos_ref[...]] * reps, axis=1)
    sin = jnp.concatenate([sin_ref[...]] * reps, axis=1)
    lane = lax.broadcasted_iota(i32, (tm, LANES), 1)
    lo_half = (lane % HD) < (HD // 2)
    sin_s = jnp.where(lo_half, -sin, sin)

    def rope(z):
        rot = jnp.where(lo_half, pltpu.roll(z, LANES - HD // 2, 1), pltpu.roll(z, HD // 2, 1))
        return z * cos + rot * sin_s

    zq = proj(512, 1024)
    for g in range(4):
        sl = slice(g * LANES, (g + 1) * LANES)
        qs_ref[:, sl] = (rope(zq[:, sl]) * (0.125 * LOG2E)).astype(bf16)
    ks_ref[...] = rope(proj(1024, 1152)).astype(bf16)
    vs_ref[...] = proj(1152, 1280).astype(bf16)

    z = _dot_nt(wfl_ref[...], xb) + bf_ref[...]
    lf_ref[...] = jnp.minimum(z, 0.0) - jnp.log(1.0 + jnp.exp(-jnp.abs(z)))


def _in_proj(x2, cos, sin, w_all, wqt, wvt, wfl, bfc, w_eg, w_eu, w_ed, tm):
    T = x2.shape[0]
    steps = T // tm
    row = lambda w: pl.BlockSpec((tm, w), lambda i: (i, 0))
    full = lambda a: pl.BlockSpec(a.shape, lambda i: (0,) * a.ndim)
    fmaj = pl.BlockSpec((1, FOX_W, tm), lambda i: (i, 0, 0))
    flat = [w.reshape(-1, w.shape[-1]) for w in (w_eg, w_eu, w_ed)]
    slices = [pl.BlockSpec((w.shape[0] // steps, w.shape[1]), lambda i: (i, 0)) for w in flat]
    ind = jnp.asarray(np.arange(FOX_W)[:, None] // HD == np.arange(LANES)[None, :], bf16)
    nrm = pl.BlockSpec((1, N_FOX, LANES), lambda i: (i, 0, 0))
    outs = pl.pallas_call(
        _inproj_kernel,
        grid=(steps,),
        in_specs=[row(D), row(cos.shape[1]), row(sin.shape[1]), full(w_all), full(wqt), full(wvt), full(wfl),
                  full(bfc), full(ind)] + slices,
        out_specs=[fmaj, row(512), fmaj, row(512), row(128), row(128),
                   pl.BlockSpec((N_FOX, tm), lambda i: (0, i)), nrm, nrm] + slices,
        out_shape=[jax.ShapeDtypeStruct((steps, FOX_W, tm), bf16), jax.ShapeDtypeStruct((T, 512), bf16),
                   jax.ShapeDtypeStruct((steps, FOX_W, tm), bf16), jax.ShapeDtypeStruct((T, 512), bf16),
                   jax.ShapeDtypeStruct((T, 128), bf16), jax.ShapeDtypeStruct((T, 128), bf16),
                   jax.ShapeDtypeStruct((N_FOX, T), f32),
                   jax.ShapeDtypeStruct((steps, N_FOX, LANES), f32), jax.ShapeDtypeStruct((steps, N_FOX, LANES), f32)]
        + [jax.ShapeDtypeStruct(w.shape, bf16) for w in flat],
        compiler_params=_cparams(("parallel",)),
        name="in_proj",
    )(x2, cos, sin, w_all, wqt, wvt, wfl, bfc, ind, *flat)
    experts_bf16 = [o.reshape(w.shape) for o, w in zip(outs[9:], (w_eg, w_eu, w_ed))]
    return outs[:9], experts_bf16


def _cumsum_kernel(lf_ref, c_ref, ca_ref):
    S = lf_ref.shape[1]
    ch = 256
    r = lax.broadcasted_iota(i32, (ch, ch), 0)
    c = lax.broadcasted_iota(i32, (ch, ch), 1)
    tri = (r <= c).astype(f32)
    eye = (r == c).astype(bf16)
    stacked = jnp.concatenate([lf_ref[:, j * ch:(j + 1) * ch] for j in range(S // ch)], axis=0)
    local = jnp.dot(stacked, tri, precision=lax.Precision.HIGHEST, preferred_element_type=f32)
    carry = jnp.zeros((N_FOX, 1), f32)
    for j in range(S // ch):
        cc = local[j * N_FOX:(j + 1) * N_FOX] + carry
        carry = cc[:, ch - 1:ch]
        c2 = cc * LOG2E
        c_ref[:, j * ch:(j + 1) * ch] = c2
        neg = -c2
        hi = neg.astype(bf16)
        r1 = neg - hi.astype(f32)
        mid = r1.astype(bf16)
        lo = (r1 - mid.astype(f32)).astype(bf16)
        terms = jnp.concatenate([hi, mid, lo, jnp.zeros((LANES - 3 * N_FOX, ch), bf16)], axis=0)
        ca_ref[j * ch:(j + 1) * ch, :] = _dot_nt(eye, terms).astype(bf16)


def _cumsum(lf, S):
    T = lf.shape[1]
    spec = pl.BlockSpec((N_FOX, S), lambda b: (0, b))
    return pl.pallas_call(
        _cumsum_kernel, grid=(T // S,), in_specs=[spec],
        out_specs=[spec, pl.BlockSpec((S, LANES), lambda b: (b, 0))],
        out_shape=[jax.ShapeDtypeStruct((N_FOX, T), f32), jax.ShapeDtypeStruct((T, LANES), bf16)],
        compiler_params=_cparams(("parallel",)), name="cumsum",
    )(lf)


def _fox_kernel(j0_ref, qt_ref, k_ref, ca_ref, vt_ref, c_ref, o_ref, t0_ref, t1_ref, *, tq):
    hp = pl.program_id(1)
    i = pl.program_id(2)
    qt = qt_ref[0]
    row = lax.broadcasted_iota(i32, (LANES, tq), 0)
    is_a = row < HD
    zero = jnp.zeros_like(qt)
    q_ops = []
    for h in range(2):
        ones = jnp.where(((row & 7) == 2 * hp + h) & (row < 3 * N_FOX), 1.0, 0.0).astype(bf16)
        qh = jnp.where(is_a, qt, zero) if h == 0 else jnp.where(is_a, zero, qt)
        q_ops.append(jnp.concatenate([qh, ones], axis=0))
    kr = lax.broadcasted_iota(i32, (tq, tq), 0)
    qc = lax.broadcasted_iota(i32, (tq, tq), 1)
    causal = kr <= qc
    cq = [c_ref[0, h, pl.ds(i, 1), :] for h in range(2)]

    def scores(j, t_ref):
        off = pl.multiple_of(j * tq, tq)
        kblk = jnp.concatenate([k_ref[0, pl.ds(off, tq), :], ca_ref[0, pl.ds(off, tq), :]], axis=1)
        for h in range(2):
            t_ref[h] = _dot(kblk, q_ops[h])

    keep = [jnp.where(is_a, 1.0, 0.0).astype(bf16), jnp.where(is_a, 0.0, 1.0).astype(bf16)]
    ones_row = [jnp.where(row == L_ROW[h], 1.0, 0.0).astype(bf16) for h in range(2)]

    def softmax_pv(j, t_ref, carry, masked):
        vt = vt_ref[j]
        vts = [vt * keep[h] + ones_row[h] for h in range(2)]
        new = []
        for h in range(2):
            m, acc = carry[h]
            t = t_ref[h]
            if masked:
                t = jnp.where(causal, t, NEG)
            m_new = jnp.maximum(m, jnp.max(t, axis=0, keepdims=True) + cq[h])
            alpha = jnp.exp2(m - m_new)
            p = jnp.exp2(t + (cq[h] - m_new))
            acc = alpha * acc + _dot(vts[h], p.astype(bf16))
            new.append((m_new, acc))
        return tuple(new)

    j0 = j0_ref[(pl.program_id(0) * pl.num_programs(1) + hp) * pl.num_programs(2) + i]
    n_full = i - j0

    def pair(k, carry):
        j = j0 + 2 * k
        scores(j + 1, t1_ref)
        carry = softmax_pv(j, t0_ref, carry, False)
        scores(j + 2, t0_ref)
        return softmax_pv(j + 1, t1_ref, carry, False)

    def odd_tail(carry):
        scores(i, t1_ref)
        carry = softmax_pv(i - 1, t0_ref, carry, False)
        return softmax_pv(i, t1_ref, carry, True)

    def even_tail(carry):
        return softmax_pv(i, t0_ref, carry, True)

    init = tuple((jnp.full((1, tq), NEG, f32), jnp.zeros((LANES, tq), f32)) for _ in range(2))
    scores(j0, t0_ref)
    carry = lax.fori_loop(0, n_full // 2, pair, init)
    (_, acca), (_, accb) = lax.cond(n_full % 2 == 1, odd_tail, even_tail, carry)
    la = acca[L_ROW[0]:L_ROW[0] + 1, :]
    lb = accb[L_ROW[1]:L_ROW[1] + 1, :]
    ot = jnp.where(is_a, acca / la, accb / lb)
    o_ref[0] = jnp.transpose(ot).astype(bf16)


def _fox_first_blocks(qn, kn, c, B, S, tq):
    nq = S // tq
    q2 = qn[:, :, 0].reshape(B, nq, N_FOX)
    k2 = jnp.max(kn[:, 0, :N_FOX].reshape(B, nq, N_FOX), axis=1)
    qk = 1.02 * jnp.sqrt(q2 * k2[:, None, :])
    cb = c.reshape(N_FOX, B, nq, tq)
    c_first = jnp.transpose(cb[..., 0], (1, 2, 0))
    c_last = jnp.transpose(cb[..., tq - 1], (1, 2, 0))
    gap = c_last[:, None, :, :] - c_first[:, :, None, :]
    blk = jnp.arange(nq)
    negligible = (2.0 * qk[:, :, None, :] - gap < -SKIP_LOG2) & (blk[None, :] < blk[:, None])[None, :, :, None]
    pair = jnp.all(negligible.reshape(B, nq, nq, N_FOX // 2, 2), axis=-1)
    j0 = jnp.sum(jnp.cumprod(pair.astype(i32), axis=2), axis=2)
    return jnp.transpose(j0, (0, 2, 1)).reshape(-1).astype(i32)


def _fox(j0, qt, kf, ca, vt, c4, B, S, tq):
    nq = S // tq
    kernel = functools.partial(_fox_kernel, tq=tq)
    return pl.pallas_call(
        kernel,
        grid=(B, N_FOX // 2, nq),
        in_specs=[
            pl.BlockSpec(memory_space=pltpu.SMEM),
            pl.BlockSpec((1, LANES, tq), lambda b, hp, i: (b * nq + i, hp, 0)),
            pl.BlockSpec((1, S, LANES), lambda b, hp, i: (b, 0, hp)),
            pl.BlockSpec((1, S, LANES), lambda b, hp, i: (b, 0, 0)),
            pl.BlockSpec((nq, LANES, tq), lambda b, hp, i: (b, hp, 0)),
            pl.BlockSpec((1, 2, nq, tq), lambda b, hp, i: (hp, 0, b, 0)),
        ],
        out_specs=pl.BlockSpec((1, tq, LANES), lambda b, hp, i: (b, i, hp)),
        out_shape=jax.ShapeDtypeStruct((B, S, FOX_W), bf16),
        scratch_shapes=[pltpu.VMEM((2, tq, tq), f32), pltpu.VMEM((2, tq, tq), f32)],
        compiler_params=_cparams(("parallel", "parallel", "arbitrary")),
        name="fox",
    )(j0, qt, kf, ca, vt, c4)


def _swa_kernel(sink_ref, q_ref, k_ref, v_ref, o_ref, *, tq):
    W = WINDOW
    nsub = tq // W
    n0 = pl.program_id(1) * nsub
    lane = lax.broadcasted_iota(i32, (W, LANES), 1)
    is0 = lane < HD
    qoff = lax.broadcasted_iota(i32, (4 * W, 2 * W), 0) % W
    cols = lax.broadcasted_iota(i32, (4 * W, 2 * W), 1)
    bias_mid = jnp.where((cols - W <= qoff) & (qoff - (cols - W) < W), 0.0, NEG)
    bias_first = jnp.where(cols <= qoff, 0.0, NEG)
    rgrp = lax.broadcasted_iota(i32, (4 * W, 1), 0) // W
    for r in range(nsub):
        nb = n0 + r
        kstart = pl.multiple_of(jnp.maximum(nb * W - W, 0), W)
        ks = k_ref[0, pl.ds(kstart, 2 * W), :]
        vs = v_ref[0, pl.ds(kstart, 2 * W), :]
        bias = jnp.where(nb == 0, bias_first, bias_mid)
        outs = []
        for kv in range(2):
            keep = is0 if kv == 0 else jnp.logical_not(is0)
            parts = []
            for g in range(4):
                qg = q_ref[0, r * W:(r + 1) * W, g * LANES:(g + 1) * LANES]
                parts.append(jnp.where(keep, qg, jnp.zeros_like(qg)))
            qstack = jnp.concatenate(parts, axis=0)
            s = bias + _dot_nt(qstack, ks)
            sink = jnp.zeros((4 * W, 1), f32)
            for g in range(4):
                sink = jnp.where(rgrp == g, sink_ref[kv * 4 + g], sink)
            m = jnp.maximum(jnp.max(s, axis=1, keepdims=True), sink)
            e = jnp.exp2(s - m)
            den = jnp.sum(e, axis=1, keepdims=True) + jnp.exp2(sink - m)
            outs.append(_dot(e.astype(bf16), vs) / den)
        for g in range(4):
            og = jnp.where(is0, outs[0][g * W:(g + 1) * W], outs[1][g * W:(g + 1) * W])
            o_ref[0, r * W:(r + 1) * W, g * LANES:(g + 1) * LANES] = og.astype(bf16)


def _swa(sinks, qs, ks, vs, B, S, tq=512):
    kernel = functools.partial(_swa_kernel, tq=tq)
    return pl.pallas_call(
        kernel,
        grid=(B, S // tq),
        in_specs=[
            pl.BlockSpec(memory_space=pltpu.SMEM),
            pl.BlockSpec((1, tq, SWA_Q_W), lambda b, i: (b, i, 0)),
            pl.BlockSpec((1, S, SWA_KV_W), lambda b, i: (b, 0, 0)),
            pl.BlockSpec((1, S, SWA_KV_W), lambda b, i: (b, 0, 0)),
        ],
        out_specs=pl.BlockSpec((1, tq, SWA_Q_W), lambda b, i: (b, i, 0)),
        out_shape=jax.ShapeDtypeStruct((B, S, SWA_Q_W), bf16),
        compiler_params=_cparams(("parallel", "arbitrary")),
        name="swa",
    )(sinks, qs, ks, vs)


def _kvproj_kernel(m_ref, w_ref, k_ref, v_ref):
    mb = m_ref[...].astype(bf16)
    k_ref[...] = _dot(mb, w_ref[:, :D]).astype(bf16)
    v_ref[...] = _dot(mb, w_ref[:, D:]).astype(bf16)


def _kvproj(mem2, w_xkv, tm=512):
    R = mem2.shape[0]
    row = pl.BlockSpec((tm, D), lambda i: (i, 0))
    return pl.pallas_call(
        _kvproj_kernel, grid=(R // tm,),
        in_specs=[row, pl.BlockSpec(w_xkv.shape, lambda i: (0, 0))],
        out_specs=[row, row],
        out_shape=[jax.ShapeDtypeStruct((R, D), bf16)] * 2,
        compiler_params=_cparams(("parallel",)), name="kvproj",
    )(mem2, w_xkv)


def _mid_kernel(x_ref, of_ref, os_ref, wo_ref, g1_ref, b1_ref, wq_ref, k_ref, v_ref,
                wxo_ref, g2_ref, b2_ref, wr_ref, br_ref,
                h2_ref, bk_ref, oc_ref, *, alpha):
    tm = x_ref.shape[0]
    mix = _dot(of_ref[...], wo_ref[:FOX_W, :]) + _dot(os_ref[...], wo_ref[FOX_W:, :])
    h1 = _layer_norm(alpha * x_ref[...] + mix, g1_ref[...], b1_ref[...])

    q = (_dot(h1.astype(bf16), wq_ref[...]) * 0.0625).astype(bf16)
    for h in range(N_XH):
        sl = slice(h * XHD, (h + 1) * XHD)
        s = _dot_nt(q[:, sl], k_ref[:, sl])
        e = jnp.exp(s - jnp.max(s, axis=1, keepdims=True))
        p = e / jnp.sum(e, axis=1, keepdims=True)
        oc_ref[:, sl] = _dot(p.astype(bf16), v_ref[:, sl]).astype(bf16)
    xo = _dot(oc_ref[...], wxo_ref[...])
    h2 = _layer_norm(alpha * h1 + xo, g2_ref[...], b2_ref[...])
    h2_ref[:, :D] = h2

    hh = h2.astype(bf16)
    hl = (h2 - hh.astype(f32)).astype(bf16)
    hi_terms = _dot(hh, wr_ref[...])
    lg = hi_terms[:, :LANES] + _dot(hl, wr_ref[:, :LANES]) + hi_terms[:, LANES:] + br_ref[...]

    lgt = jnp.transpose(lg)
    row = lax.broadcasted_iota(i32, (EPG, tm), 0).astype(f32)
    big = float(EPG)

    def first_max(vals, mask):
        vm = jnp.where(mask, vals, NEG)
        top = jnp.max(vm, axis=0, keepdims=True)
        idx = jnp.min(jnp.where(mask & (vm == top), row, big), axis=0, keepdims=True)
        return top, idx

    gl = lgt[0:EPG]
    gmask = row < float(N_GROUPS)
    gmax, gidx = first_max(gl, gmask)
    g_val = 1.0 / jnp.sum(jnp.where(gmask, jnp.exp(gl - gmax), 0.0), axis=0, keepdims=True)
    sel = jnp.zeros((EPG, tm), f32)
    for g in range(N_GROUPS):
        sel = jnp.where(gidx == float(g), lgt[EPG * (g + 1):EPG * (g + 2)], sel)
    every = row >= 0.0
    v1, e1 = first_max(sel, every)
    v2, e2 = first_max(sel, row != e1)
    ex = jnp.exp(v2 - v1)
    w1 = g_val * (1.0 / (1.0 + ex))
    w2 = g_val * (ex / (1.0 + ex))
    first_low = e1 < e2
    ea = jnp.where(first_low, e1, e2)
    eb = jnp.where(first_low, e2, e1)
    ga = jnp.where(first_low, w1, w2)
    gb = jnp.where(first_low, w2, w1)
    pidx = ea * float(EPG - 1) - ea * (ea - 1.0) * 0.5 + (eb - ea - 1.0)
    bucket = gidx * float(N_PAIRS) + pidx

    bk_ref[...] = jnp.broadcast_to(bucket, (EPG, tm))
    gates = jnp.where(row == 0.0, ga, jnp.where(row == 1.0, gb, 0.0))
    gates = jnp.concatenate([gates, jnp.zeros((LANES - EPG, tm), f32)], axis=0)
    h2_ref[:, D:] = jnp.transpose(gates)


def _mid(x2, of2, os2, w_out, g1, b1, wq, kx, vx, wxo, g2, b2, wr2, br, alpha, S, tm=1024):
    T = x2.shape[0]
    M = kx.shape[0] // (T // S)
    per_b = S // tm
    row = lambda w: pl.BlockSpec((tm, w), lambda i: (i, 0))
    full = lambda a: pl.BlockSpec(a.shape, lambda i: (0,) * a.ndim)
    kvspec = pl.BlockSpec((M, D), lambda i: (i // per_b, 0))
    kernel = functools.partial(_mid_kernel, alpha=alpha)
    return pl.pallas_call(
        kernel,
        grid=(T // tm,),
        in_specs=[row(D), row(512), row(512), full(w_out), full(g1), full(b1), full(wq),
                  kvspec, kvspec, full(wxo), full(g2), full(b2), full(wr2), full(br)],
        out_specs=[row(XW), pl.BlockSpec((8, tm), lambda i: (0, i))],
        out_shape=[jax.ShapeDtypeStruct((T, XW), f32), jax.ShapeDtypeStruct((8, T), f32)],
        scratch_shapes=[pltpu.VMEM((tm, D), bf16)],
        compiler_params=_cparams(("parallel",)),
        name="mid",
    )(x2, of2, os2, w_out, g1, b1, wq, kx, vx, wxo, g2, b2, wr2, br)


def _rank_kernel(bk_ref, rank_ref, cnt_ref, carry_ref, *, chunk):
    sub = 256

    @pl.when(pl.program_id(0) == 0)
    def _():
        carry_ref[...] = jnp.zeros_like(carry_ref)

    r = lax.broadcasted_iota(i32, (sub, sub), 0)
    c = lax.broadcasted_iota(i32, (sub, sub), 1)
    before = (r < c).astype(bf16)
    bid = lax.broadcasted_iota(i32, (LANES, sub), 0).astype(f32)
    carry = carry_ref[...]
    for j in range(chunk // sub):
        bk = bk_ref[0:1, j * sub:(j + 1) * sub]
        hit = bid == bk
        oh = jnp.where(hit, 1.0, 0.0)
        prior = _dot(oh.astype(bf16), before) + carry
        rank_ref[:, j * sub:(j + 1) * sub] = jnp.sum(jnp.where(hit, prior, 0.0), axis=0, keepdims=True)
        carry = carry + jnp.sum(oh, axis=1, keepdims=True)
    carry_ref[...] = carry
    cnt_ref[...] = carry


def _rank(bk8, chunk=2048):
    T = bk8.shape[1]
    kernel = functools.partial(_rank_kernel, chunk=chunk)
    return pl.pallas_call(
        kernel, grid=(T // chunk,),
        in_specs=[pl.BlockSpec((8, chunk), lambda i: (0, i))],
        out_specs=[pl.BlockSpec((1, chunk), lambda i: (0, i)),
                   pl.BlockSpec((LANES, 1), lambda i: (0, 0))],
        out_shape=[jax.ShapeDtypeStruct((1, T), f32), jax.ShapeDtypeStruct((LANES, 1), f32)],
        scratch_shapes=[pltpu.VMEM((LANES, 1), f32)],
        compiler_params=_cparams(("arbitrary",)), name="rank",
    )(bk8)


def _dest_kernel(bk_ref, rank_ref, ps_ref, dest_ref):
    chunk = bk_ref.shape[1]
    bid = lax.broadcasted_iota(i32, (LANES, chunk), 0).astype(f32)
    start = jnp.sum(jnp.where(bid == bk_ref[0:1, :], ps_ref[...], 0.0), axis=0, keepdims=True)
    dest_ref[...] = (start + rank_ref[...]).astype(i32)


def _dest(bk8, rank, ps_col, chunk=2048):
    T = bk8.shape[1]
    return pl.pallas_call(
        _dest_kernel, grid=(T // chunk,),
        in_specs=[pl.BlockSpec((8, chunk), lambda i: (0, i)), pl.BlockSpec((1, chunk), lambda i: (0, i)),
                  pl.BlockSpec((LANES, 1), lambda i: (0, 0))],
        out_specs=pl.BlockSpec((1, chunk), lambda i: (0, i)),
        out_shape=jax.ShapeDtypeStruct((1, T), i32),
        compiler_params=_cparams(("parallel",)), name="dest",
    )(bk8, rank, ps_col)


def _sc_invert(dest, n_rows):
    T = dest.shape[0]
    assert T & (T - 1) == 0
    lanes = SC_LANES
    mesh = plsc.VectorSubcoreMesh(core_axis_name="core", subcore_axis_name="subcore",
                                  num_cores=SC_CORES, num_subcores=SC_SUBCORES)

    @functools.partial(pl.kernel, out_type=jax.ShapeDtypeStruct((n_rows,), i32), mesh=mesh,
                       scratch_types=[pltpu.VMEM((T,), i32), pltpu.VMEM((n_rows,), i32)],
                       compiler_params=pltpu.CompilerParams(needs_layout_passes=False),
                       name="sc_invert")
    def k(dest_hbm, out_hbm, dest_v, table_v):
        wid = lax.axis_index("subcore") * SC_CORES + lax.axis_index("core")

        @pl.when(wid == 0)
        def _():
            pltpu.sync_copy(dest_hbm, dest_v)
            lane = lax.iota(i32, lanes)

            @pl.loop(0, n_rows // lanes)
            def _(j):
                off = pl.multiple_of(j * lanes, lanes)
                table_v[pl.ds(off, lanes)] = (lane + off) & (T - 1)


            @pl.loop(0, T // lanes)
            def _(j):
                off = pl.multiple_of(j * lanes, lanes)
                plsc.store_scatter(table_v, [dest_v[pl.ds(off, lanes)]], lane + off)

            pltpu.sync_copy(table_v, out_hbm)

    return k(dest)


def _sc_gather_rows(idx, src, chunk=SC_GATHER_ROWS):
    n = idx.shape[0]
    w = src.shape[1]
    workers = SC_CORES * SC_SUBCORES
    per_worker = n // workers
    mesh = plsc.VectorSubcoreMesh(core_axis_name="core", subcore_axis_name="subcore",
                                  num_cores=SC_CORES, num_subcores=SC_SUBCORES)

    n_chunks = per_worker // chunk
    assert n_chunks % 2 == 0

    @functools.partial(pl.kernel, out_type=jax.ShapeDtypeStruct((n, w), src.dtype), mesh=mesh,
                       scratch_types=[pltpu.VMEM((per_worker,), i32), pltpu.VMEM((2, chunk, w), src.dtype),
                                      pltpu.SemaphoreType.DMA((2,)), pltpu.SemaphoreType.DMA((2,))],
                       name="sc_gather_rows")
    def k(src_hbm, idx_hbm, out_hbm, idx_v, rows_v, gsem, wsem):
        wid = lax.axis_index("subcore") * SC_CORES + lax.axis_index("core")
        base = wid * per_worker
        pltpu.sync_copy(idx_hbm.at[pl.ds(base, per_worker)], idx_v)

        def gather(c, slot):
            rows = idx_v.at[pl.ds(pl.multiple_of(c * chunk, chunk), chunk)]
            return pltpu.make_async_copy(src_hbm.at[rows], rows_v.at[slot], gsem.at[slot])

        def write(c, slot):
            out = out_hbm.at[pl.ds(pl.multiple_of(base + c * chunk, chunk), chunk)]
            return pltpu.make_async_copy(rows_v.at[slot], out, wsem.at[slot])

        gather(0, 0).start()

        @pl.loop(0, n_chunks // 2)
        def _(pair):
            c = 2 * pair

            @pl.when(pair > 0)
            def _():
                write(c - 1, 1).wait()

            gather(c + 1, 1).start()
            gather(c, 0).wait()
            write(c, 0).start()

            @pl.when(c + 2 < n_chunks)
            def _():
                write(c, 0).wait()
                gather(c + 2, 0).start()

            gather(c + 1, 1).wait()
            write(c + 1, 1).start()

        write(n_chunks - 2, 0).wait()
        write(n_chunks - 1, 1).wait()

    return k(src, idx)


def _expert_kernel(grp_ref, ea_ref, eb_ref, used_ref, xs_ref, wg_ref, wu_ref, wd_ref, g_ref, b_ref, y_ref,
                   *, alpha):
    del grp_ref
    n = pl.program_id(0)

    @pl.when(n < used_ref[0])
    def _():
        h2 = xs_ref[:, :D]
        x = h2.astype(bf16)

        def expert(e):
            a = _dot(x, wg_ref[0, e])
            u = _dot(x, wu_ref[0, e])
            act = a * (1.0 / (1.0 + jnp.exp(-a))) * u
            return _dot(act.astype(bf16), wd_ref[0, e])

        ga = xs_ref[:, D:D + 1]
        gb = xs_ref[:, D + 1:D + 2]
        moe = ga * expert(ea_ref[n]) + gb * expert(eb_ref[n])
        y_ref[...] = _layer_norm(alpha * h2 + moe, g_ref[...], b_ref[...])

    @pl.when(n >= used_ref[0])
    def _():
        y_ref[...] = jnp.zeros_like(y_ref)


def _expert_kernel_into(grp_ref, ea_ref, eb_ref, used_ref, xs_ref, wg_ref, wu_ref, wd_ref, g_ref, b_ref,
                        ys_ref, y_ref, *, alpha):
    del ys_ref
    _expert_kernel(grp_ref, ea_ref, eb_ref, used_ref, xs_ref, wg_ref, wu_ref, wd_ref, g_ref, b_ref, y_ref,
                   alpha=alpha)


def _experts(grp, ea, eb, used, xs, wg, wu, wd, ln_g, ln_b, alpha, ys, first_block, total_rows):
    nblk = xs.shape[0] // ROW_BLOCK

    def xmap(n, grp, ea, eb, used):
        return (jnp.maximum(jnp.minimum(n, used[0] - 1), 0), 0)

    gmap = lambda n, grp, ea, eb, used: (grp[n], 0, 0, 0)
    gspec = lambda w: pl.BlockSpec((1,) + w.shape[1:], gmap, pipeline_mode=pl.Buffered(1))
    vec = pl.BlockSpec((1, D), lambda n, grp, ea, eb, used: (0, 0))
    in_specs = [pl.BlockSpec((ROW_BLOCK, XW), xmap), gspec(wg), gspec(wu), gspec(wd), vec, vec]
    operands = [grp, ea, eb, used, xs, wg, wu, wd, ln_g, ln_b]
    aliases = {}
    body = _expert_kernel
    if ys is not None:
        in_specs.append(pl.BlockSpec(memory_space=pl.ANY))
        aliases = {len(operands): 0}
        operands.append(ys)
        body = _expert_kernel_into
    grid_spec = pltpu.PrefetchScalarGridSpec(
        num_scalar_prefetch=4, grid=(nblk,), in_specs=in_specs,
        out_specs=pl.BlockSpec((ROW_BLOCK, D), lambda n, grp, ea, eb, used: (n + first_block, 0)),
    )
    return pl.pallas_call(
        functools.partial(body, alpha=alpha), grid_spec=grid_spec,
        out_shape=jax.ShapeDtypeStruct((total_rows, D), f32),
        input_output_aliases=aliases,
        compiler_params=_cparams(("arbitrary",)), name="experts",
    )(*operands)


def _pair_tables():
    ea = np.zeros((LANES,), np.int32)
    eb = np.zeros((LANES,), np.int32)
    for g in range(N_GROUPS):
        k = 0
        for a in range(EPG):
            for b in range(a + 1, EPG):
                ea[g * N_PAIRS + k] = a
                eb[g * N_PAIRS + k] = b
                k += 1
    return ea, eb


_PAIR_A, _PAIR_B = _pair_tables()


def _layer(h, mem, positions, w_in, b_forget, sinks, w_mix_out, ln_mix_g, ln_mix_b,
           w_xq, w_xkv, w_xout, ln_x_g, ln_x_b, w_rg, b_rg, w_re, b_re,
           w_eg, w_eu, w_ed, ln_f_g, ln_f_b, alpha):
    B, S, _ = h.shape
    T = B * S
    x2 = h.reshape(T, D)

    o = np.cumsum((0, FOX_W, FOX_W, FOX_W, N_FOX, SWA_Q_W, SWA_KV_W, SWA_KV_W))
    w_qf, w_kf, w_vf, w_fl, w_qs, w_ks, w_vs = (w_in[:, o[i]:o[i + 1]] for i in range(7))
    def regroup(a, axis):
        shp = a.shape
        a = jnp.moveaxis(a, axis, 0).reshape(N_SWA_KV, N_SWA // N_SWA_KV, HD, -1)
        return jnp.moveaxis(jnp.swapaxes(a, 0, 1).reshape(N_SWA * HD, -1), 0, axis).reshape(shp)

    w_all = jnp.concatenate([w_kf, regroup(w_qs, 1), w_ks, w_vs], axis=1).astype(bf16)
    wqt = w_qf.T.astype(bf16)
    wvt = w_vf.T.astype(bf16)
    wfl = w_fl.T.astype(bf16)
    bfc = b_forget.reshape(N_FOX, 1).astype(f32)
    half = HD // 2
    inv_freq = ROPE_THETA ** (-jnp.arange(half, dtype=f32) / half)
    per_row = LANES // half
    invf = jnp.tile(inv_freq, per_row).reshape(1, LANES)
    pos4 = jnp.repeat(positions.reshape(T // per_row, per_row).astype(i32), half, axis=1)
    cos, sin = (t.reshape(T, half) for t in _rope_table(pos4, invf))
    w_out = jnp.concatenate([w_mix_out[:FOX_W], regroup(w_mix_out[FOX_W:], 0)], axis=0).astype(bf16)

    tq = 512
    (qt, kf, vt, qs, ks, vs, lf, qn, kn), (eg16, eu16, ed16) = _in_proj(
        x2, cos, sin, w_all, wqt, wvt, wfl, bfc, w_eg, w_eu, w_ed, tq)
    c, ca = _cumsum(lf, S)
    c4 = c.reshape(N_FOX // 2, 2, T // tq, tq)
    r3 = lambda a: a.reshape(B, S, a.shape[-1])
    o_fox = _fox(_fox_first_blocks(qn, kn, c, B, S, tq), qt, r3(kf), r3(ca), vt, c4, B, S, tq)
    o_swa = _swa(sinks.astype(f32) * LOG2E, r3(qs), r3(ks), r3(vs), B, S)

    kx, vx = _kvproj(mem.reshape(-1, D), w_xkv.astype(bf16))

    gpad = EPG - N_GROUPS
    wr = jnp.concatenate([jnp.pad(w_rg, ((0, 0), (0, gpad))),
                          jnp.transpose(w_re, (1, 0, 2)).reshape(D, N_EXPERTS)], axis=1)
    wr = jnp.pad(wr, ((0, 0), (0, LANES - wr.shape[1]))).astype(f32)
    wrh = wr.astype(bf16)
    wr2 = jnp.concatenate([wrh, (wr - wrh.astype(f32)).astype(bf16)], axis=1)
    br = jnp.pad(jnp.concatenate([jnp.pad(b_rg, (0, gpad)), b_re.reshape(-1)]), (0, LANES - EPG - N_EXPERTS))
    br = br.reshape(1, LANES).astype(f32)
    v2 = lambda a: a.reshape(1, D).astype(f32)
    h2x, bk8 = _mid(x2, o_fox.reshape(T, FOX_W), o_swa.reshape(T, SWA_Q_W), w_out,
                    v2(ln_mix_g), v2(ln_mix_b), w_xq.astype(bf16), kx, vx, w_xout.astype(bf16),
                    v2(ln_x_g), v2(ln_x_b), wr2, br, alpha, S)

    rank, cnt = _rank(bk8)
    counts = cnt[:, 0].astype(i32)
    padded = ((counts + ROW_BLOCK - 1) // ROW_BLOCK) * ROW_BLOCK
    pad_end = jnp.cumsum(padded)
    pad_start = (pad_end - padded).astype(i32)
    step = np.lcm(MOE_CHUNKS * SC_CORES * SC_SUBCORES * 2 * SC_GATHER_ROWS, MOE_CHUNKS * ROW_BLOCK)
    P = int(-(-(T + N_BUCKETS * ROW_BLOCK) // step) * step)
    nblk = P // ROW_BLOCK
    used = (pad_end[-1] // ROW_BLOCK).astype(i32).reshape(1)
    blk_row = jnp.arange(nblk, dtype=i32)[:, None] * ROW_BLOCK
    blk_bucket = jnp.minimum(jnp.sum((pad_end[None, :] <= blk_row).astype(i32), axis=1), N_BUCKETS - 1)
    pick = (blk_bucket[:, None] == jnp.arange(LANES, dtype=i32)[None, :]).astype(i32)
    blk_a = jnp.sum(pick * jnp.asarray(_PAIR_A)[None, :], axis=1)
    blk_b = jnp.sum(pick * jnp.asarray(_PAIR_B)[None, :], axis=1)
    blk_g = blk_bucket // N_PAIRS
    by_group = lambda w: w.reshape((N_GROUPS, EPG) + w.shape[1:])

    dest = _dest(bk8, rank, pad_start.astype(f32).reshape(LANES, 1))[0]
    row_tok = _sc_invert(dest, P)
    cblk = nblk // MOE_CHUNKS
    ys = None
    for cidx in range(MOE_CHUNKS):
        lo = cidx * cblk
        xs = _sc_gather_rows(row_tok[lo * ROW_BLOCK:(lo + cblk) * ROW_BLOCK], h2x)
        used_c = jnp.clip(used - lo, 0, cblk)
        ys = _experts(blk_g[lo:lo + cblk], blk_a[lo:lo + cblk], blk_b[lo:lo + cblk], used_c, xs,
                      by_group(eg16), by_group(eu16), by_group(ed16), v2(ln_f_g), v2(ln_f_b), alpha,
                      ys, lo, P)
    return _sc_gather_rows(dest, ys).reshape(B, S, D)


def kernel(x, mem, positions, w_in, b_forget, sinks, w_mix_out, ln_mix_g, ln_mix_b, w_xq, w_xkv, w_xout,
           ln_x_g, ln_x_b, w_route_group, b_route_group, w_route_expert, b_route_expert,
           w_exp_gate, w_exp_up, w_exp_down, ln_ffn_g, ln_ffn_b):
    depth = w_in.shape[0]
    alpha = (2.0 * depth) ** 0.25
    h = x
    for l in range(depth):
        h = _layer(h, mem, positions, w_in[l], b_forget[l], sinks[l], w_mix_out[l], ln_mix_g[l], ln_mix_b[l],
                   w_xq[l], w_xkv[l], w_xout[l], ln_x_g[l], ln_x_b[l], w_route_group[l], b_route_group[l],
                   w_route_expert[l], b_route_expert[l], w_exp_gate[l], w_exp_up[l], w_exp_down[l],
                   ln_ffn_g[l], ln_ffn_b[l], alpha)
    return h
```

```python
import functools

import jax
import jax.numpy as jnp
import numpy as np
from jax import lax
from jax.experimental import pallas as pl
from jax.experimental.pallas import tpu as pltpu
from jax.experimental.pallas import tpu_sc as plsc

f32 = jnp.float32
bf16 = jnp.bfloat16
i32 = jnp.int32

D = 1024
HD = 64
N_FOX = 8
N_SWA = 8
N_SWA_KV = 2
FOX_W = 512
SWA_Q_W = 512
SWA_KV_W = 128
WINDOW = 128
ROPE_THETA = 10000.0
N_XH = 4
XHD = 256
N_GROUPS = 4
EPG = 8
N_EXPERTS = 32
D_EXPERT = 512
LN_EPS = 1e-5
NEG = -1e30
LOG2E = 1.4426950408889634
L_ROW = (HD, 0)
SKIP_LOG2 = 160.0

SC_CORES = 2
SC_SUBCORES = 16
SC_LANES = 16
SC_GATHER_ROWS = 32
MOE_CHUNKS = 4
LANES = 128
ROW_BLOCK = 128
N_PAIRS = EPG * (EPG - 1) // 2
N_BUCKETS = N_GROUPS * N_PAIRS
XW = D + LANES
VMEM_LIMIT = 56 * 1024 * 1024


def _cparams(sem):
    return pltpu.CompilerParams(dimension_semantics=sem, vmem_limit_bytes=VMEM_LIMIT)


def _layer_norm(v, g, b):
    mu = jnp.mean(v, axis=-1, keepdims=True)
    c = v - mu
    var = jnp.mean(c * c, axis=-1, keepdims=True)
    return c * lax.rsqrt(var + LN_EPS) * g + b


def _dot(a, b):
    return jnp.dot(a, b, preferred_element_type=f32)


def _dot_nt(a, b):
    return lax.dot_general(a, b, (((1,), (1,)), ((), ())), preferred_element_type=f32)


def _rope_table_kernel(pos_ref, invf_ref, cos_ref, sin_ref):
    ang = pos_ref[...].astype(f32) * invf_ref[...]
    cos_ref[...] = jnp.cos(ang)
    sin_ref[...] = jnp.sin(ang)


def _rope_table(pos4, invf, rows=1024):
    R = pos4.shape[0]
    blk = pl.BlockSpec((rows, LANES), lambda i: (i, 0))
    return pl.pallas_call(
        _rope_table_kernel, grid=(R // rows,),
        in_specs=[blk, pl.BlockSpec((1, LANES), lambda i: (0, 0))], out_specs=[blk, blk],
        out_shape=[jax.ShapeDtypeStruct((R, LANES), f32)] * 2,
        compiler_params=_cparams(("parallel",)), name="rope_table",
    )(pos4, invf)


def _inproj_kernel(x_ref, cos_ref, sin_ref, w_ref, wqt_ref, wvt_ref, wfl_ref, bf_ref, ind_ref,
                   eg_ref, eu_ref, ed_ref,
                   qt_ref, kf_ref, vt_ref, qs_ref, ks_ref, vs_ref, lf_ref, qn_ref, kn_ref,
                   egb_ref, eub_ref, edb_ref):
    tm = x_ref.shape[0]
    xb = x_ref[...].astype(bf16)
    egb_ref[...] = eg_ref[...].astype(bf16)
    eub_ref[...] = eu_ref[...].astype(bf16)
    edb_ref[...] = ed_ref[...].astype(bf16)

    def proj(lo, hi):
        return _dot(xb, w_ref[:, lo:hi])

    qv = _dot_nt(wqt_ref[...], xb) * (0.125 * LOG2E)
    qt_ref[0] = qv.astype(bf16)
    vt_ref[0] = _dot_nt(wvt_ref[...], xb).astype(bf16)
    kv = proj(0, 512)
    kf_ref[...] = kv.astype(bf16)
    q2 = jnp.sum((qv * qv).reshape(N_FOX, HD, tm), axis=1)
    qn_ref[0] = jnp.broadcast_to(jnp.max(q2, axis=1, keepdims=True), (N_FOX, LANES))
    k2 = _dot((kv * kv).astype(bf16), ind_ref[...])
    kn_ref[0] = jnp.broadcast_to(jnp.max(k2, axis=0, keepdims=True), (N_FOX, LANES))

    reps = LANES // cos_ref.shape[1]
    cos = jnp.concatenate([cos_ref[...]] * reps, axis=1)
    sin = jnp.concatenate([sin_ref[...]] * reps, axis=1)
    lane = lax.broadcasted_iota(i32, (tm, LANES), 1)
    lo_half = (lane % HD) < (HD // 2)
    sin_s = jnp.where(lo_half, -sin, sin)

    def rope(z):
        rot = jnp.where(lo_half, pltpu.roll(z, LANES - HD // 2, 1), pltpu.roll(z, HD // 2, 1))
        return z * cos + rot * sin_s

    zq = proj(512, 1024)
    for g in range(4):
        sl = slice(g * LANES, (g + 1) * LANES)
        qs_ref[:, sl] = (rope(zq[:, sl]) * (0.125 * LOG2E)).astype(bf16)
    ks_ref[...] = rope(proj(1024, 1152)).astype(bf16)
    vs_ref[...] = proj(1152, 1280).astype(bf16)

    z = _dot_nt(wfl_ref[...], xb) + bf_ref[...]
    lf_ref[...] = jnp.minimum(z, 0.0) - jnp.log(1.0 + jnp.exp(-jnp.abs(z)))


def _in_proj(x2, cos, sin, w_all, wqt, wvt, wfl, bfc, w_eg, w_eu, w_ed, tm):
    T = x2.shape[0]
    steps = T // tm
    row = lambda w: pl.BlockSpec((tm, w), lambda i: (i, 0))
    full = lambda a: pl.BlockSpec(a.shape, lambda i: (0,) * a.ndim)
    fmaj = pl.BlockSpec((1, FOX_W, tm), lambda i: (i, 0, 0))
    flat = [w.reshape(-1, w.shape[-1]) for w in (w_eg, w_eu, w_ed)]
    slices = [pl.BlockSpec((w.shape[0] // steps, w.shape[1]), lambda i: (i, 0)) for w in flat]
    ind = jnp.asarray(np.arange(FOX_W)[:, None] // HD == np.arange(LANES)[None, :], bf16)
    nrm = pl.BlockSpec((1, N_FOX, LANES), lambda i: (i, 0, 0))
    outs = pl.pallas_call(
        _inproj_kernel,
        grid=(steps,),
        in_specs=[row(D), row(cos.shape[1]), row(sin.shape[1]), full(w_all), full(wqt), full(wvt), full(wfl),
                  full(bfc), full(ind)] + slices,
        out_specs=[fmaj, row(512), fmaj, row(512), row(128), row(128),
                   pl.BlockSpec((N_FOX, tm), lambda i: (0, i)), nrm, nrm] + slices,
        out_shape=[jax.ShapeDtypeStruct((steps, FOX_W, tm), bf16), jax.ShapeDtypeStruct((T, 512), bf16),
                   jax.ShapeDtypeStruct((steps, FOX_W, tm), bf16), jax.ShapeDtypeStruct((T, 512), bf16),
                   jax.ShapeDtypeStruct((T, 128), bf16), jax.ShapeDtypeStruct((T, 128), bf16),
                   jax.ShapeDtypeStruct((N_FOX, T), f32),
                   jax.ShapeDtypeStruct((steps, N_FOX, LANES), f32), jax.ShapeDtypeStruct((steps, N_FOX, LANES), f32)]
        + [jax.ShapeDtypeStruct(w.shape, bf16) for w in flat],
        compiler_params=_cparams(("parallel",)),
        name="in_proj",
    )(x2, cos, sin, w_all, wqt, wvt, wfl, bfc, ind, *flat)
    experts_bf16 = [o.reshape(w.shape) for o, w in zip(outs[9:], (w_eg, w_eu, w_ed))]
    return outs[:9], experts_bf16


def _cumsum_kernel(lf_ref, c_ref, ca_ref):
    S = lf_ref.shape[1]
    ch = 256
    r = lax.broadcasted_iota(i32, (ch, ch), 0)
    c = lax.broadcasted_iota(i32, (ch, ch), 1)
    tri = (r <= c).astype(f32)
    eye = (r == c).astype(bf16)
    stacked = jnp.concatenate([lf_ref[:, j * ch:(j + 1) * ch] for j in range(S // ch)], axis=0)
    local = jnp.dot(stacked, tri, precision=lax.Precision.HIGHEST, preferred_element_type=f32)
    carry = jnp.zeros((N_FOX, 1), f32)
    for j in range(S // ch):
        cc = local[j * N_FOX:(j + 1) * N_FOX] + carry
        carry = cc[:, ch - 1:ch]
        c2 = cc * LOG2E
        c_ref[:, j * ch:(j + 1) * ch] = c2
        neg = -c2
        hi = neg.astype(bf16)
        r1 = neg - hi.astype(f32)
        mid = r1.astype(bf16)
        lo = (r1 - mid.astype(f32)).astype(bf16)
        terms = jnp.concatenate([hi, mid, lo, jnp.zeros((LANES - 3 * N_FOX, ch), bf16)], axis=0)
        ca_ref[j * ch:(j + 1) * ch, :] = _dot_nt(eye, terms).astype(bf16)


def _cumsum(lf, S):
    T = lf.shape[1]
    spec = pl.BlockSpec((N_FOX, S), lambda b: (0, b))
    return pl.pallas_call(
        _cumsum_kernel, grid=(T // S,), in_specs=[spec],
        out_specs=[spec, pl.BlockSpec((S, LANES), lambda b: (b, 0))],
        out_shape=[jax.ShapeDtypeStruct((N_FOX, T), f32), jax.ShapeDtypeStruct((T, LANES), bf16)],
        compiler_params=_cparams(("parallel",)), name="cumsum",
    )(lf)


def _fox_kernel(j0_ref, qt_ref, k_ref, ca_ref, vt_ref, c_ref, o_ref, t0_ref, t1_ref, *, tq):
    hp = pl.program_id(1)
    i = pl.program_id(2)
    qt = qt_ref[0]
    row = lax.broadcasted_iota(i32, (LANES, tq), 0)
    is_a = row < HD
    zero = jnp.zeros_like(qt)
    q_ops = []
    for h in range(2):
        ones = jnp.where(((row & 7) == 2 * hp + h) & (row < 3 * N_FOX), 1.0, 0.0).astype(bf16)
        qh = jnp.where(is_a, qt, zero) if h == 0 else jnp.where(is_a, zero, qt)
        q_ops.append(jnp.concatenate([qh, ones], axis=0))
    kr = lax.broadcasted_iota(i32, (tq, tq), 0)
    qc = lax.broadcasted_iota(i32, (tq, tq), 1)
    causal = kr <= qc
    cq = [c_ref[0, h, pl.ds(i, 1), :] for h in range(2)]

    def scores(j, t_ref):
        off = pl.multiple_of(j * tq, tq)
        kblk = jnp.concatenate([k_ref[0, pl.ds(off, tq), :], ca_ref[0, pl.ds(off, tq), :]], axis=1)
        for h in range(2):
            t_ref[h] = _dot(kblk, q_ops[h])

    keep = [jnp.where(is_a, 1.0, 0.0).astype(bf16), jnp.where(is_a, 0.0, 1.0).astype(bf16)]
    ones_row = [jnp.where(row == L_ROW[h], 1.0, 0.0).astype(bf16) for h in range(2)]

    def softmax_pv(j, t_ref, carry, masked):
        vt = vt_ref[j]
        vts = [vt * keep[h] + ones_row[h] for h in range(2)]
        new = []
        for h in range(2):
            m, acc = carry[h]
            t = t_ref[h]
            if masked:
                t = jnp.where(causal, t, NEG)
            m_new = jnp.maximum(m, jnp.max(t, axis=0, keepdims=True) + cq[h])
            alpha = jnp.exp2(m - m_new)
            p = jnp.exp2(t + (cq[h] - m_new))
            acc = alpha * acc + _dot(vts[h], p.astype(bf16))
            new.append((m_new, acc))
        return tuple(new)

    j0 = j0_ref[(pl.program_id(0) * pl.num_programs(1) + hp) * pl.num_programs(2) + i]
    n_full = i - j0

    def pair(k, carry):
        j = j0 + 2 * k
        scores(j + 1, t1_ref)
        carry = softmax_pv(j, t0_ref, carry, False)
        scores(j + 2, t0_ref)
        return softmax_pv(j + 1, t1_ref, carry, False)

    def odd_tail(carry):
        scores(i, t1_ref)
        carry = softmax_pv(i - 1, t0_ref, carry, False)
        return softmax_pv(i, t1_ref, carry, True)

    def even_tail(carry):
        return softmax_pv(i, t0_ref, carry, True)

    init = tuple((jnp.full((1, tq), NEG, f32), jnp.zeros((LANES, tq), f32)) for _ in range(2))
    scores(j0, t0_ref)
    carry = lax.fori_loop(0, n_full // 2, pair, init)
    (_, acca), (_, accb) = lax.cond(n_full % 2 == 1, odd_tail, even_tail, carry)
    la = acca[L_ROW[0]:L_ROW[0] + 1, :]
    lb = accb[L_ROW[1]:L_ROW[1] + 1, :]
    ot = jnp.where(is_a, acca / la, accb / lb)
    o_ref[0] = jnp.transpose(ot).astype(bf16)


def _fox_first_blocks(qn, kn, c, B, S, tq):
    nq = S // tq
    qmax = jnp.sqrt(qn[:, :, 0].reshape(B, nq, N_FOX))
    kmax = jnp.sqrt(kn[:, 0, :N_FOX].reshape(B, nq, N_FOX))
    qk = 1.02 * qmax[:, :, None, :] * (kmax[:, None, :, :] + kmax[:, :, None, :])
    cb = c.reshape(N_FOX, B, nq, tq)
    c_first = jnp.transpose(cb[..., 0], (1, 2, 0))
    c_last = jnp.transpose(cb[..., tq - 1], (1, 2, 0))
    gap = c_last[:, None, :, :] - c_first[:, :, None, :]
    blk = jnp.arange(nq)
    negligible = (qk - gap < -SKIP_LOG2) & (blk[None, :] < blk[:, None])[None, :, :, None]
    pair = jnp.all(negligible.reshape(B, nq, nq, N_FOX // 2, 2), axis=-1)
    j0 = jnp.sum(jnp.cumprod(pair.astype(i32), axis=2), axis=2)
    return jnp.transpose(j0, (0, 2, 1)).reshape(-1).astype(i32)


def _fox(j0, qt, kf, ca, vt, c4, B, S, tq):
    nq = S // tq
    kernel = functools.partial(_fox_kernel, tq=tq)
    return pl.pallas_call(
        kernel,
        grid=(B, N_FOX // 2, nq),
        in_specs=[
            pl.BlockSpec(memory_space=pltpu.SMEM),
            pl.BlockSpec((1, LANES, tq), lambda b, hp, i: (b * nq + i, hp, 0)),
            pl.BlockSpec((1, S, LANES), lambda b, hp, i: (b, 0, hp)),
            pl.BlockSpec((1, S, LANES), lambda b, hp, i: (b, 0, 0)),
            pl.BlockSpec((nq, LANES, tq), lambda b, hp, i: (b, hp, 0)),
            pl.BlockSpec((1, 2, nq, tq), lambda b, hp, i: (hp, 0, b, 0)),
        ],
        out_specs=pl.BlockSpec((1, tq, LANES), lambda b, hp, i: (b, i, hp)),
        out_shape=jax.ShapeDtypeStruct((B, S, FOX_W), bf16),
        scratch_shapes=[pltpu.VMEM((2, tq, tq), f32), pltpu.VMEM((2, tq, tq), f32)],
        compiler_params=_cparams(("parallel", "parallel", "arbitrary")),
        name="fox",
    )(j0, qt, kf, ca, vt, c4)


def _swa_kernel(sink_ref, q_ref, k_ref, v_ref, o_ref, *, tq):
    W = WINDOW
    nsub = tq // W
    n0 = pl.program_id(1) * nsub
    lane = lax.broadcasted_iota(i32, (W, LANES), 1)
    is0 = lane < HD
    qoff = lax.broadcasted_iota(i32, (4 * W, 2 * W), 0) % W
    cols = lax.broadcasted_iota(i32, (4 * W, 2 * W), 1)
    bias_mid = jnp.where((cols - W <= qoff) & (qoff - (cols - W) < W), 0.0, NEG)
    bias_first = jnp.where(cols <= qoff, 0.0, NEG)
    rgrp = lax.broadcasted_iota(i32, (4 * W, 1), 0) // W
    for r in range(nsub):
        nb = n0 + r
        kstart = pl.multiple_of(jnp.maximum(nb * W - W, 0), W)
        ks = k_ref[0, pl.ds(kstart, 2 * W), :]
        vs = v_ref[0, pl.ds(kstart, 2 * W), :]
        bias = jnp.where(nb == 0, bias_first, bias_mid)
        outs = []
        for kv in range(2):
            keep = is0 if kv == 0 else jnp.logical_not(is0)
            parts = []
            for g in range(4):
                qg = q_ref[0, r * W:(r + 1) * W, g * LANES:(g + 1) * LANES]
                parts.append(jnp.where(keep, qg, jnp.zeros_like(qg)))
            qstack = jnp.concatenate(parts, axis=0)
            s = bias + _dot_nt(qstack, ks)
            sink = jnp.zeros((4 * W, 1), f32)
            for g in range(4):
                sink = jnp.where(rgrp == g, sink_ref[kv * 4 + g], sink)
            m = jnp.maximum(jnp.max(s, axis=1, keepdims=True), sink)
            e = jnp.exp2(s - m)
            den = jnp.sum(e, axis=1, keepdims=True) + jnp.exp2(sink - m)
            outs.append(_dot(e.astype(bf16), vs) / den)
        for g in range(4):
            og = jnp.where(is0, outs[0][g * W:(g + 1) * W], outs[1][g * W:(g + 1) * W])
            o_ref[0, r * W:(r + 1) * W, g * LANES:(g + 1) * LANES] = og.astype(bf16)


def _swa(sinks, qs, ks, vs, B, S, tq=512):
    kernel = functools.partial(_swa_kernel, tq=tq)
    return pl.pallas_call(
        kernel,
        grid=(B, S // tq),
        in_specs=[
            pl.BlockSpec(memory_space=pltpu.SMEM),
            pl.BlockSpec((1, tq, SWA_Q_W), lambda b, i: (b, i, 0)),
            pl.BlockSpec((1, S, SWA_KV_W), lambda b, i: (b, 0, 0)),
            pl.BlockSpec((1, S, SWA_KV_W), lambda b, i: (b, 0, 0)),
        ],
        out_specs=pl.BlockSpec((1, tq, SWA_Q_W), lambda b, i: (b, i, 0)),
        out_shape=jax.ShapeDtypeStruct((B, S, SWA_Q_W), bf16),
        compiler_params=_cparams(("parallel", "arbitrary")),
        name="swa",
    )(sinks, qs, ks, vs)


def _kvproj_kernel(m_ref, w_ref, k_ref, v_ref):
    mb = m_ref[...].astype(bf16)
    k_ref[...] = _dot(mb, w_ref[:, :D]).astype(bf16)
    v_ref[...] = _dot(mb, w_ref[:, D:]).astype(bf16)


def _kvproj(mem2, w_xkv, tm=512):
    R = mem2.shape[0]
    row = pl.BlockSpec((tm, D), lambda i: (i, 0))
    return pl.pallas_call(
        _kvproj_kernel, grid=(R // tm,),
        in_specs=[row, pl.BlockSpec(w_xkv.shape, lambda i: (0, 0))],
        out_specs=[row, row],
        out_shape=[jax.ShapeDtypeStruct((R, D), bf16)] * 2,
        compiler_params=_cparams(("parallel",)), name="kvproj",
    )(mem2, w_xkv)


def _mid_kernel(x_ref, of_ref, os_ref, wo_ref, g1_ref, b1_ref, wq_ref, k_ref, v_ref,
                wxo_ref, g2_ref, b2_ref, wr_ref, br_ref,
                h2_ref, bk_ref, oc_ref, *, alpha):
    tm = x_ref.shape[0]
    mix = _dot(of_ref[...], wo_ref[:FOX_W, :]) + _dot(os_ref[...], wo_ref[FOX_W:, :])
    h1 = _layer_norm(alpha * x_ref[...] + mix, g1_ref[...], b1_ref[...])

    q = (_dot(h1.astype(bf16), wq_ref[...]) * 0.0625).astype(bf16)
    for h in range(N_XH):
        sl = slice(h * XHD, (h + 1) * XHD)
        s = _dot_nt(q[:, sl], k_ref[:, sl])
        e = jnp.exp(s - jnp.max(s, axis=1, keepdims=True))
        p = e / jnp.sum(e, axis=1, keepdims=True)
        oc_ref[:, sl] = _dot(p.astype(bf16), v_ref[:, sl]).astype(bf16)
    xo = _dot(oc_ref[...], wxo_ref[...])
    h2 = _layer_norm(alpha * h1 + xo, g2_ref[...], b2_ref[...])
    h2_ref[:, :D] = h2

    hh = h2.astype(bf16)
    hl = (h2 - hh.astype(f32)).astype(bf16)
    hi_terms = _dot(hh, wr_ref[...])
    lg = hi_terms[:, :LANES] + _dot(hl, wr_ref[:, :LANES]) + hi_terms[:, LANES:] + br_ref[...]

    lgt = jnp.transpose(lg)
    row = lax.broadcasted_iota(i32, (EPG, tm), 0).astype(f32)
    big = float(EPG)

    def first_max(vals, mask):
        vm = jnp.where(mask, vals, NEG)
        top = jnp.max(vm, axis=0, keepdims=True)
        idx = jnp.min(jnp.where(mask & (vm == top), row, big), axis=0, keepdims=True)
        return top, idx

    gl = lgt[0:EPG]
    gmask = row < float(N_GROUPS)
    gmax, gidx = first_max(gl, gmask)
    g_val = 1.0 / jnp.sum(jnp.where(gmask, jnp.exp(gl - gmax), 0.0), axis=0, keepdims=True)
    sel = jnp.zeros((EPG, tm), f32)
    for g in range(N_GROUPS):
        sel = jnp.where(gidx == float(g), lgt[EPG * (g + 1):EPG * (g + 2)], sel)
    every = row >= 0.0
    v1, e1 = first_max(sel, every)
    v2, e2 = first_max(sel, row != e1)
    ex = jnp.exp(v2 - v1)
    w1 = g_val * (1.0 / (1.0 + ex))
    w2 = g_val * (ex / (1.0 + ex))
    first_low = e1 < e2
    ea = jnp.where(first_low, e1, e2)
    eb = jnp.where(first_low, e2, e1)
    ga = jnp.where(first_low, w1, w2)
    gb = jnp.where(first_low, w2, w1)
    pidx = ea * float(EPG - 1) - ea * (ea - 1.0) * 0.5 + (eb - ea - 1.0)
    bucket = gidx * float(N_PAIRS) + pidx

    bk_ref[...] = jnp.broadcast_to(bucket, (EPG, tm))
    gates = jnp.where(row == 0.0, ga, jnp.where(row == 1.0, gb, 0.0))
    gates = jnp.concatenate([gates, jnp.zeros((LANES - EPG, tm), f32)], axis=0)
    h2_ref[:, D:] = jnp.transpose(gates)


def _mid(x2, of2, os2, w_out, g1, b1, wq, kx, vx, wxo, g2, b2, wr2, br, alpha, S, tm=1024):
    T = x2.shape[0]
    M = kx.shape[0] // (T // S)
    per_b = S // tm
    row = lambda w: pl.BlockSpec((tm, w), lambda i: (i, 0))
    full = lambda a: pl.BlockSpec(a.shape, lambda i: (0,) * a.ndim)
    kvspec = pl.BlockSpec((M, D), lambda i: (i // per_b, 0))
    kernel = functools.partial(_mid_kernel, alpha=alpha)
    return pl.pallas_call(
        kernel,
        grid=(T // tm,),
        in_specs=[row(D), row(512), row(512), full(w_out), full(g1), full(b1), full(wq),
                  kvspec, kvspec, full(wxo), full(g2), full(b2), full(wr2), full(br)],
        out_specs=[row(XW), pl.BlockSpec((8, tm), lambda i: (0, i))],
        out_shape=[jax.ShapeDtypeStruct((T, XW), f32), jax.ShapeDtypeStruct((8, T), f32)],
        scratch_shapes=[pltpu.VMEM((tm, D), bf16)],
        compiler_params=_cparams(("parallel",)),
        name="mid",
    )(x2, of2, os2, w_out, g1, b1, wq, kx, vx, wxo, g2, b2, wr2, br)


def _rank_kernel(bk_ref, rank_ref, cnt_ref, carry_ref, *, chunk):
    sub = 256

    @pl.when(pl.program_id(0) == 0)
    def _():
        carry_ref[...] = jnp.zeros_like(carry_ref)

    r = lax.broadcasted_iota(i32, (sub, sub), 0)
    c = lax.broadcasted_iota(i32, (sub, sub), 1)
    before = (r < c).astype(bf16)
    bid = lax.broadcasted_iota(i32, (LANES, sub), 0).astype(f32)
    carry = carry_ref[...]
    for j in range(chunk // sub):
        bk = bk_ref[0:1, j * sub:(j + 1) * sub]
        hit = bid == bk
        oh = jnp.where(hit, 1.0, 0.0)
        prior = _dot(oh.astype(bf16), before) + carry
        rank_ref[:, j * sub:(j + 1) * sub] = jnp.sum(jnp.where(hit, prior, 0.0), axis=0, keepdims=True)
        carry = carry + jnp.sum(oh, axis=1, keepdims=True)
    carry_ref[...] = carry
    cnt_ref[...] = carry


def _rank(bk8, chunk=2048):
    T = bk8.shape[1]
    kernel = functools.partial(_rank_kernel, chunk=chunk)
    return pl.pallas_call(
        kernel, grid=(T // chunk,),
        in_specs=[pl.BlockSpec((8, chunk), lambda i: (0, i))],
        out_specs=[pl.BlockSpec((1, chunk), lambda i: (0, i)),
                   pl.BlockSpec((LANES, 1), lambda i: (0, 0))],
        out_shape=[jax.ShapeDtypeStruct((1, T), f32), jax.ShapeDtypeStruct((LANES, 1), f32)],
        scratch_shapes=[pltpu.VMEM((LANES, 1), f32)],
        compiler_params=_cparams(("arbitrary",)), name="rank",
    )(bk8)


def _dest_kernel(bk_ref, rank_ref, ps_ref, dest_ref):
    chunk = bk_ref.shape[1]
    bid = lax.broadcasted_iota(i32, (LANES, chunk), 0).astype(f32)
    start = jnp.sum(jnp.where(bid == bk_ref[0:1, :], ps_ref[...], 0.0), axis=0, keepdims=True)
    dest_ref[...] = (start + rank_ref[...]).astype(i32)


def _dest(bk8, rank, ps_col, chunk=2048):
    T = bk8.shape[1]
    return pl.pallas_call(
        _dest_kernel, grid=(T // chunk,),
        in_specs=[pl.BlockSpec((8, chunk), lambda i: (0, i)), pl.BlockSpec((1, chunk), lambda i: (0, i)),
                  pl.BlockSpec((LANES, 1), lambda i: (0, 0))],
        out_specs=pl.BlockSpec((1, chunk), lambda i: (0, i)),
        out_shape=jax.ShapeDtypeStruct((1, T), i32),
        compiler_params=_cparams(("parallel",)), name="dest",
    )(bk8, rank, ps_col)


def _sc_invert(dest, n_rows):
    T = dest.shape[0]
    assert T & (T - 1) == 0
    lanes = SC_LANES
    mesh = plsc.VectorSubcoreMesh(core_axis_name="core", subcore_axis_name="subcore",
                                  num_cores=SC_CORES, num_subcores=SC_SUBCORES)

    @functools.partial(pl.kernel, out_type=jax.ShapeDtypeStruct((n_rows,), i32), mesh=mesh,
                       scratch_types=[pltpu.VMEM((T,), i32), pltpu.VMEM((n_rows,), i32)],
                       compiler_params=pltpu.CompilerParams(needs_layout_passes=False),
                       name="sc_invert")
    def k(dest_hbm, out_hbm, dest_v, table_v):
        wid = lax.axis_index("subcore") * SC_CORES + lax.axis_index("core")

        @pl.when(wid == 0)
        def _():
            pltpu.sync_copy(dest_hbm, dest_v)
            lane = lax.iota(i32, lanes)

            @pl.loop(0, n_rows // lanes)
            def _(j):
                off = pl.multiple_of(j * lanes, lanes)
                table_v[pl.ds(off, lanes)] = (lane + off) & (T - 1)


            @pl.loop(0, T // lanes)
            def _(j):
                off = pl.multiple_of(j * lanes, lanes)
                plsc.store_scatter(table_v, [dest_v[pl.ds(off, lanes)]], lane + off)

            pltpu.sync_copy(table_v, out_hbm)

    return k(dest)


def _sc_gather_rows(idx, src, chunk=SC_GATHER_ROWS):
    n = idx.shape[0]
    w = src.shape[1]
    workers = SC_CORES * SC_SUBCORES
    per_worker = n // workers
    mesh = plsc.VectorSubcoreMesh(core_axis_name="core", subcore_axis_name="subcore",
                                  num_cores=SC_CORES, num_subcores=SC_SUBCORES)

    n_chunks = per_worker // chunk
    assert n_chunks % 2 == 0

    @functools.partial(pl.kernel, out_type=jax.ShapeDtypeStruct((n, w), src.dtype), mesh=mesh,
                       scratch_types=[pltpu.VMEM((per_worker,), i32), pltpu.VMEM((2, chunk, w), src.dtype),
                                      pltpu.SemaphoreType.DMA((2,)), pltpu.SemaphoreType.DMA((2,))],
                       name="sc_gather_rows")
    def k(src_hbm, idx_hbm, out_hbm, idx_v, rows_v, gsem, wsem):
        wid = lax.axis_index("subcore") * SC_CORES + lax.axis_index("core")
        base = wid * per_worker
        pltpu.sync_copy(idx_hbm.at[pl.ds(base, per_worker)], idx_v)

        def gather(c, slot):
            rows = idx_v.at[pl.ds(pl.multiple_of(c * chunk, chunk), chunk)]
            return pltpu.make_async_copy(src_hbm.at[rows], rows_v.at[slot], gsem.at[slot])

        def write(c, slot):
            out = out_hbm.at[pl.ds(pl.multiple_of(base + c * chunk, chunk), chunk)]
            return pltpu.make_async_copy(rows_v.at[slot], out, wsem.at[slot])

        gather(0, 0).start()

        @pl.loop(0, n_chunks // 2)
        def _(pair):
            c = 2 * pair

            @pl.when(pair > 0)
            def _():
                write(c - 1, 1).wait()

            gather(c + 1, 1).start()
            gather(c, 0).wait()
            write(c, 0).start()

            @pl.when(c + 2 < n_chunks)
            def _():
                write(c, 0).wait()
                gather(c + 2, 0).start()

            gather(c + 1, 1).wait()
            write(c + 1, 1).start()

        write(n_chunks - 2, 0).wait()
        write(n_chunks - 1, 1).wait()

    return k(src, idx)


def _expert_kernel(grp_ref, ea_ref, eb_ref, used_ref, xs_ref, wg_ref, wu_ref, wd_ref, g_ref, b_ref, y_ref,
                   z_ref, *, alpha):
    del grp_ref
    n = pl.program_id(0)
    blk = jnp.minimum(n, pl.num_programs(0) - 2)
    used = used_ref[0]

    @pl.when(n == 0)
    def _():
        z_ref[...] = jnp.zeros_like(z_ref)

    @pl.when(n <= used)
    def _():
        y_ref[...] = _layer_norm(z_ref[...], g_ref[...], b_ref[...])
        h2 = xs_ref[:, :D]
        x = h2.astype(bf16)

        def expert(e):
            a = _dot(x, wg_ref[0, e])
            u = _dot(x, wu_ref[0, e])
            act = a * (1.0 / (1.0 + jnp.exp(-a))) * u
            return _dot(act.astype(bf16), wd_ref[0, e])

        ga = xs_ref[:, D:D + 1]
        gb = xs_ref[:, D + 1:D + 2]
        z_ref[...] = alpha * h2 + ga * expert(ea_ref[blk]) + gb * expert(eb_ref[blk])

    @pl.when(n > used)
    def _():
        y_ref[...] = jnp.zeros_like(y_ref)


def _expert_kernel_into(grp_ref, ea_ref, eb_ref, used_ref, xs_ref, wg_ref, wu_ref, wd_ref, g_ref, b_ref,
                        ys_ref, y_ref, z_ref, *, alpha):
    del ys_ref
    _expert_kernel(grp_ref, ea_ref, eb_ref, used_ref, xs_ref, wg_ref, wu_ref, wd_ref, g_ref, b_ref, y_ref,
                   z_ref, alpha=alpha)


def _experts(grp, ea, eb, used, xs, wg, wu, wd, ln_g, ln_b, alpha, ys, first_block, total_rows):
    nblk = xs.shape[0] // ROW_BLOCK

    def xmap(n, grp, ea, eb, used):
        return (jnp.maximum(jnp.minimum(n, used[0] - 1), 0), 0)

    gmap = lambda n, grp, ea, eb, used: (grp[jnp.minimum(n, nblk - 1)], 0, 0, 0)
    gspec = lambda w: pl.BlockSpec((1,) + w.shape[1:], gmap, pipeline_mode=pl.Buffered(1))
    vec = pl.BlockSpec((1, D), lambda n, grp, ea, eb, used: (0, 0))
    in_specs = [pl.BlockSpec((ROW_BLOCK, XW), xmap), gspec(wg), gspec(wu), gspec(wd), vec, vec]
    operands = [grp, ea, eb, used, xs, wg, wu, wd, ln_g, ln_b]
    aliases = {}
    body = _expert_kernel
    if ys is not None:
        in_specs.append(pl.BlockSpec(memory_space=pl.ANY))
        aliases = {len(operands): 0}
        operands.append(ys)
        body = _expert_kernel_into
    grid_spec = pltpu.PrefetchScalarGridSpec(
        num_scalar_prefetch=4, grid=(nblk + 1,), in_specs=in_specs,
        out_specs=pl.BlockSpec((ROW_BLOCK, D),
                               lambda n, grp, ea, eb, used: (jnp.maximum(n - 1, 0) + first_block, 0)),
        scratch_shapes=[pltpu.VMEM((ROW_BLOCK, D), f32)],
    )
    return pl.pallas_call(
        functools.partial(body, alpha=alpha), grid_spec=grid_spec,
        out_shape=jax.ShapeDtypeStruct((total_rows, D), f32),
        input_output_aliases=aliases,
        compiler_params=_cparams(("arbitrary",)), name="experts",
    )(*operands)


def _pair_tables():
    ea = np.zeros((LANES,), np.int32)
    eb = np.zeros((LANES,), np.int32)
    for g in range(N_GROUPS):
        k = 0
        for a in range(EPG):
            for b in range(a + 1, EPG):
                ea[g * N_PAIRS + k] = a
                eb[g * N_PAIRS + k] = b
                k += 1
    return ea, eb


_PAIR_A, _PAIR_B = _pair_tables()


def _layer(h, mem, positions, w_in, b_forget, sinks, w_mix_out, ln_mix_g, ln_mix_b,
           w_xq, w_xkv, w_xout, ln_x_g, ln_x_b, w_rg, b_rg, w_re, b_re,
           w_eg, w_eu, w_ed, ln_f_g, ln_f_b, alpha):
    B, S, _ = h.shape
    T = B * S
    x2 = h.reshape(T, D)

    o = np.cumsum((0, FOX_W, FOX_W, FOX_W, N_FOX, SWA_Q_W, SWA_KV_W, SWA_KV_W))
    w_qf, w_kf, w_vf, w_fl, w_qs, w_ks, w_vs = (w_in[:, o[i]:o[i + 1]] for i in range(7))
    def regroup(a, axis):
        shp = a.shape
        a = jnp.moveaxis(a, axis, 0).reshape(N_SWA_KV, N_SWA // N_SWA_KV, HD, -1)
        return jnp.moveaxis(jnp.swapaxes(a, 0, 1).reshape(N_SWA * HD, -1), 0, axis).reshape(shp)

    w_all = jnp.concatenate([w_kf, regroup(w_qs, 1), w_ks, w_vs], axis=1).astype(bf16)
    wqt = w_qf.T.astype(bf16)
    wvt = w_vf.T.astype(bf16)
    wfl = w_fl.T.astype(bf16)
    bfc = b_forget.reshape(N_FOX, 1).astype(f32)
    half = HD // 2
    inv_freq = ROPE_THETA ** (-jnp.arange(half, dtype=f32) / half)
    per_row = LANES // half
    invf = jnp.tile(inv_freq, per_row).reshape(1, LANES)
    pos4 = jnp.repeat(positions.reshape(T // per_row, per_row).astype(i32), half, axis=1)
    cos, sin = (t.reshape(T, half) for t in _rope_table(pos4, invf))
    w_out = jnp.concatenate([w_mix_out[:FOX_W], regroup(w_mix_out[FOX_W:], 0)], axis=0).astype(bf16)

    tq = 512
    (qt, kf, vt, qs, ks, vs, lf, qn, kn), (eg16, eu16, ed16) = _in_proj(
        x2, cos, sin, w_all, wqt, wvt, wfl, bfc, w_eg, w_eu, w_ed, tq)
    c, ca = _cumsum(lf, S)
    c4 = c.reshape(N_FOX // 2, 2, T // tq, tq)
    r3 = lambda a: a.reshape(B, S, a.shape[-1])
    o_fox = _fox(_fox_first_blocks(qn, kn, c, B, S, tq), qt, r3(kf), r3(ca), vt, c4, B, S, tq)
    o_swa = _swa(sinks.astype(f32) * LOG2E, r3(qs), r3(ks), r3(vs), B, S)

    kx, vx = _kvproj(mem.reshape(-1, D), w_xkv.astype(bf16))

    gpad = EPG - N_GROUPS
    wr = jnp.concatenate([jnp.pad(w_rg, ((0, 0), (0, gpad))),
                          jnp.transpose(w_re, (1, 0, 2)).reshape(D, N_EXPERTS)], axis=1)
    wr = jnp.pad(wr, ((0, 0), (0, LANES - wr.shape[1]))).astype(f32)
    wrh = wr.astype(bf16)
    wr2 = jnp.concatenate([wrh, (wr - wrh.astype(f32)).astype(bf16)], axis=1)
    br = jnp.pad(jnp.concatenate([jnp.pad(b_rg, (0, gpad)), b_re.reshape(-1)]), (0, LANES - EPG - N_EXPERTS))
    br = br.reshape(1, LANES).astype(f32)
    v2 = lambda a: a.reshape(1, D).astype(f32)
    h2x, bk8 = _mid(x2, o_fox.reshape(T, FOX_W), o_swa.reshape(T, SWA_Q_W), w_out,
                    v2(ln_mix_g), v2(ln_mix_b), w_xq.astype(bf16), kx, vx, w_xout.astype(bf16),
                    v2(ln_x_g), v2(ln_x_b), wr2, br, alpha, S)

    rank, cnt = _rank(bk8)
    counts = cnt[:, 0].astype(i32)
    padded = ((counts + ROW_BLOCK - 1) // ROW_BLOCK) * ROW_BLOCK
    pad_end = jnp.cumsum(padded)
    pad_start = (pad_end - padded).astype(i32)
    step = np.lcm(MOE_CHUNKS * SC_CORES * SC_SUBCORES * 2 * SC_GATHER_ROWS, MOE_CHUNKS * ROW_BLOCK)
    P = int(-(-(T + N_BUCKETS * ROW_BLOCK) // step) * step)
    nblk = P // ROW_BLOCK
    used = (pad_end[-1] // ROW_BLOCK).astype(i32).reshape(1)
    blk_row = jnp.arange(nblk, dtype=i32)[:, None] * ROW_BLOCK
    blk_bucket = jnp.minimum(jnp.sum((pad_end[None, :] <= blk_row).astype(i32), axis=1), N_BUCKETS - 1)
    pick = (blk_bucket[:, None] == jnp.arange(LANES, dtype=i32)[None, :]).astype(i32)
    blk_a = jnp.sum(pick * jnp.asarray(_PAIR_A)[None, :], axis=1)
    blk_b = jnp.sum(pick * jnp.asarray(_PAIR_B)[None, :], axis=1)
    blk_g = blk_bucket // N_PAIRS
    by_group = lambda w: w.reshape((N_GROUPS, EPG) + w.shape[1:])

    dest = _dest(bk8, rank, pad_start.astype(f32).reshape(LANES, 1))[0]
    row_tok = _sc_invert(dest, P)
    cblk = nblk // MOE_CHUNKS
    ys = None
    for cidx in range(MOE_CHUNKS):
        lo = cidx * cblk
        xs = _sc_gather_rows(row_tok[lo * ROW_BLOCK:(lo + cblk) * ROW_BLOCK], h2x)
        used_c = jnp.clip(used - lo, 0, cblk)
        ys = _experts(blk_g[lo:lo + cblk], blk_a[lo:lo + cblk], blk_b[lo:lo + cblk], used_c, xs,
                      by_group(eg16), by_group(eu16), by_group(ed16), v2(ln_f_g), v2(ln_f_b), alpha,
                      ys, lo, P)
    return _sc_gather_rows(dest, ys).reshape(B, S, D)


def kernel(x, mem, positions, w_in, b_forget, sinks, w_mix_out, ln_mix_g, ln_mix_b, w_xq, w_xkv, w_xout,
           ln_x_g, ln_x_b, w_route_group, b_route_group, w_route_expert, b_route_expert,
           w_exp_gate, w_exp_up, w_exp_down, ln_ffn_g, ln_ffn_b):
    depth = w_in.shape[0]
    alpha = (2.0 * depth) ** 0.25
    h = x
    for l in range(depth):
        h = _layer(h, mem, positions, w_in[l], b_forget[l], sinks[l], w_mix_out[l], ln_mix_g[l], ln_mix_b[l],
                   w_xq[l], w_xkv[l], w_xout[l], ln_x_g[l], ln_x_b[l], w_route_group[l], b_route_group[l],
                   w_route_expert[l], b_route_expert[l], w_exp_gate[l], w_exp_up[l], w_exp_down[l],
                   ln_ffn_g[l], ln_ffn_b[l], alpha)
    return h
```

```python
import functools

import jax
import jax.numpy as jnp
import numpy as np
from jax import lax
from jax.experimental import pallas as pl
from jax.experimental.pallas import tpu as pltpu
from jax.experimental.pallas import tpu_sc as plsc

f32 = jnp.float32
bf16 = jnp.bfloat16
i32 = jnp.int32

D = 1024
HD = 64
N_FOX = 8
N_SWA = 8
N_SWA_KV = 2
FOX_W = 512
SWA_Q_W = 512
SWA_KV_W = 128
WINDOW = 128
ROPE_THETA = 10000.0
N_XH = 4
XHD = 256
N_GROUPS = 4
EPG = 8
N_EXPERTS = 32
D_EXPERT = 512
LN_EPS = 1e-5
NEG = -1e30
LOG2E = 1.4426950408889634
L_ROW = (HD, 0)
SKIP_LOG2 = 160.0

SC_CORES = 2
SC_SUBCORES = 16
SC_LANES = 16
SC_GATHER_ROWS = 32
MOE_CHUNKS = 4
LANES = 128
ROW_BLOCK = 128
N_PAIRS = EPG * (EPG - 1) // 2
N_BUCKETS = N_GROUPS * N_PAIRS
XW = D + LANES
VMEM_LIMIT = 56 * 1024 * 1024


def _cparams(sem):
    return pltpu.CompilerParams(dimension_semantics=sem, vmem_limit_bytes=VMEM_LIMIT)


def _layer_norm(v, g, b):
    mu = jnp.mean(v, axis=-1, keepdims=True)
    c = v - mu
    var = jnp.mean(c * c, axis=-1, keepdims=True)
    return c * lax.rsqrt(var + LN_EPS) * g + b


def _dot(a, b):
    return jnp.dot(a, b, preferred_element_type=f32)


def _dot_nt(a, b):
    return lax.dot_general(a, b, (((1,), (1,)), ((), ())), preferred_element_type=f32)


def _rope_table_kernel(pos_ref, invf_ref, cos_ref, sin_ref):
    ang = pos_ref[...].astype(f32) * invf_ref[...]
    cos_ref[...] = jnp.cos(ang)
    sin_ref[...] = jnp.sin(ang)


def _rope_table(pos4, invf, rows=1024):
    R = pos4.shape[0]
    blk = pl.BlockSpec((rows, LANES), lambda i: (i, 0))
    return pl.pallas_call(
        _rope_table_kernel, grid=(R // rows,),
        in_specs=[blk, pl.BlockSpec((1, LANES), lambda i: (0, 0))], out_specs=[blk, blk],
        out_shape=[jax.ShapeDtypeStruct((R, LANES), f32)] * 2,
        compiler_params=_cparams(("parallel",)), name="rope_table",
    )(pos4, invf)


def _inproj_kernel(x_ref, cos_ref, sin_ref, w_ref, wqt_ref, wvt_ref, wfl_ref, bf_ref, ind_ref,
                   eg_ref, eu_ref, ed_ref,
                   qt_ref, kf_ref, vt_ref, qs_ref, ks_ref, vs_ref, lf_ref, qn_ref, kn_ref,
                   egb_ref, eub_ref, edb_ref):
    tm = x_ref.shape[0]
    xb = x_ref[...].astype(bf16)
    egb_ref[...] = eg_ref[...].astype(bf16)
    eub_ref[...] = eu_ref[...].astype(bf16)
    edb_ref[...] = ed_ref[...].astype(bf16)

    def proj(lo, hi):
        return _dot(xb, w_ref[:, lo:hi])

    qv = _dot_nt(wqt_ref[...], xb) * (0.125 * LOG2E)
    qt_ref[0] = qv.astype(bf16)
    vt_ref[0] = _dot_nt(wvt_ref[...], xb).astype(bf16)
    kv = proj(0, 512)
    kf_ref[...] = kv.astype(bf16)
    q2 = jnp.sum((qv * qv).reshape(N_FOX, HD, tm), axis=1)
    qn_ref[0] = jnp.broadcast_to(jnp.max(q2, axis=1, keepdims=True), (N_FOX, LANES))
    k2 = _dot((kv * kv).astype(bf16), ind_ref[...])
    kn_ref[0] = jnp.broadcast_to(jnp.max(k2, axis=0, keepdims=True), (N_FOX, LANES))

    reps = LANES // cos_ref.shape[1]
    cos = jnp.concatenate([cos_ref[...]] * reps, axis=1)
    sin = jnp.concatenate([sin_ref[...]] * reps, axis=1)
    lane = lax.broadcasted_iota(i32, (tm, LANES), 1)
    lo_half = (lane % HD) < (HD // 2)
    sin_s = jnp.where(lo_half, -sin, sin)

    def rope(z):
        rot = jnp.where(lo_half, pltpu.roll(z, LANES - HD // 2, 1), pltpu.roll(z, HD // 2, 1))
        return z * cos + rot * sin_s

    zq = proj(512, 1024)
    for g in range(4):
        sl = slice(g * LANES, (g + 1) * LANES)
        qs_ref[:, sl] = (rope(zq[:, sl]) * (0.125 * LOG2E)).astype(bf16)
    ks_ref[...] = rope(proj(1024, 1152)).astype(bf16)
    vs_ref[...] = proj(1152, 1280).astype(bf16)

    z = _dot_nt(wfl_ref[...], xb) + bf_ref[...]
    lf_ref[...] = jnp.minimum(z, 0.0) - jnp.log(1.0 + jnp.exp(-jnp.abs(z)))


def _in_proj(x2, cos, sin, w_all, wqt, wvt, wfl, bfc, w_eg, w_eu, w_ed, tm):
    T = x2.shape[0]
    steps = T // tm
    row = lambda w: pl.BlockSpec((tm, w), lambda i: (i, 0))
    full = lambda a: pl.BlockSpec(a.shape, lambda i: (0,) * a.ndim)
    fmaj = pl.BlockSpec((1, FOX_W, tm), lambda i: (i, 0, 0))
    flat = [w.reshape(-1, w.shape[-1]) for w in (w_eg, w_eu, w_ed)]
    slices = [pl.BlockSpec((w.shape[0] // steps, w.shape[1]), lambda i: (i, 0)) for w in flat]
    ind = jnp.asarray(np.arange(FOX_W)[:, None] // HD == np.arange(LANES)[None, :], bf16)
    nrm = pl.BlockSpec((1, N_FOX, LANES), lambda i: (i, 0, 0))
    outs = pl.pallas_call(
        _inproj_kernel,
        grid=(steps,),
        in_specs=[row(D), row(cos.shape[1]), row(sin.shape[1]), full(w_all), full(wqt), full(wvt), full(wfl),
                  full(bfc), full(ind)] + slices,
        out_specs=[fmaj, row(512), fmaj, row(512), row(128), row(128),
                   pl.BlockSpec((N_FOX, tm), lambda i: (0, i)), nrm, nrm] + slices,
        out_shape=[jax.ShapeDtypeStruct((steps, FOX_W, tm), bf16), jax.ShapeDtypeStruct((T, 512), bf16),
                   jax.ShapeDtypeStruct((steps, FOX_W, tm), bf16), jax.ShapeDtypeStruct((T, 512), bf16),
                   jax.ShapeDtypeStruct((T, 128), bf16), jax.ShapeDtypeStruct((T, 128), bf16),
                   jax.ShapeDtypeStruct((N_FOX, T), f32),
                   jax.ShapeDtypeStruct((steps, N_FOX, LANES), f32), jax.ShapeDtypeStruct((steps, N_FOX, LANES), f32)]
        + [jax.ShapeDtypeStruct(w.shape, bf16) for w in flat],
        compiler_params=_cparams(("parallel",)),
        name="in_proj",
    )(x2, cos, sin, w_all, wqt, wvt, wfl, bfc, ind, *flat)
    experts_bf16 = [o.reshape(w.shape) for o, w in zip(outs[9:], (w_eg, w_eu, w_ed))]
    return outs[:9], experts_bf16


def _cumsum_kernel(lf_ref, c_ref, ca_ref):
    S = lf_ref.shape[1]
    ch = 256
    r = lax.broadcasted_iota(i32, (ch, ch), 0)
    c = lax.broadcasted_iota(i32, (ch, ch), 1)
    tri = (r <= c).astype(f32)
    eye = (r == c).astype(bf16)
    stacked = jnp.concatenate([lf_ref[:, j * ch:(j + 1) * ch] for j in range(S // ch)], axis=0)
    local = jnp.dot(stacked, tri, precision=lax.Precision.HIGHEST, preferred_element_type=f32)
    carry = jnp.zeros((N_FOX, 1), f32)
    for j in range(S // ch):
        cc = local[j * N_FOX:(j + 1) * N_FOX] + carry
        carry = cc[:, ch - 1:ch]
        c2 = cc * LOG2E
        c_ref[:, j * ch:(j + 1) * ch] = c2
        neg = -c2
        hi = neg.astype(bf16)
        r1 = neg - hi.astype(f32)
        mid = r1.astype(bf16)
        lo = (r1 - mid.astype(f32)).astype(bf16)
        terms = jnp.concatenate([hi, mid, lo, jnp.zeros((LANES - 3 * N_FOX, ch), bf16)], axis=0)
        ca_ref[j * ch:(j + 1) * ch, :] = _dot_nt(eye, terms).astype(bf16)


def _cumsum(lf, S):
    T = lf.shape[1]
    spec = pl.BlockSpec((N_FOX, S), lambda b: (0, b))
    return pl.pallas_call(
        _cumsum_kernel, grid=(T // S,), in_specs=[spec],
        out_specs=[spec, pl.BlockSpec((S, LANES), lambda b: (b, 0))],
        out_shape=[jax.ShapeDtypeStruct((N_FOX, T), f32), jax.ShapeDtypeStruct((T, LANES), bf16)],
        compiler_params=_cparams(("parallel",)), name="cumsum",
    )(lf)


def _fox_kernel(j0_ref, qt_ref, k_ref, ca_ref, vt_ref, c_ref, o_ref, t0_ref, t1_ref, acc_ref, *, tq, nq):
    n = pl.program_id(0)
    last = pl.num_programs(0) - 1
    row = lax.broadcasted_iota(i32, (LANES, tq), 0)
    is_a = row < HD

    def write_out():
        acca, accb = acc_ref[0], acc_ref[1]
        la = acca[L_ROW[0]:L_ROW[0] + 1, :]
        lb = accb[L_ROW[1]:L_ROW[1] + 1, :]
        ot = jnp.where(is_a, acca / la, accb / lb)
        o_ref[0] = jnp.transpose(ot).astype(bf16)

    @pl.when(n == 0)
    def _():
        acc_ref[...] = jnp.ones_like(acc_ref)

    @pl.when(n == last)
    def _():
        write_out()

    @pl.when(n < last)
    def _():
        write_out()
        _fox_block(j0_ref, qt_ref, k_ref, ca_ref, vt_ref, c_ref, t0_ref, t1_ref, acc_ref, n, row, is_a, tq, nq)


def _fox_block(j0_ref, qt_ref, k_ref, ca_ref, vt_ref, c_ref, t0_ref, t1_ref, acc_ref, n, row, is_a, tq, nq):
    hp = (n // nq) % (N_FOX // 2)
    i = n % nq
    qt = qt_ref[0]
    zero = jnp.zeros_like(qt)
    q_ops = []
    for h in range(2):
        ones = jnp.where(((row & 7) == 2 * hp + h) & (row < 3 * N_FOX), 1.0, 0.0).astype(bf16)
        qh = jnp.where(is_a, qt, zero) if h == 0 else jnp.where(is_a, zero, qt)
        q_ops.append(jnp.concatenate([qh, ones], axis=0))
    kr = lax.broadcasted_iota(i32, (tq, tq), 0)
    qc = lax.broadcasted_iota(i32, (tq, tq), 1)
    causal = kr <= qc
    cq = [c_ref[0, h, pl.ds(i, 1), :] for h in range(2)]

    def scores(j, t_ref):
        off = pl.multiple_of(j * tq, tq)
        kblk = jnp.concatenate([k_ref[0, pl.ds(off, tq), :], ca_ref[0, pl.ds(off, tq), :]], axis=1)
        for h in range(2):
            t_ref[h] = _dot(kblk, q_ops[h])

    keep = [jnp.where(is_a, 1.0, 0.0).astype(bf16), jnp.where(is_a, 0.0, 1.0).astype(bf16)]
    ones_row = [jnp.where(row == L_ROW[h], 1.0, 0.0).astype(bf16) for h in range(2)]

    def softmax_pv(j, t_ref, carry, masked):
        vt = vt_ref[j]
        vts = [vt * keep[h] + ones_row[h] for h in range(2)]
        new = []
        for h in range(2):
            m, acc = carry[h]
            t = t_ref[h]
            if masked:
                t = jnp.where(causal, t, NEG)
            m_new = jnp.maximum(m, jnp.max(t, axis=0, keepdims=True) + cq[h])
            alpha = jnp.exp2(m - m_new)
            p = jnp.exp2(t + (cq[h] - m_new))
            acc = alpha * acc + _dot(vts[h], p.astype(bf16))
            new.append((m_new, acc))
        return tuple(new)

    j0 = j0_ref[n]
    n_full = i - j0

    def pair(k, carry):
        j = j0 + 2 * k
        scores(j + 1, t1_ref)
        carry = softmax_pv(j, t0_ref, carry, False)
        scores(j + 2, t0_ref)
        return softmax_pv(j + 1, t1_ref, carry, False)

    def odd_tail(carry):
        scores(i, t1_ref)
        carry = softmax_pv(i - 1, t0_ref, carry, False)
        return softmax_pv(i, t1_ref, carry, True)

    def even_tail(carry):
        return softmax_pv(i, t0_ref, carry, True)

    init = tuple((jnp.full((1, tq), NEG, f32), jnp.zeros((LANES, tq), f32)) for _ in range(2))
    scores(j0, t0_ref)
    carry = lax.fori_loop(0, n_full // 2, pair, init)
    (_, acca), (_, accb) = lax.cond(n_full % 2 == 1, odd_tail, even_tail, carry)
    acc_ref[0] = acca
    acc_ref[1] = accb


def _fox_first_blocks(qn, kn, c, B, S, tq):
    nq = S // tq
    qmax = jnp.sqrt(qn[:, :, 0].reshape(B, nq, N_FOX))
    kmax = jnp.sqrt(kn[:, 0, :N_FOX].reshape(B, nq, N_FOX))
    qk = 1.02 * qmax[:, :, None, :] * (kmax[:, None, :, :] + kmax[:, :, None, :])
    cb = c.reshape(N_FOX, B, nq, tq)
    c_first = jnp.transpose(cb[..., 0], (1, 2, 0))
    c_last = jnp.transpose(cb[..., tq - 1], (1, 2, 0))
    gap = c_last[:, None, :, :] - c_first[:, :, None, :]
    blk = jnp.arange(nq)
    negligible = (qk - gap < -SKIP_LOG2) & (blk[None, :] < blk[:, None])[None, :, :, None]
    pair = jnp.all(negligible.reshape(B, nq, nq, N_FOX // 2, 2), axis=-1)
    j0 = jnp.sum(jnp.cumprod(pair.astype(i32), axis=2), axis=2)
    return jnp.transpose(j0, (0, 2, 1)).reshape(-1).astype(i32)


def _fox(j0, qt, kf, ca, vt, c4, B, S, tq):
    nq = S // tq
    pairs = N_FOX // 2
    steps = B * pairs * nq
    kernel = functools.partial(_fox_kernel, tq=tq, nq=nq)

    def at(n):
        n = jnp.minimum(n, steps - 1)
        return n // (pairs * nq), (n // nq) % pairs, n % nq

    def spec(shape, index):
        return pl.BlockSpec(shape, lambda n: index(*at(n)))

    return pl.pallas_call(
        kernel,
        grid=(steps + 1,),
        in_specs=[
            pl.BlockSpec(memory_space=pltpu.SMEM),
            spec((1, LANES, tq), lambda b, hp, i: (b * nq + i, hp, 0)),
            spec((1, S, LANES), lambda b, hp, i: (b, 0, hp)),
            spec((1, S, LANES), lambda b, hp, i: (b, 0, 0)),
            spec((nq, LANES, tq), lambda b, hp, i: (b, hp, 0)),
            spec((1, 2, nq, tq), lambda b, hp, i: (hp, 0, b, 0)),
        ],
        out_specs=pl.BlockSpec((1, tq, LANES),
                               lambda n: (lambda b, hp, i: (b, i, hp))(*at(jnp.maximum(n - 1, 0)))),
        out_shape=jax.ShapeDtypeStruct((B, S, FOX_W), bf16),
        scratch_shapes=[pltpu.VMEM((2, tq, tq), f32), pltpu.VMEM((2, tq, tq), f32),
                        pltpu.VMEM((2, LANES, tq), f32)],
        compiler_params=_cparams(("arbitrary",)),
        name="fox",
    )(j0, qt, kf, ca, vt, c4)


def _swa_kernel(sink_ref, q_ref, k_ref, v_ref, o_ref, *, tq):
    W = WINDOW
    nsub = tq // W
    n0 = pl.program_id(1) * nsub
    lane = lax.broadcasted_iota(i32, (W, LANES), 1)
    is0 = lane < HD
    qoff = lax.broadcasted_iota(i32, (4 * W, 2 * W), 0) % W
    cols = lax.broadcasted_iota(i32, (4 * W, 2 * W), 1)
    bias_mid = jnp.where((cols - W <= qoff) & (qoff - (cols - W) < W), 0.0, NEG)
    bias_first = jnp.where(cols <= qoff, 0.0, NEG)
    rgrp = lax.broadcasted_iota(i32, (4 * W, 1), 0) // W
    for r in range(nsub):
        nb = n0 + r
        kstart = pl.multiple_of(jnp.maximum(nb * W - W, 0), W)
        ks = k_ref[0, pl.ds(kstart, 2 * W), :]
        vs = v_ref[0, pl.ds(kstart, 2 * W), :]
        bias = jnp.where(nb == 0, bias_first, bias_mid)
        outs = []
        for kv in range(2):
            keep = is0 if kv == 0 else jnp.logical_not(is0)
            parts = []
            for g in range(4):
                qg = q_ref[0, r * W:(r + 1) * W, g * LANES:(g + 1) * LANES]
                parts.append(jnp.where(keep, qg, jnp.zeros_like(qg)))
            qstack = jnp.concatenate(parts, axis=0)
            s = bias + _dot_nt(qstack, ks)
            sink = jnp.zeros((4 * W, 1), f32)
            for g in range(4):
                sink = jnp.where(rgrp == g, sink_ref[kv * 4 + g], sink)
            m = jnp.maximum(jnp.max(s, axis=1, keepdims=True), sink)
            e = jnp.exp2(s - m)
            den = jnp.sum(e, axis=1, keepdims=True) + jnp.exp2(sink - m)
            outs.append(_dot(e.astype(bf16), vs) / den)
        for g in range(4):
            og = jnp.where(is0, outs[0][g * W:(g + 1) * W], outs[1][g * W:(g + 1) * W])
            o_ref[0, r * W:(r + 1) * W, g * LANES:(g + 1) * LANES] = og.astype(bf16)


def _swa(sinks, qs, ks, vs, B, S, tq=512):
    kernel = functools.partial(_swa_kernel, tq=tq)
    return pl.pallas_call(
        kernel,
        grid=(B, S // tq),
        in_specs=[
            pl.BlockSpec(memory_space=pltpu.SMEM),
            pl.BlockSpec((1, tq, SWA_Q_W), lambda b, i: (b, i, 0)),
            pl.BlockSpec((1, S, SWA_KV_W), lambda b, i: (b, 0, 0)),
            pl.BlockSpec((1, S, SWA_KV_W), lambda b, i: (b, 0, 0)),
        ],
        out_specs=pl.BlockSpec((1, tq, SWA_Q_W), lambda b, i: (b, i, 0)),
        out_shape=jax.ShapeDtypeStruct((B, S, SWA_Q_W), bf16),
        compiler_params=_cparams(("parallel", "arbitrary")),
        name="swa",
    )(sinks, qs, ks, vs)


def _kvproj_kernel(m_ref, w_ref, k_ref, v_ref):
    mb = m_ref[...].astype(bf16)
    k_ref[...] = _dot(mb, w_ref[:, :D]).astype(bf16)
    v_ref[...] = _dot(mb, w_ref[:, D:]).astype(bf16)


def _kvproj(mem2, w_xkv, tm=512):
    R = mem2.shape[0]
    row = pl.BlockSpec((tm, D), lambda i: (i, 0))
    return pl.pallas_call(
        _kvproj_kernel, grid=(R // tm,),
        in_specs=[row, pl.BlockSpec(w_xkv.shape, lambda i: (0, 0))],
        out_specs=[row, row],
        out_shape=[jax.ShapeDtypeStruct((R, D), bf16)] * 2,
        compiler_params=_cparams(("parallel",)), name="kvproj",
    )(mem2, w_xkv)


def _mid_kernel(x_ref, of_ref, os_ref, wo_ref, g1_ref, b1_ref, wq_ref, k_ref, v_ref,
                wxo_ref, g2_ref, b2_ref, wr_ref, br_ref,
                h2_ref, bk_ref, oc_ref, *, alpha):
    tm = x_ref.shape[0]
    mix = _dot(of_ref[...], wo_ref[:FOX_W, :]) + _dot(os_ref[...], wo_ref[FOX_W:, :])
    h1 = _layer_norm(alpha * x_ref[...] + mix, g1_ref[...], b1_ref[...])

    q = (_dot(h1.astype(bf16), wq_ref[...]) * 0.0625).astype(bf16)
    for h in range(N_XH):
        sl = slice(h * XHD, (h + 1) * XHD)
        s = _dot_nt(q[:, sl], k_ref[:, sl])
        e = jnp.exp(s - jnp.max(s, axis=1, keepdims=True))
        p = e / jnp.sum(e, axis=1, keepdims=True)
        oc_ref[:, sl] = _dot(p.astype(bf16), v_ref[:, sl]).astype(bf16)
    xo = _dot(oc_ref[...], wxo_ref[...])
    h2 = _layer_norm(alpha * h1 + xo, g2_ref[...], b2_ref[...])
    h2_ref[:, :D] = h2

    hh = h2.astype(bf16)
    hl = (h2 - hh.astype(f32)).astype(bf16)
    hi_terms = _dot(hh, wr_ref[...])
    lg = hi_terms[:, :LANES] + _dot(hl, wr_ref[:, :LANES]) + hi_terms[:, LANES:] + br_ref[...]

    lgt = jnp.transpose(lg)
    row = lax.broadcasted_iota(i32, (EPG, tm), 0).astype(f32)
    big = float(EPG)

    def first_max(vals, mask):
        vm = jnp.where(mask, vals, NEG)
        top = jnp.max(vm, axis=0, keepdims=True)
        idx = jnp.min(jnp.where(mask & (vm == top), row, big), axis=0, keepdims=True)
        return top, idx

    gl = lgt[0:EPG]
    gmask = row < float(N_GROUPS)
    gmax, gidx = first_max(gl, gmask)
    g_val = 1.0 / jnp.sum(jnp.where(gmask, jnp.exp(gl - gmax), 0.0), axis=0, keepdims=True)
    sel = jnp.zeros((EPG, tm), f32)
    for g in range(N_GROUPS):
        sel = jnp.where(gidx == float(g), lgt[EPG * (g + 1):EPG * (g + 2)], sel)
    every = row >= 0.0
    v1, e1 = first_max(sel, every)
    v2, e2 = first_max(sel, row != e1)
    ex = jnp.exp(v2 - v1)
    w1 = g_val * (1.0 / (1.0 + ex))
    w2 = g_val * (ex / (1.0 + ex))
    first_low = e1 < e2
    ea = jnp.where(first_low, e1, e2)
    eb = jnp.where(first_low, e2, e1)
    ga = jnp.where(first_low, w1, w2)
    gb = jnp.where(first_low, w2, w1)
    pidx = ea * float(EPG - 1) - ea * (ea - 1.0) * 0.5 + (eb - ea - 1.0)
    bucket = gidx * float(N_PAIRS) + pidx

    bk_ref[...] = jnp.broadcast_to(bucket, (EPG, tm))
    gates = jnp.where(row == 0.0, ga, jnp.where(row == 1.0, gb, 0.0))
    gates = jnp.concatenate([gates, jnp.zeros((LANES - EPG, tm), f32)], axis=0)
    h2_ref[:, D:] = jnp.transpose(gates)


def _mid(x2, of2, os2, w_out, g1, b1, wq, kx, vx, wxo, g2, b2, wr2, br, alpha, S, tm=1024):
    T = x2.shape[0]
    M = kx.shape[0] // (T // S)
    per_b = S // tm
    row = lambda w: pl.BlockSpec((tm, w), lambda i: (i, 0))
    full = lambda a: pl.BlockSpec(a.shape, lambda i: (0,) * a.ndim)
    kvspec = pl.BlockSpec((M, D), lambda i: (i // per_b, 0))
    kernel = functools.partial(_mid_kernel, alpha=alpha)
    return pl.pallas_call(
        kernel,
        grid=(T // tm,),
        in_specs=[row(D), row(512), row(512), full(w_out), full(g1), full(b1), full(wq),
                  kvspec, kvspec, full(wxo), full(g2), full(b2), full(wr2), full(br)],
        out_specs=[row(XW), pl.BlockSpec((8, tm), lambda i: (0, i))],
        out_shape=[jax.ShapeDtypeStruct((T, XW), f32), jax.ShapeDtypeStruct((8, T), f32)],
        scratch_shapes=[pltpu.VMEM((tm, D), bf16)],
        compiler_params=_cparams(("parallel",)),
        name="mid",
    )(x2, of2, os2, w_out, g1, b1, wq, kx, vx, wxo, g2, b2, wr2, br)


def _rank_kernel(bk_ref, rank_ref, cnt_ref, carry_ref, *, chunk):
    sub = 256

    @pl.when(pl.program_id(0) == 0)
    def _():
        carry_ref[...] = jnp.zeros_like(carry_ref)

    r = lax.broadcasted_iota(i32, (sub, sub), 0)
    c = lax.broadcasted_iota(i32, (sub, sub), 1)
    before = (r < c).astype(bf16)
    bid = lax.broadcasted_iota(i32, (LANES, sub), 0).astype(f32)
    carry = carry_ref[...]
    for j in range(chunk // sub):
        bk = bk_ref[0:1, j * sub:(j + 1) * sub]
        hit = bid == bk
        oh = jnp.where(hit, 1.0, 0.0)
        prior = _dot(oh.astype(bf16), before) + carry
        rank_ref[:, j * sub:(j + 1) * sub] = jnp.sum(jnp.where(hit, prior, 0.0), axis=0, keepdims=True)
        carry = carry + jnp.sum(oh, axis=1, keepdims=True)
    carry_ref[...] = carry
    cnt_ref[...] = carry


def _rank(bk8, chunk=2048):
    T = bk8.shape[1]
    kernel = functools.partial(_rank_kernel, chunk=chunk)
    return pl.pallas_call(
        kernel, grid=(T // chunk,),
        in_specs=[pl.BlockSpec((8, chunk), lambda i: (0, i))],
        out_specs=[pl.BlockSpec((1, chunk), lambda i: (0, i)),
                   pl.BlockSpec((LANES, 1), lambda i: (0, 0))],
        out_shape=[jax.ShapeDtypeStruct((1, T), f32), jax.ShapeDtypeStruct((LANES, 1), f32)],
        scratch_shapes=[pltpu.VMEM((LANES, 1), f32)],
        compiler_params=_cparams(("arbitrary",)), name="rank",
    )(bk8)


def _dest_kernel(bk_ref, rank_ref, ps_ref, dest_ref):
    chunk = bk_ref.shape[1]
    bid = lax.broadcasted_iota(i32, (LANES, chunk), 0).astype(f32)
    start = jnp.sum(jnp.where(bid == bk_ref[0:1, :], ps_ref[...], 0.0), axis=0, keepdims=True)
    dest_ref[...] = (start + rank_ref[...]).astype(i32)


def _dest(bk8, rank, ps_col, chunk=2048):
    T = bk8.shape[1]
    return pl.pallas_call(
        _dest_kernel, grid=(T // chunk,),
        in_specs=[pl.BlockSpec((8, chunk), lambda i: (0, i)), pl.BlockSpec((1, chunk), lambda i: (0, i)),
                  pl.BlockSpec((LANES, 1), lambda i: (0, 0))],
        out_specs=pl.BlockSpec((1, chunk), lambda i: (0, i)),
        out_shape=jax.ShapeDtypeStruct((1, T), i32),
        compiler_params=_cparams(("parallel",)), name="dest",
    )(bk8, rank, ps_col)


def _sc_invert(dest, n_rows):
    T = dest.shape[0]
    assert T & (T - 1) == 0
    lanes = SC_LANES
    mesh = plsc.VectorSubcoreMesh(core_axis_name="core", subcore_axis_name="subcore",
                                  num_cores=SC_CORES, num_subcores=SC_SUBCORES)

    @functools.partial(pl.kernel, out_type=jax.ShapeDtypeStruct((n_rows,), i32), mesh=mesh,
                       scratch_types=[pltpu.VMEM((T,), i32), pltpu.VMEM((n_rows,), i32)],
                       compiler_params=pltpu.CompilerParams(needs_layout_passes=False),
                       name="sc_invert")
    def k(dest_hbm, out_hbm, dest_v, table_v):
        wid = lax.axis_index("subcore") * SC_CORES + lax.axis_index("core")

        @pl.when(wid == 0)
        def _():
            pltpu.sync_copy(dest_hbm, dest_v)
            lane = lax.iota(i32, lanes)

            @pl.loop(0, n_rows // lanes)
            def _(j):
                off = pl.multiple_of(j * lanes, lanes)
                table_v[pl.ds(off, lanes)] = (lane + off) & (T - 1)


            @pl.loop(0, T // lanes)
            def _(j):
                off = pl.multiple_of(j * lanes, lanes)
                plsc.store_scatter(table_v, [dest_v[pl.ds(off, lanes)]], lane + off)

            pltpu.sync_copy(table_v, out_hbm)

    return k(dest)


def _sc_gather_rows(idx, src, chunk=SC_GATHER_ROWS):
    n = idx.shape[0]
    w = src.shape[1]
    workers = SC_CORES * SC_SUBCORES
    per_worker = n // workers
    mesh = plsc.VectorSubcoreMesh(core_axis_name="core", subcore_axis_name="subcore",
                                  num_cores=SC_CORES, num_subcores=SC_SUBCORES)

    n_chunks = per_worker // chunk
    assert n_chunks % 2 == 0

    @functools.partial(pl.kernel, out_type=jax.ShapeDtypeStruct((n, w), src.dtype), mesh=mesh,
                       scratch_types=[pltpu.VMEM((per_worker,), i32), pltpu.VMEM((2, chunk, w), src.dtype),
                                      pltpu.SemaphoreType.DMA((2,)), pltpu.SemaphoreType.DMA((2,))],
                       name="sc_gather_rows")
    def k(src_hbm, idx_hbm, out_hbm, idx_v, rows_v, gsem, wsem):
        wid = lax.axis_index("subcore") * SC_CORES + lax.axis_index("core")
        base = wid * per_worker
        pltpu.sync_copy(idx_hbm.at[pl.ds(base, per_worker)], idx_v)

        def gather(c, slot):
            rows = idx_v.at[pl.ds(pl.multiple_of(c * chunk, chunk), chunk)]
            return pltpu.make_async_copy(src_hbm.at[rows], rows_v.at[slot], gsem.at[slot])

        def write(c, slot):
            out = out_hbm.at[pl.ds(pl.multiple_of(base + c * chunk, chunk), chunk)]
            return pltpu.make_async_copy(rows_v.at[slot], out, wsem.at[slot])

        gather(0, 0).start()

        @pl.loop(0, n_chunks // 2)
        def _(pair):
            c = 2 * pair

            @pl.when(pair > 0)
            def _():
                write(c - 1, 1).wait()

            gather(c + 1, 1).start()
            gather(c, 0).wait()
            write(c, 0).start()

            @pl.when(c + 2 < n_chunks)
            def _():
                write(c, 0).wait()
                gather(c + 2, 0).start()

            gather(c + 1, 1).wait()
            write(c + 1, 1).start()

        write(n_chunks - 2, 0).wait()
        write(n_chunks - 1, 1).wait()

    return k(src, idx)


def _expert_kernel(grp_ref, ea_ref, eb_ref, used_ref, xs_ref, wg_ref, wu_ref, wd_ref, g_ref, b_ref, y_ref,
                   z_ref, *, alpha):
    del grp_ref
    n = pl.program_id(0)
    blk = jnp.minimum(n, pl.num_programs(0) - 2)
    used = used_ref[0]

    @pl.when(n == 0)
    def _():
        z_ref[...] = jnp.zeros_like(z_ref)

    @pl.when(n <= used)
    def _():
        y_ref[...] = _layer_norm(z_ref[...], g_ref[...], b_ref[...])
        h2 = xs_ref[:, :D]
        x = h2.astype(bf16)

        def expert(e):
            a = _dot(x, wg_ref[0, e])
            u = _dot(x, wu_ref[0, e])
            act = a * (1.0 / (1.0 + jnp.exp(-a))) * u
            return _dot(act.astype(bf16), wd_ref[0, e])

        ga = xs_ref[:, D:D + 1]
        gb = xs_ref[:, D + 1:D + 2]
        z_ref[...] = alpha * h2 + ga * expert(ea_ref[blk]) + gb * expert(eb_ref[blk])

    @pl.when(n > used)
    def _():
        y_ref[...] = jnp.zeros_like(y_ref)


def _expert_kernel_into(grp_ref, ea_ref, eb_ref, used_ref, xs_ref, wg_ref, wu_ref, wd_ref, g_ref, b_ref,
                        ys_ref, y_ref, z_ref, *, alpha):
    del ys_ref
    _expert_kernel(grp_ref, ea_ref, eb_ref, used_ref, xs_ref, wg_ref, wu_ref, wd_ref, g_ref, b_ref, y_ref,
                   z_ref, alpha=alpha)


def _experts(grp, ea, eb, used, xs, wg, wu, wd, ln_g, ln_b, alpha, ys, first_block, total_rows):
    nblk = xs.shape[0] // ROW_BLOCK

    def xmap(n, grp, ea, eb, used):
        return (jnp.maximum(jnp.minimum(n, used[0] - 1), 0), 0)

    gmap = lambda n, grp, ea, eb, used: (grp[jnp.minimum(n, nblk - 1)], 0, 0, 0)
    gspec = lambda w: pl.BlockSpec((1,) + w.shape[1:], gmap, pipeline_mode=pl.Buffered(1))
    vec = pl.BlockSpec((1, D), lambda n, grp, ea, eb, used: (0, 0))
    in_specs = [pl.BlockSpec((ROW_BLOCK, XW), xmap), gspec(wg), gspec(wu), gspec(wd), vec, vec]
    operands = [grp, ea, eb, used, xs, wg, wu, wd, ln_g, ln_b]
    aliases = {}
    body = _expert_kernel
    if ys is not None:
        in_specs.append(pl.BlockSpec(memory_space=pl.ANY))
        aliases = {len(operands): 0}
        operands.append(ys)
        body = _expert_kernel_into
    grid_spec = pltpu.PrefetchScalarGridSpec(
        num_scalar_prefetch=4, grid=(nblk + 1,), in_specs=in_specs,
        out_specs=pl.BlockSpec((ROW_BLOCK, D),
                               lambda n, grp, ea, eb, used: (jnp.maximum(n - 1, 0) + first_block, 0)),
        scratch_shapes=[pltpu.VMEM((ROW_BLOCK, D), f32)],
    )
    return pl.pallas_call(
        functools.partial(body, alpha=alpha), grid_spec=grid_spec,
        out_shape=jax.ShapeDtypeStruct((total_rows, D), f32),
        input_output_aliases=aliases,
        compiler_params=_cparams(("arbitrary",)), name="experts",
    )(*operands)


def _pair_tables():
    ea = np.zeros((LANES,), np.int32)
    eb = np.zeros((LANES,), np.int32)
    for g in range(N_GROUPS):
        k = 0
        for a in range(EPG):
            for b in range(a + 1, EPG):
                ea[g * N_PAIRS + k] = a
                eb[g * N_PAIRS + k] = b
                k += 1
    return ea, eb


_PAIR_A, _PAIR_B = _pair_tables()


def _layer(h, mem, positions, w_in, b_forget, sinks, w_mix_out, ln_mix_g, ln_mix_b,
           w_xq, w_xkv, w_xout, ln_x_g, ln_x_b, w_rg, b_rg, w_re, b_re,
           w_eg, w_eu, w_ed, ln_f_g, ln_f_b, alpha):
    B, S, _ = h.shape
    T = B * S
    x2 = h.reshape(T, D)

    o = np.cumsum((0, FOX_W, FOX_W, FOX_W, N_FOX, SWA_Q_W, SWA_KV_W, SWA_KV_W))
    w_qf, w_kf, w_vf, w_fl, w_qs, w_ks, w_vs = (w_in[:, o[i]:o[i + 1]] for i in range(7))
    def regroup(a, axis):
        shp = a.shape
        a = jnp.moveaxis(a, axis, 0).reshape(N_SWA_KV, N_SWA // N_SWA_KV, HD, -1)
        return jnp.moveaxis(jnp.swapaxes(a, 0, 1).reshape(N_SWA * HD, -1), 0, axis).reshape(shp)

    w_all = jnp.concatenate([w_kf, regroup(w_qs, 1), w_ks, w_vs], axis=1).astype(bf16)
    wqt = w_qf.T.astype(bf16)
    wvt = w_vf.T.astype(bf16)
    wfl = w_fl.T.astype(bf16)
    bfc = b_forget.reshape(N_FOX, 1).astype(f32)
    half = HD // 2
    inv_freq = ROPE_THETA ** (-jnp.arange(half, dtype=f32) / half)
    per_row = LANES // half
    invf = jnp.tile(inv_freq, per_row).reshape(1, LANES)
    pos4 = jnp.repeat(positions.reshape(T // per_row, per_row).astype(i32), half, axis=1)
    cos, sin = (t.reshape(T, half) for t in _rope_table(pos4, invf))
    w_out = jnp.concatenate([w_mix_out[:FOX_W], regroup(w_mix_out[FOX_W:], 0)], axis=0).astype(bf16)

    tq = 512
    (qt, kf, vt, qs, ks, vs, lf, qn, kn), (eg16, eu16, ed16) = _in_proj(
        x2, cos, sin, w_all, wqt, wvt, wfl, bfc, w_eg, w_eu, w_ed, tq)
    c, ca = _cumsum(lf, S)
    c4 = c.reshape(N_FOX // 2, 2, T // tq, tq)
    r3 = lambda a: a.reshape(B, S, a.shape[-1])
    o_fox = _fox(_fox_first_blocks(qn, kn, c, B, S, tq), qt, r3(kf), r3(ca), vt, c4, B, S, tq)
    o_swa = _swa(sinks.astype(f32) * LOG2E, r3(qs), r3(ks), r3(vs), B, S)

    kx, vx = _kvproj(mem.reshape(-1, D), w_xkv.astype(bf16))

    gpad = EPG - N_GROUPS
    wr = jnp.concatenate([jnp.pad(w_rg, ((0, 0), (0, gpad))),
                          jnp.transpose(w_re, (1, 0, 2)).reshape(D, N_EXPERTS)], axis=1)
    wr = jnp.pad(wr, ((0, 0), (0, LANES - wr.shape[1]))).astype(f32)
    wrh = wr.astype(bf16)
    wr2 = jnp.concatenate([wrh, (wr - wrh.astype(f32)).astype(bf16)], axis=1)
    br = jnp.pad(jnp.concatenate([jnp.pad(b_rg, (0, gpad)), b_re.reshape(-1)]), (0, LANES - EPG - N_EXPERTS))
    br = br.reshape(1, LANES).astype(f32)
    v2 = lambda a: a.reshape(1, D).astype(f32)
    h2x, bk8 = _mid(x2, o_fox.reshape(T, FOX_W), o_swa.reshape(T, SWA_Q_W), w_out,
                    v2(ln_mix_g), v2(ln_mix_b), w_xq.astype(bf16), kx, vx, w_xout.astype(bf16),
                    v2(ln_x_g), v2(ln_x_b), wr2, br, alpha, S)

    rank, cnt = _rank(bk8)
    counts = cnt[:, 0].astype(i32)
    padded = ((counts + ROW_BLOCK - 1) // ROW_BLOCK) * ROW_BLOCK
    pad_end = jnp.cumsum(padded)
    pad_start = (pad_end - padded).astype(i32)
    step = np.lcm(MOE_CHUNKS * SC_CORES * SC_SUBCORES * 2 * SC_GATHER_ROWS, MOE_CHUNKS * ROW_BLOCK)
    P = int(-(-(T + N_BUCKETS * ROW_BLOCK) // step) * step)
    nblk = P // ROW_BLOCK
    used = (pad_end[-1] // ROW_BLOCK).astype(i32).reshape(1)
    blk_row = jnp.arange(nblk, dtype=i32)[:, None] * ROW_BLOCK
    blk_bucket = jnp.minimum(jnp.sum((pad_end[None, :] <= blk_row).astype(i32), axis=1), N_BUCKETS - 1)
    pick = (blk_bucket[:, None] == jnp.arange(LANES, dtype=i32)[None, :]).astype(i32)
    blk_a = jnp.sum(pick * jnp.asarray(_PAIR_A)[None, :], axis=1)
    blk_b = jnp.sum(pick * jnp.asarray(_PAIR_B)[None, :], axis=1)
    blk_g = blk_bucket // N_PAIRS
    by_group = lambda w: w.reshape((N_GROUPS, EPG) + w.shape[1:])

    dest = _dest(bk8, rank, pad_start.astype(f32).reshape(LANES, 1))[0]
    row_tok = _sc_invert(dest, P)
    cblk = nblk // MOE_CHUNKS
    ys = None
    for cidx in range(MOE_CHUNKS):
        lo = cidx * cblk
        xs = _sc_gather_rows(row_tok[lo * ROW_BLOCK:(lo + cblk) * ROW_BLOCK], h2x)
        used_c = jnp.clip(used - lo, 0, cblk)
        ys = _experts(blk_g[lo:lo + cblk], blk_a[lo:lo + cblk], blk_b[lo:lo + cblk], used_c, xs,
                      by_group(eg16), by_group(eu16), by_group(ed16), v2(ln_f_g), v2(ln_f_b), alpha,
                      ys, lo, P)
    return _sc_gather_rows(dest, ys).reshape(B, S, D)


def kernel(x, mem, positions, w_in, b_forget, sinks, w_mix_out, ln_mix_g, ln_mix_b, w_xq, w_xkv, w_xout,
           ln_x_g, ln_x_b, w_route_group, b_route_group, w_route_expert, b_route_expert,
           w_exp_gate, w_exp_up, w_exp_down, ln_ffn_g, ln_ffn_b):
    depth = w_in.shape[0]
    alpha = (2.0 * depth) ** 0.25
    h = x
    for l in range(depth):
        h = _layer(h, mem, positions, w_in[l], b_forget[l], sinks[l], w_mix_out[l], ln_mix_g[l], ln_mix_b[l],
                   w_xq[l], w_xkv[l], w_xout[l], ln_x_g[l], ln_x_b[l], w_route_group[l], b_route_group[l],
                   w_route_expert[l], b_route_expert[l], w_exp_gate[l], w_exp_up[l], w_exp_down[l],
                   ln_ffn_g[l], ln_ffn_b[l], alpha)
    return h
```

```python
import functools

import jax
import jax.numpy as jnp
import numpy as np
from jax import lax
from jax.experimental import pallas as pl
from jax.experimental.pallas import tpu as pltpu
from jax.experimental.pallas import tpu_sc as plsc

f32 = jnp.float32
bf16 = jnp.bfloat16
i32 = jnp.int32

D = 1024
HD = 64
N_FOX = 8
N_SWA = 8
N_SWA_KV = 2
FOX_W = 512
SWA_Q_W = 512
SWA_KV_W = 128
WINDOW = 128
ROPE_THETA = 10000.0
N_XH = 4
XHD = 256
N_GROUPS = 4
EPG = 8
N_EXPERTS = 32
D_EXPERT = 512
LN_EPS = 1e-5
NEG = -1e30
LOG2E = 1.4426950408889634
L_ROW = (HD, 0)
SKIP_LOG2 = 160.0

SC_CORES = 2
SC_SUBCORES = 16
SC_LANES = 16
SC_GATHER_ROWS = 32
MOE_CHUNKS = 4
LANES = 128
ROW_BLOCK = 128
N_PAIRS = EPG * (EPG - 1) // 2
N_BUCKETS = N_GROUPS * N_PAIRS
XW = D + LANES
VMEM_LIMIT = 56 * 1024 * 1024


def _cparams(sem):
    return pltpu.CompilerParams(dimension_semantics=sem, vmem_limit_bytes=VMEM_LIMIT)


def _layer_norm(v, g, b):
    mu = jnp.mean(v, axis=-1, keepdims=True)
    c = v - mu
    var = jnp.mean(c * c, axis=-1, keepdims=True)
    return c * lax.rsqrt(var + LN_EPS) * g + b


def _dot(a, b):
    return jnp.dot(a, b, preferred_element_type=f32)


def _dot_nt(a, b):
    return lax.dot_general(a, b, (((1,), (1,)), ((), ())), preferred_element_type=f32)


def _rope_table_kernel(pos_ref, invf_ref, cos_ref, sin_ref):
    ang = pos_ref[...].astype(f32) * invf_ref[...]
    cos_ref[...] = jnp.cos(ang)
    sin_ref[...] = jnp.sin(ang)


def _rope_table(pos4, invf, rows=1024):
    R = pos4.shape[0]
    blk = pl.BlockSpec((rows, LANES), lambda i: (i, 0))
    return pl.pallas_call(
        _rope_table_kernel, grid=(R // rows,),
        in_specs=[blk, pl.BlockSpec((1, LANES), lambda i: (0, 0))], out_specs=[blk, blk],
        out_shape=[jax.ShapeDtypeStruct((R, LANES), f32)] * 2,
        compiler_params=_cparams(("parallel",)), name="rope_table",
    )(pos4, invf)


def _inproj_kernel(x_ref, cos_ref, sin_ref, w_ref, wqt_ref, wvt_ref, wfl_ref, bf_ref, ind_ref,
                   eg_ref, eu_ref, ed_ref,
                   qt_ref, kf_ref, vt_ref, qs_ref, ks_ref, vs_ref, lf_ref, qn_ref, kn_ref,
                   egb_ref, eub_ref, edb_ref):
    tm = x_ref.shape[0]
    xb = x_ref[...].astype(bf16)
    egb_ref[...] = eg_ref[...].astype(bf16)
    eub_ref[...] = eu_ref[...].astype(bf16)
    edb_ref[...] = ed_ref[...].astype(bf16)

    def proj(lo, hi):
        return _dot(xb, w_ref[:, lo:hi])

    qv = _dot_nt(wqt_ref[...], xb) * (0.125 * LOG2E)
    qt_ref[0] = qv.astype(bf16)
    vt_ref[0] = _dot_nt(wvt_ref[...], xb).astype(bf16)
    kv = proj(0, 512)
    kf_ref[...] = kv.astype(bf16)
    q2 = jnp.sum((qv * qv).reshape(N_FOX, HD, tm), axis=1)
    qn_ref[0] = jnp.broadcast_to(jnp.max(q2, axis=1, keepdims=True), (N_FOX, LANES))
    k2 = _dot((kv * kv).astype(bf16), ind_ref[...])
    kn_ref[0] = jnp.broadcast_to(jnp.max(k2, axis=0, keepdims=True), (N_FOX, LANES))

    half = HD // 2
    reps = LANES // half
    quarter = pl.program_id(0) // (pl.num_programs(0) // reps)

    def spread(tab):
        pick = tab[:, :half]
        for k in range(1, reps):
            pick = jnp.where(quarter == k, tab[:, k * half:(k + 1) * half], pick)
        return jnp.concatenate([pick] * reps, axis=1)

    cos = spread(cos_ref[...])
    sin = spread(sin_ref[...])
    lane = lax.broadcasted_iota(i32, (tm, LANES), 1)
    lo_half = (lane % HD) < (HD // 2)
    sin_s = jnp.where(lo_half, -sin, sin)

    def rope(z):
        rot = jnp.where(lo_half, pltpu.roll(z, LANES - HD // 2, 1), pltpu.roll(z, HD // 2, 1))
        return z * cos + rot * sin_s

    zq = proj(512, 1024)
    for g in range(4):
        sl = slice(g * LANES, (g + 1) * LANES)
        qs_ref[:, sl] = (rope(zq[:, sl]) * (0.125 * LOG2E)).astype(bf16)
    zkv = proj(1024, 1280)
    ks_ref[...] = rope(zkv[:, :SWA_KV_W]).astype(bf16)
    vs_ref[...] = zkv[:, SWA_KV_W:].astype(bf16)

    z = _dot_nt(wfl_ref[...], xb) + bf_ref[...]
    lf_ref[...] = jnp.minimum(z, 0.0) - jnp.log(1.0 + jnp.exp(-jnp.abs(z)))


def _in_proj(x2, cos, sin, w_all, wqt, wvt, wfl, bfc, w_eg, w_eu, w_ed, tm):
    T = x2.shape[0]
    steps = T // tm
    row = lambda w: pl.BlockSpec((tm, w), lambda i: (i, 0))
    full = lambda a: pl.BlockSpec(a.shape, lambda i: (0,) * a.ndim)
    fmaj = pl.BlockSpec((1, FOX_W, tm), lambda i: (i, 0, 0))
    flat = [w.reshape(-1, w.shape[-1]) for w in (w_eg, w_eu, w_ed)]
    slices = [pl.BlockSpec((w.shape[0] // steps, w.shape[1]), lambda i: (i, 0)) for w in flat]
    ind = jnp.asarray(np.arange(FOX_W)[:, None] // HD == np.arange(LANES)[None, :], bf16)
    nrm = pl.BlockSpec((1, N_FOX, LANES), lambda i: (i, 0, 0))
    tab_steps = cos.shape[0] // tm
    tab = pl.BlockSpec((tm, LANES), lambda i: (i % tab_steps, 0))
    outs = pl.pallas_call(
        _inproj_kernel,
        grid=(steps,),
        in_specs=[row(D), tab, tab, full(w_all), full(wqt), full(wvt), full(wfl), full(bfc), full(ind)] + slices,
        out_specs=[fmaj, row(512), fmaj, row(512), row(128), row(128),
                   pl.BlockSpec((N_FOX, tm), lambda i: (0, i)), nrm, nrm] + slices,
        out_shape=[jax.ShapeDtypeStruct((steps, FOX_W, tm), bf16), jax.ShapeDtypeStruct((T, 512), bf16),
                   jax.ShapeDtypeStruct((steps, FOX_W, tm), bf16), jax.ShapeDtypeStruct((T, 512), bf16),
                   jax.ShapeDtypeStruct((T, 128), bf16), jax.ShapeDtypeStruct((T, 128), bf16),
                   jax.ShapeDtypeStruct((N_FOX, T), f32),
                   jax.ShapeDtypeStruct((steps, N_FOX, LANES), f32), jax.ShapeDtypeStruct((steps, N_FOX, LANES), f32)]
        + [jax.ShapeDtypeStruct(w.shape, bf16) for w in flat],
        compiler_params=_cparams(("parallel",)),
        name="in_proj",
    )(x2, cos, sin, w_all, wqt, wvt, wfl, bfc, ind, *flat)
    experts_bf16 = [o.reshape(w.shape) for o, w in zip(outs[9:], (w_eg, w_eu, w_ed))]
    return outs[:9], experts_bf16


def _cumsum_kernel(lf_ref, c_ref, ca_ref):
    S = lf_ref.shape[1]
    ch = 256
    r = lax.broadcasted_iota(i32, (ch, ch), 0)
    c = lax.broadcasted_iota(i32, (ch, ch), 1)
    tri = (r <= c).astype(f32)
    eye = (r == c).astype(bf16)
    stacked = jnp.concatenate([lf_ref[:, j * ch:(j + 1) * ch] for j in range(S // ch)], axis=0)
    local = jnp.dot(stacked, tri, precision=lax.Precision.HIGHEST, preferred_element_type=f32)
    carry = jnp.zeros((N_FOX, 1), f32)
    for j in range(S // ch):
        cc = local[j * N_FOX:(j + 1) * N_FOX] + carry
        carry = cc[:, ch - 1:ch]
        c2 = cc * LOG2E
        c_ref[:, j * ch:(j + 1) * ch] = c2
        neg = -c2
        hi = neg.astype(bf16)
        r1 = neg - hi.astype(f32)
        mid = r1.astype(bf16)
        lo = (r1 - mid.astype(f32)).astype(bf16)
        terms = jnp.concatenate([hi, mid, lo, jnp.zeros((LANES - 3 * N_FOX, ch), bf16)], axis=0)
        ca_ref[j * ch:(j + 1) * ch, :] = _dot_nt(eye, terms).astype(bf16)


def _cumsum(lf, S):
    T = lf.shape[1]
    spec = pl.BlockSpec((N_FOX, S), lambda b: (0, b))
    return pl.pallas_call(
        _cumsum_kernel, grid=(T // S,), in_specs=[spec],
        out_specs=[spec, pl.BlockSpec((S, LANES), lambda b: (b, 0))],
        out_shape=[jax.ShapeDtypeStruct((N_FOX, T), f32), jax.ShapeDtypeStruct((T, LANES), bf16)],
        compiler_params=_cparams(("parallel",)), name="cumsum",
    )(lf)


def _fox_kernel(j0_ref, qt_ref, k_ref, ca_ref, vt_ref, c_ref, o_ref, t0_ref, t1_ref, *, tq):
    hp = pl.program_id(1)
    i = pl.program_id(2)
    qt = qt_ref[0]
    row = lax.broadcasted_iota(i32, (LANES, tq), 0)
    is_a = row < HD
    zero = jnp.zeros_like(qt)
    q_ops = []
    for h in range(2):
        ones = jnp.where(((row & 7) == 2 * hp + h) & (row < 3 * N_FOX), 1.0, 0.0).astype(bf16)
        qh = jnp.where(is_a, qt, zero) if h == 0 else jnp.where(is_a, zero, qt)
        q_ops.append(jnp.concatenate([qh, ones], axis=0))
    kr = lax.broadcasted_iota(i32, (tq, tq), 0)
    qc = lax.broadcasted_iota(i32, (tq, tq), 1)
    causal = kr <= qc
    cq = [c_ref[0, h, pl.ds(i, 1), :] for h in range(2)]

    def scores(j, t_ref):
        off = pl.multiple_of(j * tq, tq)
        kblk = jnp.concatenate([k_ref[0, pl.ds(off, tq), :], ca_ref[0, pl.ds(off, tq), :]], axis=1)
        for h in range(2):
            t_ref[h] = _dot(kblk, q_ops[h])

    keep = [jnp.where(is_a, 1.0, 0.0).astype(bf16), jnp.where(is_a, 0.0, 1.0).astype(bf16)]
    ones_row = [jnp.where(row == L_ROW[h], 1.0, 0.0).astype(bf16) for h in range(2)]

    def softmax_pv(j, t_ref, carry, masked):
        vt = vt_ref[j]
        vts = [vt * keep[h] + ones_row[h] for h in range(2)]
        new = []
        for h in range(2):
            m, acc = carry[h]
            t = t_ref[h]
            if masked:
                t = jnp.where(causal, t, NEG)
            m_new = jnp.maximum(m, jnp.max(t, axis=0, keepdims=True) + cq[h])
            alpha = jnp.exp2(m - m_new)
            p = jnp.exp2(t + (cq[h] - m_new))
            acc = alpha * acc + _dot(vts[h], p.astype(bf16))
            new.append((m_new, acc))
        return tuple(new)

    j0 = j0_ref[(pl.program_id(0) * pl.num_programs(1) + hp) * pl.num_programs(2) + i]
    n_full = i - j0

    def pair(k, carry):
        j = j0 + 2 * k
        scores(j + 1, t1_ref)
        carry = softmax_pv(j, t0_ref, carry, False)
        scores(j + 2, t0_ref)
        return softmax_pv(j + 1, t1_ref, carry, False)

    def odd_tail(carry):
        scores(i, t1_ref)
        carry = softmax_pv(i - 1, t0_ref, carry, False)
        return softmax_pv(i, t1_ref, carry, True)

    def even_tail(carry):
        return softmax_pv(i, t0_ref, carry, True)

    init = tuple((jnp.full((1, tq), NEG, f32), jnp.zeros((LANES, tq), f32)) for _ in range(2))
    scores(j0, t0_ref)
    carry = lax.fori_loop(0, n_full // 2, pair, init)
    (_, acca), (_, accb) = lax.cond(n_full % 2 == 1, odd_tail, even_tail, carry)
    la = acca[L_ROW[0]:L_ROW[0] + 1, :]
    lb = accb[L_ROW[1]:L_ROW[1] + 1, :]
    ot = jnp.where(is_a, acca / la, accb / lb)
    o_ref[0] = jnp.transpose(ot).astype(bf16)


def _fox_first_blocks(qn, kn, c, B, S, tq):
    nq = S // tq
    qmax = jnp.sqrt(qn[:, :, 0].reshape(B, nq, N_FOX))
    kmax = jnp.sqrt(kn[:, 0, :N_FOX].reshape(B, nq, N_FOX))
    qk = 1.02 * qmax[:, :, None, :] * (kmax[:, None, :, :] + kmax[:, :, None, :])
    cb = c.reshape(N_FOX, B, nq, tq)
    c_first = jnp.transpose(cb[..., 0], (1, 2, 0))
    c_last = jnp.transpose(cb[..., tq - 1], (1, 2, 0))
    gap = c_last[:, None, :, :] - c_first[:, :, None, :]
    blk = jnp.arange(nq)
    negligible = (qk - gap < -SKIP_LOG2) & (blk[None, :] < blk[:, None])[None, :, :, None]
    pair = jnp.all(negligible.reshape(B, nq, nq, N_FOX // 2, 2), axis=-1)
    j0 = jnp.sum(jnp.cumprod(pair.astype(i32), axis=2), axis=2)
    return jnp.transpose(j0, (0, 2, 1)).reshape(-1).astype(i32)


def _fox(j0, qt, kf, ca, vt, c4, B, S, tq):
    nq = S // tq
    kernel = functools.partial(_fox_kernel, tq=tq)
    return pl.pallas_call(
        kernel,
        grid=(B, N_FOX // 2, nq),
        in_specs=[
            pl.BlockSpec(memory_space=pltpu.SMEM),
            pl.BlockSpec((1, LANES, tq), lambda b, hp, i: (b * nq + i, hp, 0)),
            pl.BlockSpec((1, S, LANES), lambda b, hp, i: (b, 0, hp)),
            pl.BlockSpec((1, S, LANES), lambda b, hp, i: (b, 0, 0)),
            pl.BlockSpec((nq, LANES, tq), lambda b, hp, i: (b, hp, 0)),
            pl.BlockSpec((1, 2, nq, tq), lambda b, hp, i: (hp, 0, b, 0)),
        ],
        out_specs=pl.BlockSpec((1, tq, LANES), lambda b, hp, i: (b, i, hp)),
        out_shape=jax.ShapeDtypeStruct((B, S, FOX_W), bf16),
        scratch_shapes=[pltpu.VMEM((2, tq, tq), f32), pltpu.VMEM((2, tq, tq), f32)],
        compiler_params=_cparams(("parallel", "parallel", "arbitrary")),
        name="fox",
    )(j0, qt, kf, ca, vt, c4)


def _swa_kernel(sink_ref, q_ref, k_ref, v_ref, o_ref, *, tq):
    W = WINDOW
    nsub = tq // W
    n0 = pl.program_id(1) * nsub
    lane = lax.broadcasted_iota(i32, (W, LANES), 1)
    is0 = lane < HD
    qoff = lax.broadcasted_iota(i32, (4 * W, 2 * W), 0) % W
    cols = lax.broadcasted_iota(i32, (4 * W, 2 * W), 1)
    bias_mid = jnp.where((cols - W <= qoff) & (qoff - (cols - W) < W), 0.0, NEG)
    bias_first = jnp.where(cols <= qoff, 0.0, NEG)
    rgrp = lax.broadcasted_iota(i32, (4 * W, 1), 0) // W
    for r in range(nsub):
        nb = n0 + r
        kstart = pl.multiple_of(jnp.maximum(nb * W - W, 0), W)
        ks = k_ref[0, pl.ds(kstart, 2 * W), :]
        vs = v_ref[0, pl.ds(kstart, 2 * W), :]
        bias = jnp.where(nb == 0, bias_first, bias_mid)
        outs = []
        for kv in range(2):
            keep = is0 if kv == 0 else jnp.logical_not(is0)
            parts = []
            for g in range(4):
                qg = q_ref[0, r * W:(r + 1) * W, g * LANES:(g + 1) * LANES]
                parts.append(jnp.where(keep, qg, jnp.zeros_like(qg)))
            qstack = jnp.concatenate(parts, axis=0)
            s = bias + _dot_nt(qstack, ks)
            sink = jnp.zeros((4 * W, 1), f32)
            for g in range(4):
                sink = jnp.where(rgrp == g, sink_ref[kv * 4 + g], sink)
            m = jnp.maximum(jnp.max(s, axis=1, keepdims=True), sink)
            e = jnp.exp2(s - m)
            den = jnp.sum(e, axis=1, keepdims=True) + jnp.exp2(sink - m)
            outs.append(_dot(e.astype(bf16), vs) / den)
        for g in range(4):
            og = jnp.where(is0, outs[0][g * W:(g + 1) * W], outs[1][g * W:(g + 1) * W])
            o_ref[0, r * W:(r + 1) * W, g * LANES:(g + 1) * LANES] = og.astype(bf16)


def _swa(sinks, qs, ks, vs, B, S, tq=512):
    kernel = functools.partial(_swa_kernel, tq=tq)
    return pl.pallas_call(
        kernel,
        grid=(B, S // tq),
        in_specs=[
            pl.BlockSpec(memory_space=pltpu.SMEM),
            pl.BlockSpec((1, tq, SWA_Q_W), lambda b, i: (b, i, 0)),
            pl.BlockSpec((1, S, SWA_KV_W), lambda b, i: (b, 0, 0)),
            pl.BlockSpec((1, S, SWA_KV_W), lambda b, i: (b, 0, 0)),
        ],
        out_specs=pl.BlockSpec((1, tq, SWA_Q_W), lambda b, i: (b, i, 0)),
        out_shape=jax.ShapeDtypeStruct((B, S, SWA_Q_W), bf16),
        compiler_params=_cparams(("parallel", "arbitrary")),
        name="swa",
    )(sinks, qs, ks, vs)


def _kvproj_kernel(m_ref, w_ref, k_ref, v_ref):
    mb = m_ref[...].astype(bf16)
    k_ref[...] = _dot(mb, w_ref[:, :D]).astype(bf16)
    v_ref[...] = _dot(mb, w_ref[:, D:]).astype(bf16)


def _kvproj(mem2, w_xkv, tm=512):
    R = mem2.shape[0]
    row = pl.BlockSpec((tm, D), lambda i: (i, 0))
    return pl.pallas_call(
        _kvproj_kernel, grid=(R // tm,),
        in_specs=[row, pl.BlockSpec(w_xkv.shape, lambda i: (0, 0))],
        out_specs=[row, row],
        out_shape=[jax.ShapeDtypeStruct((R, D), bf16)] * 2,
        compiler_params=_cparams(("parallel",)), name="kvproj",
    )(mem2, w_xkv)


def _mid_kernel(x_ref, of_ref, os_ref, wo_ref, g1_ref, b1_ref, wq_ref, k_ref, v_ref,
                wxo_ref, g2_ref, b2_ref, wr_ref, br_ref,
                h2_ref, bk_ref, oc_ref, *, alpha):
    tm = x_ref.shape[0]
    mix = _dot(of_ref[...], wo_ref[:FOX_W, :]) + _dot(os_ref[...], wo_ref[FOX_W:, :])
    h1 = _layer_norm(alpha * x_ref[...] + mix, g1_ref[...], b1_ref[...])

    q = (_dot(h1.astype(bf16), wq_ref[...]) * 0.0625).astype(bf16)
    for h in range(N_XH):
        sl = slice(h * XHD, (h + 1) * XHD)
        s = _dot_nt(q[:, sl], k_ref[:, sl])
        e = jnp.exp(s - jnp.max(s, axis=1, keepdims=True))
        p = e / jnp.sum(e, axis=1, keepdims=True)
        oc_ref[:, sl] = _dot(p.astype(bf16), v_ref[:, sl]).astype(bf16)
    xo = _dot(oc_ref[...], wxo_ref[...])
    h2 = _layer_norm(alpha * h1 + xo, g2_ref[...], b2_ref[...])
    h2_ref[:, :D] = h2

    hh = h2.astype(bf16)
    hl = (h2 - hh.astype(f32)).astype(bf16)
    hi_terms = _dot(hh, wr_ref[...])
    lg = hi_terms[:, :LANES] + _dot(hl, wr_ref[:, :LANES]) + hi_terms[:, LANES:] + br_ref[...]

    lgt = jnp.transpose(lg)
    row = lax.broadcasted_iota(i32, (EPG, tm), 0).astype(f32)
    big = float(EPG)

    def first_max(vals, mask):
        vm = jnp.where(mask, vals, NEG)
        top = jnp.max(vm, axis=0, keepdims=True)
        idx = jnp.min(jnp.where(mask & (vm == top), row, big), axis=0, keepdims=True)
        return top, idx

    gl = lgt[0:EPG]
    gmask = row < float(N_GROUPS)
    gmax, gidx = first_max(gl, gmask)
    g_val = 1.0 / jnp.sum(jnp.where(gmask, jnp.exp(gl - gmax), 0.0), axis=0, keepdims=True)
    sel = jnp.zeros((EPG, tm), f32)
    for g in range(N_GROUPS):
        sel = jnp.where(gidx == float(g), lgt[EPG * (g + 1):EPG * (g + 2)], sel)
    every = row >= 0.0
    v1, e1 = first_max(sel, every)
    v2, e2 = first_max(sel, row != e1)
    ex = jnp.exp(v2 - v1)
    w1 = g_val * (1.0 / (1.0 + ex))
    w2 = g_val * (ex / (1.0 + ex))
    first_low = e1 < e2
    ea = jnp.where(first_low, e1, e2)
    eb = jnp.where(first_low, e2, e1)
    ga = jnp.where(first_low, w1, w2)
    gb = jnp.where(first_low, w2, w1)
    pidx = ea * float(EPG - 1) - ea * (ea - 1.0) * 0.5 + (eb - ea - 1.0)
    bucket = gidx * float(N_PAIRS) + pidx

    bk_ref[...] = jnp.broadcast_to(bucket, (EPG, tm))
    gates = jnp.where(row == 0.0, ga, jnp.where(row == 1.0, gb, 0.0))
    gates = jnp.concatenate([gates, jnp.zeros((LANES - EPG, tm), f32)], axis=0)
    h2_ref[:, D:] = jnp.transpose(gates)


def _mid(x2, of2, os2, w_out, g1, b1, wq, kx, vx, wxo, g2, b2, wr2, br, alpha, S, tm=1024):
    T = x2.shape[0]
    M = kx.shape[0] // (T // S)
    per_b = S // tm
    row = lambda w: pl.BlockSpec((tm, w), lambda i: (i, 0))
    full = lambda a: pl.BlockSpec(a.shape, lambda i: (0,) * a.ndim)
    kvspec = pl.BlockSpec((M, D), lambda i: (i // per_b, 0))
    kernel = functools.partial(_mid_kernel, alpha=alpha)
    return pl.pallas_call(
        kernel,
        grid=(T // tm,),
        in_specs=[row(D), row(512), row(512), full(w_out), full(g1), full(b1), full(wq),
                  kvspec, kvspec, full(wxo), full(g2), full(b2), full(wr2), full(br)],
        out_specs=[row(XW), pl.BlockSpec((8, tm), lambda i: (0, i))],
        out_shape=[jax.ShapeDtypeStruct((T, XW), f32), jax.ShapeDtypeStruct((8, T), f32)],
        scratch_shapes=[pltpu.VMEM((tm, D), bf16)],
        compiler_params=_cparams(("parallel",)),
        name="mid",
    )(x2, of2, os2, w_out, g1, b1, wq, kx, vx, wxo, g2, b2, wr2, br)


def _rank_kernel(bk_ref, rank_ref, cnt_ref, carry_ref, *, chunk):
    sub = 256

    @pl.when(pl.program_id(0) == 0)
    def _():
        carry_ref[...] = jnp.zeros_like(carry_ref)

    r = lax.broadcasted_iota(i32, (sub, sub), 0)
    c = lax.broadcasted_iota(i32, (sub, sub), 1)
    before = (r < c).astype(bf16)
    bid = lax.broadcasted_iota(i32, (LANES, sub), 0).astype(f32)
    carry = carry_ref[...]
    for j in range(chunk // sub):
        bk = bk_ref[0:1, j * sub:(j + 1) * sub]
        hit = bid == bk
        oh = jnp.where(hit, 1.0, 0.0)
        prior = _dot(oh.astype(bf16), before) + carry
        rank_ref[:, j * sub:(j + 1) * sub] = jnp.sum(jnp.where(hit, prior, 0.0), axis=0, keepdims=True)
        carry = carry + jnp.sum(oh, axis=1, keepdims=True)
    carry_ref[...] = carry
    cnt_ref[...] = carry


def _rank(bk8, chunk=2048):
    T = bk8.shape[1]
    kernel = functools.partial(_rank_kernel, chunk=chunk)
    return pl.pallas_call(
        kernel, grid=(T // chunk,),
        in_specs=[pl.BlockSpec((8, chunk), lambda i: (0, i))],
        out_specs=[pl.BlockSpec((1, chunk), lambda i: (0, i)),
                   pl.BlockSpec((LANES, 1), lambda i: (0, 0))],
        out_shape=[jax.ShapeDtypeStruct((1, T), f32), jax.ShapeDtypeStruct((LANES, 1), f32)],
        scratch_shapes=[pltpu.VMEM((LANES, 1), f32)],
        compiler_params=_cparams(("arbitrary",)), name="rank",
    )(bk8)


def _dest_kernel(bk_ref, rank_ref, ps_ref, dest_ref):
    chunk = bk_ref.shape[1]
    bid = lax.broadcasted_iota(i32, (LANES, chunk), 0).astype(f32)
    start = jnp.sum(jnp.where(bid == bk_ref[0:1, :], ps_ref[...], 0.0), axis=0, keepdims=True)
    dest_ref[...] = (start + rank_ref[...]).astype(i32)


def _dest(bk8, rank, ps_col, chunk=2048):
    T = bk8.shape[1]
    return pl.pallas_call(
        _dest_kernel, grid=(T // chunk,),
        in_specs=[pl.BlockSpec((8, chunk), lambda i: (0, i)), pl.BlockSpec((1, chunk), lambda i: (0, i)),
                  pl.BlockSpec((LANES, 1), lambda i: (0, 0))],
        out_specs=pl.BlockSpec((1, chunk), lambda i: (0, i)),
        out_shape=jax.ShapeDtypeStruct((1, T), i32),
        compiler_params=_cparams(("parallel",)), name="dest",
    )(bk8, rank, ps_col)


def _sc_invert(dest, n_rows):
    T = dest.shape[0]
    assert T & (T - 1) == 0
    lanes = SC_LANES
    mesh = plsc.VectorSubcoreMesh(core_axis_name="core", subcore_axis_name="subcore",
                                  num_cores=SC_CORES, num_subcores=SC_SUBCORES)

    @functools.partial(pl.kernel, out_type=jax.ShapeDtypeStruct((n_rows,), i32), mesh=mesh,
                       scratch_types=[pltpu.VMEM((T,), i32), pltpu.VMEM((n_rows,), i32)],
                       compiler_params=pltpu.CompilerParams(needs_layout_passes=False),
                       name="sc_invert")
    def k(dest_hbm, out_hbm, dest_v, table_v):
        wid = lax.axis_index("subcore") * SC_CORES + lax.axis_index("core")

        @pl.when(wid == 0)
        def _():
            pltpu.sync_copy(dest_hbm, dest_v)
            lane = lax.iota(i32, lanes)

            @pl.loop(0, n_rows // lanes)
            def _(j):
                off = pl.multiple_of(j * lanes, lanes)
                table_v[pl.ds(off, lanes)] = (lane + off) & (T - 1)


            @pl.loop(0, T // lanes)
            def _(j):
                off = pl.multiple_of(j * lanes, lanes)
                plsc.store_scatter(table_v, [dest_v[pl.ds(off, lanes)]], lane + off)

            pltpu.sync_copy(table_v, out_hbm)

    return k(dest)


def _sc_gather_rows(idx, src, chunk=SC_GATHER_ROWS):
    n = idx.shape[0]
    w = src.shape[1]
    workers = SC_CORES * SC_SUBCORES
    per_worker = n // workers
    mesh = plsc.VectorSubcoreMesh(core_axis_name="core", subcore_axis_name="subcore",
                                  num_cores=SC_CORES, num_subcores=SC_SUBCORES)

    n_chunks = per_worker // chunk
    assert n_chunks % 2 == 0

    @functools.partial(pl.kernel, out_type=jax.ShapeDtypeStruct((n, w), src.dtype), mesh=mesh,
                       scratch_types=[pltpu.VMEM((per_worker,), i32), pltpu.VMEM((2, chunk, w), src.dtype),
                                      pltpu.SemaphoreType.DMA((2,)), pltpu.SemaphoreType.DMA((2,))],
                       name="sc_gather_rows")
    def k(src_hbm, idx_hbm, out_hbm, idx_v, rows_v, gsem, wsem):
        wid = lax.axis_index("subcore") * SC_CORES + lax.axis_index("core")
        base = wid * per_worker
        pltpu.sync_copy(idx_hbm.at[pl.ds(base, per_worker)], idx_v)

        def gather(c, slot):
            rows = idx_v.at[pl.ds(pl.multiple_of(c * chunk, chunk), chunk)]
            return pltpu.make_async_copy(src_hbm.at[rows], rows_v.at[slot], gsem.at[slot])

        def write(c, slot):
            out = out_hbm.at[pl.ds(pl.multiple_of(base + c * chunk, chunk), chunk)]
            return pltpu.make_async_copy(rows_v.at[slot], out, wsem.at[slot])

        gather(0, 0).start()

        @pl.loop(0, n_chunks // 2)
        def _(pair):
            c = 2 * pair

            @pl.when(pair > 0)
            def _():
                write(c - 1, 1).wait()

            gather(c + 1, 1).start()
            gather(c, 0).wait()
            write(c, 0).start()

            @pl.when(c + 2 < n_chunks)
            def _():
                write(c, 0).wait()
                gather(c + 2, 0).start()

            gather(c + 1, 1).wait()
            write(c + 1, 1).start()

        write(n_chunks - 2, 0).wait()
        write(n_chunks - 1, 1).wait()

    return k(src, idx)


def _expert_kernel(grp_ref, ea_ref, eb_ref, used_ref, xs_ref, wg_ref, wu_ref, wd_ref, g_ref, b_ref, y_ref,
                   z_ref, *, alpha):
    del grp_ref
    n = pl.program_id(0)
    blk = jnp.minimum(n, pl.num_programs(0) - 2)
    used = used_ref[0]

    @pl.when(n == 0)
    def _():
        z_ref[...] = jnp.zeros_like(z_ref)

    @pl.when(n <= used)
    def _():
        y_ref[...] = _layer_norm(z_ref[...], g_ref[...], b_ref[...])
        h2 = xs_ref[:, :D]
        x = h2.astype(bf16)

        def expert(e):
            a = _dot(x, wg_ref[0, e])
            u = _dot(x, wu_ref[0, e])
            act = a * (1.0 / (1.0 + jnp.exp(-a))) * u
            return _dot(act.astype(bf16), wd_ref[0, e])

        ga = xs_ref[:, D:D + 1]
        gb = xs_ref[:, D + 1:D + 2]
        z_ref[...] = alpha * h2 + ga * expert(ea_ref[blk]) + gb * expert(eb_ref[blk])

    @pl.when(n > used)
    def _():
        y_ref[...] = jnp.zeros_like(y_ref)


def _expert_kernel_into(grp_ref, ea_ref, eb_ref, used_ref, xs_ref, wg_ref, wu_ref, wd_ref, g_ref, b_ref,
                        ys_ref, y_ref, z_ref, *, alpha):
    del ys_ref
    _expert_kernel(grp_ref, ea_ref, eb_ref, used_ref, xs_ref, wg_ref, wu_ref, wd_ref, g_ref, b_ref, y_ref,
                   z_ref, alpha=alpha)


def _experts(grp, ea, eb, used, xs, wg, wu, wd, ln_g, ln_b, alpha, ys, first_block, total_rows):
    nblk = xs.shape[0] // ROW_BLOCK

    def xmap(n, grp, ea, eb, used):
        return (jnp.maximum(jnp.minimum(n, used[0] - 1), 0), 0)

    gmap = lambda n, grp, ea, eb, used: (grp[jnp.minimum(n, nblk - 1)], 0, 0, 0)
    gspec = lambda w: pl.BlockSpec((1,) + w.shape[1:], gmap, pipeline_mode=pl.Buffered(1))
    vec = pl.BlockSpec((1, D), lambda n, grp, ea, eb, used: (0, 0))
    in_specs = [pl.BlockSpec((ROW_BLOCK, XW), xmap), gspec(wg), gspec(wu), gspec(wd), vec, vec]
    operands = [grp, ea, eb, used, xs, wg, wu, wd, ln_g, ln_b]
    aliases = {}
    body = _expert_kernel
    if ys is not None:
        in_specs.append(pl.BlockSpec(memory_space=pl.ANY))
        aliases = {len(operands): 0}
        operands.append(ys)
        body = _expert_kernel_into
    grid_spec = pltpu.PrefetchScalarGridSpec(
        num_scalar_prefetch=4, grid=(nblk + 1,), in_specs=in_specs,
        out_specs=pl.BlockSpec((ROW_BLOCK, D),
                               lambda n, grp, ea, eb, used: (jnp.maximum(n - 1, 0) + first_block, 0)),
        scratch_shapes=[pltpu.VMEM((ROW_BLOCK, D), f32)],
    )
    return pl.pallas_call(
        functools.partial(body, alpha=alpha), grid_spec=grid_spec,
        out_shape=jax.ShapeDtypeStruct((total_rows, D), f32),
        input_output_aliases=aliases,
        compiler_params=_cparams(("arbitrary",)), name="experts",
    )(*operands)


def _pair_tables():
    ea = np.zeros((LANES,), np.int32)
    eb = np.zeros((LANES,), np.int32)
    for g in range(N_GROUPS):
        k = 0
        for a in range(EPG):
            for b in range(a + 1, EPG):
                ea[g * N_PAIRS + k] = a
                eb[g * N_PAIRS + k] = b
                k += 1
    return ea, eb


_PAIR_A, _PAIR_B = _pair_tables()


def _layer(h, mem, positions, w_in, b_forget, sinks, w_mix_out, ln_mix_g, ln_mix_b,
           w_xq, w_xkv, w_xout, ln_x_g, ln_x_b, w_rg, b_rg, w_re, b_re,
           w_eg, w_eu, w_ed, ln_f_g, ln_f_b, alpha):
    B, S, _ = h.shape
    T = B * S
    x2 = h.reshape(T, D)

    o = np.cumsum((0, FOX_W, FOX_W, FOX_W, N_FOX, SWA_Q_W, SWA_KV_W, SWA_KV_W))
    w_qf, w_kf, w_vf, w_fl, w_qs, w_ks, w_vs = (w_in[:, o[i]:o[i + 1]] for i in range(7))
    def regroup(a, axis):
        shp = a.shape
        a = jnp.moveaxis(a, axis, 0).reshape(N_SWA_KV, N_SWA // N_SWA_KV, HD, -1)
        return jnp.moveaxis(jnp.swapaxes(a, 0, 1).reshape(N_SWA * HD, -1), 0, axis).reshape(shp)

    w_all = jnp.concatenate([w_kf, regroup(w_qs, 1), w_ks, w_vs], axis=1).astype(bf16)
    wqt = w_qf.T.astype(bf16)
    wvt = w_vf.T.astype(bf16)
    wfl = w_fl.T.astype(bf16)
    bfc = b_forget.reshape(N_FOX, 1).astype(f32)
    half = HD // 2
    inv_freq = ROPE_THETA ** (-jnp.arange(half, dtype=f32) / half)
    per_row = LANES // half
    invf = jnp.tile(inv_freq, per_row).reshape(1, LANES)
    pos4 = jnp.repeat(positions.reshape(per_row, T // per_row).T.astype(i32), half, axis=1)
    cos, sin = _rope_table(pos4, invf)
    w_out = jnp.concatenate([w_mix_out[:FOX_W], regroup(w_mix_out[FOX_W:], 0)], axis=0).astype(bf16)

    tq = 512
    (qt, kf, vt, qs, ks, vs, lf, qn, kn), (eg16, eu16, ed16) = _in_proj(
        x2, cos, sin, w_all, wqt, wvt, wfl, bfc, w_eg, w_eu, w_ed, tq)
    c, ca = _cumsum(lf, S)
    c4 = c.reshape(N_FOX // 2, 2, T // tq, tq)
    r3 = lambda a: a.reshape(B, S, a.shape[-1])
    o_fox = _fox(_fox_first_blocks(qn, kn, c, B, S, tq), qt, r3(kf), r3(ca), vt, c4, B, S, tq)
    o_swa = _swa(sinks.astype(f32) * LOG2E, r3(qs), r3(ks), r3(vs), B, S)

    kx, vx = _kvproj(mem.reshape(-1, D), w_xkv.astype(bf16))

    gpad = EPG - N_GROUPS
    wr = jnp.concatenate([jnp.pad(w_rg, ((0, 0), (0, gpad))),
                          jnp.transpose(w_re, (1, 0, 2)).reshape(D, N_EXPERTS)], axis=1)
    wr = jnp.pad(wr, ((0, 0), (0, LANES - wr.shape[1]))).astype(f32)
    wrh = wr.astype(bf16)
    wr2 = jnp.concatenate([wrh, (wr - wrh.astype(f32)).astype(bf16)], axis=1)
    br = jnp.pad(jnp.concatenate([jnp.pad(b_rg, (0, gpad)), b_re.reshape(-1)]), (0, LANES - EPG - N_EXPERTS))
    br = br.reshape(1, LANES).astype(f32)
    v2 = lambda a: a.reshape(1, D).astype(f32)
    h2x, bk8 = _mid(x2, o_fox.reshape(T, FOX_W), o_swa.reshape(T, SWA_Q_W), w_out,
                    v2(ln_mix_g), v2(ln_mix_b), w_xq.astype(bf16), kx, vx, w_xout.astype(bf16),
                    v2(ln_x_g), v2(ln_x_b), wr2, br, alpha, S)

    rank, cnt = _rank(bk8)
    counts = cnt[:, 0].astype(i32)
    padded = ((counts + ROW_BLOCK - 1) // ROW_BLOCK) * ROW_BLOCK
    pad_end = jnp.cumsum(padded)
    pad_start = (pad_end - padded).astype(i32)
    step = np.lcm(MOE_CHUNKS * SC_CORES * SC_SUBCORES * 2 * SC_GATHER_ROWS, MOE_CHUNKS * ROW_BLOCK)
    P = int(-(-(T + N_BUCKETS * ROW_BLOCK) // step) * step)
    nblk = P // ROW_BLOCK
    used = (pad_end[-1] // ROW_BLOCK).astype(i32).reshape(1)
    blk_row = jnp.arange(nblk, dtype=i32)[:, None] * ROW_BLOCK
    blk_bucket = jnp.minimum(jnp.sum((pad_end[None, :] <= blk_row).astype(i32), axis=1), N_BUCKETS - 1)
    pick = (blk_bucket[:, None] == jnp.arange(LANES, dtype=i32)[None, :]).astype(i32)
    blk_a = jnp.sum(pick * jnp.asarray(_PAIR_A)[None, :], axis=1)
    blk_b = jnp.sum(pick * jnp.asarray(_PAIR_B)[None, :], axis=1)
    blk_g = blk_bucket // N_PAIRS
    by_group = lambda w: w.reshape((N_GROUPS, EPG) + w.shape[1:])

    dest = _dest(bk8, rank, pad_start.astype(f32).reshape(LANES, 1))[0]
    row_tok = _sc_invert(dest, P)
    cblk = nblk // MOE_CHUNKS
    ys = None
    for cidx in range(MOE_CHUNKS):
        lo = cidx * cblk
        xs = _sc_gather_rows(row_tok[lo * ROW_BLOCK:(lo + cblk) * ROW_BLOCK], h2x)
        used_c = jnp.clip(used - lo, 0, cblk)
        ys = _experts(blk_g[lo:lo + cblk], blk_a[lo:lo + cblk], blk_b[lo:lo + cblk], used_c, xs,
                      by_group(eg16), by_group(eu16), by_group(ed16), v2(ln_f_g), v2(ln_f_b), alpha,
                      ys, lo, P)
    return _sc_gather_rows(dest, ys).reshape(B, S, D)


def kernel(x, mem, positions, w_in, b_forget, sinks, w_mix_out, ln_mix_g, ln_mix_b, w_xq, w_xkv, w_xout,
           ln_x_g, ln_x_b, w_route_group, b_route_group, w_route_expert, b_route_expert,
           w_exp_gate, w_exp_up, w_exp_down, ln_ffn_g, ln_ffn_b):
    depth = w_in.shape[0]
    alpha = (2.0 * depth) ** 0.25
    h = x
    for l in range(depth):
        h = _layer(h, mem, positions, w_in[l], b_forget[l], sinks[l], w_mix_out[l], ln_mix_g[l], ln_mix_b[l],
                   w_xq[l], w_xkv[l], w_xout[l], ln_x_g[l], ln_x_b[l], w_route_group[l], b_route_group[l],
                   w_route_expert[l], b_route_expert[l], w_exp_gate[l], w_exp_up[l], w_exp_down[l],
                   ln_ffn_g[l], ln_ffn_b[l], alpha)
    return h
```

```python
import functools

import jax
import jax.numpy as jnp
import numpy as np
from jax import lax
from jax.experimental import pallas as pl
from jax.experimental.pallas import tpu as pltpu
from jax.experimental.pallas import tpu_sc as plsc

f32 = jnp.float32
bf16 = jnp.bfloat16
i32 = jnp.int32

D = 1024
HD = 64
N_FOX = 8
N_SWA = 8
N_SWA_KV = 2
FOX_W = 512
SWA_Q_W = 512
SWA_KV_W = 128
WINDOW = 128
ROPE_THETA = 10000.0
N_XH = 4
XHD = 256
N_GROUPS = 4
EPG = 8
N_EXPERTS = 32
LN_EPS = 1e-5
NEG = -1e30
LOG2E = 1.4426950408889634
L_ROW = (HD, 0)
SKIP_LOG2 = 160.0

SC_CORES = 2
SC_SUBCORES = 16
SC_LANES = 16
SC_GATHER_ROWS = 32
MOE_CHUNKS = 4
LANES = 128
ROW_BLOCK = 128
N_PAIRS = EPG * (EPG - 1) // 2
N_BUCKETS = N_GROUPS * N_PAIRS
XW = D + LANES
VMEM_LIMIT = 56 * 1024 * 1024


def _cparams(sem):
    return pltpu.CompilerParams(dimension_semantics=sem, vmem_limit_bytes=VMEM_LIMIT)


def _layer_norm(v, g, b):
    mu = jnp.mean(v, axis=-1, keepdims=True)
    c = v - mu
    var = jnp.mean(c * c, axis=-1, keepdims=True)
    return c * lax.rsqrt(var + LN_EPS) * g + b


def _dot(a, b):
    return jnp.dot(a, b, preferred_element_type=f32)


def _dot_nt(a, b):
    return lax.dot_general(a, b, (((1,), (1,)), ((), ())), preferred_element_type=f32)


def _rope_table_kernel(pos_ref, invf_ref, cos_ref, sin_ref):
    ang = pos_ref[...].astype(f32) * invf_ref[...]
    cos_ref[...] = jnp.cos(ang)
    sin_ref[...] = jnp.sin(ang)


def _rope_table(pos4, invf, rows=1024):
    R = pos4.shape[0]
    blk = pl.BlockSpec((rows, LANES), lambda i: (i, 0))
    return pl.pallas_call(
        _rope_table_kernel, grid=(R // rows,),
        in_specs=[blk, pl.BlockSpec((1, LANES), lambda i: (0, 0))], out_specs=[blk, blk],
        out_shape=[jax.ShapeDtypeStruct((R, LANES), f32)] * 2,
        compiler_params=_cparams(("parallel",)), name="rope_table",
    )(pos4, invf)


def _inproj_kernel(x_ref, cos_ref, sin_ref, w_ref, wqt_ref, wvt_ref, wfl_ref, bf_ref, ind_ref,
                   eg_ref, eu_ref, ed_ref,
                   qt_ref, kf_ref, vt_ref, qs_ref, ks_ref, vs_ref, lf_ref, qn_ref, kn_ref,
                   egb_ref, eub_ref, edb_ref):
    tm = x_ref.shape[0]
    xb = x_ref[...].astype(bf16)
    egb_ref[...] = eg_ref[...].astype(bf16)
    eub_ref[...] = eu_ref[...].astype(bf16)
    edb_ref[...] = ed_ref[...].astype(bf16)

    def proj(lo, hi):
        return _dot(xb, w_ref[:, lo:hi])

    qv = _dot_nt(wqt_ref[...], xb) * (0.125 * LOG2E)
    qt_ref[0] = qv.astype(bf16)
    vt_ref[0] = _dot_nt(wvt_ref[...], xb).astype(bf16)
    kv = proj(0, 512)
    kf_ref[...] = kv.astype(bf16)
    q2 = jnp.sum((qv * qv).reshape(N_FOX, HD, tm), axis=1)
    qn_ref[0] = jnp.broadcast_to(jnp.max(q2, axis=1, keepdims=True), (N_FOX, LANES))
    k2 = _dot((kv * kv).astype(bf16), ind_ref[...])
    kn_ref[0] = jnp.broadcast_to(jnp.max(k2, axis=0, keepdims=True), (N_FOX, LANES))

    half = HD // 2
    reps = LANES // half
    quarter = pl.program_id(0) // (pl.num_programs(0) // reps)

    def spread(tab):
        pick = tab[:, :half]
        for k in range(1, reps):
            pick = jnp.where(quarter == k, tab[:, k * half:(k + 1) * half], pick)
        return jnp.concatenate([pick] * reps, axis=1)

    cos = spread(cos_ref[...])
    sin = spread(sin_ref[...])
    lane = lax.broadcasted_iota(i32, (tm, LANES), 1)
    lo_half = (lane % HD) < (HD // 2)
    sin_s = jnp.where(lo_half, -sin, sin)

    def rope(z):
        rot = jnp.where(lo_half, pltpu.roll(z, LANES - HD // 2, 1), pltpu.roll(z, HD // 2, 1))
        return z * cos + rot * sin_s

    zq = proj(512, 1024)
    for g in range(4):
        sl = slice(g * LANES, (g + 1) * LANES)
        qs_ref[:, sl] = (rope(zq[:, sl]) * (0.125 * LOG2E)).astype(bf16)
    zkv = proj(1024, 1280)
    ks_ref[...] = rope(zkv[:, :SWA_KV_W]).astype(bf16)
    vs_ref[...] = zkv[:, SWA_KV_W:].astype(bf16)

    z = _dot_nt(wfl_ref[...], xb) + bf_ref[...]
    lf_ref[...] = jnp.minimum(z, 0.0) - jnp.log(1.0 + jnp.exp(-jnp.abs(z)))


def _in_proj(x2, cos, sin, w_all, wqt, wvt, wfl, bfc, w_eg, w_eu, w_ed, tm):
    T = x2.shape[0]
    steps = T // tm
    row = lambda w: pl.BlockSpec((tm, w), lambda i: (i, 0))
    full = lambda a: pl.BlockSpec(a.shape, lambda i: (0,) * a.ndim)
    fmaj = pl.BlockSpec((1, FOX_W, tm), lambda i: (i, 0, 0))
    flat = [w.reshape(-1, w.shape[-1]) for w in (w_eg, w_eu, w_ed)]
    slices = [pl.BlockSpec((w.shape[0] // steps, w.shape[1]), lambda i: (i, 0)) for w in flat]
    ind = jnp.asarray(np.arange(FOX_W)[:, None] // HD == np.arange(LANES)[None, :], bf16)
    nrm = pl.BlockSpec((1, N_FOX, LANES), lambda i: (i, 0, 0))
    tab_steps = cos.shape[0] // tm
    tab = pl.BlockSpec((tm, LANES), lambda i: (i % tab_steps, 0))
    outs = pl.pallas_call(
        _inproj_kernel,
        grid=(steps,),
        in_specs=[row(D), tab, tab, full(w_all), full(wqt), full(wvt), full(wfl), full(bfc), full(ind)] + slices,
        out_specs=[fmaj, row(512), fmaj, row(512), row(128), row(128),
                   pl.BlockSpec((N_FOX, tm), lambda i: (0, i)), nrm, nrm] + slices,
        out_shape=[jax.ShapeDtypeStruct((steps, FOX_W, tm), bf16), jax.ShapeDtypeStruct((T, 512), bf16),
                   jax.ShapeDtypeStruct((steps, FOX_W, tm), bf16), jax.ShapeDtypeStruct((T, 512), bf16),
                   jax.ShapeDtypeStruct((T, 128), bf16), jax.ShapeDtypeStruct((T, 128), bf16),
                   jax.ShapeDtypeStruct((N_FOX, T), f32),
                   jax.ShapeDtypeStruct((steps, N_FOX, LANES), f32), jax.ShapeDtypeStruct((steps, N_FOX, LANES), f32)]
        + [jax.ShapeDtypeStruct(w.shape, bf16) for w in flat],
        compiler_params=_cparams(("parallel",)),
        name="in_proj",
    )(x2, cos, sin, w_all, wqt, wvt, wfl, bfc, ind, *flat)
    experts_bf16 = [o.reshape(w.shape) for o, w in zip(outs[9:], (w_eg, w_eu, w_ed))]
    return outs[:9], experts_bf16


def _cumsum_kernel(lf_ref, c_ref, ca_ref):
    S = lf_ref.shape[1]
    ch = 256
    r = lax.broadcasted_iota(i32, (ch, ch), 0)
    c = lax.broadcasted_iota(i32, (ch, ch), 1)
    tri = (r <= c).astype(f32)
    eye = (r == c).astype(bf16)
    stacked = jnp.concatenate([lf_ref[:, j * ch:(j + 1) * ch] for j in range(S // ch)], axis=0)
    local = jnp.dot(stacked, tri, precision=lax.Precision.HIGHEST, preferred_element_type=f32)
    carry = jnp.zeros((N_FOX, 1), f32)
    for j in range(S // ch):
        cc = local[j * N_FOX:(j + 1) * N_FOX] + carry
        carry = cc[:, ch - 1:ch]
        c2 = cc * LOG2E
        c_ref[:, j * ch:(j + 1) * ch] = c2
        neg = -c2
        hi = neg.astype(bf16)
        r1 = neg - hi.astype(f32)
        mid = r1.astype(bf16)
        lo = (r1 - mid.astype(f32)).astype(bf16)
        terms = jnp.concatenate([hi, mid, lo, jnp.zeros((LANES - 3 * N_FOX, ch), bf16)], axis=0)
        ca_ref[j * ch:(j + 1) * ch, :] = _dot_nt(eye, terms).astype(bf16)


def _cumsum(lf, S):
    T = lf.shape[1]
    spec = pl.BlockSpec((N_FOX, S), lambda b: (0, b))
    return pl.pallas_call(
        _cumsum_kernel, grid=(T // S,), in_specs=[spec],
        out_specs=[spec, pl.BlockSpec((S, LANES), lambda b: (b, 0))],
        out_shape=[jax.ShapeDtypeStruct((N_FOX, T), f32), jax.ShapeDtypeStruct((T, LANES), bf16)],
        compiler_params=_cparams(("parallel",)), name="cumsum",
    )(lf)


def _fox_kernel(j0_ref, qt_ref, k_ref, ca_ref, vt_ref, c_ref, o_ref, t0_ref, t1_ref, *, tq):
    hp = pl.program_id(1)
    i = pl.program_id(2)
    qt = qt_ref[0]
    row = lax.broadcasted_iota(i32, (LANES, tq), 0)
    is_a = row < HD
    zero = jnp.zeros_like(qt)
    q_ops = []
    for h in range(2):
        ones = jnp.where(((row & 7) == 2 * hp + h) & (row < 3 * N_FOX), 1.0, 0.0).astype(bf16)
        qh = jnp.where(is_a, qt, zero) if h == 0 else jnp.where(is_a, zero, qt)
        q_ops.append(jnp.concatenate([qh, ones], axis=0))
    kr = lax.broadcasted_iota(i32, (tq, tq), 0)
    qc = lax.broadcasted_iota(i32, (tq, tq), 1)
    causal = kr <= qc
    cq = [c_ref[0, h, pl.ds(i, 1), :] for h in range(2)]

    def scores(j, t_ref):
        off = pl.multiple_of(j * tq, tq)
        kblk = jnp.concatenate([k_ref[0, pl.ds(off, tq), :], ca_ref[0, pl.ds(off, tq), :]], axis=1)
        for h in range(2):
            t_ref[h] = _dot(kblk, q_ops[h])

    keep = [jnp.where(is_a, 1.0, 0.0).astype(bf16), jnp.where(is_a, 0.0, 1.0).astype(bf16)]
    ones_row = [jnp.where(row == L_ROW[h], 1.0, 0.0).astype(bf16) for h in range(2)]

    def softmax_pv(j, t_ref, carry, masked):
        vt = vt_ref[j]
        vts = [vt * keep[h] + ones_row[h] for h in range(2)]
        new = []
        for h in range(2):
            m, acc = carry[h]
            t = t_ref[h]
            if masked:
                t = jnp.where(causal, t, NEG)
            m_new = jnp.maximum(m, jnp.max(t, axis=0, keepdims=True) + cq[h])
            alpha = jnp.exp2(m - m_new)
            p = jnp.exp2(t + (cq[h] - m_new))
            acc = alpha * acc + _dot(vts[h], p.astype(bf16))
            new.append((m_new, acc))
        return tuple(new)

    j0 = j0_ref[(pl.program_id(0) * pl.num_programs(1) + hp) * pl.num_programs(2) + i]
    n_full = i - j0

    def pair(k, carry):
        j = j0 + 2 * k
        scores(j + 1, t1_ref)
        carry = softmax_pv(j, t0_ref, carry, False)
        scores(j + 2, t0_ref)
        return softmax_pv(j + 1, t1_ref, carry, False)

    def odd_tail(carry):
        scores(i, t1_ref)
        carry = softmax_pv(i - 1, t0_ref, carry, False)
        return softmax_pv(i, t1_ref, carry, True)

    def even_tail(carry):
        return softmax_pv(i, t0_ref, carry, True)

    init = tuple((jnp.full((1, tq), NEG, f32), jnp.zeros((LANES, tq), f32)) for _ in range(2))
    scores(j0, t0_ref)
    carry = lax.fori_loop(0, n_full // 2, pair, init)
    (_, acca), (_, accb) = lax.cond(n_full % 2 == 1, odd_tail, even_tail, carry)
    la = acca[L_ROW[0]:L_ROW[0] + 1, :]
    lb = accb[L_ROW[1]:L_ROW[1] + 1, :]
    ot = jnp.where(is_a, acca / la, accb / lb)
    o_ref[0] = jnp.transpose(ot).astype(bf16)


def _fox_first_blocks(qn, kn, c, B, S, tq):
    nq = S // tq
    qmax = jnp.sqrt(qn[:, :, 0].reshape(B, nq, N_FOX))
    kmax = jnp.sqrt(kn[:, 0, :N_FOX].reshape(B, nq, N_FOX))
    qk = 1.02 * qmax[:, :, None, :] * (kmax[:, None, :, :] + kmax[:, :, None, :])
    cb = c.reshape(N_FOX, B, nq, tq)
    c_first = jnp.transpose(cb[..., 0], (1, 2, 0))
    c_last = jnp.transpose(cb[..., tq - 1], (1, 2, 0))
    gap = c_last[:, None, :, :] - c_first[:, :, None, :]
    blk = jnp.arange(nq)
    negligible = (qk - gap < -SKIP_LOG2) & (blk[None, :] < blk[:, None])[None, :, :, None]
    pair = jnp.all(negligible.reshape(B, nq, nq, N_FOX // 2, 2), axis=-1)
    j0 = jnp.sum(jnp.cumprod(pair.astype(i32), axis=2), axis=2)
    return jnp.transpose(j0, (0, 2, 1)).reshape(-1).astype(i32)


def _fox(j0, qt, kf, ca, vt, c4, B, S, tq):
    nq = S // tq
    kernel = functools.partial(_fox_kernel, tq=tq)
    return pl.pallas_call(
        kernel,
        grid=(B, N_FOX // 2, nq),
        in_specs=[
            pl.BlockSpec(memory_space=pltpu.SMEM),
            pl.BlockSpec((1, LANES, tq), lambda b, hp, i: (b * nq + i, hp, 0)),
            pl.BlockSpec((1, S, LANES), lambda b, hp, i: (b, 0, hp)),
            pl.BlockSpec((1, S, LANES), lambda b, hp, i: (b, 0, 0)),
            pl.BlockSpec((nq, LANES, tq), lambda b, hp, i: (b, hp, 0)),
            pl.BlockSpec((1, 2, nq, tq), lambda b, hp, i: (hp, 0, b, 0)),
        ],
        out_specs=pl.BlockSpec((1, tq, LANES), lambda b, hp, i: (b, i, hp)),
        out_shape=jax.ShapeDtypeStruct((B, S, FOX_W), bf16),
        scratch_shapes=[pltpu.VMEM((2, tq, tq), f32), pltpu.VMEM((2, tq, tq), f32)],
        compiler_params=_cparams(("parallel", "parallel", "arbitrary")),
        name="fox",
    )(j0, qt, kf, ca, vt, c4)


def _swa_kernel(sink_ref, q_ref, k_ref, v_ref, o_ref, *, tq):
    W = WINDOW
    nsub = tq // W
    n0 = pl.program_id(1) * nsub
    lane = lax.broadcasted_iota(i32, (W, LANES), 1)
    is0 = lane < HD
    qoff = lax.broadcasted_iota(i32, (4 * W, 2 * W), 0) % W
    cols = lax.broadcasted_iota(i32, (4 * W, 2 * W), 1)
    bias_mid = jnp.where((cols - W <= qoff) & (qoff - (cols - W) < W), 0.0, NEG)
    bias_first = jnp.where(cols <= qoff, 0.0, NEG)
    rgrp = lax.broadcasted_iota(i32, (4 * W, 1), 0) // W
    for r in range(nsub):
        nb = n0 + r
        kstart = pl.multiple_of(jnp.maximum(nb * W - W, 0), W)
        ks = k_ref[0, pl.ds(kstart, 2 * W), :]
        vs = v_ref[0, pl.ds(kstart, 2 * W), :]
        bias = jnp.where(nb == 0, bias_first, bias_mid)
        outs = []
        for kv in range(2):
            keep = is0 if kv == 0 else jnp.logical_not(is0)
            parts = []
            for g in range(4):
                qg = q_ref[0, r * W:(r + 1) * W, g * LANES:(g + 1) * LANES]
                parts.append(jnp.where(keep, qg, jnp.zeros_like(qg)))
            qstack = jnp.concatenate(parts, axis=0)
            s = bias + _dot_nt(qstack, ks)
            sink = jnp.zeros((4 * W, 1), f32)
            for g in range(4):
                sink = jnp.where(rgrp == g, sink_ref[kv * 4 + g], sink)
            m = jnp.maximum(jnp.max(s, axis=1, keepdims=True), sink)
            e = jnp.exp2(s - m)
            den = jnp.sum(e, axis=1, keepdims=True) + jnp.exp2(sink - m)
            outs.append(_dot(e.astype(bf16), vs) / den)
        for g in range(4):
            og = jnp.where(is0, outs[0][g * W:(g + 1) * W], outs[1][g * W:(g + 1) * W])
            o_ref[0, r * W:(r + 1) * W, g * LANES:(g + 1) * LANES] = og.astype(bf16)


def _swa(sinks, qs, ks, vs, B, S, tq=512):
    kernel = functools.partial(_swa_kernel, tq=tq)
    return pl.pallas_call(
        kernel,
        grid=(B, S // tq),
        in_specs=[
            pl.BlockSpec(memory_space=pltpu.SMEM),
            pl.BlockSpec((1, tq, SWA_Q_W), lambda b, i: (b, i, 0)),
            pl.BlockSpec((1, S, SWA_KV_W), lambda b, i: (b, 0, 0)),
            pl.BlockSpec((1, S, SWA_KV_W), lambda b, i: (b, 0, 0)),
        ],
        out_specs=pl.BlockSpec((1, tq, SWA_Q_W), lambda b, i: (b, i, 0)),
        out_shape=jax.ShapeDtypeStruct((B, S, SWA_Q_W), bf16),
        compiler_params=_cparams(("parallel", "arbitrary")),
        name="swa",
    )(sinks, qs, ks, vs)


def _kvproj_kernel(m_ref, w_ref, k_ref, v_ref):
    mb = m_ref[...].astype(bf16)
    k_ref[...] = _dot(mb, w_ref[:, :D]).astype(bf16)
    v_ref[...] = _dot(mb, w_ref[:, D:]).astype(bf16)


def _kvproj(mem2, w_xkv, tm=512):
    R = mem2.shape[0]
    row = pl.BlockSpec((tm, D), lambda i: (i, 0))
    return pl.pallas_call(
        _kvproj_kernel, grid=(R // tm,),
        in_specs=[row, pl.BlockSpec(w_xkv.shape, lambda i: (0, 0))],
        out_specs=[row, row],
        out_shape=[jax.ShapeDtypeStruct((R, D), bf16)] * 2,
        compiler_params=_cparams(("parallel",)), name="kvproj",
    )(mem2, w_xkv)


def _mid_kernel(x_ref, of_ref, os_ref, wo_ref, g1_ref, b1_ref, wq_ref, k_ref, v_ref,
                wxo_ref, g2_ref, b2_ref, wr_ref, br_ref,
                h2_ref, bk_ref, oc_ref, *, alpha):
    tm = x_ref.shape[0]
    mix = _dot(of_ref[...], wo_ref[:FOX_W, :]) + _dot(os_ref[...], wo_ref[FOX_W:, :])
    h1 = _layer_norm(alpha * x_ref[...] + mix, g1_ref[...], b1_ref[...])

    q = (_dot(h1.astype(bf16), wq_ref[...]) * 0.0625).astype(bf16)
    for h in range(N_XH):
        sl = slice(h * XHD, (h + 1) * XHD)
        s = _dot_nt(q[:, sl], k_ref[:, sl])
        e = jnp.exp(s - jnp.max(s, axis=1, keepdims=True))
        p = e / jnp.sum(e, axis=1, keepdims=True)
        oc_ref[:, sl] = _dot(p.astype(bf16), v_ref[:, sl]).astype(bf16)
    xo = _dot(oc_ref[...], wxo_ref[...])
    h2 = _layer_norm(alpha * h1 + xo, g2_ref[...], b2_ref[...])
    h2_ref[:, :D] = h2

    hh = h2.astype(bf16)
    hl = (h2 - hh.astype(f32)).astype(bf16)
    hi_terms = _dot(hh, wr_ref[...])
    lg = hi_terms[:, :LANES] + _dot(hl, wr_ref[:, :LANES]) + hi_terms[:, LANES:] + br_ref[...]

    lgt = jnp.transpose(lg)
    row = lax.broadcasted_iota(i32, (EPG, tm), 0).astype(f32)
    big = float(EPG)

    def first_max(vals, mask):
        vm = jnp.where(mask, vals, NEG)
        top = jnp.max(vm, axis=0, keepdims=True)
        idx = jnp.min(jnp.where(mask & (vm == top), row, big), axis=0, keepdims=True)
        return top, idx

    gl = lgt[0:EPG]
    gmask = row < float(N_GROUPS)
    gmax, gidx = first_max(gl, gmask)
    g_val = 1.0 / jnp.sum(jnp.where(gmask, jnp.exp(gl - gmax), 0.0), axis=0, keepdims=True)
    sel = jnp.zeros((EPG, tm), f32)
    for g in range(N_GROUPS):
        sel = jnp.where(gidx == float(g), lgt[EPG * (g + 1):EPG * (g + 2)], sel)
    every = row >= 0.0
    v1, e1 = first_max(sel, every)
    v2, e2 = first_max(sel, row != e1)
    ex = jnp.exp(v2 - v1)
    w1 = g_val * (1.0 / (1.0 + ex))
    w2 = g_val * (ex / (1.0 + ex))
    first_low = e1 < e2
    ea = jnp.where(first_low, e1, e2)
    eb = jnp.where(first_low, e2, e1)
    ga = jnp.where(first_low, w1, w2)
    gb = jnp.where(first_low, w2, w1)
    pidx = ea * float(EPG - 1) - ea * (ea - 1.0) * 0.5 + (eb - ea - 1.0)
    bucket = gidx * float(N_PAIRS) + pidx

    bk_ref[...] = jnp.broadcast_to(bucket, (EPG, tm))
    gates = jnp.where(row == 0.0, ga, jnp.where(row == 1.0, gb, 0.0))
    gates = jnp.concatenate([gates, jnp.zeros((LANES - EPG, tm), f32)], axis=0)
    h2_ref[:, D:] = jnp.transpose(gates)


def _mid(x2, of2, os2, w_out, g1, b1, wq, kx, vx, wxo, g2, b2, wr2, br, alpha, S, tm=1024):
    T = x2.shape[0]
    M = kx.shape[0] // (T // S)
    per_b = S // tm
    row = lambda w: pl.BlockSpec((tm, w), lambda i: (i, 0))
    full = lambda a: pl.BlockSpec(a.shape, lambda i: (0,) * a.ndim)
    kvspec = pl.BlockSpec((M, D), lambda i: (i // per_b, 0))
    kernel = functools.partial(_mid_kernel, alpha=alpha)
    return pl.pallas_call(
        kernel,
        grid=(T // tm,),
        in_specs=[row(D), row(512), row(512), full(w_out), full(g1), full(b1), full(wq),
                  kvspec, kvspec, full(wxo), full(g2), full(b2), full(wr2), full(br)],
        out_specs=[row(XW), pl.BlockSpec((8, tm), lambda i: (0, i))],
        out_shape=[jax.ShapeDtypeStruct((T, XW), f32), jax.ShapeDtypeStruct((8, T), f32)],
        scratch_shapes=[pltpu.VMEM((tm, D), bf16)],
        compiler_params=_cparams(("parallel",)),
        name="mid",
    )(x2, of2, os2, w_out, g1, b1, wq, kx, vx, wxo, g2, b2, wr2, br)


def _rank_kernel(bk_ref, rank_ref, cnt_ref, carry_ref, *, chunk):
    sub = 256

    @pl.when(pl.program_id(0) == 0)
    def _():
        carry_ref[...] = jnp.zeros_like(carry_ref)

    r = lax.broadcasted_iota(i32, (sub, sub), 0)
    c = lax.broadcasted_iota(i32, (sub, sub), 1)
    before = (r < c).astype(bf16)
    bid = lax.broadcasted_iota(i32, (LANES, sub), 0).astype(f32)
    carry = carry_ref[...]
    for j in range(chunk // sub):
        bk = bk_ref[0:1, j * sub:(j + 1) * sub]
        hit = bid == bk
        oh = jnp.where(hit, 1.0, 0.0)
        prior = _dot(oh.astype(bf16), before) + carry
        rank_ref[:, j * sub:(j + 1) * sub] = jnp.sum(jnp.where(hit, prior, 0.0), axis=0, keepdims=True)
        carry = carry + jnp.sum(oh, axis=1, keepdims=True)
    carry_ref[...] = carry
    cnt_ref[...] = carry


def _rank(bk8, chunk=8192):
    T = bk8.shape[1]
    kernel = functools.partial(_rank_kernel, chunk=chunk)
    return pl.pallas_call(
        kernel, grid=(T // chunk,),
        in_specs=[pl.BlockSpec((8, chunk), lambda i: (0, i))],
        out_specs=[pl.BlockSpec((1, chunk), lambda i: (0, i)),
                   pl.BlockSpec((LANES, 1), lambda i: (0, 0))],
        out_shape=[jax.ShapeDtypeStruct((1, T), f32), jax.ShapeDtypeStruct((LANES, 1), f32)],
        scratch_shapes=[pltpu.VMEM((LANES, 1), f32)],
        compiler_params=_cparams(("arbitrary",)), name="rank",
    )(bk8)


def _dest_kernel(bk_ref, rank_ref, ps_ref, dest_ref):
    chunk = bk_ref.shape[1]
    bid = lax.broadcasted_iota(i32, (LANES, chunk), 0).astype(f32)
    start = jnp.sum(jnp.where(bid == bk_ref[0:1, :], ps_ref[...], 0.0), axis=0, keepdims=True)
    dest_ref[...] = (start + rank_ref[...]).astype(i32)


def _dest(bk8, rank, ps_col, chunk=8192):
    T = bk8.shape[1]
    return pl.pallas_call(
        _dest_kernel, grid=(T // chunk,),
        in_specs=[pl.BlockSpec((8, chunk), lambda i: (0, i)), pl.BlockSpec((1, chunk), lambda i: (0, i)),
                  pl.BlockSpec((LANES, 1), lambda i: (0, 0))],
        out_specs=pl.BlockSpec((1, chunk), lambda i: (0, i)),
        out_shape=jax.ShapeDtypeStruct((1, T), i32),
        compiler_params=_cparams(("parallel",)), name="dest",
    )(bk8, rank, ps_col)


def _sc_invert(dest, n_rows):
    T = dest.shape[0]
    assert T & (T - 1) == 0
    lanes = SC_LANES
    mesh = plsc.VectorSubcoreMesh(core_axis_name="core", subcore_axis_name="subcore",
                                  num_cores=SC_CORES, num_subcores=SC_SUBCORES)

    @functools.partial(pl.kernel, out_type=jax.ShapeDtypeStruct((n_rows,), i32), mesh=mesh,
                       scratch_types=[pltpu.VMEM((T,), i32), pltpu.VMEM((n_rows,), i32)],
                       compiler_params=pltpu.CompilerParams(needs_layout_passes=False),
                       name="sc_invert")
    def k(dest_hbm, out_hbm, dest_v, table_v):
        wid = lax.axis_index("subcore") * SC_CORES + lax.axis_index("core")

        @pl.when(wid == 0)
        def _():
            pltpu.sync_copy(dest_hbm, dest_v)
            lane = lax.iota(i32, lanes)

            @pl.loop(0, n_rows // lanes)
            def _(j):
                off = pl.multiple_of(j * lanes, lanes)
                table_v[pl.ds(off, lanes)] = (lane + off) & (T - 1)


            @pl.loop(0, T // lanes)
            def _(j):
                off = pl.multiple_of(j * lanes, lanes)
                plsc.store_scatter(table_v, [dest_v[pl.ds(off, lanes)]], lane + off)

            pltpu.sync_copy(table_v, out_hbm)

    return k(dest)


def _sc_gather_rows(idx, src, chunk=SC_GATHER_ROWS):
    n = idx.shape[0]
    w = src.shape[1]
    workers = SC_CORES * SC_SUBCORES
    per_worker = n // workers
    mesh = plsc.VectorSubcoreMesh(core_axis_name="core", subcore_axis_name="subcore",
                                  num_cores=SC_CORES, num_subcores=SC_SUBCORES)

    n_chunks = per_worker // chunk
    assert n_chunks % 2 == 0

    @functools.partial(pl.kernel, out_type=jax.ShapeDtypeStruct((n, w), src.dtype), mesh=mesh,
                       scratch_types=[pltpu.VMEM((per_worker,), i32), pltpu.VMEM((2, chunk, w), src.dtype),
                                      pltpu.SemaphoreType.DMA((2,)), pltpu.SemaphoreType.DMA((2,))],
                       name="sc_gather_rows")
    def k(src_hbm, idx_hbm, out_hbm, idx_v, rows_v, gsem, wsem):
        wid = lax.axis_index("subcore") * SC_CORES + lax.axis_index("core")
        base = wid * per_worker
        pltpu.sync_copy(idx_hbm.at[pl.ds(base, per_worker)], idx_v)

        def gather(c, slot):
            rows = idx_v.at[pl.ds(pl.multiple_of(c * chunk, chunk), chunk)]
            return pltpu.make_async_copy(src_hbm.at[rows], rows_v.at[slot], gsem.at[slot])

        def write(c, slot):
            out = out_hbm.at[pl.ds(pl.multiple_of(base + c * chunk, chunk), chunk)]
            return pltpu.make_async_copy(rows_v.at[slot], out, wsem.at[slot])

        gather(0, 0).start()

        @pl.loop(0, n_chunks // 2)
        def _(pair):
            c = 2 * pair

            @pl.when(pair > 0)
            def _():
                write(c - 1, 1).wait()

            gather(c + 1, 1).start()
            gather(c, 0).wait()
            write(c, 0).start()

            @pl.when(c + 2 < n_chunks)
            def _():
                write(c, 0).wait()
                gather(c + 2, 0).start()

            gather(c + 1, 1).wait()
            write(c + 1, 1).start()

        write(n_chunks - 2, 0).wait()
        write(n_chunks - 1, 1).wait()

    return k(src, idx)


def _expert_kernel(grp_ref, ea_ref, eb_ref, used_ref, xs_ref, wg_ref, wu_ref, wd_ref, g_ref, b_ref, y_ref,
                   z_ref, *, alpha):
    del grp_ref
    n = pl.program_id(0)
    blk = jnp.minimum(n, pl.num_programs(0) - 2)
    used = used_ref[0]

    @pl.when(n == 0)
    def _():
        z_ref[...] = jnp.zeros_like(z_ref)

    @pl.when(n <= used)
    def _():
        y_ref[...] = _layer_norm(z_ref[...], g_ref[...], b_ref[...])
        h2 = xs_ref[:, :D]
        x = h2.astype(bf16)

        def expert(e):
            a = _dot(x, wg_ref[0, e])
            u = _dot(x, wu_ref[0, e])
            act = a * (1.0 / (1.0 + jnp.exp(-a))) * u
            return _dot(act.astype(bf16), wd_ref[0, e])

        ga = xs_ref[:, D:D + 1]
        gb = xs_ref[:, D + 1:D + 2]
        z_ref[...] = alpha * h2 + ga * expert(ea_ref[blk]) + gb * expert(eb_ref[blk])

    @pl.when(n > used)
    def _():
        y_ref[...] = jnp.zeros_like(y_ref)


def _expert_kernel_into(grp_ref, ea_ref, eb_ref, used_ref, xs_ref, wg_ref, wu_ref, wd_ref, g_ref, b_ref,
                        ys_ref, y_ref, z_ref, *, alpha):
    del ys_ref
    _expert_kernel(grp_ref, ea_ref, eb_ref, used_ref, xs_ref, wg_ref, wu_ref, wd_ref, g_ref, b_ref, y_ref,
                   z_ref, alpha=alpha)


def _experts(grp, ea, eb, used, xs, wg, wu, wd, ln_g, ln_b, alpha, ys, first_block, total_rows):
    nblk = xs.shape[0] // ROW_BLOCK

    def xmap(n, grp, ea, eb, used):
        return (jnp.maximum(jnp.minimum(n, used[0] - 1), 0), 0)

    gmap = lambda n, grp, ea, eb, used: (grp[jnp.minimum(n, nblk - 1)], 0, 0, 0)
    gspec = lambda w: pl.BlockSpec((1,) + w.shape[1:], gmap, pipeline_mode=pl.Buffered(1))
    vec = pl.BlockSpec((1, D), lambda n, grp, ea, eb, used: (0, 0))
    in_specs = [pl.BlockSpec((ROW_BLOCK, XW), xmap), gspec(wg), gspec(wu), gspec(wd), vec, vec]
    operands = [grp, ea, eb, used, xs, wg, wu, wd, ln_g, ln_b]
    aliases = {}
    body = _expert_kernel
    if ys is not None:
        in_specs.append(pl.BlockSpec(memory_space=pl.ANY))
        aliases = {len(operands): 0}
        operands.append(ys)
        body = _expert_kernel_into
    grid_spec = pltpu.PrefetchScalarGridSpec(
        num_scalar_prefetch=4, grid=(nblk + 1,), in_specs=in_specs,
        out_specs=pl.BlockSpec((ROW_BLOCK, D),
                               lambda n, grp, ea, eb, used: (jnp.maximum(n - 1, 0) + first_block, 0)),
        scratch_shapes=[pltpu.VMEM((ROW_BLOCK, D), f32)],
    )
    return pl.pallas_call(
        functools.partial(body, alpha=alpha), grid_spec=grid_spec,
        out_shape=jax.ShapeDtypeStruct((total_rows, D), f32),
        input_output_aliases=aliases,
        compiler_params=_cparams(("arbitrary",)), name="experts",
    )(*operands)


def _pair_tables():
    ea = np.zeros((LANES,), np.int32)
    eb = np.zeros((LANES,), np.int32)
    for g in range(N_GROUPS):
        k = 0
        for a in range(EPG):
            for b in range(a + 1, EPG):
                ea[g * N_PAIRS + k] = a
                eb[g * N_PAIRS + k] = b
                k += 1
    return ea, eb


_PAIR_A, _PAIR_B = _pair_tables()


def _layer(h, mem, positions, w_in, b_forget, sinks, w_mix_out, ln_mix_g, ln_mix_b,
           w_xq, w_xkv, w_xout, ln_x_g, ln_x_b, w_rg, b_rg, w_re, b_re,
           w_eg, w_eu, w_ed, ln_f_g, ln_f_b, alpha):
    B, S, _ = h.shape
    T = B * S
    x2 = h.reshape(T, D)

    o = np.cumsum((0, FOX_W, FOX_W, FOX_W, N_FOX, SWA_Q_W, SWA_KV_W, SWA_KV_W))
    w_qf, w_kf, w_vf, w_fl, w_qs, w_ks, w_vs = (w_in[:, o[i]:o[i + 1]] for i in range(7))
    def regroup(a, axis):
        shp = a.shape
        a = jnp.moveaxis(a, axis, 0).reshape(N_SWA_KV, N_SWA // N_SWA_KV, HD, -1)
        return jnp.moveaxis(jnp.swapaxes(a, 0, 1).reshape(N_SWA * HD, -1), 0, axis).reshape(shp)

    w_all = jnp.concatenate([w_kf, regroup(w_qs, 1), w_ks, w_vs], axis=1).astype(bf16)
    wqt = w_qf.T.astype(bf16)
    wvt = w_vf.T.astype(bf16)
    wfl = w_fl.T.astype(bf16)
    bfc = b_forget.reshape(N_FOX, 1).astype(f32)
    half = HD // 2
    inv_freq = ROPE_THETA ** (-jnp.arange(half, dtype=f32) / half)
    per_row = LANES // half
    invf = jnp.tile(inv_freq, per_row).reshape(1, LANES)
    pos4 = jnp.repeat(positions.reshape(per_row, T // per_row).T.astype(i32), half, axis=1)
    cos, sin = _rope_table(pos4, invf)
    w_out = jnp.concatenate([w_mix_out[:FOX_W], regroup(w_mix_out[FOX_W:], 0)], axis=0).astype(bf16)

    tq = 512
    (qt, kf, vt, qs, ks, vs, lf, qn, kn), (eg16, eu16, ed16) = _in_proj(
        x2, cos, sin, w_all, wqt, wvt, wfl, bfc, w_eg, w_eu, w_ed, tq)
    c, ca = _cumsum(lf, S)
    c4 = c.reshape(N_FOX // 2, 2, T // tq, tq)
    r3 = lambda a: a.reshape(B, S, a.shape[-1])
    o_fox = _fox(_fox_first_blocks(qn, kn, c, B, S, tq), qt, r3(kf), r3(ca), vt, c4, B, S, tq)
    o_swa = _swa(sinks.astype(f32) * LOG2E, r3(qs), r3(ks), r3(vs), B, S)

    kx, vx = _kvproj(mem.reshape(-1, D), w_xkv.astype(bf16))

    gpad = EPG - N_GROUPS
    wr = jnp.concatenate([jnp.pad(w_rg, ((0, 0), (0, gpad))),
                          jnp.transpose(w_re, (1, 0, 2)).reshape(D, N_EXPERTS)], axis=1)
    wr = jnp.pad(wr, ((0, 0), (0, LANES - wr.shape[1]))).astype(f32)
    wrh = wr.astype(bf16)
    wr2 = jnp.concatenate([wrh, (wr - wrh.astype(f32)).astype(bf16)], axis=1)
    br = jnp.pad(jnp.concatenate([jnp.pad(b_rg, (0, gpad)), b_re.reshape(-1)]), (0, LANES - EPG - N_EXPERTS))
    br = br.reshape(1, LANES).astype(f32)
    v2 = lambda a: a.reshape(1, D).astype(f32)
    h2x, bk8 = _mid(x2, o_fox.reshape(T, FOX_W), o_swa.reshape(T, SWA_Q_W), w_out,
                    v2(ln_mix_g), v2(ln_mix_b), w_xq.astype(bf16), kx, vx, w_xout.astype(bf16),
                    v2(ln_x_g), v2(ln_x_b), wr2, br, alpha, S)

    rank, cnt = _rank(bk8)
    counts = cnt[:, 0].astype(i32)
    padded = ((counts + ROW_BLOCK - 1) // ROW_BLOCK) * ROW_BLOCK
    pad_end = jnp.cumsum(padded)
    pad_start = (pad_end - padded).astype(i32)
    step = np.lcm(MOE_CHUNKS * SC_CORES * SC_SUBCORES * 2 * SC_GATHER_ROWS, MOE_CHUNKS * ROW_BLOCK)
    P = int(-(-(T + N_BUCKETS * ROW_BLOCK) // step) * step)
    nblk = P // ROW_BLOCK
    used = (pad_end[-1] // ROW_BLOCK).astype(i32).reshape(1)
    blk_row = jnp.arange(nblk, dtype=i32)[:, None] * ROW_BLOCK
    blk_bucket = jnp.minimum(jnp.sum((pad_end[None, :] <= blk_row).astype(i32), axis=1), N_BUCKETS - 1)
    pick = (blk_bucket[:, None] == jnp.arange(LANES, dtype=i32)[None, :]).astype(i32)
    blk_a = jnp.sum(pick * jnp.asarray(_PAIR_A)[None, :], axis=1)
    blk_b = jnp.sum(pick * jnp.asarray(_PAIR_B)[None, :], axis=1)
    blk_g = blk_bucket // N_PAIRS
    by_group = lambda w: w.reshape((N_GROUPS, EPG) + w.shape[1:])

    dest = _dest(bk8, rank, pad_start.astype(f32).reshape(LANES, 1))[0]
    row_tok = _sc_invert(dest, P)
    cblk = nblk // MOE_CHUNKS
    ys = None
    for cidx in range(MOE_CHUNKS):
        lo = cidx * cblk
        xs = _sc_gather_rows(row_tok[lo * ROW_BLOCK:(lo + cblk) * ROW_BLOCK], h2x)
        used_c = jnp.clip(used - lo, 0, cblk)
        ys = _experts(blk_g[lo:lo + cblk], blk_a[lo:lo + cblk], blk_b[lo:lo + cblk], used_c, xs,
                      by_group(eg16), by_group(eu16), by_group(ed16), v2(ln_f_g), v2(ln_f_b), alpha,
                      ys, lo, P)
    return _sc_gather_rows(dest, ys).reshape(B, S, D)


def kernel(x, mem, positions, w_in, b_forget, sinks, w_mix_out, ln_mix_g, ln_mix_b, w_xq, w_xkv, w_xout,
           ln_x_g, ln_x_b, w_route_group, b_route_group, w_route_expert, b_route_expert,
           w_exp_gate, w_exp_up, w_exp_down, ln_ffn_g, ln_ffn_b):
    depth = w_in.shape[0]
    alpha = (2.0 * depth) ** 0.25
    h = x
    for l in range(depth):
        h = _layer(h, mem, positions, w_in[l], b_forget[l], sinks[l], w_mix_out[l], ln_mix_g[l], ln_mix_b[l],
                   w_xq[l], w_xkv[l], w_xout[l], ln_x_g[l], ln_x_b[l], w_route_group[l], b_route_group[l],
                   w_route_expert[l], b_route_expert[l], w_exp_gate[l], w_exp_up[l], w_exp_down[l],
                   ln_ffn_g[l], ln_ffn_b[l], alpha)
    return h
```

```python
import functools

import jax
import jax.numpy as jnp
import numpy as np
from jax import lax
from jax.experimental import pallas as pl
from jax.experimental.pallas import tpu as pltpu
from jax.experimental.pallas import tpu_sc as plsc

f32 = jnp.float32
bf16 = jnp.bfloat16
i32 = jnp.int32

D = 1024
HD = 64
N_FOX = 8
N_SWA = 8
N_SWA_KV = 2
FOX_W = 512
SWA_Q_W = 512
SWA_KV_W = 128
WINDOW = 128
ROPE_THETA = 10000.0
N_XH = 4
XHD = 256
N_GROUPS = 4
EPG = 8
N_EXPERTS = 32
LN_EPS = 1e-5
NEG = -1e30
LOG2E = 1.4426950408889634
L_ROW = (HD, 0)
SKIP_LOG2 = 160.0

SC_CORES = 2
SC_SUBCORES = 16
SC_LANES = 16
SC_GATHER_ROWS = 32
MOE_CHUNK_SHARES = (3, 7, 7, 7)
LANES = 128
ROW_BLOCK = 128
N_PAIRS = EPG * (EPG - 1) // 2
N_BUCKETS = N_GROUPS * N_PAIRS
XW = D + LANES
VMEM_LIMIT = 56 * 1024 * 1024


def _cparams(sem):
    return pltpu.CompilerParams(dimension_semantics=sem, vmem_limit_bytes=VMEM_LIMIT)


def _layer_norm(v, g, b):
    mu = jnp.mean(v, axis=-1, keepdims=True)
    c = v - mu
    var = jnp.mean(c * c, axis=-1, keepdims=True)
    return c * lax.rsqrt(var + LN_EPS) * g + b


def _dot(a, b):
    return jnp.dot(a, b, preferred_element_type=f32)


def _dot_nt(a, b):
    return lax.dot_general(a, b, (((1,), (1,)), ((), ())), preferred_element_type=f32)


def _rope_table_kernel(pos_ref, invf_ref, cos_ref, sin_ref):
    ang = pos_ref[...].astype(f32) * invf_ref[...]
    cos_ref[...] = jnp.cos(ang)
    sin_ref[...] = jnp.sin(ang)


def _rope_table(pos4, invf, rows=1024):
    R = pos4.shape[0]
    blk = pl.BlockSpec((rows, LANES), lambda i: (i, 0))
    return pl.pallas_call(
        _rope_table_kernel, grid=(R // rows,),
        in_specs=[blk, pl.BlockSpec((1, LANES), lambda i: (0, 0))], out_specs=[blk, blk],
        out_shape=[jax.ShapeDtypeStruct((R, LANES), f32)] * 2,
        compiler_params=_cparams(("parallel",)), name="rope_table",
    )(pos4, invf)


def _inproj_kernel(x_ref, cos_ref, sin_ref, w_ref, wqt_ref, wvt_ref, wfl_ref, bf_ref, ind_ref,
                   eg_ref, eu_ref, ed_ref,
                   qt_ref, kf_ref, vt_ref, qs_ref, ks_ref, vs_ref, lf_ref, qn_ref, kn_ref,
                   egb_ref, eub_ref, edb_ref):
    tm = x_ref.shape[0]
    xb = x_ref[...].astype(bf16)
    egb_ref[...] = eg_ref[...].astype(bf16)
    eub_ref[...] = eu_ref[...].astype(bf16)
    edb_ref[...] = ed_ref[...].astype(bf16)

    def proj(lo, hi):
        return _dot(xb, w_ref[:, lo:hi])

    qv = _dot_nt(wqt_ref[...], xb) * (0.125 * LOG2E)
    qt_ref[0] = qv.astype(bf16)
    vt_ref[0] = _dot_nt(wvt_ref[...], xb).astype(bf16)
    kv = proj(0, 512)
    kf_ref[...] = kv.astype(bf16)
    q2 = jnp.sum((qv * qv).reshape(N_FOX, HD, tm), axis=1)
    qn_ref[0] = jnp.broadcast_to(jnp.max(q2, axis=1, keepdims=True), (N_FOX, LANES))
    k2 = _dot((kv * kv).astype(bf16), ind_ref[...])
    kn_ref[0] = jnp.broadcast_to(jnp.max(k2, axis=0, keepdims=True), (N_FOX, LANES))

    half = HD // 2
    reps = LANES // half
    quarter = pl.program_id(0) // (pl.num_programs(0) // reps)

    def spread(tab):
        pick = tab[:, :half]
        for k in range(1, reps):
            pick = jnp.where(quarter == k, tab[:, k * half:(k + 1) * half], pick)
        return jnp.concatenate([pick] * reps, axis=1)

    cos = spread(cos_ref[...])
    sin = spread(sin_ref[...])
    lane = lax.broadcasted_iota(i32, (tm, LANES), 1)
    lo_half = (lane % HD) < (HD // 2)
    sin_s = jnp.where(lo_half, -sin, sin)

    def rope(z):
        rot = jnp.where(lo_half, pltpu.roll(z, LANES - HD // 2, 1), pltpu.roll(z, HD // 2, 1))
        return z * cos + rot * sin_s

    zq = proj(512, 1024)
    for g in range(4):
        sl = slice(g * LANES, (g + 1) * LANES)
        qs_ref[:, sl] = (rope(zq[:, sl]) * (0.125 * LOG2E)).astype(bf16)
    zkv = proj(1024, 1280)
    ks_ref[...] = rope(zkv[:, :SWA_KV_W]).astype(bf16)
    vs_ref[...] = zkv[:, SWA_KV_W:].astype(bf16)

    z = _dot_nt(wfl_ref[...], xb) + bf_ref[...]
    lf_ref[...] = jnp.minimum(z, 0.0) - jnp.log(1.0 + jnp.exp(-jnp.abs(z)))


def _in_proj(x2, cos, sin, w_all, wqt, wvt, wfl, bfc, w_eg, w_eu, w_ed, tm):
    T = x2.shape[0]
    steps = T // tm
    row = lambda w: pl.BlockSpec((tm, w), lambda i: (i, 0))
    full = lambda a: pl.BlockSpec(a.shape, lambda i: (0,) * a.ndim)
    fmaj = pl.BlockSpec((1, FOX_W, tm), lambda i: (i, 0, 0))
    flat = [w.reshape(-1, w.shape[-1]) for w in (w_eg, w_eu, w_ed)]
    slices = [pl.BlockSpec((w.shape[0] // steps, w.shape[1]), lambda i: (i, 0)) for w in flat]
    ind = jnp.asarray(np.arange(FOX_W)[:, None] // HD == np.arange(LANES)[None, :], bf16)
    nrm = pl.BlockSpec((1, N_FOX, LANES), lambda i: (i, 0, 0))
    tab_steps = cos.shape[0] // tm
    tab = pl.BlockSpec((tm, LANES), lambda i: (i % tab_steps, 0))
    outs = pl.pallas_call(
        _inproj_kernel,
        grid=(steps,),
        in_specs=[row(D), tab, tab, full(w_all), full(wqt), full(wvt), full(wfl), full(bfc), full(ind)] + slices,
        out_specs=[fmaj, row(512), fmaj, row(512), row(128), row(128),
                   pl.BlockSpec((N_FOX, tm), lambda i: (0, i)), nrm, nrm] + slices,
        out_shape=[jax.ShapeDtypeStruct((steps, FOX_W, tm), bf16), jax.ShapeDtypeStruct((T, 512), bf16),
                   jax.ShapeDtypeStruct((steps, FOX_W, tm), bf16), jax.ShapeDtypeStruct((T, 512), bf16),
                   jax.ShapeDtypeStruct((T, 128), bf16), jax.ShapeDtypeStruct((T, 128), bf16),
                   jax.ShapeDtypeStruct((N_FOX, T), f32),
                   jax.ShapeDtypeStruct((steps, N_FOX, LANES), f32), jax.ShapeDtypeStruct((steps, N_FOX, LANES), f32)]
        + [jax.ShapeDtypeStruct(w.shape, bf16) for w in flat],
        compiler_params=_cparams(("parallel",)),
        name="in_proj",
    )(x2, cos, sin, w_all, wqt, wvt, wfl, bfc, ind, *flat)
    experts_bf16 = [o.reshape(w.shape) for o, w in zip(outs[9:], (w_eg, w_eu, w_ed))]
    return outs[:9], experts_bf16


def _cumsum_kernel(lf_ref, c_ref, ca_ref):
    S = lf_ref.shape[1]
    ch = 256
    r = lax.broadcasted_iota(i32, (ch, ch), 0)
    c = lax.broadcasted_iota(i32, (ch, ch), 1)
    tri = (r <= c).astype(f32)
    eye = (r == c).astype(bf16)
    stacked = jnp.concatenate([lf_ref[:, j * ch:(j + 1) * ch] for j in range(S // ch)], axis=0)
    local = jnp.dot(stacked, tri, precision=lax.Precision.HIGHEST, preferred_element_type=f32)
    carry = jnp.zeros((N_FOX, 1), f32)
    for j in range(S // ch):
        cc = local[j * N_FOX:(j + 1) * N_FOX] + carry
        carry = cc[:, ch - 1:ch]
        c2 = cc * LOG2E
        c_ref[:, j * ch:(j + 1) * ch] = c2
        neg = -c2
        hi = neg.astype(bf16)
        r1 = neg - hi.astype(f32)
        mid = r1.astype(bf16)
        lo = (r1 - mid.astype(f32)).astype(bf16)
        terms = jnp.concatenate([hi, mid, lo, jnp.zeros((LANES - 3 * N_FOX, ch), bf16)], axis=0)
        ca_ref[j * ch:(j + 1) * ch, :] = _dot_nt(eye, terms).astype(bf16)


def _cumsum(lf, S):
    T = lf.shape[1]
    spec = pl.BlockSpec((N_FOX, S), lambda b: (0, b))
    return pl.pallas_call(
        _cumsum_kernel, grid=(T // S,), in_specs=[spec],
        out_specs=[spec, pl.BlockSpec((S, LANES), lambda b: (b, 0))],
        out_shape=[jax.ShapeDtypeStruct((N_FOX, T), f32), jax.ShapeDtypeStruct((T, LANES), bf16)],
        compiler_params=_cparams(("parallel",)), name="cumsum",
    )(lf)


def _fox_kernel(j0_ref, qt_ref, k_ref, ca_ref, vt_ref, c_ref, o_ref, t0_ref, t1_ref, *, tq):
    hp = pl.program_id(1)
    i = pl.program_id(2)
    qt = qt_ref[0]
    row = lax.broadcasted_iota(i32, (LANES, tq), 0)
    is_a = row < HD
    zero = jnp.zeros_like(qt)
    q_ops = []
    for h in range(2):
        ones = jnp.where(((row & 7) == 2 * hp + h) & (row < 3 * N_FOX), 1.0, 0.0).astype(bf16)
        qh = jnp.where(is_a, qt, zero) if h == 0 else jnp.where(is_a, zero, qt)
        q_ops.append(jnp.concatenate([qh, ones], axis=0))
    kr = lax.broadcasted_iota(i32, (tq, tq), 0)
    qc = lax.broadcasted_iota(i32, (tq, tq), 1)
    causal = kr <= qc
    cq = [c_ref[0, h, pl.ds(i, 1), :] for h in range(2)]

    def scores(j, t_ref):
        off = pl.multiple_of(j * tq, tq)
        kblk = jnp.concatenate([k_ref[0, pl.ds(off, tq), :], ca_ref[0, pl.ds(off, tq), :]], axis=1)
        for h in range(2):
            t_ref[h] = _dot(kblk, q_ops[h])

    keep = [jnp.where(is_a, 1.0, 0.0).astype(bf16), jnp.where(is_a, 0.0, 1.0).astype(bf16)]
    ones_row = [jnp.where(row == L_ROW[h], 1.0, 0.0).astype(bf16) for h in range(2)]

    def softmax_pv(j, t_ref, carry, masked):
        vt = vt_ref[j]
        vts = [vt * keep[h] + ones_row[h] for h in range(2)]
        new = []
        for h in range(2):
            m, acc = carry[h]
            t = t_ref[h]
            if masked:
                t = jnp.where(causal, t, NEG)
            m_new = jnp.maximum(m, jnp.max(t, axis=0, keepdims=True) + cq[h])
            alpha = jnp.exp2(m - m_new)
            p = jnp.exp2(t + (cq[h] - m_new))
            acc = alpha * acc + _dot(vts[h], p.astype(bf16))
            new.append((m_new, acc))
        return tuple(new)

    j0 = j0_ref[(pl.program_id(0) * pl.num_programs(1) + hp) * pl.num_programs(2) + i]
    n_full = i - j0

    def pair(k, carry):
        j = j0 + 2 * k
        scores(j + 1, t1_ref)
        carry = softmax_pv(j, t0_ref, carry, False)
        scores(j + 2, t0_ref)
        return softmax_pv(j + 1, t1_ref, carry, False)

    def odd_tail(carry):
        scores(i, t1_ref)
        carry = softmax_pv(i - 1, t0_ref, carry, False)
        return softmax_pv(i, t1_ref, carry, True)

    def even_tail(carry):
        return softmax_pv(i, t0_ref, carry, True)

    init = tuple((jnp.full((1, tq), NEG, f32), jnp.zeros((LANES, tq), f32)) for _ in range(2))
    scores(j0, t0_ref)
    carry = lax.fori_loop(0, n_full // 2, pair, init)
    (_, acca), (_, accb) = lax.cond(n_full % 2 == 1, odd_tail, even_tail, carry)
    la = acca[L_ROW[0]:L_ROW[0] + 1, :]
    lb = accb[L_ROW[1]:L_ROW[1] + 1, :]
    ot = jnp.where(is_a, acca / la, accb / lb)
    o_ref[0] = jnp.transpose(ot).astype(bf16)


def _fox_first_blocks(qn, kn, c, B, S, tq):
    nq = S // tq
    qmax = jnp.sqrt(qn[:, :, 0].reshape(B, nq, N_FOX))
    kmax = jnp.sqrt(kn[:, 0, :N_FOX].reshape(B, nq, N_FOX))
    qk = 1.02 * qmax[:, :, None, :] * (kmax[:, None, :, :] + kmax[:, :, None, :])
    cb = c.reshape(N_FOX, B, nq, tq)
    c_first = jnp.transpose(cb[..., 0], (1, 2, 0))
    c_last = jnp.transpose(cb[..., tq - 1], (1, 2, 0))
    gap = c_last[:, None, :, :] - c_first[:, :, None, :]
    blk = jnp.arange(nq)
    negligible = (qk - gap < -SKIP_LOG2) & (blk[None, :] < blk[:, None])[None, :, :, None]
    pair = jnp.all(negligible.reshape(B, nq, nq, N_FOX // 2, 2), axis=-1)
    j0 = jnp.sum(jnp.cumprod(pair.astype(i32), axis=2), axis=2)
    return jnp.transpose(j0, (0, 2, 1)).reshape(-1).astype(i32)


def _fox(j0, qt, kf, ca, vt, c4, B, S, tq):
    nq = S // tq
    kernel = functools.partial(_fox_kernel, tq=tq)
    return pl.pallas_call(
        kernel,
        grid=(B, N_FOX // 2, nq),
        in_specs=[
            pl.BlockSpec(memory_space=pltpu.SMEM),
            pl.BlockSpec((1, LANES, tq), lambda b, hp, i: (b * nq + i, hp, 0)),
            pl.BlockSpec((1, S, LANES), lambda b, hp, i: (b, 0, hp)),
            pl.BlockSpec((1, S, LANES), lambda b, hp, i: (b, 0, 0)),
            pl.BlockSpec((nq, LANES, tq), lambda b, hp, i: (b, hp, 0)),
            pl.BlockSpec((1, 2, nq, tq), lambda b, hp, i: (hp, 0, b, 0)),
        ],
        out_specs=pl.BlockSpec((1, tq, LANES), lambda b, hp, i: (b, i, hp)),
        out_shape=jax.ShapeDtypeStruct((B, S, FOX_W), bf16),
        scratch_shapes=[pltpu.VMEM((2, tq, tq), f32), pltpu.VMEM((2, tq, tq), f32)],
        compiler_params=_cparams(("parallel", "parallel", "arbitrary")),
        name="fox",
    )(j0, qt, kf, ca, vt, c4)


def _swa_kernel(sink_ref, q_ref, k_ref, v_ref, o_ref, *, tq):
    W = WINDOW
    nsub = tq // W
    n0 = pl.program_id(1) * nsub
    lane = lax.broadcasted_iota(i32, (W, LANES), 1)
    is0 = lane < HD
    qoff = lax.broadcasted_iota(i32, (4 * W, 2 * W), 0) % W
    cols = lax.broadcasted_iota(i32, (4 * W, 2 * W), 1)
    bias_mid = jnp.where((cols - W <= qoff) & (qoff - (cols - W) < W), 0.0, NEG)
    bias_first = jnp.where(cols <= qoff, 0.0, NEG)
    rgrp = lax.broadcasted_iota(i32, (4 * W, 1), 0) // W
    for r in range(nsub):
        nb = n0 + r
        kstart = pl.multiple_of(jnp.maximum(nb * W - W, 0), W)
        ks = k_ref[0, pl.ds(kstart, 2 * W), :]
        vs = v_ref[0, pl.ds(kstart, 2 * W), :]
        bias = jnp.where(nb == 0, bias_first, bias_mid)
        outs = []
        for kv in range(2):
            keep = is0 if kv == 0 else jnp.logical_not(is0)
            parts = []
            for g in range(4):
                qg = q_ref[0, r * W:(r + 1) * W, g * LANES:(g + 1) * LANES]
                parts.append(jnp.where(keep, qg, jnp.zeros_like(qg)))
            qstack = jnp.concatenate(parts, axis=0)
            s = bias + _dot_nt(qstack, ks)
            sink = jnp.zeros((4 * W, 1), f32)
            for g in range(4):
                sink = jnp.where(rgrp == g, sink_ref[kv * 4 + g], sink)
            m = jnp.maximum(jnp.max(s, axis=1, keepdims=True), sink)
            e = jnp.exp2(s - m)
            den = jnp.sum(e, axis=1, keepdims=True) + jnp.exp2(sink - m)
            outs.append(_dot(e.astype(bf16), vs) / den)
        for g in range(4):
            og = jnp.where(is0, outs[0][g * W:(g + 1) * W], outs[1][g * W:(g + 1) * W])
            o_ref[0, r * W:(r + 1) * W, g * LANES:(g + 1) * LANES] = og.astype(bf16)


def _swa(sinks, qs, ks, vs, B, S, tq=512):
    kernel = functools.partial(_swa_kernel, tq=tq)
    return pl.pallas_call(
        kernel,
        grid=(B, S // tq),
        in_specs=[
            pl.BlockSpec(memory_space=pltpu.SMEM),
            pl.BlockSpec((1, tq, SWA_Q_W), lambda b, i: (b, i, 0)),
            pl.BlockSpec((1, S, SWA_KV_W), lambda b, i: (b, 0, 0)),
            pl.BlockSpec((1, S, SWA_KV_W), lambda b, i: (b, 0, 0)),
        ],
        out_specs=pl.BlockSpec((1, tq, SWA_Q_W), lambda b, i: (b, i, 0)),
        out_shape=jax.ShapeDtypeStruct((B, S, SWA_Q_W), bf16),
        compiler_params=_cparams(("parallel", "arbitrary")),
        name="swa",
    )(sinks, qs, ks, vs)


def _kvproj_kernel(m_ref, w_ref, k_ref, v_ref):
    mb = m_ref[...].astype(bf16)
    k_ref[...] = _dot(mb, w_ref[:, :D]).astype(bf16)
    v_ref[...] = _dot(mb, w_ref[:, D:]).astype(bf16)


def _kvproj(mem2, w_xkv, tm=512):
    R = mem2.shape[0]
    row = pl.BlockSpec((tm, D), lambda i: (i, 0))
    return pl.pallas_call(
        _kvproj_kernel, grid=(R // tm,),
        in_specs=[row, pl.BlockSpec(w_xkv.shape, lambda i: (0, 0))],
        out_specs=[row, row],
        out_shape=[jax.ShapeDtypeStruct((R, D), bf16)] * 2,
        compiler_params=_cparams(("parallel",)), name="kvproj",
    )(mem2, w_xkv)


def _mid_kernel(x_ref, of_ref, os_ref, wo_ref, g1_ref, b1_ref, wq_ref, k_ref, v_ref,
                wxo_ref, g2_ref, b2_ref, wr_ref, br_ref,
                h2_ref, bk_ref, oc_ref, *, alpha):
    tm = x_ref.shape[0]
    mix = _dot(of_ref[...], wo_ref[:FOX_W, :]) + _dot(os_ref[...], wo_ref[FOX_W:, :])
    h1 = _layer_norm(alpha * x_ref[...] + mix, g1_ref[...], b1_ref[...])

    q = (_dot(h1.astype(bf16), wq_ref[...]) * 0.0625).astype(bf16)
    for h in range(N_XH):
        sl = slice(h * XHD, (h + 1) * XHD)
        s = _dot_nt(q[:, sl], k_ref[:, sl])
        e = jnp.exp(s - jnp.max(s, axis=1, keepdims=True))
        p = e / jnp.sum(e, axis=1, keepdims=True)
        oc_ref[:, sl] = _dot(p.astype(bf16), v_ref[:, sl]).astype(bf16)
    xo = _dot(oc_ref[...], wxo_ref[...])
    h2 = _layer_norm(alpha * h1 + xo, g2_ref[...], b2_ref[...])
    h2_ref[:, :D] = h2

    hh = h2.astype(bf16)
    hl = (h2 - hh.astype(f32)).astype(bf16)
    hi_terms = _dot(hh, wr_ref[...])
    lg = hi_terms[:, :LANES] + _dot(hl, wr_ref[:, :LANES]) + hi_terms[:, LANES:] + br_ref[...]

    lgt = jnp.transpose(lg)
    row = lax.broadcasted_iota(i32, (EPG, tm), 0).astype(f32)
    big = float(EPG)

    def first_max(vals, mask):
        vm = jnp.where(mask, vals, NEG)
        top = jnp.max(vm, axis=0, keepdims=True)
        idx = jnp.min(jnp.where(mask & (vm == top), row, big), axis=0, keepdims=True)
        return top, idx

    gl = lgt[0:EPG]
    gmask = row < float(N_GROUPS)
    gmax, gidx = first_max(gl, gmask)
    g_val = 1.0 / jnp.sum(jnp.where(gmask, jnp.exp(gl - gmax), 0.0), axis=0, keepdims=True)
    sel = jnp.zeros((EPG, tm), f32)
    for g in range(N_GROUPS):
        sel = jnp.where(gidx == float(g), lgt[EPG * (g + 1):EPG * (g + 2)], sel)
    every = row >= 0.0
    v1, e1 = first_max(sel, every)
    v2, e2 = first_max(sel, row != e1)
    ex = jnp.exp(v2 - v1)
    w1 = g_val * (1.0 / (1.0 + ex))
    w2 = g_val * (ex / (1.0 + ex))
    first_low = e1 < e2
    ea = jnp.where(first_low, e1, e2)
    eb = jnp.where(first_low, e2, e1)
    ga = jnp.where(first_low, w1, w2)
    gb = jnp.where(first_low, w2, w1)
    pidx = ea * float(EPG - 1) - ea * (ea - 1.0) * 0.5 + (eb - ea - 1.0)
    bucket = gidx * float(N_PAIRS) + pidx

    bk_ref[...] = jnp.broadcast_to(bucket, (EPG, tm))
    gates = jnp.where(row == 0.0, ga, jnp.where(row == 1.0, gb, 0.0))
    gates = jnp.concatenate([gates, jnp.zeros((LANES - EPG, tm), f32)], axis=0)
    h2_ref[:, D:] = jnp.transpose(gates)


def _mid(x2, of2, os2, w_out, g1, b1, wq, kx, vx, wxo, g2, b2, wr2, br, alpha, S, tm=1024):
    T = x2.shape[0]
    M = kx.shape[0] // (T // S)
    per_b = S // tm
    row = lambda w: pl.BlockSpec((tm, w), lambda i: (i, 0))
    full = lambda a: pl.BlockSpec(a.shape, lambda i: (0,) * a.ndim)
    kvspec = pl.BlockSpec((M, D), lambda i: (i // per_b, 0))
    kernel = functools.partial(_mid_kernel, alpha=alpha)
    return pl.pallas_call(
        kernel,
        grid=(T // tm,),
        in_specs=[row(D), row(512), row(512), full(w_out), full(g1), full(b1), full(wq),
                  kvspec, kvspec, full(wxo), full(g2), full(b2), full(wr2), full(br)],
        out_specs=[row(XW), pl.BlockSpec((8, tm), lambda i: (0, i))],
        out_shape=[jax.ShapeDtypeStruct((T, XW), f32), jax.ShapeDtypeStruct((8, T), f32)],
        scratch_shapes=[pltpu.VMEM((tm, D), bf16)],
        compiler_params=_cparams(("parallel",)),
        name="mid",
    )(x2, of2, os2, w_out, g1, b1, wq, kx, vx, wxo, g2, b2, wr2, br)


def _rank_kernel(bk_ref, rank_ref, cnt_ref, carry_ref, *, chunk):
    sub = 256

    @pl.when(pl.program_id(0) == 0)
    def _():
        carry_ref[...] = jnp.zeros_like(carry_ref)

    r = lax.broadcasted_iota(i32, (sub, sub), 0)
    c = lax.broadcasted_iota(i32, (sub, sub), 1)
    before = (r < c).astype(bf16)
    bid = lax.broadcasted_iota(i32, (LANES, sub), 0).astype(f32)
    carry = carry_ref[...]
    for j in range(chunk // sub):
        bk = bk_ref[0:1, j * sub:(j + 1) * sub]
        hit = bid == bk
        oh = jnp.where(hit, 1.0, 0.0)
        prior = _dot(oh.astype(bf16), before) + carry
        rank_ref[:, j * sub:(j + 1) * sub] = jnp.sum(jnp.where(hit, prior, 0.0), axis=0, keepdims=True)
        carry = carry + jnp.sum(oh, axis=1, keepdims=True)
    carry_ref[...] = carry
    cnt_ref[...] = carry


def _rank(bk8, chunk=8192):
    T = bk8.shape[1]
    kernel = functools.partial(_rank_kernel, chunk=chunk)
    return pl.pallas_call(
        kernel, grid=(T // chunk,),
        in_specs=[pl.BlockSpec((8, chunk), lambda i: (0, i))],
        out_specs=[pl.BlockSpec((1, chunk), lambda i: (0, i)),
                   pl.BlockSpec((LANES, 1), lambda i: (0, 0))],
        out_shape=[jax.ShapeDtypeStruct((1, T), f32), jax.ShapeDtypeStruct((LANES, 1), f32)],
        scratch_shapes=[pltpu.VMEM((LANES, 1), f32)],
        compiler_params=_cparams(("arbitrary",)), name="rank",
    )(bk8)


def _dest_kernel(bk_ref, rank_ref, ps_ref, dest_ref):
    chunk = bk_ref.shape[1]
    bid = lax.broadcasted_iota(i32, (LANES, chunk), 0).astype(f32)
    start = jnp.sum(jnp.where(bid == bk_ref[0:1, :], ps_ref[...], 0.0), axis=0, keepdims=True)
    dest_ref[...] = (start + rank_ref[...]).astype(i32)


def _dest(bk8, rank, ps_col, chunk=8192):
    T = bk8.shape[1]
    return pl.pallas_call(
        _dest_kernel, grid=(T // chunk,),
        in_specs=[pl.BlockSpec((8, chunk), lambda i: (0, i)), pl.BlockSpec((1, chunk), lambda i: (0, i)),
                  pl.BlockSpec((LANES, 1), lambda i: (0, 0))],
        out_specs=pl.BlockSpec((1, chunk), lambda i: (0, i)),
        out_shape=jax.ShapeDtypeStruct((1, T), i32),
        compiler_params=_cparams(("parallel",)), name="dest",
    )(bk8, rank, ps_col)


def _sc_invert(dest, n_rows):
    T = dest.shape[0]
    assert T & (T - 1) == 0
    lanes = SC_LANES
    mesh = plsc.VectorSubcoreMesh(core_axis_name="core", subcore_axis_name="subcore",
                                  num_cores=SC_CORES, num_subcores=SC_SUBCORES)

    @functools.partial(pl.kernel, out_type=jax.ShapeDtypeStruct((n_rows,), i32), mesh=mesh,
                       scratch_types=[pltpu.VMEM((T,), i32), pltpu.VMEM((n_rows,), i32)],
                       compiler_params=pltpu.CompilerParams(needs_layout_passes=False),
                       name="sc_invert")
    def k(dest_hbm, out_hbm, dest_v, table_v):
        wid = lax.axis_index("subcore") * SC_CORES + lax.axis_index("core")

        @pl.when(wid == 0)
        def _():
            pltpu.sync_copy(dest_hbm, dest_v)
            lane = lax.iota(i32, lanes)

            @pl.loop(0, n_rows // lanes)
            def _(j):
                off = pl.multiple_of(j * lanes, lanes)
                table_v[pl.ds(off, lanes)] = (lane + off) & (T - 1)


            @pl.loop(0, T // lanes)
            def _(j):
                off = pl.multiple_of(j * lanes, lanes)
                plsc.store_scatter(table_v, [dest_v[pl.ds(off, lanes)]], lane + off)

            pltpu.sync_copy(table_v, out_hbm)

    return k(dest)


def _sc_gather_rows(idx, src, chunk=SC_GATHER_ROWS):
    n = idx.shape[0]
    w = src.shape[1]
    workers = SC_CORES * SC_SUBCORES
    per_worker = n // workers
    mesh = plsc.VectorSubcoreMesh(core_axis_name="core", subcore_axis_name="subcore",
                                  num_cores=SC_CORES, num_subcores=SC_SUBCORES)

    n_chunks = per_worker // chunk
    assert n_chunks % 2 == 0

    @functools.partial(pl.kernel, out_type=jax.ShapeDtypeStruct((n, w), src.dtype), mesh=mesh,
                       scratch_types=[pltpu.VMEM((per_worker,), i32), pltpu.VMEM((2, chunk, w), src.dtype),
                                      pltpu.SemaphoreType.DMA((2,)), pltpu.SemaphoreType.DMA((2,))],
                       name="sc_gather_rows")
    def k(src_hbm, idx_hbm, out_hbm, idx_v, rows_v, gsem, wsem):
        wid = lax.axis_index("subcore") * SC_CORES + lax.axis_index("core")
        base = wid * per_worker
        pltpu.sync_copy(idx_hbm.at[pl.ds(base, per_worker)], idx_v)

        def gather(c, slot):
            rows = idx_v.at[pl.ds(pl.multiple_of(c * chunk, chunk), chunk)]
            return pltpu.make_async_copy(src_hbm.at[rows], rows_v.at[slot], gsem.at[slot])

        def write(c, slot):
            out = out_hbm.at[pl.ds(pl.multiple_of(base + c * chunk, chunk), chunk)]
            return pltpu.make_async_copy(rows_v.at[slot], out, wsem.at[slot])

        gather(0, 0).start()

        @pl.loop(0, n_chunks // 2)
        def _(pair):
            c = 2 * pair

            @pl.when(pair > 0)
            def _():
                write(c - 1, 1).wait()

            gather(c + 1, 1).start()
            gather(c, 0).wait()
            write(c, 0).start()

            @pl.when(c + 2 < n_chunks)
            def _():
                write(c, 0).wait()
                gather(c + 2, 0).start()

            gather(c + 1, 1).wait()
            write(c + 1, 1).start()

        write(n_chunks - 2, 0).wait()
        write(n_chunks - 1, 1).wait()

    return k(src, idx)


def _expert_kernel(grp_ref, ea_ref, eb_ref, used_ref, xs_ref, wg_ref, wu_ref, wd_ref, g_ref, b_ref, y_ref,
                   z_ref, *, alpha):
    del grp_ref
    n = pl.program_id(0)
    blk = jnp.minimum(n, pl.num_programs(0) - 2)
    used = used_ref[0]

    @pl.when(n == 0)
    def _():
        z_ref[...] = jnp.zeros_like(z_ref)

    @pl.when(n <= used)
    def _():
        y_ref[...] = _layer_norm(z_ref[...], g_ref[...], b_ref[...])
        h2 = xs_ref[:, :D]
        x = h2.astype(bf16)

        def expert(e):
            a = _dot(x, wg_ref[0, e])
            u = _dot(x, wu_ref[0, e])
            act = a * (1.0 / (1.0 + jnp.exp(-a))) * u
            return _dot(act.astype(bf16), wd_ref[0, e])

        ga = xs_ref[:, D:D + 1]
        gb = xs_ref[:, D + 1:D + 2]
        z_ref[...] = alpha * h2 + ga * expert(ea_ref[blk]) + gb * expert(eb_ref[blk])

    @pl.when(n > used)
    def _():
        y_ref[...] = jnp.zeros_like(y_ref)


def _expert_kernel_into(grp_ref, ea_ref, eb_ref, used_ref, xs_ref, wg_ref, wu_ref, wd_ref, g_ref, b_ref,
                        ys_ref, y_ref, z_ref, *, alpha):
    del ys_ref
    _expert_kernel(grp_ref, ea_ref, eb_ref, used_ref, xs_ref, wg_ref, wu_ref, wd_ref, g_ref, b_ref, y_ref,
                   z_ref, alpha=alpha)


def _experts(grp, ea, eb, used, xs, wg, wu, wd, ln_g, ln_b, alpha, ys, first_block, total_rows):
    nblk = xs.shape[0] // ROW_BLOCK

    def xmap(n, grp, ea, eb, used):
        return (jnp.maximum(jnp.minimum(n, used[0] - 1), 0), 0)

    gmap = lambda n, grp, ea, eb, used: (grp[jnp.minimum(n, nblk - 1)], 0, 0, 0)
    gspec = lambda w: pl.BlockSpec((1,) + w.shape[1:], gmap, pipeline_mode=pl.Buffered(1))
    vec = pl.BlockSpec((1, D), lambda n, grp, ea, eb, used: (0, 0))
    in_specs = [pl.BlockSpec((ROW_BLOCK, XW), xmap), gspec(wg), gspec(wu), gspec(wd), vec, vec]
    operands = [grp, ea, eb, used, xs, wg, wu, wd, ln_g, ln_b]
    aliases = {}
    body = _expert_kernel
    if ys is not None:
        in_specs.append(pl.BlockSpec(memory_space=pl.ANY))
        aliases = {len(operands): 0}
        operands.append(ys)
        body = _expert_kernel_into
    grid_spec = pltpu.PrefetchScalarGridSpec(
        num_scalar_prefetch=4, grid=(nblk + 1,), in_specs=in_specs,
        out_specs=pl.BlockSpec((ROW_BLOCK, D),
                               lambda n, grp, ea, eb, used: (jnp.maximum(n - 1, 0) + first_block, 0)),
        scratch_shapes=[pltpu.VMEM((ROW_BLOCK, D), f32)],
    )
    return pl.pallas_call(
        functools.partial(body, alpha=alpha), grid_spec=grid_spec,
        out_shape=jax.ShapeDtypeStruct((total_rows, D), f32),
        input_output_aliases=aliases,
        compiler_params=_cparams(("arbitrary",)), name="experts",
    )(*operands)


def _pair_tables():
    ea = np.zeros((LANES,), np.int32)
    eb = np.zeros((LANES,), np.int32)
    for g in range(N_GROUPS):
        k = 0
        for a in range(EPG):
            for b in range(a + 1, EPG):
                ea[g * N_PAIRS + k] = a
                eb[g * N_PAIRS + k] = b
                k += 1
    return ea, eb


_PAIR_A, _PAIR_B = _pair_tables()


def _layer(h, mem, positions, w_in, b_forget, sinks, w_mix_out, ln_mix_g, ln_mix_b,
           w_xq, w_xkv, w_xout, ln_x_g, ln_x_b, w_rg, b_rg, w_re, b_re,
           w_eg, w_eu, w_ed, ln_f_g, ln_f_b, alpha):
    B, S, _ = h.shape
    T = B * S
    x2 = h.reshape(T, D)

    o = np.cumsum((0, FOX_W, FOX_W, FOX_W, N_FOX, SWA_Q_W, SWA_KV_W, SWA_KV_W))
    w_qf, w_kf, w_vf, w_fl, w_qs, w_ks, w_vs = (w_in[:, o[i]:o[i + 1]] for i in range(7))
    def regroup(a, axis):
        shp = a.shape
        a = jnp.moveaxis(a, axis, 0).reshape(N_SWA_KV, N_SWA // N_SWA_KV, HD, -1)
        return jnp.moveaxis(jnp.swapaxes(a, 0, 1).reshape(N_SWA * HD, -1), 0, axis).reshape(shp)

    w_all = jnp.concatenate([w_kf, regroup(w_qs, 1), w_ks, w_vs], axis=1).astype(bf16)
    wqt = w_qf.T.astype(bf16)
    wvt = w_vf.T.astype(bf16)
    wfl = w_fl.T.astype(bf16)
    bfc = b_forget.reshape(N_FOX, 1).astype(f32)
    half = HD // 2
    inv_freq = ROPE_THETA ** (-jnp.arange(half, dtype=f32) / half)
    per_row = LANES // half
    invf = jnp.tile(inv_freq, per_row).reshape(1, LANES)
    pos4 = jnp.repeat(positions.reshape(per_row, T // per_row).T.astype(i32), half, axis=1)
    cos, sin = _rope_table(pos4, invf)
    w_out = jnp.concatenate([w_mix_out[:FOX_W], regroup(w_mix_out[FOX_W:], 0)], axis=0).astype(bf16)

    tq = 512
    (qt, kf, vt, qs, ks, vs, lf, qn, kn), (eg16, eu16, ed16) = _in_proj(
        x2, cos, sin, w_all, wqt, wvt, wfl, bfc, w_eg, w_eu, w_ed, tq)
    c, ca = _cumsum(lf, S)
    c4 = c.reshape(N_FOX // 2, 2, T // tq, tq)
    r3 = lambda a: a.reshape(B, S, a.shape[-1])
    o_fox = _fox(_fox_first_blocks(qn, kn, c, B, S, tq), qt, r3(kf), r3(ca), vt, c4, B, S, tq)
    o_swa = _swa(sinks.astype(f32) * LOG2E, r3(qs), r3(ks), r3(vs), B, S)

    kx, vx = _kvproj(mem.reshape(-1, D), w_xkv.astype(bf16))

    gpad = EPG - N_GROUPS
    wr = jnp.concatenate([jnp.pad(w_rg, ((0, 0), (0, gpad))),
                          jnp.transpose(w_re, (1, 0, 2)).reshape(D, N_EXPERTS)], axis=1)
    wr = jnp.pad(wr, ((0, 0), (0, LANES - wr.shape[1]))).astype(f32)
    wrh = wr.astype(bf16)
    wr2 = jnp.concatenate([wrh, (wr - wrh.astype(f32)).astype(bf16)], axis=1)
    br = jnp.pad(jnp.concatenate([jnp.pad(b_rg, (0, gpad)), b_re.reshape(-1)]), (0, LANES - EPG - N_EXPERTS))
    br = br.reshape(1, LANES).astype(f32)
    v2 = lambda a: a.reshape(1, D).astype(f32)
    h2x, bk8 = _mid(x2, o_fox.reshape(T, FOX_W), o_swa.reshape(T, SWA_Q_W), w_out,
                    v2(ln_mix_g), v2(ln_mix_b), w_xq.astype(bf16), kx, vx, w_xout.astype(bf16),
                    v2(ln_x_g), v2(ln_x_b), wr2, br, alpha, S)

    rank, cnt = _rank(bk8)
    counts = cnt[:, 0].astype(i32)
    padded = ((counts + ROW_BLOCK - 1) // ROW_BLOCK) * ROW_BLOCK
    pad_end = jnp.cumsum(padded)
    pad_start = (pad_end - padded).astype(i32)
    unit = int(np.lcm(SC_CORES * SC_SUBCORES * 2 * SC_GATHER_ROWS, ROW_BLOCK))
    scale = -(-(T + N_BUCKETS * ROW_BLOCK) // (unit * sum(MOE_CHUNK_SHARES)))
    chunk_blocks = [share * scale * unit // ROW_BLOCK for share in MOE_CHUNK_SHARES]
    nblk = sum(chunk_blocks)
    P = nblk * ROW_BLOCK
    used = (pad_end[-1] // ROW_BLOCK).astype(i32).reshape(1)
    blk_row = jnp.arange(nblk, dtype=i32)[:, None] * ROW_BLOCK
    blk_bucket = jnp.minimum(jnp.sum((pad_end[None, :] <= blk_row).astype(i32), axis=1), N_BUCKETS - 1)
    pick = (blk_bucket[:, None] == jnp.arange(LANES, dtype=i32)[None, :]).astype(i32)
    blk_a = jnp.sum(pick * jnp.asarray(_PAIR_A)[None, :], axis=1)
    blk_b = jnp.sum(pick * jnp.asarray(_PAIR_B)[None, :], axis=1)
    blk_g = blk_bucket // N_PAIRS
    by_group = lambda w: w.reshape((N_GROUPS, EPG) + w.shape[1:])

    dest = _dest(bk8, rank, pad_start.astype(f32).reshape(LANES, 1))[0]
    row_tok = _sc_invert(dest, P)
    ys = None
    lo = 0
    for cblk in chunk_blocks:
        xs = _sc_gather_rows(row_tok[lo * ROW_BLOCK:(lo + cblk) * ROW_BLOCK], h2x)
        used_c = jnp.clip(used - lo, 0, cblk)
        ys = _experts(blk_g[lo:lo + cblk], blk_a[lo:lo + cblk], blk_b[lo:lo + cblk], used_c, xs,
                      by_group(eg16), by_group(eu16), by_group(ed16), v2(ln_f_g), v2(ln_f_b), alpha,
                      ys, lo, P)
        lo += cblk
    return _sc_gather_rows(dest, ys).reshape(B, S, D)


def kernel(x, mem, positions, w_in, b_forget, sinks, w_mix_out, ln_mix_g, ln_mix_b, w_xq, w_xkv, w_xout,
           ln_x_g, ln_x_b, w_route_group, b_route_group, w_route_expert, b_route_expert,
           w_exp_gate, w_exp_up, w_exp_down, ln_ffn_g, ln_ffn_b):
    depth = w_in.shape[0]
    alpha = (2.0 * depth) ** 0.25
    h = x
    for l in range(depth):
        h = _layer(h, mem, positions, w_in[l], b_forget[l], sinks[l], w_mix_out[l], ln_mix_g[l], ln_mix_b[l],
                   w_xq[l], w_xkv[l], w_xout[l], ln_x_g[l], ln_x_b[l], w_route_group[l], b_route_group[l],
                   w_route_expert[l], b_route_expert[l], w_exp_gate[l], w_exp_up[l], w_exp_down[l],
                   ln_ffn_g[l], ln_ffn_b[l], alpha)
    return h
```

```python
import functools

import jax
import jax.numpy as jnp
import numpy as np
from jax import lax
from jax.experimental import pallas as pl
from jax.experimental.pallas import tpu as pltpu
from jax.experimental.pallas import tpu_sc as plsc

f32 = jnp.float32
bf16 = jnp.bfloat16
i32 = jnp.int32

D = 1024
HD = 64
N_FOX = 8
N_SWA = 8
N_SWA_KV = 2
FOX_W = 512
SWA_Q_W = 512
SWA_KV_W = 128
WINDOW = 128
ROPE_THETA = 10000.0
N_XH = 4
XHD = 256
N_GROUPS = 4
EPG = 8
N_EXPERTS = 32
LN_EPS = 1e-5
NEG = -1e30
LOG2E = 1.4426950408889634
L_ROW = (HD, 0)
SKIP_LOG2 = 160.0

SC_CORES = 2
SC_SUBCORES = 16
SC_LANES = 16
SC_GATHER_ROWS = 32
MOE_CHUNK_SHARES = (3, 7, 7, 7)
LANES = 128
ROW_BLOCK = 128
N_PAIRS = EPG * (EPG - 1) // 2
N_BUCKETS = N_GROUPS * N_PAIRS
XW = D + LANES
VMEM_LIMIT = 56 * 1024 * 1024


def _cparams(sem):
    return pltpu.CompilerParams(dimension_semantics=sem, vmem_limit_bytes=VMEM_LIMIT)


def _layer_norm(v, g, b):
    mu = jnp.mean(v, axis=-1, keepdims=True)
    c = v - mu
    var = jnp.mean(c * c, axis=-1, keepdims=True)
    return c * lax.rsqrt(var + LN_EPS) * g + b


def _dot(a, b):
    return jnp.dot(a, b, preferred_element_type=f32)


def _dot_nt(a, b):
    return lax.dot_general(a, b, (((1,), (1,)), ((), ())), preferred_element_type=f32)


def _rope_table_kernel(pos_ref, invf_ref, cos_ref, sin_ref):
    ang = pos_ref[...].astype(f32) * invf_ref[...]
    cos_ref[...] = jnp.cos(ang)
    sin_ref[...] = jnp.sin(ang)


def _rope_table(pos4, invf, rows=1024):
    R = pos4.shape[0]
    blk = pl.BlockSpec((rows, LANES), lambda i: (i, 0))
    return pl.pallas_call(
        _rope_table_kernel, grid=(R // rows,),
        in_specs=[blk, pl.BlockSpec((1, LANES), lambda i: (0, 0))], out_specs=[blk, blk],
        out_shape=[jax.ShapeDtypeStruct((R, LANES), f32)] * 2,
        compiler_params=_cparams(("parallel",)), name="rope_table",
    )(pos4, invf)


def _inproj_kernel(x_ref, cos_ref, sin_ref, w_ref, wqt_ref, wvt_ref, wfl_ref, bf_ref, ind_ref,
                   eg_ref, eu_ref, ed_ref,
                   qt_ref, kf_ref, vt_ref, qs_ref, ks_ref, vs_ref, lf_ref, qn_ref, kn_ref,
                   egb_ref, eub_ref, edb_ref):
    tm = x_ref.shape[0]
    xb = x_ref[...].astype(bf16)
    egb_ref[...] = eg_ref[...].astype(bf16)
    eub_ref[...] = eu_ref[...].astype(bf16)
    edb_ref[...] = ed_ref[...].astype(bf16)

    def proj(lo, hi):
        return _dot(xb, w_ref[:, lo:hi])

    qv = _dot_nt(wqt_ref[...], xb) * (0.125 * LOG2E)
    qt_ref[0] = qv.astype(bf16)
    vt_ref[0] = _dot_nt(wvt_ref[...], xb).astype(bf16)
    kv = proj(0, 512)
    kf_ref[...] = kv.astype(bf16)
    q2 = jnp.sum((qv * qv).reshape(N_FOX, HD, tm), axis=1)
    qn_ref[0] = jnp.broadcast_to(jnp.max(q2, axis=1, keepdims=True), (N_FOX, LANES))
    k2 = _dot((kv * kv).astype(bf16), ind_ref[...])
    kn_ref[0] = jnp.broadcast_to(jnp.max(k2, axis=0, keepdims=True), (N_FOX, LANES))

    half = HD // 2
    reps = LANES // half
    quarter = pl.program_id(0) // (pl.num_programs(0) // reps)

    def spread(tab):
        pick = tab[:, :half]
        for k in range(1, reps):
            pick = jnp.where(quarter == k, tab[:, k * half:(k + 1) * half], pick)
        return jnp.concatenate([pick] * reps, axis=1)

    cos = spread(cos_ref[...])
    sin = spread(sin_ref[...])
    lane = lax.broadcasted_iota(i32, (tm, LANES), 1)
    lo_half = (lane % HD) < (HD // 2)
    sin_s = jnp.where(lo_half, -sin, sin)

    def rope(z):
        rot = jnp.where(lo_half, pltpu.roll(z, LANES - HD // 2, 1), pltpu.roll(z, HD // 2, 1))
        return z * cos + rot * sin_s

    zq = proj(512, 1024)
    for g in range(4):
        sl = slice(g * LANES, (g + 1) * LANES)
        qs_ref[:, sl] = (rope(zq[:, sl]) * (0.125 * LOG2E)).astype(bf16)
    zkv = proj(1024, 1280)
    ks_ref[...] = rope(zkv[:, :SWA_KV_W]).astype(bf16)
    vs_ref[...] = zkv[:, SWA_KV_W:].astype(bf16)

    z = _dot_nt(wfl_ref[...], xb) + bf_ref[...]
    lf_ref[...] = jnp.minimum(z, 0.0) - jnp.log(1.0 + jnp.exp(-jnp.abs(z)))


def _in_proj(x2, cos, sin, w_all, wqt, wvt, wfl, bfc, w_eg, w_eu, w_ed, tm):
    T = x2.shape[0]
    steps = T // tm
    row = lambda w: pl.BlockSpec((tm, w), lambda i: (i, 0))
    full = lambda a: pl.BlockSpec(a.shape, lambda i: (0,) * a.ndim)
    fmaj = pl.BlockSpec((1, FOX_W, tm), lambda i: (i, 0, 0))
    flat = [w.reshape(-1, w.shape[-1]) for w in (w_eg, w_eu, w_ed)]
    slices = [pl.BlockSpec((w.shape[0] // steps, w.shape[1]), lambda i: (i, 0)) for w in flat]
    ind = jnp.asarray(np.arange(FOX_W)[:, None] // HD == np.arange(LANES)[None, :], bf16)
    nrm = pl.BlockSpec((1, N_FOX, LANES), lambda i: (i, 0, 0))
    tab_steps = cos.shape[0] // tm
    tab = pl.BlockSpec((tm, LANES), lambda i: (i % tab_steps, 0))
    outs = pl.pallas_call(
        _inproj_kernel,
        grid=(steps,),
        in_specs=[row(D), tab, tab, full(w_all), full(wqt), full(wvt), full(wfl), full(bfc), full(ind)] + slices,
        out_specs=[fmaj, row(512), fmaj, row(512), row(128), row(128),
                   pl.BlockSpec((N_FOX, tm), lambda i: (0, i)), nrm, nrm] + slices,
        out_shape=[jax.ShapeDtypeStruct((steps, FOX_W, tm), bf16), jax.ShapeDtypeStruct((T, 512), bf16),
                   jax.ShapeDtypeStruct((steps, FOX_W, tm), bf16), jax.ShapeDtypeStruct((T, 512), bf16),
                   jax.ShapeDtypeStruct((T, 128), bf16), jax.ShapeDtypeStruct((T, 128), bf16),
                   jax.ShapeDtypeStruct((N_FOX, T), f32),
                   jax.ShapeDtypeStruct((steps, N_FOX, LANES), f32), jax.ShapeDtypeStruct((steps, N_FOX, LANES), f32)]
        + [jax.ShapeDtypeStruct(w.shape, bf16) for w in flat],
        compiler_params=_cparams(("parallel",)),
        name="in_proj",
    )(x2, cos, sin, w_all, wqt, wvt, wfl, bfc, ind, *flat)
    experts_bf16 = [o.reshape(w.shape) for o, w in zip(outs[9:], (w_eg, w_eu, w_ed))]
    return outs[:9], experts_bf16


def _cumsum_kernel(lf_ref, c_ref, ca_ref):
    S = lf_ref.shape[1]
    ch = 256
    r = lax.broadcasted_iota(i32, (ch, ch), 0)
    c = lax.broadcasted_iota(i32, (ch, ch), 1)
    tri = (r <= c).astype(f32)
    eye = (r == c).astype(bf16)
    stacked = jnp.concatenate([lf_ref[:, j * ch:(j + 1) * ch] for j in range(S // ch)], axis=0)
    local = jnp.dot(stacked, tri, precision=lax.Precision.HIGHEST, preferred_element_type=f32)
    carry = jnp.zeros((N_FOX, 1), f32)
    for j in range(S // ch):
        cc = local[j * N_FOX:(j + 1) * N_FOX] + carry
        carry = cc[:, ch - 1:ch]
        c2 = cc * LOG2E
        c_ref[:, j * ch:(j + 1) * ch] = c2
        neg = -c2
        hi = neg.astype(bf16)
        r1 = neg - hi.astype(f32)
        mid = r1.astype(bf16)
        lo = (r1 - mid.astype(f32)).astype(bf16)
        terms = jnp.concatenate([hi, mid, lo, jnp.zeros((LANES - 3 * N_FOX, ch), bf16)], axis=0)
        ca_ref[j * ch:(j + 1) * ch, :] = _dot_nt(eye, terms).astype(bf16)


def _cumsum(lf, S):
    T = lf.shape[1]
    spec = pl.BlockSpec((N_FOX, S), lambda b: (0, b))
    return pl.pallas_call(
        _cumsum_kernel, grid=(T // S,), in_specs=[spec],
        out_specs=[spec, pl.BlockSpec((S, LANES), lambda b: (b, 0))],
        out_shape=[jax.ShapeDtypeStruct((N_FOX, T), f32), jax.ShapeDtypeStruct((T, LANES), bf16)],
        compiler_params=_cparams(("parallel",)), name="cumsum",
    )(lf)


def _fox_kernel(j0_ref, qt_ref, k_ref, ca_ref, vt_ref, c_ref, o_ref, t0_ref, t1_ref, *, tq):
    hp = pl.program_id(1)
    i = pl.program_id(2)
    qt = qt_ref[0]
    row = lax.broadcasted_iota(i32, (LANES, tq), 0)
    is_a = row < HD
    zero = jnp.zeros_like(qt)
    q_ops = []
    for h in range(2):
        ones = jnp.where(((row & 7) == 2 * hp + h) & (row < 3 * N_FOX), 1.0, 0.0).astype(bf16)
        qh = jnp.where(is_a, qt, zero) if h == 0 else jnp.where(is_a, zero, qt)
        q_ops.append(jnp.concatenate([qh, ones], axis=0))
    kr = lax.broadcasted_iota(i32, (tq, tq), 0)
    qc = lax.broadcasted_iota(i32, (tq, tq), 1)
    causal = kr <= qc
    cq = [c_ref[0, h, pl.ds(i, 1), :] for h in range(2)]

    def scores(j, t_ref):
        off = pl.multiple_of(j * tq, tq)
        kblk = jnp.concatenate([k_ref[0, pl.ds(off, tq), :], ca_ref[0, pl.ds(off, tq), :]], axis=1)
        for h in range(2):
            t_ref[h] = _dot(kblk, q_ops[h])

    keep = [jnp.where(is_a, 1.0, 0.0).astype(bf16), jnp.where(is_a, 0.0, 1.0).astype(bf16)]
    ones_row = [jnp.where(row == L_ROW[h], 1.0, 0.0).astype(bf16) for h in range(2)]

    def softmax_pv(j, t_ref, carry, masked):
        vt = vt_ref[j]
        vts = [vt * keep[h] + ones_row[h] for h in range(2)]
        new = []
        for h in range(2):
            m, acc = carry[h]
            t = t_ref[h]
            if masked:
                t = jnp.where(causal, t, NEG)
            m_new = jnp.maximum(m, jnp.max(t, axis=0, keepdims=True) + cq[h])
            alpha = jnp.exp2(m - m_new)
            p = jnp.exp2(t + (cq[h] - m_new))
            acc = alpha * acc + _dot(vts[h], p.astype(bf16))
            new.append((m_new, acc))
        return tuple(new)

    j0 = j0_ref[(pl.program_id(0) * pl.num_programs(1) + hp) * pl.num_programs(2) + i]
    n_full = i - j0

    def pair(k, carry):
        j = j0 + 2 * k
        scores(j + 1, t1_ref)
        carry = softmax_pv(j, t0_ref, carry, False)
        scores(j + 2, t0_ref)
        return softmax_pv(j + 1, t1_ref, carry, False)

    def odd_tail(carry):
        scores(i, t1_ref)
        carry = softmax_pv(i - 1, t0_ref, carry, False)
        return softmax_pv(i, t1_ref, carry, True)

    def even_tail(carry):
        return softmax_pv(i, t0_ref, carry, True)

    init = tuple((jnp.full((1, tq), NEG, f32), jnp.zeros((LANES, tq), f32)) for _ in range(2))
    scores(j0, t0_ref)
    carry = lax.fori_loop(0, n_full // 2, pair, init)
    (_, acca), (_, accb) = lax.cond(n_full % 2 == 1, odd_tail, even_tail, carry)
    la = acca[L_ROW[0]:L_ROW[0] + 1, :]
    lb = accb[L_ROW[1]:L_ROW[1] + 1, :]
    ot = jnp.where(is_a, acca / la, accb / lb)
    o_ref[0] = jnp.transpose(ot).astype(bf16)


def _fox_first_blocks(qn, kn, c, B, S, tq):
    nq = S // tq
    qmax = jnp.sqrt(qn[:, :, 0].reshape(B, nq, N_FOX))
    kmax = jnp.sqrt(kn[:, 0, :N_FOX].reshape(B, nq, N_FOX))
    qk = 1.02 * qmax[:, :, None, :] * (kmax[:, None, :, :] + kmax[:, :, None, :])
    cb = c.reshape(N_FOX, B, nq, tq)
    c_first = jnp.transpose(cb[..., 0], (1, 2, 0))
    c_last = jnp.transpose(cb[..., tq - 1], (1, 2, 0))
    gap = c_last[:, None, :, :] - c_first[:, :, None, :]
    blk = jnp.arange(nq)
    negligible = (qk - gap < -SKIP_LOG2) & (blk[None, :] < blk[:, None])[None, :, :, None]
    pair = jnp.all(negligible.reshape(B, nq, nq, N_FOX // 2, 2), axis=-1)
    j0 = jnp.sum(jnp.cumprod(pair.astype(i32), axis=2), axis=2)
    return jnp.transpose(j0, (0, 2, 1)).reshape(-1).astype(i32)


def _fox(j0, qt, kf, ca, vt, c4, B, S, tq):
    nq = S // tq
    kernel = functools.partial(_fox_kernel, tq=tq)
    return pl.pallas_call(
        kernel,
        grid=(B, N_FOX // 2, nq),
        in_specs=[
            pl.BlockSpec(memory_space=pltpu.SMEM),
            pl.BlockSpec((1, LANES, tq), lambda b, hp, i: (b * nq + i, hp, 0)),
            pl.BlockSpec((1, S, LANES), lambda b, hp, i: (b, 0, hp)),
            pl.BlockSpec((1, S, LANES), lambda b, hp, i: (b, 0, 0)),
            pl.BlockSpec((nq, LANES, tq), lambda b, hp, i: (b, hp, 0)),
            pl.BlockSpec((1, 2, nq, tq), lambda b, hp, i: (hp, 0, b, 0)),
        ],
        out_specs=pl.BlockSpec((1, tq, LANES), lambda b, hp, i: (b, i, hp)),
        out_shape=jax.ShapeDtypeStruct((B, S, FOX_W), bf16),
        scratch_shapes=[pltpu.VMEM((2, tq, tq), f32), pltpu.VMEM((2, tq, tq), f32)],
        compiler_params=_cparams(("parallel", "parallel", "arbitrary")),
        name="fox",
    )(j0, qt, kf, ca, vt, c4)


def _swa_kernel(sink_ref, q_ref, k_ref, v_ref, o_ref, *, tq):
    W = WINDOW
    nsub = tq // W
    n0 = pl.program_id(1) * nsub
    lane = lax.broadcasted_iota(i32, (W, LANES), 1)
    is0 = lane < HD
    qoff = lax.broadcasted_iota(i32, (4 * W, 2 * W), 0) % W
    cols = lax.broadcasted_iota(i32, (4 * W, 2 * W), 1)
    bias_mid = jnp.where((cols - W <= qoff) & (qoff - (cols - W) < W), 0.0, NEG)
    bias_first = jnp.where(cols <= qoff, 0.0, NEG)
    rgrp = lax.broadcasted_iota(i32, (4 * W, 1), 0) // W
    for r in range(nsub):
        nb = n0 + r
        kstart = pl.multiple_of(jnp.maximum(nb * W - W, 0), W)
        ks = k_ref[0, pl.ds(kstart, 2 * W), :]
        vs = v_ref[0, pl.ds(kstart, 2 * W), :]
        bias = jnp.where(nb == 0, bias_first, bias_mid)
        outs = []
        for kv in range(2):
            keep = is0 if kv == 0 else jnp.logical_not(is0)
            parts = []
            for g in range(4):
                qg = q_ref[0, r * W:(r + 1) * W, g * LANES:(g + 1) * LANES]
                parts.append(jnp.where(keep, qg, jnp.zeros_like(qg)))
            qstack = jnp.concatenate(parts, axis=0)
            s = bias + _dot_nt(qstack, ks)
            sink = jnp.zeros((4 * W, 1), f32)
            for g in range(4):
                sink = jnp.where(rgrp == g, sink_ref[kv * 4 + g], sink)
            m = jnp.maximum(jnp.max(s, axis=1, keepdims=True), sink)
            e = jnp.exp2(s - m)
            den = jnp.sum(e, axis=1, keepdims=True) + jnp.exp2(sink - m)
            outs.append(_dot(e.astype(bf16), vs) / den)
        for g in range(4):
            og = jnp.where(is0, outs[0][g * W:(g + 1) * W], outs[1][g * W:(g + 1) * W])
            o_ref[0, r * W:(r + 1) * W, g * LANES:(g + 1) * LANES] = og.astype(bf16)


def _swa(sinks, qs, ks, vs, B, S, tq=512):
    kernel = functools.partial(_swa_kernel, tq=tq)
    return pl.pallas_call(
        kernel,
        grid=(B, S // tq),
        in_specs=[
            pl.BlockSpec(memory_space=pltpu.SMEM),
            pl.BlockSpec((1, tq, SWA_Q_W), lambda b, i: (b, i, 0)),
            pl.BlockSpec((1, S, SWA_KV_W), lambda b, i: (b, 0, 0)),
            pl.BlockSpec((1, S, SWA_KV_W), lambda b, i: (b, 0, 0)),
        ],
        out_specs=pl.BlockSpec((1, tq, SWA_Q_W), lambda b, i: (b, i, 0)),
        out_shape=jax.ShapeDtypeStruct((B, S, SWA_Q_W), bf16),
        compiler_params=_cparams(("parallel", "arbitrary")),
        name="swa",
    )(sinks, qs, ks, vs)


def _kvproj_kernel(m_ref, w_ref, k_ref, v_ref):
    mb = m_ref[...].astype(bf16)
    k_ref[...] = _dot(mb, w_ref[:, :D]).astype(bf16)
    v_ref[...] = _dot(mb, w_ref[:, D:]).astype(bf16)


def _kvproj(mem2, w_xkv, tm=512):
    R = mem2.shape[0]
    row = pl.BlockSpec((tm, D), lambda i: (i, 0))
    return pl.pallas_call(
        _kvproj_kernel, grid=(R // tm,),
        in_specs=[row, pl.BlockSpec(w_xkv.shape, lambda i: (0, 0))],
        out_specs=[row, row],
        out_shape=[jax.ShapeDtypeStruct((R, D), bf16)] * 2,
        compiler_params=_cparams(("parallel",)), name="kvproj",
    )(mem2, w_xkv)


def _mid_kernel(x_ref, of_ref, os_ref, wo_ref, g1_ref, b1_ref, wq_ref, k_ref, v_ref,
                wxo_ref, g2_ref, b2_ref, wr_ref, br_ref,
                h2_ref, bk_ref, oc_ref, *, alpha):
    tm = x_ref.shape[0]
    mix = _dot(of_ref[...], wo_ref[:FOX_W, :]) + _dot(os_ref[...], wo_ref[FOX_W:, :])
    h1 = _layer_norm(alpha * x_ref[...] + mix, g1_ref[...], b1_ref[...])

    q = (_dot(h1.astype(bf16), wq_ref[...]) * 0.0625).astype(bf16)
    for h in range(N_XH):
        sl = slice(h * XHD, (h + 1) * XHD)
        s = _dot_nt(q[:, sl], k_ref[:, sl])
        e = jnp.exp(s - jnp.max(s, axis=1, keepdims=True))
        p = e / jnp.sum(e, axis=1, keepdims=True)
        oc_ref[:, sl] = _dot(p.astype(bf16), v_ref[:, sl]).astype(bf16)
    xo = _dot(oc_ref[...], wxo_ref[...])
    h2 = _layer_norm(alpha * h1 + xo, g2_ref[...], b2_ref[...])
    h2_ref[:, :D] = h2

    hh = h2.astype(bf16)
    hl = (h2 - hh.astype(f32)).astype(bf16)
    hi_terms = _dot(hh, wr_ref[...])
    lg = hi_terms[:, :LANES] + _dot(hl, wr_ref[:, :LANES]) + hi_terms[:, LANES:] + br_ref[...]

    lgt = jnp.transpose(lg)
    row = lax.broadcasted_iota(i32, (EPG, tm), 0).astype(f32)
    big = float(EPG)

    def first_max(vals, mask):
        vm = jnp.where(mask, vals, NEG)
        top = jnp.max(vm, axis=0, keepdims=True)
        idx = jnp.min(jnp.where(mask & (vm == top), row, big), axis=0, keepdims=True)
        return top, idx

    gl = lgt[0:EPG]
    gmask = row < float(N_GROUPS)
    gmax, gidx = first_max(gl, gmask)
    g_val = 1.0 / jnp.sum(jnp.where(gmask, jnp.exp(gl - gmax), 0.0), axis=0, keepdims=True)
    sel = jnp.zeros((EPG, tm), f32)
    for g in range(N_GROUPS):
        sel = jnp.where(gidx == float(g), lgt[EPG * (g + 1):EPG * (g + 2)], sel)
    every = row >= 0.0
    v1, e1 = first_max(sel, every)
    v2, e2 = first_max(sel, row != e1)
    ex = jnp.exp(v2 - v1)
    w1 = g_val * (1.0 / (1.0 + ex))
    w2 = g_val * (ex / (1.0 + ex))
    first_low = e1 < e2
    ea = jnp.where(first_low, e1, e2)
    eb = jnp.where(first_low, e2, e1)
    ga = jnp.where(first_low, w1, w2)
    gb = jnp.where(first_low, w2, w1)
    pidx = ea * float(EPG - 1) - ea * (ea - 1.0) * 0.5 + (eb - ea - 1.0)
    bucket = gidx * float(N_PAIRS) + pidx

    bk_ref[...] = jnp.broadcast_to(bucket, (EPG, tm))
    gates = jnp.where(row == 0.0, ga, jnp.where(row == 1.0, gb, 0.0))
    gates = jnp.concatenate([gates, jnp.zeros((LANES - EPG, tm), f32)], axis=0)
    h2_ref[:, D:] = jnp.transpose(gates)


def _mid(x2, of2, os2, w_out, g1, b1, wq, kx, vx, wxo, g2, b2, wr2, br, alpha, S, tm=1024):
    T = x2.shape[0]
    M = kx.shape[0] // (T // S)
    per_b = S // tm
    row = lambda w: pl.BlockSpec((tm, w), lambda i: (i, 0))
    full = lambda a: pl.BlockSpec(a.shape, lambda i: (0,) * a.ndim)
    kvspec = pl.BlockSpec((M, D), lambda i: (i // per_b, 0))
    kernel = functools.partial(_mid_kernel, alpha=alpha)
    return pl.pallas_call(
        kernel,
        grid=(T // tm,),
        in_specs=[row(D), row(512), row(512), full(w_out), full(g1), full(b1), full(wq),
                  kvspec, kvspec, full(wxo), full(g2), full(b2), full(wr2), full(br)],
        out_specs=[row(XW), pl.BlockSpec((8, tm), lambda i: (0, i))],
        out_shape=[jax.ShapeDtypeStruct((T, XW), f32), jax.ShapeDtypeStruct((8, T), f32)],
        scratch_shapes=[pltpu.VMEM((tm, D), bf16)],
        compiler_params=_cparams(("parallel",)),
        name="mid",
    )(x2, of2, os2, w_out, g1, b1, wq, kx, vx, wxo, g2, b2, wr2, br)


def _rank_kernel(bk_ref, rank_ref, cnt_ref, carry_ref, *, chunk):
    sub = 256

    @pl.when(pl.program_id(0) == 0)
    def _():
        carry_ref[...] = jnp.zeros_like(carry_ref)

    r = lax.broadcasted_iota(i32, (sub, sub), 0)
    c = lax.broadcasted_iota(i32, (sub, sub), 1)
    before = (r < c).astype(bf16)
    bid = lax.broadcasted_iota(i32, (LANES, sub), 0).astype(f32)
    carry = carry_ref[...]
    for j in range(chunk // sub):
        bk = bk_ref[0:1, j * sub:(j + 1) * sub]
        hit = bid == bk
        oh = jnp.where(hit, 1.0, 0.0)
        prior = _dot(oh.astype(bf16), before) + carry
        rank_ref[:, j * sub:(j + 1) * sub] = jnp.sum(jnp.where(hit, prior, 0.0), axis=0, keepdims=True)
        carry = carry + jnp.sum(oh, axis=1, keepdims=True)
    carry_ref[...] = carry
    cnt_ref[...] = carry


def _rank(bk8, chunk=8192):
    T = bk8.shape[1]
    kernel = functools.partial(_rank_kernel, chunk=chunk)
    return pl.pallas_call(
        kernel, grid=(T // chunk,),
        in_specs=[pl.BlockSpec((8, chunk), lambda i: (0, i))],
        out_specs=[pl.BlockSpec((1, chunk), lambda i: (0, i)),
                   pl.BlockSpec((LANES, 1), lambda i: (0, 0))],
        out_shape=[jax.ShapeDtypeStruct((1, T), f32), jax.ShapeDtypeStruct((LANES, 1), f32)],
        scratch_shapes=[pltpu.VMEM((LANES, 1), f32)],
        compiler_params=_cparams(("arbitrary",)), name="rank",
    )(bk8)


def _dest_kernel(bk_ref, rank_ref, ps_ref, dest_ref):
    chunk = bk_ref.shape[1]
    bid = lax.broadcasted_iota(i32, (LANES, chunk), 0).astype(f32)
    start = jnp.sum(jnp.where(bid == bk_ref[0:1, :], ps_ref[...], 0.0), axis=0, keepdims=True)
    dest_ref[...] = (start + rank_ref[...]).astype(i32)


def _dest(bk8, rank, ps_col, chunk=8192):
    T = bk8.shape[1]
    return pl.pallas_call(
        _dest_kernel, grid=(T // chunk,),
        in_specs=[pl.BlockSpec((8, chunk), lambda i: (0, i)), pl.BlockSpec((1, chunk), lambda i: (0, i)),
                  pl.BlockSpec((LANES, 1), lambda i: (0, 0))],
        out_specs=pl.BlockSpec((1, chunk), lambda i: (0, i)),
        out_shape=jax.ShapeDtypeStruct((1, T), i32),
        compiler_params=_cparams(("parallel",)), name="dest",
    )(bk8, rank, ps_col)


def _sc_invert(dest, default):
    T = dest.shape[0]
    n_rows = default.shape[0]
    lanes = SC_LANES
    mesh = plsc.VectorSubcoreMesh(core_axis_name="core", subcore_axis_name="subcore",
                                  num_cores=SC_CORES, num_subcores=SC_SUBCORES)

    @functools.partial(pl.kernel, out_type=jax.ShapeDtypeStruct((n_rows,), i32), mesh=mesh,
                       scratch_types=[pltpu.VMEM((T,), i32), pltpu.VMEM((n_rows,), i32)],
                       compiler_params=pltpu.CompilerParams(needs_layout_passes=False),
                       name="sc_invert")
    def k(dest_hbm, default_hbm, out_hbm, dest_v, table_v):
        wid = lax.axis_index("subcore") * SC_CORES + lax.axis_index("core")

        @pl.when(wid == 0)
        def _():
            pltpu.sync_copy(dest_hbm, dest_v)
            pltpu.sync_copy(default_hbm, table_v)
            lane = lax.iota(i32, lanes)

            @pl.loop(0, T // lanes, unroll=8)
            def _(j):
                off = pl.multiple_of(j * lanes, lanes)
                plsc.store_scatter(table_v, [dest_v[pl.ds(off, lanes)]], lane + off)

            pltpu.sync_copy(table_v, out_hbm)

    return k(dest, default)


def _sc_gather_rows(idx, src, chunk=SC_GATHER_ROWS):
    n = idx.shape[0]
    w = src.shape[1]
    workers = SC_CORES * SC_SUBCORES
    per_worker = n // workers
    mesh = plsc.VectorSubcoreMesh(core_axis_name="core", subcore_axis_name="subcore",
                                  num_cores=SC_CORES, num_subcores=SC_SUBCORES)

    n_chunks = per_worker // chunk
    assert n_chunks % 2 == 0

    @functools.partial(pl.kernel, out_type=jax.ShapeDtypeStruct((n, w), src.dtype), mesh=mesh,
                       scratch_types=[pltpu.VMEM((per_worker,), i32), pltpu.VMEM((2, chunk, w), src.dtype),
                                      pltpu.SemaphoreType.DMA((2,)), pltpu.SemaphoreType.DMA((2,))],
                       name="sc_gather_rows")
    def k(src_hbm, idx_hbm, out_hbm, idx_v, rows_v, gsem, wsem):
        wid = lax.axis_index("subcore") * SC_CORES + lax.axis_index("core")
        base = wid * per_worker
        pltpu.sync_copy(idx_hbm.at[pl.ds(base, per_worker)], idx_v)

        def gather(c, slot):
            rows = idx_v.at[pl.ds(pl.multiple_of(c * chunk, chunk), chunk)]
            return pltpu.make_async_copy(src_hbm.at[rows], rows_v.at[slot], gsem.at[slot])

        def write(c, slot):
            out = out_hbm.at[pl.ds(pl.multiple_of(base + c * chunk, chunk), chunk)]
            return pltpu.make_async_copy(rows_v.at[slot], out, wsem.at[slot])

        gather(0, 0).start()

        @pl.loop(0, n_chunks // 2)
        def _(pair):
            c = 2 * pair

            @pl.when(pair > 0)
            def _():
                write(c - 1, 1).wait()

            gather(c + 1, 1).start()
            gather(c, 0).wait()
            write(c, 0).start()

            @pl.when(c + 2 < n_chunks)
            def _():
                write(c, 0).wait()
                gather(c + 2, 0).start()

            gather(c + 1, 1).wait()
            write(c + 1, 1).start()

        write(n_chunks - 2, 0).wait()
        write(n_chunks - 1, 1).wait()

    return k(src, idx)


def _expert_kernel(grp_ref, ea_ref, eb_ref, used_ref, xs_ref, wg_ref, wu_ref, wd_ref, g_ref, b_ref, y_ref,
                   z_ref, *, alpha):
    del grp_ref
    n = pl.program_id(0)
    blk = jnp.minimum(n, pl.num_programs(0) - 2)
    used = used_ref[0]

    @pl.when(n == 0)
    def _():
        z_ref[...] = jnp.zeros_like(z_ref)

    @pl.when(n <= used)
    def _():
        y_ref[...] = _layer_norm(z_ref[...], g_ref[...], b_ref[...])
        h2 = xs_ref[:, :D]
        x = h2.astype(bf16)

        def expert(e):
            a = _dot(x, wg_ref[0, e])
            u = _dot(x, wu_ref[0, e])
            act = a * (1.0 / (1.0 + jnp.exp(-a))) * u
            return _dot(act.astype(bf16), wd_ref[0, e])

        ga = xs_ref[:, D:D + 1]
        gb = xs_ref[:, D + 1:D + 2]
        z_ref[...] = alpha * h2 + ga * expert(ea_ref[blk]) + gb * expert(eb_ref[blk])

    @pl.when(n > used)
    def _():
        y_ref[...] = jnp.zeros_like(y_ref)


def _expert_kernel_into(grp_ref, ea_ref, eb_ref, used_ref, xs_ref, wg_ref, wu_ref, wd_ref, g_ref, b_ref,
                        ys_ref, y_ref, z_ref, *, alpha):
    del ys_ref
    _expert_kernel(grp_ref, ea_ref, eb_ref, used_ref, xs_ref, wg_ref, wu_ref, wd_ref, g_ref, b_ref, y_ref,
                   z_ref, alpha=alpha)


def _experts(grp, ea, eb, used, xs, wg, wu, wd, ln_g, ln_b, alpha, ys, first_block, total_rows):
    nblk = xs.shape[0] // ROW_BLOCK

    def xmap(n, grp, ea, eb, used):
        return (jnp.maximum(jnp.minimum(n, used[0] - 1), 0), 0)

    gmap = lambda n, grp, ea, eb, used: (grp[jnp.minimum(n, nblk - 1)], 0, 0, 0)
    gspec = lambda w: pl.BlockSpec((1,) + w.shape[1:], gmap, pipeline_mode=pl.Buffered(1))
    vec = pl.BlockSpec((1, D), lambda n, grp, ea, eb, used: (0, 0))
    in_specs = [pl.BlockSpec((ROW_BLOCK, XW), xmap), gspec(wg), gspec(wu), gspec(wd), vec, vec]
    operands = [grp, ea, eb, used, xs, wg, wu, wd, ln_g, ln_b]
    aliases = {}
    body = _expert_kernel
    if ys is not None:
        in_specs.append(pl.BlockSpec(memory_space=pl.ANY))
        aliases = {len(operands): 0}
        operands.append(ys)
        body = _expert_kernel_into
    grid_spec = pltpu.PrefetchScalarGridSpec(
        num_scalar_prefetch=4, grid=(nblk + 1,), in_specs=in_specs,
        out_specs=pl.BlockSpec((ROW_BLOCK, D),
                               lambda n, grp, ea, eb, used: (jnp.maximum(n - 1, 0) + first_block, 0)),
        scratch_shapes=[pltpu.VMEM((ROW_BLOCK, D), f32)],
    )
    return pl.pallas_call(
        functools.partial(body, alpha=alpha), grid_spec=grid_spec,
        out_shape=jax.ShapeDtypeStruct((total_rows, D), f32),
        input_output_aliases=aliases,
        compiler_params=_cparams(("arbitrary",)), name="experts",
    )(*operands)


def _pair_tables():
    ea = np.zeros((LANES,), np.int32)
    eb = np.zeros((LANES,), np.int32)
    for g in range(N_GROUPS):
        k = 0
        for a in range(EPG):
            for b in range(a + 1, EPG):
                ea[g * N_PAIRS + k] = a
                eb[g * N_PAIRS + k] = b
                k += 1
    return ea, eb


_PAIR_A, _PAIR_B = _pair_tables()


def _layer(h, mem, positions, w_in, b_forget, sinks, w_mix_out, ln_mix_g, ln_mix_b,
           w_xq, w_xkv, w_xout, ln_x_g, ln_x_b, w_rg, b_rg, w_re, b_re,
           w_eg, w_eu, w_ed, ln_f_g, ln_f_b, alpha):
    B, S, _ = h.shape
    T = B * S
    x2 = h.reshape(T, D)

    o = np.cumsum((0, FOX_W, FOX_W, FOX_W, N_FOX, SWA_Q_W, SWA_KV_W, SWA_KV_W))
    w_qf, w_kf, w_vf, w_fl, w_qs, w_ks, w_vs = (w_in[:, o[i]:o[i + 1]] for i in range(7))
    def regroup(a, axis):
        shp = a.shape
        a = jnp.moveaxis(a, axis, 0).reshape(N_SWA_KV, N_SWA // N_SWA_KV, HD, -1)
        return jnp.moveaxis(jnp.swapaxes(a, 0, 1).reshape(N_SWA * HD, -1), 0, axis).reshape(shp)

    w_all = jnp.concatenate([w_kf, regroup(w_qs, 1), w_ks, w_vs], axis=1).astype(bf16)
    wqt = w_qf.T.astype(bf16)
    wvt = w_vf.T.astype(bf16)
    wfl = w_fl.T.astype(bf16)
    bfc = b_forget.reshape(N_FOX, 1).astype(f32)
    half = HD // 2
    inv_freq = ROPE_THETA ** (-jnp.arange(half, dtype=f32) / half)
    per_row = LANES // half
    invf = jnp.tile(inv_freq, per_row).reshape(1, LANES)
    pos4 = jnp.repeat(positions.reshape(per_row, T // per_row).T.astype(i32), half, axis=1)
    cos, sin = _rope_table(pos4, invf)
    w_out = jnp.concatenate([w_mix_out[:FOX_W], regroup(w_mix_out[FOX_W:], 0)], axis=0).astype(bf16)

    tq = 512
    (qt, kf, vt, qs, ks, vs, lf, qn, kn), (eg16, eu16, ed16) = _in_proj(
        x2, cos, sin, w_all, wqt, wvt, wfl, bfc, w_eg, w_eu, w_ed, tq)
    c, ca = _cumsum(lf, S)
    c4 = c.reshape(N_FOX // 2, 2, T // tq, tq)
    r3 = lambda a: a.reshape(B, S, a.shape[-1])
    o_fox = _fox(_fox_first_blocks(qn, kn, c, B, S, tq), qt, r3(kf), r3(ca), vt, c4, B, S, tq)
    o_swa = _swa(sinks.astype(f32) * LOG2E, r3(qs), r3(ks), r3(vs), B, S)

    kx, vx = _kvproj(mem.reshape(-1, D), w_xkv.astype(bf16))

    gpad = EPG - N_GROUPS
    wr = jnp.concatenate([jnp.pad(w_rg, ((0, 0), (0, gpad))),
                          jnp.transpose(w_re, (1, 0, 2)).reshape(D, N_EXPERTS)], axis=1)
    wr = jnp.pad(wr, ((0, 0), (0, LANES - wr.shape[1]))).astype(f32)
    wrh = wr.astype(bf16)
    wr2 = jnp.concatenate([wrh, (wr - wrh.astype(f32)).astype(bf16)], axis=1)
    br = jnp.pad(jnp.concatenate([jnp.pad(b_rg, (0, gpad)), b_re.reshape(-1)]), (0, LANES - EPG - N_EXPERTS))
    br = br.reshape(1, LANES).astype(f32)
    v2 = lambda a: a.reshape(1, D).astype(f32)
    h2x, bk8 = _mid(x2, o_fox.reshape(T, FOX_W), o_swa.reshape(T, SWA_Q_W), w_out,
                    v2(ln_mix_g), v2(ln_mix_b), w_xq.astype(bf16), kx, vx, w_xout.astype(bf16),
                    v2(ln_x_g), v2(ln_x_b), wr2, br, alpha, S)

    rank, cnt = _rank(bk8)
    counts = cnt[:, 0].astype(i32)
    padded = ((counts + ROW_BLOCK - 1) // ROW_BLOCK) * ROW_BLOCK
    pad_end = jnp.cumsum(padded)
    pad_start = (pad_end - padded).astype(i32)
    unit = int(np.lcm(SC_CORES * SC_SUBCORES * 2 * SC_GATHER_ROWS, ROW_BLOCK))
    scale = -(-(T + N_BUCKETS * ROW_BLOCK) // (unit * sum(MOE_CHUNK_SHARES)))
    chunk_blocks = [share * scale * unit // ROW_BLOCK for share in MOE_CHUNK_SHARES]
    nblk = sum(chunk_blocks)
    P = nblk * ROW_BLOCK
    used = (pad_end[-1] // ROW_BLOCK).astype(i32).reshape(1)
    blk_row = jnp.arange(nblk, dtype=i32)[:, None] * ROW_BLOCK
    blk_bucket = jnp.minimum(jnp.sum((pad_end[None, :] <= blk_row).astype(i32), axis=1), N_BUCKETS - 1)
    pick = (blk_bucket[:, None] == jnp.arange(LANES, dtype=i32)[None, :]).astype(i32)
    blk_a = jnp.sum(pick * jnp.asarray(_PAIR_A)[None, :], axis=1)
    blk_b = jnp.sum(pick * jnp.asarray(_PAIR_B)[None, :], axis=1)
    blk_g = blk_bucket // N_PAIRS
    by_group = lambda w: w.reshape((N_GROUPS, EPG) + w.shape[1:])

    dest = _dest(bk8, rank, pad_start.astype(f32).reshape(LANES, 1))[0]
    row_tok = _sc_invert(dest, jnp.arange(P, dtype=i32) % T)
    ys = None
    lo = 0
    for cblk in chunk_blocks:
        xs = _sc_gather_rows(row_tok[lo * ROW_BLOCK:(lo + cblk) * ROW_BLOCK], h2x)
        used_c = jnp.clip(used - lo, 0, cblk)
        ys = _experts(blk_g[lo:lo + cblk], blk_a[lo:lo + cblk], blk_b[lo:lo + cblk], used_c, xs,
                      by_group(eg16), by_group(eu16), by_group(ed16), v2(ln_f_g), v2(ln_f_b), alpha,
                      ys, lo, P)
        lo += cblk
    return _sc_gather_rows(dest, ys).reshape(B, S, D)


def kernel(x, mem, positions, w_in, b_forget, sinks, w_mix_out, ln_mix_g, ln_mix_b, w_xq, w_xkv, w_xout,
           ln_x_g, ln_x_b, w_route_group, b_route_group, w_route_expert, b_route_expert,
           w_exp_gate, w_exp_up, w_exp_down, ln_ffn_g, ln_ffn_b):
    depth = w_in.shape[0]
    alpha = (2.0 * depth) ** 0.25
    h = x
    for l in range(depth):
        h = _layer(h, mem, positions, w_in[l], b_forget[l], sinks[l], w_mix_out[l], ln_mix_g[l], ln_mix_b[l],
                   w_xq[l], w_xkv[l], w_xout[l], ln_x_g[l], ln_x_b[l], w_route_group[l], b_route_group[l],
                   w_route_expert[l], b_route_expert[l], w_exp_gate[l], w_exp_up[l], w_exp_down[l],
                   ln_ffn_g[l], ln_ffn_b[l], alpha)
    return h
```

```python
import functools

import jax
import jax.numpy as jnp
import numpy as np
from jax import lax
from jax.experimental import pallas as pl
from jax.experimental.pallas import tpu as pltpu
from jax.experimental.pallas import tpu_sc as plsc

f32 = jnp.float32
bf16 = jnp.bfloat16
i32 = jnp.int32

D = 1024
HD = 64
N_FOX = 8
N_SWA = 8
N_SWA_KV = 2
FOX_W = 512
SWA_Q_W = 512
SWA_KV_W = 128
WINDOW = 128
ROPE_THETA = 10000.0
N_XH = 4
XHD = 256
N_GROUPS = 4
EPG = 8
N_EXPERTS = 32
LN_EPS = 1e-5
NEG = -1e30
LOG2E = 1.4426950408889634
L_ROW = (HD, 0)
SKIP_LOG2 = 160.0

SC_CORES = 2
SC_SUBCORES = 16
SC_LANES = 16
SC_GATHER_ROWS = 32
MOE_CHUNK_SHARES = (2, 9, 13)
LANES = 128
ROW_BLOCK = 128
N_PAIRS = EPG * (EPG - 1) // 2
N_BUCKETS = N_GROUPS * N_PAIRS
XW = D + LANES
VMEM_LIMIT = 56 * 1024 * 1024


def _cparams(sem):
    return pltpu.CompilerParams(dimension_semantics=sem, vmem_limit_bytes=VMEM_LIMIT)


def _layer_norm(v, g, b):
    mu = jnp.mean(v, axis=-1, keepdims=True)
    c = v - mu
    var = jnp.mean(c * c, axis=-1, keepdims=True)
    return c * lax.rsqrt(var + LN_EPS) * g + b


def _dot(a, b):
    return jnp.dot(a, b, preferred_element_type=f32)


def _dot_nt(a, b):
    return lax.dot_general(a, b, (((1,), (1,)), ((), ())), preferred_element_type=f32)


def _rope_table_kernel(pos_ref, invf_ref, cos_ref, sin_ref):
    ang = pos_ref[...].astype(f32) * invf_ref[...]
    cos_ref[...] = jnp.cos(ang)
    sin_ref[...] = jnp.sin(ang)


def _rope_table(pos4, invf, rows=1024):
    R = pos4.shape[0]
    blk = pl.BlockSpec((rows, LANES), lambda i: (i, 0))
    return pl.pallas_call(
        _rope_table_kernel, grid=(R // rows,),
        in_specs=[blk, pl.BlockSpec((1, LANES), lambda i: (0, 0))], out_specs=[blk, blk],
        out_shape=[jax.ShapeDtypeStruct((R, LANES), f32)] * 2,
        compiler_params=_cparams(("parallel",)), name="rope_table",
    )(pos4, invf)


def _inproj_kernel(x_ref, cos_ref, sin_ref, w_ref, wqt_ref, wvt_ref, wfl_ref, bf_ref, ind_ref,
                   eg_ref, eu_ref, ed_ref,
                   qt_ref, kf_ref, vt_ref, qs_ref, ks_ref, vs_ref, lf_ref, qn_ref, kn_ref,
                   egb_ref, eub_ref, edb_ref):
    tm = x_ref.shape[0]
    xb = x_ref[...].astype(bf16)
    egb_ref[...] = eg_ref[...].astype(bf16)
    eub_ref[...] = eu_ref[...].astype(bf16)
    edb_ref[...] = ed_ref[...].astype(bf16)

    def proj(lo, hi):
        return _dot(xb, w_ref[:, lo:hi])

    qv = _dot_nt(wqt_ref[...], xb) * (0.125 * LOG2E)
    qt_ref[0] = qv.astype(bf16)
    vt_ref[0] = _dot_nt(wvt_ref[...], xb).astype(bf16)
    kv = proj(0, 512)
    kf_ref[...] = kv.astype(bf16)
    q2 = jnp.sum((qv * qv).reshape(N_FOX, HD, tm), axis=1)
    qn_ref[0] = jnp.broadcast_to(jnp.max(q2, axis=1, keepdims=True), (N_FOX, LANES))
    k2 = _dot((kv * kv).astype(bf16), ind_ref[...])
    kn_ref[0] = jnp.broadcast_to(jnp.max(k2, axis=0, keepdims=True), (N_FOX, LANES))

    half = HD // 2
    reps = LANES // half
    quarter = pl.program_id(0) // (pl.num_programs(0) // reps)

    def spread(tab):
        pick = tab[:, :half]
        for k in range(1, reps):
            pick = jnp.where(quarter == k, tab[:, k * half:(k + 1) * half], pick)
        return jnp.concatenate([pick] * reps, axis=1)

    cos = spread(cos_ref[...])
    sin = spread(sin_ref[...])
    lane = lax.broadcasted_iota(i32, (tm, LANES), 1)
    lo_half = (lane % HD) < (HD // 2)
    sin_s = jnp.where(lo_half, -sin, sin)

    def rope(z):
        rot = jnp.where(lo_half, pltpu.roll(z, LANES - HD // 2, 1), pltpu.roll(z, HD // 2, 1))
        return z * cos + rot * sin_s

    zq = proj(512, 1024)
    for g in range(4):
        sl = slice(g * LANES, (g + 1) * LANES)
        qs_ref[:, sl] = (rope(zq[:, sl]) * (0.125 * LOG2E)).astype(bf16)
    zkv = proj(1024, 1280)
    ks_ref[...] = rope(zkv[:, :SWA_KV_W]).astype(bf16)
    vs_ref[...] = zkv[:, SWA_KV_W:].astype(bf16)

    z = _dot_nt(wfl_ref[...], xb) + bf_ref[...]
    lf_ref[...] = jnp.minimum(z, 0.0) - jnp.log(1.0 + jnp.exp(-jnp.abs(z)))


def _in_proj(x2, cos, sin, w_all, wqt, wvt, wfl, bfc, w_eg, w_eu, w_ed, tm):
    T = x2.shape[0]
    steps = T // tm
    row = lambda w: pl.BlockSpec((tm, w), lambda i: (i, 0))
    full = lambda a: pl.BlockSpec(a.shape, lambda i: (0,) * a.ndim)
    fmaj = pl.BlockSpec((1, FOX_W, tm), lambda i: (i, 0, 0))
    flat = [w.reshape(-1, w.shape[-1]) for w in (w_eg, w_eu, w_ed)]
    slices = [pl.BlockSpec((w.shape[0] // steps, w.shape[1]), lambda i: (i, 0)) for w in flat]
    ind = jnp.asarray(np.arange(FOX_W)[:, None] // HD == np.arange(LANES)[None, :], bf16)
    nrm = pl.BlockSpec((1, N_FOX, LANES), lambda i: (i, 0, 0))
    tab_steps = cos.shape[0] // tm
    tab = pl.BlockSpec((tm, LANES), lambda i: (i % tab_steps, 0))
    outs = pl.pallas_call(
        _inproj_kernel,
        grid=(steps,),
        in_specs=[row(D), tab, tab, full(w_all), full(wqt), full(wvt), full(wfl), full(bfc), full(ind)] + slices,
        out_specs=[fmaj, row(512), fmaj, row(512), row(128), row(128),
                   pl.BlockSpec((N_FOX, tm), lambda i: (0, i)), nrm, nrm] + slices,
        out_shape=[jax.ShapeDtypeStruct((steps, FOX_W, tm), bf16), jax.ShapeDtypeStruct((T, 512), bf16),
                   jax.ShapeDtypeStruct((steps, FOX_W, tm), bf16), jax.ShapeDtypeStruct((T, 512), bf16),
                   jax.ShapeDtypeStruct((T, 128), bf16), jax.ShapeDtypeStruct((T, 128), bf16),
                   jax.ShapeDtypeStruct((N_FOX, T), f32),
                   jax.ShapeDtypeStruct((steps, N_FOX, LANES), f32), jax.ShapeDtypeStruct((steps, N_FOX, LANES), f32)]
        + [jax.ShapeDtypeStruct(w.shape, bf16) for w in flat],
        compiler_params=_cparams(("parallel",)),
        name="in_proj",
    )(x2, cos, sin, w_all, wqt, wvt, wfl, bfc, ind, *flat)
    experts_bf16 = [o.reshape(w.shape) for o, w in zip(outs[9:], (w_eg, w_eu, w_ed))]
    return outs[:9], experts_bf16


def _cumsum_kernel(lf_ref, c_ref, ca_ref):
    S = lf_ref.shape[1]
    ch = 256
    r = lax.broadcasted_iota(i32, (ch, ch), 0)
    c = lax.broadcasted_iota(i32, (ch, ch), 1)
    tri = (r <= c).astype(f32)
    eye = (r == c).astype(bf16)
    stacked = jnp.concatenate([lf_ref[:, j * ch:(j + 1) * ch] for j in range(S // ch)], axis=0)
    local = jnp.dot(stacked, tri, precision=lax.Precision.HIGHEST, preferred_element_type=f32)
    carry = jnp.zeros((N_FOX, 1), f32)
    for j in range(S // ch):
        cc = local[j * N_FOX:(j + 1) * N_FOX] + carry
        carry = cc[:, ch - 1:ch]
        c2 = cc * LOG2E
        c_ref[:, j * ch:(j + 1) * ch] = c2
        neg = -c2
        hi = neg.astype(bf16)
        r1 = neg - hi.astype(f32)
        mid = r1.astype(bf16)
        lo = (r1 - mid.astype(f32)).astype(bf16)
        terms = jnp.concatenate([hi, mid, lo, jnp.zeros((LANES - 3 * N_FOX, ch), bf16)], axis=0)
        ca_ref[j * ch:(j + 1) * ch, :] = _dot_nt(eye, terms).astype(bf16)


def _cumsum(lf, S):
    T = lf.shape[1]
    spec = pl.BlockSpec((N_FOX, S), lambda b: (0, b))
    return pl.pallas_call(
        _cumsum_kernel, grid=(T // S,), in_specs=[spec],
        out_specs=[spec, pl.BlockSpec((S, LANES), lambda b: (b, 0))],
        out_shape=[jax.ShapeDtypeStruct((N_FOX, T), f32), jax.ShapeDtypeStruct((T, LANES), bf16)],
        compiler_params=_cparams(("parallel",)), name="cumsum",
    )(lf)


def _fox_kernel(j0_ref, qt_ref, k_ref, ca_ref, vt_ref, c_ref, o_ref, t0_ref, t1_ref, *, tq):
    hp = pl.program_id(1)
    i = pl.program_id(2)
    qt = qt_ref[0]
    row = lax.broadcasted_iota(i32, (LANES, tq), 0)
    is_a = row < HD
    zero = jnp.zeros_like(qt)
    q_ops = []
    for h in range(2):
        ones = jnp.where(((row & 7) == 2 * hp + h) & (row < 3 * N_FOX), 1.0, 0.0).astype(bf16)
        qh = jnp.where(is_a, qt, zero) if h == 0 else jnp.where(is_a, zero, qt)
        q_ops.append(jnp.concatenate([qh, ones], axis=0))
    kr = lax.broadcasted_iota(i32, (tq, tq), 0)
    qc = lax.broadcasted_iota(i32, (tq, tq), 1)
    causal = kr <= qc
    cq = [c_ref[0, h, pl.ds(i, 1), :] for h in range(2)]

    def scores(j, t_ref):
        off = pl.multiple_of(j * tq, tq)
        kblk = jnp.concatenate([k_ref[0, pl.ds(off, tq), :], ca_ref[0, pl.ds(off, tq), :]], axis=1)
        for h in range(2):
            t_ref[h] = _dot(kblk, q_ops[h])

    keep = [jnp.where(is_a, 1.0, 0.0).astype(bf16), jnp.where(is_a, 0.0, 1.0).astype(bf16)]
    ones_row = [jnp.where(row == L_ROW[h], 1.0, 0.0).astype(bf16) for h in range(2)]

    def softmax_pv(j, t_ref, carry, masked):
        vt = vt_ref[j]
        vts = [vt * keep[h] + ones_row[h] for h in range(2)]
        new = []
        for h in range(2):
            m, acc = carry[h]
            t = t_ref[h]
            if masked:
                t = jnp.where(causal, t, NEG)
            m_new = jnp.maximum(m, jnp.max(t, axis=0, keepdims=True) + cq[h])
            alpha = jnp.exp2(m - m_new)
            p = jnp.exp2(t + (cq[h] - m_new))
            acc = alpha * acc + _dot(vts[h], p.astype(bf16))
            new.append((m_new, acc))
        return tuple(new)

    j0 = j0_ref[(pl.program_id(0) * pl.num_programs(1) + hp) * pl.num_programs(2) + i]
    n_full = i - j0

    def pair(k, carry):
        j = j0 + 2 * k
        scores(j + 1, t1_ref)
        carry = softmax_pv(j, t0_ref, carry, False)
        scores(j + 2, t0_ref)
        return softmax_pv(j + 1, t1_ref, carry, False)

    def odd_tail(carry):
        scores(i, t1_ref)
        carry = softmax_pv(i - 1, t0_ref, carry, False)
        return softmax_pv(i, t1_ref, carry, True)

    def even_tail(carry):
        return softmax_pv(i, t0_ref, carry, True)

    init = tuple((jnp.full((1, tq), NEG, f32), jnp.zeros((LANES, tq), f32)) for _ in range(2))
    scores(j0, t0_ref)
    carry = lax.fori_loop(0, n_full // 2, pair, init)
    (_, acca), (_, accb) = lax.cond(n_full % 2 == 1, odd_tail, even_tail, carry)
    la = acca[L_ROW[0]:L_ROW[0] + 1, :]
    lb = accb[L_ROW[1]:L_ROW[1] + 1, :]
    ot = jnp.where(is_a, acca / la, accb / lb)
    o_ref[0] = jnp.transpose(ot).astype(bf16)


def _fox_first_blocks(qn, kn, c, B, S, tq):
    nq = S // tq
    qmax = jnp.sqrt(qn[:, :, 0].reshape(B, nq, N_FOX))
    kmax = jnp.sqrt(kn[:, 0, :N_FOX].reshape(B, nq, N_FOX))
    qk = 1.02 * qmax[:, :, None, :] * (kmax[:, None, :, :] + kmax[:, :, None, :])
    cb = c.reshape(N_FOX, B, nq, tq)
    c_first = jnp.transpose(cb[..., 0], (1, 2, 0))
    c_last = jnp.transpose(cb[..., tq - 1], (1, 2, 0))
    gap = c_last[:, None, :, :] - c_first[:, :, None, :]
    blk = jnp.arange(nq)
    negligible = (qk - gap < -SKIP_LOG2) & (blk[None, :] < blk[:, None])[None, :, :, None]
    pair = jnp.all(negligible.reshape(B, nq, nq, N_FOX // 2, 2), axis=-1)
    j0 = jnp.sum(jnp.cumprod(pair.astype(i32), axis=2), axis=2)
    return jnp.transpose(j0, (0, 2, 1)).reshape(-1).astype(i32)


def _fox(j0, qt, kf, ca, vt, c4, B, S, tq):
    nq = S // tq
    kernel = functools.partial(_fox_kernel, tq=tq)
    return pl.pallas_call(
        kernel,
        grid=(B, N_FOX // 2, nq),
        in_specs=[
            pl.BlockSpec(memory_space=pltpu.SMEM),
            pl.BlockSpec((1, LANES, tq), lambda b, hp, i: (b * nq + i, hp, 0)),
            pl.BlockSpec((1, S, LANES), lambda b, hp, i: (b, 0, hp)),
            pl.BlockSpec((1, S, LANES), lambda b, hp, i: (b, 0, 0)),
            pl.BlockSpec((nq, LANES, tq), lambda b, hp, i: (b, hp, 0)),
            pl.BlockSpec((1, 2, nq, tq), lambda b, hp, i: (hp, 0, b, 0)),
        ],
        out_specs=pl.BlockSpec((1, tq, LANES), lambda b, hp, i: (b, i, hp)),
        out_shape=jax.ShapeDtypeStruct((B, S, FOX_W), bf16),
        scratch_shapes=[pltpu.VMEM((2, tq, tq), f32), pltpu.VMEM((2, tq, tq), f32)],
        compiler_params=_cparams(("parallel", "parallel", "arbitrary")),
        name="fox",
    )(j0, qt, kf, ca, vt, c4)


def _swa_kernel(sink_ref, q_ref, k_ref, v_ref, o_ref, *, tq):
    W = WINDOW
    nsub = tq // W
    n0 = pl.program_id(1) * nsub
    lane = lax.broadcasted_iota(i32, (W, LANES), 1)
    is0 = lane < HD
    qoff = lax.broadcasted_iota(i32, (4 * W, 2 * W), 0) % W
    cols = lax.broadcasted_iota(i32, (4 * W, 2 * W), 1)
    bias_mid = jnp.where((cols - W <= qoff) & (qoff - (cols - W) < W), 0.0, NEG)
    bias_first = jnp.where(cols <= qoff, 0.0, NEG)
    rgrp = lax.broadcasted_iota(i32, (4 * W, 1), 0) // W
    for r in range(nsub):
        nb = n0 + r
        kstart = pl.multiple_of(jnp.maximum(nb * W - W, 0), W)
        ks = k_ref[0, pl.ds(kstart, 2 * W), :]
        vs = v_ref[0, pl.ds(kstart, 2 * W), :]
        bias = jnp.where(nb == 0, bias_first, bias_mid)
        outs = []
        for kv in range(2):
            keep = is0 if kv == 0 else jnp.logical_not(is0)
            parts = []
            for g in range(4):
                qg = q_ref[0, r * W:(r + 1) * W, g * LANES:(g + 1) * LANES]
                parts.append(jnp.where(keep, qg, jnp.zeros_like(qg)))
            qstack = jnp.concatenate(parts, axis=0)
            s = bias + _dot_nt(qstack, ks)
            sink = jnp.zeros((4 * W, 1), f32)
            for g in range(4):
                sink = jnp.where(rgrp == g, sink_ref[kv * 4 + g], sink)
            m = jnp.maximum(jnp.max(s, axis=1, keepdims=True), sink)
            e = jnp.exp2(s - m)
            den = jnp.sum(e, axis=1, keepdims=True) + jnp.exp2(sink - m)
            outs.append(_dot(e.astype(bf16), vs) / den)
        for g in range(4):
            og = jnp.where(is0, outs[0][g * W:(g + 1) * W], outs[1][g * W:(g + 1) * W])
            o_ref[0, r * W:(r + 1) * W, g * LANES:(g + 1) * LANES] = og.astype(bf16)


def _swa(sinks, qs, ks, vs, B, S, tq=512):
    kernel = functools.partial(_swa_kernel, tq=tq)
    return pl.pallas_call(
        kernel,
        grid=(B, S // tq),
        in_specs=[
            pl.BlockSpec(memory_space=pltpu.SMEM),
            pl.BlockSpec((1, tq, SWA_Q_W), lambda b, i: (b, i, 0)),
            pl.BlockSpec((1, S, SWA_KV_W), lambda b, i: (b, 0, 0)),
            pl.BlockSpec((1, S, SWA_KV_W), lambda b, i: (b, 0, 0)),
        ],
        out_specs=pl.BlockSpec((1, tq, SWA_Q_W), lambda b, i: (b, i, 0)),
        out_shape=jax.ShapeDtypeStruct((B, S, SWA_Q_W), bf16),
        compiler_params=_cparams(("parallel", "arbitrary")),
        name="swa",
    )(sinks, qs, ks, vs)


def _kvproj_kernel(m_ref, w_ref, k_ref, v_ref):
    mb = m_ref[...].astype(bf16)
    k_ref[...] = _dot(mb, w_ref[:, :D]).astype(bf16)
    v_ref[...] = _dot(mb, w_ref[:, D:]).astype(bf16)


def _kvproj(mem2, w_xkv, tm=512):
    R = mem2.shape[0]
    row = pl.BlockSpec((tm, D), lambda i: (i, 0))
    return pl.pallas_call(
        _kvproj_kernel, grid=(R // tm,),
        in_specs=[row, pl.BlockSpec(w_xkv.shape, lambda i: (0, 0))],
        out_specs=[row, row],
        out_shape=[jax.ShapeDtypeStruct((R, D), bf16)] * 2,
        compiler_params=_cparams(("parallel",)), name="kvproj",
    )(mem2, w_xkv)


def _mid_kernel(x_ref, of_ref, os_ref, wo_ref, g1_ref, b1_ref, wq_ref, k_ref, v_ref,
                wxo_ref, g2_ref, b2_ref, wr_ref, br_ref,
                h2_ref, bk_ref, oc_ref, *, alpha):
    tm = x_ref.shape[0]
    mix = _dot(of_ref[...], wo_ref[:FOX_W, :]) + _dot(os_ref[...], wo_ref[FOX_W:, :])
    h1 = _layer_norm(alpha * x_ref[...] + mix, g1_ref[...], b1_ref[...])

    q = (_dot(h1.astype(bf16), wq_ref[...]) * (0.0625 * LOG2E)).astype(bf16)
    for h in range(N_XH):
        sl = slice(h * XHD, (h + 1) * XHD)
        s = _dot_nt(q[:, sl], k_ref[:, sl])
        e = jnp.exp2(s - jnp.max(s, axis=1, keepdims=True))
        p = e / jnp.sum(e, axis=1, keepdims=True)
        oc_ref[:, sl] = _dot(p.astype(bf16), v_ref[:, sl]).astype(bf16)
    xo = _dot(oc_ref[...], wxo_ref[...])
    h2 = _layer_norm(alpha * h1 + xo, g2_ref[...], b2_ref[...])
    h2_ref[:, :D] = h2

    hh = h2.astype(bf16)
    hl = (h2 - hh.astype(f32)).astype(bf16)
    hi_terms = _dot(hh, wr_ref[...])
    lg = hi_terms[:, :LANES] + _dot(hl, wr_ref[:, :LANES]) + hi_terms[:, LANES:] + br_ref[...]

    lgt = jnp.transpose(lg)
    row = lax.broadcasted_iota(i32, (EPG, tm), 0).astype(f32)
    big = float(EPG)

    def first_max(vals, mask):
        vm = jnp.where(mask, vals, NEG)
        top = jnp.max(vm, axis=0, keepdims=True)
        idx = jnp.min(jnp.where(mask & (vm == top), row, big), axis=0, keepdims=True)
        return top, idx

    gl = lgt[0:EPG]
    gmask = row < float(N_GROUPS)
    gmax, gidx = first_max(gl, gmask)
    g_val = 1.0 / jnp.sum(jnp.where(gmask, jnp.exp(gl - gmax), 0.0), axis=0, keepdims=True)
    sel = jnp.zeros((EPG, tm), f32)
    for g in range(N_GROUPS):
        sel = jnp.where(gidx == float(g), lgt[EPG * (g + 1):EPG * (g + 2)], sel)
    every = row >= 0.0
    v1, e1 = first_max(sel, every)
    v2, e2 = first_max(sel, row != e1)
    ex = jnp.exp(v2 - v1)
    w1 = g_val * (1.0 / (1.0 + ex))
    w2 = g_val * (ex / (1.0 + ex))
    first_low = e1 < e2
    ea = jnp.where(first_low, e1, e2)
    eb = jnp.where(first_low, e2, e1)
    ga = jnp.where(first_low, w1, w2)
    gb = jnp.where(first_low, w2, w1)
    pidx = ea * float(EPG - 1) - ea * (ea - 1.0) * 0.5 + (eb - ea - 1.0)
    bucket = gidx * float(N_PAIRS) + pidx

    bk_ref[...] = jnp.broadcast_to(bucket, (EPG, tm))
    gates = jnp.where(row == 0.0, ga, jnp.where(row == 1.0, gb, 0.0))
    gates = jnp.concatenate([gates, jnp.zeros((LANES - EPG, tm), f32)], axis=0)
    h2_ref[:, D:] = jnp.transpose(gates)


def _mid(x2, of2, os2, w_out, g1, b1, wq, kx, vx, wxo, g2, b2, wr2, br, alpha, S, tm=1024):
    T = x2.shape[0]
    M = kx.shape[0] // (T // S)
    per_b = S // tm
    row = lambda w: pl.BlockSpec((tm, w), lambda i: (i, 0))
    full = lambda a: pl.BlockSpec(a.shape, lambda i: (0,) * a.ndim)
    kvspec = pl.BlockSpec((M, D), lambda i: (i // per_b, 0))
    kernel = functools.partial(_mid_kernel, alpha=alpha)
    return pl.pallas_call(
        kernel,
        grid=(T // tm,),
        in_specs=[row(D), row(512), row(512), full(w_out), full(g1), full(b1), full(wq),
                  kvspec, kvspec, full(wxo), full(g2), full(b2), full(wr2), full(br)],
        out_specs=[row(XW), pl.BlockSpec((8, tm), lambda i: (0, i))],
        out_shape=[jax.ShapeDtypeStruct((T, XW), f32), jax.ShapeDtypeStruct((8, T), f32)],
        scratch_shapes=[pltpu.VMEM((tm, D), bf16)],
        compiler_params=_cparams(("parallel",)),
        name="mid",
    )(x2, of2, os2, w_out, g1, b1, wq, kx, vx, wxo, g2, b2, wr2, br)


def _rank_kernel(bk_ref, rank_ref, cnt_ref, carry_ref, *, chunk):
    sub = 256

    @pl.when(pl.program_id(0) == 0)
    def _():
        carry_ref[...] = jnp.zeros_like(carry_ref)

    r = lax.broadcasted_iota(i32, (sub, sub), 0)
    c = lax.broadcasted_iota(i32, (sub, sub), 1)
    before = (r < c).astype(bf16)
    bid = lax.broadcasted_iota(i32, (LANES, sub), 0).astype(f32)
    carry = carry_ref[...]
    for j in range(chunk // sub):
        bk = bk_ref[0:1, j * sub:(j + 1) * sub]
        hit = bid == bk
        oh = jnp.where(hit, 1.0, 0.0)
        prior = _dot(oh.astype(bf16), before) + carry
        rank_ref[:, j * sub:(j + 1) * sub] = jnp.sum(jnp.where(hit, prior, 0.0), axis=0, keepdims=True)
        carry = carry + jnp.sum(oh, axis=1, keepdims=True)
    carry_ref[...] = carry
    cnt_ref[...] = carry


def _rank(bk8, chunk=8192):
    T = bk8.shape[1]
    kernel = functools.partial(_rank_kernel, chunk=chunk)
    return pl.pallas_call(
        kernel, grid=(T // chunk,),
        in_specs=[pl.BlockSpec((8, chunk), lambda i: (0, i))],
        out_specs=[pl.BlockSpec((1, chunk), lambda i: (0, i)),
                   pl.BlockSpec((LANES, 1), lambda i: (0, 0))],
        out_shape=[jax.ShapeDtypeStruct((1, T), f32), jax.ShapeDtypeStruct((LANES, 1), f32)],
        scratch_shapes=[pltpu.VMEM((LANES, 1), f32)],
        compiler_params=_cparams(("arbitrary",)), name="rank",
    )(bk8)


def _dest_kernel(bk_ref, rank_ref, ps_ref, dest_ref):
    chunk = bk_ref.shape[1]
    bid = lax.broadcasted_iota(i32, (LANES, chunk), 0).astype(f32)
    start = jnp.sum(jnp.where(bid == bk_ref[0:1, :], ps_ref[...], 0.0), axis=0, keepdims=True)
    dest_ref[...] = (start + rank_ref[...]).astype(i32)


def _dest(bk8, rank, ps_col, chunk=8192):
    T = bk8.shape[1]
    return pl.pallas_call(
        _dest_kernel, grid=(T // chunk,),
        in_specs=[pl.BlockSpec((8, chunk), lambda i: (0, i)), pl.BlockSpec((1, chunk), lambda i: (0, i)),
                  pl.BlockSpec((LANES, 1), lambda i: (0, 0))],
        out_specs=pl.BlockSpec((1, chunk), lambda i: (0, i)),
        out_shape=jax.ShapeDtypeStruct((1, T), i32),
        compiler_params=_cparams(("parallel",)), name="dest",
    )(bk8, rank, ps_col)


def _sc_invert(dest, default):
    T = dest.shape[0]
    n_rows = default.shape[0]
    lanes = SC_LANES
    mesh = plsc.VectorSubcoreMesh(core_axis_name="core", subcore_axis_name="subcore",
                                  num_cores=SC_CORES, num_subcores=SC_SUBCORES)

    @functools.partial(pl.kernel, out_type=jax.ShapeDtypeStruct((n_rows,), i32), mesh=mesh,
                       scratch_types=[pltpu.VMEM((T,), i32), pltpu.VMEM((n_rows,), i32)],
                       compiler_params=pltpu.CompilerParams(needs_layout_passes=False),
                       name="sc_invert")
    def k(dest_hbm, default_hbm, out_hbm, dest_v, table_v):
        wid = lax.axis_index("subcore") * SC_CORES + lax.axis_index("core")

        @pl.when(wid == 0)
        def _():
            pltpu.sync_copy(dest_hbm, dest_v)
            pltpu.sync_copy(default_hbm, table_v)
            lane = lax.iota(i32, lanes)

            @pl.loop(0, T // lanes, unroll=8)
            def _(j):
                off = pl.multiple_of(j * lanes, lanes)
                plsc.store_scatter(table_v, [dest_v[pl.ds(off, lanes)]], lane + off)

            pltpu.sync_copy(table_v, out_hbm)

    return k(dest, default)


def _sc_gather_rows(idx, src, chunk=SC_GATHER_ROWS):
    n = idx.shape[0]
    w = src.shape[1]
    workers = SC_CORES * SC_SUBCORES
    per_worker = n // workers
    mesh = plsc.VectorSubcoreMesh(core_axis_name="core", subcore_axis_name="subcore",
                                  num_cores=SC_CORES, num_subcores=SC_SUBCORES)

    n_chunks = per_worker // chunk
    assert n_chunks % 2 == 0

    @functools.partial(pl.kernel, out_type=jax.ShapeDtypeStruct((n, w), src.dtype), mesh=mesh,
                       scratch_types=[pltpu.VMEM((per_worker,), i32), pltpu.VMEM((2, chunk, w), src.dtype),
                                      pltpu.SemaphoreType.DMA((2,)), pltpu.SemaphoreType.DMA((2,))],
                       name="sc_gather_rows")
    def k(src_hbm, idx_hbm, out_hbm, idx_v, rows_v, gsem, wsem):
        wid = lax.axis_index("subcore") * SC_CORES + lax.axis_index("core")
        base = wid * per_worker
        pltpu.sync_copy(idx_hbm.at[pl.ds(base, per_worker)], idx_v)

        def gather(c, slot):
            rows = idx_v.at[pl.ds(pl.multiple_of(c * chunk, chunk), chunk)]
            return pltpu.make_async_copy(src_hbm.at[rows], rows_v.at[slot], gsem.at[slot])

        def write(c, slot):
            out = out_hbm.at[pl.ds(pl.multiple_of(base + c * chunk, chunk), chunk)]
            return pltpu.make_async_copy(rows_v.at[slot], out, wsem.at[slot])

        gather(0, 0).start()

        @pl.loop(0, n_chunks // 2)
        def _(pair):
            c = 2 * pair

            @pl.when(pair > 0)
            def _():
                write(c - 1, 1).wait()

            gather(c + 1, 1).start()
            gather(c, 0).wait()
            write(c, 0).start()

            @pl.when(c + 2 < n_chunks)
            def _():
                write(c, 0).wait()
                gather(c + 2, 0).start()

            gather(c + 1, 1).wait()
            write(c + 1, 1).start()

        write(n_chunks - 2, 0).wait()
        write(n_chunks - 1, 1).wait()

    return k(src, idx)


def _expert_kernel(grp_ref, ea_ref, eb_ref, used_ref, xs_ref, wg_ref, wu_ref, wd_ref, g_ref, b_ref, y_ref,
                   z_ref, *, alpha):
    del grp_ref
    n = pl.program_id(0)
    blk = jnp.minimum(n, pl.num_programs(0) - 2)
    used = used_ref[0]

    @pl.when(n == 0)
    def _():
        z_ref[...] = jnp.zeros_like(z_ref)

    @pl.when(n <= used)
    def _():
        y_ref[...] = _layer_norm(z_ref[...], g_ref[...], b_ref[...])
        h2 = xs_ref[:, :D]
        x = h2.astype(bf16)

        def expert(e):
            a = _dot(x, wg_ref[0, e])
            u = _dot(x, wu_ref[0, e])
            act = a * (1.0 / (1.0 + jnp.exp(-a))) * u
            return _dot(act.astype(bf16), wd_ref[0, e])

        ga = xs_ref[:, D:D + 1]
        gb = xs_ref[:, D + 1:D + 2]
        z_ref[...] = alpha * h2 + ga * expert(ea_ref[blk]) + gb * expert(eb_ref[blk])

    @pl.when(n > used)
    def _():
        y_ref[...] = jnp.zeros_like(y_ref)


def _expert_kernel_into(grp_ref, ea_ref, eb_ref, used_ref, xs_ref, wg_ref, wu_ref, wd_ref, g_ref, b_ref,
                        ys_ref, y_ref, z_ref, *, alpha):
    del ys_ref
    _expert_kernel(grp_ref, ea_ref, eb_ref, used_ref, xs_ref, wg_ref, wu_ref, wd_ref, g_ref, b_ref, y_ref,
                   z_ref, alpha=alpha)


def _experts(grp, ea, eb, used, xs, wg, wu, wd, ln_g, ln_b, alpha, ys, first_block, total_rows):
    nblk = xs.shape[0] // ROW_BLOCK

    def xmap(n, grp, ea, eb, used):
        return (jnp.maximum(jnp.minimum(n, used[0] - 1), 0), 0)

    gmap = lambda n, grp, ea, eb, used: (grp[jnp.minimum(n, nblk - 1)], 0, 0, 0)
    gspec = lambda w: pl.BlockSpec((1,) + w.shape[1:], gmap, pipeline_mode=pl.Buffered(1))
    vec = pl.BlockSpec((1, D), lambda n, grp, ea, eb, used: (0, 0))
    in_specs = [pl.BlockSpec((ROW_BLOCK, XW), xmap), gspec(wg), gspec(wu), gspec(wd), vec, vec]
    operands = [grp, ea, eb, used, xs, wg, wu, wd, ln_g, ln_b]
    aliases = {}
    body = _expert_kernel
    if ys is not None:
        in_specs.append(pl.BlockSpec(memory_space=pl.ANY))
        aliases = {len(operands): 0}
        operands.append(ys)
        body = _expert_kernel_into
    grid_spec = pltpu.PrefetchScalarGridSpec(
        num_scalar_prefetch=4, grid=(nblk + 1,), in_specs=in_specs,
        out_specs=pl.BlockSpec((ROW_BLOCK, D),
                               lambda n, grp, ea, eb, used: (jnp.maximum(n - 1, 0) + first_block, 0)),
        scratch_shapes=[pltpu.VMEM((ROW_BLOCK, D), f32)],
    )
    return pl.pallas_call(
        functools.partial(body, alpha=alpha), grid_spec=grid_spec,
        out_shape=jax.ShapeDtypeStruct((total_rows, D), f32),
        input_output_aliases=aliases,
        compiler_params=_cparams(("arbitrary",)), name="experts",
    )(*operands)


def _pair_tables():
    ea = np.zeros((LANES,), np.int32)
    eb = np.zeros((LANES,), np.int32)
    for g in range(N_GROUPS):
        k = 0
        for a in range(EPG):
            for b in range(a + 1, EPG):
                ea[g * N_PAIRS + k] = a
                eb[g * N_PAIRS + k] = b
                k += 1
    return ea, eb


_PAIR_A, _PAIR_B = _pair_tables()


def _layer(h, mem, positions, w_in, b_forget, sinks, w_mix_out, ln_mix_g, ln_mix_b,
           w_xq, w_xkv, w_xout, ln_x_g, ln_x_b, w_rg, b_rg, w_re, b_re,
           w_eg, w_eu, w_ed, ln_f_g, ln_f_b, alpha):
    B, S, _ = h.shape
    T = B * S
    x2 = h.reshape(T, D)

    o = np.cumsum((0, FOX_W, FOX_W, FOX_W, N_FOX, SWA_Q_W, SWA_KV_W, SWA_KV_W))
    w_qf, w_kf, w_vf, w_fl, w_qs, w_ks, w_vs = (w_in[:, o[i]:o[i + 1]] for i in range(7))
    def regroup(a, axis):
        shp = a.shape
        a = jnp.moveaxis(a, axis, 0).reshape(N_SWA_KV, N_SWA // N_SWA_KV, HD, -1)
        return jnp.moveaxis(jnp.swapaxes(a, 0, 1).reshape(N_SWA * HD, -1), 0, axis).reshape(shp)

    w_all = jnp.concatenate([w_kf, regroup(w_qs, 1), w_ks, w_vs], axis=1).astype(bf16)
    wqt = w_qf.T.astype(bf16)
    wvt = w_vf.T.astype(bf16)
    wfl = w_fl.T.astype(bf16)
    bfc = b_forget.reshape(N_FOX, 1).astype(f32)
    half = HD // 2
    inv_freq = ROPE_THETA ** (-jnp.arange(half, dtype=f32) / half)
    per_row = LANES // half
    invf = jnp.tile(inv_freq, per_row).reshape(1, LANES)
    pos4 = jnp.repeat(positions.reshape(per_row, T // per_row).T.astype(i32), half, axis=1)
    cos, sin = _rope_table(pos4, invf)
    w_out = jnp.concatenate([w_mix_out[:FOX_W], regroup(w_mix_out[FOX_W:], 0)], axis=0).astype(bf16)

    tq = 512
    (qt, kf, vt, qs, ks, vs, lf, qn, kn), (eg16, eu16, ed16) = _in_proj(
        x2, cos, sin, w_all, wqt, wvt, wfl, bfc, w_eg, w_eu, w_ed, tq)
    c, ca = _cumsum(lf, S)
    c4 = c.reshape(N_FOX // 2, 2, T // tq, tq)
    r3 = lambda a: a.reshape(B, S, a.shape[-1])
    o_fox = _fox(_fox_first_blocks(qn, kn, c, B, S, tq), qt, r3(kf), r3(ca), vt, c4, B, S, tq)
    o_swa = _swa(sinks.astype(f32) * LOG2E, r3(qs), r3(ks), r3(vs), B, S)

    kx, vx = _kvproj(mem.reshape(-1, D), w_xkv.astype(bf16))

    gpad = EPG - N_GROUPS
    wr = jnp.concatenate([jnp.pad(w_rg, ((0, 0), (0, gpad))),
                          jnp.transpose(w_re, (1, 0, 2)).reshape(D, N_EXPERTS)], axis=1)
    wr = jnp.pad(wr, ((0, 0), (0, LANES - wr.shape[1]))).astype(f32)
    wrh = wr.astype(bf16)
    wr2 = jnp.concatenate([wrh, (wr - wrh.astype(f32)).astype(bf16)], axis=1)
    br = jnp.pad(jnp.concatenate([jnp.pad(b_rg, (0, gpad)), b_re.reshape(-1)]), (0, LANES - EPG - N_EXPERTS))
    br = br.reshape(1, LANES).astype(f32)
    v2 = lambda a: a.reshape(1, D).astype(f32)
    h2x, bk8 = _mid(x2, o_fox.reshape(T, FOX_W), o_swa.reshape(T, SWA_Q_W), w_out,
                    v2(ln_mix_g), v2(ln_mix_b), w_xq.astype(bf16), kx, vx, w_xout.astype(bf16),
                    v2(ln_x_g), v2(ln_x_b), wr2, br, alpha, S)

    rank, cnt = _rank(bk8)
    counts = cnt[:, 0].astype(i32)
    padded = ((counts + ROW_BLOCK - 1) // ROW_BLOCK) * ROW_BLOCK
    pad_end = jnp.cumsum(padded)
    pad_start = (pad_end - padded).astype(i32)
    unit = int(np.lcm(SC_CORES * SC_SUBCORES * 2 * SC_GATHER_ROWS, ROW_BLOCK))
    scale = -(-(T + N_BUCKETS * ROW_BLOCK) // (unit * sum(MOE_CHUNK_SHARES)))
    chunk_blocks = [share * scale * unit // ROW_BLOCK for share in MOE_CHUNK_SHARES]
    nblk = sum(chunk_blocks)
    P = nblk * ROW_BLOCK
    used = (pad_end[-1] // ROW_BLOCK).astype(i32).reshape(1)
    blk_row = jnp.arange(nblk, dtype=i32)[:, None] * ROW_BLOCK
    blk_bucket = jnp.minimum(jnp.sum((pad_end[None, :] <= blk_row).astype(i32), axis=1), N_BUCKETS - 1)
    pick = (blk_bucket[:, None] == jnp.arange(LANES, dtype=i32)[None, :]).astype(i32)
    blk_a = jnp.sum(pick * jnp.asarray(_PAIR_A)[None, :], axis=1)
    blk_b = jnp.sum(pick * jnp.asarray(_PAIR_B)[None, :], axis=1)
    blk_g = blk_bucket // N_PAIRS
    by_group = lambda w: w.reshape((N_GROUPS, EPG) + w.shape[1:])

    dest = _dest(bk8, rank, pad_start.astype(f32).reshape(LANES, 1))[0]
    row_tok = _sc_invert(dest, jnp.arange(P, dtype=i32) % T)
    ys = None
    lo = 0
    for cblk in chunk_blocks:
        xs = _sc_gather_rows(row_tok[lo * ROW_BLOCK:(lo + cblk) * ROW_BLOCK], h2x)
        used_c = jnp.clip(used - lo, 0, cblk)
        ys = _experts(blk_g[lo:lo + cblk], blk_a[lo:lo + cblk], blk_b[lo:lo + cblk], used_c, xs,
                      by_group(eg16), by_group(eu16), by_group(ed16), v2(ln_f_g), v2(ln_f_b), alpha,
                      ys, lo, P)
        lo += cblk
    return _sc_gather_rows(dest, ys).reshape(B, S, D)


def kernel(x, mem, positions, w_in, b_forget, sinks, w_mix_out, ln_mix_g, ln_mix_b, w_xq, w_xkv, w_xout,
           ln_x_g, ln_x_b, w_route_group, b_route_group, w_route_expert, b_route_expert,
           w_exp_gate, w_exp_up, w_exp_down, ln_ffn_g, ln_ffn_b):
    depth = w_in.shape[0]
    alpha = (2.0 * depth) ** 0.25
    h = x
    for l in range(depth):
        h = _layer(h, mem, positions, w_in[l], b_forget[l], sinks[l], w_mix_out[l], ln_mix_g[l], ln_mix_b[l],
                   w_xq[l], w_xkv[l], w_xout[l], ln_x_g[l], ln_x_b[l], w_route_group[l], b_route_group[l],
                   w_route_expert[l], b_route_expert[l], w_exp_gate[l], w_exp_up[l], w_exp_down[l],
                   ln_ffn_g[l], ln_ffn_b[l], alpha)
    return h
```

```python
import functools

import jax
import jax.numpy as jnp
import numpy as np
from jax import lax
from jax.experimental import pallas as pl
from jax.experimental.pallas import tpu as pltpu
from jax.experimental.pallas import tpu_sc as plsc

f32 = jnp.float32
bf16 = jnp.bfloat16
i32 = jnp.int32

D = 1024
HD = 64
N_FOX = 8
N_SWA = 8
N_SWA_KV = 2
FOX_W = 512
SWA_Q_W = 512
SWA_KV_W = 128
WINDOW = 128
ROPE_THETA = 10000.0
N_XH = 4
XHD = 256
N_GROUPS = 4
EPG = 8
N_EXPERTS = 32
LN_EPS = 1e-5
NEG = -1e30
LOG2E = 1.4426950408889634
L_ROW = (HD, 0)
SKIP_LOG2 = 160.0

SC_CORES = 2
SC_SUBCORES = 16
SC_LANES = 16
SC_GATHER_ROWS = 32
MOE_CHUNK_SHARES = (2, 9, 13)
LANES = 128
ROW_BLOCK = 128
N_PAIRS = EPG * (EPG - 1) // 2
N_BUCKETS = N_GROUPS * N_PAIRS
XW = D + LANES
VMEM_LIMIT = 56 * 1024 * 1024


def _cparams(sem):
    return pltpu.CompilerParams(dimension_semantics=sem, vmem_limit_bytes=VMEM_LIMIT)


def _layer_norm(v, g, b):
    mu = jnp.mean(v, axis=-1, keepdims=True)
    c = v - mu
    var = jnp.mean(c * c, axis=-1, keepdims=True)
    return c * lax.rsqrt(var + LN_EPS) * g + b


def _dot(a, b):
    return jnp.dot(a, b, preferred_element_type=f32)


def _dot_nt(a, b):
    return lax.dot_general(a, b, (((1,), (1,)), ((), ())), preferred_element_type=f32)


def _rope_table_kernel(pos_ref, invf_ref, cos_ref, sin_ref):
    ang = pos_ref[...].astype(f32) * invf_ref[...]
    cos_ref[...] = jnp.cos(ang)
    sin_ref[...] = jnp.sin(ang)


def _rope_table(pos4, invf, rows=1024):
    R = pos4.shape[0]
    blk = pl.BlockSpec((rows, LANES), lambda i: (i, 0))
    return pl.pallas_call(
        _rope_table_kernel, grid=(R // rows,),
        in_specs=[blk, pl.BlockSpec((1, LANES), lambda i: (0, 0))], out_specs=[blk, blk],
        out_shape=[jax.ShapeDtypeStruct((R, LANES), f32)] * 2,
        compiler_params=_cparams(("parallel",)), name="rope_table",
    )(pos4, invf)


def _inproj_kernel(x_ref, cos_ref, sin_ref, w_ref, wqt_ref, wvt_ref, wfl_ref, bf_ref, ind_ref,
                   eg_ref, eu_ref, ed_ref,
                   qt_ref, kf_ref, vt_ref, qs_ref, ks_ref, vs_ref, lf_ref, qn_ref, kn_ref,
                   egb_ref, eub_ref, edb_ref):
    tm = x_ref.shape[0]
    xb = x_ref[...].astype(bf16)
    egb_ref[...] = eg_ref[...].astype(bf16)
    eub_ref[...] = eu_ref[...].astype(bf16)
    edb_ref[...] = ed_ref[...].astype(bf16)

    def proj(lo, hi):
        return _dot(xb, w_ref[:, lo:hi])

    qv = _dot_nt(wqt_ref[...], xb) * (0.125 * LOG2E)
    qt_ref[0] = qv.astype(bf16)
    vt_ref[0] = _dot_nt(wvt_ref[...], xb).astype(bf16)
    kv = proj(0, 512)
    kf_ref[...] = kv.astype(bf16)
    q2 = jnp.sum((qv * qv).reshape(N_FOX, HD, tm), axis=1)
    qn_ref[0] = jnp.broadcast_to(jnp.max(q2, axis=1, keepdims=True), (N_FOX, LANES))
    k2 = _dot((kv * kv).astype(bf16), ind_ref[...])
    kn_ref[0] = jnp.broadcast_to(jnp.max(k2, axis=0, keepdims=True), (N_FOX, LANES))

    half = HD // 2
    reps = LANES // half
    quarter = pl.program_id(0) // (pl.num_programs(0) // reps)

    def spread(tab):
        pick = tab[:, :half]
        for k in range(1, reps):
            pick = jnp.where(quarter == k, tab[:, k * half:(k + 1) * half], pick)
        return jnp.concatenate([pick] * reps, axis=1)

    cos = spread(cos_ref[...])
    sin = spread(sin_ref[...])
    lane = lax.broadcasted_iota(i32, (tm, LANES), 1)
    lo_half = (lane % HD) < (HD // 2)
    sin_s = jnp.where(lo_half, -sin, sin)

    def rope(z):
        rot = jnp.where(lo_half, pltpu.roll(z, LANES - HD // 2, 1), pltpu.roll(z, HD // 2, 1))
        return z * cos + rot * sin_s

    zq = proj(512, 1024)
    for g in range(4):
        sl = slice(g * LANES, (g + 1) * LANES)
        qs_ref[:, sl] = (rope(zq[:, sl]) * (0.125 * LOG2E)).astype(bf16)
    zkv = proj(1024, 1280)
    ks_ref[...] = rope(zkv[:, :SWA_KV_W]).astype(bf16)
    vs_ref[...] = zkv[:, SWA_KV_W:].astype(bf16)

    z = _dot_nt(wfl_ref[...], xb) + bf_ref[...]
    lf_ref[...] = jnp.minimum(z, 0.0) - jnp.log(1.0 + jnp.exp(-jnp.abs(z)))


def _in_proj(x2, cos, sin, w_all, wqt, wvt, wfl, bfc, w_eg, w_eu, w_ed, tm):
    T = x2.shape[0]
    steps = T // tm
    row = lambda w: pl.BlockSpec((tm, w), lambda i: (i, 0))
    full = lambda a: pl.BlockSpec(a.shape, lambda i: (0,) * a.ndim)
    fmaj = pl.BlockSpec((1, FOX_W, tm), lambda i: (i, 0, 0))
    flat = [w.reshape(-1, w.shape[-1]) for w in (w_eg, w_eu, w_ed)]
    slices = [pl.BlockSpec((w.shape[0] // steps, w.shape[1]), lambda i: (i, 0)) for w in flat]
    ind = jnp.asarray(np.arange(FOX_W)[:, None] // HD == np.arange(LANES)[None, :], bf16)
    nrm = pl.BlockSpec((1, N_FOX, LANES), lambda i: (i, 0, 0))
    tab_steps = cos.shape[0] // tm
    tab = pl.BlockSpec((tm, LANES), lambda i: (i % tab_steps, 0))
    outs = pl.pallas_call(
        _inproj_kernel,
        grid=(steps,),
        in_specs=[row(D), tab, tab, full(w_all), full(wqt), full(wvt), full(wfl), full(bfc), full(ind)] + slices,
        out_specs=[fmaj, row(512), fmaj, row(512), row(128), row(128),
                   pl.BlockSpec((N_FOX, tm), lambda i: (0, i)), nrm, nrm] + slices,
        out_shape=[jax.ShapeDtypeStruct((steps, FOX_W, tm), bf16), jax.ShapeDtypeStruct((T, 512), bf16),
                   jax.ShapeDtypeStruct((steps, FOX_W, tm), bf16), jax.ShapeDtypeStruct((T, 512), bf16),
                   jax.ShapeDtypeStruct((T, 128), bf16), jax.ShapeDtypeStruct((T, 128), bf16),
                   jax.ShapeDtypeStruct((N_FOX, T), f32),
                   jax.ShapeDtypeStruct((steps, N_FOX, LANES), f32), jax.ShapeDtypeStruct((steps, N_FOX, LANES), f32)]
        + [jax.ShapeDtypeStruct(w.shape, bf16) for w in flat],
        compiler_params=_cparams(("parallel",)),
        name="in_proj",
    )(x2, cos, sin, w_all, wqt, wvt, wfl, bfc, ind, *flat)
    experts_bf16 = [o.reshape(w.shape) for o, w in zip(outs[9:], (w_eg, w_eu, w_ed))]
    return outs[:9], experts_bf16


def _cumsum_kernel(lf_ref, c_ref, ca_ref):
    S = lf_ref.shape[1]
    ch = 256
    r = lax.broadcasted_iota(i32, (ch, ch), 0)
    c = lax.broadcasted_iota(i32, (ch, ch), 1)
    tri = (r <= c).astype(f32)
    eye = (r == c).astype(bf16)
    stacked = jnp.concatenate([lf_ref[:, j * ch:(j + 1) * ch] for j in range(S // ch)], axis=0)
    local = jnp.dot(stacked, tri, precision=lax.Precision.HIGHEST, preferred_element_type=f32)
    carry = jnp.zeros((N_FOX, 1), f32)
    for j in range(S // ch):
        cc = local[j * N_FOX:(j + 1) * N_FOX] + carry
        carry = cc[:, ch - 1:ch]
        c2 = cc * LOG2E
        c_ref[:, j * ch:(j + 1) * ch] = c2
        neg = -c2
        hi = neg.astype(bf16)
        r1 = neg - hi.astype(f32)
        mid = r1.astype(bf16)
        lo = (r1 - mid.astype(f32)).astype(bf16)
        terms = jnp.concatenate([hi, mid, lo, jnp.zeros((LANES - 3 * N_FOX, ch), bf16)], axis=0)
        ca_ref[j * ch:(j + 1) * ch, :] = _dot_nt(eye, terms).astype(bf16)


def _cumsum(lf, S):
    T = lf.shape[1]
    spec = pl.BlockSpec((N_FOX, S), lambda b: (0, b))
    return pl.pallas_call(
        _cumsum_kernel, grid=(T // S,), in_specs=[spec],
        out_specs=[spec, pl.BlockSpec((S, LANES), lambda b: (b, 0))],
        out_shape=[jax.ShapeDtypeStruct((N_FOX, T), f32), jax.ShapeDtypeStruct((T, LANES), bf16)],
        compiler_params=_cparams(("parallel",)), name="cumsum",
    )(lf)


def _fox_kernel(j0_ref, qt_ref, k_ref, ca_ref, vt_ref, c_ref, o_ref, t0_ref, t1_ref, *, tq):
    hp = pl.program_id(1)
    i = pl.program_id(2)
    qt = qt_ref[0]
    row = lax.broadcasted_iota(i32, (LANES, tq), 0)
    is_a = row < HD
    zero = jnp.zeros_like(qt)
    q_ops = []
    for h in range(2):
        ones = jnp.where(((row & 7) == 2 * hp + h) & (row < 3 * N_FOX), 1.0, 0.0).astype(bf16)
        qh = jnp.where(is_a, qt, zero) if h == 0 else jnp.where(is_a, zero, qt)
        q_ops.append(jnp.concatenate([qh, ones], axis=0))
    kr = lax.broadcasted_iota(i32, (tq, tq), 0)
    qc = lax.broadcasted_iota(i32, (tq, tq), 1)
    causal = kr <= qc
    cq = [c_ref[0, h, pl.ds(i, 1), :] for h in range(2)]

    def scores(j, t_ref):
        off = pl.multiple_of(j * tq, tq)
        kblk = jnp.concatenate([k_ref[0, pl.ds(off, tq), :], ca_ref[0, pl.ds(off, tq), :]], axis=1)
        for h in range(2):
            t_ref[h] = _dot(kblk, q_ops[h])

    keep = [jnp.where(is_a, 1.0, 0.0).astype(bf16), jnp.where(is_a, 0.0, 1.0).astype(bf16)]
    ones_row = [jnp.where(row == L_ROW[h], 1.0, 0.0).astype(bf16) for h in range(2)]

    def softmax_pv(j, t_ref, carry, masked):
        vt = vt_ref[j]
        vts = [vt * keep[h] + ones_row[h] for h in range(2)]
        new = []
        for h in range(2):
            m, acc = carry[h]
            t = t_ref[h]
            if masked:
                t = jnp.where(causal, t, NEG)
            m_new = jnp.maximum(m, jnp.max(t, axis=0, keepdims=True) + cq[h])
            alpha = jnp.exp2(m - m_new)
            p = jnp.exp2(t + (cq[h] - m_new))
            acc = alpha * acc + _dot(vts[h], p.astype(bf16))
            new.append((m_new, acc))
        return tuple(new)

    j0 = j0_ref[(pl.program_id(0) * pl.num_programs(1) + hp) * pl.num_programs(2) + i]
    n_full = i - j0

    def pair(k, carry):
        j = j0 + 2 * k
        scores(j + 1, t1_ref)
        carry = softmax_pv(j, t0_ref, carry, False)
        scores(j + 2, t0_ref)
        return softmax_pv(j + 1, t1_ref, carry, False)

    def odd_tail(carry):
        scores(i, t1_ref)
        carry = softmax_pv(i - 1, t0_ref, carry, False)
        return softmax_pv(i, t1_ref, carry, True)

    def even_tail(carry):
        return softmax_pv(i, t0_ref, carry, True)

    init = tuple((jnp.full((1, tq), NEG, f32), jnp.zeros((LANES, tq), f32)) for _ in range(2))
    scores(j0, t0_ref)
    carry = lax.fori_loop(0, n_full // 2, pair, init)
    (_, acca), (_, accb) = lax.cond(n_full % 2 == 1, odd_tail, even_tail, carry)
    la = acca[L_ROW[0]:L_ROW[0] + 1, :]
    lb = accb[L_ROW[1]:L_ROW[1] + 1, :]
    ot = jnp.where(is_a, acca / la, accb / lb)
    o_ref[0] = jnp.transpose(ot).astype(bf16)


def _fox_first_blocks(qn, kn, c, B, S, tq):
    nq = S // tq
    qmax = jnp.sqrt(qn[:, :, 0].reshape(B, nq, N_FOX))
    kmax = jnp.sqrt(kn[:, 0, :N_FOX].reshape(B, nq, N_FOX))
    qk = 1.02 * qmax[:, :, None, :] * (kmax[:, None, :, :] + kmax[:, :, None, :])
    cb = c.reshape(N_FOX, B, nq, tq)
    c_first = jnp.transpose(cb[..., 0], (1, 2, 0))
    c_last = jnp.transpose(cb[..., tq - 1], (1, 2, 0))
    gap = c_last[:, None, :, :] - c_first[:, :, None, :]
    blk = jnp.arange(nq)
    negligible = (qk - gap < -SKIP_LOG2) & (blk[None, :] < blk[:, None])[None, :, :, None]
    pair = jnp.all(negligible.reshape(B, nq, nq, N_FOX // 2, 2), axis=-1)
    j0 = jnp.sum(jnp.cumprod(pair.astype(i32), axis=2), axis=2)
    return jnp.transpose(j0, (0, 2, 1)).reshape(-1).astype(i32)


def _fox(j0, qt, kf, ca, vt, c4, B, S, tq):
    nq = S // tq
    kernel = functools.partial(_fox_kernel, tq=tq)
    return pl.pallas_call(
        kernel,
        grid=(B, N_FOX // 2, nq),
        in_specs=[
            pl.BlockSpec(memory_space=pltpu.SMEM),
            pl.BlockSpec((1, LANES, tq), lambda b, hp, i: (b * nq + i, hp, 0)),
            pl.BlockSpec((1, S, LANES), lambda b, hp, i: (b, 0, hp)),
            pl.BlockSpec((1, S, LANES), lambda b, hp, i: (b, 0, 0)),
            pl.BlockSpec((nq, LANES, tq), lambda b, hp, i: (b, hp, 0)),
            pl.BlockSpec((1, 2, nq, tq), lambda b, hp, i: (hp, 0, b, 0)),
        ],
        out_specs=pl.BlockSpec((1, tq, LANES), lambda b, hp, i: (b, i, hp)),
        out_shape=jax.ShapeDtypeStruct((B, S, FOX_W), bf16),
        scratch_shapes=[pltpu.VMEM((2, tq, tq), f32), pltpu.VMEM((2, tq, tq), f32)],
        compiler_params=_cparams(("parallel", "parallel", "arbitrary")),
        name="fox",
    )(j0, qt, kf, ca, vt, c4)


def _swa_kernel(sink_ref, q_ref, k_ref, v_ref, o_ref, *, tq):
    W = WINDOW
    nsub = tq // W
    n0 = pl.program_id(1) * nsub
    lane = lax.broadcasted_iota(i32, (W, LANES), 1)
    is0 = lane < HD
    qoff = lax.broadcasted_iota(i32, (4 * W, 2 * W), 0) % W
    cols = lax.broadcasted_iota(i32, (4 * W, 2 * W), 1)
    bias_mid = jnp.where((cols - W <= qoff) & (qoff - (cols - W) < W), 0.0, NEG)
    bias_first = jnp.where(cols <= qoff, 0.0, NEG)
    rgrp = lax.broadcasted_iota(i32, (4 * W, 1), 0) // W
    for r in range(nsub):
        nb = n0 + r
        kstart = pl.multiple_of(jnp.maximum(nb * W - W, 0), W)
        ks = k_ref[0, pl.ds(kstart, 2 * W), :]
        vs = v_ref[0, pl.ds(kstart, 2 * W), :]
        bias = jnp.where(nb == 0, bias_first, bias_mid)
        outs = []
        for kv in range(2):
            keep = is0 if kv == 0 else jnp.logical_not(is0)
            parts = []
            for g in range(4):
                qg = q_ref[0, r * W:(r + 1) * W, g * LANES:(g + 1) * LANES]
                parts.append(jnp.where(keep, qg, jnp.zeros_like(qg)))
            qstack = jnp.concatenate(parts, axis=0)
            s = bias + _dot_nt(qstack, ks)
            sink = jnp.zeros((4 * W, 1), f32)
            for g in range(4):
                sink = jnp.where(rgrp == g, sink_ref[kv * 4 + g], sink)
            m = jnp.maximum(jnp.max(s, axis=1, keepdims=True), sink)
            e = jnp.exp2(s - m)
            den = jnp.sum(e, axis=1, keepdims=True) + jnp.exp2(sink - m)
            outs.append(_dot(e.astype(bf16), vs) / den)
        for g in range(4):
            og = jnp.where(is0, outs[0][g * W:(g + 1) * W], outs[1][g * W:(g + 1) * W])
            o_ref[0, r * W:(r + 1) * W, g * LANES:(g + 1) * LANES] = og.astype(bf16)


def _swa(sinks, qs, ks, vs, B, S, tq=512):
    kernel = functools.partial(_swa_kernel, tq=tq)
    return pl.pallas_call(
        kernel,
        grid=(B, S // tq),
        in_specs=[
            pl.BlockSpec(memory_space=pltpu.SMEM),
            pl.BlockSpec((1, tq, SWA_Q_W), lambda b, i: (b, i, 0)),
            pl.BlockSpec((1, S, SWA_KV_W), lambda b, i: (b, 0, 0)),
            pl.BlockSpec((1, S, SWA_KV_W), lambda b, i: (b, 0, 0)),
        ],
        out_specs=pl.BlockSpec((1, tq, SWA_Q_W), lambda b, i: (b, i, 0)),
        out_shape=jax.ShapeDtypeStruct((B, S, SWA_Q_W), bf16),
        compiler_params=_cparams(("parallel", "arbitrary")),
        name="swa",
    )(sinks, qs, ks, vs)


def _kvproj_kernel(m_ref, w_ref, k_ref, v_ref):
    mb = m_ref[...].astype(bf16)
    k_ref[...] = _dot(mb, w_ref[:, :D]).astype(bf16)
    v_ref[...] = _dot(mb, w_ref[:, D:]).astype(bf16)


def _kvproj(mem2, w_xkv, tm=512):
    R = mem2.shape[0]
    row = pl.BlockSpec((tm, D), lambda i: (i, 0))
    return pl.pallas_call(
        _kvproj_kernel, grid=(R // tm,),
        in_specs=[row, pl.BlockSpec(w_xkv.shape, lambda i: (0, 0))],
        out_specs=[row, row],
        out_shape=[jax.ShapeDtypeStruct((R, D), bf16)] * 2,
        compiler_params=_cparams(("parallel",)), name="kvproj",
    )(mem2, w_xkv)


def _mid_kernel(x_ref, of_ref, os_ref, wo_ref, g1_ref, b1_ref, wq_ref, k_ref, v_ref,
                wxo_ref, g2_ref, b2_ref, wr_ref, br_ref,
                h2_ref, bk_ref, oc_ref, *, alpha):
    tm = x_ref.shape[0]
    mix = _dot(of_ref[...], wo_ref[:FOX_W, :]) + _dot(os_ref[...], wo_ref[FOX_W:, :])
    h1 = _layer_norm(alpha * x_ref[...] + mix, g1_ref[...], b1_ref[...])

    q = (_dot(h1.astype(bf16), wq_ref[...]) * 0.0625).astype(bf16)
    for h in range(N_XH):
        sl = slice(h * XHD, (h + 1) * XHD)
        s = _dot_nt(q[:, sl], k_ref[:, sl])
        e = jnp.exp(s - jnp.max(s, axis=1, keepdims=True))
        p = e / jnp.sum(e, axis=1, keepdims=True)
        oc_ref[:, sl] = _dot(p.astype(bf16), v_ref[:, sl]).astype(bf16)
    xo = _dot(oc_ref[...], wxo_ref[...])
    h2 = _layer_norm(alpha * h1 + xo, g2_ref[...], b2_ref[...])
    h2_ref[:, :D] = h2

    hh = h2.astype(bf16)
    hl = (h2 - hh.astype(f32)).astype(bf16)
    hi_terms = _dot(hh, wr_ref[...])
    lg = hi_terms[:, :LANES] + _dot(hl, wr_ref[:, :LANES]) + hi_terms[:, LANES:] + br_ref[...]

    lgt = jnp.transpose(lg)
    row = lax.broadcasted_iota(i32, (EPG, tm), 0).astype(f32)
    big = float(EPG)

    def first_max(vals, mask):
        vm = jnp.where(mask, vals, NEG)
        top = jnp.max(vm, axis=0, keepdims=True)
        idx = jnp.min(jnp.where(mask & (vm == top), row, big), axis=0, keepdims=True)
        return top, idx

    gl = lgt[0:EPG]
    gmask = row < float(N_GROUPS)
    gmax, gidx = first_max(gl, gmask)
    g_val = 1.0 / jnp.sum(jnp.where(gmask, jnp.exp(gl - gmax), 0.0), axis=0, keepdims=True)
    sel = jnp.zeros((EPG, tm), f32)
    for g in range(N_GROUPS):
        sel = jnp.where(gidx == float(g), lgt[EPG * (g + 1):EPG * (g + 2)], sel)
    every = row >= 0.0
    v1, e1 = first_max(sel, every)
    v2, e2 = first_max(sel, row != e1)
    ex = jnp.exp(v2 - v1)
    w1 = g_val * (1.0 / (1.0 + ex))
    w2 = g_val * (ex / (1.0 + ex))
    first_low = e1 < e2
    ea = jnp.where(first_low, e1, e2)
    eb = jnp.where(first_low, e2, e1)
    ga = jnp.where(first_low, w1, w2)
    gb = jnp.where(first_low, w2, w1)
    pidx = ea * float(EPG - 1) - ea * (ea - 1.0) * 0.5 + (eb - ea - 1.0)
    bucket = gidx * float(N_PAIRS) + pidx

    bk_ref[...] = jnp.broadcast_to(bucket, (EPG, tm))
    gates = jnp.where(row == 0.0, ga, jnp.where(row == 1.0, gb, 0.0))
    gates = jnp.concatenate([gates, jnp.zeros((LANES - EPG, tm), f32)], axis=0)
    h2_ref[:, D:] = jnp.transpose(gates)


def _mid(x2, of2, os2, w_out, g1, b1, wq, kx, vx, wxo, g2, b2, wr2, br, alpha, S, tm=1024):
    T = x2.shape[0]
    M = kx.shape[0] // (T // S)
    per_b = S // tm
    row = lambda w: pl.BlockSpec((tm, w), lambda i: (i, 0))
    full = lambda a: pl.BlockSpec(a.shape, lambda i: (0,) * a.ndim)
    kvspec = pl.BlockSpec((M, D), lambda i: (i // per_b, 0))
    kernel = functools.partial(_mid_kernel, alpha=alpha)
    return pl.pallas_call(
        kernel,
        grid=(T // tm,),
        in_specs=[row(D), row(512), row(512), full(w_out), full(g1), full(b1), full(wq),
                  kvspec, kvspec, full(wxo), full(g2), full(b2), full(wr2), full(br)],
        out_specs=[row(XW), pl.BlockSpec((8, tm), lambda i: (0, i))],
        out_shape=[jax.ShapeDtypeStruct((T, XW), f32), jax.ShapeDtypeStruct((8, T), f32)],
        scratch_shapes=[pltpu.VMEM((tm, D), bf16)],
        compiler_params=_cparams(("parallel",)),
        name="mid",
    )(x2, of2, os2, w_out, g1, b1, wq, kx, vx, wxo, g2, b2, wr2, br)


def _rank_kernel(bk_ref, rank_ref, cnt_ref, carry_ref, *, chunk):
    sub = 256

    @pl.when(pl.program_id(0) == 0)
    def _():
        carry_ref[...] = jnp.zeros_like(carry_ref)

    r = lax.broadcasted_iota(i32, (sub, sub), 0)
    c = lax.broadcasted_iota(i32, (sub, sub), 1)
    before = (r < c).astype(bf16)
    bid = lax.broadcasted_iota(i32, (LANES, sub), 0).astype(f32)
    carry = carry_ref[...]
    for j in range(chunk // sub):
        bk = bk_ref[0:1, j * sub:(j + 1) * sub]
        hit = bid == bk
        oh = jnp.where(hit, 1.0, 0.0)
        prior = _dot(oh.astype(bf16), before) + carry
        rank_ref[:, j * sub:(j + 1) * sub] = jnp.sum(jnp.where(hit, prior, 0.0), axis=0, keepdims=True)
        carry = carry + jnp.sum(oh, axis=1, keepdims=True)
    carry_ref[...] = carry
    cnt_ref[...] = carry


def _rank(bk8, chunk=8192):
    T = bk8.shape[1]
    kernel = functools.partial(_rank_kernel, chunk=chunk)
    return pl.pallas_call(
        kernel, grid=(T // chunk,),
        in_specs=[pl.BlockSpec((8, chunk), lambda i: (0, i))],
        out_specs=[pl.BlockSpec((1, chunk), lambda i: (0, i)),
                   pl.BlockSpec((LANES, 1), lambda i: (0, 0))],
        out_shape=[jax.ShapeDtypeStruct((1, T), f32), jax.ShapeDtypeStruct((LANES, 1), f32)],
        scratch_shapes=[pltpu.VMEM((LANES, 1), f32)],
        compiler_params=_cparams(("arbitrary",)), name="rank",
    )(bk8)


def _dest_kernel(bk_ref, rank_ref, ps_ref, dest_ref):
    chunk = bk_ref.shape[1]
    bid = lax.broadcasted_iota(i32, (LANES, chunk), 0).astype(f32)
    start = jnp.sum(jnp.where(bid == bk_ref[0:1, :], ps_ref[...], 0.0), axis=0, keepdims=True)
    dest_ref[...] = (start + rank_ref[...]).astype(i32)


def _dest(bk8, rank, ps_col, chunk=8192):
    T = bk8.shape[1]
    return pl.pallas_call(
        _dest_kernel, grid=(T // chunk,),
        in_specs=[pl.BlockSpec((8, chunk), lambda i: (0, i)), pl.BlockSpec((1, chunk), lambda i: (0, i)),
                  pl.BlockSpec((LANES, 1), lambda i: (0, 0))],
        out_specs=pl.BlockSpec((1, chunk), lambda i: (0, i)),
        out_shape=jax.ShapeDtypeStruct((1, T), i32),
        compiler_params=_cparams(("parallel",)), name="dest",
    )(bk8, rank, ps_col)


def _sc_invert(dest, default):
    T = dest.shape[0]
    n_rows = default.shape[0]
    lanes = SC_LANES
    mesh = plsc.VectorSubcoreMesh(core_axis_name="core", subcore_axis_name="subcore",
                                  num_cores=SC_CORES, num_subcores=SC_SUBCORES)

    @functools.partial(pl.kernel, out_type=jax.ShapeDtypeStruct((n_rows,), i32), mesh=mesh,
                       scratch_types=[pltpu.VMEM((T,), i32), pltpu.VMEM((n_rows,), i32)],
                       compiler_params=pltpu.CompilerParams(needs_layout_passes=False),
                       name="sc_invert")
    def k(dest_hbm, default_hbm, out_hbm, dest_v, table_v):
        wid = lax.axis_index("subcore") * SC_CORES + lax.axis_index("core")

        @pl.when(wid == 0)
        def _():
            pltpu.sync_copy(dest_hbm, dest_v)
            pltpu.sync_copy(default_hbm, table_v)
            lane = lax.iota(i32, lanes)

            @pl.loop(0, T // lanes, unroll=8)
            def _(j):
                off = pl.multiple_of(j * lanes, lanes)
                plsc.store_scatter(table_v, [dest_v[pl.ds(off, lanes)]], lane + off)

            pltpu.sync_copy(table_v, out_hbm)

    return k(dest, default)


def _sc_gather_rows(idx, src, chunk=SC_GATHER_ROWS):
    n = idx.shape[0]
    w = src.shape[1]
    workers = SC_CORES * SC_SUBCORES
    per_worker = n // workers
    mesh = plsc.VectorSubcoreMesh(core_axis_name="core", subcore_axis_name="subcore",
                                  num_cores=SC_CORES, num_subcores=SC_SUBCORES)

    n_chunks = per_worker // chunk
    assert n_chunks % 2 == 0

    @functools.partial(pl.kernel, out_type=jax.ShapeDtypeStruct((n, w), src.dtype), mesh=mesh,
                       scratch_types=[pltpu.VMEM((per_worker,), i32), pltpu.VMEM((2, chunk, w), src.dtype),
                                      pltpu.SemaphoreType.DMA((2,)), pltpu.SemaphoreType.DMA((2,))],
                       name="sc_gather_rows")
    def k(src_hbm, idx_hbm, out_hbm, idx_v, rows_v, gsem, wsem):
        wid = lax.axis_index("subcore") * SC_CORES + lax.axis_index("core")
        base = wid * per_worker
        pltpu.sync_copy(idx_hbm.at[pl.ds(base, per_worker)], idx_v)

        def gather(c, slot):
            rows = idx_v.at[pl.ds(pl.multiple_of(c * chunk, chunk), chunk)]
            return pltpu.make_async_copy(src_hbm.at[rows], rows_v.at[slot], gsem.at[slot])

        def write(c, slot):
            out = out_hbm.at[pl.ds(pl.multiple_of(base + c * chunk, chunk), chunk)]
            return pltpu.make_async_copy(rows_v.at[slot], out, wsem.at[slot])

        gather(0, 0).start()

        @pl.loop(0, n_chunks // 2)
        def _(pair):
            c = 2 * pair

            @pl.when(pair > 0)
            def _():
                write(c - 1, 1).wait()

            gather(c + 1, 1).start()
            gather(c, 0).wait()
            write(c, 0).start()

            @pl.when(c + 2 < n_chunks)
            def _():
                write(c, 0).wait()
                gather(c + 2, 0).start()

            gather(c + 1, 1).wait()
            write(c + 1, 1).start()

        write(n_chunks - 2, 0).wait()
        write(n_chunks - 1, 1).wait()

    return k(src, idx)


def _expert_kernel(grp_ref, ea_ref, eb_ref, used_ref, xs_ref, wg_ref, wu_ref, wd_ref, g_ref, b_ref, y_ref,
                   z_ref, *, alpha):
    del grp_ref
    n = pl.program_id(0)
    blk = jnp.minimum(n, pl.num_programs(0) - 2)
    used = used_ref[0]

    @pl.when(n == 0)
    def _():
        z_ref[...] = jnp.zeros_like(z_ref)

    @pl.when(n <= used)
    def _():
        y_ref[...] = _layer_norm(z_ref[...], g_ref[...], b_ref[...])
        h2 = xs_ref[:, :D]
        x = h2.astype(bf16)

        def expert(e):
            a = _dot(x, wg_ref[0, e])
            u = _dot(x, wu_ref[0, e])
            act = a * (1.0 / (1.0 + jnp.exp(-a))) * u
            return _dot(act.astype(bf16), wd_ref[0, e])

        ga = xs_ref[:, D:D + 1]
        gb = xs_ref[:, D + 1:D + 2]
        z_ref[...] = alpha * h2 + ga * expert(ea_ref[blk]) + gb * expert(eb_ref[blk])

    @pl.when(n > used)
    def _():
        y_ref[...] = jnp.zeros_like(y_ref)


def _expert_kernel_into(grp_ref, ea_ref, eb_ref, used_ref, xs_ref, wg_ref, wu_ref, wd_ref, g_ref, b_ref,
                        ys_ref, y_ref, z_ref, *, alpha):
    del ys_ref
    _expert_kernel(grp_ref, ea_ref, eb_ref, used_ref, xs_ref, wg_ref, wu_ref, wd_ref, g_ref, b_ref, y_ref,
                   z_ref, alpha=alpha)


def _experts(grp, ea, eb, used, xs, wg, wu, wd, ln_g, ln_b, alpha, ys, first_block, total_rows):
    nblk = xs.shape[0] // ROW_BLOCK

    def xmap(n, grp, ea, eb, used):
        return (jnp.maximum(jnp.minimum(n, used[0] - 1), 0), 0)

    gmap = lambda n, grp, ea, eb, used: (grp[jnp.minimum(n, nblk - 1)], 0, 0, 0)
    gspec = lambda w: pl.BlockSpec((1,) + w.shape[1:], gmap, pipeline_mode=pl.Buffered(1))
    vec = pl.BlockSpec((1, D), lambda n, grp, ea, eb, used: (0, 0))
    in_specs = [pl.BlockSpec((ROW_BLOCK, XW), xmap), gspec(wg), gspec(wu), gspec(wd), vec, vec]
    operands = [grp, ea, eb, used, xs, wg, wu, wd, ln_g, ln_b]
    aliases = {}
    body = _expert_kernel
    if ys is not None:
        in_specs.append(pl.BlockSpec(memory_space=pl.ANY))
        aliases = {len(operands): 0}
        operands.append(ys)
        body = _expert_kernel_into
    grid_spec = pltpu.PrefetchScalarGridSpec(
        num_scalar_prefetch=4, grid=(nblk + 1,), in_specs=in_specs,
        out_specs=pl.BlockSpec((ROW_BLOCK, D),
                               lambda n, grp, ea, eb, used: (jnp.maximum(n - 1, 0) + first_block, 0)),
        scratch_shapes=[pltpu.VMEM((ROW_BLOCK, D), f32)],
    )
    return pl.pallas_call(
        functools.partial(body, alpha=alpha), grid_spec=grid_spec,
        out_shape=jax.ShapeDtypeStruct((total_rows, D), f32),
        input_output_aliases=aliases,
        compiler_params=_cparams(("arbitrary",)), name="experts",
    )(*operands)


def _pair_tables():
    ea = np.zeros((LANES,), np.int32)
    eb = np.zeros((LANES,), np.int32)
    for g in range(N_GROUPS):
        k = 0
        for a in range(EPG):
            for b in range(a + 1, EPG):
                ea[g * N_PAIRS + k] = a
                eb[g * N_PAIRS + k] = b
                k += 1
    return ea, eb


_PAIR_A, _PAIR_B = _pair_tables()


def _layer(h, mem, positions, w_in, b_forget, sinks, w_mix_out, ln_mix_g, ln_mix_b,
           w_xq, w_xkv, w_xout, ln_x_g, ln_x_b, w_rg, b_rg, w_re, b_re,
           w_eg, w_eu, w_ed, ln_f_g, ln_f_b, alpha):
    B, S, _ = h.shape
    T = B * S
    x2 = h.reshape(T, D)

    o = np.cumsum((0, FOX_W, FOX_W, FOX_W, N_FOX, SWA_Q_W, SWA_KV_W, SWA_KV_W))
    w_qf, w_kf, w_vf, w_fl, w_qs, w_ks, w_vs = (w_in[:, o[i]:o[i + 1]] for i in range(7))
    def regroup(a, axis):
        shp = a.shape
        a = jnp.moveaxis(a, axis, 0).reshape(N_SWA_KV, N_SWA // N_SWA_KV, HD, -1)
        return jnp.moveaxis(jnp.swapaxes(a, 0, 1).reshape(N_SWA * HD, -1), 0, axis).reshape(shp)

    w_all = jnp.concatenate([w_kf, regroup(w_qs, 1), w_ks, w_vs], axis=1).astype(bf16)
    wqt = w_qf.T.astype(bf16)
    wvt = w_vf.T.astype(bf16)
    wfl = w_fl.T.astype(bf16)
    bfc = b_forget.reshape(N_FOX, 1).astype(f32)
    half = HD // 2
    inv_freq = ROPE_THETA ** (-jnp.arange(half, dtype=f32) / half)
    per_row = LANES // half
    invf = jnp.tile(inv_freq, per_row).reshape(1, LANES)
    pos4 = jnp.repeat(positions.reshape(per_row, T // per_row).T.astype(i32), half, axis=1)
    cos, sin = _rope_table(pos4, invf)
    w_out = jnp.concatenate([w_mix_out[:FOX_W], regroup(w_mix_out[FOX_W:], 0)], axis=0).astype(bf16)

    tq = 512
    (qt, kf, vt, qs, ks, vs, lf, qn, kn), (eg16, eu16, ed16) = _in_proj(
        x2, cos, sin, w_all, wqt, wvt, wfl, bfc, w_eg, w_eu, w_ed, tq)
    c, ca = _cumsum(lf, S)
    c4 = c.reshape(N_FOX // 2, 2, T // tq, tq)
    r3 = lambda a: a.reshape(B, S, a.shape[-1])
    o_fox = _fox(_fox_first_blocks(qn, kn, c, B, S, tq), qt, r3(kf), r3(ca), vt, c4, B, S, tq)
    o_swa = _swa(sinks.astype(f32) * LOG2E, r3(qs), r3(ks), r3(vs), B, S)

    kx, vx = _kvproj(mem.reshape(-1, D), w_xkv.astype(bf16))

    gpad = EPG - N_GROUPS
    wr = jnp.concatenate([jnp.pad(w_rg, ((0, 0), (0, gpad))),
                          jnp.transpose(w_re, (1, 0, 2)).reshape(D, N_EXPERTS)], axis=1)
    wr = jnp.pad(wr, ((0, 0), (0, LANES - wr.shape[1]))).astype(f32)
    wrh = wr.astype(bf16)
    wr2 = jnp.concatenate([wrh, (wr - wrh.astype(f32)).astype(bf16)], axis=1)
    br = jnp.pad(jnp.concatenate([jnp.pad(b_rg, (0, gpad)), b_re.reshape(-1)]), (0, LANES - EPG - N_EXPERTS))
    br = br.reshape(1, LANES).astype(f32)
    v2 = lambda a: a.reshape(1, D).astype(f32)
    h2x, bk8 = _mid(x2, o_fox.reshape(T, FOX_W), o_swa.reshape(T, SWA_Q_W), w_out,
                    v2(ln_mix_g), v2(ln_mix_b), w_xq.astype(bf16), kx, vx, w_xout.astype(bf16),
                    v2(ln_x_g), v2(ln_x_b), wr2, br, alpha, S)

    rank, cnt = _rank(bk8)
    counts = cnt[:, 0].astype(i32)
    padded = ((counts + ROW_BLOCK - 1) // ROW_BLOCK) * ROW_BLOCK
    pad_end = jnp.cumsum(padded)
    pad_start = (pad_end - padded).astype(i32)
    unit = int(np.lcm(SC_CORES * SC_SUBCORES * 2 * SC_GATHER_ROWS, ROW_BLOCK))
    scale = -(-(T + N_BUCKETS * ROW_BLOCK) // (unit * sum(MOE_CHUNK_SHARES)))
    chunk_blocks = [share * scale * unit // ROW_BLOCK for share in MOE_CHUNK_SHARES]
    nblk = sum(chunk_blocks)
    P = nblk * ROW_BLOCK
    used = (pad_end[-1] // ROW_BLOCK).astype(i32).reshape(1)
    blk_row = jnp.arange(nblk, dtype=i32)[:, None] * ROW_BLOCK
    blk_bucket = jnp.minimum(jnp.sum((pad_end[None, :] <= blk_row).astype(i32), axis=1), N_BUCKETS - 1)
    pick = (blk_bucket[:, None] == jnp.arange(LANES, dtype=i32)[None, :]).astype(i32)
    blk_a = jnp.sum(pick * jnp.asarray(_PAIR_A)[None, :], axis=1)
    blk_b = jnp.sum(pick * jnp.asarray(_PAIR_B)[None, :], axis=1)
    blk_g = blk_bucket // N_PAIRS
    by_group = lambda w: w.reshape((N_GROUPS, EPG) + w.shape[1:])

    dest = _dest(bk8, rank, pad_start.astype(f32).reshape(LANES, 1))[0]
    row_tok = _sc_invert(dest, jnp.arange(P, dtype=i32) % T)
    ys = None
    lo = 0
    for cblk in chunk_blocks:
        xs = _sc_gather_rows(row_tok[lo * ROW_BLOCK:(lo + cblk) * ROW_BLOCK], h2x)
        used_c = jnp.clip(used - lo, 0, cblk)
        ys = _experts(blk_g[lo:lo + cblk], blk_a[lo:lo + cblk], blk_b[lo:lo + cblk], used_c, xs,
                      by_group(eg16), by_group(eu16), by_group(ed16), v2(ln_f_g), v2(ln_f_b), alpha,
                      ys, lo, P)
        lo += cblk
    return _sc_gather_rows(dest, ys).reshape(B, S, D)


def kernel(x, mem, positions, w_in, b_forget, sinks, w_mix_out, ln_mix_g, ln_mix_b, w_xq, w_xkv, w_xout,
           ln_x_g, ln_x_b, w_route_group, b_route_group, w_route_expert, b_route_expert,
           w_exp_gate, w_exp_up, w_exp_down, ln_ffn_g, ln_ffn_b):
    depth = w_in.shape[0]
    alpha = (2.0 * depth) ** 0.25
    h = x
    for l in range(depth):
        h = _layer(h, mem, positions, w_in[l], b_forget[l], sinks[l], w_mix_out[l], ln_mix_g[l], ln_mix_b[l],
                   w_xq[l], w_xkv[l], w_xout[l], ln_x_g[l], ln_x_b[l], w_route_group[l], b_route_group[l],
                   w_route_expert[l], b_route_expert[l], w_exp_gate[l], w_exp_up[l], w_exp_down[l],
                   ln_ffn_g[l], ln_ffn_b[l], alpha)
    return h
```

```python
import functools

import jax
import jax.numpy as jnp
import numpy as np
from jax import lax
from jax.experimental import pallas as pl
from jax.experimental.pallas import tpu as pltpu
from jax.experimental.pallas import tpu_sc as plsc

f32 = jnp.float32
bf16 = jnp.bfloat16
i32 = jnp.int32

D = 1024
HD = 64
N_FOX = 8
N_SWA = 8
N_SWA_KV = 2
FOX_W = 512
SWA_Q_W = 512
SWA_KV_W = 128
WINDOW = 128
ROPE_THETA = 10000.0
N_XH = 4
XHD = 256
N_GROUPS = 4
EPG = 8
N_EXPERTS = 32
LN_EPS = 1e-5
NEG = -1e30
LOG2E = 1.4426950408889634
L_ROW = (HD, 0)
SKIP_LOG2 = 160.0

SC_CORES = 2
SC_SUBCORES = 16
SC_LANES = 16
SC_GATHER_ROWS = 32
MOE_CHUNK_SHARES = (2, 9, 13)
LANES = 128
ROW_BLOCK = 128
N_PAIRS = EPG * (EPG - 1) // 2
N_BUCKETS = N_GROUPS * N_PAIRS
XW = D + LANES
VMEM_LIMIT = 56 * 1024 * 1024


def _cparams(sem):
    return pltpu.CompilerParams(dimension_semantics=sem, vmem_limit_bytes=VMEM_LIMIT)


def _layer_norm(v, g, b):
    mu = jnp.mean(v, axis=-1, keepdims=True)
    c = v - mu
    var = jnp.mean(c * c, axis=-1, keepdims=True)
    return c * lax.rsqrt(var + LN_EPS) * g + b


def _dot(a, b):
    return jnp.dot(a, b, preferred_element_type=f32)


def _dot_nt(a, b):
    return lax.dot_general(a, b, (((1,), (1,)), ((), ())), preferred_element_type=f32)


def _rope_table_kernel(pos_ref, invf_ref, cos_ref, sin_ref):
    ang = pos_ref[...].astype(f32) * invf_ref[...]
    cos_ref[...] = jnp.cos(ang)
    sin_ref[...] = jnp.sin(ang)


def _rope_table(pos4, invf, rows=1024):
    R = pos4.shape[0]
    blk = pl.BlockSpec((rows, LANES), lambda i: (i, 0))
    return pl.pallas_call(
        _rope_table_kernel, grid=(R // rows,),
        in_specs=[blk, pl.BlockSpec((1, LANES), lambda i: (0, 0))], out_specs=[blk, blk],
        out_shape=[jax.ShapeDtypeStruct((R, LANES), f32)] * 2,
        compiler_params=_cparams(("parallel",)), name="rope_table",
    )(pos4, invf)


def _inproj_kernel(x_ref, cos_ref, sin_ref, w_ref, wqt_ref, wvt_ref, wfl_ref, bf_ref, ind_ref,
                   eg_ref, eu_ref, ed_ref,
                   qt_ref, kf_ref, vt_ref, qs_ref, ks_ref, vs_ref, lf_ref, qn_ref, kn_ref,
                   egb_ref, eub_ref, edb_ref):
    tm = x_ref.shape[0]
    xb = x_ref[...].astype(bf16)
    egb_ref[...] = eg_ref[...].astype(bf16)
    eub_ref[...] = eu_ref[...].astype(bf16)
    edb_ref[...] = ed_ref[...].astype(bf16)

    def proj(lo, hi):
        return _dot(xb, w_ref[:, lo:hi])

    qv = _dot_nt(wqt_ref[...], xb) * (0.125 * LOG2E)
    qt_ref[0] = qv.astype(bf16)
    vt_ref[0] = _dot_nt(wvt_ref[...], xb).astype(bf16)
    kv = proj(0, 512)
    kf_ref[...] = kv.astype(bf16)
    q2 = jnp.sum((qv * qv).reshape(N_FOX, HD, tm), axis=1)
    qn_ref[0] = jnp.broadcast_to(jnp.max(q2, axis=1, keepdims=True), (N_FOX, LANES))
    k2 = _dot((kv * kv).astype(bf16), ind_ref[...])
    kn_ref[0] = jnp.broadcast_to(jnp.max(k2, axis=0, keepdims=True), (N_FOX, LANES))

    half = HD // 2
    reps = LANES // half
    quarter = pl.program_id(0) // (pl.num_programs(0) // reps)

    def spread(tab):
        pick = tab[:, :half]
        for k in range(1, reps):
            pick = jnp.where(quarter == k, tab[:, k * half:(k + 1) * half], pick)
        return jnp.concatenate([pick] * reps, axis=1)

    cos = spread(cos_ref[...])
    sin = spread(sin_ref[...])
    lane = lax.broadcasted_iota(i32, (tm, LANES), 1)
    lo_half = (lane % HD) < (HD // 2)
    sin_s = jnp.where(lo_half, -sin, sin)

    def rope(z):
        rot = jnp.where(lo_half, pltpu.roll(z, LANES - HD // 2, 1), pltpu.roll(z, HD // 2, 1))
        return z * cos + rot * sin_s

    zq = proj(512, 1024)
    for g in range(4):
        sl = slice(g * LANES, (g + 1) * LANES)
        qs_ref[:, sl] = (rope(zq[:, sl]) * (0.125 * LOG2E)).astype(bf16)
    zkv = proj(1024, 1280)
    ks_ref[...] = rope(zkv[:, :SWA_KV_W]).astype(bf16)
    vs_ref[...] = zkv[:, SWA_KV_W:].astype(bf16)

    z = _dot_nt(wfl_ref[...], xb) + bf_ref[...]
    lf_ref[...] = jnp.minimum(z, 0.0) - jnp.log(1.0 + jnp.exp(-jnp.abs(z)))


def _in_proj(x2, cos, sin, w_all, wqt, wvt, wfl, bfc, w_eg, w_eu, w_ed, tm):
    T = x2.shape[0]
    steps = T // tm
    row = lambda w: pl.BlockSpec((tm, w), lambda i: (i, 0))
    full = lambda a: pl.BlockSpec(a.shape, lambda i: (0,) * a.ndim)
    fmaj = pl.BlockSpec((1, FOX_W, tm), lambda i: (i, 0, 0))
    flat = [w.reshape(-1, w.shape[-1]) for w in (w_eg, w_eu, w_ed)]
    slices = [pl.BlockSpec((w.shape[0] // steps, w.shape[1]), lambda i: (i, 0)) for w in flat]
    ind = jnp.asarray(np.arange(FOX_W)[:, None] // HD == np.arange(LANES)[None, :], bf16)
    nrm = pl.BlockSpec((1, N_FOX, LANES), lambda i: (i, 0, 0))
    tab_steps = cos.shape[0] // tm
    tab = pl.BlockSpec((tm, LANES), lambda i: (i % tab_steps, 0))
    outs = pl.pallas_call(
        _inproj_kernel,
        grid=(steps,),
        in_specs=[row(D), tab, tab, full(w_all), full(wqt), full(wvt), full(wfl), full(bfc), full(ind)] + slices,
        out_specs=[fmaj, row(512), fmaj, row(512), row(128), row(128),
                   pl.BlockSpec((N_FOX, tm), lambda i: (0, i)), nrm, nrm] + slices,
        out_shape=[jax.ShapeDtypeStruct((steps, FOX_W, tm), bf16), jax.ShapeDtypeStruct((T, 512), bf16),
                   jax.ShapeDtypeStruct((steps, FOX_W, tm), bf16), jax.ShapeDtypeStruct((T, 512), bf16),
                   jax.ShapeDtypeStruct((T, 128), bf16), jax.ShapeDtypeStruct((T, 128), bf16),
                   jax.ShapeDtypeStruct((N_FOX, T), f32),
                   jax.ShapeDtypeStruct((steps, N_FOX, LANES), f32), jax.ShapeDtypeStruct((steps, N_FOX, LANES), f32)]
        + [jax.ShapeDtypeStruct(w.shape, bf16) for w in flat],
        compiler_params=_cparams(("parallel",)),
        name="in_proj",
    )(x2, cos, sin, w_all, wqt, wvt, wfl, bfc, ind, *flat)
    experts_bf16 = [o.reshape(w.shape) for o, w in zip(outs[9:], (w_eg, w_eu, w_ed))]
    return outs[:9], experts_bf16


def _cumsum_kernel(lf_ref, c_ref, ca_ref):
    S = lf_ref.shape[1]
    ch = 256
    r = lax.broadcasted_iota(i32, (ch, ch), 0)
    c = lax.broadcasted_iota(i32, (ch, ch), 1)
    tri = (r <= c).astype(f32)
    eye = (r == c).astype(bf16)
    stacked = jnp.concatenate([lf_ref[:, j * ch:(j + 1) * ch] for j in range(S // ch)], axis=0)
    local = jnp.dot(stacked, tri, precision=lax.Precision.HIGHEST, preferred_element_type=f32)
    carry = jnp.zeros((N_FOX, 1), f32)
    for j in range(S // ch):
        cc = local[j * N_FOX:(j + 1) * N_FOX] + carry
        carry = cc[:, ch - 1:ch]
        c2 = cc * LOG2E
        c_ref[:, j * ch:(j + 1) * ch] = c2
        neg = -c2
        hi = neg.astype(bf16)
        r1 = neg - hi.astype(f32)
        mid = r1.astype(bf16)
        lo = (r1 - mid.astype(f32)).astype(bf16)
        terms = jnp.concatenate([hi, mid, lo, jnp.zeros((LANES - 3 * N_FOX, ch), bf16)], axis=0)
        ca_ref[j * ch:(j + 1) * ch, :] = _dot_nt(eye, terms).astype(bf16)


def _cumsum(lf, S):
    T = lf.shape[1]
    spec = pl.BlockSpec((N_FOX, S), lambda b: (0, b))
    return pl.pallas_call(
        _cumsum_kernel, grid=(T // S,), in_specs=[spec],
        out_specs=[spec, pl.BlockSpec((S, LANES), lambda b: (b, 0))],
        out_shape=[jax.ShapeDtypeStruct((N_FOX, T), f32), jax.ShapeDtypeStruct((T, LANES), bf16)],
        compiler_params=_cparams(("parallel",)), name="cumsum",
    )(lf)


def _fox_kernel(j0_ref, qt_ref, k_ref, ca_ref, vt_ref, c_ref, o_ref, t0_ref, t1_ref, *, tq):
    hp = pl.program_id(1)
    i = pl.program_id(2)
    qt = qt_ref[0]
    row = lax.broadcasted_iota(i32, (LANES, tq), 0)
    is_a = row < HD
    zero = jnp.zeros_like(qt)
    q_ops = []
    for h in range(2):
        ones = jnp.where(((row & 7) == 2 * hp + h) & (row < 3 * N_FOX), 1.0, 0.0).astype(bf16)
        qh = jnp.where(is_a, qt, zero) if h == 0 else jnp.where(is_a, zero, qt)
        q_ops.append(jnp.concatenate([qh, ones], axis=0))
    kr = lax.broadcasted_iota(i32, (tq, tq), 0)
    qc = lax.broadcasted_iota(i32, (tq, tq), 1)
    causal = kr <= qc
    cq = [c_ref[0, h, pl.ds(i, 1), :] for h in range(2)]

    def scores(j, t_ref):
        off = pl.multiple_of(j * tq, tq)
        kblk = jnp.concatenate([k_ref[0, pl.ds(off, tq), :], ca_ref[0, pl.ds(off, tq), :]], axis=1)
        for h in range(2):
            t_ref[h] = _dot(kblk, q_ops[h])

    keep = [jnp.where(is_a, 1.0, 0.0).astype(bf16), jnp.where(is_a, 0.0, 1.0).astype(bf16)]
    ones_row = [jnp.where(row == L_ROW[h], 1.0, 0.0).astype(bf16) for h in range(2)]

    def softmax_pv(j, t_ref, carry, masked):
        vt = vt_ref[j]
        vts = [vt * keep[h] + ones_row[h] for h in range(2)]
        new = []
        for h in range(2):
            m, acc = carry[h]
            t = t_ref[h]
            if masked:
                t = jnp.where(causal, t, NEG)
            m_new = jnp.maximum(m, jnp.max(t, axis=0, keepdims=True) + cq[h])
            alpha = jnp.exp2(m - m_new)
            p = jnp.exp2(t + (cq[h] - m_new))
            acc = alpha * acc + _dot(vts[h], p.astype(bf16))
            new.append((m_new, acc))
        return tuple(new)

    j0 = j0_ref[(pl.program_id(0) * pl.num_programs(1) + hp) * pl.num_programs(2) + i]
    n_full = i - j0

    def pair(k, carry):
        j = j0 + 2 * k
        scores(j + 1, t1_ref)
        carry = softmax_pv(j, t0_ref, carry, False)
        scores(j + 2, t0_ref)
        return softmax_pv(j + 1, t1_ref, carry, False)

    def odd_tail(carry):
        scores(i, t1_ref)
        carry = softmax_pv(i - 1, t0_ref, carry, False)
        return softmax_pv(i, t1_ref, carry, True)

    def even_tail(carry):
        return softmax_pv(i, t0_ref, carry, True)

    init = tuple((jnp.full((1, tq), NEG, f32), jnp.zeros((LANES, tq), f32)) for _ in range(2))
    scores(j0, t0_ref)
    carry = lax.fori_loop(0, n_full // 2, pair, init)
    (_, acca), (_, accb) = lax.cond(n_full % 2 == 1, odd_tail, even_tail, carry)
    la = acca[L_ROW[0]:L_ROW[0] + 1, :]
    lb = accb[L_ROW[1]:L_ROW[1] + 1, :]
    ot = jnp.where(is_a, acca / la, accb / lb)
    o_ref[0] = jnp.transpose(ot).astype(bf16)


def _fox_first_blocks(qn, kn, c, B, S, tq):
    nq = S // tq
    qmax = jnp.sqrt(qn[:, :, 0].reshape(B, nq, N_FOX))
    kmax = jnp.sqrt(kn[:, 0, :N_FOX].reshape(B, nq, N_FOX))
    qk = 1.02 * qmax[:, :, None, :] * (kmax[:, None, :, :] + kmax[:, :, None, :])
    cb = c.reshape(N_FOX, B, nq, tq)
    c_first = jnp.transpose(cb[..., 0], (1, 2, 0))
    c_last = jnp.transpose(cb[..., tq - 1], (1, 2, 0))
    gap = c_last[:, None, :, :] - c_first[:, :, None, :]
    blk = jnp.arange(nq)
    negligible = (qk - gap < -SKIP_LOG2) & (blk[None, :] < blk[:, None])[None, :, :, None]
    pair = jnp.all(negligible.reshape(B, nq, nq, N_FOX // 2, 2), axis=-1)
    j0 = jnp.sum(jnp.cumprod(pair.astype(i32), axis=2), axis=2)
    return jnp.transpose(j0, (0, 2, 1)).reshape(-1).astype(i32)


def _fox(j0, qt, kf, ca, vt, c4, B, S, tq):
    nq = S // tq
    kernel = functools.partial(_fox_kernel, tq=tq)
    return pl.pallas_call(
        kernel,
        grid=(B, N_FOX // 2, nq),
        in_specs=[
            pl.BlockSpec(memory_space=pltpu.SMEM),
            pl.BlockSpec((1, LANES, tq), lambda b, hp, i: (b * nq + i, hp, 0)),
            pl.BlockSpec((1, S, LANES), lambda b, hp, i: (b, 0, hp)),
            pl.BlockSpec((1, S, LANES), lambda b, hp, i: (b, 0, 0)),
            pl.BlockSpec((nq, LANES, tq), lambda b, hp, i: (b, hp, 0)),
            pl.BlockSpec((1, 2, nq, tq), lambda b, hp, i: (hp, 0, b, 0)),
        ],
        out_specs=pl.BlockSpec((1, tq, LANES), lambda b, hp, i: (b, i, hp)),
        out_shape=jax.ShapeDtypeStruct((B, S, FOX_W), bf16),
        scratch_shapes=[pltpu.VMEM((2, tq, tq), f32), pltpu.VMEM((2, tq, tq), f32)],
        compiler_params=_cparams(("parallel", "parallel", "arbitrary")),
        name="fox",
    )(j0, qt, kf, ca, vt, c4)


def _swa_kernel(sink_ref, q_ref, k_ref, v_ref, o_ref, *, tq):
    W = WINDOW
    nsub = tq // W
    n0 = pl.program_id(1) * nsub
    lane = lax.broadcasted_iota(i32, (W, LANES), 1)
    is0 = lane < HD
    qoff = lax.broadcasted_iota(i32, (4 * W, 2 * W), 0) % W
    cols = lax.broadcasted_iota(i32, (4 * W, 2 * W), 1)
    bias_mid = jnp.where((cols - W <= qoff) & (qoff - (cols - W) < W), 0.0, NEG)
    bias_first = jnp.where(cols <= qoff, 0.0, NEG)
    rgrp = lax.broadcasted_iota(i32, (4 * W, 1), 0) // W
    for r in range(nsub):
        nb = n0 + r
        kstart = pl.multiple_of(jnp.maximum(nb * W - W, 0), W)
        ks = k_ref[0, pl.ds(kstart, 2 * W), :]
        vs = v_ref[0, pl.ds(kstart, 2 * W), :]
        bias = jnp.where(nb == 0, bias_first, bias_mid)
        outs = []
        for kv in range(2):
            keep = is0 if kv == 0 else jnp.logical_not(is0)
            parts = []
            for g in range(4):
                qg = q_ref[0, r * W:(r + 1) * W, g * LANES:(g + 1) * LANES]
                parts.append(jnp.where(keep, qg, jnp.zeros_like(qg)))
            qstack = jnp.concatenate(parts, axis=0)
            s = bias + _dot_nt(qstack, ks)
            sink = jnp.zeros((4 * W, 1), f32)
            for g in range(4):
                sink = jnp.where(rgrp == g, sink_ref[kv * 4 + g], sink)
            m = jnp.maximum(jnp.max(s, axis=1, keepdims=True), sink)
            e = jnp.exp2(s - m)
            den = jnp.sum(e, axis=1, keepdims=True) + jnp.exp2(sink - m)
            outs.append(_dot(e.astype(bf16), vs) / den)
        for g in range(4):
            og = jnp.where(is0, outs[0][g * W:(g + 1) * W], outs[1][g * W:(g + 1) * W])
            o_ref[0, r * W:(r + 1) * W, g * LANES:(g + 1) * LANES] = og.astype(bf16)


def _swa(sinks, qs, ks, vs, B, S, tq=512):
    kernel = functools.partial(_swa_kernel, tq=tq)
    return pl.pallas_call(
        kernel,
        grid=(B, S // tq),
        in_specs=[
            pl.BlockSpec(memory_space=pltpu.SMEM),
            pl.BlockSpec((1, tq, SWA_Q_W), lambda b, i: (b, i, 0)),
            pl.BlockSpec((1, S, SWA_KV_W), lambda b, i: (b, 0, 0)),
            pl.BlockSpec((1, S, SWA_KV_W), lambda b, i: (b, 0, 0)),
        ],
        out_specs=pl.BlockSpec((1, tq, SWA_Q_W), lambda b, i: (b, i, 0)),
        out_shape=jax.ShapeDtypeStruct((B, S, SWA_Q_W), bf16),
        compiler_params=_cparams(("parallel", "arbitrary")),
        name="swa",
    )(sinks, qs, ks, vs)


def _kvproj_kernel(m_ref, w_ref, k_ref, v_ref):
    mb = m_ref[...].astype(bf16)
    k_ref[...] = _dot(mb, w_ref[:, :D]).astype(bf16)
    v_ref[...] = _dot(mb, w_ref[:, D:]).astype(bf16)


def _kvproj(mem2, w_xkv, tm=512):
    R = mem2.shape[0]
    row = pl.BlockSpec((tm, D), lambda i: (i, 0))
    return pl.pallas_call(
        _kvproj_kernel, grid=(R // tm,),
        in_specs=[row, pl.BlockSpec(w_xkv.shape, lambda i: (0, 0))],
        out_specs=[row, row],
        out_shape=[jax.ShapeDtypeStruct((R, D), bf16)] * 2,
        compiler_params=_cparams(("parallel",)), name="kvproj",
    )(mem2, w_xkv)


def _mid_kernel(x_ref, of_ref, os_ref, wo_ref, g1_ref, b1_ref, wq_ref, k_ref, v_ref,
                wxo_ref, g2_ref, b2_ref, wr_ref, br_ref,
                h2_ref, bk_ref, oc_ref, *, alpha):
    tm = x_ref.shape[0]
    mix = _dot(of_ref[...], wo_ref[:FOX_W, :]) + _dot(os_ref[...], wo_ref[FOX_W:, :])
    h1 = _layer_norm(alpha * x_ref[...] + mix, g1_ref[...], b1_ref[...])

    q = (_dot(h1.astype(bf16), wq_ref[...]) * 0.0625).astype(bf16)
    for h in range(N_XH):
        sl = slice(h * XHD, (h + 1) * XHD)
        s = _dot_nt(q[:, sl], k_ref[:, sl])
        e = jnp.exp(s - jnp.max(s, axis=1, keepdims=True))
        p = e / jnp.sum(e, axis=1, keepdims=True)
        oc_ref[:, sl] = _dot(p.astype(bf16), v_ref[:, sl]).astype(bf16)
    xo = _dot(oc_ref[...], wxo_ref[...])
    h2 = _layer_norm(alpha * h1 + xo, g2_ref[...], b2_ref[...])
    h2_ref[:, :D] = h2

    hh = h2.astype(bf16)
    hl = (h2 - hh.astype(f32)).astype(bf16)
    hi_terms = _dot(hh, wr_ref[...])
    lg = hi_terms[:, :LANES] + _dot(hl, wr_ref[:, :LANES]) + hi_terms[:, LANES:] + br_ref[...]

    lgt = jnp.transpose(lg)
    row = lax.broadcasted_iota(i32, (EPG, tm), 0).astype(f32)
    big = float(EPG)

    def first_max(vals, mask):
        vm = jnp.where(mask, vals, NEG)
        top = jnp.max(vm, axis=0, keepdims=True)
        idx = jnp.min(jnp.where(mask & (vm == top), row, big), axis=0, keepdims=True)
        return top, idx

    gl = lgt[0:EPG]
    gmask = row < float(N_GROUPS)
    gmax, gidx = first_max(gl, gmask)
    g_val = 1.0 / jnp.sum(jnp.where(gmask, jnp.exp(gl - gmax), 0.0), axis=0, keepdims=True)
    sel = jnp.zeros((EPG, tm), f32)
    for g in range(N_GROUPS):
        sel = jnp.where(gidx == float(g), lgt[EPG * (g + 1):EPG * (g + 2)], sel)
    every = row >= 0.0
    v1, e1 = first_max(sel, every)
    v2, e2 = first_max(sel, row != e1)
    ex = jnp.exp(v2 - v1)
    w1 = g_val * (1.0 / (1.0 + ex))
    w2 = g_val * (ex / (1.0 + ex))
    first_low = e1 < e2
    ea = jnp.where(first_low, e1, e2)
    eb = jnp.where(first_low, e2, e1)
    ga = jnp.where(first_low, w1, w2)
    gb = jnp.where(first_low, w2, w1)
    pidx = ea * float(EPG - 1) - ea * (ea - 1.0) * 0.5 + (eb - ea - 1.0)
    bucket = gidx * float(N_PAIRS) + pidx

    bk_ref[...] = jnp.broadcast_to(bucket, (EPG, tm))
    gates = jnp.where(row == 0.0, ga, jnp.where(row == 1.0, gb, 0.0))
    gates = jnp.concatenate([gates, jnp.zeros((LANES - EPG, tm), f32)], axis=0)
    h2_ref[:, D:] = jnp.transpose(gates)


def _mid(x2, of2, os2, w_out, g1, b1, wq, kx, vx, wxo, g2, b2, wr2, br, alpha, S, tm=1024):
    T = x2.shape[0]
    M = kx.shape[0] // (T // S)
    per_b = S // tm
    row = lambda w: pl.BlockSpec((tm, w), lambda i: (i, 0))
    full = lambda a: pl.BlockSpec(a.shape, lambda i: (0,) * a.ndim)
    kvspec = pl.BlockSpec((M, D), lambda i: (i // per_b, 0))
    kernel = functools.partial(_mid_kernel, alpha=alpha)
    return pl.pallas_call(
        kernel,
        grid=(T // tm,),
        in_specs=[row(D), row(512), row(512), full(w_out), full(g1), full(b1), full(wq),
                  kvspec, kvspec, full(wxo), full(g2), full(b2), full(wr2), full(br)],
        out_specs=[row(XW), pl.BlockSpec((8, tm), lambda i: (0, i))],
        out_shape=[jax.ShapeDtypeStruct((T, XW), f32), jax.ShapeDtypeStruct((8, T), f32)],
        scratch_shapes=[pltpu.VMEM((tm, D), bf16)],
        compiler_params=_cparams(("parallel",)),
        name="mid",
    )(x2, of2, os2, w_out, g1, b1, wq, kx, vx, wxo, g2, b2, wr2, br)


def _rank_kernel(bk_ref, rank_ref, cnt_ref, carry_ref, *, chunk):
    sub = 256

    @pl.when(pl.program_id(0) == 0)
    def _():
        carry_ref[...] = jnp.zeros_like(carry_ref)

    r = lax.broadcasted_iota(i32, (sub, sub), 0)
    c = lax.broadcasted_iota(i32, (sub, sub), 1)
    before = (r < c).astype(bf16)
    bid = lax.broadcasted_iota(i32, (LANES, sub), 0).astype(f32)
    carry = carry_ref[...]
    for j in range(chunk // sub):
        bk = bk_ref[0:1, j * sub:(j + 1) * sub]
        hit = bid == bk
        oh = jnp.where(hit, 1.0, 0.0)
        prior = _dot(oh.astype(bf16), before) + carry
        rank_ref[:, j * sub:(j + 1) * sub] = jnp.sum(jnp.where(hit, prior, 0.0), axis=0, keepdims=True)
        carry = carry + jnp.sum(oh, axis=1, keepdims=True)
    carry_ref[...] = carry
    cnt_ref[...] = carry


def _rank(bk8, chunk=8192):
    T = bk8.shape[1]
    kernel = functools.partial(_rank_kernel, chunk=chunk)
    return pl.pallas_call(
        kernel, grid=(T // chunk,),
        in_specs=[pl.BlockSpec((8, chunk), lambda i: (0, i))],
        out_specs=[pl.BlockSpec((1, chunk), lambda i: (0, i)),
                   pl.BlockSpec((LANES, 1), lambda i: (0, 0))],
        out_shape=[jax.ShapeDtypeStruct((1, T), f32), jax.ShapeDtypeStruct((LANES, 1), f32)],
        scratch_shapes=[pltpu.VMEM((LANES, 1), f32)],
        compiler_params=_cparams(("arbitrary",)), name="rank",
    )(bk8)


def _dest_kernel(bk_ref, rank_ref, ps_ref, dest_ref):
    chunk = bk_ref.shape[1]
    bid = lax.broadcasted_iota(i32, (LANES, chunk), 0).astype(f32)
    start = jnp.sum(jnp.where(bid == bk_ref[0:1, :], ps_ref[...], 0.0), axis=0, keepdims=True)
    dest_ref[...] = (start + rank_ref[...]).astype(i32)


def _dest(bk8, rank, ps_col, chunk=8192):
    T = bk8.shape[1]
    return pl.pallas_call(
        _dest_kernel, grid=(T // chunk,),
        in_specs=[pl.BlockSpec((8, chunk), lambda i: (0, i)), pl.BlockSpec((1, chunk), lambda i: (0, i)),
                  pl.BlockSpec((LANES, 1), lambda i: (0, 0))],
        out_specs=pl.BlockSpec((1, chunk), lambda i: (0, i)),
        out_shape=jax.ShapeDtypeStruct((1, T), i32),
        compiler_params=_cparams(("parallel",)), name="dest",
    )(bk8, rank, ps_col)


def _sc_invert(dest, default):
    T = dest.shape[0]
    n_rows = default.shape[0]
    lanes = SC_LANES
    mesh = plsc.VectorSubcoreMesh(core_axis_name="core", subcore_axis_name="subcore",
                                  num_cores=SC_CORES, num_subcores=SC_SUBCORES)

    @functools.partial(pl.kernel, out_type=jax.ShapeDtypeStruct((n_rows,), i32), mesh=mesh,
                       scratch_types=[pltpu.VMEM((T,), i32), pltpu.VMEM((n_rows,), i32)],
                       compiler_params=pltpu.CompilerParams(needs_layout_passes=False),
                       name="sc_invert")
    def k(dest_hbm, default_hbm, out_hbm, dest_v, table_v):
        wid = lax.axis_index("subcore") * SC_CORES + lax.axis_index("core")

        @pl.when(wid == 0)
        def _():
            pltpu.sync_copy(dest_hbm, dest_v)
            pltpu.sync_copy(default_hbm, table_v)
            lane = lax.iota(i32, lanes)

            @pl.loop(0, T // lanes, unroll=8)
            def _(j):
                off = pl.multiple_of(j * lanes, lanes)
                plsc.store_scatter(table_v, [dest_v[pl.ds(off, lanes)]], lane + off)

            pltpu.sync_copy(table_v, out_hbm)

    return k(dest, default)


def _sc_gather_rows(idx, src, chunk=SC_GATHER_ROWS):
    n = idx.shape[0]
    w = src.shape[1]
    workers = SC_CORES * SC_SUBCORES
    per_worker = n // workers
    mesh = plsc.VectorSubcoreMesh(core_axis_name="core", subcore_axis_name="subcore",
                                  num_cores=SC_CORES, num_subcores=SC_SUBCORES)

    n_chunks = per_worker // chunk
    assert n_chunks % 2 == 0

    @functools.partial(pl.kernel, out_type=jax.ShapeDtypeStruct((n, w), src.dtype), mesh=mesh,
                       scratch_types=[pltpu.VMEM((per_worker,), i32), pltpu.VMEM((2, chunk, w), src.dtype),
                                      pltpu.SemaphoreType.DMA((2,)), pltpu.SemaphoreType.DMA((2,))],
                       name="sc_gather_rows")
    def k(src_hbm, idx_hbm, out_hbm, idx_v, rows_v, gsem, wsem):
        wid = lax.axis_index("subcore") * SC_CORES + lax.axis_index("core")
        base = wid * per_worker
        pltpu.sync_copy(idx_hbm.at[pl.ds(base, per_worker)], idx_v)

        def gather(c, slot):
            rows = idx_v.at[pl.ds(pl.multiple_of(c * chunk, chunk), chunk)]
            return pltpu.make_async_copy(src_hbm.at[rows], rows_v.at[slot], gsem.at[slot])

        def write(c, slot):
            out = out_hbm.at[pl.ds(pl.multiple_of(base + c * chunk, chunk), chunk)]
            return pltpu.make_async_copy(rows_v.at[slot], out, wsem.at[slot])

        gather(0, 0).start()

        @pl.loop(0, n_chunks // 2)
        def _(pair):
            c = 2 * pair

            @pl.when(pair > 0)
            def _():
                write(c - 1, 1).wait()

            gather(c + 1, 1).start()
            gather(c, 0).wait()
            write(c, 0).start()

            @pl.when(c + 2 < n_chunks)
            def _():
                write(c, 0).wait()
                gather(c + 2, 0).start()

            gather(c + 1, 1).wait()
            write(c + 1, 1).start()

        write(n_chunks - 2, 0).wait()
        write(n_chunks - 1, 1).wait()

    return k(src, idx)


def _expert_kernel(grp_ref, ea_ref, eb_ref, used_ref, xs_ref, wg_ref, wu_ref, wd_ref, g_ref, b_ref, y_ref,
                   z_ref, *, alpha):
    del grp_ref
    n = pl.program_id(0)
    blk = jnp.minimum(n, pl.num_programs(0) - 2)
    used = used_ref[0]

    @pl.when(n == 0)
    def _():
        z_ref[...] = jnp.zeros_like(z_ref)

    @pl.when(n <= used)
    def _():
        y_ref[...] = _layer_norm(z_ref[...], g_ref[...], b_ref[...])
        h2 = xs_ref[:, :D]
        x = h2.astype(bf16)

        def expert(e):
            a = _dot(x, wg_ref[0, e])
            u = _dot(x, wu_ref[0, e])
            act = a * (1.0 / (1.0 + jnp.exp(-a))) * u
            return _dot(act.astype(bf16), wd_ref[0, e])

        ga = xs_ref[:, D:D + 1]
        gb = xs_ref[:, D + 1:D + 2]
        z_ref[...] = alpha * h2 + ga * expert(ea_ref[blk]) + gb * expert(eb_ref[blk])

    @pl.when(n > used)
    def _():
        y_ref[...] = jnp.zeros_like(y_ref)


def _expert_kernel_into(grp_ref, ea_ref, eb_ref, used_ref, xs_ref, wg_ref, wu_ref, wd_ref, g_ref, b_ref,
                        ys_ref, y_ref, z_ref, *, alpha):
    del ys_ref
    _expert_kernel(grp_ref, ea_ref, eb_ref, used_ref, xs_ref, wg_ref, wu_ref, wd_ref, g_ref, b_ref, y_ref,
                   z_ref, alpha=alpha)


def _experts(grp, ea, eb, used, xs, wg, wu, wd, ln_g, ln_b, alpha, ys, first_block, total_rows):
    nblk = xs.shape[0] // ROW_BLOCK

    def xmap(n, grp, ea, eb, used):
        return (jnp.maximum(jnp.minimum(n, used[0] - 1), 0), 0)

    gmap = lambda n, grp, ea, eb, used: (grp[jnp.minimum(n, nblk - 1)], 0, 0, 0)
    gspec = lambda w: pl.BlockSpec((1,) + w.shape[1:], gmap, pipeline_mode=pl.Buffered(1))
    vec = pl.BlockSpec((1, D), lambda n, grp, ea, eb, used: (0, 0))
    in_specs = [pl.BlockSpec((ROW_BLOCK, XW), xmap), gspec(wg), gspec(wu), gspec(wd), vec, vec]
    operands = [grp, ea, eb, used, xs, wg, wu, wd, ln_g, ln_b]
    aliases = {}
    body = _expert_kernel
    if ys is not None:
        in_specs.append(pl.BlockSpec(memory_space=pl.ANY))
        aliases = {len(operands): 0}
        operands.append(ys)
        body = _expert_kernel_into
    grid_spec = pltpu.PrefetchScalarGridSpec(
        num_scalar_prefetch=4, grid=(jnp.minimum(used[0], nblk) + 1,), in_specs=in_specs,
        out_specs=pl.BlockSpec((ROW_BLOCK, D),
                               lambda n, grp, ea, eb, used: (jnp.maximum(n - 1, 0) + first_block, 0)),
        scratch_shapes=[pltpu.VMEM((ROW_BLOCK, D), f32)],
    )
    return pl.pallas_call(
        functools.partial(body, alpha=alpha), grid_spec=grid_spec,
        out_shape=jax.ShapeDtypeStruct((total_rows, D), f32),
        input_output_aliases=aliases,
        compiler_params=_cparams(("arbitrary",)), name="experts",
    )(*operands)


def _pair_tables():
    ea = np.zeros((LANES,), np.int32)
    eb = np.zeros((LANES,), np.int32)
    for g in range(N_GROUPS):
        k = 0
        for a in range(EPG):
            for b in range(a + 1, EPG):
                ea[g * N_PAIRS + k] = a
                eb[g * N_PAIRS + k] = b
                k += 1
    return ea, eb


_PAIR_A, _PAIR_B = _pair_tables()


def _layer(h, mem, positions, w_in, b_forget, sinks, w_mix_out, ln_mix_g, ln_mix_b,
           w_xq, w_xkv, w_xout, ln_x_g, ln_x_b, w_rg, b_rg, w_re, b_re,
           w_eg, w_eu, w_ed, ln_f_g, ln_f_b, alpha):
    B, S, _ = h.shape
    T = B * S
    x2 = h.reshape(T, D)

    o = np.cumsum((0, FOX_W, FOX_W, FOX_W, N_FOX, SWA_Q_W, SWA_KV_W, SWA_KV_W))
    w_qf, w_kf, w_vf, w_fl, w_qs, w_ks, w_vs = (w_in[:, o[i]:o[i + 1]] for i in range(7))
    def regroup(a, axis):
        shp = a.shape
        a = jnp.moveaxis(a, axis, 0).reshape(N_SWA_KV, N_SWA // N_SWA_KV, HD, -1)
        return jnp.moveaxis(jnp.swapaxes(a, 0, 1).reshape(N_SWA * HD, -1), 0, axis).reshape(shp)

    w_all = jnp.concatenate([w_kf, regroup(w_qs, 1), w_ks, w_vs], axis=1).astype(bf16)
    wqt = w_qf.T.astype(bf16)
    wvt = w_vf.T.astype(bf16)
    wfl = w_fl.T.astype(bf16)
    bfc = b_forget.reshape(N_FOX, 1).astype(f32)
    half = HD // 2
    inv_freq = ROPE_THETA ** (-jnp.arange(half, dtype=f32) / half)
    per_row = LANES // half
    invf = jnp.tile(inv_freq, per_row).reshape(1, LANES)
    pos4 = jnp.repeat(positions.reshape(per_row, T // per_row).T.astype(i32), half, axis=1)
    cos, sin = _rope_table(pos4, invf)
    w_out = jnp.concatenate([w_mix_out[:FOX_W], regroup(w_mix_out[FOX_W:], 0)], axis=0).astype(bf16)

    tq = 512
    (qt, kf, vt, qs, ks, vs, lf, qn, kn), (eg16, eu16, ed16) = _in_proj(
        x2, cos, sin, w_all, wqt, wvt, wfl, bfc, w_eg, w_eu, w_ed, tq)
    c, ca = _cumsum(lf, S)
    c4 = c.reshape(N_FOX // 2, 2, T // tq, tq)
    r3 = lambda a: a.reshape(B, S, a.shape[-1])
    o_fox = _fox(_fox_first_blocks(qn, kn, c, B, S, tq), qt, r3(kf), r3(ca), vt, c4, B, S, tq)
    o_swa = _swa(sinks.astype(f32) * LOG2E, r3(qs), r3(ks), r3(vs), B, S)

    kx, vx = _kvproj(mem.reshape(-1, D), w_xkv.astype(bf16))

    gpad = EPG - N_GROUPS
    wr = jnp.concatenate([jnp.pad(w_rg, ((0, 0), (0, gpad))),
                          jnp.transpose(w_re, (1, 0, 2)).reshape(D, N_EXPERTS)], axis=1)
    wr = jnp.pad(wr, ((0, 0), (0, LANES - wr.shape[1]))).astype(f32)
    wrh = wr.astype(bf16)
    wr2 = jnp.concatenate([wrh, (wr - wrh.astype(f32)).astype(bf16)], axis=1)
    br = jnp.pad(jnp.concatenate([jnp.pad(b_rg, (0, gpad)), b_re.reshape(-1)]), (0, LANES - EPG - N_EXPERTS))
    br = br.reshape(1, LANES).astype(f32)
    v2 = lambda a: a.reshape(1, D).astype(f32)
    h2x, bk8 = _mid(x2, o_fox.reshape(T, FOX_W), o_swa.reshape(T, SWA_Q_W), w_out,
                    v2(ln_mix_g), v2(ln_mix_b), w_xq.astype(bf16), kx, vx, w_xout.astype(bf16),
                    v2(ln_x_g), v2(ln_x_b), wr2, br, alpha, S)

    rank, cnt = _rank(bk8)
    counts = cnt[:, 0].astype(i32)
    padded = ((counts + ROW_BLOCK - 1) // ROW_BLOCK) * ROW_BLOCK
    pad_end = jnp.cumsum(padded)
    pad_start = (pad_end - padded).astype(i32)
    unit = int(np.lcm(SC_CORES * SC_SUBCORES * 2 * SC_GATHER_ROWS, ROW_BLOCK))
    scale = -(-(T + N_BUCKETS * ROW_BLOCK) // (unit * sum(MOE_CHUNK_SHARES)))
    chunk_blocks = [share * scale * unit // ROW_BLOCK for share in MOE_CHUNK_SHARES]
    nblk = sum(chunk_blocks)
    P = nblk * ROW_BLOCK
    used = (pad_end[-1] // ROW_BLOCK).astype(i32).reshape(1)
    blk_row = jnp.arange(nblk, dtype=i32)[:, None] * ROW_BLOCK
    blk_bucket = jnp.minimum(jnp.sum((pad_end[None, :] <= blk_row).astype(i32), axis=1), N_BUCKETS - 1)
    pick = (blk_bucket[:, None] == jnp.arange(LANES, dtype=i32)[None, :]).astype(i32)
    blk_a = jnp.sum(pick * jnp.asarray(_PAIR_A)[None, :], axis=1)
    blk_b = jnp.sum(pick * jnp.asarray(_PAIR_B)[None, :], axis=1)
    blk_g = blk_bucket // N_PAIRS
    by_group = lambda w: w.reshape((N_GROUPS, EPG) + w.shape[1:])

    dest = _dest(bk8, rank, pad_start.astype(f32).reshape(LANES, 1))[0]
    row_tok = _sc_invert(dest, jnp.arange(P, dtype=i32) % T)
    ys = None
    lo = 0
    for cblk in chunk_blocks:
        xs = _sc_gather_rows(row_tok[lo * ROW_BLOCK:(lo + cblk) * ROW_BLOCK], h2x)
        used_c = jnp.clip(used - lo, 0, cblk)
        ys = _experts(blk_g[lo:lo + cblk], blk_a[lo:lo + cblk], blk_b[lo:lo + cblk], used_c, xs,
                      by_group(eg16), by_group(eu16), by_group(ed16), v2(ln_f_g), v2(ln_f_b), alpha,
                      ys, lo, P)
        lo += cblk
    return _sc_gather_rows(dest, ys).reshape(B, S, D)


def kernel(x, mem, positions, w_in, b_forget, sinks, w_mix_out, ln_mix_g, ln_mix_b, w_xq, w_xkv, w_xout,
           ln_x_g, ln_x_b, w_route_group, b_route_group, w_route_expert, b_route_expert,
           w_exp_gate, w_exp_up, w_exp_down, ln_ffn_g, ln_ffn_b):
    depth = w_in.shape[0]
    alpha = (2.0 * depth) ** 0.25
    h = x
    for l in range(depth):
        h = _layer(h, mem, positions, w_in[l], b_forget[l], sinks[l], w_mix_out[l], ln_mix_g[l], ln_mix_b[l],
                   w_xq[l], w_xkv[l], w_xout[l], ln_x_g[l], ln_x_b[l], w_route_group[l], b_route_group[l],
                   w_route_expert[l], b_route_expert[l], w_exp_gate[l], w_exp_up[l], w_exp_down[l],
                   ln_ffn_g[l], ln_ffn_b[l], alpha)
    return h
```
